```python
import numpy as np
import jax, jax.numpy as jnp
from jax import lax

D_MODEL = 1024
BATCH = 8
SEQ = 4096
DEPTH = 4

N_MIXERS = 4
ALPHA = (2.0 * DEPTH) ** 0.25
BETA = (8.0 * DEPTH) ** -0.25
LN_EPS = 1e-5
NEG_INF = -1e30
TINY = 1e-30
ROPE_THETA = 10000.0

MLSTM_HEADS = 8
MLSTM_QK_DIM = D_MODEL // MLSTM_HEADS // 2
MLSTM_V_DIM = D_MODEL // MLSTM_HEADS
MLSTM_CONV = 4
MLSTM_CHUNK = 64

NSA_HEADS = 16
NSA_KV_HEADS = 4
NSA_HEAD_DIM = D_MODEL // NSA_HEADS
CMP_STRIDE = 16
CMP_BLK = 2 * CMP_STRIDE
SEL_BLK = 64
N_SEL = 16
NSA_WINDOW = 512
NSA_QB = 32
FORCE_SCORE = 1e4

DIL_HEADS = 16
DIL_HEAD_DIM = D_MODEL // DIL_HEADS
DIL_GROUPS = ((128, 1), (512, 4), (2048, 16))

POOL_WINDOWS = (2, 4, 8, 16)
POOL_GROUP = D_MODEL // len(POOL_WINDOWS)

N_EXPERTS = 64
TOP_K = 8
N_EXPERT_GROUPS = 8
TOPK_GROUPS = 4
EXPERT_FF = 256
SHARED_FF = 256
ROUTED_SCALE = 2.5
ROW_BLK = 128

PLE_DIM = 256

kernel_name = "hybrid_mlstm_nsa_dilated_pool_moe"


def _n_uses(m):
    return (DEPTH - m + N_MIXERS - 1) // N_MIXERS


def split_cols(t, sizes):
    return jnp.split(t, np.cumsum(sizes)[:-1].tolist(), axis=-1)


def layer_norm(x, g, b):
    xf = x.astype(jnp.float32)
    mu = xf.mean(-1, keepdims=True)
    var = jnp.square(xf - mu).mean(-1, keepdims=True)
    return ((xf - mu) * lax.rsqrt(var + LN_EPS) * g + b).astype(x.dtype)


def rope(t, pos):
    half = t.shape[-1] // 2
    inv = ROPE_THETA ** (-jnp.arange(half, dtype=jnp.float32) / half)
    ang = pos.astype(jnp.float32)[:, None] * inv[None, :]
    cos, sin = jnp.cos(ang)[:, None, :], jnp.sin(ang)[:, None, :]
    tf = t.astype(jnp.float32)
    t1, t2 = tf[..., :half], tf[..., half:]
    return jnp.concatenate([t1 * cos - t2 * sin, t1 * sin + t2 * cos], -1).astype(t.dtype)


def masked_softmax(s, mask):
    s = jnp.where(mask, s, NEG_INF)
    m = s.max(-1, keepdims=True)
    e = jnp.where(mask, jnp.exp(s - m), 0.0)
    return e / jnp.maximum(e.sum(-1, keepdims=True), TINY)


def swiglu(t, wg, wu, wd):
    return (jax.nn.silu(t @ wg) * (t @ wu)) @ wd


def causal_dwconv(x, w):
    return lax.conv_general_dilated(x, w[:, None, :], window_strides=(1,),
                                    padding=[(w.shape[0] - 1, 0)],
                                    dimension_numbers=('NWC', 'WIO', 'NWC'),
                                    feature_group_count=x.shape[-1])


def mlstm_chunk_step(carry, inp):
    C, n, m = carry
    q, k, v, ig, lf = inp
    L = q.shape[2]
    tri = jnp.arange(L)[:, None] >= jnp.arange(L)[None, :]
    b = jnp.cumsum(lf, axis=-1)
    logD = jnp.where(tri, b[..., :, None] - b[..., None, :] + ig[..., None, :], NEG_INF)
    inter = b + m[..., None]
    m_t = jnp.maximum(inter, logD.max(-1))
    s = jnp.einsum('bhtd,bhsd->bhts', q, k) * jnp.exp(logD - m_t[..., None])
    w_inter = jnp.exp(inter - m_t)
    num = w_inter[..., None] * jnp.einsum('bhtd,bhdv->bhtv', q, C) + jnp.einsum('bhts,bhsv->bhtv', s, v)
    den = w_inter * jnp.einsum('bhtd,bhd->bht', q, n) + s.sum(-1)
    h = num / jnp.maximum(jnp.abs(den), jnp.exp(-m_t))[..., None]
    bL = b[..., -1]
    logw = bL[..., None] - b + ig
    m_new = jnp.maximum(bL + m, logw.max(-1))
    w = jnp.exp(logw - m_new[..., None])
    decay = jnp.exp(bL + m - m_new)
    C_new = decay[..., None, None] * C + jnp.einsum('bhs,bhsd,bhsv->bhdv', w, k, v)
    n_new = decay[..., None] * n + jnp.einsum('bhs,bhsd->bhd', w, k)
    return (C_new, n_new, m_new), h


def mlstm_mixer(x, w_in, conv_w, ig_bias, fg_bias, norm_g, w_out):
    B, S, _ = x.shape
    H, dk, dv, L = MLSTM_HEADS, MLSTM_QK_DIM, MLSTM_V_DIM, MLSTM_CHUNK
    qk, v, o, ig, fg = split_cols(x @ w_in, [2 * H * dk, H * dv, H * dv, H, H])
    q, k = jnp.split(jax.nn.silu(causal_dwconv(qk, conv_w)), 2, axis=-1)
    f32 = jnp.float32
    q = q.reshape(B, S, H, dk).astype(f32) * dk ** -0.5
    k = k.reshape(B, S, H, dk).astype(f32)
    v = v.reshape(B, S, H, dv).astype(f32)
    ig = ig.astype(f32) + ig_bias.astype(f32)
    lf = jax.nn.log_sigmoid(fg.astype(f32) + fg_bias.astype(f32))
    nc = S // L
    c4 = lambda t: t.reshape(B, nc, L, H, t.shape[-1]).transpose(1, 0, 3, 2, 4)
    c3 = lambda t: t.reshape(B, nc, L, H).transpose(1, 0, 3, 2)
    init = (jnp.zeros((B, H, dk, dv), f32), jnp.zeros((B, H, dk), f32), jnp.zeros((B, H), f32))
    _, h = lax.scan(mlstm_chunk_step, init, (c4(q), c4(k), c4(v), c3(ig), c3(lf)))
    h = h.transpose(1, 0, 3, 2, 4).reshape(B, S, H, dv)
    mu = h.mean(-1, keepdims=True)
    hn = (h - mu) * lax.rsqrt(jnp.square(h - mu).mean(-1, keepdims=True) + LN_EPS)
    hn = hn.reshape(B, S, H * dv) * norm_g
    out = hn * jax.nn.sigmoid(o.astype(f32))
    return out.astype(x.dtype) @ w_out


def nsa_mixer(x, w_in, cmp_pos_k, cmp_pos_v, cmp_wk, cmp_wv, w_out):
    B, S, _ = x.shape
    H, G, dh = NSA_HEADS, NSA_KV_HEADS, NSA_HEAD_DIM
    J = H // G
    kv = G * dh
    q, kc, vc, ks, vs, kw, vw, g = split_cols(x @ w_in, [H * dh] + [kv] * 6 + [3 * H])
    pos = jnp.arange(S)
    q = rope(q.reshape(B, S, H, dh), pos)

    def compress(t, pos_emb, w):
        ch = t.reshape(B, S // CMP_STRIDE, CMP_STRIDE, G, dh)
        blocks = jnp.concatenate([ch[:, :-1], ch[:, 1:]], axis=2)
        return jnp.einsum('bclgd,lde->bcge', blocks + pos_emb[None, None, :, None, :], w)

    n_cmp = S // CMP_STRIDE - 1
    cmp_end = jnp.arange(n_cmp) * CMP_STRIDE + CMP_BLK - 1
    kc = rope(compress(kc.reshape(B, S, G, dh), cmp_pos_k, cmp_wk), cmp_end)
    vc = compress(vc.reshape(B, S, G, dh), cmp_pos_v, cmp_wv)
    n_sblk = S // SEL_BLK
    n_sel = min(N_SEL, n_sblk)
    ks_blk = rope(ks.reshape(B, S, G, dh), pos).reshape(B, n_sblk, SEL_BLK, G, dh).transpose(0, 3, 1, 2, 4)
    vs_blk = vs.reshape(B, n_sblk, SEL_BLK, G, dh).transpose(0, 3, 1, 2, 4)
    padw = ((0, 0), (NSA_WINDOW, 0), (0, 0), (0, 0))
    kw_pad = jnp.pad(rope(kw.reshape(B, S, G, dh), pos), padw)
    vw_pad = jnp.pad(vw.reshape(B, S, G, dh), padw)
    c_idx, s_idx = jnp.arange(n_cmp), jnp.arange(n_sblk)
    overlap = ((c_idx[:, None] * CMP_STRIDE + CMP_BLK - 1 >= s_idx[None, :] * SEL_BLK)
               & (c_idx[:, None] * CMP_STRIDE < (s_idx[None, :] + 1) * SEL_BLK)).astype(jnp.float32)
    n_qb = S // NSA_QB
    qb = q.reshape(B, n_qb, NSA_QB, G, J, dh).transpose(1, 0, 3, 4, 2, 5)
    gb = jax.nn.sigmoid(g.astype(jnp.float32)).reshape(B, n_qb, NSA_QB, G, J, 3).transpose(1, 0, 3, 4, 2, 5)
    q0s = jnp.arange(n_qb, dtype=jnp.int32) * NSA_QB
    scale = dh ** -0.5
    bi = jnp.arange(B)[:, None, None, None]
    gi = jnp.arange(G)[None, :, None, None]

    def block(args):
        qi, gate, q0 = args
        t = q0 + jnp.arange(NSA_QB)
        s = jnp.einsum('bgjqd,bcgd->bgjqc', qi, kc).astype(jnp.float32) * scale
        p_cmp = masked_softmax(s, cmp_end[None, :] <= t[:, None])
        o_cmp = jnp.einsum('bgjqc,bcgd->bgjqd', p_cmp, vc)
        imp = jnp.einsum('bgjqc,cn->bgqn', p_cmp, overlap)
        qblk = t // SEL_BLK
        forced = (s_idx[None, :] == 0) | (s_idx[None, :] == qblk[:, None]) | (s_idx[None, :] == qblk[:, None] - 1)
        imp = jnp.where(forced, FORCE_SCORE, jnp.where(s_idx[None, :] > qblk[:, None], NEG_INF, imp))
        _, sel = lax.top_k(imp, n_sel)
        k_sel = ks_blk[bi, gi, sel]
        v_sel = vs_blk[bi, gi, sel].reshape(B, G, NSA_QB, n_sel * SEL_BLK, dh)
        s = jnp.einsum('bgjqd,bgqnkd->bgjqnk', qi, k_sel).astype(jnp.float32) * scale
        kpos = sel[..., None] * SEL_BLK + jnp.arange(SEL_BLK)
        valid = (kpos <= t[None, None, :, None, None]).reshape(B, G, 1, NSA_QB, n_sel * SEL_BLK)
        p_slc = masked_softmax(s.reshape(B, G, J, NSA_QB, n_sel * SEL_BLK), valid)
        o_slc = jnp.einsum('bgjqm,bgqmd->bgjqd', p_slc, v_sel)
        kwin = lax.dynamic_slice_in_dim(kw_pad, q0, NSA_WINDOW + NSA_QB, axis=1)
        vwin = lax.dynamic_slice_in_dim(vw_pad, q0, NSA_WINDOW + NSA_QB, axis=1)
        wpos = q0 - NSA_WINDOW + jnp.arange(NSA_WINDOW + NSA_QB)
        dist = t[:, None] - wpos[None, :]
        wvalid = (dist >= 0) & (dist < NSA_WINDOW) & (wpos[None, :] >= 0)
        s = jnp.einsum('bgjqd,bkgd->bgjqk', qi, kwin).astype(jnp.float32) * scale
        o_win = jnp.einsum('bgjqk,bkgd->bgjqd', masked_softmax(s, wvalid), vwin)
        return gate[..., 0:1] * o_cmp + gate[..., 1:2] * o_slc + gate[..., 2:3] * o_win

    o = lax.map(block, (qb, gb, q0s))
    o = o.transpose(1, 0, 4, 2, 3, 5).reshape(B, S, H * dh)
    return o.astype(x.dtype) @ w_out


def dilated_band_attn(q, k, v, window, dil):
    B, S, H, dh = q.shape
    steps = window // dil
    span = steps * dil
    sp = -(-S // span) * span
    nb = sp // span
    padw = ((0, 0), (0, sp - S), (0, 0), (0, 0))
    blocks = lambda t: jnp.pad(t, padw).reshape(B, nb, steps, dil, H, dh)
    band = lambda t: jnp.concatenate(
        [jnp.pad(t, ((0, 0), (1, 0), (0, 0), (0, 0), (0, 0), (0, 0)))[:, :-1], t], axis=2)
    qb = blocks(q)
    kband, vband = band(blocks(k)), band(blocks(v))
    s = jnp.einsum('bnqrhd,bnkrhd->bnrhqk', qb, kband).astype(jnp.float32) * dh ** -0.5
    qi, kj = jnp.arange(steps), jnp.arange(2 * steps)
    dist = steps + qi[:, None] - kj[None, :]
    valid = ((dist >= 0) & (dist <= steps))[None] & ~((jnp.arange(nb) == 0)[:, None, None] & (kj < steps)[None, None, :])
    s = jnp.where(valid[None, :, None, None], s, NEG_INF)
    m = s.max(-1, keepdims=True)
    e = jnp.exp(s - m)
    den = e.sum(-1, keepdims=True)
    o = jnp.einsum('bnrhqk,bnkrhd->bnqrhd', e / den, vband).reshape(B, sp, H, dh)[:, :S]
    lse = (m + jnp.log(den))[..., 0].transpose(0, 1, 4, 2, 3).reshape(B, sp, H)[:, :S]
    return o, lse


def dilated_mixer(x, w_in, w_out):
    B, S, _ = x.shape
    H, dh = DIL_HEADS, DIL_HEAD_DIM
    parts = split_cols(x @ w_in, [H * dh] * (2 * len(DIL_GROUPS) + 1))
    v = parts[-1].reshape(B, S, H, dh)
    pos = jnp.arange(S)
    outs, lses = [], []
    for gi_, (window, dil) in enumerate(DIL_GROUPS):
        q = rope(parts[2 * gi_].reshape(B, S, H, dh), pos)
        k = rope(parts[2 * gi_ + 1].reshape(B, S, H, dh), pos)
        o, lse = dilated_band_attn(q, k, v, window, dil)
        outs.append(o)
        lses.append(lse)
    wts = jax.nn.softmax(jnp.stack(lses, 0), axis=0)
    o = jnp.einsum('gbsh,gbshd->bshd', wts, jnp.stack(outs, 0))
    return o.reshape(B, S, H * dh).astype(x.dtype) @ w_out


def pool_mixer(x, w_grp, scale):
    B, S, _ = x.shape
    xf = x.astype(jnp.float32)
    cs = lax.cumsum(xf, axis=1)
    cnt = jnp.arange(1, S + 1, dtype=jnp.float32)[None, :, None]
    outs = []
    for g, w in enumerate(POOL_WINDOWS):
        sl = slice(g * POOL_GROUP, (g + 1) * POOL_GROUP)
        c = cs[..., sl]
        lagged = jnp.pad(c, ((0, 0), (w, 0), (0, 0)))[:, :S]
        mean = (c - lagged) / jnp.minimum(cnt, float(w))
        outs.append((mean - xf[..., sl]).astype(x.dtype) @ w_grp[g])
    return jnp.concatenate(outs, -1) * scale


def routed_experts(xf, eidx, gw, w_gate, w_up, w_down):
    T, D = xf.shape
    A = T * TOP_K
    flat_e = eidx.reshape(A)
    order = jnp.argsort(flat_e)
    sorted_e = flat_e[order]
    counts = jnp.bincount(flat_e, length=N_EXPERTS)
    padded = (counts + ROW_BLK - 1) // ROW_BLK * ROW_BLK
    pad_end = jnp.cumsum(padded)
    pad_start = pad_end - padded
    start = jnp.cumsum(counts) - counts
    dest = pad_start[sorted_e] + (jnp.arange(A) - start[sorted_e])
    n_blk = -(-A // ROW_BLK) + N_EXPERTS
    tok = order // TOP_K
    rows = jnp.zeros((n_blk * ROW_BLK, D), xf.dtype).at[dest].set(xf[tok])
    blk_expert = jnp.minimum(jnp.searchsorted(pad_end, jnp.arange(n_blk) * ROW_BLK, side='right'), N_EXPERTS - 1)

    def expert_block(args):
        xb, e = args
        return swiglu(xb, w_gate[e], w_up[e], w_down[e])

    out = lax.map(expert_block, (rows.reshape(n_blk, ROW_BLK, D), blk_expert)).reshape(-1, D)
    contrib = out[dest].astype(jnp.float32) * gw.reshape(A)[order][:, None]
    return jax.ops.segment_sum(contrib, tok, num_segments=T)


def moe_ffn(x, router_w, router_bias, w_gate, w_up, w_down, ws_gate, ws_up, ws_down):
    B, S, D = x.shape
    T = B * S
    xf = x.reshape(T, D)
    scores = jax.nn.sigmoid((xf @ router_w).astype(jnp.float32))
    biased = scores + router_bias.astype(jnp.float32)
    per = N_EXPERTS // N_EXPERT_GROUPS
    grp_score = lax.top_k(biased.reshape(T, N_EXPERT_GROUPS, per), 2)[0].sum(-1)
    _, gidx = lax.top_k(grp_score, TOPK_GROUPS)
    gmask = jax.nn.one_hot(gidx, N_EXPERT_GROUPS, dtype=jnp.float32).sum(1) > 0
    emask = jnp.repeat(gmask, per, axis=1)
    _, eidx = lax.top_k(jnp.where(emask, biased, NEG_INF), TOP_K)
    gw = jnp.take_along_axis(scores, eidx, axis=1)
    gw = gw / gw.sum(-1, keepdims=True) * ROUTED_SCALE
    routed = routed_experts(xf, eidx, gw, w_gate, w_up, w_down)
    shared = swiglu(xf, ws_gate, ws_up, ws_down).astype(jnp.float32)
    return (routed + shared).astype(x.dtype).reshape(B, S, D)


def setup_inputs(seed: int = 0) -> dict:
    key = jax.random.key(seed)
    kit = iter(list(jax.random.split(key, 40)))
    f32 = jnp.float32

    def nrm(shape, scale):
        return jax.random.normal(next(kit), shape, f32) * scale

    D = D_MODEL
    nA, nB, nC, nD = (_n_uses(m) for m in range(N_MIXERS))
    mH, dk, dv = MLSTM_HEADS, MLSTM_QK_DIM, MLSTM_V_DIM
    nH, nG, ndh = NSA_HEADS, NSA_KV_HEADS, NSA_HEAD_DIM
    dH, ddh = DIL_HEADS, DIL_HEAD_DIM
    mlstm_in = 2 * mH * dk + 2 * mH * dv + 2 * mH
    nsa_in = nH * ndh + 6 * nG * ndh + 3 * nH
    dil_in = (2 * len(DIL_GROUPS) + 1) * dH * ddh
    return {
        "x": nrm((BATCH, SEQ, D), 1.0),
        "p": nrm((DEPTH, BATCH, SEQ, PLE_DIM), 1.0),
        "ln_g": 1.0 + nrm((DEPTH, 2, D), 0.02),
        "ln_b": nrm((DEPTH, 2, D), 0.02),
        "mlstm_w_in": nrm((nA, D, mlstm_in), D ** -0.5),
        "mlstm_conv": nrm((nA, MLSTM_CONV, 2 * mH * dk), MLSTM_CONV ** -0.5),
        "mlstm_ig_bias": nrm((nA, mH), 0.1),
        "mlstm_fg_bias": jnp.linspace(3.0, 6.0, mH, dtype=f32)[None, :] + nrm((nA, mH), 0.1),
        "mlstm_norm_g": 1.0 + nrm((nA, mH * dv), 0.02),
        "mlstm_w_out": nrm((nA, mH * dv, D), BETA * (mH * dv) ** -0.5),
        "nsa_w_in": nrm((nB, D, nsa_in), D ** -0.5),
        "nsa_cmp_pos_k": nrm((nB, CMP_BLK, ndh), 0.02),
        "nsa_cmp_pos_v": nrm((nB, CMP_BLK, ndh), 0.02),
        "nsa_cmp_wk": nrm((nB, CMP_BLK, ndh, ndh), (CMP_BLK * ndh) ** -0.5),
        "nsa_cmp_wv": nrm((nB, CMP_BLK, ndh, ndh), (CMP_BLK * ndh) ** -0.5),
        "nsa_w_out": nrm((nB, nH * ndh, D), BETA * (nH * ndh) ** -0.5),
        "dil_w_in": nrm((nC, D, dil_in), D ** -0.5),
        "dil_w_out": nrm((nC, dH * ddh, D), BETA * (dH * ddh) ** -0.5),
        "pool_w": nrm((nD, len(POOL_WINDOWS), POOL_GROUP, POOL_GROUP), BETA * POOL_GROUP ** -0.5),
        "pool_scale": 1.0 + nrm((nD, D), 0.1),
        "router_w": nrm((DEPTH, D, N_EXPERTS), D ** -0.5),
        "router_bias": nrm((DEPTH, N_EXPERTS), 0.01),
        "exp_w_gate": nrm((DEPTH, N_EXPERTS, D, EXPERT_FF), D ** -0.5),
        "exp_w_up": nrm((DEPTH, N_EXPERTS, D, EXPERT_FF), D ** -0.5),
        "exp_w_down": nrm((DEPTH, N_EXPERTS, EXPERT_FF, D), BETA * EXPERT_FF ** -0.5),
        "sh_w_gate": nrm((DEPTH, D, SHARED_FF), D ** -0.5),
        "sh_w_up": nrm((DEPTH, D, SHARED_FF), D ** -0.5),
        "sh_w_down": nrm((DEPTH, SHARED_FF, D), BETA * SHARED_FF ** -0.5),
        "ple_w": nrm((DEPTH, PLE_DIM, D), BETA * PLE_DIM ** -0.5),
        "ple_gate_w": nrm((DEPTH, D, D), D ** -0.5),
    }


def reference(x, p, ln_g, ln_b, mlstm_w_in, mlstm_conv, mlstm_ig_bias, mlstm_fg_bias, mlstm_norm_g,
              mlstm_w_out, nsa_w_in, nsa_cmp_pos_k, nsa_cmp_pos_v, nsa_cmp_wk, nsa_cmp_wv, nsa_w_out,
              dil_w_in, dil_w_out, pool_w, pool_scale, router_w, router_bias, exp_w_gate, exp_w_up,
              exp_w_down, sh_w_gate, sh_w_up, sh_w_down, ple_w, ple_gate_w):
    for i in range(DEPTH):
        kind, j = i % N_MIXERS, i // N_MIXERS
        if kind == 0:
            y = mlstm_mixer(x, mlstm_w_in[j], mlstm_conv[j], mlstm_ig_bias[j], mlstm_fg_bias[j],
                            mlstm_norm_g[j], mlstm_w_out[j])
        elif kind == 1:
            y = nsa_mixer(x, nsa_w_in[j], nsa_cmp_pos_k[j], nsa_cmp_pos_v[j], nsa_cmp_wk[j],
                          nsa_cmp_wv[j], nsa_w_out[j])
        elif kind == 2:
            y = dilated_mixer(x, dil_w_in[j], dil_w_out[j])
        else:
            y = pool_mixer(x, pool_w[j], pool_scale[j])
        x = layer_norm(ALPHA * x + y, ln_g[i, 0], ln_b[i, 0])
        f = moe_ffn(x, router_w[i], router_bias[i], exp_w_gate[i], exp_w_up[i], exp_w_down[i],
                    sh_w_gate[i], sh_w_up[i], sh_w_down[i])
        x = layer_norm(ALPHA * x + f, ln_g[i, 1], ln_b[i, 1])
        x = x + (jax.nn.sigmoid(x @ ple_gate_w[i]) * (p[i] @ ple_w[i])).astype(x.dtype)
    return x
```

```python
import functools

import numpy as np
import jax
import jax.numpy as jnp
from jax import lax
from jax.experimental import pallas as pl
from jax.experimental.pallas import tpu as pltpu

F32 = jnp.float32
BF16 = jnp.bfloat16
I32 = jnp.int32
HIGHEST = lax.Precision.HIGHEST

D_MODEL = 1024
DEPTH = 4
ALPHA = (2.0 * DEPTH) ** 0.25
LN_EPS = 1e-5
NEG_INF = -1e30
TINY = 1e-30
BELOW_NEG_INF = -3e38
ROPE_THETA = 10000.0
HEAD_DIM = 64
ROPE_HALF = HEAD_DIM // 2

MLSTM_HEADS = 8
MLSTM_QK_DIM = 64
MLSTM_V_DIM = 128
MLSTM_CONV = 4
MLSTM_L = 256

NSA_HEADS = 16
NSA_KV_HEADS = 4
NSA_J = NSA_HEADS // NSA_KV_HEADS
CMP_STRIDE = 16
CMP_BLK = 32
SEL_BLK = 64
N_SEL = 16
NSA_WINDOW = 512
FORCE_SCORE = 1e4
NSA_TQ = 128
NSA_TK = 512

DIL_HEADS = 16
DIL_GROUPS = ((128, 1), (512, 4), (2048, 16))
DIL_STEPS = 128

POOL_WINDOWS = (2, 4, 8, 16)
POOL_GROUP = 256
POOL_HALO = 16

N_EXPERTS = 64
TOP_K = 8
N_EXPERT_GROUPS = 8
TOPK_GROUPS = 4
EXPERT_FF = 256
ROUTED_SCALE = 2.5
EXP_BLK = 256
PLE_DIM = 256

VMEM_LIMIT = 48 * 1024 * 1024


def _cparams(*sem):
    return pltpu.CompilerParams(dimension_semantics=sem, vmem_limit_bytes=VMEM_LIMIT)


def _nt(a, b, **kw):
    return lax.dot_general(a, b, (((1,), (1,)), ((), ())), preferred_element_type=F32, **kw)


def _tn(a, b, **kw):
    return lax.dot_general(a, b, (((0,), (0,)), ((), ())), preferred_element_type=F32, **kw)


def _mm(a, b, **kw):
    return jnp.dot(a, b, preferred_element_type=F32, **kw)


def _sigmoid(z):
    return 1.0 / (1.0 + jnp.exp(-z))


def _silu(z):
    return z * _sigmoid(z)


def _ln_rows(z, g, b):
    mu = jnp.mean(z, axis=-1, keepdims=True)
    d = z - mu
    var = jnp.mean(d * d, axis=-1, keepdims=True)
    return d * lax.rsqrt(var + LN_EPS) * g + b


def _proj_kernel(*refs, rope):
    if rope:
        x_ref, w_ref, cos_ref, sin_ref, o_ref, xb_ref = refs
    else:
        x_ref, w_ref, o_ref, xb_ref = refs

    @pl.when(pl.program_id(1) == 0)
    def _():
        xb_ref[...] = x_ref[...].astype(BF16)

    acc = _mm(xb_ref[...], w_ref[...])
    if rope:
        tn = acc.shape[1]
        lane = lax.broadcasted_iota(I32, acc.shape, 1)
        lo = (lane % HEAD_DIM) < ROPE_HALF
        rot = jnp.where(lo, pltpu.roll(acc, tn - ROPE_HALF, 1), pltpu.roll(acc, ROPE_HALF, 1))
        acc = acc * cos_ref[...] + rot * sin_ref[...]
    o_ref[...] = acc.astype(o_ref.dtype)


def _proj(x, w, *, out_dtype, tm, tn, rope_tables=None, seq=None):
    T, K = x.shape
    N = w.shape[1]
    assert T % tm == 0 and N % tn == 0
    in_specs = [pl.BlockSpec((tm, K), lambda i, j: (i, 0)), pl.BlockSpec((K, tn), lambda i, j: (0, j))]
    args = [x, w]
    if rope_tables is not None:
        nseq = seq // tm
        in_specs += [pl.BlockSpec((tm, tn), lambda i, j: (i % nseq, 0))] * 2
        args += list(rope_tables)
    return pl.pallas_call(
        functools.partial(_proj_kernel, rope=rope_tables is not None),
        grid=(T // tm, N // tn),
        in_specs=in_specs,
        out_specs=pl.BlockSpec((tm, tn), lambda i, j: (i, j)),
        out_shape=jax.ShapeDtypeStruct((T, N), out_dtype),
        scratch_shapes=[pltpu.VMEM((tm, K), BF16)],
        compiler_params=_cparams("parallel", "arbitrary"),
    )(*args)


def _rope_tables(pos, width):
    inv = ROPE_THETA ** (-jnp.arange(ROPE_HALF, dtype=F32) / ROPE_HALF)
    ang = pos.astype(F32)[:, None] * inv[None, :]
    cos, sin = jnp.cos(ang), jnp.sin(ang)
    cos64 = jnp.concatenate([cos, cos], -1)
    sin64 = jnp.concatenate([-sin, sin], -1)
    rep = width // HEAD_DIM
    return jnp.tile(cos64, (1, rep)), jnp.tile(sin64, (1, rep))


def _outproj_ln_kernel(y_ref, w_ref, x_ref, g_ref, b_ref, o_ref):
    y = _mm(y_ref[...].astype(BF16), w_ref[...])
    o_ref[...] = _ln_rows(ALPHA * x_ref[...] + y, g_ref[...], b_ref[...])


def _outproj_ln(y, w, x, g, b, *, tm=256):
    T, D = x.shape
    K = y.shape[1]
    row = lambda i: (i, 0)
    fixed = lambda i: (0, 0)
    return pl.pallas_call(
        _outproj_ln_kernel,
        grid=(T // tm,),
        in_specs=[pl.BlockSpec((tm, K), row), pl.BlockSpec((K, D), fixed), pl.BlockSpec((tm, D), row),
                  pl.BlockSpec((1, D), fixed), pl.BlockSpec((1, D), fixed)],
        out_specs=pl.BlockSpec((tm, D), row),
        out_shape=jax.ShapeDtypeStruct((T, D), F32),
        compiler_params=_cparams("parallel"),
    )(y, w, x, g.reshape(1, D), b.reshape(1, D))


def _log_sigmoid(z):
    return jnp.minimum(z, 0.0) - jnp.log(1.0 + jnp.exp(-jnp.abs(z)))


def _mlstm_kernel(qk_ref, v_ref, o_ref, gc_ref, gr_ref, convw_ref, bias_c_ref, bias_r_ref, ng_ref,
                  tri_ref, triT_ref, out_ref, C_ref, n_ref, m_ref, ext_ref):
    L = qk_ref.shape[0]
    H, dk, dv = MLSTM_HEADS, MLSTM_QK_DIM, MLSTM_V_DIM

    @pl.when(pl.program_id(1) == 0)
    def _():
        C_ref[...] = jnp.zeros_like(C_ref)
        n_ref[...] = jnp.zeros_like(n_ref)
        m_ref[...] = jnp.zeros_like(m_ref)
        ext_ref[0:8, :] = jnp.zeros((8, ext_ref.shape[1]), F32)

    cur = qk_ref[...]
    ext_ref[8:8 + L, :] = cur
    acc = convw_ref[3:4, :] * cur
    for j in range(MLSTM_CONV - 1):
        acc = acc + convw_ref[j:j + 1, :] * ext_ref[5 + j:5 + j + L, :]
    ext_ref[0:8, :] = cur[L - 8:L, :]
    qk = _silu(acc)

    gc = gc_ref[...] + bias_c_ref[...]
    gr = gr_ref[...] + bias_r_ref[...]
    b_col = _mm(tri_ref[...], _log_sigmoid(gc), precision=HIGHEST)
    b_row = _mm(_log_sigmoid(gr[H:2 * H, :]), triT_ref[...], precision=HIGHEST)
    ig_row = gr[0:H, :]
    tri = lax.broadcasted_iota(I32, (L, L), 0) >= lax.broadcasted_iota(I32, (L, L), 1)

    for h in range(H):
        q = (qk[:, h * dk:(h + 1) * dk] * dk ** -0.5).astype(BF16)
        k = qk[:, H * dk + h * dk:H * dk + (h + 1) * dk]
        kb = k.astype(BF16)
        v = v_ref[:, h * dv:(h + 1) * dv].astype(BF16)
        b_c = b_col[:, H + h:H + h + 1]
        ig_c = gc[:, h:h + 1]
        b_r = b_row[h:h + 1, :]
        m_prev = m_ref[h:h + 1, 0:1]
        C = C_ref[h]
        n = n_ref[h:h + 1, :]

        logD = jnp.where(tri, b_c - b_r + ig_row[h:h + 1, :], NEG_INF)
        inter = b_c + m_prev
        m_t = jnp.maximum(inter, jnp.max(logD, axis=-1, keepdims=True))
        s = _nt(q, kb) * jnp.exp(logD - m_t)
        w_inter = jnp.exp(inter - m_t)
        num = w_inter * _mm(q, C.astype(BF16)) + _mm(s.astype(BF16), v)
        den = w_inter * jnp.sum(q.astype(F32) * n, axis=-1, keepdims=True) + jnp.sum(s, axis=-1, keepdims=True)
        hv = num / jnp.maximum(jnp.abs(den), jnp.exp(-m_t))

        mu = jnp.mean(hv, axis=-1, keepdims=True)
        d = hv - mu
        hn = d * lax.rsqrt(jnp.mean(d * d, axis=-1, keepdims=True) + LN_EPS)
        og = _sigmoid(o_ref[:, h * dv:(h + 1) * dv])
        out_ref[:, h * dv:(h + 1) * dv] = (hn * ng_ref[:, h * dv:(h + 1) * dv] * og).astype(out_ref.dtype)

        bL = b_c[L - 1:L, :]
        logw = bL - b_c + ig_c
        m_new = jnp.maximum(bL + m_prev, jnp.max(logw, axis=0, keepdims=True))
        decay = jnp.exp(bL + m_prev - m_new)
        kw = k * jnp.exp(logw - m_new)
        C_ref[h] = decay * C + _tn(kw.astype(BF16), v)
        n_ref[h:h + 1, :] = decay * n + jnp.sum(kw, axis=0, keepdims=True)
        m_ref[h:h + 1, :] = jnp.broadcast_to(m_new, (1, m_ref.shape[1]))


def _mlstm_mixer(x, B, S, w_in, conv_w, ig_bias, fg_bias, norm_g):
    T = B * S
    H, L = MLSTM_HEADS, MLSTM_L
    wb = w_in.astype(BF16)
    main = _proj(x, wb[:, :3 * D_MODEL], out_dtype=F32, tm=512, tn=512)
    wg = jnp.pad(wb[:, 3 * D_MODEL:], ((0, 0), (0, 128 - 2 * H)))
    gates = _proj(x, wg, out_dtype=F32, tm=512, tn=128)
    gates_r = gates[:, :2 * H].reshape(B, S, 2 * H).transpose(0, 2, 1)
    bias16 = jnp.concatenate([ig_bias, fg_bias]).astype(F32)
    bias_c = jnp.pad(bias16, (0, 128 - 2 * H)).reshape(1, 128)
    bias_r = bias16.reshape(2 * H, 1)
    tri = jnp.tril(jnp.ones((L, L), F32))
    nc = S // L
    rowblk = lambda c: (lambda b, i: (b * nc + i, c))
    fixed = lambda b, i: (0, 0)
    return pl.pallas_call(
        _mlstm_kernel,
        grid=(B, nc),
        in_specs=[pl.BlockSpec((L, D_MODEL), rowblk(0)), pl.BlockSpec((L, D_MODEL), rowblk(1)),
                  pl.BlockSpec((L, D_MODEL), rowblk(2)), pl.BlockSpec((L, 128), rowblk(0)),
                  pl.BlockSpec((None, 2 * H, L), lambda b, i: (b, 0, i)),
                  pl.BlockSpec((MLSTM_CONV, D_MODEL), fixed), pl.BlockSpec((1, 128), fixed),
                  pl.BlockSpec((2 * H, 1), fixed), pl.BlockSpec((1, D_MODEL), fixed),
                  pl.BlockSpec((L, L), fixed), pl.BlockSpec((L, L), fixed)],
        out_specs=pl.BlockSpec((L, D_MODEL), rowblk(0)),
        out_shape=jax.ShapeDtypeStruct((T, D_MODEL), BF16),
        scratch_shapes=[pltpu.VMEM((H, MLSTM_QK_DIM, MLSTM_V_DIM), F32), pltpu.VMEM((H, MLSTM_QK_DIM), F32),
                        pltpu.VMEM((H, 128), F32), pltpu.VMEM((L + 8, D_MODEL), F32)],
        compiler_params=_cparams("parallel", "arbitrary"),
    )(main, main, main, gates, gates_r, conv_w.astype(F32), bias_c, bias_r, norm_g.reshape(1, D_MODEL).astype(F32),
      tri, tri.T)


def _rot_cols(w):
    shp = w.shape
    w4 = w.reshape(shp[:-1] + (shp[-1] // HEAD_DIM, 2, ROPE_HALF))
    return jnp.flip(w4, axis=-2).reshape(shp)


def _nsa_compress_kernel(chk_ref, chv_ref, wk_ref, wv_ref, pk_ref, pv_ref, cos_ref, sin_ref, kc_ref, vc_ref, sh_ref):
    NCH = chk_ref.shape[0]
    half = wk_ref.shape[0] // 2
    sh_ref[NCH:NCH + 8, :] = jnp.zeros((8, sh_ref.shape[1]), F32)
    live = lax.broadcasted_iota(I32, (NCH, 1), 0) < NCH - 1

    def blocks(ch_ref, w_ref, p_ref):
        ch = ch_ref[...].astype(BF16)
        a = _mm(ch, w_ref[0:half, :])
        sh_ref[0:NCH, :] = _mm(ch, w_ref[half:2 * half, :])
        const = _mm(p_ref[...].astype(BF16), w_ref[...])[0:1, :]
        return jnp.where(live, a + sh_ref[1:NCH + 1, :] + const, 0.0)

    k2 = blocks(chk_ref, wk_ref, pk_ref)
    kc_ref[...] = (k2[:, :HEAD_DIM] * cos_ref[...] + k2[:, HEAD_DIM:] * sin_ref[...]).astype(kc_ref.dtype)
    v2 = blocks(chv_ref, wv_ref, pv_ref)
    vc_ref[...] = v2[:, :HEAD_DIM].astype(vc_ref.dtype)


def _nsa_cmp_select_kernel(q_ref, kc_ref, vc_ref, ovl_ref, ocmp_ref, sel_ref):
    TQ = q_ref.shape[0]
    NCH = kc_ref.shape[0]
    NSB = ovl_ref.shape[1]
    q0 = pl.program_id(2) * TQ
    t = q0 + lax.broadcasted_iota(I32, (TQ, 1), 0)
    cend = lax.broadcasted_iota(I32, (1, NCH), 1) * CMP_STRIDE + (CMP_BLK - 1)
    valid = cend <= t
    kc = kc_ref[...]
    vc = vc_ref[...]
    psum = jnp.zeros((TQ, NCH), F32)
    outs = []
    for j in range(NSA_J):
        qj = q_ref[:, j * HEAD_DIM:(j + 1) * HEAD_DIM]
        s = jnp.where(valid, _nt(qj, kc) * HEAD_DIM ** -0.5, NEG_INF)
        m = jnp.max(s, axis=-1, keepdims=True)
        e = jnp.where(valid, jnp.exp(s - m), 0.0)
        p = e / jnp.maximum(jnp.sum(e, axis=-1, keepdims=True), TINY)
        outs.append(_mm(p.astype(BF16), vc))
        psum = psum + p
    ocmp_ref[...] = jnp.concatenate(outs, axis=1).astype(ocmp_ref.dtype)

    imp = _mm(psum, ovl_ref[...], precision=HIGHEST)
    nb = lax.broadcasted_iota(I32, (1, NSB), 1)
    qblk = t // SEL_BLK
    forced = (nb == 0) | (nb == qblk) | (nb == qblk - 1)
    cur = jnp.where(forced, FORCE_SCORE, jnp.where(nb > qblk, NEG_INF, imp))
    sel = jnp.zeros((TQ, NSB), F32)
    for _ in range(min(N_SEL, NSB)):
        m = jnp.max(cur, axis=-1, keepdims=True)
        idx = jnp.min(jnp.where(cur == m, nb, NSB), axis=-1, keepdims=True)
        hit = nb == idx
        sel = jnp.where(hit, 1.0, sel)
        cur = jnp.where(hit, BELOW_NEG_INF, cur)
    sel_ref[...] = sel


def _masked_softmax_pv(s, ok, v):
    s = jnp.where(ok, s, NEG_INF)
    m = jnp.max(s, axis=-1, keepdims=True)
    e = jnp.where(ok, jnp.exp(s - m), 0.0)
    p = e / jnp.maximum(jnp.sum(e, axis=-1, keepdims=True), TINY)
    return _mm(p.astype(BF16), v)


def _nsa_main_kernel(q_ref, ksT_ref, vs_ref, kwT_ref, vw_ref, sel_ref, ocmp_ref, gate_ref, gexp_ref, o_ref):
    TQ = q_ref.shape[0]
    S = vs_ref.shape[0]
    TK = min(NSA_TK, S)
    J = NSA_J
    scale = HEAD_DIM ** -0.5
    q0 = pl.program_id(2) * TQ
    q4 = jnp.concatenate([q_ref[:, j * HEAD_DIM:(j + 1) * HEAD_DIM] for j in range(J)], axis=0)
    t = q0 + lax.broadcasted_iota(I32, (TQ, 1), 0)
    selm = sel_ref[...].astype(BF16)
    rep = lambda a: jnp.concatenate([a] * J, axis=0)

    def body(c, carry):
        m, l, acc = carry
        k0 = pl.multiple_of(c * TK, TK)
        kT = ksT_ref[:, pl.ds(k0, TK)]
        v = vs_ref[pl.ds(k0, TK), :]
        kpos = k0 + lax.broadcasted_iota(I32, (1, TK), 1)
        expand = jnp.where(kpos // SEL_BLK == lax.broadcasted_iota(I32, (selm.shape[1], 1), 0), 1.0, 0.0).astype(BF16)
        okf = jnp.where((_mm(selm, expand) > 0.5) & (kpos <= t), 1.0, 0.0)
        ok = rep(okf) > 0.5
        s = jnp.where(ok, _mm(q4, kT) * scale, NEG_INF)
        m_new = jnp.maximum(m, jnp.max(s, axis=-1, keepdims=True))
        e = jnp.where(ok, jnp.exp(s - m_new), 0.0)
        a = jnp.exp(m - m_new)
        return m_new, a * l + jnp.sum(e, axis=-1, keepdims=True), a * acc + _mm(e.astype(BF16), v)

    nchunks = (q0 + TQ - 1) // TK + 1
    init = (jnp.full((J * TQ, 1), NEG_INF, F32), jnp.zeros((J * TQ, 1), F32), jnp.zeros((J * TQ, HEAD_DIM), F32))
    _, l, acc = lax.fori_loop(0, nchunks, body, init)
    o_slc = acc / jnp.maximum(l, TINY)

    span = min(NSA_WINDOW + TQ, S)
    start = pl.multiple_of(jnp.clip(q0 - NSA_WINDOW, 0, S - span), TQ)
    kpos = start + lax.broadcasted_iota(I32, (1, span), 1)
    dist = t - kpos
    okw = rep(jnp.where((dist >= 0) & (dist < NSA_WINDOW), 1.0, 0.0)) > 0.5
    o_win = _masked_softmax_pv(_mm(q4, kwT_ref[:, pl.ds(start, span)]) * scale, okw, vw_ref[pl.ds(start, span), :])

    unstack = lambda o: jnp.concatenate([o[j * TQ:(j + 1) * TQ, :] for j in range(J)], axis=1)
    g = _sigmoid(gate_ref[...])
    gx = [_mm(g, gexp_ref[c], precision=HIGHEST) for c in range(3)]
    out = gx[0] * ocmp_ref[...].astype(F32) + gx[1] * unstack(o_slc) + gx[2] * unstack(o_win)
    o_ref[...] = out.astype(o_ref.dtype)


def _nsa_mixer(x, B, S, w_in, cmp_pos_k, cmp_pos_v, cmp_wk, cmp_wv):
    T = B * S
    H, G, J, dh = NSA_HEADS, NSA_KV_HEADS, NSA_J, HEAD_DIM
    kv = G * dh
    wb = w_in.astype(BF16)
    col = lambda a, n: wb[:, a:a + n]
    o_q, o_kc, o_vc, o_ks, o_vs, o_kw, o_vw, o_g = np.cumsum([0, H * dh] + [kv] * 6).tolist()
    pos = jnp.arange(S)
    w_rope = jnp.concatenate([col(o_q, H * dh), col(o_ks, kv), col(o_kw, kv)], axis=1)
    roped = _proj(x, w_rope, out_dtype=BF16, tm=512, tn=512, rope_tables=_rope_tables(pos, 512), seq=S)
    w_plain = jnp.concatenate([col(o_kc, kv), col(o_vc, kv), col(o_vs, kv), col(o_vw, kv),
                               jnp.pad(col(o_g, 3 * H), ((0, 0), (0, 128 - 3 * H)))], axis=1)
    plain = _proj(x, w_plain, out_dtype=F32, tm=512, tn=128)

    heads_T = lambda a: a.reshape(B, S, G, dh).transpose(0, 2, 3, 1)
    heads = lambda a: a.reshape(B, S, G, dh).transpose(0, 2, 1, 3)
    ksT = heads_T(roped[:, H * dh:H * dh + kv])
    kwT = heads_T(roped[:, H * dh + kv:])
    vs = heads(plain[:, 2 * kv:3 * kv]).astype(BF16)
    vw = heads(plain[:, 3 * kv:4 * kv]).astype(BF16)
    gates = plain[:, 4 * kv:]

    nch = S // CMP_STRIDE
    chunks = lambda a: a.reshape(B, nch, CMP_STRIDE, G, dh).transpose(0, 3, 1, 2, 4).reshape(B, G, nch, CMP_STRIDE * dh)
    chk, chv = chunks(plain[:, :kv]), chunks(plain[:, kv:2 * kv])
    wk = cmp_wk.reshape(CMP_BLK * dh, dh)
    wk2 = jnp.concatenate([wk, _rot_cols(wk)], axis=1).astype(BF16)
    wv2 = jnp.pad(cmp_wv.reshape(CMP_BLK * dh, dh), ((0, 0), (0, dh))).astype(BF16)
    flat8 = lambda p_: jnp.pad(p_.reshape(1, CMP_BLK * dh), ((0, 7), (0, 0))).astype(F32)
    cend = jnp.arange(nch) * CMP_STRIDE + CMP_BLK - 1
    cos_c, sin_c = _rope_tables(cend, dh)
    bg = lambda b, g: (b, g, 0, 0)
    fixed2 = lambda b, g: (0, 0)
    kc, vc = pl.pallas_call(
        _nsa_compress_kernel,
        grid=(B, G),
        in_specs=[pl.BlockSpec((None, None, nch, CMP_STRIDE * dh), bg)] * 2
        + [pl.BlockSpec((CMP_BLK * dh, 2 * dh), fixed2)] * 2 + [pl.BlockSpec((8, CMP_BLK * dh), fixed2)] * 2
        + [pl.BlockSpec((nch, dh), fixed2)] * 2,
        out_specs=[pl.BlockSpec((None, None, nch, dh), bg)] * 2,
        out_shape=[jax.ShapeDtypeStruct((B, G, nch, dh), BF16)] * 2,
        scratch_shapes=[pltpu.VMEM((nch + 8, 2 * dh), F32)],
        compiler_params=_cparams("parallel", "parallel"),
    )(chk, chv, wk2, wv2, flat8(cmp_pos_k), flat8(cmp_pos_v), cos_c, sin_c)

    nsb = S // SEL_BLK
    c_idx, s_idx = np.arange(nch), np.arange(nsb)
    ovl = ((c_idx[:, None] * CMP_STRIDE + CMP_BLK - 1 >= s_idx[None, :] * SEL_BLK)
           & (c_idx[:, None] * CMP_STRIDE < (s_idx[None, :] + 1) * SEL_BLK)).astype(np.float32)
    TQ = NSA_TQ
    nq = S // TQ
    qblk = lambda b, g, i: (b * nq + i, g)
    bgi = lambda b, g, i: (b, g, 0, 0)
    ocmp, sel = pl.pallas_call(
        _nsa_cmp_select_kernel,
        grid=(B, G, nq),
        in_specs=[pl.BlockSpec((TQ, J * dh), qblk), pl.BlockSpec((None, None, nch, dh), bgi),
                  pl.BlockSpec((None, None, nch, dh), bgi), pl.BlockSpec((nch, nsb), lambda b, g, i: (0, 0))],
        out_specs=[pl.BlockSpec((TQ, J * dh), qblk), pl.BlockSpec((None, None, TQ, nsb), lambda b, g, i: (b, g, i, 0))],
        out_shape=[jax.ShapeDtypeStruct((T, H * dh), BF16), jax.ShapeDtypeStruct((B, G, S, nsb), F32)],
        compiler_params=_cparams("parallel", "parallel", "parallel"),
    )(roped, kc, vc, jnp.asarray(ovl))

    gexp = np.zeros((G, 3, 128, J * dh), np.float32)
    for g in range(G):
        for c in range(3):
            for j in range(J):
                gexp[g, c, (g * J + j) * 3 + c, j * dh:(j + 1) * dh] = 1.0
    return pl.pallas_call(
        _nsa_main_kernel,
        grid=(B, G, nq),
        in_specs=[pl.BlockSpec((TQ, J * dh), qblk), pl.BlockSpec((None, None, dh, S), bgi),
                  pl.BlockSpec((None, None, S, dh), bgi), pl.BlockSpec((None, None, dh, S), bgi),
                  pl.BlockSpec((None, None, S, dh), bgi),
                  pl.BlockSpec((None, None, TQ, nsb), lambda b, g, i: (b, g, i, 0)),
                  pl.BlockSpec((TQ, J * dh), qblk), pl.BlockSpec((TQ, 128), lambda b, g, i: (b * nq + i, 0)),
                  pl.BlockSpec((None, 3, 128, J * dh), lambda b, g, i: (g, 0, 0, 0))],
        out_specs=pl.BlockSpec((TQ, J * dh), qblk),
        out_shape=jax.ShapeDtypeStruct((T, H * dh), BF16),
        compiler_params=_cparams("parallel", "parallel", "arbitrary"),
    )(roped, ksT, vs, kwT, vw, sel, ocmp, gates, jnp.asarray(gexp))


def _dil_kernel(q_ref, kp_ref, kc_ref, vp_ref, vc_ref, o_ref, lse_ref):
    NQ = q_ref.shape[0]
    prev_from = jnp.where(pl.program_id(2) == 0, NQ, 0)
    qi = lax.broadcasted_iota(I32, (NQ, 1), 0)
    kj = lax.broadcasted_iota(I32, (1, 2 * NQ), 1)
    dist = NQ + qi - kj
    valid = (dist >= 0) & (dist <= NQ) & (kj >= prev_from)
    lane = lax.broadcasted_iota(I32, (NQ, 128), 1)
    lse_all = jnp.zeros((NQ, 128), F32)
    for h in range(DIL_HEADS):
        sl = slice(h * HEAD_DIM, (h + 1) * HEAD_DIM)
        kb = jnp.concatenate([kp_ref[:, sl], kc_ref[:, sl]], axis=0)
        vb = jnp.concatenate([vp_ref[:, sl], vc_ref[:, sl]], axis=0)
        s = jnp.where(valid, _nt(q_ref[:, sl], kb) * HEAD_DIM ** -0.5, NEG_INF)
        m = jnp.max(s, axis=-1, keepdims=True)
        e = jnp.exp(s - m)
        den = jnp.sum(e, axis=-1, keepdims=True)
        o_ref[:, sl] = _mm((e / den).astype(BF16), vb).astype(o_ref.dtype)
        lse_all = jnp.where(lane == h, m + jnp.log(den), lse_all)
    lse_ref[...] = lse_all


def _dil_group(parts, B, S, gi, dil):
    T = B * S
    NQ = DIL_STEPS
    U = S // dil
    nb = U // NQ
    W = D_MODEL
    view = parts.reshape(B, U, dil * 7 * W)
    cur = lambda part: (lambda b, r, n: (b, n, r * 7 + part))
    prev = lambda part: (lambda b, r, n: (b, jnp.maximum(n - 1, 0), r * 7 + part))
    blk = lambda f: pl.BlockSpec((None, NQ, W), f)
    o, lse = pl.pallas_call(
        _dil_kernel,
        grid=(B, dil, nb),
        in_specs=[blk(cur(2 * gi)), blk(prev(2 * gi + 1)), blk(cur(2 * gi + 1)), blk(prev(6)), blk(cur(6))],
        out_specs=[pl.BlockSpec((None, NQ, W), lambda b, r, n: (b, n, r)),
                   pl.BlockSpec((None, NQ, 128), lambda b, r, n: (b, n, r))],
        out_shape=[jax.ShapeDtypeStruct((B, U, dil * W), BF16), jax.ShapeDtypeStruct((B, U, dil * 128), F32)],
        compiler_params=_cparams("parallel", "parallel", "arbitrary"),
    )(view, view, view, view, view)
    return o.reshape(T, W), lse.reshape(T, 128)


def _dil_outproj_ln_kernel(o0_ref, o1_ref, o2_ref, l0_ref, l1_ref, l2_ref, hexp_ref, w_ref, x_ref, g_ref, b_ref, out_ref):
    ls = [l0_ref[...], l1_ref[...], l2_ref[...]]
    m = jnp.maximum(jnp.maximum(ls[0], ls[1]), ls[2])
    es = [jnp.exp(l - m) for l in ls]
    tot = es[0] + es[1] + es[2]
    y = jnp.zeros(o0_ref.shape, F32)
    for e, o_ref in zip(es, (o0_ref, o1_ref, o2_ref)):
        y = y + _mm(e / tot, hexp_ref[...], precision=HIGHEST) * o_ref[...].astype(F32)
    z = ALPHA * x_ref[...] + _mm(y.astype(BF16), w_ref[...])
    out_ref[...] = _ln_rows(z, g_ref[...], b_ref[...])


def _dilated_layer(x, B, S, w_in, w_out, g, b, *, tm=256):
    T, D = x.shape
    n_rope = 2 * len(DIL_GROUPS) * D
    wb = w_in.astype(BF16)
    roped = _proj(x, wb[:, :n_rope], out_dtype=BF16, tm=512, tn=512,
                  rope_tables=_rope_tables(jnp.arange(S), 512), seq=S)
    vals = _proj(x, wb[:, n_rope:], out_dtype=BF16, tm=512, tn=512)
    parts = jnp.concatenate([roped, vals], axis=1)
    outs = [_dil_group(parts, B, S, gi, dil) for gi, (_, dil) in enumerate(DIL_GROUPS)]
    hexp = np.zeros((128, D), np.float32)
    for h in range(DIL_HEADS):
        hexp[h, h * HEAD_DIM:(h + 1) * HEAD_DIM] = 1.0
    row = lambda i: (i, 0)
    fixed = lambda i: (0, 0)
    return pl.pallas_call(
        _dil_outproj_ln_kernel,
        grid=(T // tm,),
        in_specs=[pl.BlockSpec((tm, D), row)] * 3 + [pl.BlockSpec((tm, 128), row)] * 3
        + [pl.BlockSpec((128, D), fixed), pl.BlockSpec((D, D), fixed), pl.BlockSpec((tm, D), row),
           pl.BlockSpec((1, D), fixed), pl.BlockSpec((1, D), fixed)],
        out_specs=pl.BlockSpec((tm, D), row),
        out_shape=jax.ShapeDtypeStruct((T, D), F32),
        compiler_params=_cparams("parallel"),
    )(outs[0][0], outs[1][0], outs[2][0], outs[0][1], outs[1][1], outs[2][1], jnp.asarray(hexp),
      w_out.astype(BF16), x, g.reshape(1, D), b.reshape(1, D))


def _pool_ln_kernel(x_ref, halo_ref, w_ref, scale_ref, g_ref, b_ref, o_ref, ext_ref):
    TS = x_ref.shape[0]
    s = pl.program_id(1)
    x = x_ref[...]
    ext_ref[0:POOL_HALO, :] = jnp.where(s == 0, 0.0, halo_ref[...])
    ext_ref[POOL_HALO:POOL_HALO + TS, :] = x
    cnt = (s * TS + lax.broadcasted_iota(I32, (TS, 1), 0) + 1).astype(F32)
    ys = []
    for gi, w in enumerate(POOL_WINDOWS):
        sl = slice(gi * POOL_GROUP, (gi + 1) * POOL_GROUP)
        xg = x[:, sl]
        tot = xg
        for j in range(1, w):
            tot = tot + ext_ref[POOL_HALO - j:POOL_HALO - j + TS, sl]
        mean = tot / jnp.minimum(cnt, float(w))
        ys.append(_mm((mean - xg).astype(BF16), w_ref[gi]))
    y = jnp.concatenate(ys, axis=1) * scale_ref[...]
    o_ref[...] = _ln_rows(ALPHA * x + y, g_ref[...], b_ref[...])


def _pool_layer(x, B, S, w_grp, scale, g, b, *, ts=512):
    T, D = x.shape
    ns = S // ts
    hb = ts // POOL_HALO
    fixed = lambda bb, s: (0, 0)
    return pl.pallas_call(
        _pool_ln_kernel,
        grid=(B, ns),
        in_specs=[pl.BlockSpec((ts, D), lambda bb, s: (bb * ns + s, 0)),
                  pl.BlockSpec((POOL_HALO, D), lambda bb, s: (jnp.maximum((bb * ns + s) * hb - 1, 0), 0)),
                  pl.BlockSpec((len(POOL_WINDOWS), POOL_GROUP, POOL_GROUP), lambda bb, s: (0, 0, 0)),
                  pl.BlockSpec((1, D), fixed), pl.BlockSpec((1, D), fixed), pl.BlockSpec((1, D), fixed)],
        out_specs=pl.BlockSpec((ts, D), lambda bb, s: (bb * ns + s, 0)),
        out_shape=jax.ShapeDtypeStruct((T, D), F32),
        scratch_shapes=[pltpu.VMEM((ts + POOL_HALO, D), F32)],
        compiler_params=_cparams("parallel", "arbitrary"),
    )(x, x, w_grp.astype(BF16), scale.reshape(1, D), g.reshape(1, D), b.reshape(1, D))


def _router_kernel(x_ref, wT_ref, bias_ref, triu_ref, eidx_ref, gw_ref, rank_ref, cnt_ref, carry_ref):
    E = N_EXPERTS
    per = E // N_EXPERT_GROUPS
    TM = x_ref.shape[0]

    @pl.when(pl.program_id(0) == 0)
    def _():
        carry_ref[...] = jnp.zeros_like(carry_ref)

    scores = _sigmoid(_nt(wT_ref[...], x_ref[...], precision=HIGHEST))
    biased = scores + bias_ref[...]
    eio = lax.broadcasted_iota(I32, (E, TM), 0)

    gio = lax.broadcasted_iota(I32, (per, TM), 0)
    gscore = []
    for gidx in range(N_EXPERT_GROUPS):
        slab = biased[gidx * per:(gidx + 1) * per, :]
        m1 = jnp.max(slab, axis=0, keepdims=True)
        i1 = jnp.min(jnp.where(slab == m1, gio, per), axis=0, keepdims=True)
        m2 = jnp.max(jnp.where(gio == i1, BELOW_NEG_INF, slab), axis=0, keepdims=True)
        gscore.append(m1 + m2)
    slabs = []
    for gidx in range(N_EXPERT_GROUPS):
        beat = jnp.zeros((1, TM), F32)
        for o in range(N_EXPERT_GROUPS):
            if o == gidx:
                continue
            wins = (gscore[o] >= gscore[gidx]) if o < gidx else (gscore[o] > gscore[gidx])
            beat = beat + jnp.where(wins, 1.0, 0.0)
        keep = beat < float(TOPK_GROUPS)
        slabs.append(jnp.where(keep, biased[gidx * per:(gidx + 1) * per, :], NEG_INF))
    cur = jnp.concatenate(slabs, axis=0)

    picked = jnp.zeros((E, TM), F32)
    idxs, vals = [], []
    for _ in range(TOP_K):
        m = jnp.max(cur, axis=0, keepdims=True)
        idx = jnp.min(jnp.where(cur == m, eio, E), axis=0, keepdims=True)
        hit = eio == idx
        picked = jnp.where(hit, 1.0, picked)
        cur = jnp.where(hit, BELOW_NEG_INF, cur)
        idxs.append(idx)
        vals.append(jnp.sum(jnp.where(hit, scores, 0.0), axis=0, keepdims=True))
    total = vals[0]
    for v in vals[1:]:
        total = total + v

    pos = _mm(picked.astype(BF16), triu_ref[...]) + carry_ref[...]
    for k in range(TOP_K):
        eidx_ref[k:k + 1, :] = idxs[k]
        gw_ref[k:k + 1, :] = vals[k] / total * ROUTED_SCALE
        rank_ref[k:k + 1, :] = jnp.sum(jnp.where(eio == idxs[k], pos, 0.0), axis=0, keepdims=True).astype(I32)
    carry_ref[...] = carry_ref[...] + jnp.sum(picked, axis=1, keepdims=True)
    cnt_ref[...] = carry_ref[...].astype(I32)


def _router(x, router_w, router_bias, *, tm=512):
    T, D = x.shape
    E = N_EXPERTS
    triu = jnp.triu(jnp.ones((tm, tm), F32), k=1).astype(BF16)
    col = lambda i: (0, i)
    fixed = lambda i: (0, 0)
    return pl.pallas_call(
        _router_kernel,
        grid=(T // tm,),
        in_specs=[pl.BlockSpec((tm, D), lambda i: (i, 0)), pl.BlockSpec((E, D), fixed),
                  pl.BlockSpec((E, 1), fixed), pl.BlockSpec((tm, tm), fixed)],
        out_specs=[pl.BlockSpec((TOP_K, tm), col), pl.BlockSpec((TOP_K, tm), col), pl.BlockSpec((TOP_K, tm), col),
                   pl.BlockSpec((E, 1), fixed)],
        out_shape=[jax.ShapeDtypeStruct((TOP_K, T), I32), jax.ShapeDtypeStruct((TOP_K, T), F32),
                   jax.ShapeDtypeStruct((TOP_K, T), I32), jax.ShapeDtypeStruct((E, 1), I32)],
        scratch_shapes=[pltpu.VMEM((E, 1), F32)],
        compiler_params=_cparams("arbitrary"),
    )(x, router_w.T.astype(F32), router_bias.reshape(E, 1).astype(F32), triu)


def _row_copy(src_ref, s, dst_ref, d, sem):
    return pltpu.make_async_copy(src_ref.at[pl.ds(s, 1)], dst_ref.at[pl.ds(d, 1)], sem)


def _dispatch_kernel(cnt_ref, pstart_ref, dest_hbm, x_ref, rows_ref, idx_ref, zero_ref, sem_idx, sem_rows, sem_zero):
    TD = x_ref.shape[0]
    i = pl.program_id(0)
    idx_cp = pltpu.make_async_copy(dest_hbm.at[i], idx_ref, sem_idx)
    idx_cp.start()

    @pl.when(i == 0)
    def _():
        zero_ref[...] = jnp.zeros_like(zero_ref)

        def per_expert(e, c):
            n = cnt_ref[e]
            base = pstart_ref[e]
            pad_to = (n + EXP_BLK - 1) // EXP_BLK * EXP_BLK

            def fill(r, c2):
                _row_copy(zero_ref, 0, rows_ref, base + r, sem_zero).start()
                return c2

            def drain(r, c2):
                _row_copy(zero_ref, 0, rows_ref, base + r, sem_zero).wait()
                return c2

            lax.fori_loop(n, pad_to, fill, 0)
            lax.fori_loop(n, pad_to, drain, 0)
            return c

        lax.fori_loop(0, N_EXPERTS, per_expert, 0)

    idx_cp.wait()

    def issue(t, c):
        for k in range(TOP_K):
            _row_copy(x_ref, t, rows_ref, idx_ref[k * TD + t], sem_rows).start()
        return c

    def drain(t, c):
        for k in range(TOP_K):
            _row_copy(x_ref, t, rows_ref, idx_ref[k * TD + t], sem_rows).wait()
        return c

    lax.fori_loop(0, TD, issue, 0)
    lax.fori_loop(0, TD, drain, 0)


def _dispatch(x, dest_tiles, counts, pad_start, n_rows, *, td):
    T, D = x.shape
    return pl.pallas_call(
        _dispatch_kernel,
        grid_spec=pltpu.PrefetchScalarGridSpec(
            num_scalar_prefetch=2,
            grid=(T // td,),
            in_specs=[pl.BlockSpec(memory_space=pl.ANY), pl.BlockSpec((td, D), lambda i, c, p: (i, 0))],
            out_specs=pl.BlockSpec(memory_space=pl.ANY),
            scratch_shapes=[pltpu.SMEM((TOP_K * td,), I32), pltpu.VMEM((8, D), F32),
                            pltpu.SemaphoreType.DMA(()), pltpu.SemaphoreType.DMA(()), pltpu.SemaphoreType.DMA(())],
        ),
        out_shape=jax.ShapeDtypeStruct((n_rows, D), F32),
        compiler_params=_cparams("arbitrary"),
    )(counts, pad_start, dest_tiles, x)


def _expert_kernel(be_ref, nu_ref, x_ref, wgu_ref, wd_ref, o_ref):
    @pl.when(pl.program_id(0) < nu_ref[0])
    def _():
        gu = _mm(x_ref[...].astype(BF16), wgu_ref[0])
        h = _silu(gu[:, :EXPERT_FF]) * gu[:, EXPERT_FF:]
        o_ref[...] = _mm(h.astype(BF16), wd_ref[0])


def _experts(rows, blk_expert, n_used, wgu, wd):
    R, D = rows.shape
    n_blk = R // EXP_BLK
    live = lambda i, be, nu: jnp.minimum(i, nu[0] - 1)
    return pl.pallas_call(
        _expert_kernel,
        grid_spec=pltpu.PrefetchScalarGridSpec(
            num_scalar_prefetch=2,
            grid=(n_blk,),
            in_specs=[pl.BlockSpec((EXP_BLK, D), lambda i, be, nu: (live(i, be, nu), 0)),
                      pl.BlockSpec((1, D, 2 * EXPERT_FF), lambda i, be, nu: (be[live(i, be, nu)], 0, 0)),
                      pl.BlockSpec((1, EXPERT_FF, D), lambda i, be, nu: (be[live(i, be, nu)], 0, 0))],
            out_specs=pl.BlockSpec((EXP_BLK, D), lambda i, be, nu: (live(i, be, nu), 0)),
        ),
        out_shape=jax.ShapeDtypeStruct((R, D), F32),
        compiler_params=_cparams("arbitrary"),
    )(blk_expert, n_used, rows, wgu, wd)


def _combine_kernel(dest_hbm, gw_ref, x_ref, rows_ref, wsgu_ref, wsd_ref, g_ref, b_ref, p_ref, wp_ref, wpg_ref,
                    o_ref, idx_ref, buf_ref, sem_idx, sem_rows):
    TM = x_ref.shape[0]
    i = pl.program_id(0)
    idx_cp = pltpu.make_async_copy(dest_hbm.at[i], idx_ref, sem_idx)
    idx_cp.start()
    idx_cp.wait()

    def gather(t, k):
        return pltpu.make_async_copy(rows_ref.at[pl.ds(idx_ref[k * TM + t], 1)], buf_ref.at[k, pl.ds(t, 1)], sem_rows)

    def issue(t, c):
        for k in range(TOP_K):
            gather(t, k).start()
        return c

    def drain(t, c):
        for k in range(TOP_K):
            gather(t, k).wait()
        return c

    lax.fori_loop(0, TM, issue, 0)
    x = x_ref[...]
    gu = _mm(x.astype(BF16), wsgu_ref[...])
    ff = gu.shape[1] // 2
    f = _mm((_silu(gu[:, :ff]) * gu[:, ff:]).astype(BF16), wsd_ref[...])
    lax.fori_loop(0, TM, drain, 0)
    gw = gw_ref[...]
    for k in range(TOP_K):
        f = f + gw[:, k:k + 1] * buf_ref[k]
    x2 = _ln_rows(ALPHA * x + f, g_ref[...], b_ref[...])
    gate = _sigmoid(_mm(x2.astype(BF16), wpg_ref[...]))
    o_ref[...] = x2 + gate * _mm(p_ref[...].astype(BF16), wp_ref[...])


def _combine(dest_tiles, gw, x, rows_out, wsgu, wsd, g, b, p, wp, wpg, *, tm):
    T, D = x.shape
    row = lambda i: (i, 0)
    fixed = lambda i: (0, 0)
    full = lambda a: pl.BlockSpec(a.shape, fixed)
    return pl.pallas_call(
        _combine_kernel,
        grid=(T // tm,),
        in_specs=[pl.BlockSpec(memory_space=pl.ANY), pl.BlockSpec((tm, TOP_K), row), pl.BlockSpec((tm, D), row),
                  pl.BlockSpec(memory_space=pl.ANY), full(wsgu), full(wsd), pl.BlockSpec((1, D), fixed),
                  pl.BlockSpec((1, D), fixed), pl.BlockSpec((tm, PLE_DIM), row), full(wp), full(wpg)],
        out_specs=pl.BlockSpec((tm, D), row),
        out_shape=jax.ShapeDtypeStruct((T, D), F32),
        scratch_shapes=[pltpu.SMEM((TOP_K * tm,), I32), pltpu.VMEM((TOP_K, tm, D), F32),
                        pltpu.SemaphoreType.DMA(()), pltpu.SemaphoreType.DMA(())],
        compiler_params=_cparams("arbitrary"),
    )(dest_tiles, gw, x, rows_out, wsgu, wsd, g.reshape(1, D), b.reshape(1, D), p, wp, wpg)


MOE_TILE = 256


def _moe_ple_layer(x, p, router_w, router_bias, w_gate, w_up, w_down, ws_gate, ws_up, ws_down, g, b, ple_w, ple_gate_w):
    T, D = x.shape
    eidx, gw, rank, counts = _router(x, router_w, router_bias)
    counts = counts.reshape(N_EXPERTS)
    padded = (counts + EXP_BLK - 1) // EXP_BLK * EXP_BLK
    pad_end = jnp.cumsum(padded)
    pad_start = pad_end - padded
    n_blk = T * TOP_K // EXP_BLK + N_EXPERTS
    dest = pad_start[eidx] + rank
    tm = MOE_TILE
    dest_tiles = dest.reshape(TOP_K, T // tm, tm).transpose(1, 0, 2).reshape(T // tm, TOP_K * tm)
    blk_expert = jnp.minimum(jnp.searchsorted(pad_end, jnp.arange(n_blk, dtype=I32) * EXP_BLK, side='right'),
                             N_EXPERTS - 1).astype(I32)
    n_used = (pad_end[-1:] // EXP_BLK).astype(I32)

    rows = _dispatch(x, dest_tiles, counts.astype(I32), pad_start.astype(I32), n_blk * EXP_BLK, td=tm)
    wgu = jnp.concatenate([w_gate, w_up], axis=-1).astype(BF16)
    rows_out = _experts(rows, blk_expert, n_used, wgu, w_down.astype(BF16))
    wsgu = jnp.concatenate([ws_gate, ws_up], axis=-1).astype(BF16)
    return _combine(dest_tiles, gw.T, x, rows_out, wsgu, ws_down.astype(BF16), g, b, p,
                    ple_w.astype(BF16), ple_gate_w.astype(BF16), tm=tm)


def kernel(x, p, ln_g, ln_b, mlstm_w_in, mlstm_conv, mlstm_ig_bias, mlstm_fg_bias, mlstm_norm_g, mlstm_w_out, nsa_w_in, nsa_cmp_pos_k, nsa_cmp_pos_v, nsa_cmp_wk, nsa_cmp_wv, nsa_w_out, dil_w_in, dil_w_out, pool_w, pool_scale, router_w, router_bias, exp_w_gate, exp_w_up, exp_w_down, sh_w_gate, sh_w_up, sh_w_down, ple_w, ple_gate_w):
    B, S, D = x.shape
    T = B * S
    xf = x.reshape(T, D)
    pf = p.reshape(DEPTH, T, PLE_DIM)
    for i in range(DEPTH):
        kind, j = i % 4, i // 4
        g1, b1 = ln_g[i, 0], ln_b[i, 0]
        if kind == 0:
            y = _mlstm_mixer(xf, B, S, mlstm_w_in[j], mlstm_conv[j], mlstm_ig_bias[j], mlstm_fg_bias[j], mlstm_norm_g[j])
            xf = _outproj_ln(y, mlstm_w_out[j].astype(BF16), xf, g1, b1)
        elif kind == 1:
            y = _nsa_mixer(xf, B, S, nsa_w_in[j], nsa_cmp_pos_k[j], nsa_cmp_pos_v[j], nsa_cmp_wk[j], nsa_cmp_wv[j])
            xf = _outproj_ln(y, nsa_w_out[j].astype(BF16), xf, g1, b1)
        elif kind == 2:
            xf = _dilated_layer(xf, B, S, dil_w_in[j], dil_w_out[j], g1, b1)
        else:
            xf = _pool_layer(xf, B, S, pool_w[j], pool_scale[j], g1, b1)
        xf = _moe_ple_layer(xf, pf[i], router_w[i], router_bias[i], exp_w_gate[i], exp_w_up[i], exp_w_down[i],
                            sh_w_gate[i], sh_w_up[i], sh_w_down[i], ln_g[i, 1], ln_b[i, 1], ple_w[i], ple_gate_w[i])
    return xf.reshape(B, S, D)
```

```python
import functools

import numpy as np
import jax
import jax.numpy as jnp
from jax import lax
from jax.experimental import pallas as pl
from jax.experimental.pallas import tpu as pltpu

F32 = jnp.float32
BF16 = jnp.bfloat16
I32 = jnp.int32
HIGHEST = lax.Precision.HIGHEST

D_MODEL = 1024
DEPTH = 4
ALPHA = (2.0 * DEPTH) ** 0.25
LN_EPS = 1e-5
NEG_INF = -1e30
TINY = 1e-30
BELOW_NEG_INF = -3e38
ROPE_THETA = 10000.0
HEAD_DIM = 64
ROPE_HALF = HEAD_DIM // 2

MLSTM_HEADS = 8
MLSTM_QK_DIM = 64
MLSTM_V_DIM = 128
MLSTM_CONV = 4
MLSTM_L = 256

NSA_HEADS = 16
NSA_KV_HEADS = 4
NSA_J = NSA_HEADS // NSA_KV_HEADS
CMP_STRIDE = 16
CMP_BLK = 32
SEL_BLK = 64
N_SEL = 16
NSA_WINDOW = 512
FORCE_SCORE = 1e4
NSA_TQ = 128
NSA_TK = 512

DIL_HEADS = 16
DIL_GROUPS = ((128, 1), (512, 4), (2048, 16))
DIL_STEPS = 128

POOL_WINDOWS = (2, 4, 8, 16)
POOL_GROUP = 256
POOL_HALO = 16

N_EXPERTS = 64
TOP_K = 8
N_EXPERT_GROUPS = 8
TOPK_GROUPS = 4
EXPERT_FF = 256
ROUTED_SCALE = 2.5
EXP_BLK = 256
PLE_DIM = 256

VMEM_LIMIT = 48 * 1024 * 1024


def _cparams(*sem):
    return pltpu.CompilerParams(dimension_semantics=sem, vmem_limit_bytes=VMEM_LIMIT)


def _nt(a, b, **kw):
    return lax.dot_general(a, b, (((1,), (1,)), ((), ())), preferred_element_type=F32, **kw)


def _tn(a, b, **kw):
    return lax.dot_general(a, b, (((0,), (0,)), ((), ())), preferred_element_type=F32, **kw)


def _mm(a, b, **kw):
    return jnp.dot(a, b, preferred_element_type=F32, **kw)


def _sigmoid(z):
    return 1.0 / (1.0 + jnp.exp(-z))


def _silu(z):
    return z * _sigmoid(z)


def _ln_rows(z, g, b):
    mu = jnp.mean(z, axis=-1, keepdims=True)
    d = z - mu
    var = jnp.mean(d * d, axis=-1, keepdims=True)
    return d * lax.rsqrt(var + LN_EPS) * g + b


def _proj_kernel(*refs, rope):
    if rope:
        x_ref, w_ref, cos_ref, sin_ref, o_ref, xb_ref = refs
    else:
        x_ref, w_ref, o_ref, xb_ref = refs

    @pl.when(pl.program_id(1) == 0)
    def _():
        xb_ref[...] = x_ref[...].astype(BF16)

    acc = _mm(xb_ref[...], w_ref[...])
    if rope:
        tn = acc.shape[1]
        lane = lax.broadcasted_iota(I32, acc.shape, 1)
        lo = (lane % HEAD_DIM) < ROPE_HALF
        rot = jnp.where(lo, pltpu.roll(acc, tn - ROPE_HALF, 1), pltpu.roll(acc, ROPE_HALF, 1))
        acc = acc * cos_ref[...] + rot * sin_ref[...]
    o_ref[...] = acc.astype(o_ref.dtype)


def _proj(x, w, *, out_dtype, tm, tn, rope_tables=None, seq=None):
    T, K = x.shape
    N = w.shape[1]
    assert T % tm == 0 and N % tn == 0
    in_specs = [pl.BlockSpec((tm, K), lambda i, j: (i, 0)), pl.BlockSpec((K, tn), lambda i, j: (0, j))]
    args = [x, w]
    if rope_tables is not None:
        nseq = seq // tm
        in_specs += [pl.BlockSpec((tm, tn), lambda i, j: (i % nseq, 0))] * 2
        args += list(rope_tables)
    return pl.pallas_call(
        functools.partial(_proj_kernel, rope=rope_tables is not None),
        name="proj_rope" if rope_tables is not None else "proj",
        grid=(T // tm, N // tn),
        in_specs=in_specs,
        out_specs=pl.BlockSpec((tm, tn), lambda i, j: (i, j)),
        out_shape=jax.ShapeDtypeStruct((T, N), out_dtype),
        scratch_shapes=[pltpu.VMEM((tm, K), BF16)],
        compiler_params=_cparams("parallel", "arbitrary"),
    )(*args)


def _rope_tables(pos, width):
    inv = ROPE_THETA ** (-jnp.arange(ROPE_HALF, dtype=F32) / ROPE_HALF)
    ang = pos.astype(F32)[:, None] * inv[None, :]
    cos, sin = jnp.cos(ang), jnp.sin(ang)
    cos64 = jnp.concatenate([cos, cos], -1)
    sin64 = jnp.concatenate([-sin, sin], -1)
    rep = width // HEAD_DIM
    return jnp.tile(cos64, (1, rep)), jnp.tile(sin64, (1, rep))


def _outproj_ln_kernel(y_ref, w_ref, x_ref, g_ref, b_ref, o_ref):
    y = _mm(y_ref[...].astype(BF16), w_ref[...])
    o_ref[...] = _ln_rows(ALPHA * x_ref[...] + y, g_ref[...], b_ref[...])


def _outproj_ln(y, w, x, g, b, *, tm=256):
    T, D = x.shape
    K = y.shape[1]
    row = lambda i: (i, 0)
    fixed = lambda i: (0, 0)
    return pl.pallas_call(
        _outproj_ln_kernel,
        name="outproj_ln",
        grid=(T // tm,),
        in_specs=[pl.BlockSpec((tm, K), row), pl.BlockSpec((K, D), fixed), pl.BlockSpec((tm, D), row),
                  pl.BlockSpec((1, D), fixed), pl.BlockSpec((1, D), fixed)],
        out_specs=pl.BlockSpec((tm, D), row),
        out_shape=jax.ShapeDtypeStruct((T, D), F32),
        compiler_params=_cparams("parallel"),
    )(y, w, x, g.reshape(1, D), b.reshape(1, D))


def _log_sigmoid(z):
    return jnp.minimum(z, 0.0) - jnp.log(1.0 + jnp.exp(-jnp.abs(z)))


def _mlstm_kernel(qk_ref, v_ref, o_ref, gc_ref, gr_ref, convw_ref, bias_c_ref, bias_r_ref, ng_ref,
                  tri_ref, triT_ref, out_ref, C_ref, n_ref, m_ref, ext_ref):
    L = qk_ref.shape[0]
    H, dk, dv = MLSTM_HEADS, MLSTM_QK_DIM, MLSTM_V_DIM

    @pl.when(pl.program_id(1) == 0)
    def _():
        C_ref[...] = jnp.zeros_like(C_ref)
        n_ref[...] = jnp.zeros_like(n_ref)
        m_ref[...] = jnp.zeros_like(m_ref)
        ext_ref[0:8, :] = jnp.zeros((8, ext_ref.shape[1]), F32)

    cur = qk_ref[...]
    ext_ref[8:8 + L, :] = cur
    acc = convw_ref[3:4, :] * cur
    for j in range(MLSTM_CONV - 1):
        acc = acc + convw_ref[j:j + 1, :] * ext_ref[5 + j:5 + j + L, :]
    ext_ref[0:8, :] = cur[L - 8:L, :]
    qk = _silu(acc)

    gc = gc_ref[...] + bias_c_ref[...]
    gr = gr_ref[...] + bias_r_ref[...]
    b_col = _mm(tri_ref[...], _log_sigmoid(gc), precision=HIGHEST)
    b_row = _mm(_log_sigmoid(gr[H:2 * H, :]), triT_ref[...], precision=HIGHEST)
    ig_row = gr[0:H, :]
    tri = lax.broadcasted_iota(I32, (L, L), 0) >= lax.broadcasted_iota(I32, (L, L), 1)

    for h in range(H):
        q = (qk[:, h * dk:(h + 1) * dk] * dk ** -0.5).astype(BF16)
        k = qk[:, H * dk + h * dk:H * dk + (h + 1) * dk]
        kb = k.astype(BF16)
        v = v_ref[:, h * dv:(h + 1) * dv].astype(BF16)
        b_c = b_col[:, H + h:H + h + 1]
        ig_c = gc[:, h:h + 1]
        b_r = b_row[h:h + 1, :]
        m_prev = m_ref[h:h + 1, 0:1]
        C = C_ref[h]
        n = n_ref[h:h + 1, :]

        logD = jnp.where(tri, b_c - b_r + ig_row[h:h + 1, :], NEG_INF)
        inter = b_c + m_prev
        m_t = jnp.maximum(inter, jnp.max(logD, axis=-1, keepdims=True))
        s = _nt(q, kb) * jnp.exp(logD - m_t)
        w_inter = jnp.exp(inter - m_t)
        num = w_inter * _mm(q, C.astype(BF16)) + _mm(s.astype(BF16), v)
        den = w_inter * jnp.sum(q.astype(F32) * n, axis=-1, keepdims=True) + jnp.sum(s, axis=-1, keepdims=True)
        hv = num / jnp.maximum(jnp.abs(den), jnp.exp(-m_t))

        mu = jnp.mean(hv, axis=-1, keepdims=True)
        d = hv - mu
        hn = d * lax.rsqrt(jnp.mean(d * d, axis=-1, keepdims=True) + LN_EPS)
        og = _sigmoid(o_ref[:, h * dv:(h + 1) * dv])
        out_ref[:, h * dv:(h + 1) * dv] = (hn * ng_ref[:, h * dv:(h + 1) * dv] * og).astype(out_ref.dtype)

        bL = b_c[L - 1:L, :]
        logw = bL - b_c + ig_c
        m_new = jnp.maximum(bL + m_prev, jnp.max(logw, axis=0, keepdims=True))
        decay = jnp.exp(bL + m_prev - m_new)
        kw = k * jnp.exp(logw - m_new)
        C_ref[h] = decay * C + _tn(kw.astype(BF16), v)
        n_ref[h:h + 1, :] = decay * n + jnp.sum(kw, axis=0, keepdims=True)
        m_ref[h:h + 1, :] = jnp.broadcast_to(m_new, (1, m_ref.shape[1]))


def _mlstm_mixer(x, B, S, w_in, conv_w, ig_bias, fg_bias, norm_g):
    T = B * S
    H, L = MLSTM_HEADS, MLSTM_L
    wb = w_in.astype(BF16)
    main = _proj(x, wb[:, :3 * D_MODEL], out_dtype=F32, tm=512, tn=512)
    wg = jnp.pad(wb[:, 3 * D_MODEL:], ((0, 0), (0, 128 - 2 * H)))
    gates = _proj(x, wg, out_dtype=F32, tm=512, tn=128)
    gates_r = gates[:, :2 * H].reshape(B, S, 2 * H).transpose(0, 2, 1)
    bias16 = jnp.concatenate([ig_bias, fg_bias]).astype(F32)
    bias_c = jnp.pad(bias16, (0, 128 - 2 * H)).reshape(1, 128)
    bias_r = bias16.reshape(2 * H, 1)
    tri = jnp.tril(jnp.ones((L, L), F32))
    nc = S // L
    rowblk = lambda c: (lambda b, i: (b * nc + i, c))
    fixed = lambda b, i: (0, 0)
    return pl.pallas_call(
        _mlstm_kernel,
        name="mlstm",
        grid=(B, nc),
        in_specs=[pl.BlockSpec((L, D_MODEL), rowblk(0)), pl.BlockSpec((L, D_MODEL), rowblk(1)),
                  pl.BlockSpec((L, D_MODEL), rowblk(2)), pl.BlockSpec((L, 128), rowblk(0)),
                  pl.BlockSpec((None, 2 * H, L), lambda b, i: (b, 0, i)),
                  pl.BlockSpec((MLSTM_CONV, D_MODEL), fixed), pl.BlockSpec((1, 128), fixed),
                  pl.BlockSpec((2 * H, 1), fixed), pl.BlockSpec((1, D_MODEL), fixed),
                  pl.BlockSpec((L, L), fixed), pl.BlockSpec((L, L), fixed)],
        out_specs=pl.BlockSpec((L, D_MODEL), rowblk(0)),
        out_shape=jax.ShapeDtypeStruct((T, D_MODEL), BF16),
        scratch_shapes=[pltpu.VMEM((H, MLSTM_QK_DIM, MLSTM_V_DIM), F32), pltpu.VMEM((H, MLSTM_QK_DIM), F32),
                        pltpu.VMEM((H, 128), F32), pltpu.VMEM((L + 8, D_MODEL), F32)],
        compiler_params=_cparams("parallel", "arbitrary"),
    )(main, main, main, gates, gates_r, conv_w.astype(F32), bias_c, bias_r, norm_g.reshape(1, D_MODEL).astype(F32),
      tri, tri.T)


def _rot_cols(w):
    shp = w.shape
    w4 = w.reshape(shp[:-1] + (shp[-1] // HEAD_DIM, 2, ROPE_HALF))
    return jnp.flip(w4, axis=-2).reshape(shp)


def _nsa_compress_kernel(chk_ref, chv_ref, wk_ref, wv_ref, pk_ref, pv_ref, cos_ref, sin_ref, kc_ref, vc_ref, sh_ref):
    NCH = chk_ref.shape[0]
    half = wk_ref.shape[0] // 2
    sh_ref[NCH:NCH + 8, :] = jnp.zeros((8, sh_ref.shape[1]), F32)
    live = lax.broadcasted_iota(I32, (NCH, 1), 0) < NCH - 1

    def blocks(ch_ref, w_ref, p_ref):
        ch = ch_ref[...].astype(BF16)
        a = _mm(ch, w_ref[0:half, :])
        sh_ref[0:NCH, :] = _mm(ch, w_ref[half:2 * half, :])
        const = _mm(p_ref[...].astype(BF16), w_ref[...])[0:1, :]
        return jnp.where(live, a + sh_ref[1:NCH + 1, :] + const, 0.0)

    k2 = blocks(chk_ref, wk_ref, pk_ref)
    kc_ref[...] = (k2[:, :HEAD_DIM] * cos_ref[...] + k2[:, HEAD_DIM:] * sin_ref[...]).astype(kc_ref.dtype)
    v2 = blocks(chv_ref, wv_ref, pv_ref)
    vc_ref[...] = v2[:, :HEAD_DIM].astype(vc_ref.dtype)


def _nsa_cmp_select_kernel(q_ref, kc_ref, vc_ref, ovlT_ref, ocmp_ref, sel_ref):
    TQ = q_ref.shape[0]
    NCH = kc_ref.shape[0]
    NSB = ovlT_ref.shape[0]
    q0 = pl.program_id(2) * TQ
    t = q0 + lax.broadcasted_iota(I32, (TQ, 1), 0)
    cend = lax.broadcasted_iota(I32, (1, NCH), 1) * CMP_STRIDE + (CMP_BLK - 1)
    valid = cend <= t
    kc = kc_ref[...]
    vc = vc_ref[...]
    psum = jnp.zeros((TQ, NCH), F32)
    outs = []
    for j in range(NSA_J):
        qj = q_ref[:, j * HEAD_DIM:(j + 1) * HEAD_DIM]
        s = jnp.where(valid, _nt(qj, kc) * HEAD_DIM ** -0.5, NEG_INF)
        m = jnp.max(s, axis=-1, keepdims=True)
        e = jnp.where(valid, jnp.exp(s - m), 0.0)
        p = e / jnp.maximum(jnp.sum(e, axis=-1, keepdims=True), TINY)
        outs.append(_mm(p.astype(BF16), vc))
        psum = psum + p
    ocmp_ref[...] = jnp.concatenate(outs, axis=1).astype(ocmp_ref.dtype)

    imp = _nt(ovlT_ref[...], psum, precision=HIGHEST)
    nb = lax.broadcasted_iota(I32, (NSB, 1), 0)
    qblk = (q0 + lax.broadcasted_iota(I32, (1, TQ), 1)) // SEL_BLK
    forced = (nb == 0) | (nb == qblk) | (nb == qblk - 1)
    cur = jnp.where(forced, FORCE_SCORE, jnp.where(nb > qblk, NEG_INF, imp))
    sel = jnp.zeros((NSB, TQ), F32)
    for _ in range(min(N_SEL, NSB)):
        m = jnp.max(cur, axis=0, keepdims=True)
        idx = jnp.min(jnp.where(cur == m, nb, NSB), axis=0, keepdims=True)
        hit = nb == idx
        sel = jnp.where(hit, 1.0, sel)
        cur = jnp.where(hit, BELOW_NEG_INF, cur)
    sel_ref[...] = sel.astype(sel_ref.dtype)


def _masked_softmax_pv(s, ok, v):
    s = jnp.where(ok, s, NEG_INF)
    m = jnp.max(s, axis=-1, keepdims=True)
    e = jnp.where(ok, jnp.exp(s - m), 0.0)
    p = e / jnp.maximum(jnp.sum(e, axis=-1, keepdims=True), TINY)
    return _mm(p.astype(BF16), v)


def _nsa_main_kernel(q_ref, ksT_ref, vs_ref, kwT_ref, vw_ref, sel_ref, ocmp_ref, gate_ref, gexp_ref, o_ref):
    TQ = q_ref.shape[0]
    S = vs_ref.shape[0]
    TK = min(NSA_TK, S)
    J = NSA_J
    scale = HEAD_DIM ** -0.5
    q0 = pl.program_id(2) * TQ
    q4 = jnp.concatenate([q_ref[:, j * HEAD_DIM:(j + 1) * HEAD_DIM] for j in range(J)], axis=0)
    t = q0 + lax.broadcasted_iota(I32, (TQ, 1), 0)
    selT = sel_ref[...]
    rep = lambda a: jnp.concatenate([a] * J, axis=0)

    def body(c, carry):
        m, l, acc = carry
        k0 = pl.multiple_of(c * TK, TK)
        kT = ksT_ref[:, pl.ds(k0, TK)]
        v = vs_ref[pl.ds(k0, TK), :]
        kpos = k0 + lax.broadcasted_iota(I32, (1, TK), 1)
        expand = jnp.where(kpos // SEL_BLK == lax.broadcasted_iota(I32, (selT.shape[0], 1), 0), 1.0, 0.0).astype(BF16)
        okf = jnp.where((_tn(selT, expand) > 0.5) & (kpos <= t), 1.0, 0.0)
        ok = rep(okf) > 0.5
        s = jnp.where(ok, _mm(q4, kT) * scale, NEG_INF)
        m_new = jnp.maximum(m, jnp.max(s, axis=-1, keepdims=True))
        e = jnp.where(ok, jnp.exp(s - m_new), 0.0)
        a = jnp.exp(m - m_new)
        return m_new, a * l + jnp.sum(e, axis=-1, keepdims=True), a * acc + _mm(e.astype(BF16), v)

    nchunks = (q0 + TQ - 1) // TK + 1
    init = (jnp.full((J * TQ, 1), NEG_INF, F32), jnp.zeros((J * TQ, 1), F32), jnp.zeros((J * TQ, HEAD_DIM), F32))
    _, l, acc = lax.fori_loop(0, nchunks, body, init)
    o_slc = acc / jnp.maximum(l, TINY)

    span = min(NSA_WINDOW + TQ, S)
    start = pl.multiple_of(jnp.clip(q0 - NSA_WINDOW, 0, S - span), TQ)
    kpos = start + lax.broadcasted_iota(I32, (1, span), 1)
    dist = t - kpos
    okw = rep(jnp.where((dist >= 0) & (dist < NSA_WINDOW), 1.0, 0.0)) > 0.5
    o_win = _masked_softmax_pv(_mm(q4, kwT_ref[:, pl.ds(start, span)]) * scale, okw, vw_ref[pl.ds(start, span), :])

    unstack = lambda o: jnp.concatenate([o[j * TQ:(j + 1) * TQ, :] for j in range(J)], axis=1)
    g = _sigmoid(gate_ref[...])
    gx = [_mm(g, gexp_ref[c], precision=HIGHEST) for c in range(3)]
    out = gx[0] * ocmp_ref[...].astype(F32) + gx[1] * unstack(o_slc) + gx[2] * unstack(o_win)
    o_ref[...] = out.astype(o_ref.dtype)


def _nsa_mixer(x, B, S, w_in, cmp_pos_k, cmp_pos_v, cmp_wk, cmp_wv):
    T = B * S
    H, G, J, dh = NSA_HEADS, NSA_KV_HEADS, NSA_J, HEAD_DIM
    kv = G * dh
    wb = w_in.astype(BF16)
    col = lambda a, n: wb[:, a:a + n]
    o_q, o_kc, o_vc, o_ks, o_vs, o_kw, o_vw, o_g = np.cumsum([0, H * dh] + [kv] * 6).tolist()
    pos = jnp.arange(S)
    w_rope = jnp.concatenate([col(o_q, H * dh), col(o_ks, kv), col(o_kw, kv)], axis=1)
    roped = _proj(x, w_rope, out_dtype=BF16, tm=512, tn=512, rope_tables=_rope_tables(pos, 512), seq=S)
    w_plain = jnp.concatenate([col(o_kc, kv), col(o_vc, kv), col(o_vs, kv), col(o_vw, kv),
                               jnp.pad(col(o_g, 3 * H), ((0, 0), (0, 128 - 3 * H)))], axis=1)
    plain = _proj(x, w_plain, out_dtype=F32, tm=512, tn=128)

    heads_T = lambda a: a.reshape(B, S, G, dh).transpose(0, 2, 3, 1)
    heads = lambda a: a.reshape(B, S, G, dh).transpose(0, 2, 1, 3)
    ksT = heads_T(roped[:, H * dh:H * dh + kv])
    kwT = heads_T(roped[:, H * dh + kv:])
    vs = heads(plain[:, 2 * kv:3 * kv]).astype(BF16)
    vw = heads(plain[:, 3 * kv:4 * kv]).astype(BF16)
    gates = plain[:, 4 * kv:]

    nch = S // CMP_STRIDE
    chunks = lambda a: a.reshape(B, nch, CMP_STRIDE, G, dh).transpose(0, 3, 1, 2, 4).reshape(B, G, nch, CMP_STRIDE * dh)
    chk, chv = chunks(plain[:, :kv]), chunks(plain[:, kv:2 * kv])
    wk = cmp_wk.reshape(CMP_BLK * dh, dh)
    wk2 = jnp.concatenate([wk, _rot_cols(wk)], axis=1).astype(BF16)
    wv2 = jnp.pad(cmp_wv.reshape(CMP_BLK * dh, dh), ((0, 0), (0, dh))).astype(BF16)
    flat8 = lambda p_: jnp.pad(p_.reshape(1, CMP_BLK * dh), ((0, 7), (0, 0))).astype(F32)
    cend = jnp.arange(nch) * CMP_STRIDE + CMP_BLK - 1
    cos_c, sin_c = _rope_tables(cend, dh)
    bg = lambda b, g: (b, g, 0, 0)
    fixed2 = lambda b, g: (0, 0)
    kc, vc = pl.pallas_call(
        _nsa_compress_kernel,
        name="nsa_compress",
        grid=(B, G),
        in_specs=[pl.BlockSpec((None, None, nch, CMP_STRIDE * dh), bg)] * 2
        + [pl.BlockSpec((CMP_BLK * dh, 2 * dh), fixed2)] * 2 + [pl.BlockSpec((8, CMP_BLK * dh), fixed2)] * 2
        + [pl.BlockSpec((nch, dh), fixed2)] * 2,
        out_specs=[pl.BlockSpec((None, None, nch, dh), bg)] * 2,
        out_shape=[jax.ShapeDtypeStruct((B, G, nch, dh), BF16)] * 2,
        scratch_shapes=[pltpu.VMEM((nch + 8, 2 * dh), F32)],
        compiler_params=_cparams("parallel", "parallel"),
    )(chk, chv, wk2, wv2, flat8(cmp_pos_k), flat8(cmp_pos_v), cos_c, sin_c)

    nsb = S // SEL_BLK
    c_idx, s_idx = np.arange(nch), np.arange(nsb)
    ovl = ((c_idx[:, None] * CMP_STRIDE + CMP_BLK - 1 >= s_idx[None, :] * SEL_BLK)
           & (c_idx[:, None] * CMP_STRIDE < (s_idx[None, :] + 1) * SEL_BLK)).astype(np.float32)
    TQ = NSA_TQ
    nq = S // TQ
    qblk = lambda b, g, i: (b * nq + i, g)
    bgi = lambda b, g, i: (b, g, 0, 0)
    ocmp, sel = pl.pallas_call(
        _nsa_cmp_select_kernel,
        name="nsa_cmp_select",
        grid=(B, G, nq),
        in_specs=[pl.BlockSpec((TQ, J * dh), qblk), pl.BlockSpec((None, None, nch, dh), bgi),
                  pl.BlockSpec((None, None, nch, dh), bgi), pl.BlockSpec((nsb, nch), lambda b, g, i: (0, 0))],
        out_specs=[pl.BlockSpec((TQ, J * dh), qblk), pl.BlockSpec((None, None, nsb, TQ), lambda b, g, i: (b, g, 0, i))],
        out_shape=[jax.ShapeDtypeStruct((T, H * dh), BF16), jax.ShapeDtypeStruct((B, G, nsb, S), BF16)],
        compiler_params=_cparams("parallel", "parallel", "parallel"),
    )(roped, kc, vc, jnp.asarray(ovl.T))

    gexp = np.zeros((G, 3, 128, J * dh), np.float32)
    for g in range(G):
        for c in range(3):
            for j in range(J):
                gexp[g, c, (g * J + j) * 3 + c, j * dh:(j + 1) * dh] = 1.0
    return pl.pallas_call(
        _nsa_main_kernel,
        name="nsa_main",
        grid=(B, G, nq),
        in_specs=[pl.BlockSpec((TQ, J * dh), qblk), pl.BlockSpec((None, None, dh, S), bgi),
                  pl.BlockSpec((None, None, S, dh), bgi), pl.BlockSpec((None, None, dh, S), bgi),
                  pl.BlockSpec((None, None, S, dh), bgi),
                  pl.BlockSpec((None, None, nsb, TQ), lambda b, g, i: (b, g, 0, i)),
                  pl.BlockSpec((TQ, J * dh), qblk), pl.BlockSpec((TQ, 128), lambda b, g, i: (b * nq + i, 0)),
                  pl.BlockSpec((None, 3, 128, J * dh), lambda b, g, i: (g, 0, 0, 0))],
        out_specs=pl.BlockSpec((TQ, J * dh), qblk),
        out_shape=jax.ShapeDtypeStruct((T, H * dh), BF16),
        compiler_params=_cparams("parallel", "parallel", "arbitrary"),
    )(roped, ksT, vs, kwT, vw, sel, ocmp, gates, jnp.asarray(gexp))


def _dil_kernel(q_ref, kp_ref, kc_ref, vp_ref, vc_ref, o_ref, lse_ref):
    NQ = q_ref.shape[0]
    prev_from = jnp.where(pl.program_id(2) == 0, NQ, 0)
    qi = lax.broadcasted_iota(I32, (NQ, 1), 0)
    kj = lax.broadcasted_iota(I32, (1, 2 * NQ), 1)
    dist = NQ + qi - kj
    valid = (dist >= 0) & (dist <= NQ) & (kj >= prev_from)
    lane = lax.broadcasted_iota(I32, (NQ, 128), 1)
    lse_all = jnp.zeros((NQ, 128), F32)
    for h in range(DIL_HEADS):
        sl = slice(h * HEAD_DIM, (h + 1) * HEAD_DIM)
        kb = jnp.concatenate([kp_ref[:, sl], kc_ref[:, sl]], axis=0)
        vb = jnp.concatenate([vp_ref[:, sl], vc_ref[:, sl]], axis=0)
        s = jnp.where(valid, _nt(q_ref[:, sl], kb) * HEAD_DIM ** -0.5, NEG_INF)
        m = jnp.max(s, axis=-1, keepdims=True)
        e = jnp.exp(s - m)
        den = jnp.sum(e, axis=-1, keepdims=True)
        o_ref[:, sl] = _mm((e / den).astype(BF16), vb).astype(o_ref.dtype)
        lse_all = jnp.where(lane == h, m + jnp.log(den), lse_all)
    lse_ref[...] = lse_all


def _dil_group(parts, B, S, gi, dil):
    T = B * S
    NQ = DIL_STEPS
    U = S // dil
    nb = U // NQ
    W = D_MODEL
    view = parts.reshape(B, U, dil * 7 * W)
    cur = lambda part: (lambda b, r, n: (b, n, r * 7 + part))
    prev = lambda part: (lambda b, r, n: (b, jnp.maximum(n - 1, 0), r * 7 + part))
    blk = lambda f: pl.BlockSpec((None, NQ, W), f)
    o, lse = pl.pallas_call(
        _dil_kernel,
        name=f"dilated_attn_{dil}",
        grid=(B, dil, nb),
        in_specs=[blk(cur(2 * gi)), blk(prev(2 * gi + 1)), blk(cur(2 * gi + 1)), blk(prev(6)), blk(cur(6))],
        out_specs=[pl.BlockSpec((None, NQ, W), lambda b, r, n: (b, n, r)),
                   pl.BlockSpec((None, NQ, 128), lambda b, r, n: (b, n, r))],
        out_shape=[jax.ShapeDtypeStruct((B, U, dil * W), BF16), jax.ShapeDtypeStruct((B, U, dil * 128), F32)],
        compiler_params=_cparams("parallel", "parallel", "arbitrary"),
    )(view, view, view, view, view)
    return o.reshape(T, W), lse.reshape(T, 128)


def _dil_outproj_ln_kernel(o0_ref, o1_ref, o2_ref, l0_ref, l1_ref, l2_ref, hexp_ref, w_ref, x_ref, g_ref, b_ref, out_ref):
    ls = [l0_ref[...], l1_ref[...], l2_ref[...]]
    m = jnp.maximum(jnp.maximum(ls[0], ls[1]), ls[2])
    es = [jnp.exp(l - m) for l in ls]
    tot = es[0] + es[1] + es[2]
    y = jnp.zeros(o0_ref.shape, F32)
    for e, o_ref in zip(es, (o0_ref, o1_ref, o2_ref)):
        y = y + _mm(e / tot, hexp_ref[...], precision=HIGHEST) * o_ref[...].astype(F32)
    z = ALPHA * x_ref[...] + _mm(y.astype(BF16), w_ref[...])
    out_ref[...] = _ln_rows(z, g_ref[...], b_ref[...])


def _dilated_layer(x, B, S, w_in, w_out, g, b, *, tm=256):
    T, D = x.shape
    n_rope = 2 * len(DIL_GROUPS) * D
    wb = w_in.astype(BF16)
    roped = _proj(x, wb[:, :n_rope], out_dtype=BF16, tm=512, tn=512,
                  rope_tables=_rope_tables(jnp.arange(S), 512), seq=S)
    vals = _proj(x, wb[:, n_rope:], out_dtype=BF16, tm=512, tn=512)
    parts = jnp.concatenate([roped, vals], axis=1)
    outs = [_dil_group(parts, B, S, gi, dil) for gi, (_, dil) in enumerate(DIL_GROUPS)]
    hexp = np.zeros((128, D), np.float32)
    for h in range(DIL_HEADS):
        hexp[h, h * HEAD_DIM:(h + 1) * HEAD_DIM] = 1.0
    row = lambda i: (i, 0)
    fixed = lambda i: (0, 0)
    return pl.pallas_call(
        _dil_outproj_ln_kernel,
        name="dilated_outproj_ln",
        grid=(T // tm,),
        in_specs=[pl.BlockSpec((tm, D), row)] * 3 + [pl.BlockSpec((tm, 128), row)] * 3
        + [pl.BlockSpec((128, D), fixed), pl.BlockSpec((D, D), fixed), pl.BlockSpec((tm, D), row),
           pl.BlockSpec((1, D), fixed), pl.BlockSpec((1, D), fixed)],
        out_specs=pl.BlockSpec((tm, D), row),
        out_shape=jax.ShapeDtypeStruct((T, D), F32),
        compiler_params=_cparams("parallel"),
    )(outs[0][0], outs[1][0], outs[2][0], outs[0][1], outs[1][1], outs[2][1], jnp.asarray(hexp),
      w_out.astype(BF16), x, g.reshape(1, D), b.reshape(1, D))


def _pool_ln_kernel(x_ref, halo_ref, w_ref, scale_ref, g_ref, b_ref, o_ref, ext_ref):
    TS = x_ref.shape[0]
    s = pl.program_id(1)
    x = x_ref[...]
    ext_ref[0:POOL_HALO, :] = jnp.where(s == 0, 0.0, halo_ref[...])
    ext_ref[POOL_HALO:POOL_HALO + TS, :] = x
    cnt = (s * TS + lax.broadcasted_iota(I32, (TS, 1), 0) + 1).astype(F32)
    ys = []
    for gi, w in enumerate(POOL_WINDOWS):
        sl = slice(gi * POOL_GROUP, (gi + 1) * POOL_GROUP)
        xg = x[:, sl]
        tot = xg
        for j in range(1, w):
            tot = tot + ext_ref[POOL_HALO - j:POOL_HALO - j + TS, sl]
        mean = tot / jnp.minimum(cnt, float(w))
        ys.append(_mm((mean - xg).astype(BF16), w_ref[gi]))
    y = jnp.concatenate(ys, axis=1) * scale_ref[...]
    o_ref[...] = _ln_rows(ALPHA * x + y, g_ref[...], b_ref[...])


def _pool_layer(x, B, S, w_grp, scale, g, b, *, ts=512):
    T, D = x.shape
    ns = S // ts
    hb = ts // POOL_HALO
    fixed = lambda bb, s: (0, 0)
    return pl.pallas_call(
        _pool_ln_kernel,
        name="pool_ln",
        grid=(B, ns),
        in_specs=[pl.BlockSpec((ts, D), lambda bb, s: (bb * ns + s, 0)),
                  pl.BlockSpec((POOL_HALO, D), lambda bb, s: (jnp.maximum((bb * ns + s) * hb - 1, 0), 0)),
                  pl.BlockSpec((len(POOL_WINDOWS), POOL_GROUP, POOL_GROUP), lambda bb, s: (0, 0, 0)),
                  pl.BlockSpec((1, D), fixed), pl.BlockSpec((1, D), fixed), pl.BlockSpec((1, D), fixed)],
        out_specs=pl.BlockSpec((ts, D), lambda bb, s: (bb * ns + s, 0)),
        out_shape=jax.ShapeDtypeStruct((T, D), F32),
        scratch_shapes=[pltpu.VMEM((ts + POOL_HALO, D), F32)],
        compiler_params=_cparams("parallel", "arbitrary"),
    )(x, x, w_grp.astype(BF16), scale.reshape(1, D), g.reshape(1, D), b.reshape(1, D))


def _router_kernel(x_ref, wT_ref, bias_ref, triu_ref, eidx_ref, gw_ref, rank_ref, cnt_ref, carry_ref):
    E = N_EXPERTS
    per = E // N_EXPERT_GROUPS
    TM = x_ref.shape[0]

    @pl.when(pl.program_id(0) == 0)
    def _():
        carry_ref[...] = jnp.zeros_like(carry_ref)

    scores = _sigmoid(_nt(wT_ref[...], x_ref[...], precision=HIGHEST))
    biased = scores + bias_ref[...]
    eio = lax.broadcasted_iota(I32, (E, TM), 0)

    gio = lax.broadcasted_iota(I32, (per, TM), 0)
    gscore = []
    for gidx in range(N_EXPERT_GROUPS):
        slab = biased[gidx * per:(gidx + 1) * per, :]
        m1 = jnp.max(slab, axis=0, keepdims=True)
        i1 = jnp.min(jnp.where(slab == m1, gio, per), axis=0, keepdims=True)
        m2 = jnp.max(jnp.where(gio == i1, BELOW_NEG_INF, slab), axis=0, keepdims=True)
        gscore.append(m1 + m2)
    slabs = []
    for gidx in range(N_EXPERT_GROUPS):
        beat = jnp.zeros((1, TM), F32)
        for o in range(N_EXPERT_GROUPS):
            if o == gidx:
                continue
            wins = (gscore[o] >= gscore[gidx]) if o < gidx else (gscore[o] > gscore[gidx])
            beat = beat + jnp.where(wins, 1.0, 0.0)
        keep = beat < float(TOPK_GROUPS)
        slabs.append(jnp.where(keep, biased[gidx * per:(gidx + 1) * per, :], NEG_INF))
    cur = jnp.concatenate(slabs, axis=0)

    picked = jnp.zeros((E, TM), F32)
    idxs, vals = [], []
    for _ in range(TOP_K):
        m = jnp.max(cur, axis=0, keepdims=True)
        idx = jnp.min(jnp.where(cur == m, eio, E), axis=0, keepdims=True)
        hit = eio == idx
        picked = jnp.where(hit, 1.0, picked)
        cur = jnp.where(hit, BELOW_NEG_INF, cur)
        idxs.append(idx)
        vals.append(jnp.sum(jnp.where(hit, scores, 0.0), axis=0, keepdims=True))
    total = vals[0]
    for v in vals[1:]:
        total = total + v

    pos = _mm(picked.astype(BF16), triu_ref[...]) + carry_ref[...]
    for k in range(TOP_K):
        eidx_ref[k:k + 1, :] = idxs[k]
        gw_ref[k:k + 1, :] = vals[k] / total * ROUTED_SCALE
        rank_ref[k:k + 1, :] = jnp.sum(jnp.where(eio == idxs[k], pos, 0.0), axis=0, keepdims=True).astype(I32)
    carry_ref[...] = carry_ref[...] + jnp.sum(picked, axis=1, keepdims=True)
    cnt_ref[...] = carry_ref[...].astype(I32)


def _router(x, router_w, router_bias, *, tm=512):
    T, D = x.shape
    E = N_EXPERTS
    triu = jnp.triu(jnp.ones((tm, tm), F32), k=1).astype(BF16)
    col = lambda i: (0, i)
    fixed = lambda i: (0, 0)
    return pl.pallas_call(
        _router_kernel,
        name="moe_router",
        grid=(T // tm,),
        in_specs=[pl.BlockSpec((tm, D), lambda i: (i, 0)), pl.BlockSpec((E, D), fixed),
                  pl.BlockSpec((E, 1), fixed), pl.BlockSpec((tm, tm), fixed)],
        out_specs=[pl.BlockSpec((TOP_K, tm), col), pl.BlockSpec((TOP_K, tm), col), pl.BlockSpec((TOP_K, tm), col),
                   pl.BlockSpec((E, 1), fixed)],
        out_shape=[jax.ShapeDtypeStruct((TOP_K, T), I32), jax.ShapeDtypeStruct((TOP_K, T), F32),
                   jax.ShapeDtypeStruct((TOP_K, T), I32), jax.ShapeDtypeStruct((E, 1), I32)],
        scratch_shapes=[pltpu.VMEM((E, 1), F32)],
        compiler_params=_cparams("arbitrary"),
    )(x, router_w.T.astype(F32), router_bias.reshape(E, 1).astype(F32), triu)


def _row_copy(src_ref, s, dst_ref, d, sem):
    return pltpu.make_async_copy(src_ref.at[pl.ds(s, 1)], dst_ref.at[pl.ds(d, 1)], sem)


def _dispatch_kernel(cnt_ref, pstart_ref, dest_hbm, x_ref, rows_ref, idx_ref, zero_ref, sem_idx, sem_rows, sem_zero):
    TD = x_ref.shape[0]
    i = pl.program_id(0)
    idx_cp = pltpu.make_async_copy(dest_hbm.at[i], idx_ref, sem_idx)
    idx_cp.start()

    @pl.when(i == 0)
    def _():
        zero_ref[...] = jnp.zeros_like(zero_ref)

        def per_expert(e, c):
            n = cnt_ref[e]
            base = pstart_ref[e]
            pad_to = (n + EXP_BLK - 1) // EXP_BLK * EXP_BLK

            def fill(r, c2):
                _row_copy(zero_ref, 0, rows_ref, base + r, sem_zero).start()
                return c2

            def drain(r, c2):
                _row_copy(zero_ref, 0, rows_ref, base + r, sem_zero).wait()
                return c2

            lax.fori_loop(n, pad_to, fill, 0)
            lax.fori_loop(n, pad_to, drain, 0)
            return c

        lax.fori_loop(0, N_EXPERTS, per_expert, 0)

    idx_cp.wait()

    def issue(t, c):
        for k in range(TOP_K):
            _row_copy(x_ref, t, rows_ref, idx_ref[k * TD + t], sem_rows).start()
        return c

    def drain(t, c):
        for k in range(TOP_K):
            _row_copy(x_ref, t, rows_ref, idx_ref[k * TD + t], sem_rows).wait()
        return c

    lax.fori_loop(0, TD, issue, 0)
    lax.fori_loop(0, TD, drain, 0)


def _dispatch(x, dest_tiles, counts, pad_start, n_rows, *, td):
    T, D = x.shape
    return pl.pallas_call(
        _dispatch_kernel,
        name="moe_dispatch",
        grid_spec=pltpu.PrefetchScalarGridSpec(
            num_scalar_prefetch=2,
            grid=(T // td,),
            in_specs=[pl.BlockSpec(memory_space=pl.ANY), pl.BlockSpec((td, D), lambda i, c, p: (i, 0))],
            out_specs=pl.BlockSpec(memory_space=pl.ANY),
            scratch_shapes=[pltpu.SMEM((TOP_K * td,), I32), pltpu.VMEM((8, D), F32),
                            pltpu.SemaphoreType.DMA(()), pltpu.SemaphoreType.DMA(()), pltpu.SemaphoreType.DMA(())],
        ),
        out_shape=jax.ShapeDtypeStruct((n_rows, D), F32),
        compiler_params=_cparams("arbitrary"),
    )(counts, pad_start, dest_tiles, x)


def _expert_kernel(be_ref, nu_ref, x_ref, wgu_ref, wd_ref, o_ref):
    @pl.when(pl.program_id(0) < nu_ref[0])
    def _():
        gu = _mm(x_ref[...].astype(BF16), wgu_ref[0])
        h = _silu(gu[:, :EXPERT_FF]) * gu[:, EXPERT_FF:]
        o_ref[...] = _mm(h.astype(BF16), wd_ref[0])


def _experts(rows, blk_expert, n_used, wgu, wd):
    R, D = rows.shape
    n_blk = R // EXP_BLK
    live = lambda i, be, nu: jnp.minimum(i, nu[0] - 1)
    return pl.pallas_call(
        _expert_kernel,
        name="moe_experts",
        grid_spec=pltpu.PrefetchScalarGridSpec(
            num_scalar_prefetch=2,
            grid=(n_blk,),
            in_specs=[pl.BlockSpec((EXP_BLK, D), lambda i, be, nu: (live(i, be, nu), 0)),
                      pl.BlockSpec((1, D, 2 * EXPERT_FF), lambda i, be, nu: (be[live(i, be, nu)], 0, 0)),
                      pl.BlockSpec((1, EXPERT_FF, D), lambda i, be, nu: (be[live(i, be, nu)], 0, 0))],
            out_specs=pl.BlockSpec((EXP_BLK, D), lambda i, be, nu: (live(i, be, nu), 0)),
        ),
        out_shape=jax.ShapeDtypeStruct((R, D), F32),
        compiler_params=_cparams("arbitrary"),
    )(blk_expert, n_used, rows, wgu, wd)


def _combine_kernel(dest_hbm, gw_ref, x_ref, rows_ref, wsgu_ref, wsd_ref, g_ref, b_ref, p_ref, wp_ref, wpg_ref,
                    o_ref, idx_ref, buf_ref, sem_idx, sem_rows):
    TM = x_ref.shape[0]
    i = pl.program_id(0)
    idx_cp = pltpu.make_async_copy(dest_hbm.at[i], idx_ref, sem_idx)
    idx_cp.start()
    idx_cp.wait()

    def gather(t, k):
        return pltpu.make_async_copy(rows_ref.at[pl.ds(idx_ref[k * TM + t], 1)], buf_ref.at[k, pl.ds(t, 1)], sem_rows)

    def issue(t, c):
        for k in range(TOP_K):
            gather(t, k).start()
        return c

    def drain(t, c):
        for k in range(TOP_K):
            gather(t, k).wait()
        return c

    lax.fori_loop(0, TM, issue, 0)
    x = x_ref[...]
    gu = _mm(x.astype(BF16), wsgu_ref[...])
    ff = gu.shape[1] // 2
    f = _mm((_silu(gu[:, :ff]) * gu[:, ff:]).astype(BF16), wsd_ref[...])
    lax.fori_loop(0, TM, drain, 0)
    gw = gw_ref[...]
    for k in range(TOP_K):
        f = f + gw[:, k:k + 1] * buf_ref[k]
    x2 = _ln_rows(ALPHA * x + f, g_ref[...], b_ref[...])
    gate = _sigmoid(_mm(x2.astype(BF16), wpg_ref[...]))
    o_ref[...] = x2 + gate * _mm(p_ref[...].astype(BF16), wp_ref[...])


def _combine(dest_tiles, gw, x, rows_out, wsgu, wsd, g, b, p, wp, wpg, *, tm):
    T, D = x.shape
    row = lambda i: (i, 0)
    fixed = lambda i: (0, 0)
    full = lambda a: pl.BlockSpec(a.shape, fixed)
    return pl.pallas_call(
        _combine_kernel,
        name="moe_combine",
        grid=(T // tm,),
        in_specs=[pl.BlockSpec(memory_space=pl.ANY), pl.BlockSpec((tm, TOP_K), row), pl.BlockSpec((tm, D), row),
                  pl.BlockSpec(memory_space=pl.ANY), full(wsgu), full(wsd), pl.BlockSpec((1, D), fixed),
                  pl.BlockSpec((1, D), fixed), pl.BlockSpec((tm, PLE_DIM), row), full(wp), full(wpg)],
        out_specs=pl.BlockSpec((tm, D), row),
        out_shape=jax.ShapeDtypeStruct((T, D), F32),
        scratch_shapes=[pltpu.SMEM((TOP_K * tm,), I32), pltpu.VMEM((TOP_K, tm, D), F32),
                        pltpu.SemaphoreType.DMA(()), pltpu.SemaphoreType.DMA(())],
        compiler_params=_cparams("arbitrary"),
    )(dest_tiles, gw, x, rows_out, wsgu, wsd, g.reshape(1, D), b.reshape(1, D), p, wp, wpg)


MOE_TILE = 256


def _moe_ple_layer(x, p, router_w, router_bias, w_gate, w_up, w_down, ws_gate, ws_up, ws_down, g, b, ple_w, ple_gate_w):
    T, D = x.shape
    eidx, gw, rank, counts = _router(x, router_w, router_bias)
    counts = counts.reshape(N_EXPERTS)
    padded = (counts + EXP_BLK - 1) // EXP_BLK * EXP_BLK
    pad_end = jnp.cumsum(padded)
    pad_start = pad_end - padded
    n_blk = T * TOP_K // EXP_BLK + N_EXPERTS
    e_iota = jnp.arange(N_EXPERTS, dtype=I32)
    dest = rank + jnp.sum(jnp.where(eidx[..., None] == e_iota, pad_start, 0), axis=-1)
    tm = MOE_TILE
    dest_tiles = dest.reshape(TOP_K, T // tm, tm).transpose(1, 0, 2).reshape(T // tm, TOP_K * tm)
    blk_first = jnp.arange(n_blk, dtype=I32)[:, None] * EXP_BLK
    blk_expert = jnp.minimum(jnp.sum((pad_end[None, :] <= blk_first).astype(I32), axis=-1), N_EXPERTS - 1)
    n_used = (pad_end[-1:] // EXP_BLK).astype(I32)

    rows = _dispatch(x, dest_tiles, counts.astype(I32), pad_start.astype(I32), n_blk * EXP_BLK, td=tm)
    wgu = jnp.concatenate([w_gate, w_up], axis=-1).astype(BF16)
    rows_out = _experts(rows, blk_expert, n_used, wgu, w_down.astype(BF16))
    wsgu = jnp.concatenate([ws_gate, ws_up], axis=-1).astype(BF16)
    return _combine(dest_tiles, gw.T, x, rows_out, wsgu, ws_down.astype(BF16), g, b, p,
                    ple_w.astype(BF16), ple_gate_w.astype(BF16), tm=tm)


def kernel(x, p, ln_g, ln_b, mlstm_w_in, mlstm_conv, mlstm_ig_bias, mlstm_fg_bias, mlstm_norm_g, mlstm_w_out, nsa_w_in, nsa_cmp_pos_k, nsa_cmp_pos_v, nsa_cmp_wk, nsa_cmp_wv, nsa_w_out, dil_w_in, dil_w_out, pool_w, pool_scale, router_w, router_bias, exp_w_gate, exp_w_up, exp_w_down, sh_w_gate, sh_w_up, sh_w_down, ple_w, ple_gate_w):
    B, S, D = x.shape
    T = B * S
    xf = x.reshape(T, D)
    pf = p.reshape(DEPTH, T, PLE_DIM)
    for i in range(DEPTH):
        kind, j = i % 4, i // 4
        g1, b1 = ln_g[i, 0], ln_b[i, 0]
        if kind == 0:
            y = _mlstm_mixer(xf, B, S, mlstm_w_in[j], mlstm_conv[j], mlstm_ig_bias[j], mlstm_fg_bias[j], mlstm_norm_g[j])
            xf = _outproj_ln(y, mlstm_w_out[j].astype(BF16), xf, g1, b1)
        elif kind == 1:
            y = _nsa_mixer(xf, B, S, nsa_w_in[j], nsa_cmp_pos_k[j], nsa_cmp_pos_v[j], nsa_cmp_wk[j], nsa_cmp_wv[j])
            xf = _outproj_ln(y, nsa_w_out[j].astype(BF16), xf, g1, b1)
        elif kind == 2:
            xf = _dilated_layer(xf, B, S, dil_w_in[j], dil_w_out[j], g1, b1)
        else:
            xf = _pool_layer(xf, B, S, pool_w[j], pool_scale[j], g1, b1)
        xf = _moe_ple_layer(xf, pf[i], router_w[i], router_bias[i], exp_w_gate[i], exp_w_up[i], exp_w_down[i],
                            sh_w_gate[i], sh_w_up[i], sh_w_down[i], ln_g[i, 1], ln_b[i, 1], ple_w[i], ple_gate_w[i])
    return xf.reshape(B, S, D)
```

```python
import functools

import numpy as np
import jax
import jax.numpy as jnp
from jax import lax
from jax.experimental import pallas as pl
from jax.experimental.pallas import tpu as pltpu

F32 = jnp.float32
BF16 = jnp.bfloat16
I32 = jnp.int32
HIGHEST = lax.Precision.HIGHEST

D_MODEL = 1024
DEPTH = 4
ALPHA = (2.0 * DEPTH) ** 0.25
LN_EPS = 1e-5
NEG_INF = -1e30
TINY = 1e-30
BELOW_NEG_INF = -3e38
ROPE_THETA = 10000.0
HEAD_DIM = 64
ROPE_HALF = HEAD_DIM // 2

MLSTM_HEADS = 8
MLSTM_QK_DIM = 64
MLSTM_V_DIM = 128
MLSTM_CONV = 4
MLSTM_L = 256

NSA_HEADS = 16
NSA_KV_HEADS = 4
NSA_J = NSA_HEADS // NSA_KV_HEADS
CMP_STRIDE = 16
CMP_BLK = 32
SEL_BLK = 64
N_SEL = 16
NSA_WINDOW = 512
FORCE_SCORE = 1e4
NSA_TQ = 128
NSA_TK = 512

DIL_HEADS = 16
DIL_GROUPS = ((128, 1), (512, 4), (2048, 16))
DIL_STEPS = 128

POOL_WINDOWS = (2, 4, 8, 16)
POOL_GROUP = 256
POOL_HALO = 16

N_EXPERTS = 64
TOP_K = 8
N_EXPERT_GROUPS = 8
TOPK_GROUPS = 4
EXPERT_FF = 256
ROUTED_SCALE = 2.5
EXP_BLK = 256
PLE_DIM = 256

VMEM_LIMIT = 48 * 1024 * 1024


def _cparams(*sem):
    return pltpu.CompilerParams(dimension_semantics=sem, vmem_limit_bytes=VMEM_LIMIT)


def _nt(a, b, **kw):
    return lax.dot_general(a, b, (((1,), (1,)), ((), ())), preferred_element_type=F32, **kw)


def _tn(a, b, **kw):
    return lax.dot_general(a, b, (((0,), (0,)), ((), ())), preferred_element_type=F32, **kw)


def _mm(a, b, **kw):
    return jnp.dot(a, b, preferred_element_type=F32, **kw)


def _sigmoid(z):
    return 1.0 / (1.0 + jnp.exp(-z))


def _silu(z):
    return z * _sigmoid(z)


def _ln_rows(z, g, b):
    mu = jnp.mean(z, axis=-1, keepdims=True)
    d = z - mu
    var = jnp.mean(d * d, axis=-1, keepdims=True)
    return d * lax.rsqrt(var + LN_EPS) * g + b


def _proj_kernel(*refs, rope):
    if rope:
        x_ref, w_ref, cos_ref, sin_ref, o_ref, xb_ref = refs
    else:
        x_ref, w_ref, o_ref, xb_ref = refs

    @pl.when(pl.program_id(1) == 0)
    def _():
        xb_ref[...] = x_ref[...].astype(BF16)

    acc = _mm(xb_ref[...], w_ref[...])
    if rope:
        tn = acc.shape[1]
        lane = lax.broadcasted_iota(I32, acc.shape, 1)
        lo = (lane % HEAD_DIM) < ROPE_HALF
        rot = jnp.where(lo, pltpu.roll(acc, tn - ROPE_HALF, 1), pltpu.roll(acc, ROPE_HALF, 1))
        acc = acc * cos_ref[...] + rot * sin_ref[...]
    o_ref[...] = acc.astype(o_ref.dtype)


def _proj(x, w, *, out_dtype, tm, tn, rope_tables=None, seq=None):
    T, K = x.shape
    N = w.shape[1]
    assert T % tm == 0 and N % tn == 0
    in_specs = [pl.BlockSpec((tm, K), lambda i, j: (i, 0)), pl.BlockSpec((K, tn), lambda i, j: (0, j))]
    args = [x, w]
    if rope_tables is not None:
        nseq = seq // tm
        in_specs += [pl.BlockSpec((tm, tn), lambda i, j: (i % nseq, 0))] * 2
        args += list(rope_tables)
    return pl.pallas_call(
        functools.partial(_proj_kernel, rope=rope_tables is not None),
        name="proj_rope" if rope_tables is not None else "proj",
        grid=(T // tm, N // tn),
        in_specs=in_specs,
        out_specs=pl.BlockSpec((tm, tn), lambda i, j: (i, j)),
        out_shape=jax.ShapeDtypeStruct((T, N), out_dtype),
        scratch_shapes=[pltpu.VMEM((tm, K), BF16)],
        compiler_params=_cparams("parallel", "arbitrary"),
    )(*args)


def _rope_tables(pos, width):
    inv = ROPE_THETA ** (-jnp.arange(ROPE_HALF, dtype=F32) / ROPE_HALF)
    ang = pos.astype(F32)[:, None] * inv[None, :]
    cos, sin = jnp.cos(ang), jnp.sin(ang)
    cos64 = jnp.concatenate([cos, cos], -1)
    sin64 = jnp.concatenate([-sin, sin], -1)
    rep = width // HEAD_DIM
    return jnp.tile(cos64, (1, rep)), jnp.tile(sin64, (1, rep))


def _outproj_ln_kernel(y_ref, w_ref, x_ref, g_ref, b_ref, o_ref):
    y = _mm(y_ref[...].astype(BF16), w_ref[...])
    o_ref[...] = _ln_rows(ALPHA * x_ref[...] + y, g_ref[...], b_ref[...])


def _outproj_ln(y, w, x, g, b, *, tm=256):
    T, D = x.shape
    K = y.shape[1]
    row = lambda i: (i, 0)
    fixed = lambda i: (0, 0)
    return pl.pallas_call(
        _outproj_ln_kernel,
        name="outproj_ln",
        grid=(T // tm,),
        in_specs=[pl.BlockSpec((tm, K), row), pl.BlockSpec((K, D), fixed), pl.BlockSpec((tm, D), row),
                  pl.BlockSpec((1, D), fixed), pl.BlockSpec((1, D), fixed)],
        out_specs=pl.BlockSpec((tm, D), row),
        out_shape=jax.ShapeDtypeStruct((T, D), F32),
        compiler_params=_cparams("parallel"),
    )(y, w, x, g.reshape(1, D), b.reshape(1, D))


def _log_sigmoid(z):
    return jnp.minimum(z, 0.0) - jnp.log(1.0 + jnp.exp(-jnp.abs(z)))


def _mlstm_kernel(qk_ref, v_ref, o_ref, gc_ref, gr_ref, convw_ref, bias_c_ref, bias_r_ref, ng_ref,
                  tri_ref, triT_ref, out_ref, C_ref, n_ref, m_ref, ext_ref):
    L = qk_ref.shape[0]
    H, dk, dv = MLSTM_HEADS, MLSTM_QK_DIM, MLSTM_V_DIM

    @pl.when(pl.program_id(1) == 0)
    def _():
        C_ref[...] = jnp.zeros_like(C_ref)
        n_ref[...] = jnp.zeros_like(n_ref)
        m_ref[...] = jnp.zeros_like(m_ref)
        ext_ref[0:8, :] = jnp.zeros((8, ext_ref.shape[1]), F32)

    cur = qk_ref[...]
    ext_ref[8:8 + L, :] = cur
    acc = convw_ref[3:4, :] * cur
    for j in range(MLSTM_CONV - 1):
        acc = acc + convw_ref[j:j + 1, :] * ext_ref[5 + j:5 + j + L, :]
    ext_ref[0:8, :] = cur[L - 8:L, :]
    qk = _silu(acc)

    gc = gc_ref[...] + bias_c_ref[...]
    gr = gr_ref[...] + bias_r_ref[...]
    b_col = _mm(tri_ref[...], _log_sigmoid(gc), precision=HIGHEST)
    b_row = _mm(_log_sigmoid(gr[H:2 * H, :]), triT_ref[...], precision=HIGHEST)
    ig_row = gr[0:H, :]
    tri = lax.broadcasted_iota(I32, (L, L), 0) >= lax.broadcasted_iota(I32, (L, L), 1)

    for h in range(H):
        q = (qk[:, h * dk:(h + 1) * dk] * dk ** -0.5).astype(BF16)
        k = qk[:, H * dk + h * dk:H * dk + (h + 1) * dk]
        kb = k.astype(BF16)
        v = v_ref[:, h * dv:(h + 1) * dv].astype(BF16)
        b_c = b_col[:, H + h:H + h + 1]
        ig_c = gc[:, h:h + 1]
        b_r = b_row[h:h + 1, :]
        m_prev = m_ref[h:h + 1, 0:1]
        C = C_ref[h]
        n = n_ref[h:h + 1, :]

        logD = jnp.where(tri, b_c - b_r + ig_row[h:h + 1, :], NEG_INF)
        inter = b_c + m_prev
        m_t = jnp.maximum(inter, jnp.max(logD, axis=-1, keepdims=True))
        s = _nt(q, kb) * jnp.exp(logD - m_t)
        w_inter = jnp.exp(inter - m_t)
        num = w_inter * _mm(q, C.astype(BF16)) + _mm(s.astype(BF16), v)
        den = w_inter * jnp.sum(q.astype(F32) * n, axis=-1, keepdims=True) + jnp.sum(s, axis=-1, keepdims=True)
        hv = num / jnp.maximum(jnp.abs(den), jnp.exp(-m_t))

        mu = jnp.mean(hv, axis=-1, keepdims=True)
        d = hv - mu
        hn = d * lax.rsqrt(jnp.mean(d * d, axis=-1, keepdims=True) + LN_EPS)
        og = _sigmoid(o_ref[:, h * dv:(h + 1) * dv])
        out_ref[:, h * dv:(h + 1) * dv] = (hn * ng_ref[:, h * dv:(h + 1) * dv] * og).astype(out_ref.dtype)

        bL = b_c[L - 1:L, :]
        logw = bL - b_c + ig_c
        m_new = jnp.maximum(bL + m_prev, jnp.max(logw, axis=0, keepdims=True))
        decay = jnp.exp(bL + m_prev - m_new)
        kw = k * jnp.exp(logw - m_new)
        C_ref[h] = decay * C + _tn(kw.astype(BF16), v)
        n_ref[h:h + 1, :] = decay * n + jnp.sum(kw, axis=0, keepdims=True)
        m_ref[h:h + 1, :] = jnp.broadcast_to(m_new, (1, m_ref.shape[1]))


def _mlstm_mixer(x, B, S, w_in, conv_w, ig_bias, fg_bias, norm_g):
    T = B * S
    H, L = MLSTM_HEADS, MLSTM_L
    wb = w_in.astype(BF16)
    main = _proj(x, wb[:, :3 * D_MODEL], out_dtype=F32, tm=512, tn=512)
    wg = jnp.pad(wb[:, 3 * D_MODEL:], ((0, 0), (0, 128 - 2 * H)))
    gates = _proj(x, wg, out_dtype=F32, tm=512, tn=128)
    gates_r = gates[:, :2 * H].reshape(B, S, 2 * H).transpose(0, 2, 1)
    bias16 = jnp.concatenate([ig_bias, fg_bias]).astype(F32)
    bias_c = jnp.pad(bias16, (0, 128 - 2 * H)).reshape(1, 128)
    bias_r = bias16.reshape(2 * H, 1)
    tri = jnp.tril(jnp.ones((L, L), F32))
    nc = S // L
    rowblk = lambda c: (lambda b, i: (b * nc + i, c))
    fixed = lambda b, i: (0, 0)
    return pl.pallas_call(
        _mlstm_kernel,
        name="mlstm",
        grid=(B, nc),
        in_specs=[pl.BlockSpec((L, D_MODEL), rowblk(0)), pl.BlockSpec((L, D_MODEL), rowblk(1)),
                  pl.BlockSpec((L, D_MODEL), rowblk(2)), pl.BlockSpec((L, 128), rowblk(0)),
                  pl.BlockSpec((None, 2 * H, L), lambda b, i: (b, 0, i)),
                  pl.BlockSpec((MLSTM_CONV, D_MODEL), fixed), pl.BlockSpec((1, 128), fixed),
                  pl.BlockSpec((2 * H, 1), fixed), pl.BlockSpec((1, D_MODEL), fixed),
                  pl.BlockSpec((L, L), fixed), pl.BlockSpec((L, L), fixed)],
        out_specs=pl.BlockSpec((L, D_MODEL), rowblk(0)),
        out_shape=jax.ShapeDtypeStruct((T, D_MODEL), BF16),
        scratch_shapes=[pltpu.VMEM((H, MLSTM_QK_DIM, MLSTM_V_DIM), F32), pltpu.VMEM((H, MLSTM_QK_DIM), F32),
                        pltpu.VMEM((H, 128), F32), pltpu.VMEM((L + 8, D_MODEL), F32)],
        compiler_params=_cparams("parallel", "arbitrary"),
    )(main, main, main, gates, gates_r, conv_w.astype(F32), bias_c, bias_r, norm_g.reshape(1, D_MODEL).astype(F32),
      tri, tri.T)


def _rot_cols(w):
    shp = w.shape
    w4 = w.reshape(shp[:-1] + (shp[-1] // HEAD_DIM, 2, ROPE_HALF))
    return jnp.flip(w4, axis=-2).reshape(shp)


def _nsa_compress_kernel(chk_ref, chv_ref, wk_ref, wv_ref, pk_ref, pv_ref, cos_ref, sin_ref, kc_ref, vc_ref, sh_ref):
    NCH = chk_ref.shape[0]
    half = wk_ref.shape[0] // 2
    sh_ref[NCH:NCH + 8, :] = jnp.zeros((8, sh_ref.shape[1]), F32)
    live = lax.broadcasted_iota(I32, (NCH, 1), 0) < NCH - 1

    def blocks(ch_ref, w_ref, p_ref):
        ch = ch_ref[...].astype(BF16)
        a = _mm(ch, w_ref[0:half, :])
        sh_ref[0:NCH, :] = _mm(ch, w_ref[half:2 * half, :])
        const = _mm(p_ref[...].astype(BF16), w_ref[...])[0:1, :]
        return jnp.where(live, a + sh_ref[1:NCH + 1, :] + const, 0.0)

    k2 = blocks(chk_ref, wk_ref, pk_ref)
    kc_ref[...] = (k2[:, :HEAD_DIM] * cos_ref[...] + k2[:, HEAD_DIM:] * sin_ref[...]).astype(kc_ref.dtype)
    v2 = blocks(chv_ref, wv_ref, pv_ref)
    vc_ref[...] = v2[:, :HEAD_DIM].astype(vc_ref.dtype)


def _nsa_cmp_select_kernel(q_ref, kc_ref, vc_ref, ovlT_ref, ocmp_ref, sel_ref):
    TQ = q_ref.shape[0]
    NCH = kc_ref.shape[0]
    NSB = ovlT_ref.shape[0]
    q0 = pl.program_id(2) * TQ
    t = q0 + lax.broadcasted_iota(I32, (TQ, 1), 0)
    cend = lax.broadcasted_iota(I32, (1, NCH), 1) * CMP_STRIDE + (CMP_BLK - 1)
    valid = cend <= t
    kc = kc_ref[...]
    vc = vc_ref[...]
    psum = jnp.zeros((TQ, NCH), F32)
    outs = []
    for j in range(NSA_J):
        qj = q_ref[:, j * HEAD_DIM:(j + 1) * HEAD_DIM]
        s = jnp.where(valid, _nt(qj, kc) * HEAD_DIM ** -0.5, NEG_INF)
        m = jnp.max(s, axis=-1, keepdims=True)
        e = jnp.where(valid, jnp.exp(s - m), 0.0)
        p = e / jnp.maximum(jnp.sum(e, axis=-1, keepdims=True), TINY)
        outs.append(_mm(p.astype(BF16), vc))
        psum = psum + p
    ocmp_ref[...] = jnp.concatenate(outs, axis=1).astype(ocmp_ref.dtype)

    imp = _nt(ovlT_ref[...], psum, precision=HIGHEST)
    nb = lax.broadcasted_iota(I32, (NSB, 1), 0)
    qblk = (q0 + lax.broadcasted_iota(I32, (1, TQ), 1)) // SEL_BLK
    forced = (nb == 0) | (nb == qblk) | (nb == qblk - 1)
    cur = jnp.where(forced, FORCE_SCORE, jnp.where(nb > qblk, NEG_INF, imp))
    sel = jnp.zeros((NSB, TQ), F32)
    for _ in range(min(N_SEL, NSB)):
        m = jnp.max(cur, axis=0, keepdims=True)
        idx = jnp.min(jnp.where(cur == m, nb, NSB), axis=0, keepdims=True)
        hit = nb == idx
        sel = jnp.where(hit, 1.0, sel)
        cur = jnp.where(hit, BELOW_NEG_INF, cur)
    sel_ref[...] = sel.astype(sel_ref.dtype)


NEG_FLOOR = -1e29


def _nsa_main_kernel(q_ref, ks_ref, vs_ref, kw_ref, vw_ref, sel_ref, ocmp_ref, gate_ref, gexp_ref, o_ref):
    TQ = q_ref.shape[0]
    S = vs_ref.shape[0]
    TK = min(NSA_TK, S)
    J = NSA_J
    q0 = pl.program_id(2) * TQ
    qs = [q_ref[:, j * HEAD_DIM:(j + 1) * HEAD_DIM] * HEAD_DIM ** -0.5 for j in range(J)]
    t = q0 + lax.broadcasted_iota(I32, (TQ, 1), 0)
    selT = sel_ref[...]

    def attend(carry, k, v, bias):
        m, l, acc = carry
        ms, ls, accs = [], [], []
        for j in range(J):
            rows = slice(j * TQ, (j + 1) * TQ)
            s = _nt(qs[j], k) + bias
            m_new = jnp.maximum(m[rows], jnp.max(s, axis=-1, keepdims=True))
            e = jnp.exp(s - m_new)
            a = jnp.exp(m[rows] - m_new)
            ms.append(m_new)
            ls.append(a * l[rows] + jnp.sum(e, axis=-1, keepdims=True))
            accs.append(a * acc[rows] + _mm(e.astype(BF16), v))
        return jnp.concatenate(ms, axis=0), jnp.concatenate(ls, axis=0), jnp.concatenate(accs, axis=0)

    init = (jnp.full((J * TQ, 1), NEG_FLOOR, F32), jnp.zeros((J * TQ, 1), F32), jnp.zeros((J * TQ, HEAD_DIM), F32))

    def body(c, carry):
        k0 = pl.multiple_of(c * TK, TK)
        kpos = k0 + lax.broadcasted_iota(I32, (1, TK), 1)
        expand = jnp.where(kpos // SEL_BLK == lax.broadcasted_iota(I32, (selT.shape[0], 1), 0), 1.0, 0.0).astype(BF16)
        bias = jnp.where((_tn(selT, expand) > 0.5) & (kpos <= t), 0.0, NEG_INF)
        return attend(carry, ks_ref[pl.ds(k0, TK), :], vs_ref[pl.ds(k0, TK), :], bias)

    nchunks = (q0 + TQ - 1) // TK + 1
    _, l, acc = lax.fori_loop(0, nchunks, body, init)
    o_slc = acc / jnp.maximum(l, TINY)

    span = min(NSA_WINDOW + TQ, S)
    start = pl.multiple_of(jnp.clip(q0 - NSA_WINDOW, 0, S - span), TQ)
    dist = t - (start + lax.broadcasted_iota(I32, (1, span), 1))
    bias = jnp.where((dist >= 0) & (dist < NSA_WINDOW), 0.0, NEG_INF)
    _, l, acc = attend(init, kw_ref[pl.ds(start, span), :], vw_ref[pl.ds(start, span), :], bias)
    o_win = acc / jnp.maximum(l, TINY)

    unstack = lambda o: jnp.concatenate([o[j * TQ:(j + 1) * TQ, :] for j in range(J)], axis=1)
    g = _sigmoid(gate_ref[...])
    gx = [_mm(g, gexp_ref[c], precision=HIGHEST) for c in range(3)]
    out = gx[0] * ocmp_ref[...].astype(F32) + gx[1] * unstack(o_slc) + gx[2] * unstack(o_win)
    o_ref[...] = out.astype(o_ref.dtype)


def _nsa_mixer(x, B, S, w_in, cmp_pos_k, cmp_pos_v, cmp_wk, cmp_wv):
    T = B * S
    H, G, J, dh = NSA_HEADS, NSA_KV_HEADS, NSA_J, HEAD_DIM
    kv = G * dh
    wb = w_in.astype(BF16)
    col = lambda a, n: wb[:, a:a + n]
    o_q, o_kc, o_vc, o_ks, o_vs, o_kw, o_vw, o_g = np.cumsum([0, H * dh] + [kv] * 6).tolist()
    pos = jnp.arange(S)
    w_rope = jnp.concatenate([col(o_q, H * dh), col(o_ks, kv), col(o_kw, kv)], axis=1)
    roped = _proj(x, w_rope, out_dtype=BF16, tm=512, tn=512, rope_tables=_rope_tables(pos, 512), seq=S)
    w_plain = jnp.concatenate([col(o_kc, kv), col(o_vc, kv), col(o_vs, kv), col(o_vw, kv),
                               jnp.pad(col(o_g, 3 * H), ((0, 0), (0, 128 - 3 * H)))], axis=1)
    plain = _proj(x, w_plain, out_dtype=F32, tm=512, tn=128)

    heads = lambda a: a.reshape(B, S, G, dh).transpose(0, 2, 1, 3)
    ks = heads(roped[:, H * dh:H * dh + kv])
    kw = heads(roped[:, H * dh + kv:])
    vs = heads(plain[:, 2 * kv:3 * kv]).astype(BF16)
    vw = heads(plain[:, 3 * kv:4 * kv]).astype(BF16)
    gates = plain[:, 4 * kv:]

    nch = S // CMP_STRIDE
    chunks = lambda a: a.reshape(B, nch, CMP_STRIDE, G, dh).transpose(0, 3, 1, 2, 4).reshape(B, G, nch, CMP_STRIDE * dh)
    chk, chv = chunks(plain[:, :kv]), chunks(plain[:, kv:2 * kv])
    wk = cmp_wk.reshape(CMP_BLK * dh, dh)
    wk2 = jnp.concatenate([wk, _rot_cols(wk)], axis=1).astype(BF16)
    wv2 = jnp.pad(cmp_wv.reshape(CMP_BLK * dh, dh), ((0, 0), (0, dh))).astype(BF16)
    flat8 = lambda p_: jnp.pad(p_.reshape(1, CMP_BLK * dh), ((0, 7), (0, 0))).astype(F32)
    cend = jnp.arange(nch) * CMP_STRIDE + CMP_BLK - 1
    cos_c, sin_c = _rope_tables(cend, dh)
    bg = lambda b, g: (b, g, 0, 0)
    fixed2 = lambda b, g: (0, 0)
    kc, vc = pl.pallas_call(
        _nsa_compress_kernel,
        name="nsa_compress",
        grid=(B, G),
        in_specs=[pl.BlockSpec((None, None, nch, CMP_STRIDE * dh), bg)] * 2
        + [pl.BlockSpec((CMP_BLK * dh, 2 * dh), fixed2)] * 2 + [pl.BlockSpec((8, CMP_BLK * dh), fixed2)] * 2
        + [pl.BlockSpec((nch, dh), fixed2)] * 2,
        out_specs=[pl.BlockSpec((None, None, nch, dh), bg)] * 2,
        out_shape=[jax.ShapeDtypeStruct((B, G, nch, dh), BF16)] * 2,
        scratch_shapes=[pltpu.VMEM((nch + 8, 2 * dh), F32)],
        compiler_params=_cparams("parallel", "parallel"),
    )(chk, chv, wk2, wv2, flat8(cmp_pos_k), flat8(cmp_pos_v), cos_c, sin_c)

    nsb = S // SEL_BLK
    c_idx, s_idx = np.arange(nch), np.arange(nsb)
    ovl = ((c_idx[:, None] * CMP_STRIDE + CMP_BLK - 1 >= s_idx[None, :] * SEL_BLK)
           & (c_idx[:, None] * CMP_STRIDE < (s_idx[None, :] + 1) * SEL_BLK)).astype(np.float32)
    TQ = NSA_TQ
    nq = S // TQ
    qblk = lambda b, g, i: (b * nq + i, g)
    bgi = lambda b, g, i: (b, g, 0, 0)
    ocmp, sel = pl.pallas_call(
        _nsa_cmp_select_kernel,
        name="nsa_cmp_select",
        grid=(B, G, nq),
        in_specs=[pl.BlockSpec((TQ, J * dh), qblk), pl.BlockSpec((None, None, nch, dh), bgi),
                  pl.BlockSpec((None, None, nch, dh), bgi), pl.BlockSpec((nsb, nch), lambda b, g, i: (0, 0))],
        out_specs=[pl.BlockSpec((TQ, J * dh), qblk), pl.BlockSpec((None, None, nsb, TQ), lambda b, g, i: (b, g, 0, i))],
        out_shape=[jax.ShapeDtypeStruct((T, H * dh), BF16), jax.ShapeDtypeStruct((B, G, nsb, S), BF16)],
        compiler_params=_cparams("parallel", "parallel", "parallel"),
    )(roped, kc, vc, jnp.asarray(ovl.T))

    gexp = np.zeros((G, 3, 128, J * dh), np.float32)
    for g in range(G):
        for c in range(3):
            for j in range(J):
                gexp[g, c, (g * J + j) * 3 + c, j * dh:(j + 1) * dh] = 1.0
    return pl.pallas_call(
        _nsa_main_kernel,
        name="nsa_main",
        grid=(B, G, nq),
        in_specs=[pl.BlockSpec((TQ, J * dh), qblk)] + [pl.BlockSpec((None, None, S, dh), bgi)] * 4 + [
            pl.BlockSpec((None, None, nsb, TQ), lambda b, g, i: (b, g, 0, i)),
            pl.BlockSpec((TQ, J * dh), qblk), pl.BlockSpec((TQ, 128), lambda b, g, i: (b * nq + i, 0)),
            pl.BlockSpec((None, 3, 128, J * dh), lambda b, g, i: (g, 0, 0, 0))],
        out_specs=pl.BlockSpec((TQ, J * dh), qblk),
        out_shape=jax.ShapeDtypeStruct((T, H * dh), BF16),
        compiler_params=_cparams("parallel", "parallel", "arbitrary"),
    )(roped, ks, vs, kw, vw, sel, ocmp, gates, jnp.asarray(gexp))


def _dil_kernel(q_ref, kp_ref, kc_ref, vp_ref, vc_ref, o_ref, lse_ref):
    NQ = q_ref.shape[0]
    prev_from = jnp.where(pl.program_id(2) == 0, NQ, 0)
    qi = lax.broadcasted_iota(I32, (NQ, 1), 0)
    kj = lax.broadcasted_iota(I32, (1, 2 * NQ), 1)
    dist = NQ + qi - kj
    valid = (dist >= 0) & (dist <= NQ) & (kj >= prev_from)
    lane = lax.broadcasted_iota(I32, (NQ, 128), 1)
    lse_all = jnp.zeros((NQ, 128), F32)
    for h in range(DIL_HEADS):
        sl = slice(h * HEAD_DIM, (h + 1) * HEAD_DIM)
        kb = jnp.concatenate([kp_ref[:, sl], kc_ref[:, sl]], axis=0)
        vb = jnp.concatenate([vp_ref[:, sl], vc_ref[:, sl]], axis=0)
        s = jnp.where(valid, _nt(q_ref[:, sl], kb) * HEAD_DIM ** -0.5, NEG_INF)
        m = jnp.max(s, axis=-1, keepdims=True)
        e = jnp.exp(s - m)
        den = jnp.sum(e, axis=-1, keepdims=True)
        o_ref[:, sl] = _mm((e / den).astype(BF16), vb).astype(o_ref.dtype)
        lse_all = jnp.where(lane == h, m + jnp.log(den), lse_all)
    lse_ref[...] = lse_all


def _dil_group(parts, B, S, gi, dil):
    T = B * S
    NQ = DIL_STEPS
    U = S // dil
    nb = U // NQ
    W = D_MODEL
    view = parts.reshape(B, U, dil * 7 * W)
    cur = lambda part: (lambda b, r, n: (b, n, r * 7 + part))
    prev = lambda part: (lambda b, r, n: (b, jnp.maximum(n - 1, 0), r * 7 + part))
    blk = lambda f: pl.BlockSpec((None, NQ, W), f)
    o, lse = pl.pallas_call(
        _dil_kernel,
        name=f"dilated_attn_{dil}",
        grid=(B, dil, nb),
        in_specs=[blk(cur(2 * gi)), blk(prev(2 * gi + 1)), blk(cur(2 * gi + 1)), blk(prev(6)), blk(cur(6))],
        out_specs=[pl.BlockSpec((None, NQ, W), lambda b, r, n: (b, n, r)),
                   pl.BlockSpec((None, NQ, 128), lambda b, r, n: (b, n, r))],
        out_shape=[jax.ShapeDtypeStruct((B, U, dil * W), BF16), jax.ShapeDtypeStruct((B, U, dil * 128), F32)],
        compiler_params=_cparams("parallel", "parallel", "arbitrary"),
    )(view, view, view, view, view)
    return o.reshape(T, W), lse.reshape(T, 128)


def _dil_outproj_ln_kernel(o0_ref, o1_ref, o2_ref, l0_ref, l1_ref, l2_ref, hexp_ref, w_ref, x_ref, g_ref, b_ref, out_ref):
    ls = [l0_ref[...], l1_ref[...], l2_ref[...]]
    m = jnp.maximum(jnp.maximum(ls[0], ls[1]), ls[2])
    es = [jnp.exp(l - m) for l in ls]
    tot = es[0] + es[1] + es[2]
    y = jnp.zeros(o0_ref.shape, F32)
    for e, o_ref in zip(es, (o0_ref, o1_ref, o2_ref)):
        y = y + _mm(e / tot, hexp_ref[...], precision=HIGHEST) * o_ref[...].astype(F32)
    z = ALPHA * x_ref[...] + _mm(y.astype(BF16), w_ref[...])
    out_ref[...] = _ln_rows(z, g_ref[...], b_ref[...])


def _dilated_layer(x, B, S, w_in, w_out, g, b, *, tm=256):
    T, D = x.shape
    n_rope = 2 * len(DIL_GROUPS) * D
    wb = w_in.astype(BF16)
    roped = _proj(x, wb[:, :n_rope], out_dtype=BF16, tm=512, tn=512,
                  rope_tables=_rope_tables(jnp.arange(S), 512), seq=S)
    vals = _proj(x, wb[:, n_rope:], out_dtype=BF16, tm=512, tn=512)
    parts = jnp.concatenate([roped, vals], axis=1)
    outs = [_dil_group(parts, B, S, gi, dil) for gi, (_, dil) in enumerate(DIL_GROUPS)]
    hexp = np.zeros((128, D), np.float32)
    for h in range(DIL_HEADS):
        hexp[h, h * HEAD_DIM:(h + 1) * HEAD_DIM] = 1.0
    row = lambda i: (i, 0)
    fixed = lambda i: (0, 0)
    return pl.pallas_call(
        _dil_outproj_ln_kernel,
        name="dilated_outproj_ln",
        grid=(T // tm,),
        in_specs=[pl.BlockSpec((tm, D), row)] * 3 + [pl.BlockSpec((tm, 128), row)] * 3
        + [pl.BlockSpec((128, D), fixed), pl.BlockSpec((D, D), fixed), pl.BlockSpec((tm, D), row),
           pl.BlockSpec((1, D), fixed), pl.BlockSpec((1, D), fixed)],
        out_specs=pl.BlockSpec((tm, D), row),
        out_shape=jax.ShapeDtypeStruct((T, D), F32),
        compiler_params=_cparams("parallel"),
    )(outs[0][0], outs[1][0], outs[2][0], outs[0][1], outs[1][1], outs[2][1], jnp.asarray(hexp),
      w_out.astype(BF16), x, g.reshape(1, D), b.reshape(1, D))


def _pool_ln_kernel(x_ref, halo_ref, w_ref, scale_ref, g_ref, b_ref, o_ref, ext_ref):
    TS = x_ref.shape[0]
    s = pl.program_id(1)
    x = x_ref[...]
    ext_ref[0:POOL_HALO, :] = jnp.where(s == 0, 0.0, halo_ref[...])
    ext_ref[POOL_HALO:POOL_HALO + TS, :] = x
    cnt = (s * TS + lax.broadcasted_iota(I32, (TS, 1), 0) + 1).astype(F32)
    ys = []
    for gi, w in enumerate(POOL_WINDOWS):
        sl = slice(gi * POOL_GROUP, (gi + 1) * POOL_GROUP)
        xg = x[:, sl]
        tot = xg
        for j in range(1, w):
            tot = tot + ext_ref[POOL_HALO - j:POOL_HALO - j + TS, sl]
        mean = tot / jnp.minimum(cnt, float(w))
        ys.append(_mm((mean - xg).astype(BF16), w_ref[gi]))
    y = jnp.concatenate(ys, axis=1) * scale_ref[...]
    o_ref[...] = _ln_rows(ALPHA * x + y, g_ref[...], b_ref[...])


def _pool_layer(x, B, S, w_grp, scale, g, b, *, ts=512):
    T, D = x.shape
    ns = S // ts
    hb = ts // POOL_HALO
    fixed = lambda bb, s: (0, 0)
    return pl.pallas_call(
        _pool_ln_kernel,
        name="pool_ln",
        grid=(B, ns),
        in_specs=[pl.BlockSpec((ts, D), lambda bb, s: (bb * ns + s, 0)),
                  pl.BlockSpec((POOL_HALO, D), lambda bb, s: (jnp.maximum((bb * ns + s) * hb - 1, 0), 0)),
                  pl.BlockSpec((len(POOL_WINDOWS), POOL_GROUP, POOL_GROUP), lambda bb, s: (0, 0, 0)),
                  pl.BlockSpec((1, D), fixed), pl.BlockSpec((1, D), fixed), pl.BlockSpec((1, D), fixed)],
        out_specs=pl.BlockSpec((ts, D), lambda bb, s: (bb * ns + s, 0)),
        out_shape=jax.ShapeDtypeStruct((T, D), F32),
        scratch_shapes=[pltpu.VMEM((ts + POOL_HALO, D), F32)],
        compiler_params=_cparams("parallel", "arbitrary"),
    )(x, x, w_grp.astype(BF16), scale.reshape(1, D), g.reshape(1, D), b.reshape(1, D))


U32 = jnp.uint32
HI16 = 0xFFFF0000


def _pack_halves(v):
    h = v.shape[1] // 2
    bits = lambda a: lax.bitcast_convert_type(a.astype(BF16).astype(F32), U32)
    return (bits(v[:, :h]) >> 16) | (bits(v[:, h:]) & U32(HI16))


def _unpack_halves(w):
    lo = lax.bitcast_convert_type(w << 16, F32)
    hi = lax.bitcast_convert_type(w & U32(HI16), F32)
    return jnp.concatenate([lo, hi], axis=1)


def _router_kernel(x_ref, wT_ref, bias_ref, triu_ref, eidx_ref, gw_ref, rank_ref, cnt_ref, xp_ref, carry_ref):
    E = N_EXPERTS
    per = E // N_EXPERT_GROUPS
    TM = x_ref.shape[0]

    @pl.when(pl.program_id(0) == 0)
    def _():
        carry_ref[...] = jnp.zeros_like(carry_ref)

    x = x_ref[...]
    xp_ref[...] = _pack_halves(x)

    scores = _sigmoid(_nt(wT_ref[...], x, precision=HIGHEST))
    biased = scores + bias_ref[...]
    eio = lax.broadcasted_iota(I32, (E, TM), 0)

    gio = lax.broadcasted_iota(I32, (per, TM), 0)
    gscore = []
    for gidx in range(N_EXPERT_GROUPS):
        slab = biased[gidx * per:(gidx + 1) * per, :]
        m1 = jnp.max(slab, axis=0, keepdims=True)
        i1 = jnp.min(jnp.where(slab == m1, gio, per), axis=0, keepdims=True)
        m2 = jnp.max(jnp.where(gio == i1, BELOW_NEG_INF, slab), axis=0, keepdims=True)
        gscore.append(m1 + m2)
    slabs = []
    for gidx in range(N_EXPERT_GROUPS):
        beat = jnp.zeros((1, TM), F32)
        for o in range(N_EXPERT_GROUPS):
            if o == gidx:
                continue
            wins = (gscore[o] >= gscore[gidx]) if o < gidx else (gscore[o] > gscore[gidx])
            beat = beat + jnp.where(wins, 1.0, 0.0)
        keep = beat < float(TOPK_GROUPS)
        slabs.append(jnp.where(keep, biased[gidx * per:(gidx + 1) * per, :], NEG_INF))
    cur = jnp.concatenate(slabs, axis=0)

    picked = jnp.zeros((E, TM), F32)
    idxs, vals = [], []
    for _ in range(TOP_K):
        m = jnp.max(cur, axis=0, keepdims=True)
        idx = jnp.min(jnp.where(cur == m, eio, E), axis=0, keepdims=True)
        hit = eio == idx
        picked = jnp.where(hit, 1.0, picked)
        cur = jnp.where(hit, BELOW_NEG_INF, cur)
        idxs.append(idx)
        vals.append(jnp.sum(jnp.where(hit, scores, 0.0), axis=0, keepdims=True))
    total = vals[0]
    for v in vals[1:]:
        total = total + v

    pos = _mm(picked.astype(BF16), triu_ref[...]) + carry_ref[...]
    for k in range(TOP_K):
        eidx_ref[k:k + 1, :] = idxs[k]
        gw_ref[k:k + 1, :] = vals[k] / total * ROUTED_SCALE
        rank_ref[k:k + 1, :] = jnp.sum(jnp.where(eio == idxs[k], pos, 0.0), axis=0, keepdims=True).astype(I32)
    carry_ref[...] = carry_ref[...] + jnp.sum(picked, axis=1, keepdims=True)
    cnt_ref[...] = carry_ref[...].astype(I32)


def _router(x, router_w, router_bias, *, tm=512):
    T, D = x.shape
    E = N_EXPERTS
    triu = jnp.triu(jnp.ones((tm, tm), F32), k=1).astype(BF16)
    col = lambda i: (0, i)
    fixed = lambda i: (0, 0)
    return pl.pallas_call(
        _router_kernel,
        name="moe_router",
        grid=(T // tm,),
        in_specs=[pl.BlockSpec((tm, D), lambda i: (i, 0)), pl.BlockSpec((E, D), fixed),
                  pl.BlockSpec((E, 1), fixed), pl.BlockSpec((tm, tm), fixed)],
        out_specs=[pl.BlockSpec((TOP_K, tm), col), pl.BlockSpec((TOP_K, tm), col), pl.BlockSpec((TOP_K, tm), col),
                   pl.BlockSpec((E, 1), fixed), pl.BlockSpec((tm, D // 2), lambda i: (i, 0))],
        out_shape=[jax.ShapeDtypeStruct((TOP_K, T), I32), jax.ShapeDtypeStruct((TOP_K, T), F32),
                   jax.ShapeDtypeStruct((TOP_K, T), I32), jax.ShapeDtypeStruct((E, 1), I32),
                   jax.ShapeDtypeStruct((T, D // 2), U32)],
        scratch_shapes=[pltpu.VMEM((E, 1), F32)],
        compiler_params=_cparams("arbitrary"),
    )(x, router_w.T.astype(F32), router_bias.reshape(E, 1).astype(F32), triu)


def _row_copy(src_ref, s, dst_ref, d, sem):
    return pltpu.make_async_copy(src_ref.at[pl.ds(s, 1)], dst_ref.at[pl.ds(d, 1)], sem)


def _dispatch_kernel(cnt_ref, pstart_ref, dest_hbm, x_ref, rows_ref, idx_ref, zero_ref, sem_idx, sem_rows, sem_zero):
    TD = x_ref.shape[0]
    i = pl.program_id(0)
    idx_cp = pltpu.make_async_copy(dest_hbm.at[i], idx_ref, sem_idx)
    idx_cp.start()

    @pl.when(i == 0)
    def _():
        zero_ref[...] = jnp.zeros_like(zero_ref)

        def per_expert(e, c):
            n = cnt_ref[e]
            base = pstart_ref[e]
            pad_to = (n + EXP_BLK - 1) // EXP_BLK * EXP_BLK

            def fill(r, c2):
                _row_copy(zero_ref, 0, rows_ref, base + r, sem_zero).start()
                return c2

            def drain(r, c2):
                _row_copy(zero_ref, 0, rows_ref, base + r, sem_zero).wait()
                return c2

            lax.fori_loop(n, pad_to, fill, 0)
            lax.fori_loop(n, pad_to, drain, 0)
            return c

        lax.fori_loop(0, N_EXPERTS, per_expert, 0)

    idx_cp.wait()

    def issue(t, c):
        for k in range(TOP_K):
            _row_copy(x_ref, t, rows_ref, idx_ref[k * TD + t], sem_rows).start()
        return c

    def drain(t, c):
        for k in range(TOP_K):
            _row_copy(x_ref, t, rows_ref, idx_ref[k * TD + t], sem_rows).wait()
        return c

    lax.fori_loop(0, TD, issue, 0)
    lax.fori_loop(0, TD, drain, 0)


def _dispatch(xp, dest_tiles, counts, pad_start, n_rows, *, td):
    T, W = xp.shape
    return pl.pallas_call(
        _dispatch_kernel,
        name="moe_dispatch",
        grid_spec=pltpu.PrefetchScalarGridSpec(
            num_scalar_prefetch=2,
            grid=(T // td,),
            in_specs=[pl.BlockSpec(memory_space=pl.ANY), pl.BlockSpec((td, W), lambda i, c, p: (i, 0))],
            out_specs=pl.BlockSpec(memory_space=pl.ANY),
            scratch_shapes=[pltpu.SMEM((TOP_K * td,), I32), pltpu.VMEM((8, W), U32),
                            pltpu.SemaphoreType.DMA(()), pltpu.SemaphoreType.DMA(()), pltpu.SemaphoreType.DMA(())],
        ),
        out_shape=jax.ShapeDtypeStruct((n_rows, W), U32),
        compiler_params=_cparams("arbitrary"),
    )(counts, pad_start, dest_tiles, xp)


def _expert_kernel(be_ref, nu_ref, x_ref, wgu_ref, wd_ref, o_ref):
    @pl.when(pl.program_id(0) < nu_ref[0])
    def _():
        gu = _mm(_unpack_halves(x_ref[...]).astype(BF16), wgu_ref[0])
        h = _silu(gu[:, :EXPERT_FF]) * gu[:, EXPERT_FF:]
        o_ref[...] = _pack_halves(_mm(h.astype(BF16), wd_ref[0]))


def _experts(rows, blk_expert, n_used, wgu, wd):
    R, W = rows.shape
    D = 2 * W
    n_blk = R // EXP_BLK
    live = lambda i, be, nu: jnp.minimum(i, nu[0] - 1)
    return pl.pallas_call(
        _expert_kernel,
        name="moe_experts",
        grid_spec=pltpu.PrefetchScalarGridSpec(
            num_scalar_prefetch=2,
            grid=(n_blk,),
            in_specs=[pl.BlockSpec((EXP_BLK, W), lambda i, be, nu: (live(i, be, nu), 0)),
                      pl.BlockSpec((1, D, 2 * EXPERT_FF), lambda i, be, nu: (be[live(i, be, nu)], 0, 0)),
                      pl.BlockSpec((1, EXPERT_FF, D), lambda i, be, nu: (be[live(i, be, nu)], 0, 0))],
            out_specs=pl.BlockSpec((EXP_BLK, W), lambda i, be, nu: (live(i, be, nu), 0)),
        ),
        out_shape=jax.ShapeDtypeStruct((R, W), U32),
        compiler_params=_cparams("arbitrary"),
    )(blk_expert, n_used, rows, wgu, wd)


def _combine_kernel(dest_hbm, gw_ref, x_ref, rows_ref, wsgu_ref, wsd_ref, g_ref, b_ref, p_ref, wp_ref, wpg_ref,
                    o_ref, idx_ref, buf_ref, sem_idx, sem_rows):
    TM = x_ref.shape[0]
    i = pl.program_id(0)
    n = pl.num_programs(0)

    def idx_copy(step):
        return pltpu.make_async_copy(dest_hbm.at[step], idx_ref.at[step % 3], sem_idx.at[step % 3])

    def gather(step, t, k):
        row = idx_ref[step % 3, k * TM + t]
        return pltpu.make_async_copy(rows_ref.at[pl.ds(row, 1)], buf_ref.at[step % 2, k, pl.ds(t, 1)], sem_rows.at[step % 2])

    def issue(step):
        def body(t, c):
            for k in range(TOP_K):
                gather(step, t, k).start()
            return c
        lax.fori_loop(0, TM, body, 0)

    def drain(step):
        def body(t, c):
            for k in range(TOP_K):
                gather(step, t, k).wait()
            return c
        lax.fori_loop(0, TM, body, 0)

    @pl.when(i == 0)
    def _():
        idx_copy(0).start()
        idx_copy(0).wait()
        issue(0)

        @pl.when(n > 1)
        def _():
            idx_copy(1).start()

    @pl.when(i + 1 < n)
    def _():
        idx_copy(i + 1).wait()

        @pl.when(i + 2 < n)
        def _():
            idx_copy(i + 2).start()

        issue(i + 1)

    x = x_ref[...]
    gu = _mm(x.astype(BF16), wsgu_ref[...])
    ff = gu.shape[1] // 2
    f = _mm((_silu(gu[:, :ff]) * gu[:, ff:]).astype(BF16), wsd_ref[...])
    drain(i)
    gw = gw_ref[...]
    slot = i % 2
    for k in range(TOP_K):
        f = f + gw[:, k:k + 1] * _unpack_halves(buf_ref[slot, k])
    x2 = _ln_rows(ALPHA * x + f, g_ref[...], b_ref[...])
    gate = _sigmoid(_mm(x2.astype(BF16), wpg_ref[...]))
    o_ref[...] = x2 + gate * _mm(p_ref[...].astype(BF16), wp_ref[...])


def _combine(dest_tiles, gw, x, rows_out, wsgu, wsd, g, b, p, wp, wpg, *, tm):
    T, D = x.shape
    row = lambda i: (i, 0)
    fixed = lambda i: (0, 0)
    full = lambda a: pl.BlockSpec(a.shape, fixed)
    return pl.pallas_call(
        _combine_kernel,
        name="moe_combine",
        grid=(T // tm,),
        in_specs=[pl.BlockSpec(memory_space=pl.ANY), pl.BlockSpec((tm, TOP_K), row), pl.BlockSpec((tm, D), row),
                  pl.BlockSpec(memory_space=pl.ANY), full(wsgu), full(wsd), pl.BlockSpec((1, D), fixed),
                  pl.BlockSpec((1, D), fixed), pl.BlockSpec((tm, PLE_DIM), row), full(wp), full(wpg)],
        out_specs=pl.BlockSpec((tm, D), row),
        out_shape=jax.ShapeDtypeStruct((T, D), F32),
        scratch_shapes=[pltpu.SMEM((3, TOP_K * tm), I32), pltpu.VMEM((2, TOP_K, tm, D // 2), U32),
                        pltpu.SemaphoreType.DMA((3,)), pltpu.SemaphoreType.DMA((2,))],
        compiler_params=_cparams("arbitrary"),
    )(dest_tiles, gw, x, rows_out, wsgu, wsd, g.reshape(1, D), b.reshape(1, D), p, wp, wpg)


MOE_TILE = 256


def _moe_ple_layer(x, p, router_w, router_bias, w_gate, w_up, w_down, ws_gate, ws_up, ws_down, g, b, ple_w, ple_gate_w):
    T, D = x.shape
    eidx, gw, rank, counts, xp = _router(x, router_w, router_bias)
    counts = counts.reshape(N_EXPERTS)
    padded = (counts + EXP_BLK - 1) // EXP_BLK * EXP_BLK
    pad_end = jnp.cumsum(padded)
    pad_start = pad_end - padded
    n_blk = T * TOP_K // EXP_BLK + N_EXPERTS
    e_iota = jnp.arange(N_EXPERTS, dtype=I32)
    dest = rank + jnp.sum(jnp.where(eidx[..., None] == e_iota, pad_start, 0), axis=-1)
    tm = MOE_TILE
    dest_tiles = dest.reshape(TOP_K, T // tm, tm).transpose(1, 0, 2).reshape(T // tm, TOP_K * tm)
    blk_first = jnp.arange(n_blk, dtype=I32)[:, None] * EXP_BLK
    blk_expert = jnp.minimum(jnp.sum((pad_end[None, :] <= blk_first).astype(I32), axis=-1), N_EXPERTS - 1)
    n_used = (pad_end[-1:] // EXP_BLK).astype(I32)

    rows = _dispatch(xp, dest_tiles, counts.astype(I32), pad_start.astype(I32), n_blk * EXP_BLK, td=tm)
    wgu = jnp.concatenate([w_gate, w_up], axis=-1).astype(BF16)
    rows_out = _experts(rows, blk_expert, n_used, wgu, w_down.astype(BF16))
    wsgu = jnp.concatenate([ws_gate, ws_up], axis=-1).astype(BF16)
    return _combine(dest_tiles, gw.T, x, rows_out, wsgu, ws_down.astype(BF16), g, b, p,
                    ple_w.astype(BF16), ple_gate_w.astype(BF16), tm=tm)


def kernel(x, p, ln_g, ln_b, mlstm_w_in, mlstm_conv, mlstm_ig_bias, mlstm_fg_bias, mlstm_norm_g, mlstm_w_out, nsa_w_in, nsa_cmp_pos_k, nsa_cmp_pos_v, nsa_cmp_wk, nsa_cmp_wv, nsa_w_out, dil_w_in, dil_w_out, pool_w, pool_scale, router_w, router_bias, exp_w_gate, exp_w_up, exp_w_down, sh_w_gate, sh_w_up, sh_w_down, ple_w, ple_gate_w):
    B, S, D = x.shape
    T = B * S
    xf = x.reshape(T, D)
    pf = p.reshape(DEPTH, T, PLE_DIM)
    for i in range(DEPTH):
        kind, j = i % 4, i // 4
        g1, b1 = ln_g[i, 0], ln_b[i, 0]
        if kind == 0:
            y = _mlstm_mixer(xf, B, S, mlstm_w_in[j], mlstm_conv[j], mlstm_ig_bias[j], mlstm_fg_bias[j], mlstm_norm_g[j])
            xf = _outproj_ln(y, mlstm_w_out[j].astype(BF16), xf, g1, b1)
        elif kind == 1:
            y = _nsa_mixer(xf, B, S, nsa_w_in[j], nsa_cmp_pos_k[j], nsa_cmp_pos_v[j], nsa_cmp_wk[j], nsa_cmp_wv[j])
            xf = _outproj_ln(y, nsa_w_out[j].astype(BF16), xf, g1, b1)
        elif kind == 2:
            xf = _dilated_layer(xf, B, S, dil_w_in[j], dil_w_out[j], g1, b1)
        else:
            xf = _pool_layer(xf, B, S, pool_w[j], pool_scale[j], g1, b1)
        xf = _moe_ple_layer(xf, pf[i], router_w[i], router_bias[i], exp_w_gate[i], exp_w_up[i], exp_w_down[i],
                            sh_w_gate[i], sh_w_up[i], sh_w_down[i], ln_g[i, 1], ln_b[i, 1], ple_w[i], ple_gate_w[i])
    return xf.reshape(B, S, D)
```

```python
import functools

import numpy as np
import jax
import jax.numpy as jnp
from jax import lax
from jax.experimental import pallas as pl
from jax.experimental.pallas import tpu as pltpu

F32 = jnp.float32
BF16 = jnp.bfloat16
I32 = jnp.int32
HIGHEST = lax.Precision.HIGHEST

D_MODEL = 1024
DEPTH = 4
ALPHA = (2.0 * DEPTH) ** 0.25
LN_EPS = 1e-5
NEG_INF = -1e30
TINY = 1e-30
BELOW_NEG_INF = -3e38
ROPE_THETA = 10000.0
HEAD_DIM = 64
ROPE_HALF = HEAD_DIM // 2

MLSTM_HEADS = 8
MLSTM_QK_DIM = 64
MLSTM_V_DIM = 128
MLSTM_CONV = 4
MLSTM_L = 256

NSA_HEADS = 16
NSA_KV_HEADS = 4
NSA_J = NSA_HEADS // NSA_KV_HEADS
CMP_STRIDE = 16
CMP_BLK = 32
SEL_BLK = 64
N_SEL = 16
NSA_WINDOW = 512
FORCE_SCORE = 1e4
NSA_TQ = 128
NSA_TK = 512

DIL_HEADS = 16
DIL_GROUPS = ((128, 1), (512, 4), (2048, 16))
DIL_STEPS = 128

POOL_WINDOWS = (2, 4, 8, 16)
POOL_GROUP = 256
POOL_HALO = 16

N_EXPERTS = 64
TOP_K = 8
N_EXPERT_GROUPS = 8
TOPK_GROUPS = 4
EXPERT_FF = 256
ROUTED_SCALE = 2.5
EXP_BLK = 256
PLE_DIM = 256

VMEM_LIMIT = 48 * 1024 * 1024


def _cparams(*sem):
    return pltpu.CompilerParams(dimension_semantics=sem, vmem_limit_bytes=VMEM_LIMIT)


def _nt(a, b, **kw):
    return lax.dot_general(a, b, (((1,), (1,)), ((), ())), preferred_element_type=F32, **kw)


def _tn(a, b, **kw):
    return lax.dot_general(a, b, (((0,), (0,)), ((), ())), preferred_element_type=F32, **kw)


def _mm(a, b, **kw):
    return jnp.dot(a, b, preferred_element_type=F32, **kw)


def _sigmoid(z):
    return 1.0 / (1.0 + jnp.exp(-z))


def _silu(z):
    return z * _sigmoid(z)


def _ln_rows(z, g, b):
    mu = jnp.mean(z, axis=-1, keepdims=True)
    d = z - mu
    var = jnp.mean(d * d, axis=-1, keepdims=True)
    return d * lax.rsqrt(var + LN_EPS) * g + b


def _proj_kernel(*refs, rope):
    if rope:
        x_ref, w_ref, cos_ref, sin_ref, o_ref, xb_ref = refs
    else:
        x_ref, w_ref, o_ref, xb_ref = refs

    @pl.when(pl.program_id(1) == 0)
    def _():
        xb_ref[...] = x_ref[...].astype(BF16)

    acc = _mm(xb_ref[...], w_ref[...])
    if rope:
        tn = acc.shape[1]
        lane = lax.broadcasted_iota(I32, acc.shape, 1)
        lo = (lane % HEAD_DIM) < ROPE_HALF
        rot = jnp.where(lo, pltpu.roll(acc, tn - ROPE_HALF, 1), pltpu.roll(acc, ROPE_HALF, 1))
        acc = acc * cos_ref[...] + rot * sin_ref[...]
    o_ref[...] = acc.astype(o_ref.dtype)


def _proj(x, w, *, out_dtype, tm, tn, rope_tables=None, seq=None):
    T, K = x.shape
    N = w.shape[1]
    assert T % tm == 0 and N % tn == 0
    in_specs = [pl.BlockSpec((tm, K), lambda i, j: (i, 0)), pl.BlockSpec((K, tn), lambda i, j: (0, j))]
    args = [x, w]
    if rope_tables is not None:
        nseq = seq // tm
        in_specs += [pl.BlockSpec((tm, tn), lambda i, j: (i % nseq, 0))] * 2
        args += list(rope_tables)
    return pl.pallas_call(
        functools.partial(_proj_kernel, rope=rope_tables is not None),
        name="proj_rope" if rope_tables is not None else "proj",
        grid=(T // tm, N // tn),
        in_specs=in_specs,
        out_specs=pl.BlockSpec((tm, tn), lambda i, j: (i, j)),
        out_shape=jax.ShapeDtypeStruct((T, N), out_dtype),
        scratch_shapes=[pltpu.VMEM((tm, K), BF16)],
        compiler_params=_cparams("parallel", "arbitrary"),
    )(*args)


def _rope_tables(pos, width):
    inv = ROPE_THETA ** (-jnp.arange(ROPE_HALF, dtype=F32) / ROPE_HALF)
    ang = pos.astype(F32)[:, None] * inv[None, :]
    cos, sin = jnp.cos(ang), jnp.sin(ang)
    cos64 = jnp.concatenate([cos, cos], -1)
    sin64 = jnp.concatenate([-sin, sin], -1)
    rep = width // HEAD_DIM
    return jnp.tile(cos64, (1, rep)), jnp.tile(sin64, (1, rep))


def _outproj_ln_kernel(y_ref, w_ref, x_ref, g_ref, b_ref, o_ref):
    y = _mm(y_ref[...].astype(BF16), w_ref[...])
    o_ref[...] = _ln_rows(ALPHA * x_ref[...] + y, g_ref[...], b_ref[...])


def _outproj_ln(y, w, x, g, b, *, tm=256):
    T, D = x.shape
    K = y.shape[1]
    row = lambda i: (i, 0)
    fixed = lambda i: (0, 0)
    return pl.pallas_call(
        _outproj_ln_kernel,
        name="outproj_ln",
        grid=(T // tm,),
        in_specs=[pl.BlockSpec((tm, K), row), pl.BlockSpec((K, D), fixed), pl.BlockSpec((tm, D), row),
                  pl.BlockSpec((1, D), fixed), pl.BlockSpec((1, D), fixed)],
        out_specs=pl.BlockSpec((tm, D), row),
        out_shape=jax.ShapeDtypeStruct((T, D), F32),
        compiler_params=_cparams("parallel"),
    )(y, w, x, g.reshape(1, D), b.reshape(1, D))


def _log_sigmoid(z):
    return jnp.minimum(z, 0.0) - jnp.log(1.0 + jnp.exp(-jnp.abs(z)))


def _mlstm_kernel(qk_ref, v_ref, o_ref, gc_ref, gr_ref, convw_ref, bias_c_ref, bias_r_ref, ng_ref,
                  tri_ref, triT_ref, out_ref, C_ref, n_ref, m_ref, ext_ref):
    L = qk_ref.shape[0]
    H, dk, dv = MLSTM_HEADS, MLSTM_QK_DIM, MLSTM_V_DIM

    @pl.when(pl.program_id(1) == 0)
    def _():
        C_ref[...] = jnp.zeros_like(C_ref)
        n_ref[...] = jnp.zeros_like(n_ref)
        m_ref[...] = jnp.zeros_like(m_ref)
        ext_ref[0:8, :] = jnp.zeros((8, ext_ref.shape[1]), F32)

    cur = qk_ref[...]
    ext_ref[8:8 + L, :] = cur
    acc = convw_ref[3:4, :] * cur
    for j in range(MLSTM_CONV - 1):
        acc = acc + convw_ref[j:j + 1, :] * ext_ref[5 + j:5 + j + L, :]
    ext_ref[0:8, :] = cur[L - 8:L, :]
    qk = _silu(acc)

    gc = gc_ref[...] + bias_c_ref[...]
    gr = gr_ref[...] + bias_r_ref[...]
    b_col = _mm(tri_ref[...], _log_sigmoid(gc), precision=HIGHEST)
    b_row = _mm(_log_sigmoid(gr[H:2 * H, :]), triT_ref[...], precision=HIGHEST)
    ig_row = gr[0:H, :]
    tri = lax.broadcasted_iota(I32, (L, L), 0) >= lax.broadcasted_iota(I32, (L, L), 1)

    for h in range(H):
        q = (qk[:, h * dk:(h + 1) * dk] * dk ** -0.5).astype(BF16)
        k = qk[:, H * dk + h * dk:H * dk + (h + 1) * dk]
        kb = k.astype(BF16)
        v = v_ref[:, h * dv:(h + 1) * dv].astype(BF16)
        b_c = b_col[:, H + h:H + h + 1]
        ig_c = gc[:, h:h + 1]
        b_r = b_row[h:h + 1, :]
        m_prev = m_ref[h:h + 1, 0:1]
        C = C_ref[h]
        n = n_ref[h:h + 1, :]

        logD = jnp.where(tri, b_c - b_r + ig_row[h:h + 1, :], NEG_INF)
        inter = b_c + m_prev
        m_t = jnp.maximum(inter, jnp.max(logD, axis=-1, keepdims=True))
        s = _nt(q, kb) * jnp.exp(logD - m_t)
        w_inter = jnp.exp(inter - m_t)
        num = w_inter * _mm(q, C.astype(BF16)) + _mm(s.astype(BF16), v)
        den = w_inter * jnp.sum(q.astype(F32) * n, axis=-1, keepdims=True) + jnp.sum(s, axis=-1, keepdims=True)
        hv = num / jnp.maximum(jnp.abs(den), jnp.exp(-m_t))

        mu = jnp.mean(hv, axis=-1, keepdims=True)
        d = hv - mu
        hn = d * lax.rsqrt(jnp.mean(d * d, axis=-1, keepdims=True) + LN_EPS)
        og = _sigmoid(o_ref[:, h * dv:(h + 1) * dv])
        out_ref[:, h * dv:(h + 1) * dv] = (hn * ng_ref[:, h * dv:(h + 1) * dv] * og).astype(out_ref.dtype)

        bL = b_c[L - 1:L, :]
        logw = bL - b_c + ig_c
        m_new = jnp.maximum(bL + m_prev, jnp.max(logw, axis=0, keepdims=True))
        decay = jnp.exp(bL + m_prev - m_new)
        kw = k * jnp.exp(logw - m_new)
        C_ref[h] = decay * C + _tn(kw.astype(BF16), v)
        n_ref[h:h + 1, :] = decay * n + jnp.sum(kw, axis=0, keepdims=True)
        m_ref[h:h + 1, :] = jnp.broadcast_to(m_new, (1, m_ref.shape[1]))


def _mlstm_mixer(x, B, S, w_in, conv_w, ig_bias, fg_bias, norm_g):
    T = B * S
    H, L = MLSTM_HEADS, MLSTM_L
    wb = w_in.astype(BF16)
    main = _proj(x, wb[:, :3 * D_MODEL], out_dtype=F32, tm=512, tn=512)
    wg = jnp.pad(wb[:, 3 * D_MODEL:], ((0, 0), (0, 128 - 2 * H)))
    gates = _proj(x, wg, out_dtype=F32, tm=512, tn=128)
    gates_r = gates[:, :2 * H].reshape(B, S, 2 * H).transpose(0, 2, 1)
    bias16 = jnp.concatenate([ig_bias, fg_bias]).astype(F32)
    bias_c = jnp.pad(bias16, (0, 128 - 2 * H)).reshape(1, 128)
    bias_r = bias16.reshape(2 * H, 1)
    tri = jnp.tril(jnp.ones((L, L), F32))
    nc = S // L
    rowblk = lambda c: (lambda b, i: (b * nc + i, c))
    fixed = lambda b, i: (0, 0)
    return pl.pallas_call(
        _mlstm_kernel,
        name="mlstm",
        grid=(B, nc),
        in_specs=[pl.BlockSpec((L, D_MODEL), rowblk(0)), pl.BlockSpec((L, D_MODEL), rowblk(1)),
                  pl.BlockSpec((L, D_MODEL), rowblk(2)), pl.BlockSpec((L, 128), rowblk(0)),
                  pl.BlockSpec((None, 2 * H, L), lambda b, i: (b, 0, i)),
                  pl.BlockSpec((MLSTM_CONV, D_MODEL), fixed), pl.BlockSpec((1, 128), fixed),
                  pl.BlockSpec((2 * H, 1), fixed), pl.BlockSpec((1, D_MODEL), fixed),
                  pl.BlockSpec((L, L), fixed), pl.BlockSpec((L, L), fixed)],
        out_specs=pl.BlockSpec((L, D_MODEL), rowblk(0)),
        out_shape=jax.ShapeDtypeStruct((T, D_MODEL), BF16),
        scratch_shapes=[pltpu.VMEM((H, MLSTM_QK_DIM, MLSTM_V_DIM), F32), pltpu.VMEM((H, MLSTM_QK_DIM), F32),
                        pltpu.VMEM((H, 128), F32), pltpu.VMEM((L + 8, D_MODEL), F32)],
        compiler_params=_cparams("parallel", "arbitrary"),
    )(main, main, main, gates, gates_r, conv_w.astype(F32), bias_c, bias_r, norm_g.reshape(1, D_MODEL).astype(F32),
      tri, tri.T)


def _rot_cols(w):
    shp = w.shape
    w4 = w.reshape(shp[:-1] + (shp[-1] // HEAD_DIM, 2, ROPE_HALF))
    return jnp.flip(w4, axis=-2).reshape(shp)


def _nsa_compress_kernel(chk_ref, chv_ref, wk_lo_ref, wk_hi_ref, wv_lo_ref, wv_hi_ref, pk_ref, pv_ref, wk_ref, wv_ref,
                         cos_ref, sin_ref, kc_ref, vc_ref, sh_ref):
    NCH = chk_ref.shape[0]
    G = NSA_KV_HEADS
    sh_ref[NCH:NCH + 8, :] = jnp.zeros((8, sh_ref.shape[1]), F32)
    live = lax.broadcasted_iota(I32, (NCH, 1), 0) < NCH - 1

    def blocks(ch_ref, lo_ref, hi_ref, p_ref, w_ref):
        ch = ch_ref[...]
        n = lo_ref.shape[1]
        sh_ref[0:NCH, 0:n] = _mm(ch, hi_ref[...])
        const = _mm(p_ref[...].astype(BF16), w_ref[...])[0:1, :]
        return jnp.where(live, _mm(ch, lo_ref[...]) + sh_ref[1:NCH + 1, 0:n] + jnp.concatenate([const] * G, axis=1), 0.0)

    k2 = blocks(chk_ref, wk_lo_ref, wk_hi_ref, pk_ref, wk_ref)
    v2 = blocks(chv_ref, wv_lo_ref, wv_hi_ref, pv_ref, wv_ref)
    for g in range(G):
        raw = k2[:, 2 * g * HEAD_DIM:(2 * g + 1) * HEAD_DIM]
        rot = k2[:, (2 * g + 1) * HEAD_DIM:(2 * g + 2) * HEAD_DIM]
        kc_ref[g] = (raw * cos_ref[...] + rot * sin_ref[...]).astype(kc_ref.dtype)
        vc_ref[g] = v2[:, g * HEAD_DIM:(g + 1) * HEAD_DIM].astype(vc_ref.dtype)


def _nsa_cmp_select_kernel(q_ref, kc_ref, vc_ref, ovlT_ref, ocmp_ref, sel_ref):
    TQ = q_ref.shape[0]
    NCH = kc_ref.shape[0]
    NSB = ovlT_ref.shape[0]
    q0 = pl.program_id(2) * TQ
    t = q0 + lax.broadcasted_iota(I32, (TQ, 1), 0)
    cend = lax.broadcasted_iota(I32, (1, NCH), 1) * CMP_STRIDE + (CMP_BLK - 1)
    valid = cend <= t
    kc = kc_ref[...]
    vc = vc_ref[...]
    psum = jnp.zeros((TQ, NCH), F32)
    outs = []
    for j in range(NSA_J):
        qj = q_ref[:, j * HEAD_DIM:(j + 1) * HEAD_DIM]
        s = jnp.where(valid, _nt(qj, kc) * HEAD_DIM ** -0.5, NEG_INF)
        m = jnp.max(s, axis=-1, keepdims=True)
        e = jnp.where(valid, jnp.exp(s - m), 0.0)
        p = e / jnp.maximum(jnp.sum(e, axis=-1, keepdims=True), TINY)
        outs.append(_mm(p.astype(BF16), vc))
        psum = psum + p
    ocmp_ref[...] = jnp.concatenate(outs, axis=1).astype(ocmp_ref.dtype)

    imp = _nt(ovlT_ref[...], psum, precision=HIGHEST)
    nb = lax.broadcasted_iota(I32, (NSB, 1), 0)
    qblk = (q0 + lax.broadcasted_iota(I32, (1, TQ), 1)) // SEL_BLK
    forced = (nb == 0) | (nb == qblk) | (nb == qblk - 1)
    cur = jnp.where(forced, FORCE_SCORE, jnp.where(nb > qblk, NEG_INF, imp))
    sel = jnp.zeros((NSB, TQ), F32)
    for _ in range(min(N_SEL, NSB)):
        m = jnp.max(cur, axis=0, keepdims=True)
        idx = jnp.min(jnp.where(cur == m, nb, NSB), axis=0, keepdims=True)
        hit = nb == idx
        sel = jnp.where(hit, 1.0, sel)
        cur = jnp.where(hit, BELOW_NEG_INF, cur)
    sel_ref[...] = sel.astype(sel_ref.dtype)


NEG_FLOOR = -1e29


def _nsa_main_kernel(q_ref, ksT_ref, vs_ref, kwT_ref, vw_ref, sel_ref, ocmp_ref, gate_ref, gexp_ref, o_ref):
    TQ = q_ref.shape[0]
    S = vs_ref.shape[0]
    TK = min(NSA_TK, S)
    J = NSA_J
    q0 = pl.program_id(2) * TQ
    q4 = jnp.concatenate([q_ref[:, j * HEAD_DIM:(j + 1) * HEAD_DIM] for j in range(J)], axis=0) * HEAD_DIM ** -0.5
    t = q0 + lax.broadcasted_iota(I32, (TQ, 1), 0)
    selT = sel_ref[...]

    def attend(carry, kT, v, bias):
        m, l, acc = carry
        s = _mm(q4, kT) + jnp.concatenate([bias] * J, axis=0)
        m_new = jnp.maximum(m, jnp.max(s, axis=-1, keepdims=True))
        e = jnp.exp(s - m_new)
        a = jnp.exp(m - m_new)
        return m_new, a * l + jnp.sum(e, axis=-1, keepdims=True), a * acc + _mm(e.astype(BF16), v)

    init = (jnp.full((J * TQ, 1), NEG_FLOOR, F32), jnp.zeros((J * TQ, 1), F32), jnp.zeros((J * TQ, HEAD_DIM), F32))

    def body(c, carry):
        k0 = pl.multiple_of(c * TK, TK)
        kpos = k0 + lax.broadcasted_iota(I32, (1, TK), 1)
        expand = jnp.where(kpos // SEL_BLK == lax.broadcasted_iota(I32, (selT.shape[0], 1), 0), 1.0, 0.0).astype(BF16)
        bias = jnp.where((_tn(selT, expand) > 0.5) & (kpos <= t), 0.0, NEG_INF)
        return attend(carry, ksT_ref[:, pl.ds(k0, TK)], vs_ref[pl.ds(k0, TK), :], bias)

    nchunks = (q0 + TQ - 1) // TK + 1
    _, l, acc = lax.fori_loop(0, nchunks, body, init)
    o_slc = acc / jnp.maximum(l, TINY)

    span = min(NSA_WINDOW + TQ, S)
    start = pl.multiple_of(jnp.clip(q0 - NSA_WINDOW, 0, S - span), TQ)
    dist = t - (start + lax.broadcasted_iota(I32, (1, span), 1))
    bias = jnp.where((dist >= 0) & (dist < NSA_WINDOW), 0.0, NEG_INF)
    _, l, acc = attend(init, kwT_ref[:, pl.ds(start, span)], vw_ref[pl.ds(start, span), :], bias)
    o_win = acc / jnp.maximum(l, TINY)

    unstack = lambda o: jnp.concatenate([o[j * TQ:(j + 1) * TQ, :] for j in range(J)], axis=1)
    g = _sigmoid(gate_ref[...])
    gx = [_mm(g, gexp_ref[c], precision=HIGHEST) for c in range(3)]
    out = gx[0] * ocmp_ref[...].astype(F32) + gx[1] * unstack(o_slc) + gx[2] * unstack(o_win)
    o_ref[...] = out.astype(o_ref.dtype)


def _nsa_mixer(x, B, S, w_in, cmp_pos_k, cmp_pos_v, cmp_wk, cmp_wv):
    T = B * S
    H, G, J, dh = NSA_HEADS, NSA_KV_HEADS, NSA_J, HEAD_DIM
    kv = G * dh
    wb = w_in.astype(BF16)
    col = lambda a, n: wb[:, a:a + n]
    o_q, o_kc, o_vc, o_ks, o_vs, o_kw, o_vw, o_g = np.cumsum([0, H * dh] + [kv] * 6).tolist()
    pos = jnp.arange(S)
    w_rope = jnp.concatenate([col(o_q, H * dh), col(o_ks, kv), col(o_kw, kv)], axis=1)
    roped = _proj(x, w_rope, out_dtype=BF16, tm=512, tn=512, rope_tables=_rope_tables(pos, 512), seq=S)
    w_plain = jnp.concatenate([col(o_vs, kv), col(o_vw, kv), jnp.pad(col(o_g, 3 * H), ((0, 0), (0, 128 - 3 * H)))], axis=1)
    plain = _proj(x, w_plain, out_dtype=F32, tm=512, tn=128)

    heads_T = lambda a: a.reshape(B, S, G, dh).transpose(0, 2, 3, 1)
    heads = lambda a: a.reshape(B, S, G, dh).transpose(0, 2, 1, 3)
    ksT = heads_T(roped[:, H * dh:H * dh + kv])
    kwT = heads_T(roped[:, H * dh + kv:])
    vs = heads(plain[:, :kv]).astype(BF16)
    vw = heads(plain[:, kv:2 * kv]).astype(BF16)
    gates = plain[:, 2 * kv:]

    nch = S // CMP_STRIDE
    cw = CMP_STRIDE * kv
    chk = _proj(x, col(o_kc, kv), out_dtype=BF16, tm=512, tn=kv).reshape(B, nch, cw)
    chv = _proj(x, col(o_vc, kv), out_dtype=BF16, tm=512, tn=kv).reshape(B, nch, cw)
    wk = cmp_wk.reshape(CMP_BLK * dh, dh)
    wk2 = jnp.concatenate([wk, _rot_cols(wk)], axis=1).astype(BF16)
    wv2 = cmp_wv.reshape(CMP_BLK * dh, dh).astype(BF16)
    eye = jnp.eye(G, dtype=BF16)

    def per_head(w, part):
        n = w.shape[1]
        w3 = w.reshape(2, CMP_STRIDE, dh, n)[part]
        return jnp.einsum('lde,gh->lgdhe', w3, eye).reshape(cw, G * n)

    flat8 = lambda p_: jnp.pad(p_.reshape(1, CMP_BLK * dh), ((0, 7), (0, 0))).astype(F32)
    cend = jnp.arange(nch) * CMP_STRIDE + CMP_BLK - 1
    cos_c, sin_c = _rope_tables(cend, dh)
    fixed1 = lambda b: (0, 0)
    full1 = lambda a: pl.BlockSpec(a.shape, fixed1)
    consts = (per_head(wk2, 0), per_head(wk2, 1), per_head(wv2, 0), per_head(wv2, 1), flat8(cmp_pos_k), flat8(cmp_pos_v),
              wk2, wv2, cos_c, sin_c)
    kc, vc = pl.pallas_call(
        _nsa_compress_kernel,
        name="nsa_compress",
        grid=(B,),
        in_specs=[pl.BlockSpec((None, nch, cw), lambda b: (b, 0, 0))] * 2 + [full1(c) for c in consts],
        out_specs=[pl.BlockSpec((None, G, nch, dh), lambda b: (b, 0, 0, 0))] * 2,
        out_shape=[jax.ShapeDtypeStruct((B, G, nch, dh), BF16)] * 2,
        scratch_shapes=[pltpu.VMEM((nch + 8, 2 * kv), F32)],
        compiler_params=_cparams("parallel"),
    )(chk, chv, *consts)

    nsb = S // SEL_BLK
    c_idx, s_idx = np.arange(nch), np.arange(nsb)
    ovl = ((c_idx[:, None] * CMP_STRIDE + CMP_BLK - 1 >= s_idx[None, :] * SEL_BLK)
           & (c_idx[:, None] * CMP_STRIDE < (s_idx[None, :] + 1) * SEL_BLK)).astype(np.float32)
    TQ = NSA_TQ
    nq = S // TQ
    qblk = lambda b, g, i: (b * nq + i, g)
    bgi = lambda b, g, i: (b, g, 0, 0)
    ocmp, sel = pl.pallas_call(
        _nsa_cmp_select_kernel,
        name="nsa_cmp_select",
        grid=(B, G, nq),
        in_specs=[pl.BlockSpec((TQ, J * dh), qblk), pl.BlockSpec((None, None, nch, dh), bgi),
                  pl.BlockSpec((None, None, nch, dh), bgi), pl.BlockSpec((nsb, nch), lambda b, g, i: (0, 0))],
        out_specs=[pl.BlockSpec((TQ, J * dh), qblk), pl.BlockSpec((None, None, nsb, TQ), lambda b, g, i: (b, g, 0, i))],
        out_shape=[jax.ShapeDtypeStruct((T, H * dh), BF16), jax.ShapeDtypeStruct((B, G, nsb, S), BF16)],
        compiler_params=_cparams("parallel", "parallel", "parallel"),
    )(roped, kc, vc, jnp.asarray(ovl.T))

    gexp = np.zeros((G, 3, 128, J * dh), np.float32)
    for g in range(G):
        for c in range(3):
            for j in range(J):
                gexp[g, c, (g * J + j) * 3 + c, j * dh:(j + 1) * dh] = 1.0
    return pl.pallas_call(
        _nsa_main_kernel,
        name="nsa_main",
        grid=(B, G, nq),
        in_specs=[pl.BlockSpec((TQ, J * dh), qblk)] + [pl.BlockSpec((None, None, dh, S), bgi),
                                                       pl.BlockSpec((None, None, S, dh), bgi)] * 2 + [
            pl.BlockSpec((None, None, nsb, TQ), lambda b, g, i: (b, g, 0, i)),
            pl.BlockSpec((TQ, J * dh), qblk), pl.BlockSpec((TQ, 128), lambda b, g, i: (b * nq + i, 0)),
            pl.BlockSpec((None, 3, 128, J * dh), lambda b, g, i: (g, 0, 0, 0))],
        out_specs=pl.BlockSpec((TQ, J * dh), qblk),
        out_shape=jax.ShapeDtypeStruct((T, H * dh), BF16),
        compiler_params=_cparams("parallel", "parallel", "arbitrary"),
    )(roped, ksT, vs, kwT, vw, sel, ocmp, gates, jnp.asarray(gexp))


def _dil_kernel(q_ref, kp_ref, kc_ref, vp_ref, vc_ref, o_ref, lse_ref):
    NQ = q_ref.shape[0]
    prev_from = jnp.where(pl.program_id(2) == 0, NQ, 0)
    qi = lax.broadcasted_iota(I32, (NQ, 1), 0)
    kj = lax.broadcasted_iota(I32, (1, 2 * NQ), 1)
    dist = NQ + qi - kj
    valid = (dist >= 0) & (dist <= NQ) & (kj >= prev_from)
    lane = lax.broadcasted_iota(I32, (NQ, 128), 1)
    lse_all = jnp.zeros((NQ, 128), F32)
    for h in range(DIL_HEADS):
        sl = slice(h * HEAD_DIM, (h + 1) * HEAD_DIM)
        kb = jnp.concatenate([kp_ref[:, sl], kc_ref[:, sl]], axis=0)
        vb = jnp.concatenate([vp_ref[:, sl], vc_ref[:, sl]], axis=0)
        s = jnp.where(valid, _nt(q_ref[:, sl], kb) * HEAD_DIM ** -0.5, NEG_INF)
        m = jnp.max(s, axis=-1, keepdims=True)
        e = jnp.exp(s - m)
        den = jnp.sum(e, axis=-1, keepdims=True)
        o_ref[:, sl] = _mm((e / den).astype(BF16), vb).astype(o_ref.dtype)
        lse_all = jnp.where(lane == h, m + jnp.log(den), lse_all)
    lse_ref[...] = lse_all


def _dil_group(parts, B, S, gi, dil):
    T = B * S
    NQ = DIL_STEPS
    U = S // dil
    nb = U // NQ
    W = D_MODEL
    view = parts.reshape(B, U, dil * 7 * W)
    cur = lambda part: (lambda b, r, n: (b, n, r * 7 + part))
    prev = lambda part: (lambda b, r, n: (b, jnp.maximum(n - 1, 0), r * 7 + part))
    blk = lambda f: pl.BlockSpec((None, NQ, W), f)
    o, lse = pl.pallas_call(
        _dil_kernel,
        name=f"dilated_attn_{dil}",
        grid=(B, dil, nb),
        in_specs=[blk(cur(2 * gi)), blk(prev(2 * gi + 1)), blk(cur(2 * gi + 1)), blk(prev(6)), blk(cur(6))],
        out_specs=[pl.BlockSpec((None, NQ, W), lambda b, r, n: (b, n, r)),
                   pl.BlockSpec((None, NQ, 128), lambda b, r, n: (b, n, r))],
        out_shape=[jax.ShapeDtypeStruct((B, U, dil * W), BF16), jax.ShapeDtypeStruct((B, U, dil * 128), F32)],
        compiler_params=_cparams("parallel", "parallel", "arbitrary"),
    )(view, view, view, view, view)
    return o.reshape(T, W), lse.reshape(T, 128)


def _dil_outproj_ln_kernel(o0_ref, o1_ref, o2_ref, l0_ref, l1_ref, l2_ref, hexp_ref, w_ref, x_ref, g_ref, b_ref, out_ref):
    ls = [l0_ref[...], l1_ref[...], l2_ref[...]]
    m = jnp.maximum(jnp.maximum(ls[0], ls[1]), ls[2])
    es = [jnp.exp(l - m) for l in ls]
    tot = es[0] + es[1] + es[2]
    y = jnp.zeros(o0_ref.shape, F32)
    for e, o_ref in zip(es, (o0_ref, o1_ref, o2_ref)):
        y = y + _mm(e / tot, hexp_ref[...], precision=HIGHEST) * o_ref[...].astype(F32)
    z = ALPHA * x_ref[...] + _mm(y.astype(BF16), w_ref[...])
    out_ref[...] = _ln_rows(z, g_ref[...], b_ref[...])


def _dilated_layer(x, B, S, w_in, w_out, g, b, *, tm=256):
    T, D = x.shape
    n_rope = 2 * len(DIL_GROUPS) * D
    wb = w_in.astype(BF16)
    roped = _proj(x, wb[:, :n_rope], out_dtype=BF16, tm=512, tn=512,
                  rope_tables=_rope_tables(jnp.arange(S), 512), seq=S)
    vals = _proj(x, wb[:, n_rope:], out_dtype=BF16, tm=512, tn=512)
    parts = jnp.concatenate([roped, vals], axis=1)
    outs = [_dil_group(parts, B, S, gi, dil) for gi, (_, dil) in enumerate(DIL_GROUPS)]
    hexp = np.zeros((128, D), np.float32)
    for h in range(DIL_HEADS):
        hexp[h, h * HEAD_DIM:(h + 1) * HEAD_DIM] = 1.0
    row = lambda i: (i, 0)
    fixed = lambda i: (0, 0)
    return pl.pallas_call(
        _dil_outproj_ln_kernel,
        name="dilated_outproj_ln",
        grid=(T // tm,),
        in_specs=[pl.BlockSpec((tm, D), row)] * 3 + [pl.BlockSpec((tm, 128), row)] * 3
        + [pl.BlockSpec((128, D), fixed), pl.BlockSpec((D, D), fixed), pl.BlockSpec((tm, D), row),
           pl.BlockSpec((1, D), fixed), pl.BlockSpec((1, D), fixed)],
        out_specs=pl.BlockSpec((tm, D), row),
        out_shape=jax.ShapeDtypeStruct((T, D), F32),
        compiler_params=_cparams("parallel"),
    )(outs[0][0], outs[1][0], outs[2][0], outs[0][1], outs[1][1], outs[2][1], jnp.asarray(hexp),
      w_out.astype(BF16), x, g.reshape(1, D), b.reshape(1, D))


def _pool_ln_kernel(x_ref, halo_ref, w_ref, scale_ref, g_ref, b_ref, o_ref, ext_ref):
    TS = x_ref.shape[0]
    s = pl.program_id(1)
    x = x_ref[...]
    ext_ref[0:POOL_HALO, :] = jnp.where(s == 0, 0.0, halo_ref[...])
    ext_ref[POOL_HALO:POOL_HALO + TS, :] = x
    cnt = (s * TS + lax.broadcasted_iota(I32, (TS, 1), 0) + 1).astype(F32)
    ys = []
    for gi, w in enumerate(POOL_WINDOWS):
        sl = slice(gi * POOL_GROUP, (gi + 1) * POOL_GROUP)
        xg = x[:, sl]
        tot = xg
        for j in range(1, w):
            tot = tot + ext_ref[POOL_HALO - j:POOL_HALO - j + TS, sl]
        mean = tot / jnp.minimum(cnt, float(w))
        ys.append(_mm((mean - xg).astype(BF16), w_ref[gi]))
    y = jnp.concatenate(ys, axis=1) * scale_ref[...]
    o_ref[...] = _ln_rows(ALPHA * x + y, g_ref[...], b_ref[...])


def _pool_layer(x, B, S, w_grp, scale, g, b, *, ts=512):
    T, D = x.shape
    ns = S // ts
    hb = ts // POOL_HALO
    fixed = lambda bb, s: (0, 0)
    return pl.pallas_call(
        _pool_ln_kernel,
        name="pool_ln",
        grid=(B, ns),
        in_specs=[pl.BlockSpec((ts, D), lambda bb, s: (bb * ns + s, 0)),
                  pl.BlockSpec((POOL_HALO, D), lambda bb, s: (jnp.maximum((bb * ns + s) * hb - 1, 0), 0)),
                  pl.BlockSpec((len(POOL_WINDOWS), POOL_GROUP, POOL_GROUP), lambda bb, s: (0, 0, 0)),
                  pl.BlockSpec((1, D), fixed), pl.BlockSpec((1, D), fixed), pl.BlockSpec((1, D), fixed)],
        out_specs=pl.BlockSpec((ts, D), lambda bb, s: (bb * ns + s, 0)),
        out_shape=jax.ShapeDtypeStruct((T, D), F32),
        scratch_shapes=[pltpu.VMEM((ts + POOL_HALO, D), F32)],
        compiler_params=_cparams("parallel", "arbitrary"),
    )(x, x, w_grp.astype(BF16), scale.reshape(1, D), g.reshape(1, D), b.reshape(1, D))


U32 = jnp.uint32
HI16 = 0xFFFF0000
ROW_WORDS = D_MODEL // 2


def _pack_halves(v):
    h = v.shape[1] // 2
    bits = lambda a: lax.bitcast_convert_type(a.astype(BF16).astype(F32), U32)
    return (bits(v[:, :h]) >> 16) | (bits(v[:, h:]) & U32(HI16))


def _unpack_halves(w):
    lo = lax.bitcast_convert_type(w << 16, F32)
    hi = lax.bitcast_convert_type(w & U32(HI16), F32)
    return jnp.concatenate([lo, hi], axis=1)


def _router_kernel(x_ref, wT_ref, bias_ref, triu_ref, eidx_ref, gw_ref, rank_ref, cnt_ref, xp_ref, carry_ref):
    E = N_EXPERTS
    per = E // N_EXPERT_GROUPS
    TM = x_ref.shape[0]

    @pl.when(pl.program_id(0) == 0)
    def _():
        carry_ref[...] = jnp.zeros_like(carry_ref)

    x = x_ref[...]
    xp_ref[...] = _pack_halves(x)

    scores = _sigmoid(_nt(wT_ref[...], x, precision=HIGHEST))
    biased = scores + bias_ref[...]
    eio = lax.broadcasted_iota(I32, (E, TM), 0)

    gio = lax.broadcasted_iota(I32, (per, TM), 0)
    gscore = []
    for gidx in range(N_EXPERT_GROUPS):
        slab = biased[gidx * per:(gidx + 1) * per, :]
        m1 = jnp.max(slab, axis=0, keepdims=True)
        i1 = jnp.min(jnp.where(slab == m1, gio, per), axis=0, keepdims=True)
        m2 = jnp.max(jnp.where(gio == i1, BELOW_NEG_INF, slab), axis=0, keepdims=True)
        gscore.append(m1 + m2)
    slabs = []
    for gidx in range(N_EXPERT_GROUPS):
        beat = jnp.zeros((1, TM), F32)
        for o in range(N_EXPERT_GROUPS):
            if o == gidx:
                continue
            wins = (gscore[o] >= gscore[gidx]) if o < gidx else (gscore[o] > gscore[gidx])
            beat = beat + jnp.where(wins, 1.0, 0.0)
        keep = beat < float(TOPK_GROUPS)
        slabs.append(jnp.where(keep, biased[gidx * per:(gidx + 1) * per, :], NEG_INF))
    cur = jnp.concatenate(slabs, axis=0)

    picked = jnp.zeros((E, TM), F32)
    idxs, vals = [], []
    for _ in range(TOP_K):
        m = jnp.max(cur, axis=0, keepdims=True)
        idx = jnp.min(jnp.where(cur == m, eio, E), axis=0, keepdims=True)
        hit = eio == idx
        picked = jnp.where(hit, 1.0, picked)
        cur = jnp.where(hit, BELOW_NEG_INF, cur)
        idxs.append(idx)
        vals.append(jnp.sum(jnp.where(hit, scores, 0.0), axis=0, keepdims=True))
    total = vals[0]
    for v in vals[1:]:
        total = total + v

    pos = _mm(picked.astype(BF16), triu_ref[...]) + carry_ref[...]
    for k in range(TOP_K):
        eidx_ref[k:k + 1, :] = idxs[k]
        gw_ref[k:k + 1, :] = vals[k] / total * ROUTED_SCALE
        rank_ref[k:k + 1, :] = jnp.sum(jnp.where(eio == idxs[k], pos, 0.0), axis=0, keepdims=True).astype(I32)
    carry_ref[...] = carry_ref[...] + jnp.sum(picked, axis=1, keepdims=True)
    cnt_ref[...] = carry_ref[...].astype(I32)


def _router(x, router_w, router_bias, *, tm=512):
    T, D = x.shape
    E = N_EXPERTS
    triu = jnp.triu(jnp.ones((tm, tm), F32), k=1).astype(BF16)
    col = lambda i: (0, i)
    fixed = lambda i: (0, 0)
    return pl.pallas_call(
        _router_kernel,
        name="moe_router",
        grid=(T // tm,),
        in_specs=[pl.BlockSpec((tm, D), lambda i: (i, 0)), pl.BlockSpec((E, D), fixed),
                  pl.BlockSpec((E, 1), fixed), pl.BlockSpec((tm, tm), fixed)],
        out_specs=[pl.BlockSpec((TOP_K, tm), col), pl.BlockSpec((TOP_K, tm), col), pl.BlockSpec((TOP_K, tm), col),
                   pl.BlockSpec((E, 1), fixed), pl.BlockSpec((tm, ROW_WORDS), lambda i: (i, 0))],
        out_shape=[jax.ShapeDtypeStruct((TOP_K, T), I32), jax.ShapeDtypeStruct((TOP_K, T), F32),
                   jax.ShapeDtypeStruct((TOP_K, T), I32), jax.ShapeDtypeStruct((E, 1), I32),
                   jax.ShapeDtypeStruct((T, ROW_WORDS), U32)],
        scratch_shapes=[pltpu.VMEM((E, 1), F32)],
        compiler_params=_cparams("arbitrary"),
    )(x, router_w.T.astype(F32), router_bias.reshape(E, 1).astype(F32), triu)


def _tile_indices(dest_hbm, idx_ref, sem_idx, n_idx):
    i = pl.program_id(0)

    def idx_copy(step):
        slot = step % 2
        return pltpu.make_async_copy(dest_hbm.at[step], idx_ref.at[pl.ds(slot * n_idx, n_idx)], sem_idx.at[slot])

    @pl.when(i == 0)
    def _():
        idx_copy(0).start()

    idx_copy(i).wait()

    @pl.when(i + 1 < pl.num_programs(0))
    def _():
        idx_copy(i + 1).start()

    return (i % 2) * n_idx


def _dispatch_kernel(cnt_ref, pstart_ref, dest_hbm, x_ref, rows_ref, idx_ref, zero_ref, sem_idx, sem_rows, sem_zero):
    TD = x_ref.shape[0]

    @pl.when(pl.program_id(0) == 0)
    def _():
        zero_ref[...] = jnp.zeros_like(zero_ref)

        def per_expert(e, c):
            n = cnt_ref[e]
            base = pstart_ref[e]
            pad_to = (n + EXP_BLK - 1) // EXP_BLK * EXP_BLK
            fill = lambda r: pltpu.make_async_copy(zero_ref.at[pl.ds(0, 1)], rows_ref.at[base + r], sem_zero)

            def start(r, c2):
                fill(r).start()
                return c2

            def wait(r, c2):
                fill(r).wait()
                return c2

            lax.fori_loop(n, pad_to, start, 0)
            lax.fori_loop(n, pad_to, wait, 0)
            return c

        lax.fori_loop(0, N_EXPERTS, per_expert, 0)

    base = _tile_indices(dest_hbm, idx_ref, sem_idx, TOP_K * TD)
    scatter = lambda t, k: pltpu.make_async_copy(x_ref.at[pl.ds(t, 1)], rows_ref.at[idx_ref[base + k * TD + t]], sem_rows)

    def issue(t, c):
        for k in range(TOP_K):
            scatter(t, k).start()
        return c

    def drain(t, c):
        for k in range(TOP_K):
            scatter(t, k).wait()
        return c

    lax.fori_loop(0, TD, issue, 0)
    lax.fori_loop(0, TD, drain, 0)


def _dispatch(xp, dest_tiles, counts, pad_start, n_rows, *, td):
    T = xp.shape[0]
    return pl.pallas_call(
        _dispatch_kernel,
        name="moe_dispatch",
        grid_spec=pltpu.PrefetchScalarGridSpec(
            num_scalar_prefetch=2,
            grid=(T // td,),
            in_specs=[pl.BlockSpec(memory_space=pl.ANY), pl.BlockSpec((td, ROW_WORDS), lambda i, c, p: (i, 0))],
            out_specs=pl.BlockSpec(memory_space=pl.ANY),
            scratch_shapes=[pltpu.SMEM((2 * TOP_K * td,), I32), pltpu.VMEM((8, ROW_WORDS), U32),
                            pltpu.SemaphoreType.DMA((2,)), pltpu.SemaphoreType.DMA(()), pltpu.SemaphoreType.DMA(())],
        ),
        out_shape=jax.ShapeDtypeStruct((n_rows, 1, ROW_WORDS), U32),
        compiler_params=_cparams("arbitrary"),
    )(counts, pad_start, dest_tiles, xp)


def _expert_kernel(be_ref, nu_ref, rows_hbm, wgu_ref, wd_ref, out_hbm, xbuf, obuf, sem_in, sem_out):
    i = pl.program_id(0)
    nu = nu_ref[0]
    blk = lambda step: pl.ds(pl.multiple_of(step * EXP_BLK, EXP_BLK), EXP_BLK)
    in_copy = lambda step: pltpu.make_async_copy(rows_hbm.at[blk(step), 0], xbuf.at[step % 2], sem_in.at[step % 2])
    out_copy = lambda step: pltpu.make_async_copy(obuf.at[step % 2], out_hbm.at[blk(step), 0], sem_out.at[step % 2])

    @pl.when(i == 0)
    def _():
        in_copy(0).start()

    @pl.when(i < nu)
    def _():
        in_copy(i).wait()

        @pl.when(i + 1 < nu)
        def _():
            in_copy(i + 1).start()

        @pl.when(i >= 2)
        def _():
            out_copy(i - 2).wait()

        slot = i % 2
        gu = _mm(_unpack_halves(xbuf[slot]).astype(BF16), wgu_ref[0])
        h = _silu(gu[:, :EXPERT_FF]) * gu[:, EXPERT_FF:]
        obuf[slot] = _pack_halves(_mm(h.astype(BF16), wd_ref[0]))
        out_copy(i).start()

    @pl.when(i == pl.num_programs(0) - 1)
    def _():
        @pl.when(nu >= 2)
        def _():
            out_copy(nu - 2).wait()

        out_copy(nu - 1).wait()


def _experts(rows, blk_expert, n_used, wgu, wd):
    R = rows.shape[0]
    D = wd.shape[-1]
    n_blk = R // EXP_BLK
    live = lambda i, be, nu: jnp.minimum(i, nu[0] - 1)
    return pl.pallas_call(
        _expert_kernel,
        name="moe_experts",
        grid_spec=pltpu.PrefetchScalarGridSpec(
            num_scalar_prefetch=2,
            grid=(n_blk,),
            in_specs=[pl.BlockSpec(memory_space=pl.ANY),
                      pl.BlockSpec((1, D, 2 * EXPERT_FF), lambda i, be, nu: (be[live(i, be, nu)], 0, 0)),
                      pl.BlockSpec((1, EXPERT_FF, D), lambda i, be, nu: (be[live(i, be, nu)], 0, 0))],
            out_specs=pl.BlockSpec(memory_space=pl.ANY),
            scratch_shapes=[pltpu.VMEM((2, EXP_BLK, ROW_WORDS), U32), pltpu.VMEM((2, EXP_BLK, ROW_WORDS), U32),
                            pltpu.SemaphoreType.DMA((2,)), pltpu.SemaphoreType.DMA((2,))],
        ),
        out_shape=jax.ShapeDtypeStruct((R, 1, ROW_WORDS), U32),
        compiler_params=_cparams("arbitrary"),
    )(blk_expert, n_used, rows, wgu, wd)


def _combine_kernel(dest_hbm, gw_ref, x_ref, rows_ref, wsgu_ref, wsd_ref, g_ref, b_ref, p_ref, wp_ref, wpg_ref,
                    o_ref, idx_ref, buf_ref, sem_idx, sem_rows):
    TM = x_ref.shape[0]
    base = _tile_indices(dest_hbm, idx_ref, sem_idx, TOP_K * TM)
    gather = lambda t, k: pltpu.make_async_copy(rows_ref.at[idx_ref[base + k * TM + t]], buf_ref.at[k, pl.ds(t, 1)], sem_rows)

    def issue(t, c):
        for k in range(TOP_K):
            gather(t, k).start()
        return c

    def drain(t, c):
        for k in range(TOP_K):
            gather(t, k).wait()
        return c

    lax.fori_loop(0, TM, issue, 0)
    x = x_ref[...]
    gu = _mm(x.astype(BF16), wsgu_ref[...])
    ff = gu.shape[1] // 2
    f = _mm((_silu(gu[:, :ff]) * gu[:, ff:]).astype(BF16), wsd_ref[...])
    lax.fori_loop(0, TM, drain, 0)
    gw = gw_ref[...]
    for k in range(TOP_K):
        f = f + gw[:, k:k + 1] * _unpack_halves(buf_ref[k])
    x2 = _ln_rows(ALPHA * x + f, g_ref[...], b_ref[...])
    gate = _sigmoid(_mm(x2.astype(BF16), wpg_ref[...]))
    o_ref[...] = x2 + gate * _mm(p_ref[...].astype(BF16), wp_ref[...])


def _combine(dest_tiles, gw, x, rows_out, wsgu, wsd, g, b, p, wp, wpg, *, tm):
    T, D = x.shape
    row = lambda i: (i, 0)
    fixed = lambda i: (0, 0)
    full = lambda a: pl.BlockSpec(a.shape, fixed)
    return pl.pallas_call(
        _combine_kernel,
        name="moe_combine",
        grid=(T // tm,),
        in_specs=[pl.BlockSpec(memory_space=pl.ANY), pl.BlockSpec((tm, TOP_K), row), pl.BlockSpec((tm, D), row),
                  pl.BlockSpec(memory_space=pl.ANY), full(wsgu), full(wsd), pl.BlockSpec((1, D), fixed),
                  pl.BlockSpec((1, D), fixed), pl.BlockSpec((tm, PLE_DIM), row), full(wp), full(wpg)],
        out_specs=pl.BlockSpec((tm, D), row),
        out_shape=jax.ShapeDtypeStruct((T, D), F32),
        scratch_shapes=[pltpu.SMEM((2 * TOP_K * tm,), I32), pltpu.VMEM((TOP_K, tm, ROW_WORDS), U32),
                        pltpu.SemaphoreType.DMA((2,)), pltpu.SemaphoreType.DMA(())],
        compiler_params=_cparams("arbitrary"),
    )(dest_tiles, gw, x, rows_out, wsgu, wsd, g.reshape(1, D), b.reshape(1, D), p, wp, wpg)


MOE_TILE = 256


def _moe_ple_layer(x, p, router_w, router_bias, w_gate, w_up, w_down, ws_gate, ws_up, ws_down, g, b, ple_w, ple_gate_w):
    T, D = x.shape
    eidx, gw, rank, counts, xp = _router(x, router_w, router_bias)
    counts = counts.reshape(N_EXPERTS)
    padded = (counts + EXP_BLK - 1) // EXP_BLK * EXP_BLK
    pad_end = jnp.cumsum(padded)
    pad_start = pad_end - padded
    n_blk = T * TOP_K // EXP_BLK + N_EXPERTS
    e_iota = jnp.arange(N_EXPERTS, dtype=I32)
    dest = rank + jnp.sum(jnp.where(eidx[..., None] == e_iota, pad_start, 0), axis=-1)
    tm = MOE_TILE
    dest_tiles = dest.reshape(TOP_K, T // tm, tm).transpose(1, 0, 2).reshape(T // tm, TOP_K * tm)
    blk_first = jnp.arange(n_blk, dtype=I32)[:, None] * EXP_BLK
    blk_expert = jnp.minimum(jnp.sum((pad_end[None, :] <= blk_first).astype(I32), axis=-1), N_EXPERTS - 1)
    n_used = (pad_end[-1:] // EXP_BLK).astype(I32)

    rows = _dispatch(xp, dest_tiles, counts.astype(I32), pad_start.astype(I32), n_blk * EXP_BLK, td=tm)
    wgu = jnp.concatenate([w_gate, w_up], axis=-1).astype(BF16)
    rows_out = _experts(rows, blk_expert, n_used, wgu, w_down.astype(BF16))
    wsgu = jnp.concatenate([ws_gate, ws_up], axis=-1).astype(BF16)
    return _combine(dest_tiles, gw.T, x, rows_out, wsgu, ws_down.astype(BF16), g, b, p,
                    ple_w.astype(BF16), ple_gate_w.astype(BF16), tm=tm)


def kernel(x, p, ln_g, ln_b, mlstm_w_in, mlstm_conv, mlstm_ig_bias, mlstm_fg_bias, mlstm_norm_g, mlstm_w_out, nsa_w_in, nsa_cmp_pos_k, nsa_cmp_pos_v, nsa_cmp_wk, nsa_cmp_wv, nsa_w_out, dil_w_in, dil_w_out, pool_w, pool_scale, router_w, router_bias, exp_w_gate, exp_w_up, exp_w_down, sh_w_gate, sh_w_up, sh_w_down, ple_w, ple_gate_w):
    B, S, D = x.shape
    T = B * S
    xf = x.reshape(T, D)
    pf = p.reshape(DEPTH, T, PLE_DIM)
    for i in range(DEPTH):
        kind, j = i % 4, i // 4
        g1, b1 = ln_g[i, 0], ln_b[i, 0]
        if kind == 0:
            y = _mlstm_mixer(xf, B, S, mlstm_w_in[j], mlstm_conv[j], mlstm_ig_bias[j], mlstm_fg_bias[j], mlstm_norm_g[j])
            xf = _outproj_ln(y, mlstm_w_out[j].astype(BF16), xf, g1, b1)
        elif kind == 1:
            y = _nsa_mixer(xf, B, S, nsa_w_in[j], nsa_cmp_pos_k[j], nsa_cmp_pos_v[j], nsa_cmp_wk[j], nsa_cmp_wv[j])
            xf = _outproj_ln(y, nsa_w_out[j].astype(BF16), xf, g1, b1)
        elif kind == 2:
            xf = _dilated_layer(xf, B, S, dil_w_in[j], dil_w_out[j], g1, b1)
        else:
            xf = _pool_layer(xf, B, S, pool_w[j], pool_scale[j], g1, b1)
        xf = _moe_ple_layer(xf, pf[i], router_w[i], router_bias[i], exp_w_gate[i], exp_w_up[i], exp_w_down[i],
                            sh_w_gate[i], sh_w_up[i], sh_w_down[i], ln_g[i, 1], ln_b[i, 1], ple_w[i], ple_gate_w[i])
    return xf.reshape(B, S, D)
```

```python
import functools

import numpy as np
import jax
import jax.numpy as jnp
from jax import lax
from jax.experimental import pallas as pl
from jax.experimental.pallas import tpu as pltpu

F32 = jnp.float32
BF16 = jnp.bfloat16
I32 = jnp.int32
HIGHEST = lax.Precision.HIGHEST

D_MODEL = 1024
DEPTH = 4
ALPHA = (2.0 * DEPTH) ** 0.25
LN_EPS = 1e-5
NEG_INF = -1e30
TINY = 1e-30
BELOW_NEG_INF = -3e38
ROPE_THETA = 10000.0
HEAD_DIM = 64
ROPE_HALF = HEAD_DIM // 2

MLSTM_HEADS = 8
MLSTM_QK_DIM = 64
MLSTM_V_DIM = 128
MLSTM_CONV = 4
MLSTM_L = 256

NSA_HEADS = 16
NSA_KV_HEADS = 4
NSA_J = NSA_HEADS // NSA_KV_HEADS
CMP_STRIDE = 16
CMP_BLK = 32
SEL_BLK = 64
N_SEL = 16
NSA_WINDOW = 512
FORCE_SCORE = 1e4
NSA_TQ = 128
NSA_TK = 512

DIL_HEADS = 16
DIL_GROUPS = ((128, 1), (512, 4), (2048, 16))
DIL_STEPS = 128

POOL_WINDOWS = (2, 4, 8, 16)
POOL_GROUP = 256
POOL_HALO = 16

N_EXPERTS = 64
TOP_K = 8
N_EXPERT_GROUPS = 8
TOPK_GROUPS = 4
EXPERT_FF = 256
ROUTED_SCALE = 2.5
EXP_BLK = 512
PLE_DIM = 256

VMEM_LIMIT = 48 * 1024 * 1024


def _cparams(*sem):
    return pltpu.CompilerParams(dimension_semantics=sem, vmem_limit_bytes=VMEM_LIMIT)


def _nt(a, b, **kw):
    return lax.dot_general(a, b, (((1,), (1,)), ((), ())), preferred_element_type=F32, **kw)


def _tn(a, b, **kw):
    return lax.dot_general(a, b, (((0,), (0,)), ((), ())), preferred_element_type=F32, **kw)


def _mm(a, b, **kw):
    return jnp.dot(a, b, preferred_element_type=F32, **kw)


def _sigmoid(z):
    return 1.0 / (1.0 + jnp.exp(-z))


def _silu(z):
    return z * _sigmoid(z)


def _ln_rows(z, g, b):
    mu = jnp.mean(z, axis=-1, keepdims=True)
    d = z - mu
    var = jnp.mean(d * d, axis=-1, keepdims=True)
    return d * lax.rsqrt(var + LN_EPS) * g + b


def _proj_kernel(*refs, rope):
    if rope:
        x_ref, w_ref, cos_ref, sin_ref, o_ref, xb_ref = refs
    else:
        x_ref, w_ref, o_ref, xb_ref = refs

    @pl.when(pl.program_id(1) == 0)
    def _():
        xb_ref[...] = x_ref[...].astype(BF16)

    acc = _mm(xb_ref[...], w_ref[...])
    if rope:
        tn = acc.shape[1]
        lane = lax.broadcasted_iota(I32, acc.shape, 1)
        lo = (lane % HEAD_DIM) < ROPE_HALF
        rot = jnp.where(lo, pltpu.roll(acc, tn - ROPE_HALF, 1), pltpu.roll(acc, ROPE_HALF, 1))
        acc = acc * cos_ref[...] + rot * sin_ref[...]
    o_ref[...] = acc.astype(o_ref.dtype)


def _proj(x, w, *, out_dtype, tm, tn, rope_tables=None, seq=None):
    T, K = x.shape
    N = w.shape[1]
    assert T % tm == 0 and N % tn == 0
    in_specs = [pl.BlockSpec((tm, K), lambda i, j: (i, 0)), pl.BlockSpec((K, tn), lambda i, j: (0, j))]
    args = [x, w]
    if rope_tables is not None:
        nseq = seq // tm
        in_specs += [pl.BlockSpec((tm, tn), lambda i, j: (i % nseq, 0))] * 2
        args += list(rope_tables)
    return pl.pallas_call(
        functools.partial(_proj_kernel, rope=rope_tables is not None),
        name="proj_rope" if rope_tables is not None else "proj",
        grid=(T // tm, N // tn),
        in_specs=in_specs,
        out_specs=pl.BlockSpec((tm, tn), lambda i, j: (i, j)),
        out_shape=jax.ShapeDtypeStruct((T, N), out_dtype),
        scratch_shapes=[pltpu.VMEM((tm, K), BF16)],
        compiler_params=_cparams("parallel", "arbitrary"),
    )(*args)


def _rope_tables(pos, width):
    inv = ROPE_THETA ** (-jnp.arange(ROPE_HALF, dtype=F32) / ROPE_HALF)
    ang = pos.astype(F32)[:, None] * inv[None, :]
    cos, sin = jnp.cos(ang), jnp.sin(ang)
    cos64 = jnp.concatenate([cos, cos], -1)
    sin64 = jnp.concatenate([-sin, sin], -1)
    rep = width // HEAD_DIM
    return jnp.tile(cos64, (1, rep)), jnp.tile(sin64, (1, rep))


def _outproj_ln_kernel(y_ref, w_ref, x_ref, g_ref, b_ref, o_ref):
    y = _mm(y_ref[...].astype(BF16), w_ref[...])
    o_ref[...] = _ln_rows(ALPHA * x_ref[...] + y, g_ref[...], b_ref[...])


def _outproj_ln(y, w, x, g, b, *, tm=256):
    T, D = x.shape
    K = y.shape[1]
    row = lambda i: (i, 0)
    fixed = lambda i: (0, 0)
    return pl.pallas_call(
        _outproj_ln_kernel,
        name="outproj_ln",
        grid=(T // tm,),
        in_specs=[pl.BlockSpec((tm, K), row), pl.BlockSpec((K, D), fixed), pl.BlockSpec((tm, D), row),
                  pl.BlockSpec((1, D), fixed), pl.BlockSpec((1, D), fixed)],
        out_specs=pl.BlockSpec((tm, D), row),
        out_shape=jax.ShapeDtypeStruct((T, D), F32),
        compiler_params=_cparams("parallel"),
    )(y, w, x, g.reshape(1, D), b.reshape(1, D))


def _log_sigmoid(z):
    return jnp.minimum(z, 0.0) - jnp.log(1.0 + jnp.exp(-jnp.abs(z)))


def _mlstm_kernel(qk_ref, v_ref, o_ref, gc_ref, gr_ref, convw_ref, bias_c_ref, bias_r_ref, ng_ref,
                  tri_ref, triT_ref, out_ref, C_ref, n_ref, m_ref, ext_ref):
    L = qk_ref.shape[0]
    H, dk, dv = MLSTM_HEADS, MLSTM_QK_DIM, MLSTM_V_DIM

    @pl.when(pl.program_id(1) == 0)
    def _():
        C_ref[...] = jnp.zeros_like(C_ref)
        n_ref[...] = jnp.zeros_like(n_ref)
        m_ref[...] = jnp.zeros_like(m_ref)
        ext_ref[0:8, :] = jnp.zeros((8, ext_ref.shape[1]), F32)

    cur = qk_ref[...]
    ext_ref[8:8 + L, :] = cur
    acc = convw_ref[3:4, :] * cur
    for j in range(MLSTM_CONV - 1):
        acc = acc + convw_ref[j:j + 1, :] * ext_ref[5 + j:5 + j + L, :]
    ext_ref[0:8, :] = cur[L - 8:L, :]
    qk = _silu(acc)

    gc = gc_ref[...] + bias_c_ref[...]
    gr = gr_ref[...] + bias_r_ref[...]
    b_col = _mm(tri_ref[...], _log_sigmoid(gc), precision=HIGHEST)
    b_row = _mm(_log_sigmoid(gr[H:2 * H, :]), triT_ref[...], precision=HIGHEST)
    ig_row = gr[0:H, :]
    tri = lax.broadcasted_iota(I32, (L, L), 0) >= lax.broadcasted_iota(I32, (L, L), 1)

    for h in range(H):
        q = (qk[:, h * dk:(h + 1) * dk] * dk ** -0.5).astype(BF16)
        k = qk[:, H * dk + h * dk:H * dk + (h + 1) * dk]
        kb = k.astype(BF16)
        v = v_ref[:, h * dv:(h + 1) * dv].astype(BF16)
        b_c = b_col[:, H + h:H + h + 1]
        ig_c = gc[:, h:h + 1]
        b_r = b_row[h:h + 1, :]
        m_prev = m_ref[h:h + 1, 0:1]
        C = C_ref[h]
        n = n_ref[h:h + 1, :]

        logD = jnp.where(tri, b_c - b_r + ig_row[h:h + 1, :], NEG_INF)
        inter = b_c + m_prev
        m_t = jnp.maximum(inter, jnp.max(logD, axis=-1, keepdims=True))
        s = _nt(q, kb) * jnp.exp(logD - m_t)
        w_inter = jnp.exp(inter - m_t)
        num = w_inter * _mm(q, C.astype(BF16)) + _mm(s.astype(BF16), v)
        den = w_inter * jnp.sum(q.astype(F32) * n, axis=-1, keepdims=True) + jnp.sum(s, axis=-1, keepdims=True)
        hv = num / jnp.maximum(jnp.abs(den), jnp.exp(-m_t))

        mu = jnp.mean(hv, axis=-1, keepdims=True)
        d = hv - mu
        hn = d * lax.rsqrt(jnp.mean(d * d, axis=-1, keepdims=True) + LN_EPS)
        og = _sigmoid(o_ref[:, h * dv:(h + 1) * dv])
        out_ref[:, h * dv:(h + 1) * dv] = (hn * ng_ref[:, h * dv:(h + 1) * dv] * og).astype(out_ref.dtype)

        bL = b_c[L - 1:L, :]
        logw = bL - b_c + ig_c
        m_new = jnp.maximum(bL + m_prev, jnp.max(logw, axis=0, keepdims=True))
        decay = jnp.exp(bL + m_prev - m_new)
        kw = k * jnp.exp(logw - m_new)
        C_ref[h] = decay * C + _tn(kw.astype(BF16), v)
        n_ref[h:h + 1, :] = decay * n + jnp.sum(kw, axis=0, keepdims=True)
        m_ref[h:h + 1, :] = jnp.broadcast_to(m_new, (1, m_ref.shape[1]))


def _mlstm_mixer(x, B, S, w_in, conv_w, ig_bias, fg_bias, norm_g):
    T = B * S
    H, L = MLSTM_HEADS, MLSTM_L
    wb = w_in.astype(BF16)
    main = _proj(x, wb[:, :3 * D_MODEL], out_dtype=F32, tm=512, tn=512)
    wg = jnp.pad(wb[:, 3 * D_MODEL:], ((0, 0), (0, 128 - 2 * H)))
    gates = _proj(x, wg, out_dtype=F32, tm=512, tn=128)
    gates_r = gates[:, :2 * H].reshape(B, S, 2 * H).transpose(0, 2, 1)
    bias16 = jnp.concatenate([ig_bias, fg_bias]).astype(F32)
    bias_c = jnp.pad(bias16, (0, 128 - 2 * H)).reshape(1, 128)
    bias_r = bias16.reshape(2 * H, 1)
    tri = jnp.tril(jnp.ones((L, L), F32))
    nc = S // L
    rowblk = lambda c: (lambda b, i: (b * nc + i, c))
    fixed = lambda b, i: (0, 0)
    return pl.pallas_call(
        _mlstm_kernel,
        name="mlstm",
        grid=(B, nc),
        in_specs=[pl.BlockSpec((L, D_MODEL), rowblk(0)), pl.BlockSpec((L, D_MODEL), rowblk(1)),
                  pl.BlockSpec((L, D_MODEL), rowblk(2)), pl.BlockSpec((L, 128), rowblk(0)),
                  pl.BlockSpec((None, 2 * H, L), lambda b, i: (b, 0, i)),
                  pl.BlockSpec((MLSTM_CONV, D_MODEL), fixed), pl.BlockSpec((1, 128), fixed),
                  pl.BlockSpec((2 * H, 1), fixed), pl.BlockSpec((1, D_MODEL), fixed),
                  pl.BlockSpec((L, L), fixed), pl.BlockSpec((L, L), fixed)],
        out_specs=pl.BlockSpec((L, D_MODEL), rowblk(0)),
        out_shape=jax.ShapeDtypeStruct((T, D_MODEL), BF16),
        scratch_shapes=[pltpu.VMEM((H, MLSTM_QK_DIM, MLSTM_V_DIM), F32), pltpu.VMEM((H, MLSTM_QK_DIM), F32),
                        pltpu.VMEM((H, 128), F32), pltpu.VMEM((L + 8, D_MODEL), F32)],
        compiler_params=_cparams("parallel", "arbitrary"),
    )(main, main, main, gates, gates_r, conv_w.astype(F32), bias_c, bias_r, norm_g.reshape(1, D_MODEL).astype(F32),
      tri, tri.T)


def _rot_cols(w):
    shp = w.shape
    w4 = w.reshape(shp[:-1] + (shp[-1] // HEAD_DIM, 2, ROPE_HALF))
    return jnp.flip(w4, axis=-2).reshape(shp)


def _nsa_compress_kernel(chk_ref, chv_ref, wk_lo_ref, wk_hi_ref, wv_lo_ref, wv_hi_ref, pk_ref, pv_ref, wk_ref, wv_ref,
                         cos_ref, sin_ref, kc_ref, vc_ref, sh_ref):
    NCH = chk_ref.shape[0]
    G = NSA_KV_HEADS
    sh_ref[NCH:NCH + 8, :] = jnp.zeros((8, sh_ref.shape[1]), F32)
    live = lax.broadcasted_iota(I32, (NCH, 1), 0) < NCH - 1

    def blocks(ch_ref, lo_ref, hi_ref, p_ref, w_ref):
        ch = ch_ref[...]
        n = lo_ref.shape[1]
        sh_ref[0:NCH, 0:n] = _mm(ch, hi_ref[...])
        const = _mm(p_ref[...].astype(BF16), w_ref[...])[0:1, :]
        return jnp.where(live, _mm(ch, lo_ref[...]) + sh_ref[1:NCH + 1, 0:n] + jnp.concatenate([const] * G, axis=1), 0.0)

    k2 = blocks(chk_ref, wk_lo_ref, wk_hi_ref, pk_ref, wk_ref)
    v2 = blocks(chv_ref, wv_lo_ref, wv_hi_ref, pv_ref, wv_ref)
    for g in range(G):
        raw = k2[:, 2 * g * HEAD_DIM:(2 * g + 1) * HEAD_DIM]
        rot = k2[:, (2 * g + 1) * HEAD_DIM:(2 * g + 2) * HEAD_DIM]
        kc_ref[g] = (raw * cos_ref[...] + rot * sin_ref[...]).astype(kc_ref.dtype)
        vc_ref[g] = v2[:, g * HEAD_DIM:(g + 1) * HEAD_DIM].astype(vc_ref.dtype)


def _nsa_cmp_select_kernel(q_ref, kc_ref, vc_ref, ovlT_ref, ocmp_ref, sel_ref):
    TQ = q_ref.shape[0]
    NCH = kc_ref.shape[0]
    NSB = ovlT_ref.shape[0]
    q0 = pl.program_id(2) * TQ
    t = q0 + lax.broadcasted_iota(I32, (TQ, 1), 0)
    cend = lax.broadcasted_iota(I32, (1, NCH), 1) * CMP_STRIDE + (CMP_BLK - 1)
    valid = cend <= t
    kc = kc_ref[...]
    vc = vc_ref[...]
    psum = jnp.zeros((TQ, NCH), F32)
    outs = []
    for j in range(NSA_J):
        qj = q_ref[:, j * HEAD_DIM:(j + 1) * HEAD_DIM]
        s = jnp.where(valid, _nt(qj, kc) * HEAD_DIM ** -0.5, NEG_INF)
        m = jnp.max(s, axis=-1, keepdims=True)
        e = jnp.where(valid, jnp.exp(s - m), 0.0)
        p = e / jnp.maximum(jnp.sum(e, axis=-1, keepdims=True), TINY)
        outs.append(_mm(p.astype(BF16), vc))
        psum = psum + p
    ocmp_ref[...] = jnp.concatenate(outs, axis=1).astype(ocmp_ref.dtype)

    imp = _nt(ovlT_ref[...], psum, precision=HIGHEST)
    nb = lax.broadcasted_iota(I32, (NSB, 1), 0)
    qblk = (q0 + lax.broadcasted_iota(I32, (1, TQ), 1)) // SEL_BLK
    forced = (nb == 0) | (nb == qblk) | (nb == qblk - 1)
    cur = jnp.where(forced, FORCE_SCORE, jnp.where(nb > qblk, NEG_INF, imp))
    sel = jnp.zeros((NSB, TQ), F32)
    for _ in range(min(N_SEL, NSB)):
        m = jnp.max(cur, axis=0, keepdims=True)
        idx = jnp.min(jnp.where(cur == m, nb, NSB), axis=0, keepdims=True)
        hit = nb == idx
        sel = jnp.where(hit, 1.0, sel)
        cur = jnp.where(hit, BELOW_NEG_INF, cur)
    sel_ref[...] = sel.astype(sel_ref.dtype)


NEG_FLOOR = -1e29


def _nsa_main_kernel(q_ref, ksT_ref, vs_ref, kwT_ref, vw_ref, sel_ref, ocmp_ref, gate_ref, gexp_ref, o_ref):
    TQ = q_ref.shape[0]
    S = vs_ref.shape[0]
    TK = min(NSA_TK, S)
    J = NSA_J
    q0 = pl.program_id(2) * TQ
    q4 = jnp.concatenate([q_ref[:, j * HEAD_DIM:(j + 1) * HEAD_DIM] for j in range(J)], axis=0) * HEAD_DIM ** -0.5
    t = q0 + lax.broadcasted_iota(I32, (TQ, 1), 0)
    selT = sel_ref[...]

    def attend(carry, kT, v, bias):
        m, l, acc = carry
        s = _mm(q4, kT) + jnp.concatenate([bias] * J, axis=0)
        m_new = jnp.maximum(m, jnp.max(s, axis=-1, keepdims=True))
        e = jnp.exp(s - m_new)
        a = jnp.exp(m - m_new)
        return m_new, a * l + jnp.sum(e, axis=-1, keepdims=True), a * acc + _mm(e.astype(BF16), v)

    init = (jnp.full((J * TQ, 1), NEG_FLOOR, F32), jnp.zeros((J * TQ, 1), F32), jnp.zeros((J * TQ, HEAD_DIM), F32))

    def body(c, carry):
        k0 = pl.multiple_of(c * TK, TK)
        kpos = k0 + lax.broadcasted_iota(I32, (1, TK), 1)
        expand = jnp.where(kpos // SEL_BLK == lax.broadcasted_iota(I32, (selT.shape[0], 1), 0), 1.0, 0.0).astype(BF16)
        bias = jnp.where((_tn(selT, expand) > 0.5) & (kpos <= t), 0.0, NEG_INF)
        return attend(carry, ksT_ref[:, pl.ds(k0, TK)], vs_ref[pl.ds(k0, TK), :], bias)

    nchunks = (q0 + TQ - 1) // TK + 1
    _, l, acc = lax.fori_loop(0, nchunks, body, init)
    o_slc = acc / jnp.maximum(l, TINY)

    span = min(NSA_WINDOW + TQ, S)
    start = pl.multiple_of(jnp.clip(q0 - NSA_WINDOW, 0, S - span), TQ)
    dist = t - (start + lax.broadcasted_iota(I32, (1, span), 1))
    bias = jnp.where((dist >= 0) & (dist < NSA_WINDOW), 0.0, NEG_INF)
    _, l, acc = attend(init, kwT_ref[:, pl.ds(start, span)], vw_ref[pl.ds(start, span), :], bias)
    o_win = acc / jnp.maximum(l, TINY)

    unstack = lambda o: jnp.concatenate([o[j * TQ:(j + 1) * TQ, :] for j in range(J)], axis=1)
    g = _sigmoid(gate_ref[...])
    gx = [_mm(g, gexp_ref[c], precision=HIGHEST) for c in range(3)]
    out = gx[0] * ocmp_ref[...].astype(F32) + gx[1] * unstack(o_slc) + gx[2] * unstack(o_win)
    o_ref[...] = out.astype(o_ref.dtype)


def _nsa_mixer(x, B, S, w_in, cmp_pos_k, cmp_pos_v, cmp_wk, cmp_wv):
    T = B * S
    H, G, J, dh = NSA_HEADS, NSA_KV_HEADS, NSA_J, HEAD_DIM
    kv = G * dh
    wb = w_in.astype(BF16)
    col = lambda a, n: wb[:, a:a + n]
    o_q, o_kc, o_vc, o_ks, o_vs, o_kw, o_vw, o_g = np.cumsum([0, H * dh] + [kv] * 6).tolist()
    pos = jnp.arange(S)
    w_rope = jnp.concatenate([col(o_q, H * dh), col(o_ks, kv), col(o_kw, kv)], axis=1)
    roped = _proj(x, w_rope, out_dtype=BF16, tm=512, tn=512, rope_tables=_rope_tables(pos, 512), seq=S)
    w_plain = jnp.concatenate([col(o_vs, kv), col(o_vw, kv), jnp.pad(col(o_g, 3 * H), ((0, 0), (0, 128 - 3 * H)))], axis=1)
    plain = _proj(x, w_plain, out_dtype=F32, tm=512, tn=128)

    heads_T = lambda a: a.reshape(B, S, G, dh).transpose(0, 2, 3, 1)
    heads = lambda a: a.reshape(B, S, G, dh).transpose(0, 2, 1, 3)
    ksT = heads_T(roped[:, H * dh:H * dh + kv])
    kwT = heads_T(roped[:, H * dh + kv:])
    vs = heads(plain[:, :kv]).astype(BF16)
    vw = heads(plain[:, kv:2 * kv]).astype(BF16)
    gates = plain[:, 2 * kv:]

    nch = S // CMP_STRIDE
    cw = CMP_STRIDE * kv
    chk = _proj(x, col(o_kc, kv), out_dtype=BF16, tm=512, tn=kv).reshape(B, nch, cw)
    chv = _proj(x, col(o_vc, kv), out_dtype=BF16, tm=512, tn=kv).reshape(B, nch, cw)
    wk = cmp_wk.reshape(CMP_BLK * dh, dh)
    wk2 = jnp.concatenate([wk, _rot_cols(wk)], axis=1).astype(BF16)
    wv2 = cmp_wv.reshape(CMP_BLK * dh, dh).astype(BF16)
    eye = jnp.eye(G, dtype=BF16)

    def per_head(w, part):
        n = w.shape[1]
        w3 = w.reshape(2, CMP_STRIDE, dh, n)[part]
        return jnp.einsum('lde,gh->lgdhe', w3, eye).reshape(cw, G * n)

    flat8 = lambda p_: jnp.pad(p_.reshape(1, CMP_BLK * dh), ((0, 7), (0, 0))).astype(F32)
    cend = jnp.arange(nch) * CMP_STRIDE + CMP_BLK - 1
    cos_c, sin_c = _rope_tables(cend, dh)
    fixed1 = lambda b: (0, 0)
    full1 = lambda a: pl.BlockSpec(a.shape, fixed1)
    consts = (per_head(wk2, 0), per_head(wk2, 1), per_head(wv2, 0), per_head(wv2, 1), flat8(cmp_pos_k), flat8(cmp_pos_v),
              wk2, wv2, cos_c, sin_c)
    kc, vc = pl.pallas_call(
        _nsa_compress_kernel,
        name="nsa_compress",
        grid=(B,),
        in_specs=[pl.BlockSpec((None, nch, cw), lambda b: (b, 0, 0))] * 2 + [full1(c) for c in consts],
        out_specs=[pl.BlockSpec((None, G, nch, dh), lambda b: (b, 0, 0, 0))] * 2,
        out_shape=[jax.ShapeDtypeStruct((B, G, nch, dh), BF16)] * 2,
        scratch_shapes=[pltpu.VMEM((nch + 8, 2 * kv), F32)],
        compiler_params=_cparams("parallel"),
    )(chk, chv, *consts)

    nsb = S // SEL_BLK
    c_idx, s_idx = np.arange(nch), np.arange(nsb)
    ovl = ((c_idx[:, None] * CMP_STRIDE + CMP_BLK - 1 >= s_idx[None, :] * SEL_BLK)
           & (c_idx[:, None] * CMP_STRIDE < (s_idx[None, :] + 1) * SEL_BLK)).astype(np.float32)
    TQ = NSA_TQ
    nq = S // TQ
    qblk = lambda b, g, i: (b * nq + i, g)
    bgi = lambda b, g, i: (b, g, 0, 0)
    ocmp, sel = pl.pallas_call(
        _nsa_cmp_select_kernel,
        name="nsa_cmp_select",
        grid=(B, G, nq),
        in_specs=[pl.BlockSpec((TQ, J * dh), qblk), pl.BlockSpec((None, None, nch, dh), bgi),
                  pl.BlockSpec((None, None, nch, dh), bgi), pl.BlockSpec((nsb, nch), lambda b, g, i: (0, 0))],
        out_specs=[pl.BlockSpec((TQ, J * dh), qblk), pl.BlockSpec((None, None, nsb, TQ), lambda b, g, i: (b, g, 0, i))],
        out_shape=[jax.ShapeDtypeStruct((T, H * dh), BF16), jax.ShapeDtypeStruct((B, G, nsb, S), BF16)],
        compiler_params=_cparams("parallel", "parallel", "parallel"),
    )(roped, kc, vc, jnp.asarray(ovl.T))

    gexp = np.zeros((G, 3, 128, J * dh), np.float32)
    for g in range(G):
        for c in range(3):
            for j in range(J):
                gexp[g, c, (g * J + j) * 3 + c, j * dh:(j + 1) * dh] = 1.0
    return pl.pallas_call(
        _nsa_main_kernel,
        name="nsa_main",
        grid=(B, G, nq),
        in_specs=[pl.BlockSpec((TQ, J * dh), qblk)] + [pl.BlockSpec((None, None, dh, S), bgi),
                                                       pl.BlockSpec((None, None, S, dh), bgi)] * 2 + [
            pl.BlockSpec((None, None, nsb, TQ), lambda b, g, i: (b, g, 0, i)),
            pl.BlockSpec((TQ, J * dh), qblk), pl.BlockSpec((TQ, 128), lambda b, g, i: (b * nq + i, 0)),
            pl.BlockSpec((None, 3, 128, J * dh), lambda b, g, i: (g, 0, 0, 0))],
        out_specs=pl.BlockSpec((TQ, J * dh), qblk),
        out_shape=jax.ShapeDtypeStruct((T, H * dh), BF16),
        compiler_params=_cparams("parallel", "parallel", "arbitrary"),
    )(roped, ksT, vs, kwT, vw, sel, ocmp, gates, jnp.asarray(gexp))


def _dil_kernel(q_ref, kp_ref, kc_ref, vp_ref, vc_ref, o_ref, lse_ref):
    NQ = q_ref.shape[0]
    prev_from = jnp.where(pl.program_id(2) == 0, NQ, 0)
    qi = lax.broadcasted_iota(I32, (NQ, 1), 0)
    kj = lax.broadcasted_iota(I32, (1, 2 * NQ), 1)
    dist = NQ + qi - kj
    valid = (dist >= 0) & (dist <= NQ) & (kj >= prev_from)
    lane = lax.broadcasted_iota(I32, (NQ, 128), 1)
    lse_all = jnp.zeros((NQ, 128), F32)
    for h in range(DIL_HEADS):
        sl = slice(h * HEAD_DIM, (h + 1) * HEAD_DIM)
        kb = jnp.concatenate([kp_ref[:, sl], kc_ref[:, sl]], axis=0)
        vb = jnp.concatenate([vp_ref[:, sl], vc_ref[:, sl]], axis=0)
        s = jnp.where(valid, _nt(q_ref[:, sl], kb) * HEAD_DIM ** -0.5, NEG_INF)
        m = jnp.max(s, axis=-1, keepdims=True)
        e = jnp.exp(s - m)
        den = jnp.sum(e, axis=-1, keepdims=True)
        o_ref[:, sl] = _mm((e / den).astype(BF16), vb).astype(o_ref.dtype)
        lse_all = jnp.where(lane == h, m + jnp.log(den), lse_all)
    lse_ref[...] = lse_all


def _dil_group(qk, v, B, S, dil):
    T = B * S
    NQ = DIL_STEPS
    U = S // dil
    nb = U // NQ
    W = D_MODEL
    qk_view = qk.reshape(B, U, dil * 2 * W)
    v_view = v.reshape(B, U, dil * W)
    cur = lambda ncol, part: (lambda b, r, n: (b, n, r * ncol + part))
    prev = lambda ncol, part: (lambda b, r, n: (b, jnp.maximum(n - 1, 0), r * ncol + part))
    blk = lambda f: pl.BlockSpec((None, NQ, W), f)
    o, lse = pl.pallas_call(
        _dil_kernel,
        name=f"dilated_attn_{dil}",
        grid=(B, dil, nb),
        in_specs=[blk(cur(2, 0)), blk(prev(2, 1)), blk(cur(2, 1)), blk(prev(1, 0)), blk(cur(1, 0))],
        out_specs=[pl.BlockSpec((None, NQ, W), lambda b, r, n: (b, n, r)),
                   pl.BlockSpec((None, NQ, 128), lambda b, r, n: (b, n, r))],
        out_shape=[jax.ShapeDtypeStruct((B, U, dil * W), BF16), jax.ShapeDtypeStruct((B, U, dil * 128), F32)],
        compiler_params=_cparams("parallel", "parallel", "arbitrary"),
    )(qk_view, qk_view, qk_view, v_view, v_view)
    return o.reshape(T, W), lse.reshape(T, 128)


def _dil_outproj_ln_kernel(o0_ref, o1_ref, o2_ref, l0_ref, l1_ref, l2_ref, hexp_ref, w_ref, x_ref, g_ref, b_ref, out_ref):
    ls = [l0_ref[...], l1_ref[...], l2_ref[...]]
    m = jnp.maximum(jnp.maximum(ls[0], ls[1]), ls[2])
    es = [jnp.exp(l - m) for l in ls]
    tot = es[0] + es[1] + es[2]
    y = jnp.zeros(o0_ref.shape, F32)
    for e, o_ref in zip(es, (o0_ref, o1_ref, o2_ref)):
        y = y + _mm(e / tot, hexp_ref[...], precision=HIGHEST) * o_ref[...].astype(F32)
    z = ALPHA * x_ref[...] + _mm(y.astype(BF16), w_ref[...])
    out_ref[...] = _ln_rows(z, g_ref[...], b_ref[...])


def _dilated_layer(x, B, S, w_in, w_out, g, b, *, tm=256):
    T, D = x.shape
    n_rope = 2 * len(DIL_GROUPS) * D
    wb = w_in.astype(BF16)
    roped = _proj(x, wb[:, :n_rope], out_dtype=BF16, tm=512, tn=512,
                  rope_tables=_rope_tables(jnp.arange(S), 512), seq=S)
    vals = _proj(x, wb[:, n_rope:], out_dtype=BF16, tm=512, tn=512)
    outs = [_dil_group(roped[:, 2 * gi * D:(2 * gi + 2) * D], vals, B, S, dil) for gi, (_, dil) in enumerate(DIL_GROUPS)]
    hexp = np.zeros((128, D), np.float32)
    for h in range(DIL_HEADS):
        hexp[h, h * HEAD_DIM:(h + 1) * HEAD_DIM] = 1.0
    row = lambda i: (i, 0)
    fixed = lambda i: (0, 0)
    return pl.pallas_call(
        _dil_outproj_ln_kernel,
        name="dilated_outproj_ln",
        grid=(T // tm,),
        in_specs=[pl.BlockSpec((tm, D), row)] * 3 + [pl.BlockSpec((tm, 128), row)] * 3
        + [pl.BlockSpec((128, D), fixed), pl.BlockSpec((D, D), fixed), pl.BlockSpec((tm, D), row),
           pl.BlockSpec((1, D), fixed), pl.BlockSpec((1, D), fixed)],
        out_specs=pl.BlockSpec((tm, D), row),
        out_shape=jax.ShapeDtypeStruct((T, D), F32),
        compiler_params=_cparams("parallel"),
    )(outs[0][0], outs[1][0], outs[2][0], outs[0][1], outs[1][1], outs[2][1], jnp.asarray(hexp),
      w_out.astype(BF16), x, g.reshape(1, D), b.reshape(1, D))


def _pool_ln_kernel(x_ref, halo_ref, w_ref, scale_ref, g_ref, b_ref, o_ref, ext_ref):
    TS = x_ref.shape[0]
    s = pl.program_id(1)
    x = x_ref[...]
    ext_ref[0:POOL_HALO, :] = jnp.where(s == 0, 0.0, halo_ref[...])
    ext_ref[POOL_HALO:POOL_HALO + TS, :] = x
    cnt = (s * TS + lax.broadcasted_iota(I32, (TS, 1), 0) + 1).astype(F32)
    ys = []
    for gi, w in enumerate(POOL_WINDOWS):
        sl = slice(gi * POOL_GROUP, (gi + 1) * POOL_GROUP)
        xg = x[:, sl]
        tot = xg
        for j in range(1, w):
            tot = tot + ext_ref[POOL_HALO - j:POOL_HALO - j + TS, sl]
        mean = tot / jnp.minimum(cnt, float(w))
        ys.append(_mm((mean - xg).astype(BF16), w_ref[gi]))
    y = jnp.concatenate(ys, axis=1) * scale_ref[...]
    o_ref[...] = _ln_rows(ALPHA * x + y, g_ref[...], b_ref[...])


def _pool_layer(x, B, S, w_grp, scale, g, b, *, ts=512):
    T, D = x.shape
    ns = S // ts
    hb = ts // POOL_HALO
    fixed = lambda bb, s: (0, 0)
    return pl.pallas_call(
        _pool_ln_kernel,
        name="pool_ln",
        grid=(B, ns),
        in_specs=[pl.BlockSpec((ts, D), lambda bb, s: (bb * ns + s, 0)),
                  pl.BlockSpec((POOL_HALO, D), lambda bb, s: (jnp.maximum((bb * ns + s) * hb - 1, 0), 0)),
                  pl.BlockSpec((len(POOL_WINDOWS), POOL_GROUP, POOL_GROUP), lambda bb, s: (0, 0, 0)),
                  pl.BlockSpec((1, D), fixed), pl.BlockSpec((1, D), fixed), pl.BlockSpec((1, D), fixed)],
        out_specs=pl.BlockSpec((ts, D), lambda bb, s: (bb * ns + s, 0)),
        out_shape=jax.ShapeDtypeStruct((T, D), F32),
        scratch_shapes=[pltpu.VMEM((ts + POOL_HALO, D), F32)],
        compiler_params=_cparams("parallel", "arbitrary"),
    )(x, x, w_grp.astype(BF16), scale.reshape(1, D), g.reshape(1, D), b.reshape(1, D))


U32 = jnp.uint32
HI16 = 0xFFFF0000
ROW_WORDS = D_MODEL // 2


def _pack_halves(v):
    h = v.shape[1] // 2
    bits = lambda a: lax.bitcast_convert_type(a.astype(BF16).astype(F32), U32)
    return (bits(v[:, :h]) >> 16) | (bits(v[:, h:]) & U32(HI16))


def _unpack_halves(w):
    lo = lax.bitcast_convert_type(w << 16, F32)
    hi = lax.bitcast_convert_type(w & U32(HI16), F32)
    return jnp.concatenate([lo, hi], axis=1)


def _router_kernel(x_ref, wT_ref, bias_ref, triu_ref, eidx_ref, gw_ref, rank_ref, cnt_ref, xp_ref, carry_ref):
    E = N_EXPERTS
    per = E // N_EXPERT_GROUPS
    TM = x_ref.shape[0]

    @pl.when(pl.program_id(0) == 0)
    def _():
        carry_ref[...] = jnp.zeros_like(carry_ref)

    x = x_ref[...]
    xp_ref[...] = _pack_halves(x)

    scores = _sigmoid(_nt(wT_ref[...], x, precision=HIGHEST))
    biased = scores + bias_ref[...]
    eio = lax.broadcasted_iota(I32, (E, TM), 0)

    gio = lax.broadcasted_iota(I32, (per, TM), 0)
    gscore = []
    for gidx in range(N_EXPERT_GROUPS):
        slab = biased[gidx * per:(gidx + 1) * per, :]
        m1 = jnp.max(slab, axis=0, keepdims=True)
        i1 = jnp.min(jnp.where(slab == m1, gio, per), axis=0, keepdims=True)
        m2 = jnp.max(jnp.where(gio == i1, BELOW_NEG_INF, slab), axis=0, keepdims=True)
        gscore.append(m1 + m2)
    slabs = []
    for gidx in range(N_EXPERT_GROUPS):
        beat = jnp.zeros((1, TM), F32)
        for o in range(N_EXPERT_GROUPS):
            if o == gidx:
                continue
            wins = (gscore[o] >= gscore[gidx]) if o < gidx else (gscore[o] > gscore[gidx])
            beat = beat + jnp.where(wins, 1.0, 0.0)
        keep = beat < float(TOPK_GROUPS)
        slabs.append(jnp.where(keep, biased[gidx * per:(gidx + 1) * per, :], NEG_INF))
    cur = jnp.concatenate(slabs, axis=0)

    picked = jnp.zeros((E, TM), F32)
    idxs, vals = [], []
    for _ in range(TOP_K):
        m = jnp.max(cur, axis=0, keepdims=True)
        idx = jnp.min(jnp.where(cur == m, eio, E), axis=0, keepdims=True)
        hit = eio == idx
        picked = jnp.where(hit, 1.0, picked)
        cur = jnp.where(hit, BELOW_NEG_INF, cur)
        idxs.append(idx)
        vals.append(jnp.sum(jnp.where(hit, scores, 0.0), axis=0, keepdims=True))
    total = vals[0]
    for v in vals[1:]:
        total = total + v

    pos = _mm(picked.astype(BF16), triu_ref[...]) + carry_ref[...]
    for k in range(TOP_K):
        eidx_ref[k:k + 1, :] = idxs[k]
        gw_ref[k:k + 1, :] = vals[k] / total * ROUTED_SCALE
        rank_ref[k:k + 1, :] = jnp.sum(jnp.where(eio == idxs[k], pos, 0.0), axis=0, keepdims=True).astype(I32)
    carry_ref[...] = carry_ref[...] + jnp.sum(picked, axis=1, keepdims=True)
    cnt_ref[...] = carry_ref[...].astype(I32)


def _router(x, router_w, router_bias, *, tm=512):
    T, D = x.shape
    E = N_EXPERTS
    triu = jnp.triu(jnp.ones((tm, tm), F32), k=1).astype(BF16)
    col = lambda i: (0, i)
    fixed = lambda i: (0, 0)
    return pl.pallas_call(
        _router_kernel,
        name="moe_router",
        grid=(T // tm,),
        in_specs=[pl.BlockSpec((tm, D), lambda i: (i, 0)), pl.BlockSpec((E, D), fixed),
                  pl.BlockSpec((E, 1), fixed), pl.BlockSpec((tm, tm), fixed)],
        out_specs=[pl.BlockSpec((TOP_K, tm), col), pl.BlockSpec((TOP_K, tm), col), pl.BlockSpec((TOP_K, tm), col),
                   pl.BlockSpec((E, 1), fixed), pl.BlockSpec((tm, ROW_WORDS), lambda i: (i, 0))],
        out_shape=[jax.ShapeDtypeStruct((TOP_K, T), I32), jax.ShapeDtypeStruct((TOP_K, T), F32),
                   jax.ShapeDtypeStruct((TOP_K, T), I32), jax.ShapeDtypeStruct((E, 1), I32),
                   jax.ShapeDtypeStruct((T, ROW_WORDS), U32)],
        scratch_shapes=[pltpu.VMEM((E, 1), F32)],
        compiler_params=_cparams("arbitrary"),
    )(x, router_w.T.astype(F32), router_bias.reshape(E, 1).astype(F32), triu)


def _tile_indices(dest_hbm, idx_ref, sem_idx, n_idx):
    i = pl.program_id(0)

    def idx_copy(step):
        slot = step % 2
        return pltpu.make_async_copy(dest_hbm.at[step], idx_ref.at[pl.ds(slot * n_idx, n_idx)], sem_idx.at[slot])

    @pl.when(i == 0)
    def _():
        idx_copy(0).start()

    idx_copy(i).wait()

    @pl.when(i + 1 < pl.num_programs(0))
    def _():
        idx_copy(i + 1).start()

    return (i % 2) * n_idx


def _dispatch_kernel(cnt_ref, pstart_ref, dest_hbm, x_ref, rows_ref, idx_ref, zero_ref, sem_idx, sem_rows, sem_zero):
    TD = x_ref.shape[0]

    @pl.when(pl.program_id(0) == 0)
    def _():
        zero_ref[...] = jnp.zeros_like(zero_ref)

        def per_expert(e, c):
            n = cnt_ref[e]
            base = pstart_ref[e]
            pad_to = (n + EXP_BLK - 1) // EXP_BLK * EXP_BLK
            fill = lambda r: pltpu.make_async_copy(zero_ref.at[pl.ds(0, 1)], rows_ref.at[base + r], sem_zero)

            def start(r, c2):
                fill(r).start()
                return c2

            def wait(r, c2):
                fill(r).wait()
                return c2

            lax.fori_loop(n, pad_to, start, 0)
            lax.fori_loop(n, pad_to, wait, 0)
            return c

        lax.fori_loop(0, N_EXPERTS, per_expert, 0)

    base = _tile_indices(dest_hbm, idx_ref, sem_idx, TOP_K * TD)
    scatter = lambda t, k: pltpu.make_async_copy(x_ref.at[pl.ds(t, 1)], rows_ref.at[idx_ref[base + k * TD + t]], sem_rows)

    def issue(t, c):
        for k in range(TOP_K):
            scatter(t, k).start(priority=k % 2)
        return c

    def drain(t, c):
        for k in range(TOP_K):
            scatter(t, k).wait()
        return c

    lax.fori_loop(0, TD, issue, 0)
    lax.fori_loop(0, TD, drain, 0)


def _dispatch(xp, dest_tiles, counts, pad_start, n_rows, *, td):
    T = xp.shape[0]
    return pl.pallas_call(
        _dispatch_kernel,
        name="moe_dispatch",
        grid_spec=pltpu.PrefetchScalarGridSpec(
            num_scalar_prefetch=2,
            grid=(T // td,),
            in_specs=[pl.BlockSpec(memory_space=pl.ANY), pl.BlockSpec((td, ROW_WORDS), lambda i, c, p: (i, 0))],
            out_specs=pl.BlockSpec(memory_space=pl.ANY),
            scratch_shapes=[pltpu.SMEM((2 * TOP_K * td,), I32), pltpu.VMEM((8, ROW_WORDS), U32),
                            pltpu.SemaphoreType.DMA((2,)), pltpu.SemaphoreType.DMA(()), pltpu.SemaphoreType.DMA(())],
        ),
        out_shape=jax.ShapeDtypeStruct((n_rows, 1, ROW_WORDS), U32),
        compiler_params=_cparams("arbitrary"),
    )(counts, pad_start, dest_tiles, xp)


def _expert_kernel(be_ref, nu_ref, rows_hbm, wgu_ref, wd_ref, out_hbm, xbuf, obuf, sem_in, sem_out):
    i = pl.program_id(0)
    nu = nu_ref[0]
    blk = lambda step: pl.ds(pl.multiple_of(step * EXP_BLK, EXP_BLK), EXP_BLK)
    in_copy = lambda step: pltpu.make_async_copy(rows_hbm.at[blk(step), 0], xbuf.at[step % 2], sem_in.at[step % 2])
    out_copy = lambda step: pltpu.make_async_copy(obuf.at[step % 2], out_hbm.at[blk(step), 0], sem_out.at[step % 2])

    @pl.when(i == 0)
    def _():
        in_copy(0).start()

    @pl.when(i < nu)
    def _():
        in_copy(i).wait()

        @pl.when(i + 1 < nu)
        def _():
            in_copy(i + 1).start()

        @pl.when(i >= 2)
        def _():
            out_copy(i - 2).wait()

        slot = i % 2
        gu = _mm(_unpack_halves(xbuf[slot]).astype(BF16), wgu_ref[0])
        h = _silu(gu[:, :EXPERT_FF]) * gu[:, EXPERT_FF:]
        obuf[slot] = _pack_halves(_mm(h.astype(BF16), wd_ref[0]))
        out_copy(i).start()

    @pl.when(i == pl.num_programs(0) - 1)
    def _():
        @pl.when(nu >= 2)
        def _():
            out_copy(nu - 2).wait()

        out_copy(nu - 1).wait()


def _experts(rows, blk_expert, n_used, wgu, wd):
    R = rows.shape[0]
    D = wd.shape[-1]
    n_blk = R // EXP_BLK
    live = lambda i, be, nu: jnp.minimum(i, nu[0] - 1)
    return pl.pallas_call(
        _expert_kernel,
        name="moe_experts",
        grid_spec=pltpu.PrefetchScalarGridSpec(
            num_scalar_prefetch=2,
            grid=(n_blk,),
            in_specs=[pl.BlockSpec(memory_space=pl.ANY),
                      pl.BlockSpec((1, D, 2 * EXPERT_FF), lambda i, be, nu: (be[live(i, be, nu)], 0, 0)),
                      pl.BlockSpec((1, EXPERT_FF, D), lambda i, be, nu: (be[live(i, be, nu)], 0, 0))],
            out_specs=pl.BlockSpec(memory_space=pl.ANY),
            scratch_shapes=[pltpu.VMEM((2, EXP_BLK, ROW_WORDS), U32), pltpu.VMEM((2, EXP_BLK, ROW_WORDS), U32),
                            pltpu.SemaphoreType.DMA((2,)), pltpu.SemaphoreType.DMA((2,))],
        ),
        out_shape=jax.ShapeDtypeStruct((R, 1, ROW_WORDS), U32),
        compiler_params=_cparams("arbitrary"),
    )(blk_expert, n_used, rows, wgu, wd)


def _combine_kernel(dest_hbm, gw_ref, x_ref, rows_ref, wsgu_ref, wsd_ref, g_ref, b_ref, p_ref, wp_ref, wpg_ref,
                    o_ref, idx_ref, buf_ref, sem_idx, sem_rows):
    TM = x_ref.shape[0]
    base = _tile_indices(dest_hbm, idx_ref, sem_idx, TOP_K * TM)
    gather = lambda t, k: pltpu.make_async_copy(rows_ref.at[idx_ref[base + k * TM + t]], buf_ref.at[k, pl.ds(t, 1)], sem_rows)

    def issue(t, c):
        for k in range(TOP_K):
            gather(t, k).start(priority=k % 2)
        return c

    def drain(t, c):
        for k in range(TOP_K):
            gather(t, k).wait()
        return c

    lax.fori_loop(0, TM, issue, 0)
    x = x_ref[...]
    gu = _mm(x.astype(BF16), wsgu_ref[...])
    ff = gu.shape[1] // 2
    f = _mm((_silu(gu[:, :ff]) * gu[:, ff:]).astype(BF16), wsd_ref[...])
    lax.fori_loop(0, TM, drain, 0)
    gw = gw_ref[...]
    for k in range(TOP_K):
        f = f + gw[:, k:k + 1] * _unpack_halves(buf_ref[k])
    x2 = _ln_rows(ALPHA * x + f, g_ref[...], b_ref[...])
    gate = _sigmoid(_mm(x2.astype(BF16), wpg_ref[...]))
    o_ref[...] = x2 + gate * _mm(p_ref[...].astype(BF16), wp_ref[...])


def _combine(dest_tiles, gw, x, rows_out, wsgu, wsd, g, b, p, wp, wpg, *, tm):
    T, D = x.shape
    row = lambda i: (i, 0)
    fixed = lambda i: (0, 0)
    full = lambda a: pl.BlockSpec(a.shape, fixed)
    return pl.pallas_call(
        _combine_kernel,
        name="moe_combine",
        grid=(T // tm,),
        in_specs=[pl.BlockSpec(memory_space=pl.ANY), pl.BlockSpec((tm, TOP_K), row), pl.BlockSpec((tm, D), row),
                  pl.BlockSpec(memory_space=pl.ANY), full(wsgu), full(wsd), pl.BlockSpec((1, D), fixed),
                  pl.BlockSpec((1, D), fixed), pl.BlockSpec((tm, PLE_DIM), row), full(wp), full(wpg)],
        out_specs=pl.BlockSpec((tm, D), row),
        out_shape=jax.ShapeDtypeStruct((T, D), F32),
        scratch_shapes=[pltpu.SMEM((2 * TOP_K * tm,), I32), pltpu.VMEM((TOP_K, tm, ROW_WORDS), U32),
                        pltpu.SemaphoreType.DMA((2,)), pltpu.SemaphoreType.DMA(())],
        compiler_params=_cparams("arbitrary"),
    )(dest_tiles, gw, x, rows_out, wsgu, wsd, g.reshape(1, D), b.reshape(1, D), p, wp, wpg)


MOE_TILE = 256


def _moe_ple_layer(x, p, router_w, router_bias, w_gate, w_up, w_down, ws_gate, ws_up, ws_down, g, b, ple_w, ple_gate_w):
    T, D = x.shape
    eidx, gw, rank, counts, xp = _router(x, router_w, router_bias)
    counts = counts.reshape(N_EXPERTS)
    padded = (counts + EXP_BLK - 1) // EXP_BLK * EXP_BLK
    pad_end = jnp.cumsum(padded)
    pad_start = pad_end - padded
    n_blk = T * TOP_K // EXP_BLK + N_EXPERTS
    e_iota = jnp.arange(N_EXPERTS, dtype=I32)
    dest = rank + jnp.sum(jnp.where(eidx[..., None] == e_iota, pad_start, 0), axis=-1)
    tm = MOE_TILE
    dest_tiles = dest.reshape(TOP_K, T // tm, tm).transpose(1, 0, 2).reshape(T // tm, TOP_K * tm)
    blk_first = jnp.arange(n_blk, dtype=I32)[:, None] * EXP_BLK
    blk_expert = jnp.minimum(jnp.sum((pad_end[None, :] <= blk_first).astype(I32), axis=-1), N_EXPERTS - 1)
    n_used = (pad_end[-1:] // EXP_BLK).astype(I32)

    rows = _dispatch(xp, dest_tiles, counts.astype(I32), pad_start.astype(I32), n_blk * EXP_BLK, td=tm)
    wgu = jnp.concatenate([w_gate, w_up], axis=-1).astype(BF16)
    rows_out = _experts(rows, blk_expert, n_used, wgu, w_down.astype(BF16))
    wsgu = jnp.concatenate([ws_gate, ws_up], axis=-1).astype(BF16)
    return _combine(dest_tiles, gw.T, x, rows_out, wsgu, ws_down.astype(BF16), g, b, p,
                    ple_w.astype(BF16), ple_gate_w.astype(BF16), tm=tm)


def kernel(x, p, ln_g, ln_b, mlstm_w_in, mlstm_conv, mlstm_ig_bias, mlstm_fg_bias, mlstm_norm_g, mlstm_w_out, nsa_w_in, nsa_cmp_pos_k, nsa_cmp_pos_v, nsa_cmp_wk, nsa_cmp_wv, nsa_w_out, dil_w_in, dil_w_out, pool_w, pool_scale, router_w, router_bias, exp_w_gate, exp_w_up, exp_w_down, sh_w_gate, sh_w_up, sh_w_down, ple_w, ple_gate_w):
    B, S, D = x.shape
    T = B * S
    xf = x.reshape(T, D)
    pf = p.reshape(DEPTH, T, PLE_DIM)
    for i in range(DEPTH):
        kind, j = i % 4, i // 4
        g1, b1 = ln_g[i, 0], ln_b[i, 0]
        if kind == 0:
            y = _mlstm_mixer(xf, B, S, mlstm_w_in[j], mlstm_conv[j], mlstm_ig_bias[j], mlstm_fg_bias[j], mlstm_norm_g[j])
            xf = _outproj_ln(y, mlstm_w_out[j].astype(BF16), xf, g1, b1)
        elif kind == 1:
            y = _nsa_mixer(xf, B, S, nsa_w_in[j], nsa_cmp_pos_k[j], nsa_cmp_pos_v[j], nsa_cmp_wk[j], nsa_cmp_wv[j])
            xf = _outproj_ln(y, nsa_w_out[j].astype(BF16), xf, g1, b1)
        elif kind == 2:
            xf = _dilated_layer(xf, B, S, dil_w_in[j], dil_w_out[j], g1, b1)
        else:
            xf = _pool_layer(xf, B, S, pool_w[j], pool_scale[j], g1, b1)
        xf = _moe_ple_layer(xf, pf[i], router_w[i], router_bias[i], exp_w_gate[i], exp_w_up[i], exp_w_down[i],
                            sh_w_gate[i], sh_w_up[i], sh_w_down[i], ln_g[i, 1], ln_b[i, 1], ple_w[i], ple_gate_w[i])
    return xf.reshape(B, S, D)
```

```python
import functools

import numpy as np
import jax
import jax.numpy as jnp
from jax import lax
from jax.experimental import pallas as pl
from jax.experimental.pallas import tpu as pltpu

F32 = jnp.float32
BF16 = jnp.bfloat16
I32 = jnp.int32
HIGHEST = lax.Precision.HIGHEST

D_MODEL = 1024
DEPTH = 4
ALPHA = (2.0 * DEPTH) ** 0.25
LN_EPS = 1e-5
NEG_INF = -1e30
TINY = 1e-30
BELOW_NEG_INF = -3e38
ROPE_THETA = 10000.0
HEAD_DIM = 64
ROPE_HALF = HEAD_DIM // 2

MLSTM_HEADS = 8
MLSTM_QK_DIM = 64
MLSTM_V_DIM = 128
MLSTM_CONV = 4
MLSTM_L = 256

NSA_HEADS = 16
NSA_KV_HEADS = 4
NSA_J = NSA_HEADS // NSA_KV_HEADS
CMP_STRIDE = 16
CMP_BLK = 32
SEL_BLK = 64
N_SEL = 16
NSA_WINDOW = 512
FORCE_SCORE = 1e4
NSA_TQ = 128
NSA_TK = 512

DIL_HEADS = 16
DIL_GROUPS = ((128, 1), (512, 4), (2048, 16))
DIL_STEPS = 128

POOL_WINDOWS = (2, 4, 8, 16)
POOL_GROUP = 256
POOL_HALO = 16

N_EXPERTS = 64
TOP_K = 8
N_EXPERT_GROUPS = 8
TOPK_GROUPS = 4
EXPERT_FF = 256
ROUTED_SCALE = 2.5
EXP_BLK = 512
PLE_DIM = 256

VMEM_LIMIT = 48 * 1024 * 1024


def _cparams(*sem):
    return pltpu.CompilerParams(dimension_semantics=sem, vmem_limit_bytes=VMEM_LIMIT)


def _nt(a, b, **kw):
    return lax.dot_general(a, b, (((1,), (1,)), ((), ())), preferred_element_type=F32, **kw)


def _tn(a, b, **kw):
    return lax.dot_general(a, b, (((0,), (0,)), ((), ())), preferred_element_type=F32, **kw)


def _mm(a, b, **kw):
    return jnp.dot(a, b, preferred_element_type=F32, **kw)


def _sigmoid(z):
    return 1.0 / (1.0 + jnp.exp(-z))


def _silu(z):
    return z * _sigmoid(z)


def _ln_rows(z, g, b):
    mu = jnp.mean(z, axis=-1, keepdims=True)
    d = z - mu
    var = jnp.mean(d * d, axis=-1, keepdims=True)
    return d * lax.rsqrt(var + LN_EPS) * g + b


def _proj_kernel(*refs, rope):
    if rope:
        x_ref, w_ref, cos_ref, sin_ref, o_ref, xb_ref = refs
    else:
        x_ref, w_ref, o_ref, xb_ref = refs

    @pl.when(pl.program_id(1) == 0)
    def _():
        xb_ref[...] = x_ref[...].astype(BF16)

    acc = _mm(xb_ref[...], w_ref[...])
    if rope:
        tn = acc.shape[1]
        lane = lax.broadcasted_iota(I32, acc.shape, 1)
        lo = (lane % HEAD_DIM) < ROPE_HALF
        rot = jnp.where(lo, pltpu.roll(acc, tn - ROPE_HALF, 1), pltpu.roll(acc, ROPE_HALF, 1))
        acc = acc * cos_ref[...] + rot * sin_ref[...]
    o_ref[...] = acc.astype(o_ref.dtype)


def _proj(x, w, *, out_dtype, tm, tn, rope_tables=None, seq=None):
    T, K = x.shape
    N = w.shape[1]
    assert T % tm == 0 and N % tn == 0
    in_specs = [pl.BlockSpec((tm, K), lambda i, j: (i, 0)), pl.BlockSpec((K, tn), lambda i, j: (0, j))]
    args = [x, w]
    if rope_tables is not None:
        nseq = seq // tm
        in_specs += [pl.BlockSpec((tm, tn), lambda i, j: (i % nseq, 0))] * 2
        args += list(rope_tables)
    return pl.pallas_call(
        functools.partial(_proj_kernel, rope=rope_tables is not None),
        name="proj_rope" if rope_tables is not None else "proj",
        grid=(T // tm, N // tn),
        in_specs=in_specs,
        out_specs=pl.BlockSpec((tm, tn), lambda i, j: (i, j)),
        out_shape=jax.ShapeDtypeStruct((T, N), out_dtype),
        scratch_shapes=[pltpu.VMEM((tm, K), BF16)],
        compiler_params=_cparams("parallel", "arbitrary"),
    )(*args)


def _rope_tables(pos, width):
    inv = ROPE_THETA ** (-jnp.arange(ROPE_HALF, dtype=F32) / ROPE_HALF)
    ang = pos.astype(F32)[:, None] * inv[None, :]
    cos, sin = jnp.cos(ang), jnp.sin(ang)
    cos64 = jnp.concatenate([cos, cos], -1)
    sin64 = jnp.concatenate([-sin, sin], -1)
    rep = width // HEAD_DIM
    return jnp.tile(cos64, (1, rep)), jnp.tile(sin64, (1, rep))


def _outproj_ln_kernel(y_ref, w_ref, x_ref, g_ref, b_ref, o_ref):
    y = _mm(y_ref[...].astype(BF16), w_ref[...])
    o_ref[...] = _ln_rows(ALPHA * x_ref[...] + y, g_ref[...], b_ref[...])


def _outproj_ln(y, w, x, g, b, *, tm=256):
    T, D = x.shape
    K = y.shape[1]
    row = lambda i: (i, 0)
    fixed = lambda i: (0, 0)
    return pl.pallas_call(
        _outproj_ln_kernel,
        name="outproj_ln",
        grid=(T // tm,),
        in_specs=[pl.BlockSpec((tm, K), row), pl.BlockSpec((K, D), fixed), pl.BlockSpec((tm, D), row),
                  pl.BlockSpec((1, D), fixed), pl.BlockSpec((1, D), fixed)],
        out_specs=pl.BlockSpec((tm, D), row),
        out_shape=jax.ShapeDtypeStruct((T, D), F32),
        compiler_params=_cparams("parallel"),
    )(y, w, x, g.reshape(1, D), b.reshape(1, D))


def _log_sigmoid(z):
    return jnp.minimum(z, 0.0) - jnp.log(1.0 + jnp.exp(-jnp.abs(z)))


def _mlstm_kernel(qk_ref, v_ref, o_ref, gc_ref, gr_ref, convw_ref, bias_c_ref, bias_r_ref, ng_ref,
                  tri_ref, triT_ref, out_ref, C_ref, n_ref, m_ref, ext_ref):
    L = qk_ref.shape[0]
    H, dk, dv = MLSTM_HEADS, MLSTM_QK_DIM, MLSTM_V_DIM

    @pl.when(pl.program_id(1) == 0)
    def _():
        C_ref[...] = jnp.zeros_like(C_ref)
        n_ref[...] = jnp.zeros_like(n_ref)
        m_ref[...] = jnp.zeros_like(m_ref)
        ext_ref[0:8, :] = jnp.zeros((8, ext_ref.shape[1]), F32)

    cur = qk_ref[...]
    ext_ref[8:8 + L, :] = cur
    acc = convw_ref[3:4, :] * cur
    for j in range(MLSTM_CONV - 1):
        acc = acc + convw_ref[j:j + 1, :] * ext_ref[5 + j:5 + j + L, :]
    ext_ref[0:8, :] = cur[L - 8:L, :]
    qk = _silu(acc)

    gc = gc_ref[...] + bias_c_ref[...]
    gr = gr_ref[...] + bias_r_ref[...]
    b_col = _mm(tri_ref[...], _log_sigmoid(gc), precision=HIGHEST)
    b_row = _mm(_log_sigmoid(gr[H:2 * H, :]), triT_ref[...], precision=HIGHEST)
    ig_row = gr[0:H, :]
    tri = lax.broadcasted_iota(I32, (L, L), 0) >= lax.broadcasted_iota(I32, (L, L), 1)

    for h in range(H):
        q = (qk[:, h * dk:(h + 1) * dk] * dk ** -0.5).astype(BF16)
        k = qk[:, H * dk + h * dk:H * dk + (h + 1) * dk]
        kb = k.astype(BF16)
        v = v_ref[:, h * dv:(h + 1) * dv].astype(BF16)
        b_c = b_col[:, H + h:H + h + 1]
        ig_c = gc[:, h:h + 1]
        b_r = b_row[h:h + 1, :]
        m_prev = m_ref[h:h + 1, 0:1]
        C = C_ref[h]
        n = n_ref[h:h + 1, :]

        logD = jnp.where(tri, b_c - b_r + ig_row[h:h + 1, :], NEG_INF)
        inter = b_c + m_prev
        m_t = jnp.maximum(inter, jnp.max(logD, axis=-1, keepdims=True))
        s = _nt(q, kb) * jnp.exp(logD - m_t)
        w_inter = jnp.exp(inter - m_t)
        num = w_inter * _mm(q, C.astype(BF16)) + _mm(s.astype(BF16), v)
        den = w_inter * jnp.sum(q.astype(F32) * n, axis=-1, keepdims=True) + jnp.sum(s, axis=-1, keepdims=True)
        hv = num / jnp.maximum(jnp.abs(den), jnp.exp(-m_t))

        mu = jnp.mean(hv, axis=-1, keepdims=True)
        d = hv - mu
        hn = d * lax.rsqrt(jnp.mean(d * d, axis=-1, keepdims=True) + LN_EPS)
        og = _sigmoid(o_ref[:, h * dv:(h + 1) * dv])
        out_ref[:, h * dv:(h + 1) * dv] = (hn * ng_ref[:, h * dv:(h + 1) * dv] * og).astype(out_ref.dtype)

        bL = b_c[L - 1:L, :]
        logw = bL - b_c + ig_c
        m_new = jnp.maximum(bL + m_prev, jnp.max(logw, axis=0, keepdims=True))
        decay = jnp.exp(bL + m_prev - m_new)
        kw = k * jnp.exp(logw - m_new)
        C_ref[h] = decay * C + _tn(kw.astype(BF16), v)
        n_ref[h:h + 1, :] = decay * n + jnp.sum(kw, axis=0, keepdims=True)
        m_ref[h:h + 1, :] = jnp.broadcast_to(m_new, (1, m_ref.shape[1]))


def _mlstm_mixer(x, B, S, w_in, conv_w, ig_bias, fg_bias, norm_g):
    T = B * S
    H, L = MLSTM_HEADS, MLSTM_L
    wb = w_in.astype(BF16)
    main = _proj(x, wb[:, :3 * D_MODEL], out_dtype=F32, tm=512, tn=512)
    wg = jnp.pad(wb[:, 3 * D_MODEL:], ((0, 0), (0, 128 - 2 * H)))
    gates = _proj(x, wg, out_dtype=F32, tm=512, tn=128)
    gates_r = gates[:, :2 * H].reshape(B, S, 2 * H).transpose(0, 2, 1)
    bias16 = jnp.concatenate([ig_bias, fg_bias]).astype(F32)
    bias_c = jnp.pad(bias16, (0, 128 - 2 * H)).reshape(1, 128)
    bias_r = bias16.reshape(2 * H, 1)
    tri = jnp.tril(jnp.ones((L, L), F32))
    nc = S // L
    rowblk = lambda c: (lambda b, i: (b * nc + i, c))
    fixed = lambda b, i: (0, 0)
    return pl.pallas_call(
        _mlstm_kernel,
        name="mlstm",
        grid=(B, nc),
        in_specs=[pl.BlockSpec((L, D_MODEL), rowblk(0)), pl.BlockSpec((L, D_MODEL), rowblk(1)),
                  pl.BlockSpec((L, D_MODEL), rowblk(2)), pl.BlockSpec((L, 128), rowblk(0)),
                  pl.BlockSpec((None, 2 * H, L), lambda b, i: (b, 0, i)),
                  pl.BlockSpec((MLSTM_CONV, D_MODEL), fixed), pl.BlockSpec((1, 128), fixed),
                  pl.BlockSpec((2 * H, 1), fixed), pl.BlockSpec((1, D_MODEL), fixed),
                  pl.BlockSpec((L, L), fixed), pl.BlockSpec((L, L), fixed)],
        out_specs=pl.BlockSpec((L, D_MODEL), rowblk(0)),
        out_shape=jax.ShapeDtypeStruct((T, D_MODEL), BF16),
        scratch_shapes=[pltpu.VMEM((H, MLSTM_QK_DIM, MLSTM_V_DIM), F32), pltpu.VMEM((H, MLSTM_QK_DIM), F32),
                        pltpu.VMEM((H, 128), F32), pltpu.VMEM((L + 8, D_MODEL), F32)],
        compiler_params=_cparams("parallel", "arbitrary"),
    )(main, main, main, gates, gates_r, conv_w.astype(F32), bias_c, bias_r, norm_g.reshape(1, D_MODEL).astype(F32),
      tri, tri.T)


def _rot_cols(w):
    shp = w.shape
    w4 = w.reshape(shp[:-1] + (shp[-1] // HEAD_DIM, 2, ROPE_HALF))
    return jnp.flip(w4, axis=-2).reshape(shp)


def _nsa_compress_kernel(chk_ref, chv_ref, wk_lo_ref, wk_hi_ref, wv_lo_ref, wv_hi_ref, pk_ref, pv_ref, wk_ref, wv_ref,
                         cos_ref, sin_ref, kc_ref, vc_ref, sh_ref):
    NCH = chk_ref.shape[0]
    G = NSA_KV_HEADS
    sh_ref[NCH:NCH + 8, :] = jnp.zeros((8, sh_ref.shape[1]), F32)
    live = lax.broadcasted_iota(I32, (NCH, 1), 0) < NCH - 1

    def blocks(ch_ref, lo_ref, hi_ref, p_ref, w_ref):
        ch = ch_ref[...]
        n = lo_ref.shape[1]
        sh_ref[0:NCH, 0:n] = _mm(ch, hi_ref[...])
        const = _mm(p_ref[...].astype(BF16), w_ref[...])[0:1, :]
        return jnp.where(live, _mm(ch, lo_ref[...]) + sh_ref[1:NCH + 1, 0:n] + jnp.concatenate([const] * G, axis=1), 0.0)

    k2 = blocks(chk_ref, wk_lo_ref, wk_hi_ref, pk_ref, wk_ref)
    v2 = blocks(chv_ref, wv_lo_ref, wv_hi_ref, pv_ref, wv_ref)
    for g in range(G):
        raw = k2[:, 2 * g * HEAD_DIM:(2 * g + 1) * HEAD_DIM]
        rot = k2[:, (2 * g + 1) * HEAD_DIM:(2 * g + 2) * HEAD_DIM]
        kc_ref[g] = (raw * cos_ref[...] + rot * sin_ref[...]).astype(kc_ref.dtype)
        vc_ref[g] = v2[:, g * HEAD_DIM:(g + 1) * HEAD_DIM].astype(vc_ref.dtype)


def _nsa_cmp_select_kernel(q_ref, kc_ref, vc_ref, ovlT_ref, ocmp_ref, sel_ref):
    TQ = q_ref.shape[0]
    NCH = kc_ref.shape[0]
    NSB = ovlT_ref.shape[0]
    q0 = pl.program_id(2) * TQ
    t = q0 + lax.broadcasted_iota(I32, (TQ, 1), 0)
    cend = lax.broadcasted_iota(I32, (1, NCH), 1) * CMP_STRIDE + (CMP_BLK - 1)
    valid = cend <= t
    kc = kc_ref[...]
    vc = vc_ref[...]
    psum = jnp.zeros((TQ, NCH), F32)
    outs = []
    for j in range(NSA_J):
        qj = q_ref[:, j * HEAD_DIM:(j + 1) * HEAD_DIM]
        s = jnp.where(valid, _nt(qj, kc) * HEAD_DIM ** -0.5, NEG_INF)
        m = jnp.max(s, axis=-1, keepdims=True)
        e = jnp.where(valid, jnp.exp(s - m), 0.0)
        p = e / jnp.maximum(jnp.sum(e, axis=-1, keepdims=True), TINY)
        outs.append(_mm(p.astype(BF16), vc))
        psum = psum + p
    ocmp_ref[...] = jnp.concatenate(outs, axis=1).astype(ocmp_ref.dtype)

    imp = _nt(ovlT_ref[...], psum, precision=HIGHEST)
    nb = lax.broadcasted_iota(I32, (NSB, 1), 0)
    qblk = (q0 + lax.broadcasted_iota(I32, (1, TQ), 1)) // SEL_BLK
    forced = (nb == 0) | (nb == qblk) | (nb == qblk - 1)
    cur = jnp.where(forced, FORCE_SCORE, jnp.where(nb > qblk, NEG_INF, imp))
    sel = jnp.zeros((NSB, TQ), F32)
    for _ in range(min(N_SEL, NSB)):
        m = jnp.max(cur, axis=0, keepdims=True)
        idx = jnp.min(jnp.where(cur == m, nb, NSB), axis=0, keepdims=True)
        hit = nb == idx
        sel = jnp.where(hit, 1.0, sel)
        cur = jnp.where(hit, BELOW_NEG_INF, cur)
    sel_ref[...] = sel.astype(sel_ref.dtype)


NEG_FLOOR = -1e29


def _nsa_main_kernel(q_ref, ksT_ref, vs_ref, kwT_ref, vw_ref, sel_ref, ocmp_ref, gate_ref, gexp_ref, o_ref):
    TQ = q_ref.shape[0]
    S = vs_ref.shape[0]
    TK = min(NSA_TK, S)
    J = NSA_J
    q0 = pl.program_id(2) * TQ
    q4 = jnp.concatenate([q_ref[:, j * HEAD_DIM:(j + 1) * HEAD_DIM] for j in range(J)], axis=0) * HEAD_DIM ** -0.5
    t = q0 + lax.broadcasted_iota(I32, (TQ, 1), 0)
    selT = sel_ref[...]

    def attend(carry, kT, v, bias):
        m, l, acc = carry
        s = _mm(q4, kT) + jnp.concatenate([bias] * J, axis=0)
        m_new = jnp.maximum(m, jnp.max(s, axis=-1, keepdims=True))
        e = jnp.exp(s - m_new)
        a = jnp.exp(m - m_new)
        return m_new, a * l + jnp.sum(e, axis=-1, keepdims=True), a * acc + _mm(e.astype(BF16), v)

    init = (jnp.full((J * TQ, 1), NEG_FLOOR, F32), jnp.zeros((J * TQ, 1), F32), jnp.zeros((J * TQ, HEAD_DIM), F32))

    def body(c, carry):
        k0 = pl.multiple_of(c * TK, TK)
        kpos = k0 + lax.broadcasted_iota(I32, (1, TK), 1)
        expand = jnp.where(kpos // SEL_BLK == lax.broadcasted_iota(I32, (selT.shape[0], 1), 0), 1.0, 0.0).astype(BF16)
        bias = jnp.where((_tn(selT, expand) > 0.5) & (kpos <= t), 0.0, NEG_INF)
        return attend(carry, ksT_ref[:, pl.ds(k0, TK)], vs_ref[pl.ds(k0, TK), :], bias)

    nchunks = (q0 + TQ - 1) // TK + 1
    _, l, acc = lax.fori_loop(0, nchunks, body, init)
    o_slc = acc / jnp.maximum(l, TINY)

    span = min(NSA_WINDOW + TQ, S)
    start = pl.multiple_of(jnp.clip(q0 - NSA_WINDOW, 0, S - span), TQ)
    dist = t - (start + lax.broadcasted_iota(I32, (1, span), 1))
    bias = jnp.where((dist >= 0) & (dist < NSA_WINDOW), 0.0, NEG_INF)
    _, l, acc = attend(init, kwT_ref[:, pl.ds(start, span)], vw_ref[pl.ds(start, span), :], bias)
    o_win = acc / jnp.maximum(l, TINY)

    unstack = lambda o: jnp.concatenate([o[j * TQ:(j + 1) * TQ, :] for j in range(J)], axis=1)
    g = _sigmoid(gate_ref[...])
    gx = [_mm(g, gexp_ref[c], precision=HIGHEST) for c in range(3)]
    out = gx[0] * ocmp_ref[...].astype(F32) + gx[1] * unstack(o_slc) + gx[2] * unstack(o_win)
    o_ref[...] = out.astype(o_ref.dtype)


def _nsa_mixer(x, B, S, w_in, cmp_pos_k, cmp_pos_v, cmp_wk, cmp_wv):
    T = B * S
    H, G, J, dh = NSA_HEADS, NSA_KV_HEADS, NSA_J, HEAD_DIM
    kv = G * dh
    wb = w_in.astype(BF16)
    col = lambda a, n: wb[:, a:a + n]
    o_q, o_kc, o_vc, o_ks, o_vs, o_kw, o_vw, o_g = np.cumsum([0, H * dh] + [kv] * 6).tolist()
    pos = jnp.arange(S)
    w_rope = jnp.concatenate([col(o_q, H * dh), col(o_ks, kv), col(o_kw, kv)], axis=1)
    roped = _proj(x, w_rope, out_dtype=BF16, tm=512, tn=512, rope_tables=_rope_tables(pos, 512), seq=S)
    w_plain = jnp.concatenate([col(o_vs, kv), col(o_vw, kv), jnp.pad(col(o_g, 3 * H), ((0, 0), (0, 128 - 3 * H)))], axis=1)
    plain = _proj(x, w_plain, out_dtype=F32, tm=512, tn=128)

    heads_T = lambda a: a.reshape(B, S, G, dh).transpose(0, 2, 3, 1)
    heads = lambda a: a.reshape(B, S, G, dh).transpose(0, 2, 1, 3)
    ksT = heads_T(roped[:, H * dh:H * dh + kv])
    kwT = heads_T(roped[:, H * dh + kv:])
    vs = heads(plain[:, :kv]).astype(BF16)
    vw = heads(plain[:, kv:2 * kv]).astype(BF16)
    gates = plain[:, 2 * kv:]

    nch = S // CMP_STRIDE
    cw = CMP_STRIDE * kv
    chk = _proj(x, col(o_kc, kv), out_dtype=BF16, tm=512, tn=kv).reshape(B, nch, cw)
    chv = _proj(x, col(o_vc, kv), out_dtype=BF16, tm=512, tn=kv).reshape(B, nch, cw)
    wk = cmp_wk.reshape(CMP_BLK * dh, dh)
    wk2 = jnp.concatenate([wk, _rot_cols(wk)], axis=1).astype(BF16)
    wv2 = cmp_wv.reshape(CMP_BLK * dh, dh).astype(BF16)
    eye = jnp.eye(G, dtype=BF16)

    def per_head(w, part):
        n = w.shape[1]
        w3 = w.reshape(2, CMP_STRIDE, dh, n)[part]
        return jnp.einsum('lde,gh->lgdhe', w3, eye).reshape(cw, G * n)

    flat8 = lambda p_: jnp.pad(p_.reshape(1, CMP_BLK * dh), ((0, 7), (0, 0))).astype(F32)
    cend = jnp.arange(nch) * CMP_STRIDE + CMP_BLK - 1
    cos_c, sin_c = _rope_tables(cend, dh)
    fixed1 = lambda b: (0, 0)
    full1 = lambda a: pl.BlockSpec(a.shape, fixed1)
    consts = (per_head(wk2, 0), per_head(wk2, 1), per_head(wv2, 0), per_head(wv2, 1), flat8(cmp_pos_k), flat8(cmp_pos_v),
              wk2, wv2, cos_c, sin_c)
    kc, vc = pl.pallas_call(
        _nsa_compress_kernel,
        name="nsa_compress",
        grid=(B,),
        in_specs=[pl.BlockSpec((None, nch, cw), lambda b: (b, 0, 0))] * 2 + [full1(c) for c in consts],
        out_specs=[pl.BlockSpec((None, G, nch, dh), lambda b: (b, 0, 0, 0))] * 2,
        out_shape=[jax.ShapeDtypeStruct((B, G, nch, dh), BF16)] * 2,
        scratch_shapes=[pltpu.VMEM((nch + 8, 2 * kv), F32)],
        compiler_params=_cparams("parallel"),
    )(chk, chv, *consts)

    nsb = S // SEL_BLK
    c_idx, s_idx = np.arange(nch), np.arange(nsb)
    ovl = ((c_idx[:, None] * CMP_STRIDE + CMP_BLK - 1 >= s_idx[None, :] * SEL_BLK)
           & (c_idx[:, None] * CMP_STRIDE < (s_idx[None, :] + 1) * SEL_BLK)).astype(np.float32)
    TQ = NSA_TQ
    nq = S // TQ
    qblk = lambda b, g, i: (b * nq + i, g)
    bgi = lambda b, g, i: (b, g, 0, 0)
    ocmp, sel = pl.pallas_call(
        _nsa_cmp_select_kernel,
        name="nsa_cmp_select",
        grid=(B, G, nq),
        in_specs=[pl.BlockSpec((TQ, J * dh), qblk), pl.BlockSpec((None, None, nch, dh), bgi),
                  pl.BlockSpec((None, None, nch, dh), bgi), pl.BlockSpec((nsb, nch), lambda b, g, i: (0, 0))],
        out_specs=[pl.BlockSpec((TQ, J * dh), qblk), pl.BlockSpec((None, None, nsb, TQ), lambda b, g, i: (b, g, 0, i))],
        out_shape=[jax.ShapeDtypeStruct((T, H * dh), BF16), jax.ShapeDtypeStruct((B, G, nsb, S), BF16)],
        compiler_params=_cparams("parallel", "parallel", "parallel"),
    )(roped, kc, vc, jnp.asarray(ovl.T))

    gexp = np.zeros((G, 3, 128, J * dh), np.float32)
    for g in range(G):
        for c in range(3):
            for j in range(J):
                gexp[g, c, (g * J + j) * 3 + c, j * dh:(j + 1) * dh] = 1.0
    return pl.pallas_call(
        _nsa_main_kernel,
        name="nsa_main",
        grid=(B, G, nq),
        in_specs=[pl.BlockSpec((TQ, J * dh), qblk)] + [pl.BlockSpec((None, None, dh, S), bgi),
                                                       pl.BlockSpec((None, None, S, dh), bgi)] * 2 + [
            pl.BlockSpec((None, None, nsb, TQ), lambda b, g, i: (b, g, 0, i)),
            pl.BlockSpec((TQ, J * dh), qblk), pl.BlockSpec((TQ, 128), lambda b, g, i: (b * nq + i, 0)),
            pl.BlockSpec((None, 3, 128, J * dh), lambda b, g, i: (g, 0, 0, 0))],
        out_specs=pl.BlockSpec((TQ, J * dh), qblk),
        out_shape=jax.ShapeDtypeStruct((T, H * dh), BF16),
        compiler_params=_cparams("parallel", "parallel", "arbitrary"),
    )(roped, ksT, vs, kwT, vw, sel, ocmp, gates, jnp.asarray(gexp))


def _dil_kernel(q_ref, kp_ref, kc_ref, vp_ref, vc_ref, o_ref, lse_ref):
    NQ = q_ref.shape[0]
    prev_from = jnp.where(pl.program_id(2) == 0, NQ, 0)
    qi = lax.broadcasted_iota(I32, (NQ, 1), 0)
    kj = lax.broadcasted_iota(I32, (1, 2 * NQ), 1)
    dist = NQ + qi - kj
    valid = (dist >= 0) & (dist <= NQ) & (kj >= prev_from)
    lane = lax.broadcasted_iota(I32, (NQ, 128), 1)
    lse_all = jnp.zeros((NQ, 128), F32)
    for h in range(DIL_HEADS):
        sl = slice(h * HEAD_DIM, (h + 1) * HEAD_DIM)
        kb = jnp.concatenate([kp_ref[:, sl], kc_ref[:, sl]], axis=0)
        vb = jnp.concatenate([vp_ref[:, sl], vc_ref[:, sl]], axis=0)
        s = jnp.where(valid, _nt(q_ref[:, sl], kb) * HEAD_DIM ** -0.5, NEG_INF)
        m = jnp.max(s, axis=-1, keepdims=True)
        e = jnp.exp(s - m)
        den = jnp.sum(e, axis=-1, keepdims=True)
        o_ref[:, sl] = _mm((e / den).astype(BF16), vb).astype(o_ref.dtype)
        lse_all = jnp.where(lane == h, m + jnp.log(den), lse_all)
    lse_ref[...] = lse_all


def _proj_res_kernel(*refs, rope, dils):
    n_out = len(dils)
    x_ref, w_ref = refs[:2]
    rest = refs[2:]
    if rope:
        cos_ref, sin_ref = rest[:2]
        rest = rest[2:]
    out_refs, xb_ref, scr_ref = rest[:n_out], rest[n_out], rest[n_out + 1]
    tm, tn = x_ref.shape[0], w_ref.shape[1]

    @pl.when(pl.program_id(1) == 0)
    def _():
        xb_ref[...] = x_ref[...].astype(BF16)

    acc = _mm(xb_ref[...], w_ref[...])
    if rope:
        lane = lax.broadcasted_iota(I32, acc.shape, 1)
        lo = (lane % HEAD_DIM) < ROPE_HALF
        rot = jnp.where(lo, pltpu.roll(acc, tn - ROPE_HALF, 1), pltpu.roll(acc, ROPE_HALF, 1))
        acc = acc * cos_ref[...] + rot * sin_ref[...]
    if any(d > 1 for d in dils):
        for c in range(tn // 128):
            scr_ref[c] = acc[:, c * 128:(c + 1) * 128]
    for o_ref, dil in zip(out_refs, dils):
        if dil == 1:
            o_ref[...] = acc.astype(o_ref.dtype)
            continue
        for r in range(dil):
            for c in range(tn // 128):
                o_ref[:, r * tn + c * 128:r * tn + (c + 1) * 128] = (
                    scr_ref[c, pl.ds(r, tm // dil, stride=dil), :].astype(o_ref.dtype))


def _proj_res(x, w, dils, *, seq, rope, tm=512, tn=D_MODEL):
    T, K = x.shape
    N = w.shape[1]
    nparts = N // tn
    in_specs = [pl.BlockSpec((tm, K), lambda i, j: (i, 0)), pl.BlockSpec((K, tn), lambda i, j: (0, j))]
    args = [x, w]
    if rope:
        nseq = seq // tm
        in_specs += [pl.BlockSpec((tm, tn), lambda i, j: (i % nseq, 0))] * 2
        args += list(_rope_tables(jnp.arange(seq), tn))
    return pl.pallas_call(
        functools.partial(_proj_res_kernel, rope=rope, dils=dils),
        name="proj_residue_rope" if rope else "proj_residue",
        grid=(T // tm, nparts),
        in_specs=in_specs,
        out_specs=[pl.BlockSpec((tm // d, d * tn), lambda i, j: (i, j)) for d in dils],
        out_shape=[jax.ShapeDtypeStruct((T // d, nparts * d * tn), BF16) for d in dils],
        scratch_shapes=[pltpu.VMEM((tm, K), BF16), pltpu.VMEM((tn // 128, tm, 128), F32)],
        compiler_params=_cparams("parallel", "arbitrary"),
    )(*args)


def _dil_group(qk, v, B, S, dil):
    NQ = DIL_STEPS
    U = S // dil
    nb = U // NQ
    W = D_MODEL
    qk_view = qk.reshape(B, U, 2 * dil * W)
    v_view = v.reshape(B, U, dil * W)
    cur = lambda part: (lambda b, r, n: (b, n, part * dil + r))
    prev = lambda part: (lambda b, r, n: (b, jnp.maximum(n - 1, 0), part * dil + r))
    blk = lambda f: pl.BlockSpec((None, NQ, W), f)
    o, lse = pl.pallas_call(
        _dil_kernel,
        name=f"dilated_attn_{dil}",
        grid=(B, dil, nb),
        in_specs=[blk(cur(0)), blk(prev(1)), blk(cur(1)), blk(prev(0)), blk(cur(0))],
        out_specs=[pl.BlockSpec((None, NQ, W), lambda b, r, n: (b, n, r)),
                   pl.BlockSpec((None, NQ, 128), lambda b, r, n: (b, n, r))],
        out_shape=[jax.ShapeDtypeStruct((B, U, dil * W), BF16), jax.ShapeDtypeStruct((B, U, dil * 128), F32)],
        compiler_params=_cparams("parallel", "parallel", "arbitrary"),
    )(qk_view, qk_view, qk_view, v_view, v_view)
    return o.reshape(B * U, dil * W), lse.reshape(B * U, dil * 128)


def _dil_outproj_ln_kernel(o0_ref, o1_ref, o2_ref, l0_ref, l1_ref, l2_ref, hexp_ref, w_ref, x_ref, g_ref, b_ref, out_ref,
                           oscr_ref, lscr_ref, *, dils):
    tm, W = x_ref.shape

    def natural(o_ref, l_ref, dil):
        if dil == 1:
            return o_ref[...].astype(F32), l_ref[...]
        n = tm // dil
        for r in range(dil):
            lscr_ref[0, pl.ds(r, n, stride=dil), :] = l_ref[:, r * 128:(r + 1) * 128]
            for c in range(W // 128):
                oscr_ref[c, pl.ds(r, n, stride=dil), :] = o_ref[:, r * W + c * 128:r * W + (c + 1) * 128].astype(F32)
        return jnp.concatenate([oscr_ref[c] for c in range(W // 128)], axis=1), lscr_ref[0]

    os_, ls = zip(*[natural(o, l, d) for o, l, d in zip((o0_ref, o1_ref, o2_ref), (l0_ref, l1_ref, l2_ref), dils)])
    m = jnp.maximum(jnp.maximum(ls[0], ls[1]), ls[2])
    es = [jnp.exp(l - m) for l in ls]
    tot = es[0] + es[1] + es[2]
    y = jnp.zeros((tm, W), F32)
    for e, o in zip(es, os_):
        y = y + _mm(e / tot, hexp_ref[...], precision=HIGHEST) * o
    z = ALPHA * x_ref[...] + _mm(y.astype(BF16), w_ref[...])
    out_ref[...] = _ln_rows(z, g_ref[...], b_ref[...])


def _dilated_layer(x, B, S, w_in, w_out, g, b, *, tm=256):
    T, D = x.shape
    dils = tuple(d for _, d in DIL_GROUPS)
    wb = w_in.astype(BF16)
    vals = _proj_res(x, wb[:, 2 * len(dils) * D:], dils, seq=S, rope=False)
    outs = []
    for gi, dil in enumerate(dils):
        qk, = _proj_res(x, wb[:, 2 * gi * D:(2 * gi + 2) * D], (dil,), seq=S, rope=True)
        outs.append(_dil_group(qk, vals[gi], B, S, dil))
    hexp = np.zeros((128, D), np.float32)
    for h in range(DIL_HEADS):
        hexp[h, h * HEAD_DIM:(h + 1) * HEAD_DIM] = 1.0
    row = lambda i: (i, 0)
    fixed = lambda i: (0, 0)
    return pl.pallas_call(
        functools.partial(_dil_outproj_ln_kernel, dils=dils),
        name="dilated_outproj_ln",
        grid=(T // tm,),
        in_specs=[pl.BlockSpec((tm // d, d * D), row) for d in dils] + [pl.BlockSpec((tm // d, d * 128), row) for d in dils]
        + [pl.BlockSpec((128, D), fixed), pl.BlockSpec((D, D), fixed), pl.BlockSpec((tm, D), row),
           pl.BlockSpec((1, D), fixed), pl.BlockSpec((1, D), fixed)],
        out_specs=pl.BlockSpec((tm, D), row),
        out_shape=jax.ShapeDtypeStruct((T, D), F32),
        scratch_shapes=[pltpu.VMEM((D // 128, tm, 128), F32), pltpu.VMEM((1, tm, 128), F32)],
        compiler_params=_cparams("parallel"),
    )(outs[0][0], outs[1][0], outs[2][0], outs[0][1], outs[1][1], outs[2][1], jnp.asarray(hexp),
      w_out.astype(BF16), x, g.reshape(1, D), b.reshape(1, D))


def _pool_ln_kernel(x_ref, halo_ref, w_ref, scale_ref, g_ref, b_ref, o_ref, ext_ref):
    TS = x_ref.shape[0]
    s = pl.program_id(1)
    x = x_ref[...]
    ext_ref[0:POOL_HALO, :] = jnp.where(s == 0, 0.0, halo_ref[...])
    ext_ref[POOL_HALO:POOL_HALO + TS, :] = x
    cnt = (s * TS + lax.broadcasted_iota(I32, (TS, 1), 0) + 1).astype(F32)
    ys = []
    for gi, w in enumerate(POOL_WINDOWS):
        sl = slice(gi * POOL_GROUP, (gi + 1) * POOL_GROUP)
        xg = x[:, sl]
        tot = xg
        for j in range(1, w):
            tot = tot + ext_ref[POOL_HALO - j:POOL_HALO - j + TS, sl]
        mean = tot / jnp.minimum(cnt, float(w))
        ys.append(_mm((mean - xg).astype(BF16), w_ref[gi]))
    y = jnp.concatenate(ys, axis=1) * scale_ref[...]
    o_ref[...] = _ln_rows(ALPHA * x + y, g_ref[...], b_ref[...])


def _pool_layer(x, B, S, w_grp, scale, g, b, *, ts=512):
    T, D = x.shape
    ns = S // ts
    hb = ts // POOL_HALO
    fixed = lambda bb, s: (0, 0)
    return pl.pallas_call(
        _pool_ln_kernel,
        name="pool_ln",
        grid=(B, ns),
        in_specs=[pl.BlockSpec((ts, D), lambda bb, s: (bb * ns + s, 0)),
                  pl.BlockSpec((POOL_HALO, D), lambda bb, s: (jnp.maximum((bb * ns + s) * hb - 1, 0), 0)),
                  pl.BlockSpec((len(POOL_WINDOWS), POOL_GROUP, POOL_GROUP), lambda bb, s: (0, 0, 0)),
                  pl.BlockSpec((1, D), fixed), pl.BlockSpec((1, D), fixed), pl.BlockSpec((1, D), fixed)],
        out_specs=pl.BlockSpec((ts, D), lambda bb, s: (bb * ns + s, 0)),
        out_shape=jax.ShapeDtypeStruct((T, D), F32),
        scratch_shapes=[pltpu.VMEM((ts + POOL_HALO, D), F32)],
        compiler_params=_cparams("parallel", "arbitrary"),
    )(x, x, w_grp.astype(BF16), scale.reshape(1, D), g.reshape(1, D), b.reshape(1, D))


U32 = jnp.uint32
HI16 = 0xFFFF0000
ROW_WORDS = D_MODEL // 2


def _pack_halves(v):
    h = v.shape[1] // 2
    bits = lambda a: lax.bitcast_convert_type(a.astype(BF16).astype(F32), U32)
    return (bits(v[:, :h]) >> 16) | (bits(v[:, h:]) & U32(HI16))


def _unpack_halves(w):
    lo = lax.bitcast_convert_type(w << 16, F32)
    hi = lax.bitcast_convert_type(w & U32(HI16), F32)
    return jnp.concatenate([lo, hi], axis=1)


def _router_kernel(x_ref, wT_ref, bias_ref, triu_ref, eidx_ref, gw_ref, rank_ref, cnt_ref, xp_ref, carry_ref):
    E = N_EXPERTS
    per = E // N_EXPERT_GROUPS
    TM = x_ref.shape[0]

    @pl.when(pl.program_id(0) == 0)
    def _():
        carry_ref[...] = jnp.zeros_like(carry_ref)

    x = x_ref[...]
    xp_ref[...] = _pack_halves(x)

    scores = _sigmoid(_nt(wT_ref[...], x, precision=HIGHEST))
    biased = scores + bias_ref[...]
    eio = lax.broadcasted_iota(I32, (E, TM), 0)

    gio = lax.broadcasted_iota(I32, (per, TM), 0)
    gscore = []
    for gidx in range(N_EXPERT_GROUPS):
        slab = biased[gidx * per:(gidx + 1) * per, :]
        m1 = jnp.max(slab, axis=0, keepdims=True)
        i1 = jnp.min(jnp.where(slab == m1, gio, per), axis=0, keepdims=True)
        m2 = jnp.max(jnp.where(gio == i1, BELOW_NEG_INF, slab), axis=0, keepdims=True)
        gscore.append(m1 + m2)
    slabs = []
    for gidx in range(N_EXPERT_GROUPS):
        beat = jnp.zeros((1, TM), F32)
        for o in range(N_EXPERT_GROUPS):
            if o == gidx:
                continue
            wins = (gscore[o] >= gscore[gidx]) if o < gidx else (gscore[o] > gscore[gidx])
            beat = beat + jnp.where(wins, 1.0, 0.0)
        keep = beat < float(TOPK_GROUPS)
        slabs.append(jnp.where(keep, biased[gidx * per:(gidx + 1) * per, :], NEG_INF))
    cur = jnp.concatenate(slabs, axis=0)

    picked = jnp.zeros((E, TM), F32)
    idxs, vals = [], []
    for _ in range(TOP_K):
        m = jnp.max(cur, axis=0, keepdims=True)
        idx = jnp.min(jnp.where(cur == m, eio, E), axis=0, keepdims=True)
        hit = eio == idx
        picked = jnp.where(hit, 1.0, picked)
        cur = jnp.where(hit, BELOW_NEG_INF, cur)
        idxs.append(idx)
        vals.append(jnp.sum(jnp.where(hit, scores, 0.0), axis=0, keepdims=True))
    total = vals[0]
    for v in vals[1:]:
        total = total + v

    pos = _mm(picked.astype(BF16), triu_ref[...]) + carry_ref[...]
    for k in range(TOP_K):
        eidx_ref[k:k + 1, :] = idxs[k]
        gw_ref[k:k + 1, :] = vals[k] / total * ROUTED_SCALE
        rank_ref[k:k + 1, :] = jnp.sum(jnp.where(eio == idxs[k], pos, 0.0), axis=0, keepdims=True).astype(I32)
    carry_ref[...] = carry_ref[...] + jnp.sum(picked, axis=1, keepdims=True)
    cnt_ref[...] = carry_ref[...].astype(I32)


def _router(x, router_w, router_bias, *, tm=512):
    T, D = x.shape
    E = N_EXPERTS
    triu = jnp.triu(jnp.ones((tm, tm), F32), k=1).astype(BF16)
    col = lambda i: (0, i)
    fixed = lambda i: (0, 0)
    return pl.pallas_call(
        _router_kernel,
        name="moe_router",
        grid=(T // tm,),
        in_specs=[pl.BlockSpec((tm, D), lambda i: (i, 0)), pl.BlockSpec((E, D), fixed),
                  pl.BlockSpec((E, 1), fixed), pl.BlockSpec((tm, tm), fixed)],
        out_specs=[pl.BlockSpec((TOP_K, tm), col), pl.BlockSpec((TOP_K, tm), col), pl.BlockSpec((TOP_K, tm), col),
                   pl.BlockSpec((E, 1), fixed), pl.BlockSpec((tm, ROW_WORDS), lambda i: (i, 0))],
        out_shape=[jax.ShapeDtypeStruct((TOP_K, T), I32), jax.ShapeDtypeStruct((TOP_K, T), F32),
                   jax.ShapeDtypeStruct((TOP_K, T), I32), jax.ShapeDtypeStruct((E, 1), I32),
                   jax.ShapeDtypeStruct((T, ROW_WORDS), U32)],
        scratch_shapes=[pltpu.VMEM((E, 1), F32)],
        compiler_params=_cparams("arbitrary"),
    )(x, router_w.T.astype(F32), router_bias.reshape(E, 1).astype(F32), triu)


def _tile_indices(dest_hbm, idx_ref, sem_idx, n_idx):
    i = pl.program_id(0)

    def idx_copy(step):
        slot = step % 2
        return pltpu.make_async_copy(dest_hbm.at[step], idx_ref.at[pl.ds(slot * n_idx, n_idx)], sem_idx.at[slot])

    @pl.when(i == 0)
    def _():
        idx_copy(0).start()

    idx_copy(i).wait()

    @pl.when(i + 1 < pl.num_programs(0))
    def _():
        idx_copy(i + 1).start()

    return (i % 2) * n_idx


def _dispatch_kernel(cnt_ref, pstart_ref, dest_hbm, x_ref, rows_ref, idx_ref, zero_ref, sem_idx, sem_rows, sem_zero):
    TD = x_ref.shape[0]

    @pl.when(pl.program_id(0) == 0)
    def _():
        zero_ref[...] = jnp.zeros_like(zero_ref)

        def per_expert(e, c):
            n = cnt_ref[e]
            base = pstart_ref[e]
            pad_to = (n + EXP_BLK - 1) // EXP_BLK * EXP_BLK
            fill = lambda r: pltpu.make_async_copy(zero_ref.at[pl.ds(0, 1)], rows_ref.at[base + r], sem_zero)

            def start(r, c2):
                fill(r).start()
                return c2

            def wait(r, c2):
                fill(r).wait()
                return c2

            lax.fori_loop(n, pad_to, start, 0)
            lax.fori_loop(n, pad_to, wait, 0)
            return c

        lax.fori_loop(0, N_EXPERTS, per_expert, 0)

    base = _tile_indices(dest_hbm, idx_ref, sem_idx, TOP_K * TD)
    scatter = lambda t, k: pltpu.make_async_copy(x_ref.at[pl.ds(t, 1)], rows_ref.at[idx_ref[base + k * TD + t]], sem_rows)

    def issue(t, c):
        for k in range(TOP_K):
            scatter(t, k).start(priority=k % 2)
        return c

    def drain(t, c):
        for k in range(TOP_K):
            scatter(t, k).wait()
        return c

    lax.fori_loop(0, TD, issue, 0)
    lax.fori_loop(0, TD, drain, 0)


def _dispatch(xp, dest_tiles, counts, pad_start, n_rows, *, td):
    T = xp.shape[0]
    return pl.pallas_call(
        _dispatch_kernel,
        name="moe_dispatch",
        grid_spec=pltpu.PrefetchScalarGridSpec(
            num_scalar_prefetch=2,
            grid=(T // td,),
            in_specs=[pl.BlockSpec(memory_space=pl.ANY), pl.BlockSpec((td, ROW_WORDS), lambda i, c, p: (i, 0))],
            out_specs=pl.BlockSpec(memory_space=pl.ANY),
            scratch_shapes=[pltpu.SMEM((2 * TOP_K * td,), I32), pltpu.VMEM((8, ROW_WORDS), U32),
                            pltpu.SemaphoreType.DMA((2,)), pltpu.SemaphoreType.DMA(()), pltpu.SemaphoreType.DMA(())],
        ),
        out_shape=jax.ShapeDtypeStruct((n_rows, 1, ROW_WORDS), U32),
        compiler_params=_cparams("arbitrary"),
    )(counts, pad_start, dest_tiles, xp)


def _expert_kernel(be_ref, nu_ref, rows_hbm, wgu_ref, wd_ref, out_hbm, xbuf, obuf, sem_in, sem_out):
    i = pl.program_id(0)
    nu = nu_ref[0]
    blk = lambda step: pl.ds(pl.multiple_of(step * EXP_BLK, EXP_BLK), EXP_BLK)
    in_copy = lambda step: pltpu.make_async_copy(rows_hbm.at[blk(step), 0], xbuf.at[step % 2], sem_in.at[step % 2])
    out_copy = lambda step: pltpu.make_async_copy(obuf.at[step % 2], out_hbm.at[blk(step), 0], sem_out.at[step % 2])

    @pl.when(i == 0)
    def _():
        in_copy(0).start()

    @pl.when(i < nu)
    def _():
        in_copy(i).wait()

        @pl.when(i + 1 < nu)
        def _():
            in_copy(i + 1).start()

        @pl.when(i >= 2)
        def _():
            out_copy(i - 2).wait()

        slot = i % 2
        gu = _mm(_unpack_halves(xbuf[slot]).astype(BF16), wgu_ref[0])
        h = _silu(gu[:, :EXPERT_FF]) * gu[:, EXPERT_FF:]
        obuf[slot] = _pack_halves(_mm(h.astype(BF16), wd_ref[0]))
        out_copy(i).start()

    @pl.when(i == pl.num_programs(0) - 1)
    def _():
        @pl.when(nu >= 2)
        def _():
            out_copy(nu - 2).wait()

        out_copy(nu - 1).wait()


def _experts(rows, blk_expert, n_used, wgu, wd):
    R = rows.shape[0]
    D = wd.shape[-1]
    n_blk = R // EXP_BLK
    live = lambda i, be, nu: jnp.minimum(i, nu[0] - 1)
    return pl.pallas_call(
        _expert_kernel,
        name="moe_experts",
        grid_spec=pltpu.PrefetchScalarGridSpec(
            num_scalar_prefetch=2,
            grid=(n_blk,),
            in_specs=[pl.BlockSpec(memory_space=pl.ANY),
                      pl.BlockSpec((1, D, 2 * EXPERT_FF), lambda i, be, nu: (be[live(i, be, nu)], 0, 0)),
                      pl.BlockSpec((1, EXPERT_FF, D), lambda i, be, nu: (be[live(i, be, nu)], 0, 0))],
            out_specs=pl.BlockSpec(memory_space=pl.ANY),
            scratch_shapes=[pltpu.VMEM((2, EXP_BLK, ROW_WORDS), U32), pltpu.VMEM((2, EXP_BLK, ROW_WORDS), U32),
                            pltpu.SemaphoreType.DMA((2,)), pltpu.SemaphoreType.DMA((2,))],
        ),
        out_shape=jax.ShapeDtypeStruct((R, 1, ROW_WORDS), U32),
        compiler_params=_cparams("arbitrary"),
    )(blk_expert, n_used, rows, wgu, wd)


def _combine_kernel(dest_hbm, gw_ref, x_ref, rows_ref, wsgu_ref, wsd_ref, g_ref, b_ref, p_ref, wp_ref, wpg_ref,
                    o_ref, idx_ref, buf_ref, sem_idx, sem_rows):
    TM = x_ref.shape[0]
    n_idx = TOP_K * TM
    i = pl.program_id(0)
    n = pl.num_programs(0)

    def idx_copy(step):
        slot = step % 3
        return pltpu.make_async_copy(dest_hbm.at[step], idx_ref.at[pl.ds(slot * n_idx, n_idx)], sem_idx.at[slot])

    def row_loop(step, slot, start):
        base = (step % 3) * n_idx

        def body(t, c):
            for k in range(TOP_K):
                cp = pltpu.make_async_copy(rows_ref.at[idx_ref[base + k * TM + t]], buf_ref.at[slot, k, pl.ds(t, 1)],
                                           sem_rows.at[slot])
                if start:
                    cp.start(priority=k % 2)
                else:
                    cp.wait()
            return c

        lax.fori_loop(0, TM, body, 0)

    def for_parity(step, fn):
        for slot in range(2):
            pl.when(step % 2 == slot)(functools.partial(fn, slot))

    @pl.when(i == 0)
    def _():
        idx_copy(0).start()
        idx_copy(0).wait()
        row_loop(0, 0, True)

        @pl.when(n > 1)
        def _():
            idx_copy(1).start()

    @pl.when(i + 1 < n)
    def _():
        idx_copy(i + 1).wait()

        @pl.when(i + 2 < n)
        def _():
            idx_copy(i + 2).start()

        for_parity(i + 1, lambda slot: row_loop(i + 1, slot, True))

    x = x_ref[...]
    gu = _mm(x.astype(BF16), wsgu_ref[...])
    ff = gu.shape[1] // 2
    f = _mm((_silu(gu[:, :ff]) * gu[:, ff:]).astype(BF16), wsd_ref[...])
    for_parity(i, lambda slot: row_loop(i, slot, False))
    gw = gw_ref[...]
    cur = i % 2
    for k in range(TOP_K):
        f = f + gw[:, k:k + 1] * _unpack_halves(buf_ref[cur, k])
    x2 = _ln_rows(ALPHA * x + f, g_ref[...], b_ref[...])
    gate = _sigmoid(_mm(x2.astype(BF16), wpg_ref[...]))
    o_ref[...] = x2 + gate * _mm(p_ref[...].astype(BF16), wp_ref[...])


def _combine(dest_tiles, gw, x, rows_out, wsgu, wsd, g, b, p, wp, wpg, *, tm):
    T, D = x.shape
    row = lambda i: (i, 0)
    fixed = lambda i: (0, 0)
    full = lambda a: pl.BlockSpec(a.shape, fixed)
    return pl.pallas_call(
        _combine_kernel,
        name="moe_combine",
        grid=(T // tm,),
        in_specs=[pl.BlockSpec(memory_space=pl.ANY), pl.BlockSpec((tm, TOP_K), row), pl.BlockSpec((tm, D), row),
                  pl.BlockSpec(memory_space=pl.ANY), full(wsgu), full(wsd), pl.BlockSpec((1, D), fixed),
                  pl.BlockSpec((1, D), fixed), pl.BlockSpec((tm, PLE_DIM), row), full(wp), full(wpg)],
        out_specs=pl.BlockSpec((tm, D), row),
        out_shape=jax.ShapeDtypeStruct((T, D), F32),
        scratch_shapes=[pltpu.SMEM((3 * TOP_K * tm,), I32), pltpu.VMEM((2, TOP_K, tm, ROW_WORDS), U32),
                        pltpu.SemaphoreType.DMA((3,)), pltpu.SemaphoreType.DMA((2,))],
        compiler_params=_cparams("arbitrary"),
    )(dest_tiles, gw, x, rows_out, wsgu, wsd, g.reshape(1, D), b.reshape(1, D), p, wp, wpg)


MOE_TILE = 256


def _moe_ple_layer(x, p, router_w, router_bias, w_gate, w_up, w_down, ws_gate, ws_up, ws_down, g, b, ple_w, ple_gate_w):
    T, D = x.shape
    eidx, gw, rank, counts, xp = _router(x, router_w, router_bias)
    counts = counts.reshape(N_EXPERTS)
    padded = (counts + EXP_BLK - 1) // EXP_BLK * EXP_BLK
    pad_end = jnp.cumsum(padded)
    pad_start = pad_end - padded
    n_blk = T * TOP_K // EXP_BLK + N_EXPERTS
    e_iota = jnp.arange(N_EXPERTS, dtype=I32)
    dest = rank + jnp.sum(jnp.where(eidx[..., None] == e_iota, pad_start, 0), axis=-1)
    tm = MOE_TILE
    dest_tiles = dest.reshape(TOP_K, T // tm, tm).transpose(1, 0, 2).reshape(T // tm, TOP_K * tm)
    blk_first = jnp.arange(n_blk, dtype=I32)[:, None] * EXP_BLK
    blk_expert = jnp.minimum(jnp.sum((pad_end[None, :] <= blk_first).astype(I32), axis=-1), N_EXPERTS - 1)
    n_used = (pad_end[-1:] // EXP_BLK).astype(I32)

    rows = _dispatch(xp, dest_tiles, counts.astype(I32), pad_start.astype(I32), n_blk * EXP_BLK, td=tm)
    wgu = jnp.concatenate([w_gate, w_up], axis=-1).astype(BF16)
    rows_out = _experts(rows, blk_expert, n_used, wgu, w_down.astype(BF16))
    wsgu = jnp.concatenate([ws_gate, ws_up], axis=-1).astype(BF16)
    return _combine(dest_tiles, gw.T, x, rows_out, wsgu, ws_down.astype(BF16), g, b, p,
                    ple_w.astype(BF16), ple_gate_w.astype(BF16), tm=tm)


def kernel(x, p, ln_g, ln_b, mlstm_w_in, mlstm_conv, mlstm_ig_bias, mlstm_fg_bias, mlstm_norm_g, mlstm_w_out, nsa_w_in, nsa_cmp_pos_k, nsa_cmp_pos_v, nsa_cmp_wk, nsa_cmp_wv, nsa_w_out, dil_w_in, dil_w_out, pool_w, pool_scale, router_w, router_bias, exp_w_gate, exp_w_up, exp_w_down, sh_w_gate, sh_w_up, sh_w_down, ple_w, ple_gate_w):
    B, S, D = x.shape
    T = B * S
    xf = x.reshape(T, D)
    pf = p.reshape(DEPTH, T, PLE_DIM)
    for i in range(DEPTH):
        kind, j = i % 4, i // 4
        g1, b1 = ln_g[i, 0], ln_b[i, 0]
        if kind == 0:
            y = _mlstm_mixer(xf, B, S, mlstm_w_in[j], mlstm_conv[j], mlstm_ig_bias[j], mlstm_fg_bias[j], mlstm_norm_g[j])
            xf = _outproj_ln(y, mlstm_w_out[j].astype(BF16), xf, g1, b1)
        elif kind == 1:
            y = _nsa_mixer(xf, B, S, nsa_w_in[j], nsa_cmp_pos_k[j], nsa_cmp_pos_v[j], nsa_cmp_wk[j], nsa_cmp_wv[j])
            xf = _outproj_ln(y, nsa_w_out[j].astype(BF16), xf, g1, b1)
        elif kind == 2:
            xf = _dilated_layer(xf, B, S, dil_w_in[j], dil_w_out[j], g1, b1)
        else:
            xf = _pool_layer(xf, B, S, pool_w[j], pool_scale[j], g1, b1)
        xf = _moe_ple_layer(xf, pf[i], router_w[i], router_bias[i], exp_w_gate[i], exp_w_up[i], exp_w_down[i],
                            sh_w_gate[i], sh_w_up[i], sh_w_down[i], ln_g[i, 1], ln_b[i, 1], ple_w[i], ple_gate_w[i])
    return xf.reshape(B, S, D)
```

```python
import functools

import numpy as np
import jax
import jax.numpy as jnp
from jax import lax
from jax.experimental import pallas as pl
from jax.experimental.pallas import tpu as pltpu

F32 = jnp.float32
BF16 = jnp.bfloat16
I32 = jnp.int32
HIGHEST = lax.Precision.HIGHEST

D_MODEL = 1024
DEPTH = 4
ALPHA = (2.0 * DEPTH) ** 0.25
LN_EPS = 1e-5
NEG_INF = -1e30
TINY = 1e-30
BELOW_NEG_INF = -3e38
ROPE_THETA = 10000.0
HEAD_DIM = 64
ROPE_HALF = HEAD_DIM // 2

MLSTM_HEADS = 8
MLSTM_QK_DIM = 64
MLSTM_V_DIM = 128
MLSTM_CONV = 4
MLSTM_L = 256

NSA_HEADS = 16
NSA_KV_HEADS = 4
NSA_J = NSA_HEADS // NSA_KV_HEADS
CMP_STRIDE = 16
CMP_BLK = 32
SEL_BLK = 64
N_SEL = 16
NSA_WINDOW = 512
FORCE_SCORE = 1e4
NSA_TQ = 128
NSA_TQ_SELECT = 256
NSA_TK = 512

DIL_HEADS = 16
DIL_GROUPS = ((128, 1), (512, 4), (2048, 16))
DIL_STEPS = 128

POOL_WINDOWS = (2, 4, 8, 16)
POOL_GROUP = 256
POOL_HALO = 16

N_EXPERTS = 64
TOP_K = 8
N_EXPERT_GROUPS = 8
TOPK_GROUPS = 4
EXPERT_FF = 256
ROUTED_SCALE = 2.5
EXP_BLK = 512
PLE_DIM = 256

VMEM_LIMIT = 48 * 1024 * 1024


def _cparams(*sem):
    return pltpu.CompilerParams(dimension_semantics=sem, vmem_limit_bytes=VMEM_LIMIT)


def _nt(a, b, **kw):
    return lax.dot_general(a, b, (((1,), (1,)), ((), ())), preferred_element_type=F32, **kw)


def _tn(a, b, **kw):
    return lax.dot_general(a, b, (((0,), (0,)), ((), ())), preferred_element_type=F32, **kw)


def _mm(a, b, **kw):
    return jnp.dot(a, b, preferred_element_type=F32, **kw)


def _sigmoid(z):
    return 1.0 / (1.0 + jnp.exp(-z))


def _silu(z):
    return z * _sigmoid(z)


def _ln_rows(z, g, b):
    mu = jnp.mean(z, axis=-1, keepdims=True)
    d = z - mu
    var = jnp.mean(d * d, axis=-1, keepdims=True)
    return d * lax.rsqrt(var + LN_EPS) * g + b


def _proj_kernel(*refs, rope):
    if rope:
        x_ref, w_ref, cos_ref, sin_ref, o_ref, xb_ref = refs
    else:
        x_ref, w_ref, o_ref, xb_ref = refs

    @pl.when(pl.program_id(1) == 0)
    def _():
        xb_ref[...] = x_ref[...].astype(BF16)

    acc = _mm(xb_ref[...], w_ref[...])
    if rope:
        tn = acc.shape[1]
        lane = lax.broadcasted_iota(I32, acc.shape, 1)
        lo = (lane % HEAD_DIM) < ROPE_HALF
        rot = jnp.where(lo, pltpu.roll(acc, tn - ROPE_HALF, 1), pltpu.roll(acc, ROPE_HALF, 1))
        acc = acc * cos_ref[...] + rot * sin_ref[...]
    o_ref[...] = acc.astype(o_ref.dtype)


def _proj(x, w, *, out_dtype, tm, tn, rope_tables=None, seq=None):
    T, K = x.shape
    N = w.shape[1]
    assert T % tm == 0 and N % tn == 0
    in_specs = [pl.BlockSpec((tm, K), lambda i, j: (i, 0)), pl.BlockSpec((K, tn), lambda i, j: (0, j))]
    args = [x, w]
    if rope_tables is not None:
        nseq = seq // tm
        in_specs += [pl.BlockSpec((tm, tn), lambda i, j: (i % nseq, 0))] * 2
        args += list(rope_tables)
    return pl.pallas_call(
        functools.partial(_proj_kernel, rope=rope_tables is not None),
        name="proj_rope" if rope_tables is not None else "proj",
        grid=(T // tm, N // tn),
        in_specs=in_specs,
        out_specs=pl.BlockSpec((tm, tn), lambda i, j: (i, j)),
        out_shape=jax.ShapeDtypeStruct((T, N), out_dtype),
        scratch_shapes=[pltpu.VMEM((tm, K), BF16)],
        compiler_params=_cparams("parallel", "arbitrary"),
    )(*args)


def _rope_tables(pos, width):
    inv = ROPE_THETA ** (-jnp.arange(ROPE_HALF, dtype=F32) / ROPE_HALF)
    ang = pos.astype(F32)[:, None] * inv[None, :]
    cos, sin = jnp.cos(ang), jnp.sin(ang)
    cos64 = jnp.concatenate([cos, cos], -1)
    sin64 = jnp.concatenate([-sin, sin], -1)
    rep = width // HEAD_DIM
    return jnp.tile(cos64, (1, rep)), jnp.tile(sin64, (1, rep))


def _outproj_ln_kernel(y_ref, w_ref, x_ref, g_ref, b_ref, o_ref):
    y = _mm(y_ref[...].astype(BF16), w_ref[...])
    o_ref[...] = _ln_rows(ALPHA * x_ref[...] + y, g_ref[...], b_ref[...])


def _outproj_ln(y, w, x, g, b, *, tm=256):
    T, D = x.shape
    K = y.shape[1]
    row = lambda i: (i, 0)
    fixed = lambda i: (0, 0)
    return pl.pallas_call(
        _outproj_ln_kernel,
        name="outproj_ln",
        grid=(T // tm,),
        in_specs=[pl.BlockSpec((tm, K), row), pl.BlockSpec((K, D), fixed), pl.BlockSpec((tm, D), row),
                  pl.BlockSpec((1, D), fixed), pl.BlockSpec((1, D), fixed)],
        out_specs=pl.BlockSpec((tm, D), row),
        out_shape=jax.ShapeDtypeStruct((T, D), F32),
        compiler_params=_cparams("parallel"),
    )(y, w, x, g.reshape(1, D), b.reshape(1, D))


def _log_sigmoid(z):
    return jnp.minimum(z, 0.0) - jnp.log(1.0 + jnp.exp(-jnp.abs(z)))


def _mlstm_kernel(qk_ref, v_ref, o_ref, gc_ref, gr_ref, convw_ref, bias_c_ref, bias_r_ref, ng_ref,
                  tri_ref, triT_ref, out_ref, C_ref, n_ref, m_ref, ext_ref):
    L = qk_ref.shape[0]
    H, dk, dv = MLSTM_HEADS, MLSTM_QK_DIM, MLSTM_V_DIM

    @pl.when(pl.program_id(1) == 0)
    def _():
        C_ref[...] = jnp.zeros_like(C_ref)
        n_ref[...] = jnp.zeros_like(n_ref)
        m_ref[...] = jnp.zeros_like(m_ref)
        ext_ref[0:8, :] = jnp.zeros((8, ext_ref.shape[1]), F32)

    cur = qk_ref[...]
    ext_ref[8:8 + L, :] = cur
    acc = convw_ref[3:4, :] * cur
    for j in range(MLSTM_CONV - 1):
        acc = acc + convw_ref[j:j + 1, :] * ext_ref[5 + j:5 + j + L, :]
    ext_ref[0:8, :] = cur[L - 8:L, :]
    qk = _silu(acc)

    gc = gc_ref[...] + bias_c_ref[...]
    gr = gr_ref[...] + bias_r_ref[...]
    b_col = _mm(tri_ref[...], _log_sigmoid(gc), precision=HIGHEST)
    b_row = _mm(_log_sigmoid(gr[H:2 * H, :]), triT_ref[...], precision=HIGHEST)
    ig_row = gr[0:H, :]
    tri = lax.broadcasted_iota(I32, (L, L), 0) >= lax.broadcasted_iota(I32, (L, L), 1)

    for h in range(H):
        q = (qk[:, h * dk:(h + 1) * dk] * dk ** -0.5).astype(BF16)
        k = qk[:, H * dk + h * dk:H * dk + (h + 1) * dk]
        kb = k.astype(BF16)
        v = v_ref[:, h * dv:(h + 1) * dv].astype(BF16)
        b_c = b_col[:, H + h:H + h + 1]
        ig_c = gc[:, h:h + 1]
        b_r = b_row[h:h + 1, :]
        m_prev = m_ref[h:h + 1, 0:1]
        C = C_ref[h]
        n = n_ref[h:h + 1, :]

        logD = jnp.where(tri, b_c - b_r + ig_row[h:h + 1, :], NEG_INF)
        inter = b_c + m_prev
        m_t = jnp.maximum(inter, jnp.max(logD, axis=-1, keepdims=True))
        s = _nt(q, kb) * jnp.exp(logD - m_t)
        w_inter = jnp.exp(inter - m_t)
        num = w_inter * _mm(q, C.astype(BF16)) + _mm(s.astype(BF16), v)
        den = w_inter * jnp.sum(q.astype(F32) * n, axis=-1, keepdims=True) + jnp.sum(s, axis=-1, keepdims=True)
        hv = num / jnp.maximum(jnp.abs(den), jnp.exp(-m_t))

        mu = jnp.mean(hv, axis=-1, keepdims=True)
        d = hv - mu
        hn = d * lax.rsqrt(jnp.mean(d * d, axis=-1, keepdims=True) + LN_EPS)
        og = _sigmoid(o_ref[:, h * dv:(h + 1) * dv])
        out_ref[:, h * dv:(h + 1) * dv] = (hn * ng_ref[:, h * dv:(h + 1) * dv] * og).astype(out_ref.dtype)

        bL = b_c[L - 1:L, :]
        logw = bL - b_c + ig_c
        m_new = jnp.maximum(bL + m_prev, jnp.max(logw, axis=0, keepdims=True))
        decay = jnp.exp(bL + m_prev - m_new)
        kw = k * jnp.exp(logw - m_new)
        C_ref[h] = decay * C + _tn(kw.astype(BF16), v)
        n_ref[h:h + 1, :] = decay * n + jnp.sum(kw, axis=0, keepdims=True)
        m_ref[h:h + 1, :] = jnp.broadcast_to(m_new, (1, m_ref.shape[1]))


def _mlstm_mixer(x, B, S, w_in, conv_w, ig_bias, fg_bias, norm_g):
    T = B * S
    H, L = MLSTM_HEADS, MLSTM_L
    wb = w_in.astype(BF16)
    main = _proj(x, wb[:, :3 * D_MODEL], out_dtype=F32, tm=512, tn=512)
    wg = jnp.pad(wb[:, 3 * D_MODEL:], ((0, 0), (0, 128 - 2 * H)))
    gates = _proj(x, wg, out_dtype=F32, tm=512, tn=128)
    gates_r = gates[:, :2 * H].reshape(B, S, 2 * H).transpose(0, 2, 1)
    bias16 = jnp.concatenate([ig_bias, fg_bias]).astype(F32)
    bias_c = jnp.pad(bias16, (0, 128 - 2 * H)).reshape(1, 128)
    bias_r = bias16.reshape(2 * H, 1)
    tri = jnp.tril(jnp.ones((L, L), F32))
    nc = S // L
    rowblk = lambda c: (lambda b, i: (b * nc + i, c))
    fixed = lambda b, i: (0, 0)
    return pl.pallas_call(
        _mlstm_kernel,
        name="mlstm",
        grid=(B, nc),
        in_specs=[pl.BlockSpec((L, D_MODEL), rowblk(0)), pl.BlockSpec((L, D_MODEL), rowblk(1)),
                  pl.BlockSpec((L, D_MODEL), rowblk(2)), pl.BlockSpec((L, 128), rowblk(0)),
                  pl.BlockSpec((None, 2 * H, L), lambda b, i: (b, 0, i)),
                  pl.BlockSpec((MLSTM_CONV, D_MODEL), fixed), pl.BlockSpec((1, 128), fixed),
                  pl.BlockSpec((2 * H, 1), fixed), pl.BlockSpec((1, D_MODEL), fixed),
                  pl.BlockSpec((L, L), fixed), pl.BlockSpec((L, L), fixed)],
        out_specs=pl.BlockSpec((L, D_MODEL), rowblk(0)),
        out_shape=jax.ShapeDtypeStruct((T, D_MODEL), BF16),
        scratch_shapes=[pltpu.VMEM((H, MLSTM_QK_DIM, MLSTM_V_DIM), F32), pltpu.VMEM((H, MLSTM_QK_DIM), F32),
                        pltpu.VMEM((H, 128), F32), pltpu.VMEM((L + 8, D_MODEL), F32)],
        compiler_params=_cparams("parallel", "arbitrary"),
    )(main, main, main, gates, gates_r, conv_w.astype(F32), bias_c, bias_r, norm_g.reshape(1, D_MODEL).astype(F32),
      tri, tri.T)


def _rot_cols(w):
    shp = w.shape
    w4 = w.reshape(shp[:-1] + (shp[-1] // HEAD_DIM, 2, ROPE_HALF))
    return jnp.flip(w4, axis=-2).reshape(shp)


def _nsa_compress_kernel(chk_ref, chv_ref, wk_lo_ref, wk_hi_ref, wv_lo_ref, wv_hi_ref, pk_ref, pv_ref, wk_ref, wv_ref,
                         cos_ref, sin_ref, kc_ref, vc_ref, sh_ref):
    NCH = chk_ref.shape[0]
    G = NSA_KV_HEADS
    sh_ref[NCH:NCH + 8, :] = jnp.zeros((8, sh_ref.shape[1]), F32)
    live = lax.broadcasted_iota(I32, (NCH, 1), 0) < NCH - 1

    def blocks(ch_ref, lo_ref, hi_ref, p_ref, w_ref):
        ch = ch_ref[...]
        n = lo_ref.shape[1]
        sh_ref[0:NCH, 0:n] = _mm(ch, hi_ref[...])
        const = _mm(p_ref[...].astype(BF16), w_ref[...])[0:1, :]
        return jnp.where(live, _mm(ch, lo_ref[...]) + sh_ref[1:NCH + 1, 0:n] + jnp.concatenate([const] * G, axis=1), 0.0)

    k2 = blocks(chk_ref, wk_lo_ref, wk_hi_ref, pk_ref, wk_ref)
    v2 = blocks(chv_ref, wv_lo_ref, wv_hi_ref, pv_ref, wv_ref)
    for g in range(G):
        raw = k2[:, 2 * g * HEAD_DIM:(2 * g + 1) * HEAD_DIM]
        rot = k2[:, (2 * g + 1) * HEAD_DIM:(2 * g + 2) * HEAD_DIM]
        kc_ref[g] = (raw * cos_ref[...] + rot * sin_ref[...]).astype(kc_ref.dtype)
        vc_ref[g] = v2[:, g * HEAD_DIM:(g + 1) * HEAD_DIM].astype(vc_ref.dtype)


def _nsa_cmp_select_kernel(q_ref, kc_ref, vc_ref, ovlT_ref, ocmp_ref, sel_ref):
    TQ = q_ref.shape[0]
    NCH = kc_ref.shape[0]
    NSB = ovlT_ref.shape[0]
    q0 = pl.program_id(2) * TQ
    t = q0 + lax.broadcasted_iota(I32, (TQ, 1), 0)
    cend = lax.broadcasted_iota(I32, (1, NCH), 1) * CMP_STRIDE + (CMP_BLK - 1)
    valid = cend <= t
    kc = kc_ref[...]
    vc = vc_ref[...]
    psum = jnp.zeros((TQ, NCH), F32)
    outs = []
    for j in range(NSA_J):
        qj = q_ref[:, j * HEAD_DIM:(j + 1) * HEAD_DIM]
        s = jnp.where(valid, _nt(qj, kc) * HEAD_DIM ** -0.5, NEG_INF)
        m = jnp.max(s, axis=-1, keepdims=True)
        e = jnp.where(valid, jnp.exp(s - m), 0.0)
        p = e / jnp.maximum(jnp.sum(e, axis=-1, keepdims=True), TINY)
        outs.append(_mm(p.astype(BF16), vc))
        psum = psum + p
    ocmp_ref[...] = jnp.concatenate(outs, axis=1).astype(ocmp_ref.dtype)

    imp = _nt(ovlT_ref[...], psum, precision=HIGHEST)
    nb = lax.broadcasted_iota(I32, (NSB, 1), 0)
    qblk = (q0 + lax.broadcasted_iota(I32, (1, TQ), 1)) // SEL_BLK
    forced = (nb == 0) | (nb == qblk) | (nb == qblk - 1)
    cur = jnp.where(forced, FORCE_SCORE, jnp.where(nb > qblk, NEG_INF, imp))
    sel = jnp.zeros((NSB, TQ), F32)
    for _ in range(min(N_SEL, NSB)):
        m = jnp.max(cur, axis=0, keepdims=True)
        idx = jnp.min(jnp.where(cur == m, nb, NSB), axis=0, keepdims=True)
        hit = nb == idx
        sel = jnp.where(hit, 1.0, sel)
        cur = jnp.where(hit, BELOW_NEG_INF, cur)
    sel_ref[...] = sel.astype(sel_ref.dtype)


NEG_FLOOR = -1e29


def _nsa_main_kernel(q_ref, ksT_ref, vs_ref, kwT_ref, vw_ref, sel_ref, ocmp_ref, gate_ref, gexp_ref, o_ref):
    TQ = q_ref.shape[0]
    S = vs_ref.shape[0]
    TK = min(NSA_TK, S)
    J = NSA_J
    q0 = pl.program_id(2) * TQ
    q4 = jnp.concatenate([q_ref[:, j * HEAD_DIM:(j + 1) * HEAD_DIM] for j in range(J)], axis=0) * HEAD_DIM ** -0.5
    t = q0 + lax.broadcasted_iota(I32, (TQ, 1), 0)
    selT = sel_ref[...]

    def attend(carry, kT, v, bias):
        m, l, acc = carry
        s = _mm(q4, kT) + jnp.concatenate([bias] * J, axis=0)
        m_new = jnp.maximum(m, jnp.max(s, axis=-1, keepdims=True))
        e = jnp.exp(s - m_new)
        a = jnp.exp(m - m_new)
        return m_new, a * l + jnp.sum(e, axis=-1, keepdims=True), a * acc + _mm(e.astype(BF16), v)

    init = (jnp.full((J * TQ, 1), NEG_FLOOR, F32), jnp.zeros((J * TQ, 1), F32), jnp.zeros((J * TQ, HEAD_DIM), F32))

    def body(c, carry):
        k0 = pl.multiple_of(c * TK, TK)
        kpos = k0 + lax.broadcasted_iota(I32, (1, TK), 1)
        expand = jnp.where(kpos // SEL_BLK == lax.broadcasted_iota(I32, (selT.shape[0], 1), 0), 1.0, 0.0).astype(BF16)
        bias = jnp.where((_tn(selT, expand) > 0.5) & (kpos <= t), 0.0, NEG_INF)
        return attend(carry, ksT_ref[:, pl.ds(k0, TK)], vs_ref[pl.ds(k0, TK), :], bias)

    nchunks = (q0 + TQ - 1) // TK + 1
    _, l, acc = lax.fori_loop(0, nchunks, body, init)
    o_slc = acc / jnp.maximum(l, TINY)

    span = min(NSA_WINDOW + TQ, S)
    start = pl.multiple_of(jnp.clip(q0 - NSA_WINDOW, 0, S - span), TQ)
    dist = t - (start + lax.broadcasted_iota(I32, (1, span), 1))
    bias = jnp.where((dist >= 0) & (dist < NSA_WINDOW), 0.0, NEG_INF)
    _, l, acc = attend(init, kwT_ref[:, pl.ds(start, span)], vw_ref[pl.ds(start, span), :], bias)
    o_win = acc / jnp.maximum(l, TINY)

    unstack = lambda o: jnp.concatenate([o[j * TQ:(j + 1) * TQ, :] for j in range(J)], axis=1)
    g = _sigmoid(gate_ref[...])
    gx = [_mm(g, gexp_ref[c], precision=HIGHEST) for c in range(3)]
    out = gx[0] * ocmp_ref[...].astype(F32) + gx[1] * unstack(o_slc) + gx[2] * unstack(o_win)
    o_ref[...] = out.astype(o_ref.dtype)


def _nsa_mixer(x, B, S, w_in, cmp_pos_k, cmp_pos_v, cmp_wk, cmp_wv):
    T = B * S
    H, G, J, dh = NSA_HEADS, NSA_KV_HEADS, NSA_J, HEAD_DIM
    kv = G * dh
    wb = w_in.astype(BF16)
    col = lambda a, n: wb[:, a:a + n]
    o_q, o_kc, o_vc, o_ks, o_vs, o_kw, o_vw, o_g = np.cumsum([0, H * dh] + [kv] * 6).tolist()
    pos = jnp.arange(S)
    w_rope = jnp.concatenate([col(o_q, H * dh), col(o_ks, kv), col(o_kw, kv)], axis=1)
    roped = _proj(x, w_rope, out_dtype=BF16, tm=512, tn=512, rope_tables=_rope_tables(pos, 512), seq=S)
    w_plain = jnp.concatenate([col(o_vs, kv), col(o_vw, kv), jnp.pad(col(o_g, 3 * H), ((0, 0), (0, 128 - 3 * H)))], axis=1)
    plain = _proj(x, w_plain, out_dtype=F32, tm=512, tn=128)

    heads_T = lambda a: a.reshape(B, S, G, dh).transpose(0, 2, 3, 1)
    heads = lambda a: a.reshape(B, S, G, dh).transpose(0, 2, 1, 3)
    ksT = heads_T(roped[:, H * dh:H * dh + kv])
    kwT = heads_T(roped[:, H * dh + kv:])
    vs = heads(plain[:, :kv]).astype(BF16)
    vw = heads(plain[:, kv:2 * kv]).astype(BF16)
    gates = plain[:, 2 * kv:]

    nch = S // CMP_STRIDE
    cw = CMP_STRIDE * kv
    chk = _proj(x, col(o_kc, kv), out_dtype=BF16, tm=512, tn=kv).reshape(B, nch, cw)
    chv = _proj(x, col(o_vc, kv), out_dtype=BF16, tm=512, tn=kv).reshape(B, nch, cw)
    wk = cmp_wk.reshape(CMP_BLK * dh, dh)
    wk2 = jnp.concatenate([wk, _rot_cols(wk)], axis=1).astype(BF16)
    wv2 = cmp_wv.reshape(CMP_BLK * dh, dh).astype(BF16)
    eye = jnp.eye(G, dtype=BF16)

    def per_head(w, part):
        n = w.shape[1]
        w3 = w.reshape(2, CMP_STRIDE, dh, n)[part]
        return jnp.einsum('lde,gh->lgdhe', w3, eye).reshape(cw, G * n)

    flat8 = lambda p_: jnp.pad(p_.reshape(1, CMP_BLK * dh), ((0, 7), (0, 0))).astype(F32)
    cend = jnp.arange(nch) * CMP_STRIDE + CMP_BLK - 1
    cos_c, sin_c = _rope_tables(cend, dh)
    fixed1 = lambda b: (0, 0)
    full1 = lambda a: pl.BlockSpec(a.shape, fixed1)
    consts = (per_head(wk2, 0), per_head(wk2, 1), per_head(wv2, 0), per_head(wv2, 1), flat8(cmp_pos_k), flat8(cmp_pos_v),
              wk2, wv2, cos_c, sin_c)
    kc, vc = pl.pallas_call(
        _nsa_compress_kernel,
        name="nsa_compress",
        grid=(B,),
        in_specs=[pl.BlockSpec((None, nch, cw), lambda b: (b, 0, 0))] * 2 + [full1(c) for c in consts],
        out_specs=[pl.BlockSpec((None, G, nch, dh), lambda b: (b, 0, 0, 0))] * 2,
        out_shape=[jax.ShapeDtypeStruct((B, G, nch, dh), BF16)] * 2,
        scratch_shapes=[pltpu.VMEM((nch + 8, 2 * kv), F32)],
        compiler_params=_cparams("parallel"),
    )(chk, chv, *consts)

    nsb = S // SEL_BLK
    c_idx, s_idx = np.arange(nch), np.arange(nsb)
    ovl = ((c_idx[:, None] * CMP_STRIDE + CMP_BLK - 1 >= s_idx[None, :] * SEL_BLK)
           & (c_idx[:, None] * CMP_STRIDE < (s_idx[None, :] + 1) * SEL_BLK)).astype(np.float32)
    bgi = lambda b, g, i: (b, g, 0, 0)
    TS = min(NSA_TQ_SELECT, S)
    ns = S // TS
    sblk = lambda b, g, i: (b * ns + i, g)
    ocmp, sel = pl.pallas_call(
        _nsa_cmp_select_kernel,
        name="nsa_cmp_select",
        grid=(B, G, ns),
        in_specs=[pl.BlockSpec((TS, J * dh), sblk), pl.BlockSpec((None, None, nch, dh), bgi),
                  pl.BlockSpec((None, None, nch, dh), bgi), pl.BlockSpec((nsb, nch), lambda b, g, i: (0, 0))],
        out_specs=[pl.BlockSpec((TS, J * dh), sblk), pl.BlockSpec((None, None, nsb, TS), lambda b, g, i: (b, g, 0, i))],
        out_shape=[jax.ShapeDtypeStruct((T, H * dh), BF16), jax.ShapeDtypeStruct((B, G, nsb, S), BF16)],
        compiler_params=_cparams("parallel", "parallel", "parallel"),
    )(roped, kc, vc, jnp.asarray(ovl.T))
    TQ = NSA_TQ
    nq = S // TQ
    qblk = lambda b, g, i: (b * nq + i, g)

    gexp = np.zeros((G, 3, 128, J * dh), np.float32)
    for g in range(G):
        for c in range(3):
            for j in range(J):
                gexp[g, c, (g * J + j) * 3 + c, j * dh:(j + 1) * dh] = 1.0
    return pl.pallas_call(
        _nsa_main_kernel,
        name="nsa_main",
        grid=(B, G, nq),
        in_specs=[pl.BlockSpec((TQ, J * dh), qblk)] + [pl.BlockSpec((None, None, dh, S), bgi),
                                                       pl.BlockSpec((None, None, S, dh), bgi)] * 2 + [
            pl.BlockSpec((None, None, nsb, TQ), lambda b, g, i: (b, g, 0, i)),
            pl.BlockSpec((TQ, J * dh), qblk), pl.BlockSpec((TQ, 128), lambda b, g, i: (b * nq + i, 0)),
            pl.BlockSpec((None, 3, 128, J * dh), lambda b, g, i: (g, 0, 0, 0))],
        out_specs=pl.BlockSpec((TQ, J * dh), qblk),
        out_shape=jax.ShapeDtypeStruct((T, H * dh), BF16),
        compiler_params=_cparams("parallel", "parallel", "arbitrary"),
    )(roped, ksT, vs, kwT, vw, sel, ocmp, gates, jnp.asarray(gexp))


def _dil_kernel(q_ref, kp_ref, kc_ref, vp_ref, vc_ref, o_ref, lse_ref):
    NQ = q_ref.shape[0]
    prev_from = jnp.where(pl.program_id(2) == 0, NQ, 0)
    qi = lax.broadcasted_iota(I32, (NQ, 1), 0)
    kj = lax.broadcasted_iota(I32, (1, 2 * NQ), 1)
    dist = NQ + qi - kj
    valid = (dist >= 0) & (dist <= NQ) & (kj >= prev_from)
    lane = lax.broadcasted_iota(I32, (NQ, 128), 1)
    lse_all = jnp.zeros((NQ, 128), F32)
    for h in range(DIL_HEADS):
        sl = slice(h * HEAD_DIM, (h + 1) * HEAD_DIM)
        kb = jnp.concatenate([kp_ref[:, sl], kc_ref[:, sl]], axis=0)
        vb = jnp.concatenate([vp_ref[:, sl], vc_ref[:, sl]], axis=0)
        s = jnp.where(valid, _nt(q_ref[:, sl], kb) * HEAD_DIM ** -0.5, NEG_INF)
        m = jnp.max(s, axis=-1, keepdims=True)
        e = jnp.exp(s - m)
        den = jnp.sum(e, axis=-1, keepdims=True)
        o_ref[:, sl] = _mm((e / den).astype(BF16), vb).astype(o_ref.dtype)
        lse_all = jnp.where(lane == h, m + jnp.log(den), lse_all)
    lse_ref[...] = lse_all


def _proj_res_kernel(*refs, rope, dils):
    n_out = len(dils)
    x_ref, w_ref = refs[:2]
    rest = refs[2:]
    if rope:
        cos_ref, sin_ref = rest[:2]
        rest = rest[2:]
    out_refs, xb_ref, scr_ref = rest[:n_out], rest[n_out], rest[n_out + 1]
    tm, tn = x_ref.shape[0], w_ref.shape[1]

    @pl.when(pl.program_id(1) == 0)
    def _():
        xb_ref[...] = x_ref[...].astype(BF16)

    acc = _mm(xb_ref[...], w_ref[...])
    if rope:
        lane = lax.broadcasted_iota(I32, acc.shape, 1)
        lo = (lane % HEAD_DIM) < ROPE_HALF
        rot = jnp.where(lo, pltpu.roll(acc, tn - ROPE_HALF, 1), pltpu.roll(acc, ROPE_HALF, 1))
        acc = acc * cos_ref[...] + rot * sin_ref[...]
    if any(d > 1 for d in dils):
        for c in range(tn // 128):
            scr_ref[c] = acc[:, c * 128:(c + 1) * 128]
    for o_ref, dil in zip(out_refs, dils):
        if dil == 1:
            o_ref[...] = acc.astype(o_ref.dtype)
            continue
        for r in range(dil):
            for c in range(tn // 128):
                o_ref[:, r * tn + c * 128:r * tn + (c + 1) * 128] = (
                    scr_ref[c, pl.ds(r, tm // dil, stride=dil), :].astype(o_ref.dtype))


def _proj_res(x, w, dils, *, seq, rope, tm=512, tn=D_MODEL):
    T, K = x.shape
    N = w.shape[1]
    nparts = N // tn
    in_specs = [pl.BlockSpec((tm, K), lambda i, j: (i, 0)), pl.BlockSpec((K, tn), lambda i, j: (0, j))]
    args = [x, w]
    if rope:
        nseq = seq // tm
        in_specs += [pl.BlockSpec((tm, tn), lambda i, j: (i % nseq, 0))] * 2
        args += list(_rope_tables(jnp.arange(seq), tn))
    return pl.pallas_call(
        functools.partial(_proj_res_kernel, rope=rope, dils=dils),
        name="proj_residue_rope" if rope else "proj_residue",
        grid=(T // tm, nparts),
        in_specs=in_specs,
        out_specs=[pl.BlockSpec((tm // d, d * tn), lambda i, j: (i, j)) for d in dils],
        out_shape=[jax.ShapeDtypeStruct((T // d, nparts * d * tn), BF16) for d in dils],
        scratch_shapes=[pltpu.VMEM((tm, K), BF16), pltpu.VMEM((tn // 128, tm, 128), F32)],
        compiler_params=_cparams("parallel", "arbitrary"),
    )(*args)


def _dil_group(qk, v, B, S, dil):
    NQ = DIL_STEPS
    U = S // dil
    nb = U // NQ
    W = D_MODEL
    qk_view = qk.reshape(B, U, 2 * dil * W)
    v_view = v.reshape(B, U, dil * W)
    cur = lambda part: (lambda b, r, n: (b, n, part * dil + r))
    prev = lambda part: (lambda b, r, n: (b, jnp.maximum(n - 1, 0), part * dil + r))
    blk = lambda f: pl.BlockSpec((None, NQ, W), f)
    o, lse = pl.pallas_call(
        _dil_kernel,
        name=f"dilated_attn_{dil}",
        grid=(B, dil, nb),
        in_specs=[blk(cur(0)), blk(prev(1)), blk(cur(1)), blk(prev(0)), blk(cur(0))],
        out_specs=[pl.BlockSpec((None, NQ, W), lambda b, r, n: (b, n, r)),
                   pl.BlockSpec((None, NQ, 128), lambda b, r, n: (b, n, r))],
        out_shape=[jax.ShapeDtypeStruct((B, U, dil * W), BF16), jax.ShapeDtypeStruct((B, U, dil * 128), F32)],
        compiler_params=_cparams("parallel", "parallel", "arbitrary"),
    )(qk_view, qk_view, qk_view, v_view, v_view)
    return o.reshape(B * U, dil * W), lse.reshape(B * U, dil * 128)


def _dil_outproj_ln_kernel(o0_ref, o1_ref, o2_ref, l0_ref, l1_ref, l2_ref, hexp_ref, w_ref, x_ref, g_ref, b_ref, out_ref,
                           oscr_ref, lscr_ref, *, dils):
    tm, W = x_ref.shape

    def natural(o_ref, l_ref, dil):
        if dil == 1:
            return o_ref[...].astype(F32), l_ref[...]
        n = tm // dil
        for r in range(dil):
            lscr_ref[0, pl.ds(r, n, stride=dil), :] = l_ref[:, r * 128:(r + 1) * 128]
            for c in range(W // 128):
                oscr_ref[c, pl.ds(r, n, stride=dil), :] = o_ref[:, r * W + c * 128:r * W + (c + 1) * 128].astype(F32)
        return jnp.concatenate([oscr_ref[c] for c in range(W // 128)], axis=1), lscr_ref[0]

    os_, ls = zip(*[natural(o, l, d) for o, l, d in zip((o0_ref, o1_ref, o2_ref), (l0_ref, l1_ref, l2_ref), dils)])
    m = jnp.maximum(jnp.maximum(ls[0], ls[1]), ls[2])
    es = [jnp.exp(l - m) for l in ls]
    tot = es[0] + es[1] + es[2]
    y = jnp.zeros((tm, W), F32)
    for e, o in zip(es, os_):
        y = y + _mm(e / tot, hexp_ref[...], precision=HIGHEST) * o
    z = ALPHA * x_ref[...] + _mm(y.astype(BF16), w_ref[...])
    out_ref[...] = _ln_rows(z, g_ref[...], b_ref[...])


def _dilated_layer(x, B, S, w_in, w_out, g, b, *, tm=256):
    T, D = x.shape
    dils = tuple(d for _, d in DIL_GROUPS)
    wb = w_in.astype(BF16)
    vals = _proj_res(x, wb[:, 2 * len(dils) * D:], dils, seq=S, rope=False)
    outs = []
    for gi, dil in enumerate(dils):
        qk, = _proj_res(x, wb[:, 2 * gi * D:(2 * gi + 2) * D], (dil,), seq=S, rope=True)
        outs.append(_dil_group(qk, vals[gi], B, S, dil))
    hexp = np.zeros((128, D), np.float32)
    for h in range(DIL_HEADS):
        hexp[h, h * HEAD_DIM:(h + 1) * HEAD_DIM] = 1.0
    row = lambda i: (i, 0)
    fixed = lambda i: (0, 0)
    return pl.pallas_call(
        functools.partial(_dil_outproj_ln_kernel, dils=dils),
        name="dilated_outproj_ln",
        grid=(T // tm,),
        in_specs=[pl.BlockSpec((tm // d, d * D), row) for d in dils] + [pl.BlockSpec((tm // d, d * 128), row) for d in dils]
        + [pl.BlockSpec((128, D), fixed), pl.BlockSpec((D, D), fixed), pl.BlockSpec((tm, D), row),
           pl.BlockSpec((1, D), fixed), pl.BlockSpec((1, D), fixed)],
        out_specs=pl.BlockSpec((tm, D), row),
        out_shape=jax.ShapeDtypeStruct((T, D), F32),
        scratch_shapes=[pltpu.VMEM((D // 128, tm, 128), F32), pltpu.VMEM((1, tm, 128), F32)],
        compiler_params=_cparams("parallel"),
    )(outs[0][0], outs[1][0], outs[2][0], outs[0][1], outs[1][1], outs[2][1], jnp.asarray(hexp),
      w_out.astype(BF16), x, g.reshape(1, D), b.reshape(1, D))


def _pool_ln_kernel(x_ref, halo_ref, w_ref, scale_ref, g_ref, b_ref, o_ref, ext_ref):
    TS = x_ref.shape[0]
    s = pl.program_id(1)
    x = x_ref[...]
    ext_ref[0:POOL_HALO, :] = jnp.where(s == 0, 0.0, halo_ref[...])
    ext_ref[POOL_HALO:POOL_HALO + TS, :] = x
    cnt = (s * TS + lax.broadcasted_iota(I32, (TS, 1), 0) + 1).astype(F32)
    ys = []
    for gi, w in enumerate(POOL_WINDOWS):
        sl = slice(gi * POOL_GROUP, (gi + 1) * POOL_GROUP)
        xg = x[:, sl]
        tot = xg
        for j in range(1, w):
            tot = tot + ext_ref[POOL_HALO - j:POOL_HALO - j + TS, sl]
        mean = tot / jnp.minimum(cnt, float(w))
        ys.append(_mm((mean - xg).astype(BF16), w_ref[gi]))
    y = jnp.concatenate(ys, axis=1) * scale_ref[...]
    o_ref[...] = _ln_rows(ALPHA * x + y, g_ref[...], b_ref[...])


def _pool_layer(x, B, S, w_grp, scale, g, b, *, ts=512):
    T, D = x.shape
    ns = S // ts
    hb = ts // POOL_HALO
    fixed = lambda bb, s: (0, 0)
    return pl.pallas_call(
        _pool_ln_kernel,
        name="pool_ln",
        grid=(B, ns),
        in_specs=[pl.BlockSpec((ts, D), lambda bb, s: (bb * ns + s, 0)),
                  pl.BlockSpec((POOL_HALO, D), lambda bb, s: (jnp.maximum((bb * ns + s) * hb - 1, 0), 0)),
                  pl.BlockSpec((len(POOL_WINDOWS), POOL_GROUP, POOL_GROUP), lambda bb, s: (0, 0, 0)),
                  pl.BlockSpec((1, D), fixed), pl.BlockSpec((1, D), fixed), pl.BlockSpec((1, D), fixed)],
        out_specs=pl.BlockSpec((ts, D), lambda bb, s: (bb * ns + s, 0)),
        out_shape=jax.ShapeDtypeStruct((T, D), F32),
        scratch_shapes=[pltpu.VMEM((ts + POOL_HALO, D), F32)],
        compiler_params=_cparams("parallel", "arbitrary"),
    )(x, x, w_grp.astype(BF16), scale.reshape(1, D), g.reshape(1, D), b.reshape(1, D))


U32 = jnp.uint32
HI16 = 0xFFFF0000
ROW_WORDS = D_MODEL // 2


def _pack_halves(v):
    h = v.shape[1] // 2
    bits = lambda a: lax.bitcast_convert_type(a.astype(BF16).astype(F32), U32)
    return (bits(v[:, :h]) >> 16) | (bits(v[:, h:]) & U32(HI16))


def _unpack_halves(w):
    lo = lax.bitcast_convert_type(w << 16, F32)
    hi = lax.bitcast_convert_type(w & U32(HI16), F32)
    return jnp.concatenate([lo, hi], axis=1)


def _router_kernel(x_ref, wT_ref, bias_ref, triu_ref, eidx_ref, gw_ref, rank_ref, cnt_ref, xp_ref, carry_ref):
    E = N_EXPERTS
    per = E // N_EXPERT_GROUPS
    TM = x_ref.shape[0]

    @pl.when(pl.program_id(0) == 0)
    def _():
        carry_ref[...] = jnp.zeros_like(carry_ref)

    x = x_ref[...]
    xp_ref[...] = _pack_halves(x)

    scores = _sigmoid(_nt(wT_ref[...], x, precision=HIGHEST))
    biased = scores + bias_ref[...]
    eio = lax.broadcasted_iota(I32, (E, TM), 0)

    gio = lax.broadcasted_iota(I32, (per, TM), 0)
    gscore = []
    for gidx in range(N_EXPERT_GROUPS):
        slab = biased[gidx * per:(gidx + 1) * per, :]
        m1 = jnp.max(slab, axis=0, keepdims=True)
        i1 = jnp.min(jnp.where(slab == m1, gio, per), axis=0, keepdims=True)
        m2 = jnp.max(jnp.where(gio == i1, BELOW_NEG_INF, slab), axis=0, keepdims=True)
        gscore.append(m1 + m2)
    slabs = []
    for gidx in range(N_EXPERT_GROUPS):
        beat = jnp.zeros((1, TM), F32)
        for o in range(N_EXPERT_GROUPS):
            if o == gidx:
                continue
            wins = (gscore[o] >= gscore[gidx]) if o < gidx else (gscore[o] > gscore[gidx])
            beat = beat + jnp.where(wins, 1.0, 0.0)
        keep = beat < float(TOPK_GROUPS)
        slabs.append(jnp.where(keep, biased[gidx * per:(gidx + 1) * per, :], NEG_INF))
    cur = jnp.concatenate(slabs, axis=0)

    picked = jnp.zeros((E, TM), F32)
    idxs, vals = [], []
    for _ in range(TOP_K):
        m = jnp.max(cur, axis=0, keepdims=True)
        idx = jnp.min(jnp.where(cur == m, eio, E), axis=0, keepdims=True)
        hit = eio == idx
        picked = jnp.where(hit, 1.0, picked)
        cur = jnp.where(hit, BELOW_NEG_INF, cur)
        idxs.append(idx)
        vals.append(jnp.sum(jnp.where(hit, scores, 0.0), axis=0, keepdims=True))
    total = vals[0]
    for v in vals[1:]:
        total = total + v

    pos = _mm(picked.astype(BF16), triu_ref[...]) + carry_ref[...]
    for k in range(TOP_K):
        eidx_ref[k:k + 1, :] = idxs[k]
        gw_ref[k:k + 1, :] = vals[k] / total * ROUTED_SCALE
        rank_ref[k:k + 1, :] = jnp.sum(jnp.where(eio == idxs[k], pos, 0.0), axis=0, keepdims=True).astype(I32)
    carry_ref[...] = carry_ref[...] + jnp.sum(picked, axis=1, keepdims=True)
    cnt_ref[...] = carry_ref[...].astype(I32)


def _router(x, router_w, router_bias, *, tm=512):
    T, D = x.shape
    E = N_EXPERTS
    triu = jnp.triu(jnp.ones((tm, tm), F32), k=1).astype(BF16)
    col = lambda i: (0, i)
    fixed = lambda i: (0, 0)
    return pl.pallas_call(
        _router_kernel,
        name="moe_router",
        grid=(T // tm,),
        in_specs=[pl.BlockSpec((tm, D), lambda i: (i, 0)), pl.BlockSpec((E, D), fixed),
                  pl.BlockSpec((E, 1), fixed), pl.BlockSpec((tm, tm), fixed)],
        out_specs=[pl.BlockSpec((TOP_K, tm), col), pl.BlockSpec((TOP_K, tm), col), pl.BlockSpec((TOP_K, tm), col),
                   pl.BlockSpec((E, 1), fixed), pl.BlockSpec((tm, ROW_WORDS), lambda i: (i, 0))],
        out_shape=[jax.ShapeDtypeStruct((TOP_K, T), I32), jax.ShapeDtypeStruct((TOP_K, T), F32),
                   jax.ShapeDtypeStruct((TOP_K, T), I32), jax.ShapeDtypeStruct((E, 1), I32),
                   jax.ShapeDtypeStruct((T, ROW_WORDS), U32)],
        scratch_shapes=[pltpu.VMEM((E, 1), F32)],
        compiler_params=_cparams("arbitrary"),
    )(x, router_w.T.astype(F32), router_bias.reshape(E, 1).astype(F32), triu)


def _tile_indices(dest_hbm, idx_ref, sem_idx, n_idx):
    i = pl.program_id(0)

    def idx_copy(step):
        slot = step % 2
        return pltpu.make_async_copy(dest_hbm.at[step], idx_ref.at[pl.ds(slot * n_idx, n_idx)], sem_idx.at[slot])

    @pl.when(i == 0)
    def _():
        idx_copy(0).start()

    idx_copy(i).wait()

    @pl.when(i + 1 < pl.num_programs(0))
    def _():
        idx_copy(i + 1).start()

    return (i % 2) * n_idx


def _dispatch_kernel(cnt_ref, pstart_ref, dest_hbm, x_ref, rows_ref, idx_ref, zero_ref, sem_idx, sem_rows, sem_zero):
    TD = x_ref.shape[0]

    @pl.when(pl.program_id(0) == 0)
    def _():
        zero_ref[...] = jnp.zeros_like(zero_ref)

        def per_expert(e, c):
            n = cnt_ref[e]
            base = pstart_ref[e]
            pad_to = (n + EXP_BLK - 1) // EXP_BLK * EXP_BLK
            fill = lambda r: pltpu.make_async_copy(zero_ref.at[pl.ds(0, 1)], rows_ref.at[base + r], sem_zero)

            def start(r, c2):
                fill(r).start()
                return c2

            def wait(r, c2):
                fill(r).wait()
                return c2

            lax.fori_loop(n, pad_to, start, 0)
            lax.fori_loop(n, pad_to, wait, 0)
            return c

        lax.fori_loop(0, N_EXPERTS, per_expert, 0)

    base = _tile_indices(dest_hbm, idx_ref, sem_idx, TOP_K * TD)
    scatter = lambda t, k: pltpu.make_async_copy(x_ref.at[pl.ds(t, 1)], rows_ref.at[idx_ref[base + k * TD + t]], sem_rows)

    def issue(t, c):
        for k in range(TOP_K):
            scatter(t, k).start(priority=k % 2)
        return c

    def drain(t, c):
        for k in range(TOP_K):
            scatter(t, k).wait()
        return c

    lax.fori_loop(0, TD, issue, 0, unroll=4)
    lax.fori_loop(0, TD, drain, 0, unroll=4)


def _dispatch(xp, dest_tiles, counts, pad_start, n_rows, *, td):
    T = xp.shape[0]
    return pl.pallas_call(
        _dispatch_kernel,
        name="moe_dispatch",
        grid_spec=pltpu.PrefetchScalarGridSpec(
            num_scalar_prefetch=2,
            grid=(T // td,),
            in_specs=[pl.BlockSpec(memory_space=pl.ANY), pl.BlockSpec((td, ROW_WORDS), lambda i, c, p: (i, 0))],
            out_specs=pl.BlockSpec(memory_space=pl.ANY),
            scratch_shapes=[pltpu.SMEM((2 * TOP_K * td,), I32), pltpu.VMEM((8, ROW_WORDS), U32),
                            pltpu.SemaphoreType.DMA((2,)), pltpu.SemaphoreType.DMA(()), pltpu.SemaphoreType.DMA(())],
        ),
        out_shape=jax.ShapeDtypeStruct((n_rows, 1, ROW_WORDS), U32),
        compiler_params=_cparams("arbitrary"),
    )(counts, pad_start, dest_tiles, xp)


def _expert_kernel(be_ref, nu_ref, rows_hbm, wg_ref, wu_ref, wd_ref, out_hbm, xbuf, obuf, wgu_bf, wd_bf, sem_in, sem_out):
    i = pl.program_id(0)
    nu = nu_ref[0]

    @pl.when((i < nu) & ((i == 0) | (be_ref[i] != be_ref[jnp.maximum(i - 1, 0)])))
    def _():
        wgu_bf[:, :EXPERT_FF] = wg_ref[0].astype(BF16)
        wgu_bf[:, EXPERT_FF:] = wu_ref[0].astype(BF16)
        wd_bf[...] = wd_ref[0].astype(BF16)

    blk = lambda step: pl.ds(pl.multiple_of(step * EXP_BLK, EXP_BLK), EXP_BLK)
    in_copy = lambda step: pltpu.make_async_copy(rows_hbm.at[blk(step), 0], xbuf.at[step % 2], sem_in.at[step % 2])
    out_copy = lambda step: pltpu.make_async_copy(obuf.at[step % 2], out_hbm.at[blk(step), 0], sem_out.at[step % 2])

    @pl.when(i == 0)
    def _():
        in_copy(0).start()

    @pl.when(i < nu)
    def _():
        in_copy(i).wait()

        @pl.when(i + 1 < nu)
        def _():
            in_copy(i + 1).start()

        @pl.when(i >= 2)
        def _():
            out_copy(i - 2).wait()

        slot = i % 2
        gu = _mm(_unpack_halves(xbuf[slot]).astype(BF16), wgu_bf[...])
        h = _silu(gu[:, :EXPERT_FF]) * gu[:, EXPERT_FF:]
        obuf[slot] = _pack_halves(_mm(h.astype(BF16), wd_bf[...]))
        out_copy(i).start()

    @pl.when(i == pl.num_programs(0) - 1)
    def _():
        @pl.when(nu >= 2)
        def _():
            out_copy(nu - 2).wait()

        out_copy(nu - 1).wait()


def _experts(rows, blk_expert, n_used, w_gate, w_up, w_down):
    R = rows.shape[0]
    D = w_down.shape[-1]
    n_blk = R // EXP_BLK
    expert = lambda i, be, nu: (be[jnp.minimum(i, nu[0] - 1)], 0, 0)
    return pl.pallas_call(
        _expert_kernel,
        name="moe_experts",
        grid_spec=pltpu.PrefetchScalarGridSpec(
            num_scalar_prefetch=2,
            grid=(n_blk,),
            in_specs=[pl.BlockSpec(memory_space=pl.ANY), pl.BlockSpec((1, D, EXPERT_FF), expert),
                      pl.BlockSpec((1, D, EXPERT_FF), expert), pl.BlockSpec((1, EXPERT_FF, D), expert)],
            out_specs=pl.BlockSpec(memory_space=pl.ANY),
            scratch_shapes=[pltpu.VMEM((2, EXP_BLK, ROW_WORDS), U32), pltpu.VMEM((2, EXP_BLK, ROW_WORDS), U32),
                            pltpu.VMEM((D, 2 * EXPERT_FF), BF16), pltpu.VMEM((EXPERT_FF, D), BF16),
                            pltpu.SemaphoreType.DMA((2,)), pltpu.SemaphoreType.DMA((2,))],
        ),
        out_shape=jax.ShapeDtypeStruct((R, 1, ROW_WORDS), U32),
        compiler_params=_cparams("arbitrary"),
    )(blk_expert, n_used, rows, w_gate, w_up, w_down)


def _combine_kernel(dest_hbm, gw_ref, x_ref, rows_ref, wsgu_ref, wsd_ref, g_ref, b_ref, p_ref, wp_ref, wpg_ref,
                    o_ref, idx_ref, buf_ref, sem_idx, sem_rows):
    TM = x_ref.shape[0]
    n_idx = TOP_K * TM
    i = pl.program_id(0)
    n = pl.num_programs(0)

    def idx_copy(step):
        slot = step % 3
        return pltpu.make_async_copy(dest_hbm.at[step], idx_ref.at[pl.ds(slot * n_idx, n_idx)], sem_idx.at[slot])

    def row_loop(step, slot, start):
        base = (step % 3) * n_idx

        def body(t, c):
            for k in range(TOP_K):
                cp = pltpu.make_async_copy(rows_ref.at[idx_ref[base + k * TM + t]], buf_ref.at[slot, k, pl.ds(t, 1)],
                                           sem_rows.at[slot])
                if start:
                    cp.start(priority=k % 2)
                else:
                    cp.wait()
            return c

        lax.fori_loop(0, TM, body, 0, unroll=4)

    def for_parity(step, fn):
        for slot in range(2):
            pl.when(step % 2 == slot)(functools.partial(fn, slot))

    @pl.when(i == 0)
    def _():
        idx_copy(0).start()
        idx_copy(0).wait()
        row_loop(0, 0, True)

        @pl.when(n > 1)
        def _():
            idx_copy(1).start()

    @pl.when(i + 1 < n)
    def _():
        idx_copy(i + 1).wait()

        @pl.when(i + 2 < n)
        def _():
            idx_copy(i + 2).start()

        for_parity(i + 1, lambda slot: row_loop(i + 1, slot, True))

    x = x_ref[...]
    gu = _mm(x.astype(BF16), wsgu_ref[...])
    ff = gu.shape[1] // 2
    f = _mm((_silu(gu[:, :ff]) * gu[:, ff:]).astype(BF16), wsd_ref[...])
    for_parity(i, lambda slot: row_loop(i, slot, False))
    gw = gw_ref[...]
    cur = i % 2
    for k in range(TOP_K):
        f = f + gw[:, k:k + 1] * _unpack_halves(buf_ref[cur, k])
    x2 = _ln_rows(ALPHA * x + f, g_ref[...], b_ref[...])
    gate = _sigmoid(_mm(x2.astype(BF16), wpg_ref[...]))
    o_ref[...] = x2 + gate * _mm(p_ref[...].astype(BF16), wp_ref[...])


def _combine(dest_tiles, gw, x, rows_out, wsgu, wsd, g, b, p, wp, wpg, *, tm):
    T, D = x.shape
    row = lambda i: (i, 0)
    fixed = lambda i: (0, 0)
    full = lambda a: pl.BlockSpec(a.shape, fixed)
    return pl.pallas_call(
        _combine_kernel,
        name="moe_combine",
        grid=(T // tm,),
        in_specs=[pl.BlockSpec(memory_space=pl.ANY), pl.BlockSpec((tm, TOP_K), row), pl.BlockSpec((tm, D), row),
                  pl.BlockSpec(memory_space=pl.ANY), full(wsgu), full(wsd), pl.BlockSpec((1, D), fixed),
                  pl.BlockSpec((1, D), fixed), pl.BlockSpec((tm, PLE_DIM), row), full(wp), full(wpg)],
        out_specs=pl.BlockSpec((tm, D), row),
        out_shape=jax.ShapeDtypeStruct((T, D), F32),
        scratch_shapes=[pltpu.SMEM((3 * TOP_K * tm,), I32), pltpu.VMEM((2, TOP_K, tm, ROW_WORDS), U32),
                        pltpu.SemaphoreType.DMA((3,)), pltpu.SemaphoreType.DMA((2,))],
        compiler_params=_cparams("arbitrary"),
    )(dest_tiles, gw, x, rows_out, wsgu, wsd, g.reshape(1, D), b.reshape(1, D), p, wp, wpg)


MOE_TILE = 256


def _moe_ple_layer(x, p, router_w, router_bias, w_gate, w_up, w_down, ws_gate, ws_up, ws_down, g, b, ple_w, ple_gate_w):
    T, D = x.shape
    eidx, gw, rank, counts, xp = _router(x, router_w, router_bias)
    counts = counts.reshape(N_EXPERTS)
    padded = (counts + EXP_BLK - 1) // EXP_BLK * EXP_BLK
    pad_end = jnp.cumsum(padded)
    pad_start = pad_end - padded
    n_blk = T * TOP_K // EXP_BLK + N_EXPERTS
    e_iota = jnp.arange(N_EXPERTS, dtype=I32)
    dest = rank + jnp.sum(jnp.where(eidx[..., None] == e_iota, pad_start, 0), axis=-1)
    tm = MOE_TILE
    dest_tiles = dest.reshape(TOP_K, T // tm, tm).transpose(1, 0, 2).reshape(T // tm, TOP_K * tm)
    blk_first = jnp.arange(n_blk, dtype=I32)[:, None] * EXP_BLK
    blk_expert = jnp.minimum(jnp.sum((pad_end[None, :] <= blk_first).astype(I32), axis=-1), N_EXPERTS - 1)
    n_used = (pad_end[-1:] // EXP_BLK).astype(I32)

    rows = _dispatch(xp, dest_tiles, counts.astype(I32), pad_start.astype(I32), n_blk * EXP_BLK, td=tm)
    rows_out = _experts(rows, blk_expert, n_used, w_gate, w_up, w_down)
    wsgu = jnp.concatenate([ws_gate, ws_up], axis=-1).astype(BF16)
    return _combine(dest_tiles, gw.T, x, rows_out, wsgu, ws_down.astype(BF16), g, b, p,
                    ple_w.astype(BF16), ple_gate_w.astype(BF16), tm=tm)


def kernel(x, p, ln_g, ln_b, mlstm_w_in, mlstm_conv, mlstm_ig_bias, mlstm_fg_bias, mlstm_norm_g, mlstm_w_out, nsa_w_in, nsa_cmp_pos_k, nsa_cmp_pos_v, nsa_cmp_wk, nsa_cmp_wv, nsa_w_out, dil_w_in, dil_w_out, pool_w, pool_scale, router_w, router_bias, exp_w_gate, exp_w_up, exp_w_down, sh_w_gate, sh_w_up, sh_w_down, ple_w, ple_gate_w):
    B, S, D = x.shape
    T = B * S
    xf = x.reshape(T, D)
    pf = p.reshape(DEPTH, T, PLE_DIM)
    for i in range(DEPTH):
        kind, j = i % 4, i // 4
        g1, b1 = ln_g[i, 0], ln_b[i, 0]
        if kind == 0:
            y = _mlstm_mixer(xf, B, S, mlstm_w_in[j], mlstm_conv[j], mlstm_ig_bias[j], mlstm_fg_bias[j], mlstm_norm_g[j])
            xf = _outproj_ln(y, mlstm_w_out[j].astype(BF16), xf, g1, b1)
        elif kind == 1:
            y = _nsa_mixer(xf, B, S, nsa_w_in[j], nsa_cmp_pos_k[j], nsa_cmp_pos_v[j], nsa_cmp_wk[j], nsa_cmp_wv[j])
            xf = _outproj_ln(y, nsa_w_out[j].astype(BF16), xf, g1, b1)
        elif kind == 2:
            xf = _dilated_layer(xf, B, S, dil_w_in[j], dil_w_out[j], g1, b1)
        else:
            xf = _pool_layer(xf, B, S, pool_w[j], pool_scale[j], g1, b1)
        xf = _moe_ple_layer(xf, pf[i], router_w[i], router_bias[i], exp_w_gate[i], exp_w_up[i], exp_w_down[i],
                            sh_w_gate[i], sh_w_up[i], sh_w_down[i], ln_g[i, 1], ln_b[i, 1], ple_w[i], ple_gate_w[i])
    return xf.reshape(B, S, D)
```

```python
import functools

import numpy as np
import jax
import jax.numpy as jnp
from jax import lax
from jax.experimental import pallas as pl
from jax.experimental.pallas import tpu as pltpu

F32 = jnp.float32
BF16 = jnp.bfloat16
I32 = jnp.int32
HIGHEST = lax.Precision.HIGHEST

D_MODEL = 1024
DEPTH = 4
ALPHA = (2.0 * DEPTH) ** 0.25
LN_EPS = 1e-5
NEG_INF = -1e30
TINY = 1e-30
BELOW_NEG_INF = -3e38
ROPE_THETA = 10000.0
HEAD_DIM = 64
ROPE_HALF = HEAD_DIM // 2

MLSTM_HEADS = 8
MLSTM_QK_DIM = 64
MLSTM_V_DIM = 128
MLSTM_CONV = 4
MLSTM_L = 256

NSA_HEADS = 16
NSA_KV_HEADS = 4
NSA_J = NSA_HEADS // NSA_KV_HEADS
CMP_STRIDE = 16
CMP_BLK = 32
SEL_BLK = 64
N_SEL = 16
NSA_WINDOW = 512
FORCE_SCORE = 1e4
NSA_TQ = 128
NSA_TQ_SELECT = 256
NSA_TK = 512

DIL_HEADS = 16
DIL_GROUPS = ((128, 1), (512, 4), (2048, 16))
DIL_STEPS = 128

POOL_WINDOWS = (2, 4, 8, 16)
POOL_GROUP = 256
POOL_HALO = 16

N_EXPERTS = 64
TOP_K = 8
N_EXPERT_GROUPS = 8
TOPK_GROUPS = 4
EXPERT_FF = 256
ROUTED_SCALE = 2.5
EXP_BLK = 512
PLE_DIM = 256

VMEM_LIMIT = 48 * 1024 * 1024


def _cparams(*sem):
    return pltpu.CompilerParams(dimension_semantics=sem, vmem_limit_bytes=VMEM_LIMIT)


def _nt(a, b, **kw):
    return lax.dot_general(a, b, (((1,), (1,)), ((), ())), preferred_element_type=F32, **kw)


def _tn(a, b, **kw):
    return lax.dot_general(a, b, (((0,), (0,)), ((), ())), preferred_element_type=F32, **kw)


def _mm(a, b, **kw):
    return jnp.dot(a, b, preferred_element_type=F32, **kw)


def _sigmoid(z):
    return 1.0 / (1.0 + jnp.exp(-z))


def _silu(z):
    return z * _sigmoid(z)


def _ln_rows(z, g, b):
    mu = jnp.mean(z, axis=-1, keepdims=True)
    d = z - mu
    var = jnp.mean(d * d, axis=-1, keepdims=True)
    return d * lax.rsqrt(var + LN_EPS) * g + b


def _proj_kernel(*refs, rope):
    if rope:
        x_ref, w_ref, cos_ref, sin_ref, o_ref, xb_ref = refs
    else:
        x_ref, w_ref, o_ref, xb_ref = refs

    @pl.when(pl.program_id(1) == 0)
    def _():
        xb_ref[...] = x_ref[...].astype(BF16)

    acc = _mm(xb_ref[...], w_ref[...])
    if rope:
        tn = acc.shape[1]
        lane = lax.broadcasted_iota(I32, acc.shape, 1)
        lo = (lane % HEAD_DIM) < ROPE_HALF
        rot = jnp.where(lo, pltpu.roll(acc, tn - ROPE_HALF, 1), pltpu.roll(acc, ROPE_HALF, 1))
        acc = acc * cos_ref[...] + rot * sin_ref[...]
    o_ref[...] = acc.astype(o_ref.dtype)


def _proj(x, w, *, out_dtype, tm, tn, rope_tables=None, seq=None):
    T, K = x.shape
    N = w.shape[1]
    assert T % tm == 0 and N % tn == 0
    in_specs = [pl.BlockSpec((tm, K), lambda i, j: (i, 0)), pl.BlockSpec((K, tn), lambda i, j: (0, j))]
    args = [x, w]
    if rope_tables is not None:
        nseq = seq // tm
        in_specs += [pl.BlockSpec((tm, tn), lambda i, j: (i % nseq, 0))] * 2
        args += list(rope_tables)
    return pl.pallas_call(
        functools.partial(_proj_kernel, rope=rope_tables is not None),
        name="proj_rope" if rope_tables is not None else "proj",
        grid=(T // tm, N // tn),
        in_specs=in_specs,
        out_specs=pl.BlockSpec((tm, tn), lambda i, j: (i, j)),
        out_shape=jax.ShapeDtypeStruct((T, N), out_dtype),
        scratch_shapes=[pltpu.VMEM((tm, K), BF16)],
        compiler_params=_cparams("parallel", "arbitrary"),
    )(*args)


def _rope_tables(pos, width):
    inv = ROPE_THETA ** (-jnp.arange(ROPE_HALF, dtype=F32) / ROPE_HALF)
    ang = pos.astype(F32)[:, None] * inv[None, :]
    cos, sin = jnp.cos(ang), jnp.sin(ang)
    cos64 = jnp.concatenate([cos, cos], -1)
    sin64 = jnp.concatenate([-sin, sin], -1)
    rep = width // HEAD_DIM
    return jnp.tile(cos64, (1, rep)), jnp.tile(sin64, (1, rep))


def _outproj_ln_kernel(y_ref, w_ref, x_ref, g_ref, b_ref, o_ref):
    y = _mm(y_ref[...].astype(BF16), w_ref[...])
    o_ref[...] = _ln_rows(ALPHA * x_ref[...] + y, g_ref[...], b_ref[...])


def _outproj_ln(y, w, x, g, b, *, tm=256):
    T, D = x.shape
    K = y.shape[1]
    row = lambda i: (i, 0)
    fixed = lambda i: (0, 0)
    return pl.pallas_call(
        _outproj_ln_kernel,
        name="outproj_ln",
        grid=(T // tm,),
        in_specs=[pl.BlockSpec((tm, K), row), pl.BlockSpec((K, D), fixed), pl.BlockSpec((tm, D), row),
                  pl.BlockSpec((1, D), fixed), pl.BlockSpec((1, D), fixed)],
        out_specs=pl.BlockSpec((tm, D), row),
        out_shape=jax.ShapeDtypeStruct((T, D), F32),
        compiler_params=_cparams("parallel"),
    )(y, w, x, g.reshape(1, D), b.reshape(1, D))


def _log_sigmoid(z):
    return jnp.minimum(z, 0.0) - jnp.log(1.0 + jnp.exp(-jnp.abs(z)))


def _mlstm_kernel(qk_ref, v_ref, o_ref, gc_ref, gr_ref, convw_ref, bias_c_ref, bias_r_ref, ng_ref,
                  tri_ref, triT_ref, out_ref, C_ref, n_ref, m_ref, ext_ref):
    L = qk_ref.shape[0]
    H, dk, dv = MLSTM_HEADS, MLSTM_QK_DIM, MLSTM_V_DIM

    @pl.when(pl.program_id(1) == 0)
    def _():
        C_ref[...] = jnp.zeros_like(C_ref)
        n_ref[...] = jnp.zeros_like(n_ref)
        m_ref[...] = jnp.zeros_like(m_ref)
        ext_ref[0:8, :] = jnp.zeros((8, ext_ref.shape[1]), F32)

    cur = qk_ref[...]
    ext_ref[8:8 + L, :] = cur
    acc = convw_ref[3:4, :] * cur
    for j in range(MLSTM_CONV - 1):
        acc = acc + convw_ref[j:j + 1, :] * ext_ref[5 + j:5 + j + L, :]
    ext_ref[0:8, :] = cur[L - 8:L, :]
    qk = _silu(acc)

    gc = gc_ref[...] + bias_c_ref[...]
    gr = gr_ref[...] + bias_r_ref[...]
    b_col = _mm(tri_ref[...], _log_sigmoid(gc), precision=HIGHEST)
    b_row = _mm(_log_sigmoid(gr[H:2 * H, :]), triT_ref[...], precision=HIGHEST)
    ig_row = gr[0:H, :]
    tri = lax.broadcasted_iota(I32, (L, L), 0) >= lax.broadcasted_iota(I32, (L, L), 1)

    for h in range(H):
        q = (qk[:, h * dk:(h + 1) * dk] * dk ** -0.5).astype(BF16)
        k = qk[:, H * dk + h * dk:H * dk + (h + 1) * dk]
        kb = k.astype(BF16)
        v = v_ref[:, h * dv:(h + 1) * dv].astype(BF16)
        b_c = b_col[:, H + h:H + h + 1]
        ig_c = gc[:, h:h + 1]
        b_r = b_row[h:h + 1, :]
        m_prev = m_ref[h:h + 1, 0:1]
        C = C_ref[h]
        n = n_ref[h:h + 1, :]

        logD = jnp.where(tri, b_c - b_r + ig_row[h:h + 1, :], NEG_INF)
        inter = b_c + m_prev
        m_t = jnp.maximum(inter, jnp.max(logD, axis=-1, keepdims=True))
        s = _nt(q, kb) * jnp.exp(logD - m_t)
        w_inter = jnp.exp(inter - m_t)
        num = w_inter * _mm(q, C.astype(BF16)) + _mm(s.astype(BF16), v)
        den = w_inter * jnp.sum(q.astype(F32) * n, axis=-1, keepdims=True) + jnp.sum(s, axis=-1, keepdims=True)
        hv = num / jnp.maximum(jnp.abs(den), jnp.exp(-m_t))

        mu = jnp.mean(hv, axis=-1, keepdims=True)
        d = hv - mu
        hn = d * lax.rsqrt(jnp.mean(d * d, axis=-1, keepdims=True) + LN_EPS)
        og = _sigmoid(o_ref[:, h * dv:(h + 1) * dv])
        out_ref[:, h * dv:(h + 1) * dv] = (hn * ng_ref[:, h * dv:(h + 1) * dv] * og).astype(out_ref.dtype)

        bL = b_c[L - 1:L, :]
        logw = bL - b_c + ig_c
        m_new = jnp.maximum(bL + m_prev, jnp.max(logw, axis=0, keepdims=True))
        decay = jnp.exp(bL + m_prev - m_new)
        kw = k * jnp.exp(logw - m_new)
        C_ref[h] = decay * C + _tn(kw.astype(BF16), v)
        n_ref[h:h + 1, :] = decay * n + jnp.sum(kw, axis=0, keepdims=True)
        m_ref[h:h + 1, :] = jnp.broadcast_to(m_new, (1, m_ref.shape[1]))


def _mlstm_mixer(x, B, S, w_in, conv_w, ig_bias, fg_bias, norm_g):
    T = B * S
    H, L = MLSTM_HEADS, MLSTM_L
    wb = w_in.astype(BF16)
    main = _proj(x, wb[:, :3 * D_MODEL], out_dtype=F32, tm=512, tn=1024)
    wg = jnp.pad(wb[:, 3 * D_MODEL:], ((0, 0), (0, 128 - 2 * H)))
    gates = _proj(x, wg, out_dtype=F32, tm=512, tn=128)
    gates_r = gates[:, :2 * H].reshape(B, S, 2 * H).transpose(0, 2, 1)
    bias16 = jnp.concatenate([ig_bias, fg_bias]).astype(F32)
    bias_c = jnp.pad(bias16, (0, 128 - 2 * H)).reshape(1, 128)
    bias_r = bias16.reshape(2 * H, 1)
    tri = jnp.tril(jnp.ones((L, L), F32))
    nc = S // L
    rowblk = lambda c: (lambda b, i: (b * nc + i, c))
    fixed = lambda b, i: (0, 0)
    return pl.pallas_call(
        _mlstm_kernel,
        name="mlstm",
        grid=(B, nc),
        in_specs=[pl.BlockSpec((L, D_MODEL), rowblk(0)), pl.BlockSpec((L, D_MODEL), rowblk(1)),
                  pl.BlockSpec((L, D_MODEL), rowblk(2)), pl.BlockSpec((L, 128), rowblk(0)),
                  pl.BlockSpec((None, 2 * H, L), lambda b, i: (b, 0, i)),
                  pl.BlockSpec((MLSTM_CONV, D_MODEL), fixed), pl.BlockSpec((1, 128), fixed),
                  pl.BlockSpec((2 * H, 1), fixed), pl.BlockSpec((1, D_MODEL), fixed),
                  pl.BlockSpec((L, L), fixed), pl.BlockSpec((L, L), fixed)],
        out_specs=pl.BlockSpec((L, D_MODEL), rowblk(0)),
        out_shape=jax.ShapeDtypeStruct((T, D_MODEL), BF16),
        scratch_shapes=[pltpu.VMEM((H, MLSTM_QK_DIM, MLSTM_V_DIM), F32), pltpu.VMEM((H, MLSTM_QK_DIM), F32),
                        pltpu.VMEM((H, 128), F32), pltpu.VMEM((L + 8, D_MODEL), F32)],
        compiler_params=_cparams("parallel", "arbitrary"),
    )(main, main, main, gates, gates_r, conv_w.astype(F32), bias_c, bias_r, norm_g.reshape(1, D_MODEL).astype(F32),
      tri, tri.T)


def _rot_cols(w):
    shp = w.shape
    w4 = w.reshape(shp[:-1] + (shp[-1] // HEAD_DIM, 2, ROPE_HALF))
    return jnp.flip(w4, axis=-2).reshape(shp)


def _nsa_compress_kernel(chk_ref, chv_ref, wk_lo_ref, wk_hi_ref, wv_lo_ref, wv_hi_ref, pk_ref, pv_ref, wk_ref, wv_ref,
                         cos_ref, sin_ref, kc_ref, vc_ref, sh_ref):
    NCH = chk_ref.shape[0]
    G = NSA_KV_HEADS
    sh_ref[NCH:NCH + 8, :] = jnp.zeros((8, sh_ref.shape[1]), F32)
    live = lax.broadcasted_iota(I32, (NCH, 1), 0) < NCH - 1

    def blocks(ch_ref, lo_ref, hi_ref, p_ref, w_ref):
        ch = ch_ref[...]
        n = lo_ref.shape[1]
        sh_ref[0:NCH, 0:n] = _mm(ch, hi_ref[...])
        const = _mm(p_ref[...].astype(BF16), w_ref[...])[0:1, :]
        return jnp.where(live, _mm(ch, lo_ref[...]) + sh_ref[1:NCH + 1, 0:n] + jnp.concatenate([const] * G, axis=1), 0.0)

    k2 = blocks(chk_ref, wk_lo_ref, wk_hi_ref, pk_ref, wk_ref)
    v2 = blocks(chv_ref, wv_lo_ref, wv_hi_ref, pv_ref, wv_ref)
    for g in range(G):
        raw = k2[:, 2 * g * HEAD_DIM:(2 * g + 1) * HEAD_DIM]
        rot = k2[:, (2 * g + 1) * HEAD_DIM:(2 * g + 2) * HEAD_DIM]
        kc_ref[g] = (raw * cos_ref[...] + rot * sin_ref[...]).astype(kc_ref.dtype)
        vc_ref[g] = v2[:, g * HEAD_DIM:(g + 1) * HEAD_DIM].astype(vc_ref.dtype)


def _nsa_cmp_select_kernel(q_ref, kc_ref, vc_ref, ovlT_ref, ocmp_ref, sel_ref):
    TQ = q_ref.shape[0]
    NCH = kc_ref.shape[0]
    NSB = ovlT_ref.shape[0]
    q0 = pl.program_id(2) * TQ
    t = q0 + lax.broadcasted_iota(I32, (TQ, 1), 0)
    cend = lax.broadcasted_iota(I32, (1, NCH), 1) * CMP_STRIDE + (CMP_BLK - 1)
    valid = cend <= t
    kc = kc_ref[...]
    vc = vc_ref[...]
    psum = jnp.zeros((TQ, NCH), F32)
    outs = []
    for j in range(NSA_J):
        qj = q_ref[:, j * HEAD_DIM:(j + 1) * HEAD_DIM]
        s = jnp.where(valid, _nt(qj, kc) * HEAD_DIM ** -0.5, NEG_INF)
        m = jnp.max(s, axis=-1, keepdims=True)
        e = jnp.where(valid, jnp.exp(s - m), 0.0)
        p = e / jnp.maximum(jnp.sum(e, axis=-1, keepdims=True), TINY)
        outs.append(_mm(p.astype(BF16), vc))
        psum = psum + p
    ocmp_ref[...] = jnp.concatenate(outs, axis=1).astype(ocmp_ref.dtype)

    imp = _nt(ovlT_ref[...], psum, precision=HIGHEST)
    nb = lax.broadcasted_iota(I32, (NSB, 1), 0)
    qblk = (q0 + lax.broadcasted_iota(I32, (1, TQ), 1)) // SEL_BLK
    forced = (nb == 0) | (nb == qblk) | (nb == qblk - 1)
    cur = jnp.where(forced, FORCE_SCORE, jnp.where(nb > qblk, NEG_INF, imp))
    sel = jnp.zeros((NSB, TQ), F32)
    for _ in range(min(N_SEL, NSB)):
        m = jnp.max(cur, axis=0, keepdims=True)
        idx = jnp.min(jnp.where(cur == m, nb, NSB), axis=0, keepdims=True)
        hit = nb == idx
        sel = jnp.where(hit, 1.0, sel)
        cur = jnp.where(hit, BELOW_NEG_INF, cur)
    sel_ref[...] = sel.astype(sel_ref.dtype)


NEG_FLOOR = -1e29


def _nsa_main_kernel(q_ref, ksT_ref, vs_ref, kwT_ref, vw_ref, sel_ref, ocmp_ref, gate_ref, gexp_ref, o_ref):
    TQ = q_ref.shape[0]
    S = vs_ref.shape[0]
    TK = min(NSA_TK, S)
    J = NSA_J
    q0 = pl.program_id(2) * TQ
    q4 = jnp.concatenate([q_ref[:, j * HEAD_DIM:(j + 1) * HEAD_DIM] for j in range(J)], axis=0) * HEAD_DIM ** -0.5
    t = q0 + lax.broadcasted_iota(I32, (TQ, 1), 0)
    selT = sel_ref[...]

    def attend(carry, kT, v, bias):
        m, l, acc = carry
        s = _mm(q4, kT) + jnp.concatenate([bias] * J, axis=0)
        m_new = jnp.maximum(m, jnp.max(s, axis=-1, keepdims=True))
        e = jnp.exp(s - m_new)
        a = jnp.exp(m - m_new)
        return m_new, a * l + jnp.sum(e, axis=-1, keepdims=True), a * acc + _mm(e.astype(BF16), v)

    init = (jnp.full((J * TQ, 1), NEG_FLOOR, F32), jnp.zeros((J * TQ, 1), F32), jnp.zeros((J * TQ, HEAD_DIM), F32))

    def body(c, carry):
        k0 = pl.multiple_of(c * TK, TK)
        kpos = k0 + lax.broadcasted_iota(I32, (1, TK), 1)
        expand = jnp.where(kpos // SEL_BLK == lax.broadcasted_iota(I32, (selT.shape[0], 1), 0), 1.0, 0.0).astype(BF16)
        bias = jnp.where((_tn(selT, expand) > 0.5) & (kpos <= t), 0.0, NEG_INF)
        return attend(carry, ksT_ref[:, pl.ds(k0, TK)], vs_ref[pl.ds(k0, TK), :], bias)

    nchunks = (q0 + TQ - 1) // TK + 1
    _, l, acc = lax.fori_loop(0, nchunks, body, init)
    o_slc = acc / jnp.maximum(l, TINY)

    span = min(NSA_WINDOW + TQ, S)
    start = pl.multiple_of(jnp.clip(q0 - NSA_WINDOW, 0, S - span), TQ)
    dist = t - (start + lax.broadcasted_iota(I32, (1, span), 1))
    bias = jnp.where((dist >= 0) & (dist < NSA_WINDOW), 0.0, NEG_INF)
    _, l, acc = attend(init, kwT_ref[:, pl.ds(start, span)], vw_ref[pl.ds(start, span), :], bias)
    o_win = acc / jnp.maximum(l, TINY)

    unstack = lambda o: jnp.concatenate([o[j * TQ:(j + 1) * TQ, :] for j in range(J)], axis=1)
    g = _sigmoid(gate_ref[...])
    gx = [_mm(g, gexp_ref[c], precision=HIGHEST) for c in range(3)]
    out = gx[0] * ocmp_ref[...].astype(F32) + gx[1] * unstack(o_slc) + gx[2] * unstack(o_win)
    o_ref[...] = out.astype(o_ref.dtype)


def _nsa_mixer(x, B, S, w_in, cmp_pos_k, cmp_pos_v, cmp_wk, cmp_wv):
    T = B * S
    H, G, J, dh = NSA_HEADS, NSA_KV_HEADS, NSA_J, HEAD_DIM
    kv = G * dh
    wb = w_in.astype(BF16)
    col = lambda a, n: wb[:, a:a + n]
    o_q, o_kc, o_vc, o_ks, o_vs, o_kw, o_vw, o_g = np.cumsum([0, H * dh] + [kv] * 6).tolist()
    pos = jnp.arange(S)
    w_rope = jnp.concatenate([col(o_q, H * dh), col(o_ks, kv), col(o_kw, kv)], axis=1)
    roped = _proj(x, w_rope, out_dtype=BF16, tm=512, tn=512, rope_tables=_rope_tables(pos, 512), seq=S)
    w_plain = jnp.concatenate([col(o_vs, kv), col(o_vw, kv), jnp.pad(col(o_g, 3 * H), ((0, 0), (0, 128 - 3 * H)))], axis=1)
    plain = _proj(x, w_plain, out_dtype=F32, tm=512, tn=128)

    heads_T = lambda a: a.reshape(B, S, G, dh).transpose(0, 2, 3, 1)
    heads = lambda a: a.reshape(B, S, G, dh).transpose(0, 2, 1, 3)
    ksT = heads_T(roped[:, H * dh:H * dh + kv])
    kwT = heads_T(roped[:, H * dh + kv:])
    vs = heads(plain[:, :kv]).astype(BF16)
    vw = heads(plain[:, kv:2 * kv]).astype(BF16)
    gates = plain[:, 2 * kv:]

    nch = S // CMP_STRIDE
    cw = CMP_STRIDE * kv
    chk = _proj(x, col(o_kc, kv), out_dtype=BF16, tm=512, tn=kv).reshape(B, nch, cw)
    chv = _proj(x, col(o_vc, kv), out_dtype=BF16, tm=512, tn=kv).reshape(B, nch, cw)
    wk = cmp_wk.reshape(CMP_BLK * dh, dh)
    wk2 = jnp.concatenate([wk, _rot_cols(wk)], axis=1).astype(BF16)
    wv2 = cmp_wv.reshape(CMP_BLK * dh, dh).astype(BF16)
    eye = jnp.eye(G, dtype=BF16)

    def per_head(w, part):
        n = w.shape[1]
        w3 = w.reshape(2, CMP_STRIDE, dh, n)[part]
        return jnp.einsum('lde,gh->lgdhe', w3, eye).reshape(cw, G * n)

    flat8 = lambda p_: jnp.pad(p_.reshape(1, CMP_BLK * dh), ((0, 7), (0, 0))).astype(F32)
    cend = jnp.arange(nch) * CMP_STRIDE + CMP_BLK - 1
    cos_c, sin_c = _rope_tables(cend, dh)
    fixed1 = lambda b: (0, 0)
    full1 = lambda a: pl.BlockSpec(a.shape, fixed1)
    consts = (per_head(wk2, 0), per_head(wk2, 1), per_head(wv2, 0), per_head(wv2, 1), flat8(cmp_pos_k), flat8(cmp_pos_v),
              wk2, wv2, cos_c, sin_c)
    kc, vc = pl.pallas_call(
        _nsa_compress_kernel,
        name="nsa_compress",
        grid=(B,),
        in_specs=[pl.BlockSpec((None, nch, cw), lambda b: (b, 0, 0))] * 2 + [full1(c) for c in consts],
        out_specs=[pl.BlockSpec((None, G, nch, dh), lambda b: (b, 0, 0, 0))] * 2,
        out_shape=[jax.ShapeDtypeStruct((B, G, nch, dh), BF16)] * 2,
        scratch_shapes=[pltpu.VMEM((nch + 8, 2 * kv), F32)],
        compiler_params=_cparams("parallel"),
    )(chk, chv, *consts)

    nsb = S // SEL_BLK
    c_idx, s_idx = np.arange(nch), np.arange(nsb)
    ovl = ((c_idx[:, None] * CMP_STRIDE + CMP_BLK - 1 >= s_idx[None, :] * SEL_BLK)
           & (c_idx[:, None] * CMP_STRIDE < (s_idx[None, :] + 1) * SEL_BLK)).astype(np.float32)
    bgi = lambda b, g, i: (b, g, 0, 0)
    TS = min(NSA_TQ_SELECT, S)
    ns = S // TS
    sblk = lambda b, g, i: (b * ns + i, g)
    ocmp, sel = pl.pallas_call(
        _nsa_cmp_select_kernel,
        name="nsa_cmp_select",
        grid=(B, G, ns),
        in_specs=[pl.BlockSpec((TS, J * dh), sblk), pl.BlockSpec((None, None, nch, dh), bgi),
                  pl.BlockSpec((None, None, nch, dh), bgi), pl.BlockSpec((nsb, nch), lambda b, g, i: (0, 0))],
        out_specs=[pl.BlockSpec((TS, J * dh), sblk), pl.BlockSpec((None, None, nsb, TS), lambda b, g, i: (b, g, 0, i))],
        out_shape=[jax.ShapeDtypeStruct((T, H * dh), BF16), jax.ShapeDtypeStruct((B, G, nsb, S), BF16)],
        compiler_params=_cparams("parallel", "parallel", "parallel"),
    )(roped, kc, vc, jnp.asarray(ovl.T))
    TQ = NSA_TQ
    nq = S // TQ
    qblk = lambda b, g, i: (b * nq + i, g)

    gexp = np.zeros((G, 3, 128, J * dh), np.float32)
    for g in range(G):
        for c in range(3):
            for j in range(J):
                gexp[g, c, (g * J + j) * 3 + c, j * dh:(j + 1) * dh] = 1.0
    return pl.pallas_call(
        _nsa_main_kernel,
        name="nsa_main",
        grid=(B, G, nq),
        in_specs=[pl.BlockSpec((TQ, J * dh), qblk)] + [pl.BlockSpec((None, None, dh, S), bgi),
                                                       pl.BlockSpec((None, None, S, dh), bgi)] * 2 + [
            pl.BlockSpec((None, None, nsb, TQ), lambda b, g, i: (b, g, 0, i)),
            pl.BlockSpec((TQ, J * dh), qblk), pl.BlockSpec((TQ, 128), lambda b, g, i: (b * nq + i, 0)),
            pl.BlockSpec((None, 3, 128, J * dh), lambda b, g, i: (g, 0, 0, 0))],
        out_specs=pl.BlockSpec((TQ, J * dh), qblk),
        out_shape=jax.ShapeDtypeStruct((T, H * dh), BF16),
        compiler_params=_cparams("parallel", "parallel", "arbitrary"),
    )(roped, ksT, vs, kwT, vw, sel, ocmp, gates, jnp.asarray(gexp))


def _dil_kernel(q_ref, kp_ref, kc_ref, vp_ref, vc_ref, o_ref, lse_ref):
    NQ = q_ref.shape[0]
    prev_from = jnp.where(pl.program_id(2) == 0, NQ, 0)
    qi = lax.broadcasted_iota(I32, (NQ, 1), 0)
    kj = lax.broadcasted_iota(I32, (1, 2 * NQ), 1)
    dist = NQ + qi - kj
    valid = (dist >= 0) & (dist <= NQ) & (kj >= prev_from)
    lane = lax.broadcasted_iota(I32, (NQ, 128), 1)
    first = lane < HEAD_DIM
    lse_all = jnp.zeros((NQ, 128), F32)
    zero = jnp.zeros((), q_ref.dtype)
    for hp in range(DIL_HEADS // 2):
        sl = slice(hp * 128, (hp + 1) * 128)
        q2 = q_ref[:, sl]
        kb = jnp.concatenate([kp_ref[:, sl], kc_ref[:, sl]], axis=0)
        vb = jnp.concatenate([vp_ref[:, sl], vc_ref[:, sl]], axis=0)
        outs = []
        for sub in range(2):
            keep = first if sub == 0 else jnp.logical_not(first)
            s = jnp.where(valid, _nt(jnp.where(keep, q2, zero), kb) * HEAD_DIM ** -0.5, NEG_INF)
            m = jnp.max(s, axis=-1, keepdims=True)
            e = jnp.exp(s - m)
            den = jnp.sum(e, axis=-1, keepdims=True)
            outs.append(_mm((e / den).astype(BF16), vb))
            lse_all = jnp.where(lane == 2 * hp + sub, m + jnp.log(den), lse_all)
        o_ref[:, sl] = jnp.where(first, outs[0], outs[1]).astype(o_ref.dtype)
    lse_ref[...] = lse_all


def _proj_res_kernel(*refs, rope, dils):
    n_out = len(dils)
    x_ref, w_ref = refs[:2]
    rest = refs[2:]
    if rope:
        cos_ref, sin_ref = rest[:2]
        rest = rest[2:]
    out_refs, xb_ref, scr_ref = rest[:n_out], rest[n_out], rest[n_out + 1]
    tm, tn = x_ref.shape[0], w_ref.shape[1]

    @pl.when(pl.program_id(1) == 0)
    def _():
        xb_ref[...] = x_ref[...].astype(BF16)

    acc = _mm(xb_ref[...], w_ref[...])
    if rope:
        lane = lax.broadcasted_iota(I32, acc.shape, 1)
        lo = (lane % HEAD_DIM) < ROPE_HALF
        rot = jnp.where(lo, pltpu.roll(acc, tn - ROPE_HALF, 1), pltpu.roll(acc, ROPE_HALF, 1))
        acc = acc * cos_ref[...] + rot * sin_ref[...]
    if any(d > 1 for d in dils):
        for c in range(tn // 128):
            scr_ref[c] = acc[:, c * 128:(c + 1) * 128]
    for o_ref, dil in zip(out_refs, dils):
        if dil == 1:
            o_ref[...] = acc.astype(o_ref.dtype)
            continue
        for r in range(dil):
            for c in range(tn // 128):
                o_ref[:, r * tn + c * 128:r * tn + (c + 1) * 128] = (
                    scr_ref[c, pl.ds(r, tm // dil, stride=dil), :].astype(o_ref.dtype))


def _proj_res(x, w, dils, *, seq, rope, tm=512, tn=D_MODEL):
    T, K = x.shape
    N = w.shape[1]
    nparts = N // tn
    in_specs = [pl.BlockSpec((tm, K), lambda i, j: (i, 0)), pl.BlockSpec((K, tn), lambda i, j: (0, j))]
    args = [x, w]
    if rope:
        nseq = seq // tm
        in_specs += [pl.BlockSpec((tm, tn), lambda i, j: (i % nseq, 0))] * 2
        args += list(_rope_tables(jnp.arange(seq), tn))
    return pl.pallas_call(
        functools.partial(_proj_res_kernel, rope=rope, dils=dils),
        name="proj_residue_rope" if rope else "proj_residue",
        grid=(T // tm, nparts),
        in_specs=in_specs,
        out_specs=[pl.BlockSpec((tm // d, d * tn), lambda i, j: (i, j)) for d in dils],
        out_shape=[jax.ShapeDtypeStruct((T // d, nparts * d * tn), BF16) for d in dils],
        scratch_shapes=[pltpu.VMEM((tm, K), BF16), pltpu.VMEM((tn // 128, tm, 128), F32)],
        compiler_params=_cparams("parallel", "arbitrary"),
    )(*args)


def _dil_group(qk, v, B, S, dil):
    NQ = DIL_STEPS
    U = S // dil
    nb = U // NQ
    W = D_MODEL
    qk_view = qk.reshape(B, U, 2 * dil * W)
    v_view = v.reshape(B, U, dil * W)
    cur = lambda part: (lambda b, r, n: (b, n, part * dil + r))
    prev = lambda part: (lambda b, r, n: (b, jnp.maximum(n - 1, 0), part * dil + r))
    blk = lambda f: pl.BlockSpec((None, NQ, W), f)
    o, lse = pl.pallas_call(
        _dil_kernel,
        name=f"dilated_attn_{dil}",
        grid=(B, dil, nb),
        in_specs=[blk(cur(0)), blk(prev(1)), blk(cur(1)), blk(prev(0)), blk(cur(0))],
        out_specs=[pl.BlockSpec((None, NQ, W), lambda b, r, n: (b, n, r)),
                   pl.BlockSpec((None, NQ, 128), lambda b, r, n: (b, n, r))],
        out_shape=[jax.ShapeDtypeStruct((B, U, dil * W), BF16), jax.ShapeDtypeStruct((B, U, dil * 128), F32)],
        compiler_params=_cparams("parallel", "parallel", "arbitrary"),
    )(qk_view, qk_view, qk_view, v_view, v_view)
    return o.reshape(B * U, dil * W), lse.reshape(B * U, dil * 128)


def _dil_outproj_ln_kernel(o0_ref, o1_ref, o2_ref, l0_ref, l1_ref, l2_ref, hexp_ref, w_ref, x_ref, g_ref, b_ref, out_ref,
                           oscr_ref, lscr_ref, *, dils):
    tm, W = x_ref.shape

    def natural(o_ref, l_ref, dil):
        if dil == 1:
            return o_ref[...].astype(F32), l_ref[...]
        n = tm // dil
        for r in range(dil):
            lscr_ref[0, pl.ds(r, n, stride=dil), :] = l_ref[:, r * 128:(r + 1) * 128]
            for c in range(W // 128):
                oscr_ref[c, pl.ds(r, n, stride=dil), :] = o_ref[:, r * W + c * 128:r * W + (c + 1) * 128].astype(F32)
        return jnp.concatenate([oscr_ref[c] for c in range(W // 128)], axis=1), lscr_ref[0]

    os_, ls = zip(*[natural(o, l, d) for o, l, d in zip((o0_ref, o1_ref, o2_ref), (l0_ref, l1_ref, l2_ref), dils)])
    m = jnp.maximum(jnp.maximum(ls[0], ls[1]), ls[2])
    es = [jnp.exp(l - m) for l in ls]
    tot = es[0] + es[1] + es[2]
    y = jnp.zeros((tm, W), F32)
    for e, o in zip(es, os_):
        y = y + _mm(e / tot, hexp_ref[...], precision=HIGHEST) * o
    z = ALPHA * x_ref[...] + _mm(y.astype(BF16), w_ref[...])
    out_ref[...] = _ln_rows(z, g_ref[...], b_ref[...])


def _dilated_layer(x, B, S, w_in, w_out, g, b, *, tm=256):
    T, D = x.shape
    dils = tuple(d for _, d in DIL_GROUPS)
    wb = w_in.astype(BF16)
    vals = _proj_res(x, wb[:, 2 * len(dils) * D:], dils, seq=S, rope=False)
    outs = []
    for gi, dil in enumerate(dils):
        qk, = _proj_res(x, wb[:, 2 * gi * D:(2 * gi + 2) * D], (dil,), seq=S, rope=True)
        outs.append(_dil_group(qk, vals[gi], B, S, dil))
    hexp = np.zeros((128, D), np.float32)
    for h in range(DIL_HEADS):
        hexp[h, h * HEAD_DIM:(h + 1) * HEAD_DIM] = 1.0
    row = lambda i: (i, 0)
    fixed = lambda i: (0, 0)
    return pl.pallas_call(
        functools.partial(_dil_outproj_ln_kernel, dils=dils),
        name="dilated_outproj_ln",
        grid=(T // tm,),
        in_specs=[pl.BlockSpec((tm // d, d * D), row) for d in dils] + [pl.BlockSpec((tm // d, d * 128), row) for d in dils]
        + [pl.BlockSpec((128, D), fixed), pl.BlockSpec((D, D), fixed), pl.BlockSpec((tm, D), row),
           pl.BlockSpec((1, D), fixed), pl.BlockSpec((1, D), fixed)],
        out_specs=pl.BlockSpec((tm, D), row),
        out_shape=jax.ShapeDtypeStruct((T, D), F32),
        scratch_shapes=[pltpu.VMEM((D // 128, tm, 128), F32), pltpu.VMEM((1, tm, 128), F32)],
        compiler_params=_cparams("parallel"),
    )(outs[0][0], outs[1][0], outs[2][0], outs[0][1], outs[1][1], outs[2][1], jnp.asarray(hexp),
      w_out.astype(BF16), x, g.reshape(1, D), b.reshape(1, D))


def _pool_ln_kernel(x_ref, halo_ref, w_ref, scale_ref, g_ref, b_ref, o_ref, ext_ref):
    TS = x_ref.shape[0]
    s = pl.program_id(1)
    x = x_ref[...]
    ext_ref[0:POOL_HALO, :] = jnp.where(s == 0, 0.0, halo_ref[...])
    ext_ref[POOL_HALO:POOL_HALO + TS, :] = x
    cnt = (s * TS + lax.broadcasted_iota(I32, (TS, 1), 0) + 1).astype(F32)
    ys = []
    for gi, w in enumerate(POOL_WINDOWS):
        sl = slice(gi * POOL_GROUP, (gi + 1) * POOL_GROUP)
        xg = x[:, sl]
        tot = xg
        for j in range(1, w):
            tot = tot + ext_ref[POOL_HALO - j:POOL_HALO - j + TS, sl]
        mean = tot / jnp.minimum(cnt, float(w))
        ys.append(_mm((mean - xg).astype(BF16), w_ref[gi]))
    y = jnp.concatenate(ys, axis=1) * scale_ref[...]
    o_ref[...] = _ln_rows(ALPHA * x + y, g_ref[...], b_ref[...])


def _pool_layer(x, B, S, w_grp, scale, g, b, *, ts=512):
    T, D = x.shape
    ns = S // ts
    hb = ts // POOL_HALO
    fixed = lambda bb, s: (0, 0)
    return pl.pallas_call(
        _pool_ln_kernel,
        name="pool_ln",
        grid=(B, ns),
        in_specs=[pl.BlockSpec((ts, D), lambda bb, s: (bb * ns + s, 0)),
                  pl.BlockSpec((POOL_HALO, D), lambda bb, s: (jnp.maximum((bb * ns + s) * hb - 1, 0), 0)),
                  pl.BlockSpec((len(POOL_WINDOWS), POOL_GROUP, POOL_GROUP), lambda bb, s: (0, 0, 0)),
                  pl.BlockSpec((1, D), fixed), pl.BlockSpec((1, D), fixed), pl.BlockSpec((1, D), fixed)],
        out_specs=pl.BlockSpec((ts, D), lambda bb, s: (bb * ns + s, 0)),
        out_shape=jax.ShapeDtypeStruct((T, D), F32),
        scratch_shapes=[pltpu.VMEM((ts + POOL_HALO, D), F32)],
        compiler_params=_cparams("parallel", "arbitrary"),
    )(x, x, w_grp.astype(BF16), scale.reshape(1, D), g.reshape(1, D), b.reshape(1, D))


U32 = jnp.uint32
HI16 = 0xFFFF0000
ROW_WORDS = D_MODEL // 2


def _pack_halves(v):
    h = v.shape[1] // 2
    bits = lambda a: lax.bitcast_convert_type(a.astype(BF16).astype(F32), U32)
    return (bits(v[:, :h]) >> 16) | (bits(v[:, h:]) & U32(HI16))


def _unpack_halves(w):
    lo = lax.bitcast_convert_type(w << 16, F32)
    hi = lax.bitcast_convert_type(w & U32(HI16), F32)
    return jnp.concatenate([lo, hi], axis=1)


def _router_kernel(x_ref, wT_ref, bias_ref, triu_ref, eidx_ref, gw_ref, rank_ref, cnt_ref, xp_ref, carry_ref):
    E = N_EXPERTS
    per = E // N_EXPERT_GROUPS
    TM = x_ref.shape[0]

    @pl.when(pl.program_id(0) == 0)
    def _():
        carry_ref[...] = jnp.zeros_like(carry_ref)

    x = x_ref[...]
    xp_ref[...] = _pack_halves(x)

    scores = _sigmoid(_nt(wT_ref[...], x, precision=HIGHEST))
    biased = scores + bias_ref[...]
    eio = lax.broadcasted_iota(I32, (E, TM), 0)

    gio = lax.broadcasted_iota(I32, (per, TM), 0)
    gscore = []
    for gidx in range(N_EXPERT_GROUPS):
        slab = biased[gidx * per:(gidx + 1) * per, :]
        m1 = jnp.max(slab, axis=0, keepdims=True)
        i1 = jnp.min(jnp.where(slab == m1, gio, per), axis=0, keepdims=True)
        m2 = jnp.max(jnp.where(gio == i1, BELOW_NEG_INF, slab), axis=0, keepdims=True)
        gscore.append(m1 + m2)
    slabs = []
    for gidx in range(N_EXPERT_GROUPS):
        beat = jnp.zeros((1, TM), F32)
        for o in range(N_EXPERT_GROUPS):
            if o == gidx:
                continue
            wins = (gscore[o] >= gscore[gidx]) if o < gidx else (gscore[o] > gscore[gidx])
            beat = beat + jnp.where(wins, 1.0, 0.0)
        keep = beat < float(TOPK_GROUPS)
        slabs.append(jnp.where(keep, biased[gidx * per:(gidx + 1) * per, :], NEG_INF))
    cur = jnp.concatenate(slabs, axis=0)

    picked = jnp.zeros((E, TM), F32)
    idxs, vals = [], []
    for _ in range(TOP_K):
        m = jnp.max(cur, axis=0, keepdims=True)
        idx = jnp.min(jnp.where(cur == m, eio, E), axis=0, keepdims=True)
        hit = eio == idx
        picked = jnp.where(hit, 1.0, picked)
        cur = jnp.where(hit, BELOW_NEG_INF, cur)
        idxs.append(idx)
        vals.append(jnp.sum(jnp.where(hit, scores, 0.0), axis=0, keepdims=True))
    total = vals[0]
    for v in vals[1:]:
        total = total + v

    pos = _mm(picked.astype(BF16), triu_ref[...]) + carry_ref[...]
    for k in range(TOP_K):
        eidx_ref[k:k + 1, :] = idxs[k]
        gw_ref[k:k + 1, :] = vals[k] / total * ROUTED_SCALE
        rank_ref[k:k + 1, :] = jnp.sum(jnp.where(eio == idxs[k], pos, 0.0), axis=0, keepdims=True).astype(I32)
    carry_ref[...] = carry_ref[...] + jnp.sum(picked, axis=1, keepdims=True)
    cnt_ref[...] = carry_ref[...].astype(I32)


def _router(x, router_w, router_bias, *, tm=256):
    T, D = x.shape
    E = N_EXPERTS
    triu = jnp.triu(jnp.ones((tm, tm), F32), k=1).astype(BF16)
    col = lambda i: (0, i)
    fixed = lambda i: (0, 0)
    return pl.pallas_call(
        _router_kernel,
        name="moe_router",
        grid=(T // tm,),
        in_specs=[pl.BlockSpec((tm, D), lambda i: (i, 0)), pl.BlockSpec((E, D), fixed),
                  pl.BlockSpec((E, 1), fixed), pl.BlockSpec((tm, tm), fixed)],
        out_specs=[pl.BlockSpec((TOP_K, tm), col), pl.BlockSpec((TOP_K, tm), col), pl.BlockSpec((TOP_K, tm), col),
                   pl.BlockSpec((E, 1), fixed), pl.BlockSpec((tm, ROW_WORDS), lambda i: (i, 0))],
        out_shape=[jax.ShapeDtypeStruct((TOP_K, T), I32), jax.ShapeDtypeStruct((TOP_K, T), F32),
                   jax.ShapeDtypeStruct((TOP_K, T), I32), jax.ShapeDtypeStruct((E, 1), I32),
                   jax.ShapeDtypeStruct((T, ROW_WORDS), U32)],
        scratch_shapes=[pltpu.VMEM((E, 1), F32)],
        compiler_params=_cparams("arbitrary"),
    )(x, router_w.T.astype(F32), router_bias.reshape(E, 1).astype(F32), triu)


def _tile_indices(dest_hbm, idx_ref, sem_idx, n_idx):
    i = pl.program_id(0)

    def idx_copy(step):
        slot = step % 2
        return pltpu.make_async_copy(dest_hbm.at[step], idx_ref.at[pl.ds(slot * n_idx, n_idx)], sem_idx.at[slot])

    @pl.when(i == 0)
    def _():
        idx_copy(0).start()

    idx_copy(i).wait()

    @pl.when(i + 1 < pl.num_programs(0))
    def _():
        idx_copy(i + 1).start()

    return (i % 2) * n_idx


def _dispatch_kernel(cnt_ref, pstart_ref, dest_hbm, x_ref, rows_ref, idx_ref, zero_ref, sem_idx, sem_rows, sem_zero):
    TD = x_ref.shape[0]

    @pl.when(pl.program_id(0) == 0)
    def _():
        zero_ref[...] = jnp.zeros_like(zero_ref)

        def per_expert(e, c):
            n = cnt_ref[e]
            base = pstart_ref[e]
            pad_to = (n + EXP_BLK - 1) // EXP_BLK * EXP_BLK
            fill = lambda r: pltpu.make_async_copy(zero_ref.at[pl.ds(0, 1)], rows_ref.at[base + r], sem_zero)

            def start(r, c2):
                fill(r).start()
                return c2

            def wait(r, c2):
                fill(r).wait()
                return c2

            lax.fori_loop(n, pad_to, start, 0)
            lax.fori_loop(n, pad_to, wait, 0)
            return c

        lax.fori_loop(0, N_EXPERTS, per_expert, 0)

    base = _tile_indices(dest_hbm, idx_ref, sem_idx, TOP_K * TD)
    scatter = lambda t, k: pltpu.make_async_copy(x_ref.at[pl.ds(t, 1)], rows_ref.at[idx_ref[base + k * TD + t]], sem_rows)

    def issue(t, c):
        for k in range(TOP_K):
            scatter(t, k).start(priority=k % 2)
        return c

    def drain(t, c):
        for k in range(TOP_K):
            scatter(t, k).wait()
        return c

    lax.fori_loop(0, TD, issue, 0, unroll=8)
    lax.fori_loop(0, TD, drain, 0, unroll=8)


def _dispatch(xp, dest_tiles, counts, pad_start, n_rows, *, td):
    T = xp.shape[0]
    return pl.pallas_call(
        _dispatch_kernel,
        name="moe_dispatch",
        grid_spec=pltpu.PrefetchScalarGridSpec(
            num_scalar_prefetch=2,
            grid=(T // td,),
            in_specs=[pl.BlockSpec(memory_space=pl.ANY), pl.BlockSpec((td, ROW_WORDS), lambda i, c, p: (i, 0))],
            out_specs=pl.BlockSpec(memory_space=pl.ANY),
            scratch_shapes=[pltpu.SMEM((2 * TOP_K * td,), I32), pltpu.VMEM((8, ROW_WORDS), U32),
                            pltpu.SemaphoreType.DMA((2,)), pltpu.SemaphoreType.DMA(()), pltpu.SemaphoreType.DMA(())],
        ),
        out_shape=jax.ShapeDtypeStruct((n_rows, 1, ROW_WORDS), U32),
        compiler_params=_cparams("arbitrary"),
    )(counts, pad_start, dest_tiles, xp)


def _expert_kernel(be_ref, nu_ref, rows_hbm, wg_ref, wu_ref, wd_ref, out_hbm, xbuf, obuf, wgu_bf, wd_bf, sem_in, sem_out):
    i = pl.program_id(0)
    nu = nu_ref[0]

    @pl.when((i < nu) & ((i == 0) | (be_ref[i] != be_ref[jnp.maximum(i - 1, 0)])))
    def _():
        wgu_bf[:, :EXPERT_FF] = wg_ref[0].astype(BF16)
        wgu_bf[:, EXPERT_FF:] = wu_ref[0].astype(BF16)
        wd_bf[...] = wd_ref[0].astype(BF16)

    blk = lambda step: pl.ds(pl.multiple_of(step * EXP_BLK, EXP_BLK), EXP_BLK)
    in_copy = lambda step: pltpu.make_async_copy(rows_hbm.at[blk(step), 0], xbuf.at[step % 2], sem_in.at[step % 2])
    out_copy = lambda step: pltpu.make_async_copy(obuf.at[step % 2], out_hbm.at[blk(step), 0], sem_out.at[step % 2])

    @pl.when(i == 0)
    def _():
        in_copy(0).start()

    @pl.when(i < nu)
    def _():
        in_copy(i).wait()

        @pl.when(i + 1 < nu)
        def _():
            in_copy(i + 1).start()

        @pl.when(i >= 2)
        def _():
            out_copy(i - 2).wait()

        slot = i % 2
        gu = _mm(_unpack_halves(xbuf[slot]).astype(BF16), wgu_bf[...])
        h = _silu(gu[:, :EXPERT_FF]) * gu[:, EXPERT_FF:]
        obuf[slot] = _pack_halves(_mm(h.astype(BF16), wd_bf[...]))
        out_copy(i).start()

    @pl.when(i == pl.num_programs(0) - 1)
    def _():
        @pl.when(nu >= 2)
        def _():
            out_copy(nu - 2).wait()

        out_copy(nu - 1).wait()


def _experts(rows, blk_expert, n_used, w_gate, w_up, w_down):
    R = rows.shape[0]
    D = w_down.shape[-1]
    n_blk = R // EXP_BLK
    expert = lambda i, be, nu: (be[jnp.minimum(i, nu[0] - 1)], 0, 0)
    return pl.pallas_call(
        _expert_kernel,
        name="moe_experts",
        grid_spec=pltpu.PrefetchScalarGridSpec(
            num_scalar_prefetch=2,
            grid=(n_blk,),
            in_specs=[pl.BlockSpec(memory_space=pl.ANY), pl.BlockSpec((1, D, EXPERT_FF), expert),
                      pl.BlockSpec((1, D, EXPERT_FF), expert), pl.BlockSpec((1, EXPERT_FF, D), expert)],
            out_specs=pl.BlockSpec(memory_space=pl.ANY),
            scratch_shapes=[pltpu.VMEM((2, EXP_BLK, ROW_WORDS), U32), pltpu.VMEM((2, EXP_BLK, ROW_WORDS), U32),
                            pltpu.VMEM((D, 2 * EXPERT_FF), BF16), pltpu.VMEM((EXPERT_FF, D), BF16),
                            pltpu.SemaphoreType.DMA((2,)), pltpu.SemaphoreType.DMA((2,))],
        ),
        out_shape=jax.ShapeDtypeStruct((R, 1, ROW_WORDS), U32),
        compiler_params=_cparams("arbitrary"),
    )(blk_expert, n_used, rows, w_gate, w_up, w_down)


def _combine_kernel(dest_hbm, gw_ref, x_ref, rows_ref, wsgu_ref, wsd_ref, g_ref, b_ref, p_ref, wp_ref, wpg_ref,
                    o_ref, idx_ref, buf_ref, sem_idx, sem_rows):
    TM = x_ref.shape[0]
    n_idx = TOP_K * TM
    i = pl.program_id(0)
    n = pl.num_programs(0)

    def idx_copy(step):
        slot = step % 3
        return pltpu.make_async_copy(dest_hbm.at[step], idx_ref.at[pl.ds(slot * n_idx, n_idx)], sem_idx.at[slot])

    def row_loop(step, slot, start):
        base = (step % 3) * n_idx

        def body(t, c):
            for k in range(TOP_K):
                cp = pltpu.make_async_copy(rows_ref.at[idx_ref[base + k * TM + t]], buf_ref.at[slot, k, pl.ds(t, 1)],
                                           sem_rows.at[slot])
                if start:
                    cp.start(priority=k % 2)
                else:
                    cp.wait()
            return c

        lax.fori_loop(0, TM, body, 0, unroll=8)

    def for_parity(step, fn):
        for slot in range(2):
            pl.when(step % 2 == slot)(functools.partial(fn, slot))

    @pl.when(i == 0)
    def _():
        idx_copy(0).start()
        idx_copy(0).wait()
        row_loop(0, 0, True)

        @pl.when(n > 1)
        def _():
            idx_copy(1).start()

    @pl.when(i + 1 < n)
    def _():
        idx_copy(i + 1).wait()

        @pl.when(i + 2 < n)
        def _():
            idx_copy(i + 2).start()

        for_parity(i + 1, lambda slot: row_loop(i + 1, slot, True))

    x = x_ref[...]
    gu = _mm(x.astype(BF16), wsgu_ref[...])
    ff = gu.shape[1] // 2
    f = _mm((_silu(gu[:, :ff]) * gu[:, ff:]).astype(BF16), wsd_ref[...])
    for_parity(i, lambda slot: row_loop(i, slot, False))
    gw = gw_ref[...]
    cur = i % 2
    for k in range(TOP_K):
        f = f + gw[:, k:k + 1] * _unpack_halves(buf_ref[cur, k])
    x2 = _ln_rows(ALPHA * x + f, g_ref[...], b_ref[...])
    gate = _sigmoid(_mm(x2.astype(BF16), wpg_ref[...]))
    o_ref[...] = x2 + gate * _mm(p_ref[...].astype(BF16), wp_ref[...])


def _combine(dest_tiles, gw, x, rows_out, wsgu, wsd, g, b, p, wp, wpg, *, tm):
    T, D = x.shape
    row = lambda i: (i, 0)
    fixed = lambda i: (0, 0)
    full = lambda a: pl.BlockSpec(a.shape, fixed)
    return pl.pallas_call(
        _combine_kernel,
        name="moe_combine",
        grid=(T // tm,),
        in_specs=[pl.BlockSpec(memory_space=pl.ANY), pl.BlockSpec((tm, TOP_K), row), pl.BlockSpec((tm, D), row),
                  pl.BlockSpec(memory_space=pl.ANY), full(wsgu), full(wsd), pl.BlockSpec((1, D), fixed),
                  pl.BlockSpec((1, D), fixed), pl.BlockSpec((tm, PLE_DIM), row), full(wp), full(wpg)],
        out_specs=pl.BlockSpec((tm, D), row),
        out_shape=jax.ShapeDtypeStruct((T, D), F32),
        scratch_shapes=[pltpu.SMEM((3 * TOP_K * tm,), I32), pltpu.VMEM((2, TOP_K, tm, ROW_WORDS), U32),
                        pltpu.SemaphoreType.DMA((3,)), pltpu.SemaphoreType.DMA((2,))],
        compiler_params=_cparams("arbitrary"),
    )(dest_tiles, gw, x, rows_out, wsgu, wsd, g.reshape(1, D), b.reshape(1, D), p, wp, wpg)


MOE_TILE = 256


def _moe_ple_layer(x, p, router_w, router_bias, w_gate, w_up, w_down, ws_gate, ws_up, ws_down, g, b, ple_w, ple_gate_w):
    T, D = x.shape
    eidx, gw, rank, counts, xp = _router(x, router_w, router_bias)
    counts = counts.reshape(N_EXPERTS)
    padded = (counts + EXP_BLK - 1) // EXP_BLK * EXP_BLK
    pad_end = jnp.cumsum(padded)
    pad_start = pad_end - padded
    n_blk = T * TOP_K // EXP_BLK + N_EXPERTS
    e_iota = jnp.arange(N_EXPERTS, dtype=I32)
    dest = rank + jnp.sum(jnp.where(eidx[..., None] == e_iota, pad_start, 0), axis=-1)
    tm = MOE_TILE
    dest_tiles = dest.reshape(TOP_K, T // tm, tm).transpose(1, 0, 2).reshape(T // tm, TOP_K * tm)
    blk_first = jnp.arange(n_blk, dtype=I32)[:, None] * EXP_BLK
    blk_expert = jnp.minimum(jnp.sum((pad_end[None, :] <= blk_first).astype(I32), axis=-1), N_EXPERTS - 1)
    n_used = (pad_end[-1:] // EXP_BLK).astype(I32)

    rows = _dispatch(xp, dest_tiles, counts.astype(I32), pad_start.astype(I32), n_blk * EXP_BLK, td=tm)
    rows_out = _experts(rows, blk_expert, n_used, w_gate, w_up, w_down)
    wsgu = jnp.concatenate([ws_gate, ws_up], axis=-1).astype(BF16)
    return _combine(dest_tiles, gw.T, x, rows_out, wsgu, ws_down.astype(BF16), g, b, p,
                    ple_w.astype(BF16), ple_gate_w.astype(BF16), tm=tm)


def kernel(x, p, ln_g, ln_b, mlstm_w_in, mlstm_conv, mlstm_ig_bias, mlstm_fg_bias, mlstm_norm_g, mlstm_w_out, nsa_w_in, nsa_cmp_pos_k, nsa_cmp_pos_v, nsa_cmp_wk, nsa_cmp_wv, nsa_w_out, dil_w_in, dil_w_out, pool_w, pool_scale, router_w, router_bias, exp_w_gate, exp_w_up, exp_w_down, sh_w_gate, sh_w_up, sh_w_down, ple_w, ple_gate_w):
    B, S, D = x.shape
    T = B * S
    xf = x.reshape(T, D)
    pf = p.reshape(DEPTH, T, PLE_DIM)
    for i in range(DEPTH):
        kind, j = i % 4, i // 4
        g1, b1 = ln_g[i, 0], ln_b[i, 0]
        if kind == 0:
            y = _mlstm_mixer(xf, B, S, mlstm_w_in[j], mlstm_conv[j], mlstm_ig_bias[j], mlstm_fg_bias[j], mlstm_norm_g[j])
            xf = _outproj_ln(y, mlstm_w_out[j].astype(BF16), xf, g1, b1)
        elif kind == 1:
            y = _nsa_mixer(xf, B, S, nsa_w_in[j], nsa_cmp_pos_k[j], nsa_cmp_pos_v[j], nsa_cmp_wk[j], nsa_cmp_wv[j])
            xf = _outproj_ln(y, nsa_w_out[j].astype(BF16), xf, g1, b1)
        elif kind == 2:
            xf = _dilated_layer(xf, B, S, dil_w_in[j], dil_w_out[j], g1, b1)
        else:
            xf = _pool_layer(xf, B, S, pool_w[j], pool_scale[j], g1, b1)
        xf = _moe_ple_layer(xf, pf[i], router_w[i], router_bias[i], exp_w_gate[i], exp_w_up[i], exp_w_down[i],
                            sh_w_gate[i], sh_w_up[i], sh_w_down[i], ln_g[i, 1], ln_b[i, 1], ple_w[i], ple_gate_w[i])
    return xf.reshape(B, S, D)
```

```python
import functools

import numpy as np
import jax
import jax.numpy as jnp
from jax import lax
from jax.experimental import pallas as pl
from jax.experimental.pallas import tpu as pltpu

F32 = jnp.float32
BF16 = jnp.bfloat16
I32 = jnp.int32
HIGHEST = lax.Precision.HIGHEST

D_MODEL = 1024
DEPTH = 4
ALPHA = (2.0 * DEPTH) ** 0.25
LN_EPS = 1e-5
NEG_INF = -1e30
TINY = 1e-30
BELOW_NEG_INF = -3e38
ROPE_THETA = 10000.0
HEAD_DIM = 64
ROPE_HALF = HEAD_DIM // 2

MLSTM_HEADS = 8
MLSTM_QK_DIM = 64
MLSTM_V_DIM = 128
MLSTM_CONV = 4
MLSTM_L = 256

NSA_HEADS = 16
NSA_KV_HEADS = 4
NSA_J = NSA_HEADS // NSA_KV_HEADS
CMP_STRIDE = 16
CMP_BLK = 32
SEL_BLK = 64
N_SEL = 16
NSA_WINDOW = 512
FORCE_SCORE = 1e4
NSA_TQ = 128
NSA_TQ_SELECT = 256
NSA_TK = 512

DIL_HEADS = 16
DIL_GROUPS = ((128, 1), (512, 4), (2048, 16))
DIL_STEPS = 128

POOL_WINDOWS = (2, 4, 8, 16)
POOL_GROUP = 256
POOL_HALO = 16

N_EXPERTS = 64
TOP_K = 8
N_EXPERT_GROUPS = 8
TOPK_GROUPS = 4
EXPERT_FF = 256
ROUTED_SCALE = 2.5
EXP_BLK = 512
PLE_DIM = 256

VMEM_LIMIT = 48 * 1024 * 1024


def _cparams(*sem):
    return pltpu.CompilerParams(dimension_semantics=sem, vmem_limit_bytes=VMEM_LIMIT)


def _nt(a, b, **kw):
    return lax.dot_general(a, b, (((1,), (1,)), ((), ())), preferred_element_type=F32, **kw)


def _tn(a, b, **kw):
    return lax.dot_general(a, b, (((0,), (0,)), ((), ())), preferred_element_type=F32, **kw)


def _mm(a, b, **kw):
    return jnp.dot(a, b, preferred_element_type=F32, **kw)


def _split3(x):
    a = x.astype(BF16)
    r = x - a.astype(F32)
    b = r.astype(BF16)
    return a, b, (r - b.astype(F32)).astype(BF16)


def _mm01(x, e01):
    eb = e01.astype(BF16)
    a, b, c = _split3(x)
    return _mm(a, eb) + _mm(b, eb) + _mm(c, eb)


def _sigmoid(z):
    return 1.0 / (1.0 + jnp.exp(-z))


def _silu(z):
    return z * _sigmoid(z)


def _ln_rows(z, g, b):
    mu = jnp.mean(z, axis=-1, keepdims=True)
    d = z - mu
    var = jnp.mean(d * d, axis=-1, keepdims=True)
    return d * lax.rsqrt(var + LN_EPS) * g + b


def _proj_kernel(*refs, rope):
    if rope:
        x_ref, w_ref, cos_ref, sin_ref, o_ref, xb_ref = refs
    else:
        x_ref, w_ref, o_ref, xb_ref = refs

    @pl.when(pl.program_id(1) == 0)
    def _():
        xb_ref[...] = x_ref[...].astype(BF16)

    acc = _mm(xb_ref[...], w_ref[...])
    if rope:
        tn = acc.shape[1]
        lane = lax.broadcasted_iota(I32, acc.shape, 1)
        lo = (lane % HEAD_DIM) < ROPE_HALF
        rot = jnp.where(lo, pltpu.roll(acc, tn - ROPE_HALF, 1), pltpu.roll(acc, ROPE_HALF, 1))
        acc = acc * cos_ref[...] + rot * sin_ref[...]
    o_ref[...] = acc.astype(o_ref.dtype)


def _proj(x, w, *, out_dtype, tm, tn, rope_tables=None, seq=None):
    T, K = x.shape
    N = w.shape[1]
    assert T % tm == 0 and N % tn == 0
    in_specs = [pl.BlockSpec((tm, K), lambda i, j: (i, 0)), pl.BlockSpec((K, tn), lambda i, j: (0, j))]
    args = [x, w]
    if rope_tables is not None:
        nseq = seq // tm
        in_specs += [pl.BlockSpec((tm, tn), lambda i, j: (i % nseq, 0))] * 2
        args += list(rope_tables)
    return pl.pallas_call(
        functools.partial(_proj_kernel, rope=rope_tables is not None),
        name="proj_rope" if rope_tables is not None else "proj",
        grid=(T // tm, N // tn),
        in_specs=in_specs,
        out_specs=pl.BlockSpec((tm, tn), lambda i, j: (i, j)),
        out_shape=jax.ShapeDtypeStruct((T, N), out_dtype),
        scratch_shapes=[pltpu.VMEM((tm, K), BF16)],
        compiler_params=_cparams("parallel", "arbitrary"),
    )(*args)


def _rope_tables(pos, width):
    inv = ROPE_THETA ** (-jnp.arange(ROPE_HALF, dtype=F32) / ROPE_HALF)
    ang = pos.astype(F32)[:, None] * inv[None, :]
    cos, sin = jnp.cos(ang), jnp.sin(ang)
    cos64 = jnp.concatenate([cos, cos], -1)
    sin64 = jnp.concatenate([-sin, sin], -1)
    rep = width // HEAD_DIM
    return jnp.tile(cos64, (1, rep)), jnp.tile(sin64, (1, rep))


def _outproj_ln_kernel(y_ref, w_ref, x_ref, g_ref, b_ref, o_ref):
    y = _mm(y_ref[...].astype(BF16), w_ref[...])
    o_ref[...] = _ln_rows(ALPHA * x_ref[...] + y, g_ref[...], b_ref[...])


def _outproj_ln(y, w, x, g, b, *, tm=256):
    T, D = x.shape
    K = y.shape[1]
    row = lambda i: (i, 0)
    fixed = lambda i: (0, 0)
    return pl.pallas_call(
        _outproj_ln_kernel,
        name="outproj_ln",
        grid=(T // tm,),
        in_specs=[pl.BlockSpec((tm, K), row), pl.BlockSpec((K, D), fixed), pl.BlockSpec((tm, D), row),
                  pl.BlockSpec((1, D), fixed), pl.BlockSpec((1, D), fixed)],
        out_specs=pl.BlockSpec((tm, D), row),
        out_shape=jax.ShapeDtypeStruct((T, D), F32),
        compiler_params=_cparams("parallel"),
    )(y, w, x, g.reshape(1, D), b.reshape(1, D))


def _log_sigmoid(z):
    return jnp.minimum(z, 0.0) - jnp.log(1.0 + jnp.exp(-jnp.abs(z)))


def _mlstm_kernel(qk_ref, v_ref, o_ref, gc_ref, gr_ref, convw_ref, bias_c_ref, bias_r_ref, ng_ref,
                  tri_ref, triT_ref, out_ref, C_ref, n_ref, m_ref, ext_ref):
    L = qk_ref.shape[0]
    H, dk, dv = MLSTM_HEADS, MLSTM_QK_DIM, MLSTM_V_DIM

    @pl.when(pl.program_id(1) == 0)
    def _():
        C_ref[...] = jnp.zeros_like(C_ref)
        n_ref[...] = jnp.zeros_like(n_ref)
        m_ref[...] = jnp.zeros_like(m_ref)
        ext_ref[0:8, :] = jnp.zeros((8, ext_ref.shape[1]), F32)

    cur = qk_ref[...]
    ext_ref[8:8 + L, :] = cur
    acc = convw_ref[3:4, :] * cur
    for j in range(MLSTM_CONV - 1):
        acc = acc + convw_ref[j:j + 1, :] * ext_ref[5 + j:5 + j + L, :]
    ext_ref[0:8, :] = cur[L - 8:L, :]
    qk = _silu(acc)

    gc = gc_ref[...] + bias_c_ref[...]
    gr = gr_ref[...] + bias_r_ref[...]
    tri01 = tri_ref[...].astype(BF16)
    b_col = sum(_mm(tri01, t) for t in _split3(_log_sigmoid(gc)))
    b_row = _mm01(_log_sigmoid(gr[H:2 * H, :]), triT_ref[...])
    ig_row = gr[0:H, :]
    tri = lax.broadcasted_iota(I32, (L, L), 0) >= lax.broadcasted_iota(I32, (L, L), 1)

    for h in range(H):
        q = (qk[:, h * dk:(h + 1) * dk] * dk ** -0.5).astype(BF16)
        k = qk[:, H * dk + h * dk:H * dk + (h + 1) * dk]
        kb = k.astype(BF16)
        v = v_ref[:, h * dv:(h + 1) * dv].astype(BF16)
        b_c = b_col[:, H + h:H + h + 1]
        ig_c = gc[:, h:h + 1]
        b_r = b_row[h:h + 1, :]
        m_prev = m_ref[h:h + 1, 0:1]
        C = C_ref[h]
        n = n_ref[h:h + 1, :]

        logD = jnp.where(tri, b_c - b_r + ig_row[h:h + 1, :], NEG_INF)
        inter = b_c + m_prev
        m_t = jnp.maximum(inter, jnp.max(logD, axis=-1, keepdims=True))
        s = _nt(q, kb) * jnp.exp(logD - m_t)
        w_inter = jnp.exp(inter - m_t)
        num = w_inter * _mm(q, C.astype(BF16)) + _mm(s.astype(BF16), v)
        den = w_inter * jnp.sum(q.astype(F32) * n, axis=-1, keepdims=True) + jnp.sum(s, axis=-1, keepdims=True)
        hv = num / jnp.maximum(jnp.abs(den), jnp.exp(-m_t))

        mu = jnp.mean(hv, axis=-1, keepdims=True)
        d = hv - mu
        hn = d * lax.rsqrt(jnp.mean(d * d, axis=-1, keepdims=True) + LN_EPS)
        og = _sigmoid(o_ref[:, h * dv:(h + 1) * dv])
        out_ref[:, h * dv:(h + 1) * dv] = (hn * ng_ref[:, h * dv:(h + 1) * dv] * og).astype(out_ref.dtype)

        bL = b_c[L - 1:L, :]
        logw = bL - b_c + ig_c
        m_new = jnp.maximum(bL + m_prev, jnp.max(logw, axis=0, keepdims=True))
        decay = jnp.exp(bL + m_prev - m_new)
        kw = k * jnp.exp(logw - m_new)
        C_ref[h] = decay * C + _tn(kw.astype(BF16), v)
        n_ref[h:h + 1, :] = decay * n + jnp.sum(kw, axis=0, keepdims=True)
        m_ref[h:h + 1, :] = jnp.broadcast_to(m_new, (1, m_ref.shape[1]))


def _mlstm_mixer(x, B, S, w_in, conv_w, ig_bias, fg_bias, norm_g):
    T = B * S
    H, L = MLSTM_HEADS, MLSTM_L
    wb = w_in.astype(BF16)
    main = _proj(x, wb[:, :3 * D_MODEL], out_dtype=F32, tm=512, tn=1024)
    wg = jnp.pad(wb[:, 3 * D_MODEL:], ((0, 0), (0, 128 - 2 * H)))
    gates = _proj(x, wg, out_dtype=F32, tm=512, tn=128)
    gates_r = gates[:, :2 * H].reshape(B, S, 2 * H).transpose(0, 2, 1)
    bias16 = jnp.concatenate([ig_bias, fg_bias]).astype(F32)
    bias_c = jnp.pad(bias16, (0, 128 - 2 * H)).reshape(1, 128)
    bias_r = bias16.reshape(2 * H, 1)
    tri = jnp.tril(jnp.ones((L, L), F32))
    nc = S // L
    rowblk = lambda c: (lambda b, i: (b * nc + i, c))
    fixed = lambda b, i: (0, 0)
    return pl.pallas_call(
        _mlstm_kernel,
        name="mlstm",
        grid=(B, nc),
        in_specs=[pl.BlockSpec((L, D_MODEL), rowblk(0)), pl.BlockSpec((L, D_MODEL), rowblk(1)),
                  pl.BlockSpec((L, D_MODEL), rowblk(2)), pl.BlockSpec((L, 128), rowblk(0)),
                  pl.BlockSpec((None, 2 * H, L), lambda b, i: (b, 0, i)),
                  pl.BlockSpec((MLSTM_CONV, D_MODEL), fixed), pl.BlockSpec((1, 128), fixed),
                  pl.BlockSpec((2 * H, 1), fixed), pl.BlockSpec((1, D_MODEL), fixed),
                  pl.BlockSpec((L, L), fixed), pl.BlockSpec((L, L), fixed)],
        out_specs=pl.BlockSpec((L, D_MODEL), rowblk(0)),
        out_shape=jax.ShapeDtypeStruct((T, D_MODEL), BF16),
        scratch_shapes=[pltpu.VMEM((H, MLSTM_QK_DIM, MLSTM_V_DIM), F32), pltpu.VMEM((H, MLSTM_QK_DIM), F32),
                        pltpu.VMEM((H, 128), F32), pltpu.VMEM((L + 8, D_MODEL), F32)],
        compiler_params=_cparams("parallel", "arbitrary"),
    )(main, main, main, gates, gates_r, conv_w.astype(F32), bias_c, bias_r, norm_g.reshape(1, D_MODEL).astype(F32),
      tri, tri.T)


def _rot_cols(w):
    shp = w.shape
    w4 = w.reshape(shp[:-1] + (shp[-1] // HEAD_DIM, 2, ROPE_HALF))
    return jnp.flip(w4, axis=-2).reshape(shp)


def _nsa_compress_kernel(chk_ref, chv_ref, wk_lo_ref, wk_hi_ref, wv_lo_ref, wv_hi_ref, pk_ref, pv_ref, wk_ref, wv_ref,
                         cos_ref, sin_ref, kc_ref, vc_ref, sh_ref):
    NCH = chk_ref.shape[0]
    G = NSA_KV_HEADS
    sh_ref[NCH:NCH + 8, :] = jnp.zeros((8, sh_ref.shape[1]), F32)
    live = lax.broadcasted_iota(I32, (NCH, 1), 0) < NCH - 1

    def blocks(ch_ref, lo_ref, hi_ref, p_ref, w_ref):
        ch = ch_ref[...]
        n = lo_ref.shape[1]
        sh_ref[0:NCH, 0:n] = _mm(ch, hi_ref[...])
        const = _mm(p_ref[...].astype(BF16), w_ref[...])[0:1, :]
        return jnp.where(live, _mm(ch, lo_ref[...]) + sh_ref[1:NCH + 1, 0:n] + jnp.concatenate([const] * G, axis=1), 0.0)

    k2 = blocks(chk_ref, wk_lo_ref, wk_hi_ref, pk_ref, wk_ref)
    v2 = blocks(chv_ref, wv_lo_ref, wv_hi_ref, pv_ref, wv_ref)
    for g in range(G):
        raw = k2[:, 2 * g * HEAD_DIM:(2 * g + 1) * HEAD_DIM]
        rot = k2[:, (2 * g + 1) * HEAD_DIM:(2 * g + 2) * HEAD_DIM]
        kc_ref[g] = (raw * cos_ref[...] + rot * sin_ref[...]).astype(kc_ref.dtype)
        vc_ref[g] = v2[:, g * HEAD_DIM:(g + 1) * HEAD_DIM].astype(vc_ref.dtype)


def _nsa_cmp_select_kernel(q_ref, kc_ref, vc_ref, ovlT_ref, ocmp_ref, sel_ref):
    TQ = q_ref.shape[0]
    NCH = kc_ref.shape[0]
    NSB = ovlT_ref.shape[0]
    q0 = pl.program_id(2) * TQ
    t = q0 + lax.broadcasted_iota(I32, (TQ, 1), 0)
    cend = lax.broadcasted_iota(I32, (1, NCH), 1) * CMP_STRIDE + (CMP_BLK - 1)
    valid = cend <= t
    kc = kc_ref[...]
    vc = vc_ref[...]
    psum = jnp.zeros((TQ, NCH), F32)
    outs = []
    for j in range(NSA_J):
        qj = q_ref[:, j * HEAD_DIM:(j + 1) * HEAD_DIM]
        s = jnp.where(valid, _nt(qj, kc) * HEAD_DIM ** -0.5, NEG_INF)
        m = jnp.max(s, axis=-1, keepdims=True)
        e = jnp.where(valid, jnp.exp(s - m), 0.0)
        p = e / jnp.maximum(jnp.sum(e, axis=-1, keepdims=True), TINY)
        outs.append(_mm(p.astype(BF16), vc))
        psum = psum + p
    ocmp_ref[...] = jnp.concatenate(outs, axis=1).astype(ocmp_ref.dtype)

    ovl01 = ovlT_ref[...].astype(BF16)
    imp = sum(_nt(ovl01, t) for t in _split3(psum))
    nb = lax.broadcasted_iota(I32, (NSB, 1), 0)
    qblk = (q0 + lax.broadcasted_iota(I32, (1, TQ), 1)) // SEL_BLK
    forced = (nb == 0) | (nb == qblk) | (nb == qblk - 1)
    cur = jnp.where(forced, FORCE_SCORE, jnp.where(nb > qblk, NEG_INF, imp))
    sel = jnp.zeros((NSB, TQ), F32)
    for _ in range(min(N_SEL, NSB)):
        m = jnp.max(cur, axis=0, keepdims=True)
        idx = jnp.min(jnp.where(cur == m, nb, NSB), axis=0, keepdims=True)
        hit = nb == idx
        sel = jnp.where(hit, 1.0, sel)
        cur = jnp.where(hit, BELOW_NEG_INF, cur)
    sel_ref[...] = sel.astype(sel_ref.dtype)


NEG_FLOOR = -1e29


def _nsa_main_kernel(q_ref, ksT_ref, vs_ref, kwT_ref, vw_ref, sel_ref, ocmp_ref, gate_ref, gexp_ref, o_ref):
    TQ = q_ref.shape[0]
    S = vs_ref.shape[0]
    TK = min(NSA_TK, S)
    J = NSA_J
    q0 = pl.program_id(2) * TQ
    q4 = jnp.concatenate([q_ref[:, j * HEAD_DIM:(j + 1) * HEAD_DIM] for j in range(J)], axis=0) * HEAD_DIM ** -0.5
    t = q0 + lax.broadcasted_iota(I32, (TQ, 1), 0)
    selT = sel_ref[...]

    def attend(carry, kT, v1, bias):
        m, acc = carry
        s = _mm(q4, kT) + jnp.concatenate([bias] * J, axis=0)
        m_new = jnp.maximum(m, jnp.max(s, axis=-1, keepdims=True))
        e = jnp.exp(s - m_new)
        return m_new, jnp.exp(m - m_new) * acc + _mm(e.astype(BF16), v1)

    def finish(acc):
        return acc[:, :HEAD_DIM] / jnp.maximum(acc[:, HEAD_DIM:], TINY)

    init = (jnp.full((J * TQ, 1), NEG_FLOOR, F32), jnp.zeros((J * TQ, 2 * HEAD_DIM), F32))

    def body(c, carry):
        k0 = pl.multiple_of(c * TK, TK)
        kpos = k0 + lax.broadcasted_iota(I32, (1, TK), 1)
        expand = jnp.where(kpos // SEL_BLK == lax.broadcasted_iota(I32, (selT.shape[0], 1), 0), 1.0, 0.0).astype(BF16)
        bias = jnp.where((_tn(selT, expand) > 0.5) & (kpos <= t), 0.0, NEG_INF)
        return attend(carry, ksT_ref[:, pl.ds(k0, TK)], vs_ref[pl.ds(k0, TK), :], bias)

    nchunks = (q0 + TQ - 1) // TK + 1
    o_slc = finish(lax.fori_loop(0, nchunks, body, init)[1])

    span = min(NSA_WINDOW + TQ, S)
    start = pl.multiple_of(jnp.clip(q0 - NSA_WINDOW, 0, S - span), TQ)
    dist = t - (start + lax.broadcasted_iota(I32, (1, span), 1))
    bias = jnp.where((dist >= 0) & (dist < NSA_WINDOW), 0.0, NEG_INF)
    o_win = finish(attend(init, kwT_ref[:, pl.ds(start, span)], vw_ref[pl.ds(start, span), :], bias)[1])

    unstack = lambda o: jnp.concatenate([o[j * TQ:(j + 1) * TQ, :] for j in range(J)], axis=1)
    g = _sigmoid(gate_ref[...])
    gx = [_mm01(g, gexp_ref[c]) for c in range(3)]
    out = gx[0] * ocmp_ref[...].astype(F32) + gx[1] * unstack(o_slc) + gx[2] * unstack(o_win)
    o_ref[...] = out.astype(o_ref.dtype)


def _nsa_mixer(x, B, S, w_in, cmp_pos_k, cmp_pos_v, cmp_wk, cmp_wv):
    T = B * S
    H, G, J, dh = NSA_HEADS, NSA_KV_HEADS, NSA_J, HEAD_DIM
    kv = G * dh
    wb = w_in.astype(BF16)
    col = lambda a, n: wb[:, a:a + n]
    o_q, o_kc, o_vc, o_ks, o_vs, o_kw, o_vw, o_g = np.cumsum([0, H * dh] + [kv] * 6).tolist()
    pos = jnp.arange(S)
    w_rope = jnp.concatenate([col(o_q, H * dh), col(o_ks, kv), col(o_kw, kv)], axis=1)
    roped = _proj(x, w_rope, out_dtype=BF16, tm=512, tn=512, rope_tables=_rope_tables(pos, 512), seq=S)
    w_plain = jnp.concatenate([col(o_vs, kv), col(o_vw, kv), jnp.pad(col(o_g, 3 * H), ((0, 0), (0, 128 - 3 * H)))], axis=1)
    plain = _proj(x, w_plain, out_dtype=F32, tm=512, tn=w_plain.shape[1])

    heads_T = lambda a: a.reshape(B, S, G, dh).transpose(0, 2, 3, 1)
    heads = lambda a: a.reshape(B, S, G, dh).transpose(0, 2, 1, 3)
    ksT = heads_T(roped[:, H * dh:H * dh + kv])
    kwT = heads_T(roped[:, H * dh + kv:])
    with_ones = lambda v: jnp.concatenate([v, jnp.ones_like(v)], axis=-1)
    vs = with_ones(heads(plain[:, :kv]).astype(BF16))
    vw = with_ones(heads(plain[:, kv:2 * kv]).astype(BF16))
    gates = plain[:, 2 * kv:]

    nch = S // CMP_STRIDE
    cw = CMP_STRIDE * kv
    chk = _proj(x, col(o_kc, kv), out_dtype=BF16, tm=512, tn=kv).reshape(B, nch, cw)
    chv = _proj(x, col(o_vc, kv), out_dtype=BF16, tm=512, tn=kv).reshape(B, nch, cw)
    wk = cmp_wk.reshape(CMP_BLK * dh, dh)
    wk2 = jnp.concatenate([wk, _rot_cols(wk)], axis=1).astype(BF16)
    wv2 = cmp_wv.reshape(CMP_BLK * dh, dh).astype(BF16)
    eye = jnp.eye(G, dtype=BF16)

    def per_head(w, part):
        n = w.shape[1]
        w3 = w.reshape(2, CMP_STRIDE, dh, n)[part]
        return jnp.einsum('lde,gh->lgdhe', w3, eye).reshape(cw, G * n)

    flat8 = lambda p_: jnp.pad(p_.reshape(1, CMP_BLK * dh), ((0, 7), (0, 0))).astype(F32)
    cend = jnp.arange(nch) * CMP_STRIDE + CMP_BLK - 1
    cos_c, sin_c = _rope_tables(cend, dh)
    fixed1 = lambda b: (0, 0)
    full1 = lambda a: pl.BlockSpec(a.shape, fixed1)
    consts = (per_head(wk2, 0), per_head(wk2, 1), per_head(wv2, 0), per_head(wv2, 1), flat8(cmp_pos_k), flat8(cmp_pos_v),
              wk2, wv2, cos_c, sin_c)
    kc, vc = pl.pallas_call(
        _nsa_compress_kernel,
        name="nsa_compress",
        grid=(B,),
        in_specs=[pl.BlockSpec((None, nch, cw), lambda b: (b, 0, 0))] * 2 + [full1(c) for c in consts],
        out_specs=[pl.BlockSpec((None, G, nch, dh), lambda b: (b, 0, 0, 0))] * 2,
        out_shape=[jax.ShapeDtypeStruct((B, G, nch, dh), BF16)] * 2,
        scratch_shapes=[pltpu.VMEM((nch + 8, 2 * kv), F32)],
        compiler_params=_cparams("parallel"),
    )(chk, chv, *consts)

    nsb = S // SEL_BLK
    c_idx, s_idx = np.arange(nch), np.arange(nsb)
    ovl = ((c_idx[:, None] * CMP_STRIDE + CMP_BLK - 1 >= s_idx[None, :] * SEL_BLK)
           & (c_idx[:, None] * CMP_STRIDE < (s_idx[None, :] + 1) * SEL_BLK)).astype(np.float32)
    bgi = lambda b, g, i: (b, g, 0, 0)
    TS = min(NSA_TQ_SELECT, S)
    ns = S // TS
    sblk = lambda b, g, i: (b * ns + i, g)
    ocmp, sel = pl.pallas_call(
        _nsa_cmp_select_kernel,
        name="nsa_cmp_select",
        grid=(B, G, ns),
        in_specs=[pl.BlockSpec((TS, J * dh), sblk), pl.BlockSpec((None, None, nch, dh), bgi),
                  pl.BlockSpec((None, None, nch, dh), bgi), pl.BlockSpec((nsb, nch), lambda b, g, i: (0, 0))],
        out_specs=[pl.BlockSpec((TS, J * dh), sblk), pl.BlockSpec((None, None, nsb, TS), lambda b, g, i: (b, g, 0, i))],
        out_shape=[jax.ShapeDtypeStruct((T, H * dh), BF16), jax.ShapeDtypeStruct((B, G, nsb, S), BF16)],
        compiler_params=_cparams("parallel", "parallel", "parallel"),
    )(roped, kc, vc, jnp.asarray(ovl.T))
    TQ = NSA_TQ
    nq = S // TQ
    qblk = lambda b, g, i: (b * nq + i, g)

    gexp = np.zeros((G, 3, 128, J * dh), np.float32)
    for g in range(G):
        for c in range(3):
            for j in range(J):
                gexp[g, c, (g * J + j) * 3 + c, j * dh:(j + 1) * dh] = 1.0
    return pl.pallas_call(
        _nsa_main_kernel,
        name="nsa_main",
        grid=(B, G, nq),
        in_specs=[pl.BlockSpec((TQ, J * dh), qblk)] + [pl.BlockSpec((None, None, dh, S), bgi),
                                                       pl.BlockSpec((None, None, S, 2 * dh), bgi)] * 2 + [
            pl.BlockSpec((None, None, nsb, TQ), lambda b, g, i: (b, g, 0, i)),
            pl.BlockSpec((TQ, J * dh), qblk), pl.BlockSpec((TQ, 128), lambda b, g, i: (b * nq + i, 0)),
            pl.BlockSpec((None, 3, 128, J * dh), lambda b, g, i: (g, 0, 0, 0))],
        out_specs=pl.BlockSpec((TQ, J * dh), qblk),
        out_shape=jax.ShapeDtypeStruct((T, H * dh), BF16),
        compiler_params=_cparams("parallel", "parallel", "arbitrary"),
    )(roped, ksT, vs, kwT, vw, sel, ocmp, gates, jnp.asarray(gexp))


def _dil_kernel(q_ref, kp_ref, kc_ref, vp_ref, vc_ref, o_ref, lse_ref):
    NQ = q_ref.shape[0]
    prev_from = jnp.where(pl.program_id(2) == 0, NQ, 0)
    qi = lax.broadcasted_iota(I32, (NQ, 1), 0)
    kj = lax.broadcasted_iota(I32, (1, 2 * NQ), 1)
    dist = NQ + qi - kj
    valid = (dist >= 0) & (dist <= NQ) & (kj >= prev_from)
    lane = lax.broadcasted_iota(I32, (NQ, 128), 1)
    first = lane < HEAD_DIM
    lse_all = jnp.zeros((NQ, 128), F32)
    zero = jnp.zeros((), q_ref.dtype)
    for hp in range(DIL_HEADS // 2):
        sl = slice(hp * 128, (hp + 1) * 128)
        q2 = q_ref[:, sl]
        kb = jnp.concatenate([kp_ref[:, sl], kc_ref[:, sl]], axis=0)
        vb = jnp.concatenate([vp_ref[:, sl], vc_ref[:, sl]], axis=0)
        outs = []
        for sub in range(2):
            keep = first if sub == 0 else jnp.logical_not(first)
            s = jnp.where(valid, _nt(jnp.where(keep, q2, zero), kb) * HEAD_DIM ** -0.5, NEG_INF)
            m = jnp.max(s, axis=-1, keepdims=True)
            e = jnp.exp(s - m)
            den = jnp.sum(e, axis=-1, keepdims=True)
            outs.append(_mm((e / den).astype(BF16), vb))
            lse_all = jnp.where(lane == 2 * hp + sub, m + jnp.log(den), lse_all)
        o_ref[:, sl] = jnp.where(first, outs[0], outs[1]).astype(o_ref.dtype)
    lse_ref[...] = lse_all


def _proj_res_kernel(*refs, rope, dils):
    n_out = len(dils)
    x_ref, w_ref = refs[:2]
    rest = refs[2:]
    if rope:
        cos_ref, sin_ref = rest[:2]
        rest = rest[2:]
    out_refs, xb_ref, scr_ref = rest[:n_out], rest[n_out], rest[n_out + 1]
    tm, tn = x_ref.shape[0], w_ref.shape[1]

    @pl.when(pl.program_id(1) == 0)
    def _():
        xb_ref[...] = x_ref[...].astype(BF16)

    acc = _mm(xb_ref[...], w_ref[...])
    if rope:
        lane = lax.broadcasted_iota(I32, acc.shape, 1)
        lo = (lane % HEAD_DIM) < ROPE_HALF
        rot = jnp.where(lo, pltpu.roll(acc, tn - ROPE_HALF, 1), pltpu.roll(acc, ROPE_HALF, 1))
        acc = acc * cos_ref[...] + rot * sin_ref[...]
    if any(d > 1 for d in dils):
        for c in range(tn // 128):
            scr_ref[c] = acc[:, c * 128:(c + 1) * 128]
    for o_ref, dil in zip(out_refs, dils):
        if dil == 1:
            o_ref[...] = acc.astype(o_ref.dtype)
            continue
        for r in range(dil):
            for c in range(tn // 128):
                o_ref[:, r * tn + c * 128:r * tn + (c + 1) * 128] = (
                    scr_ref[c, pl.ds(r, tm // dil, stride=dil), :].astype(o_ref.dtype))


def _proj_res(x, w, dils, *, seq, rope, tm=512, tn=D_MODEL):
    T, K = x.shape
    N = w.shape[1]
    nparts = N // tn
    in_specs = [pl.BlockSpec((tm, K), lambda i, j: (i, 0)), pl.BlockSpec((K, tn), lambda i, j: (0, j))]
    args = [x, w]
    if rope:
        nseq = seq // tm
        in_specs += [pl.BlockSpec((tm, tn), lambda i, j: (i % nseq, 0))] * 2
        args += list(_rope_tables(jnp.arange(seq), tn))
    return pl.pallas_call(
        functools.partial(_proj_res_kernel, rope=rope, dils=dils),
        name="proj_residue_rope" if rope else "proj_residue",
        grid=(T // tm, nparts),
        in_specs=in_specs,
        out_specs=[pl.BlockSpec((tm // d, d * tn), lambda i, j: (i, j)) for d in dils],
        out_shape=[jax.ShapeDtypeStruct((T // d, nparts * d * tn), BF16) for d in dils],
        scratch_shapes=[pltpu.VMEM((tm, K), BF16), pltpu.VMEM((tn // 128, tm, 128), F32)],
        compiler_params=_cparams("parallel", "arbitrary"),
    )(*args)


def _dil_group(qk, v, B, S, dil):
    NQ = DIL_STEPS
    U = S // dil
    nb = U // NQ
    W = D_MODEL
    qk_view = qk.reshape(B, U, 2 * dil * W)
    v_view = v.reshape(B, U, dil * W)
    cur = lambda part: (lambda b, r, n: (b, n, part * dil + r))
    prev = lambda part: (lambda b, r, n: (b, jnp.maximum(n - 1, 0), part * dil + r))
    blk = lambda f: pl.BlockSpec((None, NQ, W), f)
    o, lse = pl.pallas_call(
        _dil_kernel,
        name=f"dilated_attn_{dil}",
        grid=(B, dil, nb),
        in_specs=[blk(cur(0)), blk(prev(1)), blk(cur(1)), blk(prev(0)), blk(cur(0))],
        out_specs=[pl.BlockSpec((None, NQ, W), lambda b, r, n: (b, n, r)),
                   pl.BlockSpec((None, NQ, 128), lambda b, r, n: (b, n, r))],
        out_shape=[jax.ShapeDtypeStruct((B, U, dil * W), BF16), jax.ShapeDtypeStruct((B, U, dil * 128), F32)],
        compiler_params=_cparams("parallel", "parallel", "arbitrary"),
    )(qk_view, qk_view, qk_view, v_view, v_view)
    return o.reshape(B * U, dil * W), lse.reshape(B * U, dil * 128)


def _dil_outproj_ln_kernel(o0_ref, o1_ref, o2_ref, l0_ref, l1_ref, l2_ref, hexp_ref, w_ref, x_ref, g_ref, b_ref, out_ref,
                           oscr_ref, lscr_ref, *, dils):
    tm, W = x_ref.shape

    def natural(o_ref, l_ref, dil):
        if dil == 1:
            return o_ref[...].astype(F32), l_ref[...]
        n = tm // dil
        for r in range(dil):
            lscr_ref[0, pl.ds(r, n, stride=dil), :] = l_ref[:, r * 128:(r + 1) * 128]
            for c in range(W // 128):
                oscr_ref[c, pl.ds(r, n, stride=dil), :] = o_ref[:, r * W + c * 128:r * W + (c + 1) * 128].astype(F32)
        return jnp.concatenate([oscr_ref[c] for c in range(W // 128)], axis=1), lscr_ref[0]

    os_, ls = zip(*[natural(o, l, d) for o, l, d in zip((o0_ref, o1_ref, o2_ref), (l0_ref, l1_ref, l2_ref), dils)])
    m = jnp.maximum(jnp.maximum(ls[0], ls[1]), ls[2])
    es = [jnp.exp(l - m) for l in ls]
    tot = es[0] + es[1] + es[2]
    y = jnp.zeros((tm, W), F32)
    for e, o in zip(es, os_):
        y = y + _mm01(e / tot, hexp_ref[...]) * o
    z = ALPHA * x_ref[...] + _mm(y.astype(BF16), w_ref[...])
    out_ref[...] = _ln_rows(z, g_ref[...], b_ref[...])


def _dilated_layer(x, B, S, w_in, w_out, g, b, *, tm=256):
    T, D = x.shape
    dils = tuple(d for _, d in DIL_GROUPS)
    wb = w_in.astype(BF16)
    vals = _proj_res(x, wb[:, 2 * len(dils) * D:], dils, seq=S, rope=False)
    outs = []
    for gi, dil in enumerate(dils):
        qk, = _proj_res(x, wb[:, 2 * gi * D:(2 * gi + 2) * D], (dil,), seq=S, rope=True)
        outs.append(_dil_group(qk, vals[gi], B, S, dil))
    hexp = np.zeros((128, D), np.float32)
    for h in range(DIL_HEADS):
        hexp[h, h * HEAD_DIM:(h + 1) * HEAD_DIM] = 1.0
    row = lambda i: (i, 0)
    fixed = lambda i: (0, 0)
    return pl.pallas_call(
        functools.partial(_dil_outproj_ln_kernel, dils=dils),
        name="dilated_outproj_ln",
        grid=(T // tm,),
        in_specs=[pl.BlockSpec((tm // d, d * D), row) for d in dils] + [pl.BlockSpec((tm // d, d * 128), row) for d in dils]
        + [pl.BlockSpec((128, D), fixed), pl.BlockSpec((D, D), fixed), pl.BlockSpec((tm, D), row),
           pl.BlockSpec((1, D), fixed), pl.BlockSpec((1, D), fixed)],
        out_specs=pl.BlockSpec((tm, D), row),
        out_shape=jax.ShapeDtypeStruct((T, D), F32),
        scratch_shapes=[pltpu.VMEM((D // 128, tm, 128), F32), pltpu.VMEM((1, tm, 128), F32)],
        compiler_params=_cparams("parallel"),
    )(outs[0][0], outs[1][0], outs[2][0], outs[0][1], outs[1][1], outs[2][1], jnp.asarray(hexp),
      w_out.astype(BF16), x, g.reshape(1, D), b.reshape(1, D))


def _pool_ln_kernel(x_ref, halo_ref, w_ref, scale_ref, g_ref, b_ref, o_ref, ext_ref):
    TS = x_ref.shape[0]
    s = pl.program_id(1)
    x = x_ref[...]
    ext_ref[0:POOL_HALO, :] = jnp.where(s == 0, 0.0, halo_ref[...])
    ext_ref[POOL_HALO:POOL_HALO + TS, :] = x
    cnt = (s * TS + lax.broadcasted_iota(I32, (TS, 1), 0) + 1).astype(F32)
    ys = []
    for gi, w in enumerate(POOL_WINDOWS):
        sl = slice(gi * POOL_GROUP, (gi + 1) * POOL_GROUP)
        xg = x[:, sl]
        tot = xg
        for j in range(1, w):
            tot = tot + ext_ref[POOL_HALO - j:POOL_HALO - j + TS, sl]
        mean = tot / jnp.minimum(cnt, float(w))
        ys.append(_mm((mean - xg).astype(BF16), w_ref[gi]))
    y = jnp.concatenate(ys, axis=1) * scale_ref[...]
    o_ref[...] = _ln_rows(ALPHA * x + y, g_ref[...], b_ref[...])


def _pool_layer(x, B, S, w_grp, scale, g, b, *, ts=512):
    T, D = x.shape
    ns = S // ts
    hb = ts // POOL_HALO
    fixed = lambda bb, s: (0, 0)
    return pl.pallas_call(
        _pool_ln_kernel,
        name="pool_ln",
        grid=(B, ns),
        in_specs=[pl.BlockSpec((ts, D), lambda bb, s: (bb * ns + s, 0)),
                  pl.BlockSpec((POOL_HALO, D), lambda bb, s: (jnp.maximum((bb * ns + s) * hb - 1, 0), 0)),
                  pl.BlockSpec((len(POOL_WINDOWS), POOL_GROUP, POOL_GROUP), lambda bb, s: (0, 0, 0)),
                  pl.BlockSpec((1, D), fixed), pl.BlockSpec((1, D), fixed), pl.BlockSpec((1, D), fixed)],
        out_specs=pl.BlockSpec((ts, D), lambda bb, s: (bb * ns + s, 0)),
        out_shape=jax.ShapeDtypeStruct((T, D), F32),
        scratch_shapes=[pltpu.VMEM((ts + POOL_HALO, D), F32)],
        compiler_params=_cparams("parallel", "arbitrary"),
    )(x, x, w_grp.astype(BF16), scale.reshape(1, D), g.reshape(1, D), b.reshape(1, D))


U32 = jnp.uint32
HI16 = 0xFFFF0000
ROW_WORDS = D_MODEL // 2


def _pack_halves(v):
    h = v.shape[1] // 2
    bits = lambda a: lax.bitcast_convert_type(a.astype(BF16).astype(F32), U32)
    return (bits(v[:, :h]) >> 16) | (bits(v[:, h:]) & U32(HI16))


def _unpack_halves(w):
    lo = lax.bitcast_convert_type(w << 16, F32)
    hi = lax.bitcast_convert_type(w & U32(HI16), F32)
    return jnp.concatenate([lo, hi], axis=1)


def _router_kernel(x_ref, wT_ref, bias_ref, triu_ref, eidx_ref, gw_ref, rank_ref, cnt_ref, xp_ref, carry_ref):
    E = N_EXPERTS
    per = E // N_EXPERT_GROUPS
    TM = x_ref.shape[0]

    @pl.when(pl.program_id(0) == 0)
    def _():
        carry_ref[...] = jnp.zeros_like(carry_ref)

    x = x_ref[...]
    xp_ref[...] = _pack_halves(x)

    scores = _sigmoid(_nt(wT_ref[...], x, precision=HIGHEST))
    biased = scores + bias_ref[...]
    eio = lax.broadcasted_iota(I32, (E, TM), 0)

    gio = lax.broadcasted_iota(I32, (per, TM), 0)
    gscore = []
    for gidx in range(N_EXPERT_GROUPS):
        slab = biased[gidx * per:(gidx + 1) * per, :]
        m1 = jnp.max(slab, axis=0, keepdims=True)
        i1 = jnp.min(jnp.where(slab == m1, gio, per), axis=0, keepdims=True)
        m2 = jnp.max(jnp.where(gio == i1, BELOW_NEG_INF, slab), axis=0, keepdims=True)
        gscore.append(m1 + m2)
    slabs = []
    for gidx in range(N_EXPERT_GROUPS):
        beat = jnp.zeros((1, TM), F32)
        for o in range(N_EXPERT_GROUPS):
            if o == gidx:
                continue
            wins = (gscore[o] >= gscore[gidx]) if o < gidx else (gscore[o] > gscore[gidx])
            beat = beat + jnp.where(wins, 1.0, 0.0)
        keep = beat < float(TOPK_GROUPS)
        slabs.append(jnp.where(keep, biased[gidx * per:(gidx + 1) * per, :], NEG_INF))
    cur = jnp.concatenate(slabs, axis=0)

    picked = jnp.zeros((E, TM), F32)
    idxs, vals = [], []
    for _ in range(TOP_K):
        m = jnp.max(cur, axis=0, keepdims=True)
        idx = jnp.min(jnp.where(cur == m, eio, E), axis=0, keepdims=True)
        hit = eio == idx
        picked = jnp.where(hit, 1.0, picked)
        cur = jnp.where(hit, BELOW_NEG_INF, cur)
        idxs.append(idx)
        vals.append(jnp.sum(jnp.where(hit, scores, 0.0), axis=0, keepdims=True))
    total = vals[0]
    for v in vals[1:]:
        total = total + v

    pos = _mm(picked.astype(BF16), triu_ref[...]) + carry_ref[...]
    for k in range(TOP_K):
        eidx_ref[k:k + 1, :] = idxs[k]
        gw_ref[k:k + 1, :] = vals[k] / total * ROUTED_SCALE
        rank_ref[k:k + 1, :] = jnp.sum(jnp.where(eio == idxs[k], pos, 0.0), axis=0, keepdims=True).astype(I32)
    carry_ref[...] = carry_ref[...] + jnp.sum(picked, axis=1, keepdims=True)
    cnt_ref[...] = carry_ref[...].astype(I32)


def _router(x, router_w, router_bias, *, tm=512):
    T, D = x.shape
    E = N_EXPERTS
    triu = jnp.triu(jnp.ones((tm, tm), F32), k=1).astype(BF16)
    col = lambda i: (0, i)
    fixed = lambda i: (0, 0)
    return pl.pallas_call(
        _router_kernel,
        name="moe_router",
        grid=(T // tm,),
        in_specs=[pl.BlockSpec((tm, D), lambda i: (i, 0)), pl.BlockSpec((E, D), fixed),
                  pl.BlockSpec((E, 1), fixed), pl.BlockSpec((tm, tm), fixed)],
        out_specs=[pl.BlockSpec((TOP_K, tm), col), pl.BlockSpec((TOP_K, tm), col), pl.BlockSpec((TOP_K, tm), col),
                   pl.BlockSpec((E, 1), fixed), pl.BlockSpec((tm, ROW_WORDS), lambda i: (i, 0))],
        out_shape=[jax.ShapeDtypeStruct((TOP_K, T), I32), jax.ShapeDtypeStruct((TOP_K, T), F32),
                   jax.ShapeDtypeStruct((TOP_K, T), I32), jax.ShapeDtypeStruct((E, 1), I32),
                   jax.ShapeDtypeStruct((T, ROW_WORDS), U32)],
        scratch_shapes=[pltpu.VMEM((E, 1), F32)],
        compiler_params=_cparams("arbitrary"),
    )(x, router_w.T.astype(F32), router_bias.reshape(E, 1).astype(F32), triu)


def _tile_indices(dest_hbm, idx_ref, sem_idx, n_idx):
    i = pl.program_id(0)

    def idx_copy(step):
        slot = step % 2
        return pltpu.make_async_copy(dest_hbm.at[step], idx_ref.at[pl.ds(slot * n_idx, n_idx)], sem_idx.at[slot])

    @pl.when(i == 0)
    def _():
        idx_copy(0).start()

    idx_copy(i).wait()

    @pl.when(i + 1 < pl.num_programs(0))
    def _():
        idx_copy(i + 1).start()

    return (i % 2) * n_idx


def _dispatch_kernel(cnt_ref, pstart_ref, dest_hbm, x_ref, rows_ref, idx_ref, zero_ref, sem_idx, sem_rows, sem_zero):
    TD = x_ref.shape[0]

    @pl.when(pl.program_id(0) == 0)
    def _():
        zero_ref[...] = jnp.zeros_like(zero_ref)

        def per_expert(e, c):
            n = cnt_ref[e]
            base = pstart_ref[e]
            pad_to = (n + EXP_BLK - 1) // EXP_BLK * EXP_BLK
            fill = lambda r: pltpu.make_async_copy(zero_ref.at[pl.ds(0, 1)], rows_ref.at[base + r], sem_zero)

            def start(r, c2):
                fill(r).start()
                return c2

            def wait(r, c2):
                fill(r).wait()
                return c2

            lax.fori_loop(n, pad_to, start, 0)
            lax.fori_loop(n, pad_to, wait, 0)
            return c

        lax.fori_loop(0, N_EXPERTS, per_expert, 0)

    base = _tile_indices(dest_hbm, idx_ref, sem_idx, TOP_K * TD)
    scatter = lambda t, k: pltpu.make_async_copy(x_ref.at[pl.ds(t, 1)], rows_ref.at[idx_ref[base + k * TD + t]], sem_rows)

    def issue(t, c):
        for k in range(TOP_K):
            scatter(t, k).start(priority=k % 2)
        return c

    def drain(t, c):
        for k in range(TOP_K):
            scatter(t, k).wait()
        return c

    lax.fori_loop(0, TD, issue, 0, unroll=8)
    lax.fori_loop(0, TD, drain, 0, unroll=8)


def _dispatch(xp, dest_tiles, counts, pad_start, n_rows, *, td):
    T = xp.shape[0]
    return pl.pallas_call(
        _dispatch_kernel,
        name="moe_dispatch",
        grid_spec=pltpu.PrefetchScalarGridSpec(
            num_scalar_prefetch=2,
            grid=(T // td,),
            in_specs=[pl.BlockSpec(memory_space=pl.ANY), pl.BlockSpec((td, ROW_WORDS), lambda i, c, p: (i, 0))],
            out_specs=pl.BlockSpec(memory_space=pl.ANY),
            scratch_shapes=[pltpu.SMEM((2 * TOP_K * td,), I32), pltpu.VMEM((8, ROW_WORDS), U32),
                            pltpu.SemaphoreType.DMA((2,)), pltpu.SemaphoreType.DMA(()), pltpu.SemaphoreType.DMA(())],
        ),
        out_shape=jax.ShapeDtypeStruct((n_rows, 1, ROW_WORDS), U32),
        compiler_params=_cparams("arbitrary"),
    )(counts, pad_start, dest_tiles, xp)


def _expert_kernel(be_ref, nu_ref, rows_hbm, wg_ref, wu_ref, wd_ref, out_hbm, xbuf, obuf, wgu_bf, wd_bf, sem_in, sem_out):
    i = pl.program_id(0)
    nu = nu_ref[0]

    @pl.when((i < nu) & ((i == 0) | (be_ref[i] != be_ref[jnp.maximum(i - 1, 0)])))
    def _():
        wgu_bf[:, :EXPERT_FF] = wg_ref[0].astype(BF16)
        wgu_bf[:, EXPERT_FF:] = wu_ref[0].astype(BF16)
        wd_bf[...] = wd_ref[0].astype(BF16)

    blk = lambda step: pl.ds(pl.multiple_of(step * EXP_BLK, EXP_BLK), EXP_BLK)
    in_copy = lambda step: pltpu.make_async_copy(rows_hbm.at[blk(step), 0], xbuf.at[step % 2], sem_in.at[step % 2])
    out_copy = lambda step: pltpu.make_async_copy(obuf.at[step % 2], out_hbm.at[blk(step), 0], sem_out.at[step % 2])

    @pl.when(i == 0)
    def _():
        in_copy(0).start()

    @pl.when(i < nu)
    def _():
        in_copy(i).wait()

        @pl.when(i + 1 < nu)
        def _():
            in_copy(i + 1).start()

        @pl.when(i >= 2)
        def _():
            out_copy(i - 2).wait()

        slot = i % 2
        gu = _mm(_unpack_halves(xbuf[slot]).astype(BF16), wgu_bf[...])
        h = _silu(gu[:, :EXPERT_FF]) * gu[:, EXPERT_FF:]
        obuf[slot] = _pack_halves(_mm(h.astype(BF16), wd_bf[...]))
        out_copy(i).start()

    @pl.when(i == pl.num_programs(0) - 1)
    def _():
        @pl.when(nu >= 2)
        def _():
            out_copy(nu - 2).wait()

        out_copy(nu - 1).wait()


def _experts(rows, blk_expert, n_used, w_gate, w_up, w_down, layer):
    R = rows.shape[0]
    D = w_down.shape[-1]
    n_blk = R // EXP_BLK
    expert = lambda i, be, nu: (layer, be[jnp.minimum(i, nu[0] - 1)], 0, 0)
    return pl.pallas_call(
        _expert_kernel,
        name="moe_experts",
        grid_spec=pltpu.PrefetchScalarGridSpec(
            num_scalar_prefetch=2,
            grid=(n_blk,),
            in_specs=[pl.BlockSpec(memory_space=pl.ANY), pl.BlockSpec((None, 1, D, EXPERT_FF), expert),
                      pl.BlockSpec((None, 1, D, EXPERT_FF), expert), pl.BlockSpec((None, 1, EXPERT_FF, D), expert)],
            out_specs=pl.BlockSpec(memory_space=pl.ANY),
            scratch_shapes=[pltpu.VMEM((2, EXP_BLK, ROW_WORDS), U32), pltpu.VMEM((2, EXP_BLK, ROW_WORDS), U32),
                            pltpu.VMEM((D, 2 * EXPERT_FF), BF16), pltpu.VMEM((EXPERT_FF, D), BF16),
                            pltpu.SemaphoreType.DMA((2,)), pltpu.SemaphoreType.DMA((2,))],
        ),
        out_shape=jax.ShapeDtypeStruct((R, 1, ROW_WORDS), U32),
        compiler_params=_cparams("arbitrary"),
    )(blk_expert, n_used, rows, w_gate, w_up, w_down)


def _combine_kernel(dest_hbm, gw_ref, x_ref, rows_ref, wsgu_ref, wsd_ref, g_ref, b_ref, p_ref, wp_ref, wpg_ref,
                    o_ref, idx_ref, buf_ref, sem_idx, sem_rows):
    TM = x_ref.shape[0]
    n_idx = TOP_K * TM
    i = pl.program_id(0)
    n = pl.num_programs(0)

    def idx_copy(step):
        slot = step % 3
        return pltpu.make_async_copy(dest_hbm.at[step], idx_ref.at[pl.ds(slot * n_idx, n_idx)], sem_idx.at[slot])

    def row_loop(step, slot, start):
        base = (step % 3) * n_idx

        def body(t, c):
            for k in range(TOP_K):
                cp = pltpu.make_async_copy(rows_ref.at[idx_ref[base + k * TM + t]], buf_ref.at[slot, k, pl.ds(t, 1)],
                                           sem_rows.at[slot])
                if start:
                    cp.start(priority=k % 2)
                else:
                    cp.wait()
            return c

        lax.fori_loop(0, TM, body, 0, unroll=8)

    def for_parity(step, fn):
        for slot in range(2):
            pl.when(step % 2 == slot)(functools.partial(fn, slot))

    @pl.when(i == 0)
    def _():
        idx_copy(0).start()
        idx_copy(0).wait()
        row_loop(0, 0, True)

        @pl.when(n > 1)
        def _():
            idx_copy(1).start()

    @pl.when(i + 1 < n)
    def _():
        idx_copy(i + 1).wait()

        @pl.when(i + 2 < n)
        def _():
            idx_copy(i + 2).start()

        for_parity(i + 1, lambda slot: row_loop(i + 1, slot, True))

    x = x_ref[...]
    gu = _mm(x.astype(BF16), wsgu_ref[...])
    ff = gu.shape[1] // 2
    f = _mm((_silu(gu[:, :ff]) * gu[:, ff:]).astype(BF16), wsd_ref[...])
    for_parity(i, lambda slot: row_loop(i, slot, False))
    gw = gw_ref[...]
    cur = i % 2
    for k in range(TOP_K):
        f = f + gw[:, k:k + 1] * _unpack_halves(buf_ref[cur, k])
    x2 = _ln_rows(ALPHA * x + f, g_ref[...], b_ref[...])
    gate = _sigmoid(_mm(x2.astype(BF16), wpg_ref[...]))
    o_ref[...] = x2 + gate * _mm(p_ref[...].astype(BF16), wp_ref[...])


def _combine(dest_tiles, gw, x, rows_out, wsgu, wsd, g, b, p, wp, wpg, *, tm):
    T, D = x.shape
    row = lambda i: (i, 0)
    fixed = lambda i: (0, 0)
    full = lambda a: pl.BlockSpec(a.shape, fixed)
    return pl.pallas_call(
        _combine_kernel,
        name="moe_combine",
        grid=(T // tm,),
        in_specs=[pl.BlockSpec(memory_space=pl.ANY), pl.BlockSpec((tm, TOP_K), row), pl.BlockSpec((tm, D), row),
                  pl.BlockSpec(memory_space=pl.ANY), full(wsgu), full(wsd), pl.BlockSpec((1, D), fixed),
                  pl.BlockSpec((1, D), fixed), pl.BlockSpec((tm, PLE_DIM), row), full(wp), full(wpg)],
        out_specs=pl.BlockSpec((tm, D), row),
        out_shape=jax.ShapeDtypeStruct((T, D), F32),
        scratch_shapes=[pltpu.SMEM((3 * TOP_K * tm,), I32), pltpu.VMEM((2, TOP_K, tm, ROW_WORDS), U32),
                        pltpu.SemaphoreType.DMA((3,)), pltpu.SemaphoreType.DMA((2,))],
        compiler_params=_cparams("arbitrary"),
    )(dest_tiles, gw, x, rows_out, wsgu, wsd, g.reshape(1, D), b.reshape(1, D), p, wp, wpg)


MOE_TILE = 256


def _moe_ple_layer(x, p, layer, router_w, router_bias, w_gate, w_up, w_down, ws_gate, ws_up, ws_down, g, b, ple_w, ple_gate_w):
    T, D = x.shape
    eidx, gw, rank, counts, xp = _router(x, router_w, router_bias)
    counts = counts.reshape(N_EXPERTS)
    padded = (counts + EXP_BLK - 1) // EXP_BLK * EXP_BLK
    pad_end = jnp.cumsum(padded)
    pad_start = pad_end - padded
    n_blk = T * TOP_K // EXP_BLK + N_EXPERTS
    e_iota = jnp.arange(N_EXPERTS, dtype=I32)
    dest = rank + jnp.sum(jnp.where(eidx[..., None] == e_iota, pad_start, 0), axis=-1)
    tm = MOE_TILE
    dest_tiles = dest.reshape(TOP_K, T // tm, tm).transpose(1, 0, 2).reshape(T // tm, TOP_K * tm)
    blk_first = jnp.arange(n_blk, dtype=I32)[:, None] * EXP_BLK
    blk_expert = jnp.minimum(jnp.sum((pad_end[None, :] <= blk_first).astype(I32), axis=-1), N_EXPERTS - 1)
    n_used = (pad_end[-1:] // EXP_BLK).astype(I32)

    rows = _dispatch(xp, dest_tiles, counts.astype(I32), pad_start.astype(I32), n_blk * EXP_BLK, td=tm)
    rows_out = _experts(rows, blk_expert, n_used, w_gate, w_up, w_down, layer)
    wsgu = jnp.concatenate([ws_gate, ws_up], axis=-1).astype(BF16)
    return _combine(dest_tiles, gw.T, x, rows_out, wsgu, ws_down.astype(BF16), g, b, p,
                    ple_w.astype(BF16), ple_gate_w.astype(BF16), tm=tm)


def kernel(x, p, ln_g, ln_b, mlstm_w_in, mlstm_conv, mlstm_ig_bias, mlstm_fg_bias, mlstm_norm_g, mlstm_w_out, nsa_w_in, nsa_cmp_pos_k, nsa_cmp_pos_v, nsa_cmp_wk, nsa_cmp_wv, nsa_w_out, dil_w_in, dil_w_out, pool_w, pool_scale, router_w, router_bias, exp_w_gate, exp_w_up, exp_w_down, sh_w_gate, sh_w_up, sh_w_down, ple_w, ple_gate_w):
    B, S, D = x.shape
    T = B * S
    xf = x.reshape(T, D)
    pf = p.reshape(DEPTH, T, PLE_DIM)
    for i in range(DEPTH):
        kind, j = i % 4, i // 4
        g1, b1 = ln_g[i, 0], ln_b[i, 0]
        if kind == 0:
            y = _mlstm_mixer(xf, B, S, mlstm_w_in[j], mlstm_conv[j], mlstm_ig_bias[j], mlstm_fg_bias[j], mlstm_norm_g[j])
            xf = _outproj_ln(y, mlstm_w_out[j].astype(BF16), xf, g1, b1)
        elif kind == 1:
            y = _nsa_mixer(xf, B, S, nsa_w_in[j], nsa_cmp_pos_k[j], nsa_cmp_pos_v[j], nsa_cmp_wk[j], nsa_cmp_wv[j])
            xf = _outproj_ln(y, nsa_w_out[j].astype(BF16), xf, g1, b1)
        elif kind == 2:
            xf = _dilated_layer(xf, B, S, dil_w_in[j], dil_w_out[j], g1, b1)
        else:
            xf = _pool_layer(xf, B, S, pool_w[j], pool_scale[j], g1, b1)
        xf = _moe_ple_layer(xf, pf[i], i, router_w[i], router_bias[i], exp_w_gate, exp_w_up, exp_w_down,
                            sh_w_gate[i], sh_w_up[i], sh_w_down[i], ln_g[i, 1], ln_b[i, 1], ple_w[i], ple_gate_w[i])
    return xf.reshape(B, S, D)
```

```python
import functools

import numpy as np
import jax
import jax.numpy as jnp
from jax import lax
from jax.experimental import pallas as pl
from jax.experimental.pallas import tpu as pltpu

F32 = jnp.float32
BF16 = jnp.bfloat16
I32 = jnp.int32
HIGHEST = lax.Precision.HIGHEST

D_MODEL = 1024
DEPTH = 4
ALPHA = (2.0 * DEPTH) ** 0.25
LN_EPS = 1e-5
NEG_INF = -1e30
TINY = 1e-30
BELOW_NEG_INF = -3e38
NEG_FLOOR = -1e29
ROPE_THETA = 10000.0
HEAD_DIM = 64
ROPE_HALF = HEAD_DIM // 2

MLSTM_HEADS = 8
MLSTM_QK_DIM = 64
MLSTM_V_DIM = 128
MLSTM_CONV = 4
MLSTM_L = 256

NSA_HEADS = 16
NSA_KV_HEADS = 4
NSA_J = NSA_HEADS // NSA_KV_HEADS
CMP_STRIDE = 16
CMP_BLK = 32
SEL_BLK = 64
N_SEL = 16
NSA_WINDOW = 512
FORCE_SCORE = 1e4
NSA_TQ = 256
NSA_TQ_SELECT = 256
NSA_TK = 512

DIL_HEADS = 16
DIL_GROUPS = ((128, 1), (512, 4), (2048, 16))
DIL_STEPS = 128

POOL_WINDOWS = (2, 4, 8, 16)
POOL_GROUP = 256
POOL_HALO = 16

N_EXPERTS = 64
TOP_K = 8
N_EXPERT_GROUPS = 8
TOPK_GROUPS = 4
EXPERT_FF = 256
ROUTED_SCALE = 2.5
EXP_BLK = 512
PLE_DIM = 256

VMEM_LIMIT = 48 * 1024 * 1024


def _cparams(*sem):
    return pltpu.CompilerParams(dimension_semantics=sem, vmem_limit_bytes=VMEM_LIMIT)


def _nt(a, b, **kw):
    return lax.dot_general(a, b, (((1,), (1,)), ((), ())), preferred_element_type=F32, **kw)


def _tn(a, b, **kw):
    return lax.dot_general(a, b, (((0,), (0,)), ((), ())), preferred_element_type=F32, **kw)


def _mm(a, b, **kw):
    return jnp.dot(a, b, preferred_element_type=F32, **kw)


def _split3(x):
    a = x.astype(BF16)
    r = x - a.astype(F32)
    b = r.astype(BF16)
    return a, b, (r - b.astype(F32)).astype(BF16)


def _mm01(x, e01):
    eb = e01.astype(BF16)
    a, b, c = _split3(x)
    return _mm(a, eb) + _mm(b, eb) + _mm(c, eb)


def _sigmoid(z):
    return 1.0 / (1.0 + jnp.exp(-z))


def _silu(z):
    return z * _sigmoid(z)


def _ln_rows(z, g, b):
    mu = jnp.mean(z, axis=-1, keepdims=True)
    d = z - mu
    var = jnp.mean(d * d, axis=-1, keepdims=True)
    return d * lax.rsqrt(var + LN_EPS) * g + b


def _proj_kernel(*refs, rope):
    if rope:
        x_ref, w_ref, cos_ref, sin_ref, o_ref, xb_ref = refs
    else:
        x_ref, w_ref, o_ref, xb_ref = refs

    @pl.when(pl.program_id(1) == 0)
    def _():
        xb_ref[...] = x_ref[...].astype(BF16)

    acc = _mm(xb_ref[...], w_ref[...])
    if rope:
        tn = acc.shape[1]
        lane = lax.broadcasted_iota(I32, acc.shape, 1)
        lo = (lane % HEAD_DIM) < ROPE_HALF
        rot = jnp.where(lo, pltpu.roll(acc, tn - ROPE_HALF, 1), pltpu.roll(acc, ROPE_HALF, 1))
        acc = acc * cos_ref[...] + rot * sin_ref[...]
    o_ref[...] = acc.astype(o_ref.dtype)


def _proj(x, w, *, out_dtype, tm, tn, rope_tables=None, seq=None):
    T, K = x.shape
    N = w.shape[1]
    assert T % tm == 0 and N % tn == 0
    in_specs = [pl.BlockSpec((tm, K), lambda i, j: (i, 0)), pl.BlockSpec((K, tn), lambda i, j: (0, j))]
    args = [x, w]
    if rope_tables is not None:
        nseq = seq // tm
        in_specs += [pl.BlockSpec((tm, tn), lambda i, j: (i % nseq, 0))] * 2
        args += list(rope_tables)
    return pl.pallas_call(
        functools.partial(_proj_kernel, rope=rope_tables is not None),
        name="proj_rope" if rope_tables is not None else "proj",
        grid=(T // tm, N // tn),
        in_specs=in_specs,
        out_specs=pl.BlockSpec((tm, tn), lambda i, j: (i, j)),
        out_shape=jax.ShapeDtypeStruct((T, N), out_dtype),
        scratch_shapes=[pltpu.VMEM((tm, K), BF16)],
        compiler_params=_cparams("parallel", "arbitrary"),
    )(*args)


def _rope_tables(pos, width):
    inv = ROPE_THETA ** (-jnp.arange(ROPE_HALF, dtype=F32) / ROPE_HALF)
    ang = pos.astype(F32)[:, None] * inv[None, :]
    cos, sin = jnp.cos(ang), jnp.sin(ang)
    cos64 = jnp.concatenate([cos, cos], -1)
    sin64 = jnp.concatenate([-sin, sin], -1)
    rep = width // HEAD_DIM
    return jnp.tile(cos64, (1, rep)), jnp.tile(sin64, (1, rep))


def _outproj_ln_kernel(y_ref, w_ref, x_ref, g_ref, b_ref, o_ref):
    y = _mm(y_ref[...].astype(BF16), w_ref[...])
    o_ref[...] = _ln_rows(ALPHA * x_ref[...] + y, g_ref[...], b_ref[...])


def _outproj_ln(y, w, x, g, b, *, tm=512):
    T, D = x.shape
    K = y.shape[1]
    row = lambda i: (i, 0)
    fixed = lambda i: (0, 0)
    return pl.pallas_call(
        _outproj_ln_kernel,
        name="outproj_ln",
        grid=(T // tm,),
        in_specs=[pl.BlockSpec((tm, K), row), pl.BlockSpec((K, D), fixed), pl.BlockSpec((tm, D), row),
                  pl.BlockSpec((1, D), fixed), pl.BlockSpec((1, D), fixed)],
        out_specs=pl.BlockSpec((tm, D), row),
        out_shape=jax.ShapeDtypeStruct((T, D), F32),
        compiler_params=_cparams("parallel"),
    )(y, w, x, g.reshape(1, D), b.reshape(1, D))


def _log_sigmoid(z):
    return jnp.minimum(z, 0.0) - jnp.log(1.0 + jnp.exp(-jnp.abs(z)))


def _mlstm_kernel(qk_ref, v_ref, o_ref, gc_ref, gr_ref, convw_ref, bias_c_ref, bias_r_ref, ng_ref,
                  tri_ref, triT_ref, out_ref, C_ref, n_ref, m_ref, ext_ref):
    L = qk_ref.shape[0]
    H, dk, dv = MLSTM_HEADS, MLSTM_QK_DIM, MLSTM_V_DIM

    @pl.when(pl.program_id(1) == 0)
    def _():
        C_ref[...] = jnp.zeros_like(C_ref)
        n_ref[...] = jnp.zeros_like(n_ref)
        m_ref[...] = jnp.zeros_like(m_ref)
        ext_ref[0:8, :] = jnp.zeros((8, ext_ref.shape[1]), F32)

    cur = qk_ref[...]
    ext_ref[8:8 + L, :] = cur
    acc = convw_ref[3:4, :] * cur
    for j in range(MLSTM_CONV - 1):
        acc = acc + convw_ref[j:j + 1, :] * ext_ref[5 + j:5 + j + L, :]
    ext_ref[0:8, :] = cur[L - 8:L, :]
    qk = _silu(acc)

    gc = gc_ref[...] + bias_c_ref[...]
    gr = gr_ref[...] + bias_r_ref[...]
    tri01 = tri_ref[...].astype(BF16)
    b_col = sum(_mm(tri01, t) for t in _split3(_log_sigmoid(gc)))
    b_row = _mm01(_log_sigmoid(gr[H:2 * H, :]), triT_ref[...])
    ig_row = gr[0:H, :]
    tri = lax.broadcasted_iota(I32, (L, L), 0) >= lax.broadcasted_iota(I32, (L, L), 1)

    for h in range(H):
        q = (qk[:, h * dk:(h + 1) * dk] * dk ** -0.5).astype(BF16)
        k = qk[:, H * dk + h * dk:H * dk + (h + 1) * dk]
        kb = k.astype(BF16)
        v = v_ref[:, h * dv:(h + 1) * dv].astype(BF16)
        b_c = b_col[:, H + h:H + h + 1]
        ig_c = gc[:, h:h + 1]
        b_r = b_row[h:h + 1, :]
        m_prev = m_ref[h:h + 1, 0:1]
        C = C_ref[h]
        n = n_ref[h:h + 1, :]

        logD = jnp.where(tri, b_c - b_r + ig_row[h:h + 1, :], NEG_INF)
        inter = b_c + m_prev
        m_t = jnp.maximum(inter, jnp.max(logD, axis=-1, keepdims=True))
        s = _nt(q, kb) * jnp.exp(logD - m_t)
        w_inter = jnp.exp(inter - m_t)
        num = w_inter * _mm(q, C.astype(BF16)) + _mm(s.astype(BF16), v)
        den = w_inter * jnp.sum(q.astype(F32) * n, axis=-1, keepdims=True) + jnp.sum(s, axis=-1, keepdims=True)
        hv = num / jnp.maximum(jnp.abs(den), jnp.exp(-m_t))

        mu = jnp.mean(hv, axis=-1, keepdims=True)
        d = hv - mu
        hn = d * lax.rsqrt(jnp.mean(d * d, axis=-1, keepdims=True) + LN_EPS)
        og = _sigmoid(o_ref[:, h * dv:(h + 1) * dv])
        out_ref[:, h * dv:(h + 1) * dv] = (hn * ng_ref[:, h * dv:(h + 1) * dv] * og).astype(out_ref.dtype)

        bL = b_c[L - 1:L, :]
        logw = bL - b_c + ig_c
        m_new = jnp.maximum(bL + m_prev, jnp.max(logw, axis=0, keepdims=True))
        decay = jnp.exp(bL + m_prev - m_new)
        kw = k * jnp.exp(logw - m_new)
        C_ref[h] = decay * C + _tn(kw.astype(BF16), v)
        n_ref[h:h + 1, :] = decay * n + jnp.sum(kw, axis=0, keepdims=True)
        m_ref[h:h + 1, :] = jnp.broadcast_to(m_new, (1, m_ref.shape[1]))


def _mlstm_mixer(x, B, S, w_in, conv_w, ig_bias, fg_bias, norm_g):
    T = B * S
    H, L = MLSTM_HEADS, MLSTM_L
    wb = w_in.astype(BF16)
    main = _proj(x, wb[:, :3 * D_MODEL], out_dtype=F32, tm=512, tn=1024)
    wg = jnp.pad(wb[:, 3 * D_MODEL:], ((0, 0), (0, 128 - 2 * H)))
    gates = _proj(x, wg, out_dtype=F32, tm=512, tn=128)
    gates_r = gates[:, :2 * H].reshape(B, S, 2 * H).transpose(0, 2, 1)
    bias16 = jnp.concatenate([ig_bias, fg_bias]).astype(F32)
    bias_c = jnp.pad(bias16, (0, 128 - 2 * H)).reshape(1, 128)
    bias_r = bias16.reshape(2 * H, 1)
    tri = jnp.tril(jnp.ones((L, L), F32))
    nc = S // L
    rowblk = lambda c: (lambda b, i: (b * nc + i, c))
    fixed = lambda b, i: (0, 0)
    return pl.pallas_call(
        _mlstm_kernel,
        name="mlstm",
        grid=(B, nc),
        in_specs=[pl.BlockSpec((L, D_MODEL), rowblk(0)), pl.BlockSpec((L, D_MODEL), rowblk(1)),
                  pl.BlockSpec((L, D_MODEL), rowblk(2)), pl.BlockSpec((L, 128), rowblk(0)),
                  pl.BlockSpec((None, 2 * H, L), lambda b, i: (b, 0, i)),
                  pl.BlockSpec((MLSTM_CONV, D_MODEL), fixed), pl.BlockSpec((1, 128), fixed),
                  pl.BlockSpec((2 * H, 1), fixed), pl.BlockSpec((1, D_MODEL), fixed),
                  pl.BlockSpec((L, L), fixed), pl.BlockSpec((L, L), fixed)],
        out_specs=pl.BlockSpec((L, D_MODEL), rowblk(0)),
        out_shape=jax.ShapeDtypeStruct((T, D_MODEL), BF16),
        scratch_shapes=[pltpu.VMEM((H, MLSTM_QK_DIM, MLSTM_V_DIM), F32), pltpu.VMEM((H, MLSTM_QK_DIM), F32),
                        pltpu.VMEM((H, 128), F32), pltpu.VMEM((L + 8, D_MODEL), F32)],
        compiler_params=_cparams("parallel", "arbitrary"),
    )(main, main, main, gates, gates_r, conv_w.astype(F32), bias_c, bias_r, norm_g.reshape(1, D_MODEL).astype(F32),
      tri, tri.T)


def _rot_cols(w):
    shp = w.shape
    w4 = w.reshape(shp[:-1] + (shp[-1] // HEAD_DIM, 2, ROPE_HALF))
    return jnp.flip(w4, axis=-2).reshape(shp)


def _nsa_compress_kernel(chk_ref, chv_ref, wk_lo_ref, wk_hi_ref, wv_lo_ref, wv_hi_ref, pk_ref, pv_ref, wk_ref, wv_ref,
                         cos_ref, sin_ref, kc_ref, vc_ref, sh_ref):
    NCH = chk_ref.shape[0]
    G = NSA_KV_HEADS
    sh_ref[NCH:NCH + 8, :] = jnp.zeros((8, sh_ref.shape[1]), F32)
    live = lax.broadcasted_iota(I32, (NCH, 1), 0) < NCH - 1

    def blocks(ch_ref, lo_ref, hi_ref, p_ref, w_ref):
        ch = ch_ref[...]
        n = lo_ref.shape[1]
        sh_ref[0:NCH, 0:n] = _mm(ch, hi_ref[...])
        const = _mm(p_ref[...].astype(BF16), w_ref[...])[0:1, :]
        return jnp.where(live, _mm(ch, lo_ref[...]) + sh_ref[1:NCH + 1, 0:n] + jnp.concatenate([const] * G, axis=1), 0.0)

    k2 = blocks(chk_ref, wk_lo_ref, wk_hi_ref, pk_ref, wk_ref)
    v2 = blocks(chv_ref, wv_lo_ref, wv_hi_ref, pv_ref, wv_ref)
    for g in range(G):
        raw = k2[:, 2 * g * HEAD_DIM:(2 * g + 1) * HEAD_DIM]
        rot = k2[:, (2 * g + 1) * HEAD_DIM:(2 * g + 2) * HEAD_DIM]
        kc_ref[g] = (raw * cos_ref[...] + rot * sin_ref[...]).astype(kc_ref.dtype)
        vc_ref[g] = v2[:, g * HEAD_DIM:(g + 1) * HEAD_DIM].astype(vc_ref.dtype)


def _nsa_cmp_select_kernel(q_ref, kc_ref, vc_ref, ovlT_ref, ocmp_ref, sel_ref):
    TQ = q_ref.shape[0]
    NCH = kc_ref.shape[0]
    NSB = ovlT_ref.shape[0]
    q0 = pl.program_id(2) * TQ
    t = q0 + lax.broadcasted_iota(I32, (TQ, 1), 0)
    cend = lax.broadcasted_iota(I32, (1, NCH), 1) * CMP_STRIDE + (CMP_BLK - 1)
    bias = jnp.where(cend <= t, 0.0, NEG_INF)
    kc = kc_ref[...]
    vc = vc_ref[...]
    psum = jnp.zeros((TQ, NCH), F32)
    outs = []
    for j in range(NSA_J):
        qj = q_ref[:, j * HEAD_DIM:(j + 1) * HEAD_DIM]
        s = _nt(qj, kc) * HEAD_DIM ** -0.5 + bias
        m = jnp.maximum(jnp.max(s, axis=-1, keepdims=True), NEG_FLOOR)
        e = jnp.exp(s - m)
        p = e / jnp.maximum(jnp.sum(e, axis=-1, keepdims=True), TINY)
        outs.append(_mm(p.astype(BF16), vc))
        psum = psum + p
    ocmp_ref[...] = jnp.concatenate(outs, axis=1).astype(ocmp_ref.dtype)

    ovl01 = ovlT_ref[...].astype(BF16)
    imp = sum(_nt(ovl01, t) for t in _split3(psum))
    nb = lax.broadcasted_iota(I32, (NSB, 1), 0)
    qblk = (q0 + lax.broadcasted_iota(I32, (1, TQ), 1)) // SEL_BLK
    forced = (nb == 0) | (nb == qblk) | (nb == qblk - 1)
    cur = jnp.where(forced, FORCE_SCORE, jnp.where(nb > qblk, NEG_INF, imp))
    sel = jnp.zeros((NSB, TQ), F32)
    for _ in range(min(N_SEL, NSB)):
        m = jnp.max(cur, axis=0, keepdims=True)
        idx = jnp.min(jnp.where(cur == m, nb, NSB), axis=0, keepdims=True)
        hit = nb == idx
        sel = jnp.where(hit, 1.0, sel)
        cur = jnp.where(hit, BELOW_NEG_INF, cur)
    sel_ref[...] = sel.astype(sel_ref.dtype)


def _nsa_main_kernel(q_ref, ksT_ref, vs_ref, kwT_ref, vw_ref, sel_ref, ocmp_ref, gate_ref, gexp_ref, o_ref):
    TQ = q_ref.shape[0]
    S = vs_ref.shape[0]
    TK = min(NSA_TK, S)
    J = NSA_J
    q0 = pl.program_id(2) * TQ
    q4 = jnp.concatenate([q_ref[:, j * HEAD_DIM:(j + 1) * HEAD_DIM] for j in range(J)], axis=0) * HEAD_DIM ** -0.5
    t = q0 + lax.broadcasted_iota(I32, (TQ, 1), 0)
    selT = sel_ref[...]

    def attend(carry, kT, v1, bias):
        m, acc = carry
        s = _mm(q4, kT) + jnp.concatenate([bias] * J, axis=0)
        m_new = jnp.maximum(m, jnp.max(s, axis=-1, keepdims=True))
        e = jnp.exp(s - m_new)
        return m_new, jnp.exp(m - m_new) * acc + _mm(e.astype(BF16), v1)

    def finish(acc):
        return acc[:, :HEAD_DIM] / jnp.maximum(acc[:, HEAD_DIM:], TINY)

    init = (jnp.full((J * TQ, 1), NEG_FLOOR, F32), jnp.zeros((J * TQ, 2 * HEAD_DIM), F32))

    def body(c, carry):
        k0 = pl.multiple_of(c * TK, TK)
        kpos = k0 + lax.broadcasted_iota(I32, (1, TK), 1)
        expand = jnp.where(kpos // SEL_BLK == lax.broadcasted_iota(I32, (selT.shape[0], 1), 0), 1.0, 0.0).astype(BF16)
        bias = jnp.where((_tn(selT, expand) > 0.5) & (kpos <= t), 0.0, NEG_INF)
        return attend(carry, ksT_ref[:, pl.ds(k0, TK)], vs_ref[pl.ds(k0, TK), :], bias)

    nchunks = (q0 + TQ - 1) // TK + 1
    o_slc = finish(lax.fori_loop(0, nchunks, body, init)[1])

    span = min(NSA_WINDOW + TQ, S)
    start = pl.multiple_of(jnp.clip(q0 - NSA_WINDOW, 0, S - span), TQ)
    dist = t - (start + lax.broadcasted_iota(I32, (1, span), 1))
    bias = jnp.where((dist >= 0) & (dist < NSA_WINDOW), 0.0, NEG_INF)
    o_win = finish(attend(init, kwT_ref[:, pl.ds(start, span)], vw_ref[pl.ds(start, span), :], bias)[1])

    unstack = lambda o: jnp.concatenate([o[j * TQ:(j + 1) * TQ, :] for j in range(J)], axis=1)
    g = _sigmoid(gate_ref[...])
    gx = [_mm01(g, gexp_ref[c]) for c in range(3)]
    out = gx[0] * ocmp_ref[...].astype(F32) + gx[1] * unstack(o_slc) + gx[2] * unstack(o_win)
    o_ref[...] = out.astype(o_ref.dtype)


def _nsa_mixer(x, B, S, w_in, cmp_pos_k, cmp_pos_v, cmp_wk, cmp_wv):
    T = B * S
    H, G, J, dh = NSA_HEADS, NSA_KV_HEADS, NSA_J, HEAD_DIM
    kv = G * dh
    wb = w_in.astype(BF16)
    col = lambda a, n: wb[:, a:a + n]
    o_q, o_kc, o_vc, o_ks, o_vs, o_kw, o_vw, o_g = np.cumsum([0, H * dh] + [kv] * 6).tolist()
    pos = jnp.arange(S)
    w_rope = jnp.concatenate([col(o_q, H * dh), col(o_ks, kv), col(o_kw, kv)], axis=1)
    roped = _proj(x, w_rope, out_dtype=BF16, tm=512, tn=512, rope_tables=_rope_tables(pos, 512), seq=S)
    w_plain = jnp.concatenate([col(o_vs, kv), col(o_vw, kv), jnp.pad(col(o_g, 3 * H), ((0, 0), (0, 128 - 3 * H)))], axis=1)
    plain = _proj(x, w_plain, out_dtype=F32, tm=512, tn=w_plain.shape[1])

    heads_T = lambda a: a.reshape(B, S, G, dh).transpose(0, 2, 3, 1)
    heads = lambda a: a.reshape(B, S, G, dh).transpose(0, 2, 1, 3)
    ksT = heads_T(roped[:, H * dh:H * dh + kv])
    kwT = heads_T(roped[:, H * dh + kv:])
    with_ones = lambda v: jnp.concatenate([v, jnp.ones_like(v)], axis=-1)
    vs = with_ones(heads(plain[:, :kv]).astype(BF16))
    vw = with_ones(heads(plain[:, kv:2 * kv]).astype(BF16))
    gates = plain[:, 2 * kv:]

    nch = S // CMP_STRIDE
    cw = CMP_STRIDE * kv
    chk = _proj(x, col(o_kc, kv), out_dtype=BF16, tm=512, tn=kv).reshape(B, nch, cw)
    chv = _proj(x, col(o_vc, kv), out_dtype=BF16, tm=512, tn=kv).reshape(B, nch, cw)
    wk = cmp_wk.reshape(CMP_BLK * dh, dh)
    wk2 = jnp.concatenate([wk, _rot_cols(wk)], axis=1).astype(BF16)
    wv2 = cmp_wv.reshape(CMP_BLK * dh, dh).astype(BF16)
    eye = jnp.eye(G, dtype=BF16)

    def per_head(w, part):
        n = w.shape[1]
        w3 = w.reshape(2, CMP_STRIDE, dh, n)[part]
        return jnp.einsum('lde,gh->lgdhe', w3, eye).reshape(cw, G * n)

    flat8 = lambda p_: jnp.pad(p_.reshape(1, CMP_BLK * dh), ((0, 7), (0, 0))).astype(F32)
    cend = jnp.arange(nch) * CMP_STRIDE + CMP_BLK - 1
    cos_c, sin_c = _rope_tables(cend, dh)
    fixed1 = lambda b: (0, 0)
    full1 = lambda a: pl.BlockSpec(a.shape, fixed1)
    consts = (per_head(wk2, 0), per_head(wk2, 1), per_head(wv2, 0), per_head(wv2, 1), flat8(cmp_pos_k), flat8(cmp_pos_v),
              wk2, wv2, cos_c, sin_c)
    kc, vc = pl.pallas_call(
        _nsa_compress_kernel,
        name="nsa_compress",
        grid=(B,),
        in_specs=[pl.BlockSpec((None, nch, cw), lambda b: (b, 0, 0))] * 2 + [full1(c) for c in consts],
        out_specs=[pl.BlockSpec((None, G, nch, dh), lambda b: (b, 0, 0, 0))] * 2,
        out_shape=[jax.ShapeDtypeStruct((B, G, nch, dh), BF16)] * 2,
        scratch_shapes=[pltpu.VMEM((nch + 8, 2 * kv), F32)],
        compiler_params=_cparams("parallel"),
    )(chk, chv, *consts)

    nsb = S // SEL_BLK
    c_idx, s_idx = np.arange(nch), np.arange(nsb)
    ovl = ((c_idx[:, None] * CMP_STRIDE + CMP_BLK - 1 >= s_idx[None, :] * SEL_BLK)
           & (c_idx[:, None] * CMP_STRIDE < (s_idx[None, :] + 1) * SEL_BLK)).astype(np.float32)
    bgi = lambda b, g, i: (b, g, 0, 0)
    TS = min(NSA_TQ_SELECT, S)
    ns = S // TS
    sblk = lambda b, g, i: (b * ns + i, g)
    ocmp, sel = pl.pallas_call(
        _nsa_cmp_select_kernel,
        name="nsa_cmp_select",
        grid=(B, G, ns),
        in_specs=[pl.BlockSpec((TS, J * dh), sblk), pl.BlockSpec((None, None, nch, dh), bgi),
                  pl.BlockSpec((None, None, nch, dh), bgi), pl.BlockSpec((nsb, nch), lambda b, g, i: (0, 0))],
        out_specs=[pl.BlockSpec((TS, J * dh), sblk), pl.BlockSpec((None, None, nsb, TS), lambda b, g, i: (b, g, 0, i))],
        out_shape=[jax.ShapeDtypeStruct((T, H * dh), BF16), jax.ShapeDtypeStruct((B, G, nsb, S), BF16)],
        compiler_params=_cparams("parallel", "parallel", "parallel"),
    )(roped, kc, vc, jnp.asarray(ovl.T))
    TQ = NSA_TQ
    nq = S // TQ
    qblk = lambda b, g, i: (b * nq + i, g)

    gexp = np.zeros((G, 3, 128, J * dh), np.float32)
    for g in range(G):
        for c in range(3):
            for j in range(J):
                gexp[g, c, (g * J + j) * 3 + c, j * dh:(j + 1) * dh] = 1.0
    return pl.pallas_call(
        _nsa_main_kernel,
        name="nsa_main",
        grid=(B, G, nq),
        in_specs=[pl.BlockSpec((TQ, J * dh), qblk)] + [pl.BlockSpec((None, None, dh, S), bgi),
                                                       pl.BlockSpec((None, None, S, 2 * dh), bgi)] * 2 + [
            pl.BlockSpec((None, None, nsb, TQ), lambda b, g, i: (b, g, 0, i)),
            pl.BlockSpec((TQ, J * dh), qblk), pl.BlockSpec((TQ, 128), lambda b, g, i: (b * nq + i, 0)),
            pl.BlockSpec((None, 3, 128, J * dh), lambda b, g, i: (g, 0, 0, 0))],
        out_specs=pl.BlockSpec((TQ, J * dh), qblk),
        out_shape=jax.ShapeDtypeStruct((T, H * dh), BF16),
        compiler_params=_cparams("parallel", "parallel", "arbitrary"),
    )(roped, ksT, vs, kwT, vw, sel, ocmp, gates, jnp.asarray(gexp))


def _dil_kernel(q_ref, kp_ref, kc_ref, vp_ref, vc_ref, o_ref, lse_ref):
    NQ = q_ref.shape[0]
    prev_from = jnp.where(pl.program_id(2) == 0, NQ, 0)
    qi = lax.broadcasted_iota(I32, (NQ, 1), 0)
    kj = lax.broadcasted_iota(I32, (1, 2 * NQ), 1)
    dist = NQ + qi - kj
    bias = jnp.where((dist >= 0) & (dist <= NQ) & (kj >= prev_from), 0.0, NEG_INF)
    lane = lax.broadcasted_iota(I32, (NQ, 128), 1)
    first = lane < HEAD_DIM
    lse_all = jnp.zeros((NQ, 128), F32)
    zero = jnp.zeros((), q_ref.dtype)
    for hp in range(DIL_HEADS // 2):
        sl = slice(hp * 128, (hp + 1) * 128)
        q2 = q_ref[:, sl]
        kb = jnp.concatenate([kp_ref[:, sl], kc_ref[:, sl]], axis=0)
        vb = jnp.concatenate([vp_ref[:, sl], vc_ref[:, sl]], axis=0)
        outs = []
        for sub in range(2):
            keep = first if sub == 0 else jnp.logical_not(first)
            s = _nt(jnp.where(keep, q2, zero), kb) * HEAD_DIM ** -0.5 + bias
            m = jnp.max(s, axis=-1, keepdims=True)
            e = jnp.exp(s - m)
            den = jnp.sum(e, axis=-1, keepdims=True)
            outs.append(_mm((e / den).astype(BF16), vb))
            lse_all = jnp.where(lane == 2 * hp + sub, m + jnp.log(den), lse_all)
        o_ref[:, sl] = jnp.where(first, outs[0], outs[1]).astype(o_ref.dtype)
    lse_ref[...] = lse_all


def _proj_res_kernel(*refs, rope, dils):
    n_out = len(dils)
    x_ref, w_ref = refs[:2]
    rest = refs[2:]
    if rope:
        cos_ref, sin_ref = rest[:2]
        rest = rest[2:]
    out_refs, xb_ref, scr_ref = rest[:n_out], rest[n_out], rest[n_out + 1]
    tm, tn = x_ref.shape[0], w_ref.shape[1]

    @pl.when(pl.program_id(1) == 0)
    def _():
        xb_ref[...] = x_ref[...].astype(BF16)

    acc = _mm(xb_ref[...], w_ref[...])
    if rope:
        lane = lax.broadcasted_iota(I32, acc.shape, 1)
        lo = (lane % HEAD_DIM) < ROPE_HALF
        rot = jnp.where(lo, pltpu.roll(acc, tn - ROPE_HALF, 1), pltpu.roll(acc, ROPE_HALF, 1))
        acc = acc * cos_ref[...] + rot * sin_ref[...]
    if any(d > 1 for d in dils):
        for c in range(tn // 128):
            scr_ref[c] = acc[:, c * 128:(c + 1) * 128]
    for o_ref, dil in zip(out_refs, dils):
        if dil == 1:
            o_ref[...] = acc.astype(o_ref.dtype)
            continue
        for r in range(dil):
            for c in range(tn // 128):
                o_ref[:, r * tn + c * 128:r * tn + (c + 1) * 128] = (
                    scr_ref[c, pl.ds(r, tm // dil, stride=dil), :].astype(o_ref.dtype))


def _proj_res(x, w, dils, *, seq, rope, tm=512, tn=D_MODEL):
    T, K = x.shape
    N = w.shape[1]
    nparts = N // tn
    in_specs = [pl.BlockSpec((tm, K), lambda i, j: (i, 0)), pl.BlockSpec((K, tn), lambda i, j: (0, j))]
    args = [x, w]
    if rope:
        nseq = seq // tm
        in_specs += [pl.BlockSpec((tm, tn), lambda i, j: (i % nseq, 0))] * 2
        args += list(_rope_tables(jnp.arange(seq), tn))
    return pl.pallas_call(
        functools.partial(_proj_res_kernel, rope=rope, dils=dils),
        name="proj_residue_rope" if rope else "proj_residue",
        grid=(T // tm, nparts),
        in_specs=in_specs,
        out_specs=[pl.BlockSpec((tm // d, d * tn), lambda i, j: (i, j)) for d in dils],
        out_shape=[jax.ShapeDtypeStruct((T // d, nparts * d * tn), BF16) for d in dils],
        scratch_shapes=[pltpu.VMEM((tm, K), BF16), pltpu.VMEM((tn // 128, tm, 128), F32)],
        compiler_params=_cparams("parallel", "arbitrary"),
    )(*args)


def _dil_group(qk, v, B, S, dil):
    NQ = DIL_STEPS
    U = S // dil
    nb = U // NQ
    W = D_MODEL
    qk_view = qk.reshape(B, U, 2 * dil * W)
    v_view = v.reshape(B, U, dil * W)
    cur = lambda part: (lambda b, r, n: (b, n, part * dil + r))
    prev = lambda part: (lambda b, r, n: (b, jnp.maximum(n - 1, 0), part * dil + r))
    blk = lambda f: pl.BlockSpec((None, NQ, W), f)
    o, lse = pl.pallas_call(
        _dil_kernel,
        name=f"dilated_attn_{dil}",
        grid=(B, dil, nb),
        in_specs=[blk(cur(0)), blk(prev(1)), blk(cur(1)), blk(prev(0)), blk(cur(0))],
        out_specs=[pl.BlockSpec((None, NQ, W), lambda b, r, n: (b, n, r)),
                   pl.BlockSpec((None, NQ, 128), lambda b, r, n: (b, n, r))],
        out_shape=[jax.ShapeDtypeStruct((B, U, dil * W), BF16), jax.ShapeDtypeStruct((B, U, dil * 128), F32)],
        compiler_params=_cparams("parallel", "parallel", "arbitrary"),
    )(qk_view, qk_view, qk_view, v_view, v_view)
    return o.reshape(B * U, dil * W), lse.reshape(B * U, dil * 128)


def _dil_outproj_ln_kernel(o0_ref, o1_ref, o2_ref, l0_ref, l1_ref, l2_ref, hexp_ref, w_ref, x_ref, g_ref, b_ref, out_ref,
                           oscr_ref, lscr_ref, *, dils):
    tm, W = x_ref.shape

    def natural(o_ref, l_ref, dil):
        if dil == 1:
            return o_ref[...].astype(F32), l_ref[...]
        n = tm // dil
        for r in range(dil):
            lscr_ref[0, pl.ds(r, n, stride=dil), :] = l_ref[:, r * 128:(r + 1) * 128]
            for c in range(W // 128):
                oscr_ref[c, pl.ds(r, n, stride=dil), :] = o_ref[:, r * W + c * 128:r * W + (c + 1) * 128].astype(F32)
        return jnp.concatenate([oscr_ref[c] for c in range(W // 128)], axis=1), lscr_ref[0]

    os_, ls = zip(*[natural(o, l, d) for o, l, d in zip((o0_ref, o1_ref, o2_ref), (l0_ref, l1_ref, l2_ref), dils)])
    m = jnp.maximum(jnp.maximum(ls[0], ls[1]), ls[2])
    es = [jnp.exp(l - m) for l in ls]
    tot = es[0] + es[1] + es[2]
    y = jnp.zeros((tm, W), F32)
    for e, o in zip(es, os_):
        y = y + _mm01(e / tot, hexp_ref[...]) * o
    z = ALPHA * x_ref[...] + _mm(y.astype(BF16), w_ref[...])
    out_ref[...] = _ln_rows(z, g_ref[...], b_ref[...])


def _dilated_layer(x, B, S, w_in, w_out, g, b, *, tm=256):
    T, D = x.shape
    dils = tuple(d for _, d in DIL_GROUPS)
    wb = w_in.astype(BF16)
    vals = _proj_res(x, wb[:, 2 * len(dils) * D:], dils, seq=S, rope=False)
    outs = []
    for gi, dil in enumerate(dils):
        qk, = _proj_res(x, wb[:, 2 * gi * D:(2 * gi + 2) * D], (dil,), seq=S, rope=True)
        outs.append(_dil_group(qk, vals[gi], B, S, dil))
    hexp = np.zeros((128, D), np.float32)
    for h in range(DIL_HEADS):
        hexp[h, h * HEAD_DIM:(h + 1) * HEAD_DIM] = 1.0
    row = lambda i: (i, 0)
    fixed = lambda i: (0, 0)
    return pl.pallas_call(
        functools.partial(_dil_outproj_ln_kernel, dils=dils),
        name="dilated_outproj_ln",
        grid=(T // tm,),
        in_specs=[pl.BlockSpec((tm // d, d * D), row) for d in dils] + [pl.BlockSpec((tm // d, d * 128), row) for d in dils]
        + [pl.BlockSpec((128, D), fixed), pl.BlockSpec((D, D), fixed), pl.BlockSpec((tm, D), row),
           pl.BlockSpec((1, D), fixed), pl.BlockSpec((1, D), fixed)],
        out_specs=pl.BlockSpec((tm, D), row),
        out_shape=jax.ShapeDtypeStruct((T, D), F32),
        scratch_shapes=[pltpu.VMEM((D // 128, tm, 128), F32), pltpu.VMEM((1, tm, 128), F32)],
        compiler_params=_cparams("parallel"),
    )(outs[0][0], outs[1][0], outs[2][0], outs[0][1], outs[1][1], outs[2][1], jnp.asarray(hexp),
      w_out.astype(BF16), x, g.reshape(1, D), b.reshape(1, D))


def _pool_ln_kernel(x_ref, halo_ref, w_ref, scale_ref, g_ref, b_ref, o_ref, ext_ref):
    TS = x_ref.shape[0]
    s = pl.program_id(1)
    x = x_ref[...]
    ext_ref[0:POOL_HALO, :] = jnp.where(s == 0, 0.0, halo_ref[...])
    ext_ref[POOL_HALO:POOL_HALO + TS, :] = x
    cnt = (s * TS + lax.broadcasted_iota(I32, (TS, 1), 0) + 1).astype(F32)
    ys = []
    for gi, w in enumerate(POOL_WINDOWS):
        sl = slice(gi * POOL_GROUP, (gi + 1) * POOL_GROUP)
        xg = x[:, sl]
        tot = xg
        for j in range(1, w):
            tot = tot + ext_ref[POOL_HALO - j:POOL_HALO - j + TS, sl]
        mean = tot / jnp.minimum(cnt, float(w))
        ys.append(_mm((mean - xg).astype(BF16), w_ref[gi]))
    y = jnp.concatenate(ys, axis=1) * scale_ref[...]
    o_ref[...] = _ln_rows(ALPHA * x + y, g_ref[...], b_ref[...])


def _pool_layer(x, B, S, w_grp, scale, g, b, *, ts=512):
    T, D = x.shape
    ns = S // ts
    hb = ts // POOL_HALO
    fixed = lambda bb, s: (0, 0)
    return pl.pallas_call(
        _pool_ln_kernel,
        name="pool_ln",
        grid=(B, ns),
        in_specs=[pl.BlockSpec((ts, D), lambda bb, s: (bb * ns + s, 0)),
                  pl.BlockSpec((POOL_HALO, D), lambda bb, s: (jnp.maximum((bb * ns + s) * hb - 1, 0), 0)),
                  pl.BlockSpec((len(POOL_WINDOWS), POOL_GROUP, POOL_GROUP), lambda bb, s: (0, 0, 0)),
                  pl.BlockSpec((1, D), fixed), pl.BlockSpec((1, D), fixed), pl.BlockSpec((1, D), fixed)],
        out_specs=pl.BlockSpec((ts, D), lambda bb, s: (bb * ns + s, 0)),
        out_shape=jax.ShapeDtypeStruct((T, D), F32),
        scratch_shapes=[pltpu.VMEM((ts + POOL_HALO, D), F32)],
        compiler_params=_cparams("parallel", "arbitrary"),
    )(x, x, w_grp.astype(BF16), scale.reshape(1, D), g.reshape(1, D), b.reshape(1, D))


U32 = jnp.uint32
HI16 = 0xFFFF0000
ROW_WORDS = D_MODEL // 2


def _pack_halves(v):
    h = v.shape[1] // 2
    bits = lambda a: lax.bitcast_convert_type(a.astype(BF16).astype(F32), U32)
    return (bits(v[:, :h]) >> 16) | (bits(v[:, h:]) & U32(HI16))


def _unpack_halves(w):
    lo = lax.bitcast_convert_type(w << 16, F32)
    hi = lax.bitcast_convert_type(w & U32(HI16), F32)
    return jnp.concatenate([lo, hi], axis=1)


def _router_kernel(x_ref, wT_ref, bias_ref, triu_ref, eidx_ref, gw_ref, rank_ref, cnt_ref, xp_ref, carry_ref):
    E = N_EXPERTS
    per = E // N_EXPERT_GROUPS
    TM = x_ref.shape[0]

    @pl.when(pl.program_id(0) == 0)
    def _():
        carry_ref[...] = jnp.zeros_like(carry_ref)

    x = x_ref[...]
    xp_ref[...] = _pack_halves(x)

    scores = _sigmoid(_nt(wT_ref[...], x, precision=HIGHEST))
    biased = scores + bias_ref[...]
    eio = lax.broadcasted_iota(I32, (E, TM), 0)

    gio = lax.broadcasted_iota(I32, (per, TM), 0)
    gscore = []
    for gidx in range(N_EXPERT_GROUPS):
        slab = biased[gidx * per:(gidx + 1) * per, :]
        m1 = jnp.max(slab, axis=0, keepdims=True)
        i1 = jnp.min(jnp.where(slab == m1, gio, per), axis=0, keepdims=True)
        m2 = jnp.max(jnp.where(gio == i1, BELOW_NEG_INF, slab), axis=0, keepdims=True)
        gscore.append(m1 + m2)
    slabs = []
    for gidx in range(N_EXPERT_GROUPS):
        beat = jnp.zeros((1, TM), F32)
        for o in range(N_EXPERT_GROUPS):
            if o == gidx:
                continue
            wins = (gscore[o] >= gscore[gidx]) if o < gidx else (gscore[o] > gscore[gidx])
            beat = beat + jnp.where(wins, 1.0, 0.0)
        keep = beat < float(TOPK_GROUPS)
        slabs.append(jnp.where(keep, biased[gidx * per:(gidx + 1) * per, :], NEG_INF))
    cur = jnp.concatenate(slabs, axis=0)

    picked = jnp.zeros((E, TM), F32)
    idxs, vals = [], []
    for _ in range(TOP_K):
        m = jnp.max(cur, axis=0, keepdims=True)
        idx = jnp.min(jnp.where(cur == m, eio, E), axis=0, keepdims=True)
        hit = eio == idx
        picked = jnp.where(hit, 1.0, picked)
        cur = jnp.where(hit, BELOW_NEG_INF, cur)
        idxs.append(idx)
        vals.append(jnp.sum(jnp.where(hit, scores, 0.0), axis=0, keepdims=True))
    total = vals[0]
    for v in vals[1:]:
        total = total + v

    pos = _mm(picked.astype(BF16), triu_ref[...]) + carry_ref[...]
    for k in range(TOP_K):
        eidx_ref[k:k + 1, :] = idxs[k]
        gw_ref[k:k + 1, :] = vals[k] / total * ROUTED_SCALE
        rank_ref[k:k + 1, :] = jnp.sum(jnp.where(eio == idxs[k], pos, 0.0), axis=0, keepdims=True).astype(I32)
    carry_ref[...] = carry_ref[...] + jnp.sum(picked, axis=1, keepdims=True)
    cnt_ref[...] = carry_ref[...].astype(I32)


def _router(x, router_w, router_bias, *, tm=512):
    T, D = x.shape
    E = N_EXPERTS
    triu = jnp.triu(jnp.ones((tm, tm), F32), k=1).astype(BF16)
    col = lambda i: (0, i)
    fixed = lambda i: (0, 0)
    return pl.pallas_call(
        _router_kernel,
        name="moe_router",
        grid=(T // tm,),
        in_specs=[pl.BlockSpec((tm, D), lambda i: (i, 0)), pl.BlockSpec((E, D), fixed),
                  pl.BlockSpec((E, 1), fixed), pl.BlockSpec((tm, tm), fixed)],
        out_specs=[pl.BlockSpec((TOP_K, tm), col), pl.BlockSpec((TOP_K, tm), col), pl.BlockSpec((TOP_K, tm), col),
                   pl.BlockSpec((E, 1), fixed), pl.BlockSpec((tm, ROW_WORDS), lambda i: (i, 0))],
        out_shape=[jax.ShapeDtypeStruct((TOP_K, T), I32), jax.ShapeDtypeStruct((TOP_K, T), F32),
                   jax.ShapeDtypeStruct((TOP_K, T), I32), jax.ShapeDtypeStruct((E, 1), I32),
                   jax.ShapeDtypeStruct((T, ROW_WORDS), U32)],
        scratch_shapes=[pltpu.VMEM((E, 1), F32)],
        compiler_params=_cparams("arbitrary"),
    )(x, router_w.T.astype(F32), router_bias.reshape(E, 1).astype(F32), triu)


def _tile_indices(dest_hbm, idx_ref, sem_idx, n_idx):
    i = pl.program_id(0)

    def idx_copy(step):
        slot = step % 2
        return pltpu.make_async_copy(dest_hbm.at[step], idx_ref.at[pl.ds(slot * n_idx, n_idx)], sem_idx.at[slot])

    @pl.when(i == 0)
    def _():
        idx_copy(0).start()

    idx_copy(i).wait()

    @pl.when(i + 1 < pl.num_programs(0))
    def _():
        idx_copy(i + 1).start()

    return (i % 2) * n_idx


def _dispatch_kernel(cnt_ref, pstart_ref, dest_hbm, x_ref, rows_ref, idx_ref, zero_ref, sem_idx, sem_rows, sem_zero):
    TD = x_ref.shape[0]

    @pl.when(pl.program_id(0) == 0)
    def _():
        zero_ref[...] = jnp.zeros_like(zero_ref)

        def per_expert(e, c):
            n = cnt_ref[e]
            base = pstart_ref[e]
            pad_to = (n + EXP_BLK - 1) // EXP_BLK * EXP_BLK
            fill = lambda r: pltpu.make_async_copy(zero_ref.at[pl.ds(0, 1)], rows_ref.at[base + r], sem_zero)

            def start(r, c2):
                fill(r).start()
                return c2

            def wait(r, c2):
                fill(r).wait()
                return c2

            lax.fori_loop(n, pad_to, start, 0)
            lax.fori_loop(n, pad_to, wait, 0)
            return c

        lax.fori_loop(0, N_EXPERTS, per_expert, 0)

    base = _tile_indices(dest_hbm, idx_ref, sem_idx, TOP_K * TD)
    scatter = lambda t, k: pltpu.make_async_copy(x_ref.at[pl.ds(t, 1)], rows_ref.at[idx_ref[base + k * TD + t]], sem_rows)

    def issue(t, c):
        for k in range(TOP_K):
            scatter(t, k).start(priority=k % 2)
        return c

    def drain(t, c):
        for k in range(TOP_K):
            scatter(t, k).wait()
        return c

    lax.fori_loop(0, TD, issue, 0, unroll=8)
    lax.fori_loop(0, TD, drain, 0, unroll=8)


def _dispatch(xp, dest_tiles, counts, pad_start, n_rows, *, td):
    T = xp.shape[0]
    return pl.pallas_call(
        _dispatch_kernel,
        name="moe_dispatch",
        grid_spec=pltpu.PrefetchScalarGridSpec(
            num_scalar_prefetch=2,
            grid=(T // td,),
            in_specs=[pl.BlockSpec(memory_space=pl.ANY), pl.BlockSpec((td, ROW_WORDS), lambda i, c, p: (i, 0))],
            out_specs=pl.BlockSpec(memory_space=pl.ANY),
            scratch_shapes=[pltpu.SMEM((2 * TOP_K * td,), I32), pltpu.VMEM((8, ROW_WORDS), U32),
                            pltpu.SemaphoreType.DMA((2,)), pltpu.SemaphoreType.DMA(()), pltpu.SemaphoreType.DMA(())],
        ),
        out_shape=jax.ShapeDtypeStruct((n_rows, 1, ROW_WORDS), U32),
        compiler_params=_cparams("arbitrary"),
    )(counts, pad_start, dest_tiles, xp)


def _expert_kernel(be_ref, nu_ref, rows_hbm, wg_ref, wu_ref, wd_ref, out_hbm, xbuf, obuf, wgu_bf, wd_bf, sem_in, sem_out):
    i = pl.program_id(0)
    nu = nu_ref[0]

    @pl.when((i < nu) & ((i == 0) | (be_ref[i] != be_ref[jnp.maximum(i - 1, 0)])))
    def _():
        wgu_bf[:, :EXPERT_FF] = wg_ref[0].astype(BF16)
        wgu_bf[:, EXPERT_FF:] = wu_ref[0].astype(BF16)
        wd_bf[...] = wd_ref[0].astype(BF16)

    blk = lambda step: pl.ds(pl.multiple_of(step * EXP_BLK, EXP_BLK), EXP_BLK)
    in_copy = lambda step: pltpu.make_async_copy(rows_hbm.at[blk(step), 0], xbuf.at[step % 2], sem_in.at[step % 2])
    out_copy = lambda step: pltpu.make_async_copy(obuf.at[step % 2], out_hbm.at[blk(step), 0], sem_out.at[step % 2])

    @pl.when(i == 0)
    def _():
        in_copy(0).start()

    @pl.when(i < nu)
    def _():
        in_copy(i).wait()

        @pl.when(i + 1 < nu)
        def _():
            in_copy(i + 1).start()

        @pl.when(i >= 2)
        def _():
            out_copy(i - 2).wait()

        slot = i % 2
        gu = _mm(_unpack_halves(xbuf[slot]).astype(BF16), wgu_bf[...])
        h = _silu(gu[:, :EXPERT_FF]) * gu[:, EXPERT_FF:]
        obuf[slot] = _pack_halves(_mm(h.astype(BF16), wd_bf[...]))
        out_copy(i).start()

    @pl.when(i == pl.num_programs(0) - 1)
    def _():
        @pl.when(nu >= 2)
        def _():
            out_copy(nu - 2).wait()

        out_copy(nu - 1).wait()


def _experts(rows, blk_expert, n_used, w_gate, w_up, w_down, layer):
    R = rows.shape[0]
    D = w_down.shape[-1]
    n_blk = R // EXP_BLK
    expert = lambda i, be, nu: (layer, be[jnp.minimum(i, nu[0] - 1)], 0, 0)
    return pl.pallas_call(
        _expert_kernel,
        name="moe_experts",
        grid_spec=pltpu.PrefetchScalarGridSpec(
            num_scalar_prefetch=2,
            grid=(n_blk,),
            in_specs=[pl.BlockSpec(memory_space=pl.ANY), pl.BlockSpec((None, 1, D, EXPERT_FF), expert),
                      pl.BlockSpec((None, 1, D, EXPERT_FF), expert), pl.BlockSpec((None, 1, EXPERT_FF, D), expert)],
            out_specs=pl.BlockSpec(memory_space=pl.ANY),
            scratch_shapes=[pltpu.VMEM((2, EXP_BLK, ROW_WORDS), U32), pltpu.VMEM((2, EXP_BLK, ROW_WORDS), U32),
                            pltpu.VMEM((D, 2 * EXPERT_FF), BF16), pltpu.VMEM((EXPERT_FF, D), BF16),
                            pltpu.SemaphoreType.DMA((2,)), pltpu.SemaphoreType.DMA((2,))],
        ),
        out_shape=jax.ShapeDtypeStruct((R, 1, ROW_WORDS), U32),
        compiler_params=_cparams("arbitrary"),
    )(blk_expert, n_used, rows, w_gate, w_up, w_down)


def _combine_kernel(dest_hbm, gw_ref, x_ref, rows_ref, wsgu_ref, wsd_ref, g_ref, b_ref, p_ref, wp_ref, wpg_ref,
                    o_ref, idx_ref, buf_ref, sem_idx, sem_rows):
    TM = x_ref.shape[0]
    n_idx = TOP_K * TM
    i = pl.program_id(0)
    n = pl.num_programs(0)

    def idx_copy(step):
        slot = step % 3
        return pltpu.make_async_copy(dest_hbm.at[step], idx_ref.at[pl.ds(slot * n_idx, n_idx)], sem_idx.at[slot])

    def row_loop(step, slot, start):
        base = (step % 3) * n_idx

        def body(t, c):
            for k in range(TOP_K):
                cp = pltpu.make_async_copy(rows_ref.at[idx_ref[base + k * TM + t]], buf_ref.at[slot, k, pl.ds(t, 1)],
                                           sem_rows.at[slot])
                if start:
                    cp.start(priority=k % 2)
                else:
                    cp.wait()
            return c

        lax.fori_loop(0, TM, body, 0, unroll=8)

    def for_parity(step, fn):
        for slot in range(2):
            pl.when(step % 2 == slot)(functools.partial(fn, slot))

    @pl.when(i == 0)
    def _():
        idx_copy(0).start()
        idx_copy(0).wait()
        row_loop(0, 0, True)

        @pl.when(n > 1)
        def _():
            idx_copy(1).start()

    @pl.when(i + 1 < n)
    def _():
        idx_copy(i + 1).wait()

        @pl.when(i + 2 < n)
        def _():
            idx_copy(i + 2).start()

        for_parity(i + 1, lambda slot: row_loop(i + 1, slot, True))

    x = x_ref[...]
    gu = _mm(x.astype(BF16), wsgu_ref[...])
    ff = gu.shape[1] // 2
    f = _mm((_silu(gu[:, :ff]) * gu[:, ff:]).astype(BF16), wsd_ref[...])
    for_parity(i, lambda slot: row_loop(i, slot, False))
    gw = gw_ref[...]
    cur = i % 2
    for k in range(TOP_K):
        f = f + gw[:, k:k + 1] * _unpack_halves(buf_ref[cur, k])
    x2 = _ln_rows(ALPHA * x + f, g_ref[...], b_ref[...])
    gate = _sigmoid(_mm(x2.astype(BF16), wpg_ref[...]))
    o_ref[...] = x2 + gate * _mm(p_ref[...].astype(BF16), wp_ref[...])


def _combine(dest_tiles, gw, x, rows_out, wsgu, wsd, g, b, p, wp, wpg, *, tm):
    T, D = x.shape
    row = lambda i: (i, 0)
    fixed = lambda i: (0, 0)
    full = lambda a: pl.BlockSpec(a.shape, fixed)
    return pl.pallas_call(
        _combine_kernel,
        name="moe_combine",
        grid=(T // tm,),
        in_specs=[pl.BlockSpec(memory_space=pl.ANY), pl.BlockSpec((tm, TOP_K), row), pl.BlockSpec((tm, D), row),
                  pl.BlockSpec(memory_space=pl.ANY), full(wsgu), full(wsd), pl.BlockSpec((1, D), fixed),
                  pl.BlockSpec((1, D), fixed), pl.BlockSpec((tm, PLE_DIM), row), full(wp), full(wpg)],
        out_specs=pl.BlockSpec((tm, D), row),
        out_shape=jax.ShapeDtypeStruct((T, D), F32),
        scratch_shapes=[pltpu.SMEM((3 * TOP_K * tm,), I32), pltpu.VMEM((2, TOP_K, tm, ROW_WORDS), U32),
                        pltpu.SemaphoreType.DMA((3,)), pltpu.SemaphoreType.DMA((2,))],
        compiler_params=_cparams("arbitrary"),
    )(dest_tiles, gw, x, rows_out, wsgu, wsd, g.reshape(1, D), b.reshape(1, D), p, wp, wpg)


MOE_TILE = 256


def _moe_ple_layer(x, p, layer, router_w, router_bias, w_gate, w_up, w_down, ws_gate, ws_up, ws_down, g, b, ple_w, ple_gate_w):
    T, D = x.shape
    eidx, gw, rank, counts, xp = _router(x, router_w, router_bias)
    counts = counts.reshape(N_EXPERTS)
    padded = (counts + EXP_BLK - 1) // EXP_BLK * EXP_BLK
    pad_end = jnp.cumsum(padded)
    pad_start = pad_end - padded
    n_blk = T * TOP_K // EXP_BLK + N_EXPERTS
    e_iota = jnp.arange(N_EXPERTS, dtype=I32)
    dest = rank + jnp.sum(jnp.where(eidx[..., None] == e_iota, pad_start, 0), axis=-1)
    tm = MOE_TILE
    dest_tiles = dest.reshape(TOP_K, T // tm, tm).transpose(1, 0, 2).reshape(T // tm, TOP_K * tm)
    blk_first = jnp.arange(n_blk, dtype=I32)[:, None] * EXP_BLK
    blk_expert = jnp.minimum(jnp.sum((pad_end[None, :] <= blk_first).astype(I32), axis=-1), N_EXPERTS - 1)
    n_used = (pad_end[-1:] // EXP_BLK).astype(I32)

    rows = _dispatch(xp, dest_tiles, counts.astype(I32), pad_start.astype(I32), n_blk * EXP_BLK, td=tm)
    rows_out = _experts(rows, blk_expert, n_used, w_gate, w_up, w_down, layer)
    wsgu = jnp.concatenate([ws_gate, ws_up], axis=-1).astype(BF16)
    return _combine(dest_tiles, gw.T, x, rows_out, wsgu, ws_down.astype(BF16), g, b, p,
                    ple_w.astype(BF16), ple_gate_w.astype(BF16), tm=tm)


def kernel(x, p, ln_g, ln_b, mlstm_w_in, mlstm_conv, mlstm_ig_bias, mlstm_fg_bias, mlstm_norm_g, mlstm_w_out, nsa_w_in, nsa_cmp_pos_k, nsa_cmp_pos_v, nsa_cmp_wk, nsa_cmp_wv, nsa_w_out, dil_w_in, dil_w_out, pool_w, pool_scale, router_w, router_bias, exp_w_gate, exp_w_up, exp_w_down, sh_w_gate, sh_w_up, sh_w_down, ple_w, ple_gate_w):
    B, S, D = x.shape
    T = B * S
    xf = x.reshape(T, D)
    pf = p.reshape(DEPTH, T, PLE_DIM)
    for i in range(DEPTH):
        kind, j = i % 4, i // 4
        g1, b1 = ln_g[i, 0], ln_b[i, 0]
        if kind == 0:
            y = _mlstm_mixer(xf, B, S, mlstm_w_in[j], mlstm_conv[j], mlstm_ig_bias[j], mlstm_fg_bias[j], mlstm_norm_g[j])
            xf = _outproj_ln(y, mlstm_w_out[j].astype(BF16), xf, g1, b1)
        elif kind == 1:
            y = _nsa_mixer(xf, B, S, nsa_w_in[j], nsa_cmp_pos_k[j], nsa_cmp_pos_v[j], nsa_cmp_wk[j], nsa_cmp_wv[j])
            xf = _outproj_ln(y, nsa_w_out[j].astype(BF16), xf, g1, b1)
        elif kind == 2:
            xf = _dilated_layer(xf, B, S, dil_w_in[j], dil_w_out[j], g1, b1)
        else:
            xf = _pool_layer(xf, B, S, pool_w[j], pool_scale[j], g1, b1)
        xf = _moe_ple_layer(xf, pf[i], i, router_w[i], router_bias[i], exp_w_gate, exp_w_up, exp_w_down,
                            sh_w_gate[i], sh_w_up[i], sh_w_down[i], ln_g[i, 1], ln_b[i, 1], ple_w[i], ple_gate_w[i])
    return xf.reshape(B, S, D)
```

```python
import functools

import numpy as np
import jax
import jax.numpy as jnp
from jax import lax
from jax.experimental import pallas as pl
from jax.experimental.pallas import tpu as pltpu

F32 = jnp.float32
BF16 = jnp.bfloat16
I32 = jnp.int32
HIGHEST = lax.Precision.HIGHEST

D_MODEL = 1024
DEPTH = 4
ALPHA = (2.0 * DEPTH) ** 0.25
LN_EPS = 1e-5
NEG_INF = -1e30
TINY = 1e-30
BELOW_NEG_INF = -3e38
NEG_FLOOR = -1e29
ROPE_THETA = 10000.0
HEAD_DIM = 64
ROPE_HALF = HEAD_DIM // 2

MLSTM_HEADS = 8
MLSTM_QK_DIM = 64
MLSTM_V_DIM = 128
MLSTM_CONV = 4
MLSTM_L = 256

NSA_HEADS = 16
NSA_KV_HEADS = 4
NSA_J = NSA_HEADS // NSA_KV_HEADS
CMP_STRIDE = 16
CMP_BLK = 32
SEL_BLK = 64
N_SEL = 16
NSA_WINDOW = 512
FORCE_SCORE = 1e4
NSA_TQ = 256
NSA_TQ_SELECT = 512
NSA_TK = 512

DIL_HEADS = 16
DIL_GROUPS = ((128, 1), (512, 4), (2048, 16))
DIL_STEPS = 128

POOL_WINDOWS = (2, 4, 8, 16)
POOL_GROUP = 256
POOL_HALO = 16

N_EXPERTS = 64
TOP_K = 8
N_EXPERT_GROUPS = 8
TOPK_GROUPS = 4
EXPERT_FF = 256
ROUTED_SCALE = 2.5
EXP_BLK = 512
PLE_DIM = 256

VMEM_LIMIT = 48 * 1024 * 1024


def _cparams(*sem):
    return pltpu.CompilerParams(dimension_semantics=sem, vmem_limit_bytes=VMEM_LIMIT)


def _nt(a, b, **kw):
    return lax.dot_general(a, b, (((1,), (1,)), ((), ())), preferred_element_type=F32, **kw)


def _tn(a, b, **kw):
    return lax.dot_general(a, b, (((0,), (0,)), ((), ())), preferred_element_type=F32, **kw)


def _mm(a, b, **kw):
    return jnp.dot(a, b, preferred_element_type=F32, **kw)


def _split3(x):
    a = x.astype(BF16)
    r = x - a.astype(F32)
    b = r.astype(BF16)
    return a, b, (r - b.astype(F32)).astype(BF16)


def _mm01(x, e01):
    eb = e01.astype(BF16)
    a, b, c = _split3(x)
    return _mm(a, eb) + _mm(b, eb) + _mm(c, eb)


def _sigmoid(z):
    return 1.0 / (1.0 + jnp.exp(-z))


def _silu(z):
    return z * _sigmoid(z)


def _ln_rows(z, g, b):
    mu = jnp.mean(z, axis=-1, keepdims=True)
    d = z - mu
    var = jnp.mean(d * d, axis=-1, keepdims=True)
    return d * lax.rsqrt(var + LN_EPS) * g + b


def _proj_kernel(*refs, rope):
    if rope:
        x_ref, w_ref, cos_ref, sin_ref, o_ref, xb_ref = refs
    else:
        x_ref, w_ref, o_ref, xb_ref = refs

    @pl.when(pl.program_id(1) == 0)
    def _():
        xb_ref[...] = x_ref[...].astype(BF16)

    acc = _mm(xb_ref[...], w_ref[...])
    if rope:
        tn = acc.shape[1]
        lane = lax.broadcasted_iota(I32, acc.shape, 1)
        lo = (lane % HEAD_DIM) < ROPE_HALF
        rot = jnp.where(lo, pltpu.roll(acc, tn - ROPE_HALF, 1), pltpu.roll(acc, ROPE_HALF, 1))
        acc = acc * cos_ref[...] + rot * sin_ref[...]
    o_ref[...] = acc.astype(o_ref.dtype)


def _proj(x, w, *, out_dtype, tm, tn, rope_tables=None, seq=None):
    T, K = x.shape
    N = w.shape[1]
    assert T % tm == 0 and N % tn == 0
    in_specs = [pl.BlockSpec((tm, K), lambda i, j: (i, 0)), pl.BlockSpec((K, tn), lambda i, j: (0, j))]
    args = [x, w]
    if rope_tables is not None:
        nseq = seq // tm
        in_specs += [pl.BlockSpec((tm, tn), lambda i, j: (i % nseq, 0))] * 2
        args += list(rope_tables)
    return pl.pallas_call(
        functools.partial(_proj_kernel, rope=rope_tables is not None),
        name="proj_rope" if rope_tables is not None else "proj",
        grid=(T // tm, N // tn),
        in_specs=in_specs,
        out_specs=pl.BlockSpec((tm, tn), lambda i, j: (i, j)),
        out_shape=jax.ShapeDtypeStruct((T, N), out_dtype),
        scratch_shapes=[pltpu.VMEM((tm, K), BF16)],
        compiler_params=_cparams("parallel", "arbitrary"),
    )(*args)


def _rope_tables(pos, width):
    inv = ROPE_THETA ** (-jnp.arange(ROPE_HALF, dtype=F32) / ROPE_HALF)
    ang = pos.astype(F32)[:, None] * inv[None, :]
    cos, sin = jnp.cos(ang), jnp.sin(ang)
    cos64 = jnp.concatenate([cos, cos], -1)
    sin64 = jnp.concatenate([-sin, sin], -1)
    rep = width // HEAD_DIM
    return jnp.tile(cos64, (1, rep)), jnp.tile(sin64, (1, rep))


def _outproj_ln_kernel(y_ref, w_ref, x_ref, g_ref, b_ref, o_ref):
    y = _mm(y_ref[...].astype(BF16), w_ref[...])
    o_ref[...] = _ln_rows(ALPHA * x_ref[...] + y, g_ref[...], b_ref[...])


def _outproj_ln(y, w, x, g, b, *, tm=512):
    T, D = x.shape
    K = y.shape[1]
    row = lambda i: (i, 0)
    fixed = lambda i: (0, 0)
    return pl.pallas_call(
        _outproj_ln_kernel,
        name="outproj_ln",
        grid=(T // tm,),
        in_specs=[pl.BlockSpec((tm, K), row), pl.BlockSpec((K, D), fixed), pl.BlockSpec((tm, D), row),
                  pl.BlockSpec((1, D), fixed), pl.BlockSpec((1, D), fixed)],
        out_specs=pl.BlockSpec((tm, D), row),
        out_shape=jax.ShapeDtypeStruct((T, D), F32),
        compiler_params=_cparams("parallel"),
    )(y, w, x, g.reshape(1, D), b.reshape(1, D))


def _log_sigmoid(z):
    return jnp.minimum(z, 0.0) - jnp.log(1.0 + jnp.exp(-jnp.abs(z)))


def _mlstm_kernel(qk_ref, v_ref, o_ref, gc_ref, gr_ref, convw_ref, bias_c_ref, bias_r_ref, ng_ref,
                  tri_ref, triT_ref, out_ref, C_ref, n_ref, m_ref, ext_ref):
    L = qk_ref.shape[0]
    H, dk, dv = MLSTM_HEADS, MLSTM_QK_DIM, MLSTM_V_DIM

    @pl.when(pl.program_id(1) == 0)
    def _():
        C_ref[...] = jnp.zeros_like(C_ref)
        n_ref[...] = jnp.zeros_like(n_ref)
        m_ref[...] = jnp.zeros_like(m_ref)
        ext_ref[0:8, :] = jnp.zeros((8, ext_ref.shape[1]), F32)

    cur = qk_ref[...]
    ext_ref[8:8 + L, :] = cur
    acc = convw_ref[3:4, :] * cur
    for j in range(MLSTM_CONV - 1):
        acc = acc + convw_ref[j:j + 1, :] * ext_ref[5 + j:5 + j + L, :]
    ext_ref[0:8, :] = cur[L - 8:L, :]
    qk = _silu(acc)

    gc = gc_ref[...] + bias_c_ref[...]
    gr = gr_ref[...] + bias_r_ref[...]
    tri01 = tri_ref[...].astype(BF16)
    b_col = sum(_mm(tri01, t) for t in _split3(_log_sigmoid(gc)))
    b_row = _mm01(_log_sigmoid(gr[H:2 * H, :]), triT_ref[...])
    ig_row = gr[0:H, :]
    tri = lax.broadcasted_iota(I32, (L, L), 0) >= lax.broadcasted_iota(I32, (L, L), 1)

    for h in range(H):
        q = (qk[:, h * dk:(h + 1) * dk] * dk ** -0.5).astype(BF16)
        k = qk[:, H * dk + h * dk:H * dk + (h + 1) * dk]
        kb = k.astype(BF16)
        v = v_ref[:, h * dv:(h + 1) * dv].astype(BF16)
        b_c = b_col[:, H + h:H + h + 1]
        ig_c = gc[:, h:h + 1]
        b_r = b_row[h:h + 1, :]
        m_prev = m_ref[h:h + 1, 0:1]
        C = C_ref[h]
        n = n_ref[h:h + 1, :]

        logD = jnp.where(tri, b_c - b_r + ig_row[h:h + 1, :], NEG_INF)
        inter = b_c + m_prev
        m_t = jnp.maximum(inter, jnp.max(logD, axis=-1, keepdims=True))
        s = _nt(q, kb) * jnp.exp(logD - m_t)
        w_inter = jnp.exp(inter - m_t)
        num = w_inter * _mm(q, C.astype(BF16)) + _mm(s.astype(BF16), v)
        den = w_inter * jnp.sum(q.astype(F32) * n, axis=-1, keepdims=True) + jnp.sum(s, axis=-1, keepdims=True)
        hv = num / jnp.maximum(jnp.abs(den), jnp.exp(-m_t))

        mu = jnp.mean(hv, axis=-1, keepdims=True)
        d = hv - mu
        hn = d * lax.rsqrt(jnp.mean(d * d, axis=-1, keepdims=True) + LN_EPS)
        og = _sigmoid(o_ref[:, h * dv:(h + 1) * dv])
        out_ref[:, h * dv:(h + 1) * dv] = (hn * ng_ref[:, h * dv:(h + 1) * dv] * og).astype(out_ref.dtype)

        bL = b_c[L - 1:L, :]
        logw = bL - b_c + ig_c
        m_new = jnp.maximum(bL + m_prev, jnp.max(logw, axis=0, keepdims=True))
        decay = jnp.exp(bL + m_prev - m_new)
        kw = k * jnp.exp(logw - m_new)
        C_ref[h] = decay * C + _tn(kw.astype(BF16), v)
        n_ref[h:h + 1, :] = decay * n + jnp.sum(kw, axis=0, keepdims=True)
        m_ref[h:h + 1, :] = jnp.broadcast_to(m_new, (1, m_ref.shape[1]))


def _mlstm_mixer(x, B, S, w_in, conv_w, ig_bias, fg_bias, norm_g):
    T = B * S
    H, L = MLSTM_HEADS, MLSTM_L
    wb = w_in.astype(BF16)
    main = _proj(x, wb[:, :3 * D_MODEL], out_dtype=F32, tm=512, tn=1024)
    wg = jnp.pad(wb[:, 3 * D_MODEL:], ((0, 0), (0, 128 - 2 * H)))
    gates = _proj(x, wg, out_dtype=F32, tm=512, tn=128)
    gates_r = gates[:, :2 * H].reshape(B, S, 2 * H).transpose(0, 2, 1)
    bias16 = jnp.concatenate([ig_bias, fg_bias]).astype(F32)
    bias_c = jnp.pad(bias16, (0, 128 - 2 * H)).reshape(1, 128)
    bias_r = bias16.reshape(2 * H, 1)
    tri = jnp.tril(jnp.ones((L, L), F32))
    nc = S // L
    rowblk = lambda c: (lambda b, i: (b * nc + i, c))
    fixed = lambda b, i: (0, 0)
    return pl.pallas_call(
        _mlstm_kernel,
        name="mlstm",
        grid=(B, nc),
        in_specs=[pl.BlockSpec((L, D_MODEL), rowblk(0)), pl.BlockSpec((L, D_MODEL), rowblk(1)),
                  pl.BlockSpec((L, D_MODEL), rowblk(2)), pl.BlockSpec((L, 128), rowblk(0)),
                  pl.BlockSpec((None, 2 * H, L), lambda b, i: (b, 0, i)),
                  pl.BlockSpec((MLSTM_CONV, D_MODEL), fixed), pl.BlockSpec((1, 128), fixed),
                  pl.BlockSpec((2 * H, 1), fixed), pl.BlockSpec((1, D_MODEL), fixed),
                  pl.BlockSpec((L, L), fixed), pl.BlockSpec((L, L), fixed)],
        out_specs=pl.BlockSpec((L, D_MODEL), rowblk(0)),
        out_shape=jax.ShapeDtypeStruct((T, D_MODEL), BF16),
        scratch_shapes=[pltpu.VMEM((H, MLSTM_QK_DIM, MLSTM_V_DIM), F32), pltpu.VMEM((H, MLSTM_QK_DIM), F32),
                        pltpu.VMEM((H, 128), F32), pltpu.VMEM((L + 8, D_MODEL), F32)],
        compiler_params=_cparams("parallel", "arbitrary"),
    )(main, main, main, gates, gates_r, conv_w.astype(F32), bias_c, bias_r, norm_g.reshape(1, D_MODEL).astype(F32),
      tri, tri.T)


def _rot_cols(w):
    shp = w.shape
    w4 = w.reshape(shp[:-1] + (shp[-1] // HEAD_DIM, 2, ROPE_HALF))
    return jnp.flip(w4, axis=-2).reshape(shp)


def _nsa_compress_kernel(chk_ref, chv_ref, wk_lo_ref, wk_hi_ref, wv_lo_ref, wv_hi_ref, pk_ref, pv_ref, wk_ref, wv_ref,
                         cos_ref, sin_ref, kc_ref, vc_ref, sh_ref):
    NCH = chk_ref.shape[0]
    G = NSA_KV_HEADS
    sh_ref[NCH:NCH + 8, :] = jnp.zeros((8, sh_ref.shape[1]), F32)
    live = lax.broadcasted_iota(I32, (NCH, 1), 0) < NCH - 1

    def blocks(ch_ref, lo_ref, hi_ref, p_ref, w_ref):
        ch = ch_ref[...]
        n = lo_ref.shape[1]
        sh_ref[0:NCH, 0:n] = _mm(ch, hi_ref[...])
        const = _mm(p_ref[...].astype(BF16), w_ref[...])[0:1, :]
        return jnp.where(live, _mm(ch, lo_ref[...]) + sh_ref[1:NCH + 1, 0:n] + jnp.concatenate([const] * G, axis=1), 0.0)

    k2 = blocks(chk_ref, wk_lo_ref, wk_hi_ref, pk_ref, wk_ref)
    v2 = blocks(chv_ref, wv_lo_ref, wv_hi_ref, pv_ref, wv_ref)
    for g in range(G):
        raw = k2[:, 2 * g * HEAD_DIM:(2 * g + 1) * HEAD_DIM]
        rot = k2[:, (2 * g + 1) * HEAD_DIM:(2 * g + 2) * HEAD_DIM]
        kc_ref[g] = (raw * cos_ref[...] + rot * sin_ref[...]).astype(kc_ref.dtype)
        vc_ref[g] = v2[:, g * HEAD_DIM:(g + 1) * HEAD_DIM].astype(vc_ref.dtype)


def _nsa_cmp_select_kernel(q_ref, kc_ref, vc_ref, ovlT_ref, ocmp_ref, sel_ref):
    TQ = q_ref.shape[0]
    NCH = kc_ref.shape[0]
    NSB = ovlT_ref.shape[0]
    q0 = pl.program_id(2) * TQ
    t = q0 + lax.broadcasted_iota(I32, (TQ, 1), 0)
    cend = lax.broadcasted_iota(I32, (1, NCH), 1) * CMP_STRIDE + (CMP_BLK - 1)
    bias = jnp.where(cend <= t, 0.0, NEG_INF)
    kc = kc_ref[...]
    vc = vc_ref[...]
    psum = jnp.zeros((TQ, NCH), F32)
    outs = []
    for j in range(NSA_J):
        qj = q_ref[:, j * HEAD_DIM:(j + 1) * HEAD_DIM]
        s = _nt(qj, kc) * HEAD_DIM ** -0.5 + bias
        m = jnp.maximum(jnp.max(s, axis=-1, keepdims=True), NEG_FLOOR)
        e = jnp.exp(s - m)
        p = e / jnp.maximum(jnp.sum(e, axis=-1, keepdims=True), TINY)
        outs.append(_mm(p.astype(BF16), vc))
        psum = psum + p
    ocmp_ref[...] = jnp.concatenate(outs, axis=1).astype(ocmp_ref.dtype)

    ovl01 = ovlT_ref[...].astype(BF16)
    imp = sum(_nt(ovl01, t) for t in _split3(psum))
    nb = lax.broadcasted_iota(I32, (NSB, 1), 0)
    qblk = (q0 + lax.broadcasted_iota(I32, (1, TQ), 1)) // SEL_BLK
    forced = (nb == 0) | (nb == qblk) | (nb == qblk - 1)
    cur = jnp.where(forced, FORCE_SCORE, jnp.where(nb > qblk, NEG_INF, imp))
    sel = jnp.zeros((NSB, TQ), F32)
    for _ in range(min(N_SEL, NSB)):
        m = jnp.max(cur, axis=0, keepdims=True)
        idx = jnp.min(jnp.where(cur == m, nb, NSB), axis=0, keepdims=True)
        hit = nb == idx
        sel = jnp.where(hit, 1.0, sel)
        cur = jnp.where(hit, BELOW_NEG_INF, cur)
    sel_ref[...] = sel.astype(sel_ref.dtype)


def _nsa_main_kernel(q_ref, ksT_ref, vs_ref, kwT_ref, vw_ref, sel_ref, ocmp_ref, gate_ref, gexp_ref, o_ref):
    TQ = q_ref.shape[0]
    S = vs_ref.shape[0]
    TK = min(NSA_TK, S)
    J = NSA_J
    q0 = pl.program_id(2) * TQ
    q4 = jnp.concatenate([q_ref[:, j * HEAD_DIM:(j + 1) * HEAD_DIM] for j in range(J)], axis=0) * HEAD_DIM ** -0.5
    t = q0 + lax.broadcasted_iota(I32, (TQ, 1), 0)
    selT = sel_ref[...]

    def attend(carry, kT, v1, bias):
        m, acc = carry
        s = _mm(q4, kT) + jnp.concatenate([bias] * J, axis=0)
        m_new = jnp.maximum(m, jnp.max(s, axis=-1, keepdims=True))
        e = jnp.exp(s - m_new)
        return m_new, jnp.exp(m - m_new) * acc + _mm(e.astype(BF16), v1)

    def finish(acc):
        return acc[:, :HEAD_DIM] / jnp.maximum(acc[:, HEAD_DIM:], TINY)

    init = (jnp.full((J * TQ, 1), NEG_FLOOR, F32), jnp.zeros((J * TQ, 2 * HEAD_DIM), F32))

    def body(c, carry):
        k0 = pl.multiple_of(c * TK, TK)
        kpos = k0 + lax.broadcasted_iota(I32, (1, TK), 1)
        expand = jnp.where(kpos // SEL_BLK == lax.broadcasted_iota(I32, (selT.shape[0], 1), 0), 1.0, 0.0).astype(BF16)
        bias = jnp.where((_tn(selT, expand) > 0.5) & (kpos <= t), 0.0, NEG_INF)
        return attend(carry, ksT_ref[:, pl.ds(k0, TK)], vs_ref[pl.ds(k0, TK), :], bias)

    nchunks = (q0 + TQ - 1) // TK + 1
    o_slc = finish(lax.fori_loop(0, nchunks, body, init)[1])

    span = min(NSA_WINDOW + TQ, S)
    start = pl.multiple_of(jnp.clip(q0 - NSA_WINDOW, 0, S - span), TQ)
    dist = t - (start + lax.broadcasted_iota(I32, (1, span), 1))
    bias = jnp.where((dist >= 0) & (dist < NSA_WINDOW), 0.0, NEG_INF)
    o_win = finish(attend(init, kwT_ref[:, pl.ds(start, span)], vw_ref[pl.ds(start, span), :], bias)[1])

    unstack = lambda o: jnp.concatenate([o[j * TQ:(j + 1) * TQ, :] for j in range(J)], axis=1)
    g = _sigmoid(gate_ref[...])
    gx = [_mm01(g, gexp_ref[c]) for c in range(3)]
    out = gx[0] * ocmp_ref[...].astype(F32) + gx[1] * unstack(o_slc) + gx[2] * unstack(o_win)
    o_ref[...] = out.astype(o_ref.dtype)


def _nsa_mixer(x, B, S, w_in, cmp_pos_k, cmp_pos_v, cmp_wk, cmp_wv):
    T = B * S
    H, G, J, dh = NSA_HEADS, NSA_KV_HEADS, NSA_J, HEAD_DIM
    kv = G * dh
    wb = w_in.astype(BF16)
    col = lambda a, n: wb[:, a:a + n]
    o_q, o_kc, o_vc, o_ks, o_vs, o_kw, o_vw, o_g = np.cumsum([0, H * dh] + [kv] * 6).tolist()
    pos = jnp.arange(S)
    w_rope = jnp.concatenate([col(o_q, H * dh), col(o_ks, kv), col(o_kw, kv)], axis=1)
    roped = _proj(x, w_rope, out_dtype=BF16, tm=512, tn=512, rope_tables=_rope_tables(pos, 512), seq=S)
    w_plain = jnp.concatenate([col(o_vs, kv), col(o_vw, kv), jnp.pad(col(o_g, 3 * H), ((0, 0), (0, 128 - 3 * H)))], axis=1)
    plain = _proj(x, w_plain, out_dtype=F32, tm=512, tn=w_plain.shape[1])

    heads_T = lambda a: a.reshape(B, S, G, dh).transpose(0, 2, 3, 1)
    heads = lambda a: a.reshape(B, S, G, dh).transpose(0, 2, 1, 3)
    ksT = heads_T(roped[:, H * dh:H * dh + kv])
    kwT = heads_T(roped[:, H * dh + kv:])
    with_ones = lambda v: jnp.concatenate([v, jnp.ones_like(v)], axis=-1)
    vs = with_ones(heads(plain[:, :kv]).astype(BF16))
    vw = with_ones(heads(plain[:, kv:2 * kv]).astype(BF16))
    gates = plain[:, 2 * kv:]

    nch = S // CMP_STRIDE
    cw = CMP_STRIDE * kv
    chk = _proj(x, col(o_kc, kv), out_dtype=BF16, tm=512, tn=kv).reshape(B, nch, cw)
    chv = _proj(x, col(o_vc, kv), out_dtype=BF16, tm=512, tn=kv).reshape(B, nch, cw)
    wk = cmp_wk.reshape(CMP_BLK * dh, dh)
    wk2 = jnp.concatenate([wk, _rot_cols(wk)], axis=1).astype(BF16)
    wv2 = cmp_wv.reshape(CMP_BLK * dh, dh).astype(BF16)
    eye = jnp.eye(G, dtype=BF16)

    def per_head(w, part):
        n = w.shape[1]
        w3 = w.reshape(2, CMP_STRIDE, dh, n)[part]
        return jnp.einsum('lde,gh->lgdhe', w3, eye).reshape(cw, G * n)

    flat8 = lambda p_: jnp.pad(p_.reshape(1, CMP_BLK * dh), ((0, 7), (0, 0))).astype(F32)
    cend = jnp.arange(nch) * CMP_STRIDE + CMP_BLK - 1
    cos_c, sin_c = _rope_tables(cend, dh)
    fixed1 = lambda b: (0, 0)
    full1 = lambda a: pl.BlockSpec(a.shape, fixed1)
    consts = (per_head(wk2, 0), per_head(wk2, 1), per_head(wv2, 0), per_head(wv2, 1), flat8(cmp_pos_k), flat8(cmp_pos_v),
              wk2, wv2, cos_c, sin_c)
    kc, vc = pl.pallas_call(
        _nsa_compress_kernel,
        name="nsa_compress",
        grid=(B,),
        in_specs=[pl.BlockSpec((None, nch, cw), lambda b: (b, 0, 0))] * 2 + [full1(c) for c in consts],
        out_specs=[pl.BlockSpec((None, G, nch, dh), lambda b: (b, 0, 0, 0))] * 2,
        out_shape=[jax.ShapeDtypeStruct((B, G, nch, dh), BF16)] * 2,
        scratch_shapes=[pltpu.VMEM((nch + 8, 2 * kv), F32)],
        compiler_params=_cparams("parallel"),
    )(chk, chv, *consts)

    nsb = S // SEL_BLK
    c_idx, s_idx = np.arange(nch), np.arange(nsb)
    ovl = ((c_idx[:, None] * CMP_STRIDE + CMP_BLK - 1 >= s_idx[None, :] * SEL_BLK)
           & (c_idx[:, None] * CMP_STRIDE < (s_idx[None, :] + 1) * SEL_BLK)).astype(np.float32)
    bgi = lambda b, g, i: (b, g, 0, 0)
    TS = min(NSA_TQ_SELECT, S)
    ns = S // TS
    sblk = lambda b, g, i: (b * ns + i, g)
    ocmp, sel = pl.pallas_call(
        _nsa_cmp_select_kernel,
        name="nsa_cmp_select",
        grid=(B, G, ns),
        in_specs=[pl.BlockSpec((TS, J * dh), sblk), pl.BlockSpec((None, None, nch, dh), bgi),
                  pl.BlockSpec((None, None, nch, dh), bgi), pl.BlockSpec((nsb, nch), lambda b, g, i: (0, 0))],
        out_specs=[pl.BlockSpec((TS, J * dh), sblk), pl.BlockSpec((None, None, nsb, TS), lambda b, g, i: (b, g, 0, i))],
        out_shape=[jax.ShapeDtypeStruct((T, H * dh), BF16), jax.ShapeDtypeStruct((B, G, nsb, S), BF16)],
        compiler_params=_cparams("parallel", "parallel", "parallel"),
    )(roped, kc, vc, jnp.asarray(ovl.T))
    TQ = NSA_TQ
    nq = S // TQ
    qblk = lambda b, g, i: (b * nq + i, g)

    gexp = np.zeros((G, 3, 128, J * dh), np.float32)
    for g in range(G):
        for c in range(3):
            for j in range(J):
                gexp[g, c, (g * J + j) * 3 + c, j * dh:(j + 1) * dh] = 1.0
    return pl.pallas_call(
        _nsa_main_kernel,
        name="nsa_main",
        grid=(B, G, nq),
        in_specs=[pl.BlockSpec((TQ, J * dh), qblk)] + [pl.BlockSpec((None, None, dh, S), bgi),
                                                       pl.BlockSpec((None, None, S, 2 * dh), bgi)] * 2 + [
            pl.BlockSpec((None, None, nsb, TQ), lambda b, g, i: (b, g, 0, i)),
            pl.BlockSpec((TQ, J * dh), qblk), pl.BlockSpec((TQ, 128), lambda b, g, i: (b * nq + i, 0)),
            pl.BlockSpec((None, 3, 128, J * dh), lambda b, g, i: (g, 0, 0, 0))],
        out_specs=pl.BlockSpec((TQ, J * dh), qblk),
        out_shape=jax.ShapeDtypeStruct((T, H * dh), BF16),
        compiler_params=_cparams("parallel", "parallel", "arbitrary"),
    )(roped, ksT, vs, kwT, vw, sel, ocmp, gates, jnp.asarray(gexp))


def _dil_kernel(q_ref, kp_ref, kc_ref, vp_ref, vc_ref, o_ref, lse_ref):
    NQ = q_ref.shape[0]
    prev_from = jnp.where(pl.program_id(2) == 0, NQ, 0)
    qi = lax.broadcasted_iota(I32, (NQ, 1), 0)
    kj = lax.broadcasted_iota(I32, (1, 2 * NQ), 1)
    dist = NQ + qi - kj
    bias = jnp.where((dist >= 0) & (dist <= NQ) & (kj >= prev_from), 0.0, NEG_INF)
    lane = lax.broadcasted_iota(I32, (NQ, 128), 1)
    first = lane < HEAD_DIM
    lse_all = jnp.zeros((NQ, 128), F32)
    zero = jnp.zeros((), q_ref.dtype)
    for hp in range(DIL_HEADS // 2):
        sl = slice(hp * 128, (hp + 1) * 128)
        q2 = q_ref[:, sl]
        kb = jnp.concatenate([kp_ref[:, sl], kc_ref[:, sl]], axis=0)
        vb = jnp.concatenate([vp_ref[:, sl], vc_ref[:, sl]], axis=0)
        outs = []
        for sub in range(2):
            keep = first if sub == 0 else jnp.logical_not(first)
            s = _nt(jnp.where(keep, q2, zero), kb) * HEAD_DIM ** -0.5 + bias
            m = jnp.max(s, axis=-1, keepdims=True)
            e = jnp.exp(s - m)
            den = jnp.sum(e, axis=-1, keepdims=True)
            outs.append(_mm((e / den).astype(BF16), vb))
            lse_all = jnp.where(lane == 2 * hp + sub, m + jnp.log(den), lse_all)
        o_ref[:, sl] = jnp.where(first, outs[0], outs[1]).astype(o_ref.dtype)
    lse_ref[...] = lse_all


def _proj_res_kernel(*refs, rope, dils):
    n_out = len(dils)
    x_ref, w_ref = refs[:2]
    rest = refs[2:]
    if rope:
        cos_ref, sin_ref = rest[:2]
        rest = rest[2:]
    out_refs, xb_ref, scr_ref = rest[:n_out], rest[n_out], rest[n_out + 1]
    tm, tn = x_ref.shape[0], w_ref.shape[1]

    @pl.when(pl.program_id(1) == 0)
    def _():
        xb_ref[...] = x_ref[...].astype(BF16)

    acc = _mm(xb_ref[...], w_ref[...])
    if rope:
        lane = lax.broadcasted_iota(I32, acc.shape, 1)
        lo = (lane % HEAD_DIM) < ROPE_HALF
        rot = jnp.where(lo, pltpu.roll(acc, tn - ROPE_HALF, 1), pltpu.roll(acc, ROPE_HALF, 1))
        acc = acc * cos_ref[...] + rot * sin_ref[...]
    if any(d > 1 for d in dils):
        for c in range(tn // 128):
            scr_ref[c] = acc[:, c * 128:(c + 1) * 128]
    for o_ref, dil in zip(out_refs, dils):
        if dil == 1:
            o_ref[...] = acc.astype(o_ref.dtype)
            continue
        for r in range(dil):
            for c in range(tn // 128):
                o_ref[:, r * tn + c * 128:r * tn + (c + 1) * 128] = (
                    scr_ref[c, pl.ds(r, tm // dil, stride=dil), :].astype(o_ref.dtype))


def _proj_res(x, w, dils, *, seq, rope, tm=1024, tn=D_MODEL):
    T, K = x.shape
    N = w.shape[1]
    nparts = N // tn
    in_specs = [pl.BlockSpec((tm, K), lambda i, j: (i, 0)), pl.BlockSpec((K, tn), lambda i, j: (0, j))]
    args = [x, w]
    if rope:
        nseq = seq // tm
        in_specs += [pl.BlockSpec((tm, tn), lambda i, j: (i % nseq, 0))] * 2
        args += list(_rope_tables(jnp.arange(seq), tn))
    return pl.pallas_call(
        functools.partial(_proj_res_kernel, rope=rope, dils=dils),
        name="proj_residue_rope" if rope else "proj_residue",
        grid=(T // tm, nparts),
        in_specs=in_specs,
        out_specs=[pl.BlockSpec((tm // d, d * tn), lambda i, j: (i, j)) for d in dils],
        out_shape=[jax.ShapeDtypeStruct((T // d, nparts * d * tn), BF16) for d in dils],
        scratch_shapes=[pltpu.VMEM((tm, K), BF16), pltpu.VMEM((tn // 128, tm, 128), F32)],
        compiler_params=_cparams("parallel", "arbitrary"),
    )(*args)


def _dil_group(qk, v, B, S, dil):
    NQ = DIL_STEPS
    U = S // dil
    nb = U // NQ
    W = D_MODEL
    qk_view = qk.reshape(B, U, 2 * dil * W)
    v_view = v.reshape(B, U, dil * W)
    cur = lambda part: (lambda b, r, n: (b, n, part * dil + r))
    prev = lambda part: (lambda b, r, n: (b, jnp.maximum(n - 1, 0), part * dil + r))
    blk = lambda f: pl.BlockSpec((None, NQ, W), f)
    o, lse = pl.pallas_call(
        _dil_kernel,
        name=f"dilated_attn_{dil}",
        grid=(B, dil, nb),
        in_specs=[blk(cur(0)), blk(prev(1)), blk(cur(1)), blk(prev(0)), blk(cur(0))],
        out_specs=[pl.BlockSpec((None, NQ, W), lambda b, r, n: (b, n, r)),
                   pl.BlockSpec((None, NQ, 128), lambda b, r, n: (b, n, r))],
        out_shape=[jax.ShapeDtypeStruct((B, U, dil * W), BF16), jax.ShapeDtypeStruct((B, U, dil * 128), F32)],
        compiler_params=_cparams("parallel", "parallel", "arbitrary"),
    )(qk_view, qk_view, qk_view, v_view, v_view)
    return o.reshape(B * U, dil * W), lse.reshape(B * U, dil * 128)


def _dil_outproj_ln_kernel(o0_ref, o1_ref, o2_ref, l0_ref, l1_ref, l2_ref, hexp_ref, w_ref, x_ref, g_ref, b_ref, out_ref,
                           oscr_ref, lscr_ref, *, dils):
    tm, W = x_ref.shape

    def natural(o_ref, l_ref, dil):
        if dil == 1:
            return o_ref[...].astype(F32), l_ref[...]
        n = tm // dil
        for r in range(dil):
            lscr_ref[0, pl.ds(r, n, stride=dil), :] = l_ref[:, r * 128:(r + 1) * 128]
            for c in range(W // 128):
                oscr_ref[c, pl.ds(r, n, stride=dil), :] = o_ref[:, r * W + c * 128:r * W + (c + 1) * 128].astype(F32)
        return jnp.concatenate([oscr_ref[c] for c in range(W // 128)], axis=1), lscr_ref[0]

    os_, ls = zip(*[natural(o, l, d) for o, l, d in zip((o0_ref, o1_ref, o2_ref), (l0_ref, l1_ref, l2_ref), dils)])
    m = jnp.maximum(jnp.maximum(ls[0], ls[1]), ls[2])
    es = [jnp.exp(l - m) for l in ls]
    tot = es[0] + es[1] + es[2]
    y = jnp.zeros((tm, W), F32)
    for e, o in zip(es, os_):
        y = y + _mm01(e / tot, hexp_ref[...]) * o
    z = ALPHA * x_ref[...] + _mm(y.astype(BF16), w_ref[...])
    out_ref[...] = _ln_rows(z, g_ref[...], b_ref[...])


def _dilated_layer(x, B, S, w_in, w_out, g, b, *, tm=256):
    T, D = x.shape
    dils = tuple(d for _, d in DIL_GROUPS)
    wb = w_in.astype(BF16)
    vals = _proj_res(x, wb[:, 2 * len(dils) * D:], dils, seq=S, rope=False)
    outs = []
    for gi, dil in enumerate(dils):
        qk, = _proj_res(x, wb[:, 2 * gi * D:(2 * gi + 2) * D], (dil,), seq=S, rope=True)
        outs.append(_dil_group(qk, vals[gi], B, S, dil))
    hexp = np.zeros((128, D), np.float32)
    for h in range(DIL_HEADS):
        hexp[h, h * HEAD_DIM:(h + 1) * HEAD_DIM] = 1.0
    row = lambda i: (i, 0)
    fixed = lambda i: (0, 0)
    return pl.pallas_call(
        functools.partial(_dil_outproj_ln_kernel, dils=dils),
        name="dilated_outproj_ln",
        grid=(T // tm,),
        in_specs=[pl.BlockSpec((tm // d, d * D), row) for d in dils] + [pl.BlockSpec((tm // d, d * 128), row) for d in dils]
        + [pl.BlockSpec((128, D), fixed), pl.BlockSpec((D, D), fixed), pl.BlockSpec((tm, D), row),
           pl.BlockSpec((1, D), fixed), pl.BlockSpec((1, D), fixed)],
        out_specs=pl.BlockSpec((tm, D), row),
        out_shape=jax.ShapeDtypeStruct((T, D), F32),
        scratch_shapes=[pltpu.VMEM((D // 128, tm, 128), F32), pltpu.VMEM((1, tm, 128), F32)],
        compiler_params=_cparams("parallel"),
    )(outs[0][0], outs[1][0], outs[2][0], outs[0][1], outs[1][1], outs[2][1], jnp.asarray(hexp),
      w_out.astype(BF16), x, g.reshape(1, D), b.reshape(1, D))


def _pool_ln_kernel(x_ref, halo_ref, w_ref, scale_ref, g_ref, b_ref, o_ref, ext_ref):
    TS = x_ref.shape[0]
    s = pl.program_id(1)
    x = x_ref[...]
    ext_ref[0:POOL_HALO, :] = jnp.where(s == 0, 0.0, halo_ref[...])
    ext_ref[POOL_HALO:POOL_HALO + TS, :] = x
    cnt = (s * TS + lax.broadcasted_iota(I32, (TS, 1), 0) + 1).astype(F32)
    ys = []
    for gi, w in enumerate(POOL_WINDOWS):
        sl = slice(gi * POOL_GROUP, (gi + 1) * POOL_GROUP)
        xg = x[:, sl]
        tot = xg
        for j in range(1, w):
            tot = tot + ext_ref[POOL_HALO - j:POOL_HALO - j + TS, sl]
        mean = tot / jnp.minimum(cnt, float(w))
        ys.append(_mm((mean - xg).astype(BF16), w_ref[gi]))
    y = jnp.concatenate(ys, axis=1) * scale_ref[...]
    o_ref[...] = _ln_rows(ALPHA * x + y, g_ref[...], b_ref[...])


def _pool_layer(x, B, S, w_grp, scale, g, b, *, ts=512):
    T, D = x.shape
    ns = S // ts
    hb = ts // POOL_HALO
    fixed = lambda bb, s: (0, 0)
    return pl.pallas_call(
        _pool_ln_kernel,
        name="pool_ln",
        grid=(B, ns),
        in_specs=[pl.BlockSpec((ts, D), lambda bb, s: (bb * ns + s, 0)),
                  pl.BlockSpec((POOL_HALO, D), lambda bb, s: (jnp.maximum((bb * ns + s) * hb - 1, 0), 0)),
                  pl.BlockSpec((len(POOL_WINDOWS), POOL_GROUP, POOL_GROUP), lambda bb, s: (0, 0, 0)),
                  pl.BlockSpec((1, D), fixed), pl.BlockSpec((1, D), fixed), pl.BlockSpec((1, D), fixed)],
        out_specs=pl.BlockSpec((ts, D), lambda bb, s: (bb * ns + s, 0)),
        out_shape=jax.ShapeDtypeStruct((T, D), F32),
        scratch_shapes=[pltpu.VMEM((ts + POOL_HALO, D), F32)],
        compiler_params=_cparams("parallel", "arbitrary"),
    )(x, x, w_grp.astype(BF16), scale.reshape(1, D), g.reshape(1, D), b.reshape(1, D))


U32 = jnp.uint32
HI16 = 0xFFFF0000
ROW_WORDS = D_MODEL // 2


def _pack_halves(v):
    h = v.shape[1] // 2
    bits = lambda a: lax.bitcast_convert_type(a.astype(BF16).astype(F32), U32)
    return (bits(v[:, :h]) >> 16) | (bits(v[:, h:]) & U32(HI16))


def _unpack_halves(w):
    lo = lax.bitcast_convert_type(w << 16, F32)
    hi = lax.bitcast_convert_type(w & U32(HI16), F32)
    return jnp.concatenate([lo, hi], axis=1)


def _router_kernel(x_ref, wT_ref, bias_ref, triu_ref, eidx_ref, gw_ref, rank_ref, cnt_ref, xp_ref, carry_ref):
    E = N_EXPERTS
    per = E // N_EXPERT_GROUPS
    TM = x_ref.shape[0]

    @pl.when(pl.program_id(0) == 0)
    def _():
        carry_ref[...] = jnp.zeros_like(carry_ref)

    x = x_ref[...]
    xp_ref[...] = _pack_halves(x)

    scores = _sigmoid(_nt(wT_ref[...], x, precision=HIGHEST))
    biased = scores + bias_ref[...]
    eio = lax.broadcasted_iota(I32, (E, TM), 0)

    gio = lax.broadcasted_iota(I32, (per, TM), 0)
    gscore = []
    for gidx in range(N_EXPERT_GROUPS):
        slab = biased[gidx * per:(gidx + 1) * per, :]
        m1 = jnp.max(slab, axis=0, keepdims=True)
        i1 = jnp.min(jnp.where(slab == m1, gio, per), axis=0, keepdims=True)
        m2 = jnp.max(jnp.where(gio == i1, BELOW_NEG_INF, slab), axis=0, keepdims=True)
        gscore.append(m1 + m2)
    slabs = []
    for gidx in range(N_EXPERT_GROUPS):
        beat = jnp.zeros((1, TM), F32)
        for o in range(N_EXPERT_GROUPS):
            if o == gidx:
                continue
            wins = (gscore[o] >= gscore[gidx]) if o < gidx else (gscore[o] > gscore[gidx])
            beat = beat + jnp.where(wins, 1.0, 0.0)
        keep = beat < float(TOPK_GROUPS)
        slabs.append(jnp.where(keep, biased[gidx * per:(gidx + 1) * per, :], NEG_INF))
    cur = jnp.concatenate(slabs, axis=0)

    picked = jnp.zeros((E, TM), F32)
    idxs, vals = [], []
    for _ in range(TOP_K):
        m = jnp.max(cur, axis=0, keepdims=True)
        idx = jnp.min(jnp.where(cur == m, eio, E), axis=0, keepdims=True)
        hit = eio == idx
        picked = jnp.where(hit, 1.0, picked)
        cur = jnp.where(hit, BELOW_NEG_INF, cur)
        idxs.append(idx)
        vals.append(jnp.sum(jnp.where(hit, scores, 0.0), axis=0, keepdims=True))
    total = vals[0]
    for v in vals[1:]:
        total = total + v

    pos = _mm(picked.astype(BF16), triu_ref[...]) + carry_ref[...]
    for k in range(TOP_K):
        eidx_ref[k:k + 1, :] = idxs[k]
        gw_ref[k:k + 1, :] = vals[k] / total * ROUTED_SCALE
        rank_ref[k:k + 1, :] = jnp.sum(jnp.where(eio == idxs[k], pos, 0.0), axis=0, keepdims=True).astype(I32)
    carry_ref[...] = carry_ref[...] + jnp.sum(picked, axis=1, keepdims=True)
    cnt_ref[...] = carry_ref[...].astype(I32)


def _router(x, router_w, router_bias, *, tm=512):
    T, D = x.shape
    E = N_EXPERTS
    triu = jnp.triu(jnp.ones((tm, tm), F32), k=1).astype(BF16)
    col = lambda i: (0, i)
    fixed = lambda i: (0, 0)
    return pl.pallas_call(
        _router_kernel,
        name="moe_router",
        grid=(T // tm,),
        in_specs=[pl.BlockSpec((tm, D), lambda i: (i, 0)), pl.BlockSpec((E, D), fixed),
                  pl.BlockSpec((E, 1), fixed), pl.BlockSpec((tm, tm), fixed)],
        out_specs=[pl.BlockSpec((TOP_K, tm), col), pl.BlockSpec((TOP_K, tm), col), pl.BlockSpec((TOP_K, tm), col),
                   pl.BlockSpec((E, 1), fixed), pl.BlockSpec((tm, ROW_WORDS), lambda i: (i, 0))],
        out_shape=[jax.ShapeDtypeStruct((TOP_K, T), I32), jax.ShapeDtypeStruct((TOP_K, T), F32),
                   jax.ShapeDtypeStruct((TOP_K, T), I32), jax.ShapeDtypeStruct((E, 1), I32),
                   jax.ShapeDtypeStruct((T, ROW_WORDS), U32)],
        scratch_shapes=[pltpu.VMEM((E, 1), F32)],
        compiler_params=_cparams("arbitrary"),
    )(x, router_w.T.astype(F32), router_bias.reshape(E, 1).astype(F32), triu)


def _tile_indices(dest_hbm, idx_ref, sem_idx, n_idx):
    i = pl.program_id(0)

    def idx_copy(step):
        slot = step % 2
        return pltpu.make_async_copy(dest_hbm.at[step], idx_ref.at[pl.ds(slot * n_idx, n_idx)], sem_idx.at[slot])

    @pl.when(i == 0)
    def _():
        idx_copy(0).start()

    idx_copy(i).wait()

    @pl.when(i + 1 < pl.num_programs(0))
    def _():
        idx_copy(i + 1).start()

    return (i % 2) * n_idx


def _dispatch_kernel(cnt_ref, pstart_ref, dest_hbm, x_ref, rows_ref, idx_ref, zero_ref, sem_idx, sem_rows, sem_zero):
    TD = x_ref.shape[0]

    @pl.when(pl.program_id(0) == 0)
    def _():
        zero_ref[...] = jnp.zeros_like(zero_ref)

        def per_expert(e, c):
            n = cnt_ref[e]
            base = pstart_ref[e]
            pad_to = (n + EXP_BLK - 1) // EXP_BLK * EXP_BLK
            fill = lambda r: pltpu.make_async_copy(zero_ref.at[pl.ds(0, 1)], rows_ref.at[base + r], sem_zero)

            def start(r, c2):
                fill(r).start()
                return c2

            def wait(r, c2):
                fill(r).wait()
                return c2

            lax.fori_loop(n, pad_to, start, 0)
            lax.fori_loop(n, pad_to, wait, 0)
            return c

        lax.fori_loop(0, N_EXPERTS, per_expert, 0)

    base = _tile_indices(dest_hbm, idx_ref, sem_idx, TOP_K * TD)
    scatter = lambda t, k: pltpu.make_async_copy(x_ref.at[pl.ds(t, 1)], rows_ref.at[idx_ref[base + k * TD + t]], sem_rows)

    def issue(t, c):
        for k in range(TOP_K):
            scatter(t, k).start(priority=k % 2)
        return c

    def drain(t, c):
        for k in range(TOP_K):
            scatter(t, k).wait()
        return c

    lax.fori_loop(0, TD, issue, 0, unroll=8)
    lax.fori_loop(0, TD, drain, 0, unroll=8)


def _dispatch(xp, dest_tiles, counts, pad_start, n_rows, *, td):
    T = xp.shape[0]
    return pl.pallas_call(
        _dispatch_kernel,
        name="moe_dispatch",
        grid_spec=pltpu.PrefetchScalarGridSpec(
            num_scalar_prefetch=2,
            grid=(T // td,),
            in_specs=[pl.BlockSpec(memory_space=pl.ANY), pl.BlockSpec((td, ROW_WORDS), lambda i, c, p: (i, 0))],
            out_specs=pl.BlockSpec(memory_space=pl.ANY),
            scratch_shapes=[pltpu.SMEM((2 * TOP_K * td,), I32), pltpu.VMEM((8, ROW_WORDS), U32),
                            pltpu.SemaphoreType.DMA((2,)), pltpu.SemaphoreType.DMA(()), pltpu.SemaphoreType.DMA(())],
        ),
        out_shape=jax.ShapeDtypeStruct((n_rows, 1, ROW_WORDS), U32),
        compiler_params=_cparams("arbitrary"),
    )(counts, pad_start, dest_tiles, xp)


def _expert_kernel(be_ref, nu_ref, rows_hbm, wg_ref, wu_ref, wd_ref, out_hbm, xbuf, obuf, wgu_bf, wd_bf, sem_in, sem_out):
    i = pl.program_id(0)
    nu = nu_ref[0]

    @pl.when((i < nu) & ((i == 0) | (be_ref[i] != be_ref[jnp.maximum(i - 1, 0)])))
    def _():
        wgu_bf[:, :EXPERT_FF] = wg_ref[0].astype(BF16)
        wgu_bf[:, EXPERT_FF:] = wu_ref[0].astype(BF16)
        wd_bf[...] = wd_ref[0].astype(BF16)

    blk = lambda step: pl.ds(pl.multiple_of(step * EXP_BLK, EXP_BLK), EXP_BLK)
    in_copy = lambda step: pltpu.make_async_copy(rows_hbm.at[blk(step), 0], xbuf.at[step % 2], sem_in.at[step % 2])
    out_copy = lambda step: pltpu.make_async_copy(obuf.at[step % 2], out_hbm.at[blk(step), 0], sem_out.at[step % 2])

    @pl.when(i == 0)
    def _():
        in_copy(0).start()

    @pl.when(i < nu)
    def _():
        in_copy(i).wait()

        @pl.when(i + 1 < nu)
        def _():
            in_copy(i + 1).start()

        @pl.when(i >= 2)
        def _():
            out_copy(i - 2).wait()

        slot = i % 2
        gu = _mm(_unpack_halves(xbuf[slot]).astype(BF16), wgu_bf[...])
        h = _silu(gu[:, :EXPERT_FF]) * gu[:, EXPERT_FF:]
        obuf[slot] = _pack_halves(_mm(h.astype(BF16), wd_bf[...]))
        out_copy(i).start()

    @pl.when(i == pl.num_programs(0) - 1)
    def _():
        @pl.when(nu >= 2)
        def _():
            out_copy(nu - 2).wait()

        out_copy(nu - 1).wait()


def _experts(rows, blk_expert, n_used, w_gate, w_up, w_down, layer):
    R = rows.shape[0]
    D = w_down.shape[-1]
    n_blk = R // EXP_BLK
    expert = lambda i, be, nu: (layer, be[jnp.minimum(i, nu[0] - 1)], 0, 0)
    return pl.pallas_call(
        _expert_kernel,
        name="moe_experts",
        grid_spec=pltpu.PrefetchScalarGridSpec(
            num_scalar_prefetch=2,
            grid=(n_blk,),
            in_specs=[pl.BlockSpec(memory_space=pl.ANY), pl.BlockSpec((None, 1, D, EXPERT_FF), expert),
                      pl.BlockSpec((None, 1, D, EXPERT_FF), expert), pl.BlockSpec((None, 1, EXPERT_FF, D), expert)],
            out_specs=pl.BlockSpec(memory_space=pl.ANY),
            scratch_shapes=[pltpu.VMEM((2, EXP_BLK, ROW_WORDS), U32), pltpu.VMEM((2, EXP_BLK, ROW_WORDS), U32),
                            pltpu.VMEM((D, 2 * EXPERT_FF), BF16), pltpu.VMEM((EXPERT_FF, D), BF16),
                            pltpu.SemaphoreType.DMA((2,)), pltpu.SemaphoreType.DMA((2,))],
        ),
        out_shape=jax.ShapeDtypeStruct((R, 1, ROW_WORDS), U32),
        compiler_params=_cparams("arbitrary"),
    )(blk_expert, n_used, rows, w_gate, w_up, w_down)


def _combine_kernel(dest_hbm, gw_ref, x_ref, rows_ref, wsgu_ref, wsd_ref, g_ref, b_ref, p_ref, wp_ref, wpg_ref,
                    o_ref, idx_ref, buf_ref, sem_idx, sem_rows):
    TM = x_ref.shape[0]
    n_idx = TOP_K * TM
    i = pl.program_id(0)
    n = pl.num_programs(0)

    def idx_copy(step):
        slot = step % 3
        return pltpu.make_async_copy(dest_hbm.at[step], idx_ref.at[pl.ds(slot * n_idx, n_idx)], sem_idx.at[slot])

    def row_loop(step, slot, start):
        base = (step % 3) * n_idx

        def body(t, c):
            for k in range(TOP_K):
                cp = pltpu.make_async_copy(rows_ref.at[idx_ref[base + k * TM + t]], buf_ref.at[slot, k, pl.ds(t, 1)],
                                           sem_rows.at[slot])
                if start:
                    cp.start(priority=k % 2)
                else:
                    cp.wait()
            return c

        lax.fori_loop(0, TM, body, 0, unroll=8)

    def for_parity(step, fn):
        for slot in range(2):
            pl.when(step % 2 == slot)(functools.partial(fn, slot))

    @pl.when(i == 0)
    def _():
        idx_copy(0).start()
        idx_copy(0).wait()
        row_loop(0, 0, True)

        @pl.when(n > 1)
        def _():
            idx_copy(1).start()

    @pl.when(i + 1 < n)
    def _():
        idx_copy(i + 1).wait()

        @pl.when(i + 2 < n)
        def _():
            idx_copy(i + 2).start()

        for_parity(i + 1, lambda slot: row_loop(i + 1, slot, True))

    x = x_ref[...]
    gu = _mm(x.astype(BF16), wsgu_ref[...])
    ff = gu.shape[1] // 2
    f = _mm((_silu(gu[:, :ff]) * gu[:, ff:]).astype(BF16), wsd_ref[...])
    for_parity(i, lambda slot: row_loop(i, slot, False))
    gw = gw_ref[...]
    cur = i % 2
    for k in range(TOP_K):
        f = f + gw[:, k:k + 1] * _unpack_halves(buf_ref[cur, k])
    x2 = _ln_rows(ALPHA * x + f, g_ref[...], b_ref[...])
    gate = _sigmoid(_mm(x2.astype(BF16), wpg_ref[...]))
    o_ref[...] = x2 + gate * _mm(p_ref[...].astype(BF16), wp_ref[...])


def _combine(dest_tiles, gw, x, rows_out, wsgu, wsd, g, b, p, wp, wpg, *, tm):
    T, D = x.shape
    row = lambda i: (i, 0)
    fixed = lambda i: (0, 0)
    full = lambda a: pl.BlockSpec(a.shape, fixed)
    return pl.pallas_call(
        _combine_kernel,
        name="moe_combine",
        grid=(T // tm,),
        in_specs=[pl.BlockSpec(memory_space=pl.ANY), pl.BlockSpec((tm, TOP_K), row), pl.BlockSpec((tm, D), row),
                  pl.BlockSpec(memory_space=pl.ANY), full(wsgu), full(wsd), pl.BlockSpec((1, D), fixed),
                  pl.BlockSpec((1, D), fixed), pl.BlockSpec((tm, PLE_DIM), row), full(wp), full(wpg)],
        out_specs=pl.BlockSpec((tm, D), row),
        out_shape=jax.ShapeDtypeStruct((T, D), F32),
        scratch_shapes=[pltpu.SMEM((3 * TOP_K * tm,), I32), pltpu.VMEM((2, TOP_K, tm, ROW_WORDS), U32),
                        pltpu.SemaphoreType.DMA((3,)), pltpu.SemaphoreType.DMA((2,))],
        compiler_params=_cparams("arbitrary"),
    )(dest_tiles, gw, x, rows_out, wsgu, wsd, g.reshape(1, D), b.reshape(1, D), p, wp, wpg)


MOE_TILE = 256


def _moe_ple_layer(x, p, layer, router_w, router_bias, w_gate, w_up, w_down, ws_gate, ws_up, ws_down, g, b, ple_w, ple_gate_w):
    T, D = x.shape
    eidx, gw, rank, counts, xp = _router(x, router_w, router_bias)
    counts = counts.reshape(N_EXPERTS)
    padded = (counts + EXP_BLK - 1) // EXP_BLK * EXP_BLK
    pad_end = jnp.cumsum(padded)
    pad_start = pad_end - padded
    n_blk = T * TOP_K // EXP_BLK + N_EXPERTS
    e_iota = jnp.arange(N_EXPERTS, dtype=I32)
    dest = rank + jnp.sum(jnp.where(eidx[..., None] == e_iota, pad_start, 0), axis=-1)
    tm = MOE_TILE
    dest_tiles = dest.reshape(TOP_K, T // tm, tm).transpose(1, 0, 2).reshape(T // tm, TOP_K * tm)
    blk_first = jnp.arange(n_blk, dtype=I32)[:, None] * EXP_BLK
    blk_expert = jnp.minimum(jnp.sum((pad_end[None, :] <= blk_first).astype(I32), axis=-1), N_EXPERTS - 1)
    n_used = (pad_end[-1:] // EXP_BLK).astype(I32)

    rows = _dispatch(xp, dest_tiles, counts.astype(I32), pad_start.astype(I32), n_blk * EXP_BLK, td=tm)
    rows_out = _experts(rows, blk_expert, n_used, w_gate, w_up, w_down, layer)
    wsgu = jnp.concatenate([ws_gate, ws_up], axis=-1).astype(BF16)
    return _combine(dest_tiles, gw.T, x, rows_out, wsgu, ws_down.astype(BF16), g, b, p,
                    ple_w.astype(BF16), ple_gate_w.astype(BF16), tm=tm)


def kernel(x, p, ln_g, ln_b, mlstm_w_in, mlstm_conv, mlstm_ig_bias, mlstm_fg_bias, mlstm_norm_g, mlstm_w_out, nsa_w_in, nsa_cmp_pos_k, nsa_cmp_pos_v, nsa_cmp_wk, nsa_cmp_wv, nsa_w_out, dil_w_in, dil_w_out, pool_w, pool_scale, router_w, router_bias, exp_w_gate, exp_w_up, exp_w_down, sh_w_gate, sh_w_up, sh_w_down, ple_w, ple_gate_w):
    B, S, D = x.shape
    T = B * S
    xf = x.reshape(T, D)
    pf = p.reshape(DEPTH, T, PLE_DIM)
    for i in range(DEPTH):
        kind, j = i % 4, i // 4
        g1, b1 = ln_g[i, 0], ln_b[i, 0]
        if kind == 0:
            y = _mlstm_mixer(xf, B, S, mlstm_w_in[j], mlstm_conv[j], mlstm_ig_bias[j], mlstm_fg_bias[j], mlstm_norm_g[j])
            xf = _outproj_ln(y, mlstm_w_out[j].astype(BF16), xf, g1, b1)
        elif kind == 1:
            y = _nsa_mixer(xf, B, S, nsa_w_in[j], nsa_cmp_pos_k[j], nsa_cmp_pos_v[j], nsa_cmp_wk[j], nsa_cmp_wv[j])
            xf = _outproj_ln(y, nsa_w_out[j].astype(BF16), xf, g1, b1)
        elif kind == 2:
            xf = _dilated_layer(xf, B, S, dil_w_in[j], dil_w_out[j], g1, b1)
        else:
            xf = _pool_layer(xf, B, S, pool_w[j], pool_scale[j], g1, b1)
        xf = _moe_ple_layer(xf, pf[i], i, router_w[i], router_bias[i], exp_w_gate, exp_w_up, exp_w_down,
                            sh_w_gate[i], sh_w_up[i], sh_w_down[i], ln_g[i, 1], ln_b[i, 1], ple_w[i], ple_gate_w[i])
    return xf.reshape(B, S, D)
```

```python
import functools

import numpy as np
import jax
import jax.numpy as jnp
from jax import lax
from jax.experimental import pallas as pl
from jax.experimental.pallas import tpu as pltpu

F32 = jnp.float32
BF16 = jnp.bfloat16
I32 = jnp.int32
HIGHEST = lax.Precision.HIGHEST

D_MODEL = 1024
DEPTH = 4
ALPHA = (2.0 * DEPTH) ** 0.25
LN_EPS = 1e-5
NEG_INF = -1e30
TINY = 1e-30
BELOW_NEG_INF = -3e38
NEG_FLOOR = -1e29
ROPE_THETA = 10000.0
HEAD_DIM = 64
ROPE_HALF = HEAD_DIM // 2

MLSTM_HEADS = 8
MLSTM_QK_DIM = 64
MLSTM_V_DIM = 128
MLSTM_CONV = 4
MLSTM_L = 256

NSA_HEADS = 16
NSA_KV_HEADS = 4
NSA_J = NSA_HEADS // NSA_KV_HEADS
CMP_STRIDE = 16
CMP_BLK = 32
SEL_BLK = 64
N_SEL = 16
NSA_WINDOW = 512
FORCE_SCORE = 1e4
NSA_TQ = 256
NSA_TQ_SELECT = 512
NSA_TK = 512

DIL_HEADS = 16
DIL_GROUPS = ((128, 1), (512, 4), (2048, 16))
DIL_STEPS = 128

POOL_WINDOWS = (2, 4, 8, 16)
POOL_GROUP = 256
POOL_HALO = 16

N_EXPERTS = 64
TOP_K = 8
N_EXPERT_GROUPS = 8
TOPK_GROUPS = 4
EXPERT_FF = 256
ROUTED_SCALE = 2.5
EXP_BLK = 1024
PLE_DIM = 256

VMEM_LIMIT = 48 * 1024 * 1024


def _cparams(*sem):
    return pltpu.CompilerParams(dimension_semantics=sem, vmem_limit_bytes=VMEM_LIMIT)


def _nt(a, b, **kw):
    return lax.dot_general(a, b, (((1,), (1,)), ((), ())), preferred_element_type=F32, **kw)


def _tn(a, b, **kw):
    return lax.dot_general(a, b, (((0,), (0,)), ((), ())), preferred_element_type=F32, **kw)


def _mm(a, b, **kw):
    return jnp.dot(a, b, preferred_element_type=F32, **kw)


def _split3(x):
    a = x.astype(BF16)
    r = x - a.astype(F32)
    b = r.astype(BF16)
    return a, b, (r - b.astype(F32)).astype(BF16)


def _mm01(x, e01):
    eb = e01.astype(BF16)
    a, b, c = _split3(x)
    return _mm(a, eb) + _mm(b, eb) + _mm(c, eb)


def _sigmoid(z):
    return 1.0 / (1.0 + jnp.exp(-z))


def _silu(z):
    return z * _sigmoid(z)


def _ln_rows(z, g, b):
    mu = jnp.mean(z, axis=-1, keepdims=True)
    d = z - mu
    var = jnp.mean(d * d, axis=-1, keepdims=True)
    return d * lax.rsqrt(var + LN_EPS) * g + b


def _proj_kernel(*refs, rope):
    if rope:
        x_ref, w_ref, cos_ref, sin_ref, o_ref, xb_ref = refs
    else:
        x_ref, w_ref, o_ref, xb_ref = refs

    @pl.when(pl.program_id(1) == 0)
    def _():
        xb_ref[...] = x_ref[...].astype(BF16)

    acc = _mm(xb_ref[...], w_ref[...])
    if rope:
        tn = acc.shape[1]
        lane = lax.broadcasted_iota(I32, acc.shape, 1)
        lo = (lane % HEAD_DIM) < ROPE_HALF
        rot = jnp.where(lo, pltpu.roll(acc, tn - ROPE_HALF, 1), pltpu.roll(acc, ROPE_HALF, 1))
        acc = acc * cos_ref[...] + rot * sin_ref[...]
    o_ref[...] = acc.astype(o_ref.dtype)


def _proj(x, w, *, out_dtype, tm, tn, rope_tables=None, seq=None):
    T, K = x.shape
    N = w.shape[1]
    assert T % tm == 0 and N % tn == 0
    in_specs = [pl.BlockSpec((tm, K), lambda i, j: (i, 0)), pl.BlockSpec((K, tn), lambda i, j: (0, j))]
    args = [x, w]
    if rope_tables is not None:
        nseq = seq // tm
        in_specs += [pl.BlockSpec((tm, tn), lambda i, j: (i % nseq, 0))] * 2
        args += list(rope_tables)
    return pl.pallas_call(
        functools.partial(_proj_kernel, rope=rope_tables is not None),
        name="proj_rope" if rope_tables is not None else "proj",
        grid=(T // tm, N // tn),
        in_specs=in_specs,
        out_specs=pl.BlockSpec((tm, tn), lambda i, j: (i, j)),
        out_shape=jax.ShapeDtypeStruct((T, N), out_dtype),
        scratch_shapes=[pltpu.VMEM((tm, K), BF16)],
        compiler_params=_cparams("parallel", "arbitrary"),
    )(*args)


def _rope_tables(pos, width):
    inv = ROPE_THETA ** (-jnp.arange(ROPE_HALF, dtype=F32) / ROPE_HALF)
    ang = pos.astype(F32)[:, None] * inv[None, :]
    cos, sin = jnp.cos(ang), jnp.sin(ang)
    cos64 = jnp.concatenate([cos, cos], -1)
    sin64 = jnp.concatenate([-sin, sin], -1)
    rep = width // HEAD_DIM
    return jnp.tile(cos64, (1, rep)), jnp.tile(sin64, (1, rep))


def _outproj_ln_kernel(y_ref, w_ref, x_ref, g_ref, b_ref, o_ref):
    y = _mm(y_ref[...].astype(BF16), w_ref[...])
    o_ref[...] = _ln_rows(ALPHA * x_ref[...] + y, g_ref[...], b_ref[...])


def _outproj_ln(y, w, x, g, b, *, tm=512):
    T, D = x.shape
    K = y.shape[1]
    row = lambda i: (i, 0)
    fixed = lambda i: (0, 0)
    return pl.pallas_call(
        _outproj_ln_kernel,
        name="outproj_ln",
        grid=(T // tm,),
        in_specs=[pl.BlockSpec((tm, K), row), pl.BlockSpec((K, D), fixed), pl.BlockSpec((tm, D), row),
                  pl.BlockSpec((1, D), fixed), pl.BlockSpec((1, D), fixed)],
        out_specs=pl.BlockSpec((tm, D), row),
        out_shape=jax.ShapeDtypeStruct((T, D), F32),
        compiler_params=_cparams("parallel"),
    )(y, w, x, g.reshape(1, D), b.reshape(1, D))


def _log_sigmoid(z):
    return jnp.minimum(z, 0.0) - jnp.log(1.0 + jnp.exp(-jnp.abs(z)))


def _mlstm_kernel(qk_ref, v_ref, o_ref, gc_ref, gr_ref, convw_ref, bias_c_ref, bias_r_ref, ng_ref,
                  tri_ref, triT_ref, out_ref, C_ref, n_ref, m_ref, ext_ref):
    L = qk_ref.shape[0]
    H, dk, dv = MLSTM_HEADS, MLSTM_QK_DIM, MLSTM_V_DIM

    @pl.when(pl.program_id(1) == 0)
    def _():
        C_ref[...] = jnp.zeros_like(C_ref)
        n_ref[...] = jnp.zeros_like(n_ref)
        m_ref[...] = jnp.zeros_like(m_ref)
        ext_ref[0:8, :] = jnp.zeros((8, ext_ref.shape[1]), F32)

    cur = qk_ref[...]
    ext_ref[8:8 + L, :] = cur
    acc = convw_ref[3:4, :] * cur
    for j in range(MLSTM_CONV - 1):
        acc = acc + convw_ref[j:j + 1, :] * ext_ref[5 + j:5 + j + L, :]
    ext_ref[0:8, :] = cur[L - 8:L, :]
    qk = _silu(acc)

    gc = gc_ref[...] + bias_c_ref[...]
    gr = gr_ref[...] + bias_r_ref[...]
    tri01 = tri_ref[...].astype(BF16)
    b_col = sum(_mm(tri01, t) for t in _split3(_log_sigmoid(gc)))
    b_row = _mm01(_log_sigmoid(gr[H:2 * H, :]), triT_ref[...])
    ig_row = gr[0:H, :]
    tri = lax.broadcasted_iota(I32, (L, L), 0) >= lax.broadcasted_iota(I32, (L, L), 1)

    for h in range(H):
        q = (qk[:, h * dk:(h + 1) * dk] * dk ** -0.5).astype(BF16)
        k = qk[:, H * dk + h * dk:H * dk + (h + 1) * dk]
        kb = k.astype(BF16)
        v = v_ref[:, h * dv:(h + 1) * dv].astype(BF16)
        b_c = b_col[:, H + h:H + h + 1]
        ig_c = gc[:, h:h + 1]
        b_r = b_row[h:h + 1, :]
        m_prev = m_ref[h:h + 1, 0:1]
        C = C_ref[h]
        n = n_ref[h:h + 1, :]

        logD = jnp.where(tri, b_c - b_r + ig_row[h:h + 1, :], NEG_INF)
        inter = b_c + m_prev
        m_t = jnp.maximum(inter, jnp.max(logD, axis=-1, keepdims=True))
        s = _nt(q, kb) * jnp.exp(logD - m_t)
        w_inter = jnp.exp(inter - m_t)
        num = w_inter * _mm(q, C.astype(BF16)) + _mm(s.astype(BF16), v)
        den = w_inter * jnp.sum(q.astype(F32) * n, axis=-1, keepdims=True) + jnp.sum(s, axis=-1, keepdims=True)
        hv = num / jnp.maximum(jnp.abs(den), jnp.exp(-m_t))

        mu = jnp.mean(hv, axis=-1, keepdims=True)
        d = hv - mu
        hn = d * lax.rsqrt(jnp.mean(d * d, axis=-1, keepdims=True) + LN_EPS)
        og = _sigmoid(o_ref[:, h * dv:(h + 1) * dv])
        out_ref[:, h * dv:(h + 1) * dv] = (hn * ng_ref[:, h * dv:(h + 1) * dv] * og).astype(out_ref.dtype)

        bL = b_c[L - 1:L, :]
        logw = bL - b_c + ig_c
        m_new = jnp.maximum(bL + m_prev, jnp.max(logw, axis=0, keepdims=True))
        decay = jnp.exp(bL + m_prev - m_new)
        kw = k * jnp.exp(logw - m_new)
        C_ref[h] = decay * C + _tn(kw.astype(BF16), v)
        n_ref[h:h + 1, :] = decay * n + jnp.sum(kw, axis=0, keepdims=True)
        m_ref[h:h + 1, :] = jnp.broadcast_to(m_new, (1, m_ref.shape[1]))


def _mlstm_mixer(x, B, S, w_in, conv_w, ig_bias, fg_bias, norm_g):
    T = B * S
    H, L = MLSTM_HEADS, MLSTM_L
    wb = w_in.astype(BF16)
    main = _proj(x, wb[:, :3 * D_MODEL], out_dtype=F32, tm=512, tn=1024)
    wg = jnp.pad(wb[:, 3 * D_MODEL:], ((0, 0), (0, 128 - 2 * H)))
    gates = _proj(x, wg, out_dtype=F32, tm=512, tn=128)
    gates_r = gates[:, :2 * H].reshape(B, S, 2 * H).transpose(0, 2, 1)
    bias16 = jnp.concatenate([ig_bias, fg_bias]).astype(F32)
    bias_c = jnp.pad(bias16, (0, 128 - 2 * H)).reshape(1, 128)
    bias_r = bias16.reshape(2 * H, 1)
    tri = jnp.tril(jnp.ones((L, L), F32))
    nc = S // L
    rowblk = lambda c: (lambda b, i: (b * nc + i, c))
    fixed = lambda b, i: (0, 0)
    return pl.pallas_call(
        _mlstm_kernel,
        name="mlstm",
        grid=(B, nc),
        in_specs=[pl.BlockSpec((L, D_MODEL), rowblk(0)), pl.BlockSpec((L, D_MODEL), rowblk(1)),
                  pl.BlockSpec((L, D_MODEL), rowblk(2)), pl.BlockSpec((L, 128), rowblk(0)),
                  pl.BlockSpec((None, 2 * H, L), lambda b, i: (b, 0, i)),
                  pl.BlockSpec((MLSTM_CONV, D_MODEL), fixed), pl.BlockSpec((1, 128), fixed),
                  pl.BlockSpec((2 * H, 1), fixed), pl.BlockSpec((1, D_MODEL), fixed),
                  pl.BlockSpec((L, L), fixed), pl.BlockSpec((L, L), fixed)],
        out_specs=pl.BlockSpec((L, D_MODEL), rowblk(0)),
        out_shape=jax.ShapeDtypeStruct((T, D_MODEL), BF16),
        scratch_shapes=[pltpu.VMEM((H, MLSTM_QK_DIM, MLSTM_V_DIM), F32), pltpu.VMEM((H, MLSTM_QK_DIM), F32),
                        pltpu.VMEM((H, 128), F32), pltpu.VMEM((L + 8, D_MODEL), F32)],
        compiler_params=_cparams("parallel", "arbitrary"),
    )(main, main, main, gates, gates_r, conv_w.astype(F32), bias_c, bias_r, norm_g.reshape(1, D_MODEL).astype(F32),
      tri, tri.T)


def _rot_cols(w):
    shp = w.shape
    w4 = w.reshape(shp[:-1] + (shp[-1] // HEAD_DIM, 2, ROPE_HALF))
    return jnp.flip(w4, axis=-2).reshape(shp)


def _nsa_compress_kernel(chk_ref, chv_ref, wk_lo_ref, wk_hi_ref, wv_lo_ref, wv_hi_ref, pk_ref, pv_ref, wk_ref, wv_ref,
                         cos_ref, sin_ref, kc_ref, vc_ref, sh_ref):
    NCH = chk_ref.shape[0]
    G = NSA_KV_HEADS
    sh_ref[NCH:NCH + 8, :] = jnp.zeros((8, sh_ref.shape[1]), F32)
    live = lax.broadcasted_iota(I32, (NCH, 1), 0) < NCH - 1

    def blocks(ch_ref, lo_ref, hi_ref, p_ref, w_ref):
        ch = ch_ref[...]
        n = lo_ref.shape[1]
        sh_ref[0:NCH, 0:n] = _mm(ch, hi_ref[...])
        const = _mm(p_ref[...].astype(BF16), w_ref[...])[0:1, :]
        return jnp.where(live, _mm(ch, lo_ref[...]) + sh_ref[1:NCH + 1, 0:n] + jnp.concatenate([const] * G, axis=1), 0.0)

    k2 = blocks(chk_ref, wk_lo_ref, wk_hi_ref, pk_ref, wk_ref)
    v2 = blocks(chv_ref, wv_lo_ref, wv_hi_ref, pv_ref, wv_ref)
    for g in range(G):
        raw = k2[:, 2 * g * HEAD_DIM:(2 * g + 1) * HEAD_DIM]
        rot = k2[:, (2 * g + 1) * HEAD_DIM:(2 * g + 2) * HEAD_DIM]
        kc_ref[g] = (raw * cos_ref[...] + rot * sin_ref[...]).astype(kc_ref.dtype)
        vc_ref[g] = v2[:, g * HEAD_DIM:(g + 1) * HEAD_DIM].astype(vc_ref.dtype)


def _nsa_cmp_select_kernel(q_ref, kc_ref, vc_ref, ovlT_ref, ocmp_ref, sel_ref):
    TQ = q_ref.shape[0]
    NCH = kc_ref.shape[0]
    NSB = ovlT_ref.shape[0]
    q0 = pl.program_id(2) * TQ
    t = q0 + lax.broadcasted_iota(I32, (TQ, 1), 0)
    cend = lax.broadcasted_iota(I32, (1, NCH), 1) * CMP_STRIDE + (CMP_BLK - 1)
    bias = jnp.where(cend <= t, 0.0, NEG_INF)
    kc = kc_ref[...]
    vc = vc_ref[...]
    psum = jnp.zeros((TQ, NCH), F32)
    outs = []
    for j in range(NSA_J):
        qj = q_ref[:, j * HEAD_DIM:(j + 1) * HEAD_DIM]
        s = _nt(qj, kc) * HEAD_DIM ** -0.5 + bias
        m = jnp.maximum(jnp.max(s, axis=-1, keepdims=True), NEG_FLOOR)
        e = jnp.exp(s - m)
        p = e / jnp.maximum(jnp.sum(e, axis=-1, keepdims=True), TINY)
        outs.append(_mm(p.astype(BF16), vc))
        psum = psum + p
    ocmp_ref[...] = jnp.concatenate(outs, axis=1).astype(ocmp_ref.dtype)

    ovl01 = ovlT_ref[...].astype(BF16)
    imp = sum(_nt(ovl01, t) for t in _split3(psum))
    nb = lax.broadcasted_iota(I32, (NSB, 1), 0)
    qblk = (q0 + lax.broadcasted_iota(I32, (1, TQ), 1)) // SEL_BLK
    forced = (nb == 0) | (nb == qblk) | (nb == qblk - 1)
    cur = jnp.where(forced, FORCE_SCORE, jnp.where(nb > qblk, NEG_INF, imp))
    sel = jnp.zeros((NSB, TQ), F32)
    for _ in range(min(N_SEL, NSB)):
        m = jnp.max(cur, axis=0, keepdims=True)
        idx = jnp.min(jnp.where(cur == m, nb, NSB), axis=0, keepdims=True)
        hit = nb == idx
        sel = jnp.where(hit, 1.0, sel)
        cur = jnp.where(hit, BELOW_NEG_INF, cur)
    sel_ref[...] = sel.astype(sel_ref.dtype)


def _nsa_main_kernel(q_ref, ksT_ref, vs_ref, kwT_ref, vw_ref, sel_ref, ocmp_ref, gate_ref, gexp_ref, o_ref):
    TQ = q_ref.shape[0]
    S = vs_ref.shape[0]
    TK = min(NSA_TK, S)
    J = NSA_J
    q0 = pl.program_id(2) * TQ
    q4 = jnp.concatenate([q_ref[:, j * HEAD_DIM:(j + 1) * HEAD_DIM] for j in range(J)], axis=0) * HEAD_DIM ** -0.5
    t = q0 + lax.broadcasted_iota(I32, (TQ, 1), 0)
    selT = sel_ref[...]

    def attend(carry, kT, v1, bias):
        m, acc = carry
        s = _mm(q4, kT) + jnp.concatenate([bias] * J, axis=0)
        m_new = jnp.maximum(m, jnp.max(s, axis=-1, keepdims=True))
        e = jnp.exp(s - m_new)
        return m_new, jnp.exp(m - m_new) * acc + _mm(e.astype(BF16), v1)

    def finish(acc):
        return acc[:, :HEAD_DIM] / jnp.maximum(acc[:, HEAD_DIM:], TINY)

    init = (jnp.full((J * TQ, 1), NEG_FLOOR, F32), jnp.zeros((J * TQ, 2 * HEAD_DIM), F32))

    def body(c, carry):
        k0 = pl.multiple_of(c * TK, TK)
        kpos = k0 + lax.broadcasted_iota(I32, (1, TK), 1)
        expand = jnp.where(kpos // SEL_BLK == lax.broadcasted_iota(I32, (selT.shape[0], 1), 0), 1.0, 0.0).astype(BF16)
        bias = jnp.where((_tn(selT, expand) > 0.5) & (kpos <= t), 0.0, NEG_INF)
        return attend(carry, ksT_ref[:, pl.ds(k0, TK)], vs_ref[pl.ds(k0, TK), :], bias)

    nchunks = (q0 + TQ - 1) // TK + 1
    o_slc = finish(lax.fori_loop(0, nchunks, body, init)[1])

    span = min(NSA_WINDOW + TQ, S)
    start = pl.multiple_of(jnp.clip(q0 - NSA_WINDOW, 0, S - span), TQ)
    dist = t - (start + lax.broadcasted_iota(I32, (1, span), 1))
    bias = jnp.where((dist >= 0) & (dist < NSA_WINDOW), 0.0, NEG_INF)
    o_win = finish(attend(init, kwT_ref[:, pl.ds(start, span)], vw_ref[pl.ds(start, span), :], bias)[1])

    unstack = lambda o: jnp.concatenate([o[j * TQ:(j + 1) * TQ, :] for j in range(J)], axis=1)
    g = _sigmoid(gate_ref[...])
    gx = [_mm01(g, gexp_ref[c]) for c in range(3)]
    out = gx[0] * ocmp_ref[...].astype(F32) + gx[1] * unstack(o_slc) + gx[2] * unstack(o_win)
    o_ref[...] = out.astype(o_ref.dtype)


def _nsa_mixer(x, B, S, w_in, cmp_pos_k, cmp_pos_v, cmp_wk, cmp_wv):
    T = B * S
    H, G, J, dh = NSA_HEADS, NSA_KV_HEADS, NSA_J, HEAD_DIM
    kv = G * dh
    wb = w_in.astype(BF16)
    col = lambda a, n: wb[:, a:a + n]
    o_q, o_kc, o_vc, o_ks, o_vs, o_kw, o_vw, o_g = np.cumsum([0, H * dh] + [kv] * 6).tolist()
    pos = jnp.arange(S)
    w_rope = jnp.concatenate([col(o_q, H * dh), col(o_ks, kv), col(o_kw, kv)], axis=1)
    roped = _proj(x, w_rope, out_dtype=BF16, tm=512, tn=512, rope_tables=_rope_tables(pos, 512), seq=S)
    w_plain = jnp.concatenate([col(o_vs, kv), col(o_vw, kv), jnp.pad(col(o_g, 3 * H), ((0, 0), (0, 128 - 3 * H)))], axis=1)
    plain = _proj(x, w_plain, out_dtype=F32, tm=512, tn=w_plain.shape[1])

    heads_T = lambda a: a.reshape(B, S, G, dh).transpose(0, 2, 3, 1)
    heads = lambda a: a.reshape(B, S, G, dh).transpose(0, 2, 1, 3)
    ksT = heads_T(roped[:, H * dh:H * dh + kv])
    kwT = heads_T(roped[:, H * dh + kv:])
    with_ones = lambda v: jnp.concatenate([v, jnp.ones_like(v)], axis=-1)
    vs = with_ones(heads(plain[:, :kv]).astype(BF16))
    vw = with_ones(heads(plain[:, kv:2 * kv]).astype(BF16))
    gates = plain[:, 2 * kv:]

    nch = S // CMP_STRIDE
    cw = CMP_STRIDE * kv
    chk = _proj(x, col(o_kc, kv), out_dtype=BF16, tm=512, tn=kv).reshape(B, nch, cw)
    chv = _proj(x, col(o_vc, kv), out_dtype=BF16, tm=512, tn=kv).reshape(B, nch, cw)
    wk = cmp_wk.reshape(CMP_BLK * dh, dh)
    wk2 = jnp.concatenate([wk, _rot_cols(wk)], axis=1).astype(BF16)
    wv2 = cmp_wv.reshape(CMP_BLK * dh, dh).astype(BF16)
    eye = jnp.eye(G, dtype=BF16)

    def per_head(w, part):
        n = w.shape[1]
        w3 = w.reshape(2, CMP_STRIDE, dh, n)[part]
        return jnp.einsum('lde,gh->lgdhe', w3, eye).reshape(cw, G * n)

    flat8 = lambda p_: jnp.pad(p_.reshape(1, CMP_BLK * dh), ((0, 7), (0, 0))).astype(F32)
    cend = jnp.arange(nch) * CMP_STRIDE + CMP_BLK - 1
    cos_c, sin_c = _rope_tables(cend, dh)
    fixed1 = lambda b: (0, 0)
    full1 = lambda a: pl.BlockSpec(a.shape, fixed1)
    consts = (per_head(wk2, 0), per_head(wk2, 1), per_head(wv2, 0), per_head(wv2, 1), flat8(cmp_pos_k), flat8(cmp_pos_v),
              wk2, wv2, cos_c, sin_c)
    kc, vc = pl.pallas_call(
        _nsa_compress_kernel,
        name="nsa_compress",
        grid=(B,),
        in_specs=[pl.BlockSpec((None, nch, cw), lambda b: (b, 0, 0))] * 2 + [full1(c) for c in consts],
        out_specs=[pl.BlockSpec((None, G, nch, dh), lambda b: (b, 0, 0, 0))] * 2,
        out_shape=[jax.ShapeDtypeStruct((B, G, nch, dh), BF16)] * 2,
        scratch_shapes=[pltpu.VMEM((nch + 8, 2 * kv), F32)],
        compiler_params=_cparams("parallel"),
    )(chk, chv, *consts)

    nsb = S // SEL_BLK
    c_idx, s_idx = np.arange(nch), np.arange(nsb)
    ovl = ((c_idx[:, None] * CMP_STRIDE + CMP_BLK - 1 >= s_idx[None, :] * SEL_BLK)
           & (c_idx[:, None] * CMP_STRIDE < (s_idx[None, :] + 1) * SEL_BLK)).astype(np.float32)
    bgi = lambda b, g, i: (b, g, 0, 0)
    TS = min(NSA_TQ_SELECT, S)
    ns = S // TS
    sblk = lambda b, g, i: (b * ns + i, g)
    ocmp, sel = pl.pallas_call(
        _nsa_cmp_select_kernel,
        name="nsa_cmp_select",
        grid=(B, G, ns),
        in_specs=[pl.BlockSpec((TS, J * dh), sblk), pl.BlockSpec((None, None, nch, dh), bgi),
                  pl.BlockSpec((None, None, nch, dh), bgi), pl.BlockSpec((nsb, nch), lambda b, g, i: (0, 0))],
        out_specs=[pl.BlockSpec((TS, J * dh), sblk), pl.BlockSpec((None, None, nsb, TS), lambda b, g, i: (b, g, 0, i))],
        out_shape=[jax.ShapeDtypeStruct((T, H * dh), BF16), jax.ShapeDtypeStruct((B, G, nsb, S), BF16)],
        compiler_params=_cparams("parallel", "parallel", "parallel"),
    )(roped, kc, vc, jnp.asarray(ovl.T))
    TQ = NSA_TQ
    nq = S // TQ
    qblk = lambda b, g, i: (b * nq + i, g)

    gexp = np.zeros((G, 3, 128, J * dh), np.float32)
    for g in range(G):
        for c in range(3):
            for j in range(J):
                gexp[g, c, (g * J + j) * 3 + c, j * dh:(j + 1) * dh] = 1.0
    return pl.pallas_call(
        _nsa_main_kernel,
        name="nsa_main",
        grid=(B, G, nq),
        in_specs=[pl.BlockSpec((TQ, J * dh), qblk)] + [pl.BlockSpec((None, None, dh, S), bgi),
                                                       pl.BlockSpec((None, None, S, 2 * dh), bgi)] * 2 + [
            pl.BlockSpec((None, None, nsb, TQ), lambda b, g, i: (b, g, 0, i)),
            pl.BlockSpec((TQ, J * dh), qblk), pl.BlockSpec((TQ, 128), lambda b, g, i: (b * nq + i, 0)),
            pl.BlockSpec((None, 3, 128, J * dh), lambda b, g, i: (g, 0, 0, 0))],
        out_specs=pl.BlockSpec((TQ, J * dh), qblk),
        out_shape=jax.ShapeDtypeStruct((T, H * dh), BF16),
        compiler_params=_cparams("parallel", "parallel", "arbitrary"),
    )(roped, ksT, vs, kwT, vw, sel, ocmp, gates, jnp.asarray(gexp))


def _dil_kernel(q_ref, kp_ref, kc_ref, vp_ref, vc_ref, o_ref, lse_ref):
    NQ = q_ref.shape[0]
    prev_from = jnp.where(pl.program_id(2) == 0, NQ, 0)
    qi = lax.broadcasted_iota(I32, (NQ, 1), 0)
    kj = lax.broadcasted_iota(I32, (1, 2 * NQ), 1)
    dist = NQ + qi - kj
    bias = jnp.where((dist >= 0) & (dist <= NQ) & (kj >= prev_from), 0.0, NEG_INF)
    lane = lax.broadcasted_iota(I32, (NQ, 128), 1)
    first = lane < HEAD_DIM
    lse_all = jnp.zeros((NQ, 128), F32)
    zero = jnp.zeros((), q_ref.dtype)
    for hp in range(DIL_HEADS // 2):
        sl = slice(hp * 128, (hp + 1) * 128)
        q2 = q_ref[:, sl]
        kb = jnp.concatenate([kp_ref[:, sl], kc_ref[:, sl]], axis=0)
        vb = jnp.concatenate([vp_ref[:, sl], vc_ref[:, sl]], axis=0)
        outs = []
        for sub in range(2):
            keep = first if sub == 0 else jnp.logical_not(first)
            s = _nt(jnp.where(keep, q2, zero), kb) * HEAD_DIM ** -0.5 + bias
            m = jnp.max(s, axis=-1, keepdims=True)
            e = jnp.exp(s - m)
            den = jnp.sum(e, axis=-1, keepdims=True)
            outs.append(_mm((e / den).astype(BF16), vb))
            lse_all = jnp.where(lane == 2 * hp + sub, m + jnp.log(den), lse_all)
        o_ref[:, sl] = jnp.where(first, outs[0], outs[1]).astype(o_ref.dtype)
    lse_ref[...] = lse_all


def _proj_res_kernel(*refs, rope, dils):
    n_out = len(dils)
    x_ref, w_ref = refs[:2]
    rest = refs[2:]
    if rope:
        cos_ref, sin_ref = rest[:2]
        rest = rest[2:]
    out_refs, xb_ref, scr_ref = rest[:n_out], rest[n_out], rest[n_out + 1]
    tm, tn = x_ref.shape[0], w_ref.shape[1]

    @pl.when(pl.program_id(1) == 0)
    def _():
        xb_ref[...] = x_ref[...].astype(BF16)

    acc = _mm(xb_ref[...], w_ref[...])
    if rope:
        lane = lax.broadcasted_iota(I32, acc.shape, 1)
        lo = (lane % HEAD_DIM) < ROPE_HALF
        rot = jnp.where(lo, pltpu.roll(acc, tn - ROPE_HALF, 1), pltpu.roll(acc, ROPE_HALF, 1))
        acc = acc * cos_ref[...] + rot * sin_ref[...]
    if any(d > 1 for d in dils):
        for c in range(tn // 128):
            scr_ref[c] = acc[:, c * 128:(c + 1) * 128]
    for o_ref, dil in zip(out_refs, dils):
        if dil == 1:
            o_ref[...] = acc.astype(o_ref.dtype)
            continue
        for r in range(dil):
            for c in range(tn // 128):
                o_ref[:, r * tn + c * 128:r * tn + (c + 1) * 128] = (
                    scr_ref[c, pl.ds(r, tm // dil, stride=dil), :].astype(o_ref.dtype))


def _proj_res(x, w, dils, *, seq, rope, tm=1024, tn=D_MODEL):
    T, K = x.shape
    N = w.shape[1]
    nparts = N // tn
    in_specs = [pl.BlockSpec((tm, K), lambda i, j: (i, 0)), pl.BlockSpec((K, tn), lambda i, j: (0, j))]
    args = [x, w]
    if rope:
        nseq = seq // tm
        in_specs += [pl.BlockSpec((tm, tn), lambda i, j: (i % nseq, 0))] * 2
        args += list(_rope_tables(jnp.arange(seq), tn))
    return pl.pallas_call(
        functools.partial(_proj_res_kernel, rope=rope, dils=dils),
        name="proj_residue_rope" if rope else "proj_residue",
        grid=(T // tm, nparts),
        in_specs=in_specs,
        out_specs=[pl.BlockSpec((tm // d, d * tn), lambda i, j: (i, j)) for d in dils],
        out_shape=[jax.ShapeDtypeStruct((T // d, nparts * d * tn), BF16) for d in dils],
        scratch_shapes=[pltpu.VMEM((tm, K), BF16), pltpu.VMEM((tn // 128, tm, 128), F32)],
        compiler_params=_cparams("parallel", "arbitrary"),
    )(*args)


def _dil_group(qk, v, B, S, dil):
    NQ = DIL_STEPS
    U = S // dil
    nb = U // NQ
    W = D_MODEL
    qk_view = qk.reshape(B, U, 2 * dil * W)
    v_view = v.reshape(B, U, dil * W)
    cur = lambda part: (lambda b, r, n: (b, n, part * dil + r))
    prev = lambda part: (lambda b, r, n: (b, jnp.maximum(n - 1, 0), part * dil + r))
    blk = lambda f: pl.BlockSpec((None, NQ, W), f)
    o, lse = pl.pallas_call(
        _dil_kernel,
        name=f"dilated_attn_{dil}",
        grid=(B, dil, nb),
        in_specs=[blk(cur(0)), blk(prev(1)), blk(cur(1)), blk(prev(0)), blk(cur(0))],
        out_specs=[pl.BlockSpec((None, NQ, W), lambda b, r, n: (b, n, r)),
                   pl.BlockSpec((None, NQ, 128), lambda b, r, n: (b, n, r))],
        out_shape=[jax.ShapeDtypeStruct((B, U, dil * W), BF16), jax.ShapeDtypeStruct((B, U, dil * 128), F32)],
        compiler_params=_cparams("parallel", "parallel", "arbitrary"),
    )(qk_view, qk_view, qk_view, v_view, v_view)
    return o.reshape(B * U, dil * W), lse.reshape(B * U, dil * 128)


def _dil_outproj_ln_kernel(o0_ref, o1_ref, o2_ref, l0_ref, l1_ref, l2_ref, hexp_ref, w_ref, x_ref, g_ref, b_ref, out_ref,
                           oscr_ref, lscr_ref, *, dils):
    tm, W = x_ref.shape

    def natural(o_ref, l_ref, dil):
        if dil == 1:
            return o_ref[...].astype(F32), l_ref[...]
        n = tm // dil
        for r in range(dil):
            lscr_ref[0, pl.ds(r, n, stride=dil), :] = l_ref[:, r * 128:(r + 1) * 128]
            for c in range(W // 128):
                oscr_ref[c, pl.ds(r, n, stride=dil), :] = o_ref[:, r * W + c * 128:r * W + (c + 1) * 128].astype(F32)
        return jnp.concatenate([oscr_ref[c] for c in range(W // 128)], axis=1), lscr_ref[0]

    os_, ls = zip(*[natural(o, l, d) for o, l, d in zip((o0_ref, o1_ref, o2_ref), (l0_ref, l1_ref, l2_ref), dils)])
    m = jnp.maximum(jnp.maximum(ls[0], ls[1]), ls[2])
    es = [jnp.exp(l - m) for l in ls]
    tot = es[0] + es[1] + es[2]
    y = jnp.zeros((tm, W), F32)
    for e, o in zip(es, os_):
        y = y + _mm01(e / tot, hexp_ref[...]) * o
    z = ALPHA * x_ref[...] + _mm(y.astype(BF16), w_ref[...])
    out_ref[...] = _ln_rows(z, g_ref[...], b_ref[...])


def _dilated_layer(x, B, S, w_in, w_out, g, b, *, tm=256):
    T, D = x.shape
    dils = tuple(d for _, d in DIL_GROUPS)
    wb = w_in.astype(BF16)
    vals = _proj_res(x, wb[:, 2 * len(dils) * D:], dils, seq=S, rope=False)
    outs = []
    for gi, dil in enumerate(dils):
        qk, = _proj_res(x, wb[:, 2 * gi * D:(2 * gi + 2) * D], (dil,), seq=S, rope=True)
        outs.append(_dil_group(qk, vals[gi], B, S, dil))
    hexp = np.zeros((128, D), np.float32)
    for h in range(DIL_HEADS):
        hexp[h, h * HEAD_DIM:(h + 1) * HEAD_DIM] = 1.0
    row = lambda i: (i, 0)
    fixed = lambda i: (0, 0)
    return pl.pallas_call(
        functools.partial(_dil_outproj_ln_kernel, dils=dils),
        name="dilated_outproj_ln",
        grid=(T // tm,),
        in_specs=[pl.BlockSpec((tm // d, d * D), row) for d in dils] + [pl.BlockSpec((tm // d, d * 128), row) for d in dils]
        + [pl.BlockSpec((128, D), fixed), pl.BlockSpec((D, D), fixed), pl.BlockSpec((tm, D), row),
           pl.BlockSpec((1, D), fixed), pl.BlockSpec((1, D), fixed)],
        out_specs=pl.BlockSpec((tm, D), row),
        out_shape=jax.ShapeDtypeStruct((T, D), F32),
        scratch_shapes=[pltpu.VMEM((D // 128, tm, 128), F32), pltpu.VMEM((1, tm, 128), F32)],
        compiler_params=_cparams("parallel"),
    )(outs[0][0], outs[1][0], outs[2][0], outs[0][1], outs[1][1], outs[2][1], jnp.asarray(hexp),
      w_out.astype(BF16), x, g.reshape(1, D), b.reshape(1, D))


def _pool_ln_kernel(x_ref, halo_ref, w_ref, scale_ref, g_ref, b_ref, o_ref, ext_ref):
    TS = x_ref.shape[0]
    s = pl.program_id(1)
    x = x_ref[...]
    ext_ref[0:POOL_HALO, :] = jnp.where(s == 0, 0.0, halo_ref[...])
    ext_ref[POOL_HALO:POOL_HALO + TS, :] = x
    cnt = (s * TS + lax.broadcasted_iota(I32, (TS, 1), 0) + 1).astype(F32)
    ys = []
    for gi, w in enumerate(POOL_WINDOWS):
        sl = slice(gi * POOL_GROUP, (gi + 1) * POOL_GROUP)
        xg = x[:, sl]
        tot = xg
        for j in range(1, w):
            tot = tot + ext_ref[POOL_HALO - j:POOL_HALO - j + TS, sl]
        mean = tot / jnp.minimum(cnt, float(w))
        ys.append(_mm((mean - xg).astype(BF16), w_ref[gi]))
    y = jnp.concatenate(ys, axis=1) * scale_ref[...]
    o_ref[...] = _ln_rows(ALPHA * x + y, g_ref[...], b_ref[...])


def _pool_layer(x, B, S, w_grp, scale, g, b, *, ts=512):
    T, D = x.shape
    ns = S // ts
    hb = ts // POOL_HALO
    fixed = lambda bb, s: (0, 0)
    return pl.pallas_call(
        _pool_ln_kernel,
        name="pool_ln",
        grid=(B, ns),
        in_specs=[pl.BlockSpec((ts, D), lambda bb, s: (bb * ns + s, 0)),
                  pl.BlockSpec((POOL_HALO, D), lambda bb, s: (jnp.maximum((bb * ns + s) * hb - 1, 0), 0)),
                  pl.BlockSpec((len(POOL_WINDOWS), POOL_GROUP, POOL_GROUP), lambda bb, s: (0, 0, 0)),
                  pl.BlockSpec((1, D), fixed), pl.BlockSpec((1, D), fixed), pl.BlockSpec((1, D), fixed)],
        out_specs=pl.BlockSpec((ts, D), lambda bb, s: (bb * ns + s, 0)),
        out_shape=jax.ShapeDtypeStruct((T, D), F32),
        scratch_shapes=[pltpu.VMEM((ts + POOL_HALO, D), F32)],
        compiler_params=_cparams("parallel", "arbitrary"),
    )(x, x, w_grp.astype(BF16), scale.reshape(1, D), g.reshape(1, D), b.reshape(1, D))


U32 = jnp.uint32
HI16 = 0xFFFF0000
ROW_WORDS = D_MODEL // 2


def _pack_halves(v):
    h = v.shape[1] // 2
    bits = lambda a: lax.bitcast_convert_type(a.astype(BF16).astype(F32), U32)
    return (bits(v[:, :h]) >> 16) | (bits(v[:, h:]) & U32(HI16))


def _unpack_halves(w):
    lo = lax.bitcast_convert_type(w << 16, F32)
    hi = lax.bitcast_convert_type(w & U32(HI16), F32)
    return jnp.concatenate([lo, hi], axis=1)


def _router_kernel(x_ref, wT_ref, bias_ref, triu_ref, eidx_ref, gw_ref, rank_ref, cnt_ref, xp_ref, carry_ref):
    E = N_EXPERTS
    per = E // N_EXPERT_GROUPS
    TM = x_ref.shape[0]

    @pl.when(pl.program_id(0) == 0)
    def _():
        carry_ref[...] = jnp.zeros_like(carry_ref)

    x = x_ref[...]
    xp_ref[...] = _pack_halves(x)

    scores = _sigmoid(_nt(wT_ref[...], x, precision=HIGHEST))
    biased = scores + bias_ref[...]
    eio = lax.broadcasted_iota(I32, (E, TM), 0)

    gio = lax.broadcasted_iota(I32, (per, TM), 0)
    gscore = []
    for gidx in range(N_EXPERT_GROUPS):
        slab = biased[gidx * per:(gidx + 1) * per, :]
        m1 = jnp.max(slab, axis=0, keepdims=True)
        i1 = jnp.min(jnp.where(slab == m1, gio, per), axis=0, keepdims=True)
        m2 = jnp.max(jnp.where(gio == i1, BELOW_NEG_INF, slab), axis=0, keepdims=True)
        gscore.append(m1 + m2)
    slabs = []
    for gidx in range(N_EXPERT_GROUPS):
        beat = jnp.zeros((1, TM), F32)
        for o in range(N_EXPERT_GROUPS):
            if o == gidx:
                continue
            wins = (gscore[o] >= gscore[gidx]) if o < gidx else (gscore[o] > gscore[gidx])
            beat = beat + jnp.where(wins, 1.0, 0.0)
        keep = beat < float(TOPK_GROUPS)
        slabs.append(jnp.where(keep, biased[gidx * per:(gidx + 1) * per, :], NEG_INF))
    cur = jnp.concatenate(slabs, axis=0)

    picked = jnp.zeros((E, TM), F32)
    idxs, vals = [], []
    for _ in range(TOP_K):
        m = jnp.max(cur, axis=0, keepdims=True)
        idx = jnp.min(jnp.where(cur == m, eio, E), axis=0, keepdims=True)
        hit = eio == idx
        picked = jnp.where(hit, 1.0, picked)
        cur = jnp.where(hit, BELOW_NEG_INF, cur)
        idxs.append(idx)
        vals.append(jnp.sum(jnp.where(hit, scores, 0.0), axis=0, keepdims=True))
    total = vals[0]
    for v in vals[1:]:
        total = total + v

    pos = _mm(picked.astype(BF16), triu_ref[...]) + carry_ref[...]
    for k in range(TOP_K):
        eidx_ref[k:k + 1, :] = idxs[k]
        gw_ref[k:k + 1, :] = vals[k] / total * ROUTED_SCALE
        rank_ref[k:k + 1, :] = jnp.sum(jnp.where(eio == idxs[k], pos, 0.0), axis=0, keepdims=True).astype(I32)
    carry_ref[...] = carry_ref[...] + jnp.sum(picked, axis=1, keepdims=True)
    cnt_ref[...] = carry_ref[...].astype(I32)


def _router(x, router_w, router_bias, *, tm=512):
    T, D = x.shape
    E = N_EXPERTS
    triu = jnp.triu(jnp.ones((tm, tm), F32), k=1).astype(BF16)
    col = lambda i: (0, i)
    fixed = lambda i: (0, 0)
    return pl.pallas_call(
        _router_kernel,
        name="moe_router",
        grid=(T // tm,),
        in_specs=[pl.BlockSpec((tm, D), lambda i: (i, 0)), pl.BlockSpec((E, D), fixed),
                  pl.BlockSpec((E, 1), fixed), pl.BlockSpec((tm, tm), fixed)],
        out_specs=[pl.BlockSpec((TOP_K, tm), col), pl.BlockSpec((TOP_K, tm), col), pl.BlockSpec((TOP_K, tm), col),
                   pl.BlockSpec((E, 1), fixed), pl.BlockSpec((tm, ROW_WORDS), lambda i: (i, 0))],
        out_shape=[jax.ShapeDtypeStruct((TOP_K, T), I32), jax.ShapeDtypeStruct((TOP_K, T), F32),
                   jax.ShapeDtypeStruct((TOP_K, T), I32), jax.ShapeDtypeStruct((E, 1), I32),
                   jax.ShapeDtypeStruct((T, ROW_WORDS), U32)],
        scratch_shapes=[pltpu.VMEM((E, 1), F32)],
        compiler_params=_cparams("arbitrary"),
    )(x, router_w.T.astype(F32), router_bias.reshape(E, 1).astype(F32), triu)


def _tile_indices(dest_hbm, idx_ref, sem_idx, n_idx):
    i = pl.program_id(0)

    def idx_copy(step):
        slot = step % 2
        return pltpu.make_async_copy(dest_hbm.at[step], idx_ref.at[pl.ds(slot * n_idx, n_idx)], sem_idx.at[slot])

    @pl.when(i == 0)
    def _():
        idx_copy(0).start()

    idx_copy(i).wait()

    @pl.when(i + 1 < pl.num_programs(0))
    def _():
        idx_copy(i + 1).start()

    return (i % 2) * n_idx


def _dispatch_kernel(cnt_ref, pstart_ref, dest_hbm, x_ref, rows_ref, idx_ref, zero_ref, sem_idx, sem_rows, sem_zero):
    TD = x_ref.shape[0]

    @pl.when(pl.program_id(0) == 0)
    def _():
        zero_ref[...] = jnp.zeros_like(zero_ref)

        def per_expert(e, c):
            n = cnt_ref[e]
            base = pstart_ref[e]
            pad_to = (n + EXP_BLK - 1) // EXP_BLK * EXP_BLK
            fill = lambda r: pltpu.make_async_copy(zero_ref.at[pl.ds(0, 1)], rows_ref.at[base + r], sem_zero)

            def start(r, c2):
                fill(r).start()
                return c2

            def wait(r, c2):
                fill(r).wait()
                return c2

            lax.fori_loop(n, pad_to, start, 0)
            lax.fori_loop(n, pad_to, wait, 0)
            return c

        lax.fori_loop(0, N_EXPERTS, per_expert, 0)

    base = _tile_indices(dest_hbm, idx_ref, sem_idx, TOP_K * TD)
    scatter = lambda t, k: pltpu.make_async_copy(x_ref.at[pl.ds(t, 1)], rows_ref.at[idx_ref[base + k * TD + t]], sem_rows)

    def issue(t, c):
        for k in range(TOP_K):
            scatter(t, k).start(priority=k % 2)
        return c

    def drain(t, c):
        for k in range(TOP_K):
            scatter(t, k).wait()
        return c

    lax.fori_loop(0, TD, issue, 0, unroll=8)
    lax.fori_loop(0, TD, drain, 0, unroll=8)


def _dispatch(xp, dest_tiles, counts, pad_start, n_rows, *, td):
    T = xp.shape[0]
    return pl.pallas_call(
        _dispatch_kernel,
        name="moe_dispatch",
        grid_spec=pltpu.PrefetchScalarGridSpec(
            num_scalar_prefetch=2,
            grid=(T // td,),
            in_specs=[pl.BlockSpec(memory_space=pl.ANY), pl.BlockSpec((td, ROW_WORDS), lambda i, c, p: (i, 0))],
            out_specs=pl.BlockSpec(memory_space=pl.ANY),
            scratch_shapes=[pltpu.SMEM((2 * TOP_K * td,), I32), pltpu.VMEM((8, ROW_WORDS), U32),
                            pltpu.SemaphoreType.DMA((2,)), pltpu.SemaphoreType.DMA(()), pltpu.SemaphoreType.DMA(())],
        ),
        out_shape=jax.ShapeDtypeStruct((n_rows, 1, ROW_WORDS), U32),
        compiler_params=_cparams("arbitrary"),
    )(counts, pad_start, dest_tiles, xp)


def _expert_kernel(be_ref, nu_ref, rows_hbm, wg_ref, wu_ref, wd_ref, out_hbm, xbuf, obuf, wgu_bf, wd_bf, sem_in, sem_out):
    i = pl.program_id(0)
    nu = nu_ref[0]

    @pl.when((i < nu) & ((i == 0) | (be_ref[i] != be_ref[jnp.maximum(i - 1, 0)])))
    def _():
        wgu_bf[:, :EXPERT_FF] = wg_ref[0].astype(BF16)
        wgu_bf[:, EXPERT_FF:] = wu_ref[0].astype(BF16)
        wd_bf[...] = wd_ref[0].astype(BF16)

    blk = lambda step: pl.ds(pl.multiple_of(step * EXP_BLK, EXP_BLK), EXP_BLK)
    in_copy = lambda step: pltpu.make_async_copy(rows_hbm.at[blk(step), 0], xbuf.at[step % 2], sem_in.at[step % 2])
    out_copy = lambda step: pltpu.make_async_copy(obuf.at[step % 2], out_hbm.at[blk(step), 0], sem_out.at[step % 2])

    @pl.when(i == 0)
    def _():
        in_copy(0).start()

    @pl.when(i < nu)
    def _():
        in_copy(i).wait()

        @pl.when(i + 1 < nu)
        def _():
            in_copy(i + 1).start()

        @pl.when(i >= 2)
        def _():
            out_copy(i - 2).wait()

        slot = i % 2
        gu = _mm(_unpack_halves(xbuf[slot]).astype(BF16), wgu_bf[...])
        h = _silu(gu[:, :EXPERT_FF]) * gu[:, EXPERT_FF:]
        obuf[slot] = _pack_halves(_mm(h.astype(BF16), wd_bf[...]))
        out_copy(i).start()

    @pl.when(i == pl.num_programs(0) - 1)
    def _():
        @pl.when(nu >= 2)
        def _():
            out_copy(nu - 2).wait()

        out_copy(nu - 1).wait()


def _experts(rows, blk_expert, n_used, w_gate, w_up, w_down, layer):
    R = rows.shape[0]
    D = w_down.shape[-1]
    n_blk = R // EXP_BLK
    expert = lambda i, be, nu: (layer, be[jnp.minimum(i, nu[0] - 1)], 0, 0)
    return pl.pallas_call(
        _expert_kernel,
        name="moe_experts",
        grid_spec=pltpu.PrefetchScalarGridSpec(
            num_scalar_prefetch=2,
            grid=(n_blk,),
            in_specs=[pl.BlockSpec(memory_space=pl.ANY), pl.BlockSpec((None, 1, D, EXPERT_FF), expert),
                      pl.BlockSpec((None, 1, D, EXPERT_FF), expert), pl.BlockSpec((None, 1, EXPERT_FF, D), expert)],
            out_specs=pl.BlockSpec(memory_space=pl.ANY),
            scratch_shapes=[pltpu.VMEM((2, EXP_BLK, ROW_WORDS), U32), pltpu.VMEM((2, EXP_BLK, ROW_WORDS), U32),
                            pltpu.VMEM((D, 2 * EXPERT_FF), BF16), pltpu.VMEM((EXPERT_FF, D), BF16),
                            pltpu.SemaphoreType.DMA((2,)), pltpu.SemaphoreType.DMA((2,))],
        ),
        out_shape=jax.ShapeDtypeStruct((R, 1, ROW_WORDS), U32),
        compiler_params=_cparams("arbitrary"),
    )(blk_expert, n_used, rows, w_gate, w_up, w_down)


def _combine_kernel(dest_hbm, gw_ref, x_ref, rows_ref, wsgu_ref, wsd_ref, g_ref, b_ref, p_ref, wp_ref, wpg_ref,
                    o_ref, idx_ref, buf_ref, sem_idx, sem_rows):
    TM = x_ref.shape[0]
    n_idx = TOP_K * TM
    i = pl.program_id(0)
    n = pl.num_programs(0)

    def idx_copy(step):
        slot = step % 3
        return pltpu.make_async_copy(dest_hbm.at[step], idx_ref.at[pl.ds(slot * n_idx, n_idx)], sem_idx.at[slot])

    def row_loop(step, slot, start):
        base = (step % 3) * n_idx

        def body(t, c):
            for k in range(TOP_K):
                cp = pltpu.make_async_copy(rows_ref.at[idx_ref[base + k * TM + t]], buf_ref.at[slot, k, pl.ds(t, 1)],
                                           sem_rows.at[slot])
                if start:
                    cp.start(priority=k % 2)
                else:
                    cp.wait()
            return c

        lax.fori_loop(0, TM, body, 0, unroll=8)

    def for_parity(step, fn):
        for slot in range(2):
            pl.when(step % 2 == slot)(functools.partial(fn, slot))

    @pl.when(i == 0)
    def _():
        idx_copy(0).start()
        idx_copy(0).wait()
        row_loop(0, 0, True)

        @pl.when(n > 1)
        def _():
            idx_copy(1).start()

    @pl.when(i + 1 < n)
    def _():
        idx_copy(i + 1).wait()

        @pl.when(i + 2 < n)
        def _():
            idx_copy(i + 2).start()

        for_parity(i + 1, lambda slot: row_loop(i + 1, slot, True))

    x = x_ref[...]
    gu = _mm(x.astype(BF16), wsgu_ref[...])
    ff = gu.shape[1] // 2
    f = _mm((_silu(gu[:, :ff]) * gu[:, ff:]).astype(BF16), wsd_ref[...])
    for_parity(i, lambda slot: row_loop(i, slot, False))
    gw = gw_ref[...]
    cur = i % 2
    for k in range(TOP_K):
        f = f + gw[:, k:k + 1] * _unpack_halves(buf_ref[cur, k])
    x2 = _ln_rows(ALPHA * x + f, g_ref[...], b_ref[...])
    gate = _sigmoid(_mm(x2.astype(BF16), wpg_ref[...]))
    o_ref[...] = x2 + gate * _mm(p_ref[...].astype(BF16), wp_ref[...])


def _combine(dest_tiles, gw, x, rows_out, wsgu, wsd, g, b, p, wp, wpg, *, tm):
    T, D = x.shape
    row = lambda i: (i, 0)
    fixed = lambda i: (0, 0)
    full = lambda a: pl.BlockSpec(a.shape, fixed)
    return pl.pallas_call(
        _combine_kernel,
        name="moe_combine",
        grid=(T // tm,),
        in_specs=[pl.BlockSpec(memory_space=pl.ANY), pl.BlockSpec((tm, TOP_K), row), pl.BlockSpec((tm, D), row),
                  pl.BlockSpec(memory_space=pl.ANY), full(wsgu), full(wsd), pl.BlockSpec((1, D), fixed),
                  pl.BlockSpec((1, D), fixed), pl.BlockSpec((tm, PLE_DIM), row), full(wp), full(wpg)],
        out_specs=pl.BlockSpec((tm, D), row),
        out_shape=jax.ShapeDtypeStruct((T, D), F32),
        scratch_shapes=[pltpu.SMEM((3 * TOP_K * tm,), I32), pltpu.VMEM((2, TOP_K, tm, ROW_WORDS), U32),
                        pltpu.SemaphoreType.DMA((3,)), pltpu.SemaphoreType.DMA((2,))],
        compiler_params=_cparams("arbitrary"),
    )(dest_tiles, gw, x, rows_out, wsgu, wsd, g.reshape(1, D), b.reshape(1, D), p, wp, wpg)


MOE_TILE = 256


def _moe_ple_layer(x, p, layer, router_w, router_bias, w_gate, w_up, w_down, ws_gate, ws_up, ws_down, g, b, ple_w, ple_gate_w):
    T, D = x.shape
    eidx, gw, rank, counts, xp = _router(x, router_w, router_bias)
    counts = counts.reshape(N_EXPERTS)
    padded = (counts + EXP_BLK - 1) // EXP_BLK * EXP_BLK
    pad_end = jnp.cumsum(padded)
    pad_start = pad_end - padded
    n_blk = T * TOP_K // EXP_BLK + N_EXPERTS
    e_iota = jnp.arange(N_EXPERTS, dtype=I32)
    dest = rank + jnp.sum(jnp.where(eidx[..., None] == e_iota, pad_start, 0), axis=-1)
    tm = MOE_TILE
    dest_tiles = dest.reshape(TOP_K, T // tm, tm).transpose(1, 0, 2).reshape(T // tm, TOP_K * tm)
    blk_first = jnp.arange(n_blk, dtype=I32)[:, None] * EXP_BLK
    blk_expert = jnp.minimum(jnp.sum((pad_end[None, :] <= blk_first).astype(I32), axis=-1), N_EXPERTS - 1)
    n_used = (pad_end[-1:] // EXP_BLK).astype(I32)

    rows = _dispatch(xp, dest_tiles, counts.astype(I32), pad_start.astype(I32), n_blk * EXP_BLK, td=tm)
    rows_out = _experts(rows, blk_expert, n_used, w_gate, w_up, w_down, layer)
    wsgu = jnp.concatenate([ws_gate, ws_up], axis=-1).astype(BF16)
    return _combine(dest_tiles, gw.T, x, rows_out, wsgu, ws_down.astype(BF16), g, b, p,
                    ple_w.astype(BF16), ple_gate_w.astype(BF16), tm=tm)


def kernel(x, p, ln_g, ln_b, mlstm_w_in, mlstm_conv, mlstm_ig_bias, mlstm_fg_bias, mlstm_norm_g, mlstm_w_out, nsa_w_in, nsa_cmp_pos_k, nsa_cmp_pos_v, nsa_cmp_wk, nsa_cmp_wv, nsa_w_out, dil_w_in, dil_w_out, pool_w, pool_scale, router_w, router_bias, exp_w_gate, exp_w_up, exp_w_down, sh_w_gate, sh_w_up, sh_w_down, ple_w, ple_gate_w):
    B, S, D = x.shape
    T = B * S
    xf = x.reshape(T, D)
    pf = p.reshape(DEPTH, T, PLE_DIM)
    for i in range(DEPTH):
        kind, j = i % 4, i // 4
        g1, b1 = ln_g[i, 0], ln_b[i, 0]
        if kind == 0:
            y = _mlstm_mixer(xf, B, S, mlstm_w_in[j], mlstm_conv[j], mlstm_ig_bias[j], mlstm_fg_bias[j], mlstm_norm_g[j])
            xf = _outproj_ln(y, mlstm_w_out[j].astype(BF16), xf, g1, b1)
        elif kind == 1:
            y = _nsa_mixer(xf, B, S, nsa_w_in[j], nsa_cmp_pos_k[j], nsa_cmp_pos_v[j], nsa_cmp_wk[j], nsa_cmp_wv[j])
            xf = _outproj_ln(y, nsa_w_out[j].astype(BF16), xf, g1, b1)
        elif kind == 2:
            xf = _dilated_layer(xf, B, S, dil_w_in[j], dil_w_out[j], g1, b1)
        else:
            xf = _pool_layer(xf, B, S, pool_w[j], pool_scale[j], g1, b1)
        xf = _moe_ple_layer(xf, pf[i], i, router_w[i], router_bias[i], exp_w_gate, exp_w_up, exp_w_down,
                            sh_w_gate[i], sh_w_up[i], sh_w_down[i], ln_g[i, 1], ln_b[i, 1], ple_w[i], ple_gate_w[i])
    return xf.reshape(B, S, D)
```

```python
import functools

import numpy as np
import jax
import jax.numpy as jnp
from jax import lax
from jax.experimental import pallas as pl
from jax.experimental.pallas import tpu as pltpu

F32 = jnp.float32
BF16 = jnp.bfloat16
I32 = jnp.int32
HIGHEST = lax.Precision.HIGHEST

D_MODEL = 1024
DEPTH = 4
ALPHA = (2.0 * DEPTH) ** 0.25
LN_EPS = 1e-5
NEG_INF = -1e30
TINY = 1e-30
BELOW_NEG_INF = -3e38
NEG_FLOOR = 0.999 * NEG_INF
ROPE_THETA = 10000.0
HEAD_DIM = 64
ROPE_HALF = HEAD_DIM // 2

MLSTM_HEADS = 8
MLSTM_QK_DIM = 64
MLSTM_V_DIM = 128
MLSTM_CONV = 4
MLSTM_L = 256

NSA_HEADS = 16
NSA_KV_HEADS = 4
NSA_J = NSA_HEADS // NSA_KV_HEADS
CMP_STRIDE = 16
CMP_BLK = 32
SEL_BLK = 64
N_SEL = 16
NSA_WINDOW = 512
FORCE_SCORE = 1e4
NSA_TQ = 256
NSA_TQ_SELECT = 512
NSA_TK = 512

DIL_HEADS = 16
DIL_GROUPS = ((128, 1), (512, 4), (2048, 16))
DIL_STEPS = 128

POOL_WINDOWS = (2, 4, 8, 16)
POOL_GROUP = 256
POOL_HALO = 16

N_EXPERTS = 64
TOP_K = 8
N_EXPERT_GROUPS = 8
TOPK_GROUPS = 4
EXPERT_FF = 256
ROUTED_SCALE = 2.5
EXP_BLK = 512
PLE_DIM = 256

VMEM_LIMIT = 48 * 1024 * 1024


def _cparams(*sem):
    return pltpu.CompilerParams(dimension_semantics=sem, vmem_limit_bytes=VMEM_LIMIT)


def _nt(a, b, **kw):
    return lax.dot_general(a, b, (((1,), (1,)), ((), ())), preferred_element_type=F32, **kw)


def _tn(a, b, **kw):
    return lax.dot_general(a, b, (((0,), (0,)), ((), ())), preferred_element_type=F32, **kw)


def _mm(a, b, **kw):
    return jnp.dot(a, b, preferred_element_type=F32, **kw)


def _split3(x):
    a = x.astype(BF16)
    r = x - a.astype(F32)
    b = r.astype(BF16)
    return a, b, (r - b.astype(F32)).astype(BF16)


def _mm01(x, e01):
    eb = e01.astype(BF16)
    a, b, c = _split3(x)
    return _mm(a, eb) + _mm(b, eb) + _mm(c, eb)


def _sigmoid(z):
    return 1.0 / (1.0 + jnp.exp(-z))


def _silu(z):
    return z * _sigmoid(z)


def _ln_rows(z, g, b):
    mu = jnp.mean(z, axis=-1, keepdims=True)
    d = z - mu
    var = jnp.mean(d * d, axis=-1, keepdims=True)
    return d * lax.rsqrt(var + LN_EPS) * g + b


def _proj_kernel(*refs, rope):
    if rope:
        x_ref, w_ref, cos_ref, sin_ref, o_ref, xb_ref = refs
    else:
        x_ref, w_ref, o_ref, xb_ref = refs

    @pl.when(pl.program_id(1) == 0)
    def _():
        xb_ref[...] = x_ref[...].astype(BF16)

    acc = _mm(xb_ref[...], w_ref[...])
    if rope:
        tn = acc.shape[1]
        lane = lax.broadcasted_iota(I32, acc.shape, 1)
        lo = (lane % HEAD_DIM) < ROPE_HALF
        rot = jnp.where(lo, pltpu.roll(acc, tn - ROPE_HALF, 1), pltpu.roll(acc, ROPE_HALF, 1))
        acc = acc * cos_ref[...] + rot * sin_ref[...]
    o_ref[...] = acc.astype(o_ref.dtype)


def _proj(x, w, *, out_dtype, tm, tn, rope_tables=None, seq=None):
    T, K = x.shape
    N = w.shape[1]
    assert T % tm == 0 and N % tn == 0
    in_specs = [pl.BlockSpec((tm, K), lambda i, j: (i, 0)), pl.BlockSpec((K, tn), lambda i, j: (0, j))]
    args = [x, w]
    if rope_tables is not None:
        nseq = seq // tm
        in_specs += [pl.BlockSpec((tm, tn), lambda i, j: (i % nseq, 0))] * 2
        args += list(rope_tables)
    return pl.pallas_call(
        functools.partial(_proj_kernel, rope=rope_tables is not None),
        name="proj_rope" if rope_tables is not None else "proj",
        grid=(T // tm, N // tn),
        in_specs=in_specs,
        out_specs=pl.BlockSpec((tm, tn), lambda i, j: (i, j)),
        out_shape=jax.ShapeDtypeStruct((T, N), out_dtype),
        scratch_shapes=[pltpu.VMEM((tm, K), BF16)],
        compiler_params=_cparams("parallel", "arbitrary"),
    )(*args)


def _rope_tables(pos, width):
    inv = ROPE_THETA ** (-jnp.arange(ROPE_HALF, dtype=F32) / ROPE_HALF)
    ang = pos.astype(F32)[:, None] * inv[None, :]
    cos, sin = jnp.cos(ang), jnp.sin(ang)
    cos64 = jnp.concatenate([cos, cos], -1)
    sin64 = jnp.concatenate([-sin, sin], -1)
    rep = width // HEAD_DIM
    return jnp.tile(cos64, (1, rep)), jnp.tile(sin64, (1, rep))


def _outproj_ln_kernel(y_ref, w_ref, x_ref, g_ref, b_ref, o_ref):
    y = _mm(y_ref[...].astype(BF16), w_ref[...])
    o_ref[...] = _ln_rows(ALPHA * x_ref[...] + y, g_ref[...], b_ref[...])


def _outproj_ln(y, w, x, g, b, *, tm=512):
    T, D = x.shape
    K = y.shape[1]
    row = lambda i: (i, 0)
    fixed = lambda i: (0, 0)
    return pl.pallas_call(
        _outproj_ln_kernel,
        name="outproj_ln",
        grid=(T // tm,),
        in_specs=[pl.BlockSpec((tm, K), row), pl.BlockSpec((K, D), fixed), pl.BlockSpec((tm, D), row),
                  pl.BlockSpec((1, D), fixed), pl.BlockSpec((1, D), fixed)],
        out_specs=pl.BlockSpec((tm, D), row),
        out_shape=jax.ShapeDtypeStruct((T, D), F32),
        compiler_params=_cparams("parallel"),
    )(y, w, x, g.reshape(1, D), b.reshape(1, D))


def _log_sigmoid(z):
    return jnp.minimum(z, 0.0) - jnp.log(1.0 + jnp.exp(-jnp.abs(z)))


def _mlstm_kernel(qk_ref, v_ref, o_ref, gc_ref, gr_ref, convw_ref, bias_c_ref, bias_r_ref, ng_ref,
                  tri_ref, triT_ref, out_ref, C_ref, n_ref, m_ref, ext_ref):
    L = qk_ref.shape[0]
    H, dk, dv = MLSTM_HEADS, MLSTM_QK_DIM, MLSTM_V_DIM

    @pl.when(pl.program_id(1) == 0)
    def _():
        C_ref[...] = jnp.zeros_like(C_ref)
        n_ref[...] = jnp.zeros_like(n_ref)
        m_ref[...] = jnp.zeros_like(m_ref)
        ext_ref[0:8, :] = jnp.zeros((8, ext_ref.shape[1]), F32)

    cur = qk_ref[...]
    ext_ref[8:8 + L, :] = cur
    acc = convw_ref[3:4, :] * cur
    for j in range(MLSTM_CONV - 1):
        acc = acc + convw_ref[j:j + 1, :] * ext_ref[5 + j:5 + j + L, :]
    ext_ref[0:8, :] = cur[L - 8:L, :]
    qk = _silu(acc)

    gc = gc_ref[...] + bias_c_ref[...]
    gr = gr_ref[...] + bias_r_ref[...]
    tri01 = tri_ref[...].astype(BF16)
    b_col = sum(_mm(tri01, t) for t in _split3(_log_sigmoid(gc)))
    b_row = _mm01(_log_sigmoid(gr[H:2 * H, :]), triT_ref[...])
    ig_row = gr[0:H, :]
    tri = lax.broadcasted_iota(I32, (L, L), 0) >= lax.broadcasted_iota(I32, (L, L), 1)

    for h in range(H):
        q = (qk[:, h * dk:(h + 1) * dk] * dk ** -0.5).astype(BF16)
        k = qk[:, H * dk + h * dk:H * dk + (h + 1) * dk]
        kb = k.astype(BF16)
        v = v_ref[:, h * dv:(h + 1) * dv].astype(BF16)
        b_c = b_col[:, H + h:H + h + 1]
        ig_c = gc[:, h:h + 1]
        b_r = b_row[h:h + 1, :]
        m_prev = m_ref[h:h + 1, 0:1]
        C = C_ref[h]
        n = n_ref[h:h + 1, :]

        logD = jnp.where(tri, b_c - b_r + ig_row[h:h + 1, :], NEG_INF)
        inter = b_c + m_prev
        m_t = jnp.maximum(inter, jnp.max(logD, axis=-1, keepdims=True))
        s = _nt(q, kb) * jnp.exp(logD - m_t)
        w_inter = jnp.exp(inter - m_t)
        num = w_inter * _mm(q, C.astype(BF16)) + _mm(s.astype(BF16), v)
        den = w_inter * jnp.sum(q.astype(F32) * n, axis=-1, keepdims=True) + jnp.sum(s, axis=-1, keepdims=True)
        hv = num / jnp.maximum(jnp.abs(den), jnp.exp(-m_t))

        mu = jnp.mean(hv, axis=-1, keepdims=True)
        d = hv - mu
        hn = d * lax.rsqrt(jnp.mean(d * d, axis=-1, keepdims=True) + LN_EPS)
        og = _sigmoid(o_ref[:, h * dv:(h + 1) * dv])
        out_ref[:, h * dv:(h + 1) * dv] = (hn * ng_ref[:, h * dv:(h + 1) * dv] * og).astype(out_ref.dtype)

        bL = b_c[L - 1:L, :]
        logw = bL - b_c + ig_c
        m_new = jnp.maximum(bL + m_prev, jnp.max(logw, axis=0, keepdims=True))
        decay = jnp.exp(bL + m_prev - m_new)
        kw = k * jnp.exp(logw - m_new)
        C_ref[h] = decay * C + _tn(kw.astype(BF16), v)
        n_ref[h:h + 1, :] = decay * n + jnp.sum(kw, axis=0, keepdims=True)
        m_ref[h:h + 1, :] = jnp.broadcast_to(m_new, (1, m_ref.shape[1]))


def _mlstm_mixer(x, B, S, w_in, conv_w, ig_bias, fg_bias, norm_g):
    T = B * S
    H, L = MLSTM_HEADS, MLSTM_L
    wb = w_in.astype(BF16)
    main = _proj(x, wb[:, :3 * D_MODEL], out_dtype=F32, tm=512, tn=1024)
    wg = jnp.pad(wb[:, 3 * D_MODEL:], ((0, 0), (0, 128 - 2 * H)))
    gates = _proj(x, wg, out_dtype=F32, tm=512, tn=128)
    gates_r = gates[:, :2 * H].reshape(B, S, 2 * H).transpose(0, 2, 1)
    bias16 = jnp.concatenate([ig_bias, fg_bias]).astype(F32)
    bias_c = jnp.pad(bias16, (0, 128 - 2 * H)).reshape(1, 128)
    bias_r = bias16.reshape(2 * H, 1)
    tri = jnp.tril(jnp.ones((L, L), F32))
    nc = S // L
    rowblk = lambda c: (lambda b, i: (b * nc + i, c))
    fixed = lambda b, i: (0, 0)
    return pl.pallas_call(
        _mlstm_kernel,
        name="mlstm",
        grid=(B, nc),
        in_specs=[pl.BlockSpec((L, D_MODEL), rowblk(0)), pl.BlockSpec((L, D_MODEL), rowblk(1)),
                  pl.BlockSpec((L, D_MODEL), rowblk(2)), pl.BlockSpec((L, 128), rowblk(0)),
                  pl.BlockSpec((None, 2 * H, L), lambda b, i: (b, 0, i)),
                  pl.BlockSpec((MLSTM_CONV, D_MODEL), fixed), pl.BlockSpec((1, 128), fixed),
                  pl.BlockSpec((2 * H, 1), fixed), pl.BlockSpec((1, D_MODEL), fixed),
                  pl.BlockSpec((L, L), fixed), pl.BlockSpec((L, L), fixed)],
        out_specs=pl.BlockSpec((L, D_MODEL), rowblk(0)),
        out_shape=jax.ShapeDtypeStruct((T, D_MODEL), BF16),
        scratch_shapes=[pltpu.VMEM((H, MLSTM_QK_DIM, MLSTM_V_DIM), F32), pltpu.VMEM((H, MLSTM_QK_DIM), F32),
                        pltpu.VMEM((H, 128), F32), pltpu.VMEM((L + 8, D_MODEL), F32)],
        compiler_params=_cparams("parallel", "arbitrary"),
    )(main, main, main, gates, gates_r, conv_w.astype(F32), bias_c, bias_r, norm_g.reshape(1, D_MODEL).astype(F32),
      tri, tri.T)


def _rot_cols(w):
    shp = w.shape
    w4 = w.reshape(shp[:-1] + (shp[-1] // HEAD_DIM, 2, ROPE_HALF))
    return jnp.flip(w4, axis=-2).reshape(shp)


def _nsa_compress_kernel(chk_ref, chv_ref, wk_lo_ref, wk_hi_ref, wv_lo_ref, wv_hi_ref, pk_ref, pv_ref, wk_ref, wv_ref,
                         cos_ref, sin_ref, kc_ref, vc_ref, sh_ref):
    NCH = chk_ref.shape[0]
    G = NSA_KV_HEADS
    sh_ref[NCH:NCH + 8, :] = jnp.zeros((8, sh_ref.shape[1]), F32)
    live = lax.broadcasted_iota(I32, (NCH, 1), 0) < NCH - 1

    def blocks(ch_ref, lo_ref, hi_ref, p_ref, w_ref):
        ch = ch_ref[...]
        n = lo_ref.shape[1]
        sh_ref[0:NCH, 0:n] = _mm(ch, hi_ref[...])
        const = _mm(p_ref[...].astype(BF16), w_ref[...])[0:1, :]
        return jnp.where(live, _mm(ch, lo_ref[...]) + sh_ref[1:NCH + 1, 0:n] + jnp.concatenate([const] * G, axis=1), 0.0)

    k2 = blocks(chk_ref, wk_lo_ref, wk_hi_ref, pk_ref, wk_ref)
    v2 = blocks(chv_ref, wv_lo_ref, wv_hi_ref, pv_ref, wv_ref)
    for g in range(G):
        raw = k2[:, 2 * g * HEAD_DIM:(2 * g + 1) * HEAD_DIM]
        rot = k2[:, (2 * g + 1) * HEAD_DIM:(2 * g + 2) * HEAD_DIM]
        kc_ref[g] = (raw * cos_ref[...] + rot * sin_ref[...]).astype(kc_ref.dtype)
        vc_ref[g] = v2[:, g * HEAD_DIM:(g + 1) * HEAD_DIM].astype(vc_ref.dtype)


def _nsa_cmp_select_kernel(q_ref, kc_ref, vc_ref, ovlT_ref, ocmp_ref, sel_ref):
    TQ = q_ref.shape[0]
    NCH = kc_ref.shape[0]
    NSB = ovlT_ref.shape[0]
    q0 = pl.program_id(2) * TQ
    t = q0 + lax.broadcasted_iota(I32, (TQ, 1), 0)
    cend = lax.broadcasted_iota(I32, (1, NCH), 1) * CMP_STRIDE + (CMP_BLK - 1)
    bias = jnp.where(cend <= t, 0.0, NEG_INF)
    kc = kc_ref[...]
    vc = vc_ref[...]
    psum = jnp.zeros((TQ, NCH), F32)
    outs = []
    for j in range(NSA_J):
        qj = q_ref[:, j * HEAD_DIM:(j + 1) * HEAD_DIM]
        s = _nt(qj, kc) * HEAD_DIM ** -0.5 + bias
        m = jnp.maximum(jnp.max(s, axis=-1, keepdims=True), NEG_FLOOR)
        e = jnp.exp(s - m)
        p = e / jnp.maximum(jnp.sum(e, axis=-1, keepdims=True), TINY)
        outs.append(_mm(p.astype(BF16), vc))
        psum = psum + p
    ocmp_ref[...] = jnp.concatenate(outs, axis=1).astype(ocmp_ref.dtype)

    ovl01 = ovlT_ref[...].astype(BF16)
    imp = sum(_nt(ovl01, t) for t in _split3(psum))
    nb = lax.broadcasted_iota(I32, (NSB, 1), 0)
    qblk = (q0 + lax.broadcasted_iota(I32, (1, TQ), 1)) // SEL_BLK
    forced = (nb == 0) | (nb == qblk) | (nb == qblk - 1)
    cur = jnp.where(forced, FORCE_SCORE, jnp.where(nb > qblk, NEG_INF, imp))
    sel = jnp.zeros((NSB, TQ), F32)
    for _ in range(min(N_SEL, NSB)):
        m = jnp.max(cur, axis=0, keepdims=True)
        idx = jnp.min(jnp.where(cur == m, nb, NSB), axis=0, keepdims=True)
        hit = nb == idx
        sel = jnp.where(hit, 1.0, sel)
        cur = jnp.where(hit, BELOW_NEG_INF, cur)
    sel_ref[...] = sel.astype(sel_ref.dtype)


def _nsa_main_kernel(q_ref, ksT_ref, vs_ref, kwT_ref, vw_ref, sel_ref, ocmp_ref, gate_ref, gexp_ref, o_ref):
    TQ = q_ref.shape[0]
    S = vs_ref.shape[0]
    TK = min(NSA_TK, S)
    J = NSA_J
    q0 = pl.program_id(2) * TQ
    q4 = jnp.concatenate([q_ref[:, j * HEAD_DIM:(j + 1) * HEAD_DIM] for j in range(J)], axis=0) * HEAD_DIM ** -0.5
    t = q0 + lax.broadcasted_iota(I32, (TQ, 1), 0)
    selT = sel_ref[...]

    def attend(carry, kT, v1, bias):
        m, acc = carry
        s = _mm(q4, kT) + jnp.concatenate([bias] * J, axis=0)
        m_new = jnp.maximum(m, jnp.max(s, axis=-1, keepdims=True))
        e = jnp.exp(s - m_new)
        return m_new, jnp.exp(m - m_new) * acc + _mm(e.astype(BF16), v1)

    def finish(acc):
        return acc[:, :HEAD_DIM] / jnp.maximum(acc[:, HEAD_DIM:], TINY)

    init = (jnp.full((J * TQ, 1), NEG_FLOOR, F32), jnp.zeros((J * TQ, 2 * HEAD_DIM), F32))

    def body(c, carry):
        k0 = pl.multiple_of(c * TK, TK)
        kpos = k0 + lax.broadcasted_iota(I32, (1, TK), 1)
        expand = jnp.where(kpos // SEL_BLK == lax.broadcasted_iota(I32, (selT.shape[0], 1), 0), 1.0, 0.0).astype(BF16)
        bias = jnp.where((_tn(selT, expand) > 0.5) & (kpos <= t), 0.0, NEG_INF)
        return attend(carry, ksT_ref[:, pl.ds(k0, TK)], vs_ref[pl.ds(k0, TK), :], bias)

    nchunks = (q0 + TQ - 1) // TK + 1
    o_slc = finish(lax.fori_loop(0, nchunks, body, init)[1])

    span = min(NSA_WINDOW + TQ, S)
    start = pl.multiple_of(jnp.clip(q0 - NSA_WINDOW, 0, S - span), TQ)
    dist = t - (start + lax.broadcasted_iota(I32, (1, span), 1))
    bias = jnp.where((dist >= 0) & (dist < NSA_WINDOW), 0.0, NEG_INF)
    o_win = finish(attend(init, kwT_ref[:, pl.ds(start, span)], vw_ref[pl.ds(start, span), :], bias)[1])

    unstack = lambda o: jnp.concatenate([o[j * TQ:(j + 1) * TQ, :] for j in range(J)], axis=1)
    g = _sigmoid(gate_ref[...])
    gx = [_mm01(g, gexp_ref[c]) for c in range(3)]
    out = gx[0] * ocmp_ref[...].astype(F32) + gx[1] * unstack(o_slc) + gx[2] * unstack(o_win)
    o_ref[...] = out.astype(o_ref.dtype)


def _nsa_mixer(x, B, S, w_in, cmp_pos_k, cmp_pos_v, cmp_wk, cmp_wv):
    T = B * S
    H, G, J, dh = NSA_HEADS, NSA_KV_HEADS, NSA_J, HEAD_DIM
    kv = G * dh
    wb = w_in.astype(BF16)
    col = lambda a, n: wb[:, a:a + n]
    o_q, o_kc, o_vc, o_ks, o_vs, o_kw, o_vw, o_g = np.cumsum([0, H * dh] + [kv] * 6).tolist()
    pos = jnp.arange(S)
    w_rope = jnp.concatenate([col(o_q, H * dh), col(o_ks, kv), col(o_kw, kv)], axis=1)
    roped = _proj(x, w_rope, out_dtype=BF16, tm=512, tn=512, rope_tables=_rope_tables(pos, 512), seq=S)
    w_plain = jnp.concatenate([col(o_vs, kv), col(o_vw, kv), jnp.pad(col(o_g, 3 * H), ((0, 0), (0, 128 - 3 * H)))], axis=1)
    plain = _proj(x, w_plain, out_dtype=F32, tm=512, tn=w_plain.shape[1])

    heads_T = lambda a: a.reshape(B, S, G, dh).transpose(0, 2, 3, 1)
    heads = lambda a: a.reshape(B, S, G, dh).transpose(0, 2, 1, 3)
    ksT = heads_T(roped[:, H * dh:H * dh + kv])
    kwT = heads_T(roped[:, H * dh + kv:])
    with_ones = lambda v: jnp.concatenate([v, jnp.ones_like(v)], axis=-1)
    vs = with_ones(heads(plain[:, :kv]).astype(BF16))
    vw = with_ones(heads(plain[:, kv:2 * kv]).astype(BF16))
    gates = plain[:, 2 * kv:]

    nch = S // CMP_STRIDE
    cw = CMP_STRIDE * kv
    chk = _proj(x, col(o_kc, kv), out_dtype=BF16, tm=512, tn=kv).reshape(B, nch, cw)
    chv = _proj(x, col(o_vc, kv), out_dtype=BF16, tm=512, tn=kv).reshape(B, nch, cw)
    wk = cmp_wk.reshape(CMP_BLK * dh, dh)
    wk2 = jnp.concatenate([wk, _rot_cols(wk)], axis=1).astype(BF16)
    wv2 = cmp_wv.reshape(CMP_BLK * dh, dh).astype(BF16)
    eye = jnp.eye(G, dtype=BF16)

    def per_head(w, part):
        n = w.shape[1]
        w3 = w.reshape(2, CMP_STRIDE, dh, n)[part]
        return jnp.einsum('lde,gh->lgdhe', w3, eye).reshape(cw, G * n)

    flat8 = lambda p_: jnp.pad(p_.reshape(1, CMP_BLK * dh), ((0, 7), (0, 0))).astype(F32)
    cend = jnp.arange(nch) * CMP_STRIDE + CMP_BLK - 1
    cos_c, sin_c = _rope_tables(cend, dh)
    fixed1 = lambda b: (0, 0)
    full1 = lambda a: pl.BlockSpec(a.shape, fixed1)
    consts = (per_head(wk2, 0), per_head(wk2, 1), per_head(wv2, 0), per_head(wv2, 1), flat8(cmp_pos_k), flat8(cmp_pos_v),
              wk2, wv2, cos_c, sin_c)
    kc, vc = pl.pallas_call(
        _nsa_compress_kernel,
        name="nsa_compress",
        grid=(B,),
        in_specs=[pl.BlockSpec((None, nch, cw), lambda b: (b, 0, 0))] * 2 + [full1(c) for c in consts],
        out_specs=[pl.BlockSpec((None, G, nch, dh), lambda b: (b, 0, 0, 0))] * 2,
        out_shape=[jax.ShapeDtypeStruct((B, G, nch, dh), BF16)] * 2,
        scratch_shapes=[pltpu.VMEM((nch + 8, 2 * kv), F32)],
        compiler_params=_cparams("parallel"),
    )(chk, chv, *consts)

    nsb = S // SEL_BLK
    c_idx, s_idx = np.arange(nch), np.arange(nsb)
    ovl = ((c_idx[:, None] * CMP_STRIDE + CMP_BLK - 1 >= s_idx[None, :] * SEL_BLK)
           & (c_idx[:, None] * CMP_STRIDE < (s_idx[None, :] + 1) * SEL_BLK)).astype(np.float32)
    bgi = lambda b, g, i: (b, g, 0, 0)
    TS = min(NSA_TQ_SELECT, S)
    ns = S // TS
    sblk = lambda b, g, i: (b * ns + i, g)
    ocmp, sel = pl.pallas_call(
        _nsa_cmp_select_kernel,
        name="nsa_cmp_select",
        grid=(B, G, ns),
        in_specs=[pl.BlockSpec((TS, J * dh), sblk), pl.BlockSpec((None, None, nch, dh), bgi),
                  pl.BlockSpec((None, None, nch, dh), bgi), pl.BlockSpec((nsb, nch), lambda b, g, i: (0, 0))],
        out_specs=[pl.BlockSpec((TS, J * dh), sblk), pl.BlockSpec((None, None, nsb, TS), lambda b, g, i: (b, g, 0, i))],
        out_shape=[jax.ShapeDtypeStruct((T, H * dh), BF16), jax.ShapeDtypeStruct((B, G, nsb, S), BF16)],
        compiler_params=_cparams("parallel", "parallel", "parallel"),
    )(roped, kc, vc, jnp.asarray(ovl.T))
    TQ = NSA_TQ
    nq = S // TQ
    qblk = lambda b, g, i: (b * nq + i, g)

    gexp = np.zeros((G, 3, 128, J * dh), np.float32)
    for g in range(G):
        for c in range(3):
            for j in range(J):
                gexp[g, c, (g * J + j) * 3 + c, j * dh:(j + 1) * dh] = 1.0
    return pl.pallas_call(
        _nsa_main_kernel,
        name="nsa_main",
        grid=(B, G, nq),
        in_specs=[pl.BlockSpec((TQ, J * dh), qblk)] + [pl.BlockSpec((None, None, dh, S), bgi),
                                                       pl.BlockSpec((None, None, S, 2 * dh), bgi)] * 2 + [
            pl.BlockSpec((None, None, nsb, TQ), lambda b, g, i: (b, g, 0, i)),
            pl.BlockSpec((TQ, J * dh), qblk), pl.BlockSpec((TQ, 128), lambda b, g, i: (b * nq + i, 0)),
            pl.BlockSpec((None, 3, 128, J * dh), lambda b, g, i: (g, 0, 0, 0))],
        out_specs=pl.BlockSpec((TQ, J * dh), qblk),
        out_shape=jax.ShapeDtypeStruct((T, H * dh), BF16),
        compiler_params=_cparams("parallel", "parallel", "arbitrary"),
    )(roped, ksT, vs, kwT, vw, sel, ocmp, gates, jnp.asarray(gexp))


def _dil_kernel(q_ref, kp_ref, kc_ref, vp_ref, vc_ref, o_ref, lse_ref):
    NQ = q_ref.shape[0]
    prev_from = jnp.where(pl.program_id(2) == 0, NQ, 0)
    qi = lax.broadcasted_iota(I32, (NQ, 1), 0)
    kj = lax.broadcasted_iota(I32, (1, 2 * NQ), 1)
    dist = NQ + qi - kj
    bias = jnp.where((dist >= 0) & (dist <= NQ) & (kj >= prev_from), 0.0, NEG_INF)
    lane = lax.broadcasted_iota(I32, (NQ, 128), 1)
    first = lane < HEAD_DIM
    lse_all = jnp.zeros((NQ, 128), F32)
    zero = jnp.zeros((), q_ref.dtype)
    for hp in range(DIL_HEADS // 2):
        sl = slice(hp * 128, (hp + 1) * 128)
        q2 = q_ref[:, sl]
        kb = jnp.concatenate([kp_ref[:, sl], kc_ref[:, sl]], axis=0)
        vb = jnp.concatenate([vp_ref[:, sl], vc_ref[:, sl]], axis=0)
        outs = []
        for sub in range(2):
            keep = first if sub == 0 else jnp.logical_not(first)
            s = _nt(jnp.where(keep, q2, zero), kb) * HEAD_DIM ** -0.5 + bias
            m = jnp.max(s, axis=-1, keepdims=True)
            e = jnp.exp(s - m)
            den = jnp.sum(e, axis=-1, keepdims=True)
            outs.append(_mm((e / den).astype(BF16), vb))
            lse_all = jnp.where(lane == 2 * hp + sub, m + jnp.log(den), lse_all)
        o_ref[:, sl] = jnp.where(first, outs[0], outs[1]).astype(o_ref.dtype)
    lse_ref[...] = lse_all


def _proj_res_kernel(*refs, rope, dils):
    n_out = len(dils)
    x_ref, w_ref = refs[:2]
    rest = refs[2:]
    if rope:
        cos_ref, sin_ref = rest[:2]
        rest = rest[2:]
    out_refs, xb_ref, scr_ref = rest[:n_out], rest[n_out], rest[n_out + 1]
    tm, tn = x_ref.shape[0], w_ref.shape[1]

    @pl.when(pl.program_id(1) == 0)
    def _():
        xb_ref[...] = x_ref[...].astype(BF16)

    acc = _mm(xb_ref[...], w_ref[...])
    if rope:
        lane = lax.broadcasted_iota(I32, acc.shape, 1)
        lo = (lane % HEAD_DIM) < ROPE_HALF
        rot = jnp.where(lo, pltpu.roll(acc, tn - ROPE_HALF, 1), pltpu.roll(acc, ROPE_HALF, 1))
        acc = acc * cos_ref[...] + rot * sin_ref[...]
    if any(d > 1 for d in dils):
        for c in range(tn // 128):
            scr_ref[c] = acc[:, c * 128:(c + 1) * 128]
    for o_ref, dil in zip(out_refs, dils):
        if dil == 1:
            o_ref[...] = acc.astype(o_ref.dtype)
            continue
        for r in range(dil):
            for c in range(tn // 128):
                o_ref[:, r * tn + c * 128:r * tn + (c + 1) * 128] = (
                    scr_ref[c, pl.ds(r, tm // dil, stride=dil), :].astype(o_ref.dtype))


def _proj_res(x, w, dils, *, seq, rope, tm=1024, tn=D_MODEL):
    T, K = x.shape
    N = w.shape[1]
    nparts = N // tn
    in_specs = [pl.BlockSpec((tm, K), lambda i, j: (i, 0)), pl.BlockSpec((K, tn), lambda i, j: (0, j))]
    args = [x, w]
    if rope:
        nseq = seq // tm
        in_specs += [pl.BlockSpec((tm, tn), lambda i, j: (i % nseq, 0))] * 2
        args += list(_rope_tables(jnp.arange(seq), tn))
    return pl.pallas_call(
        functools.partial(_proj_res_kernel, rope=rope, dils=dils),
        name="proj_residue_rope" if rope else "proj_residue",
        grid=(T // tm, nparts),
        in_specs=in_specs,
        out_specs=[pl.BlockSpec((tm // d, d * tn), lambda i, j: (i, j)) for d in dils],
        out_shape=[jax.ShapeDtypeStruct((T // d, nparts * d * tn), BF16) for d in dils],
        scratch_shapes=[pltpu.VMEM((tm, K), BF16), pltpu.VMEM((tn // 128, tm, 128), F32)],
        compiler_params=_cparams("parallel", "arbitrary"),
    )(*args)


def _dil_group(qk, v, B, S, dil):
    NQ = DIL_STEPS
    U = S // dil
    nb = U // NQ
    W = D_MODEL
    qk_view = qk.reshape(B, U, 2 * dil * W)
    v_view = v.reshape(B, U, dil * W)
    cur = lambda part: (lambda b, r, n: (b, n, part * dil + r))
    prev = lambda part: (lambda b, r, n: (b, jnp.maximum(n - 1, 0), part * dil + r))
    blk = lambda f: pl.BlockSpec((None, NQ, W), f)
    o, lse = pl.pallas_call(
        _dil_kernel,
        name=f"dilated_attn_{dil}",
        grid=(B, dil, nb),
        in_specs=[blk(cur(0)), blk(prev(1)), blk(cur(1)), blk(prev(0)), blk(cur(0))],
        out_specs=[pl.BlockSpec((None, NQ, W), lambda b, r, n: (b, n, r)),
                   pl.BlockSpec((None, NQ, 128), lambda b, r, n: (b, n, r))],
        out_shape=[jax.ShapeDtypeStruct((B, U, dil * W), BF16), jax.ShapeDtypeStruct((B, U, dil * 128), F32)],
        compiler_params=_cparams("parallel", "parallel", "arbitrary"),
    )(qk_view, qk_view, qk_view, v_view, v_view)
    return o.reshape(B * U, dil * W), lse.reshape(B * U, dil * 128)


def _dil_outproj_ln_kernel(o0_ref, o1_ref, o2_ref, l0_ref, l1_ref, l2_ref, hexp_ref, w_ref, x_ref, g_ref, b_ref, out_ref,
                           oscr_ref, lscr_ref, *, dils):
    tm, W = x_ref.shape

    def natural(o_ref, l_ref, dil):
        if dil == 1:
            return o_ref[...].astype(F32), l_ref[...]
        n = tm // dil
        for r in range(dil):
            lscr_ref[0, pl.ds(r, n, stride=dil), :] = l_ref[:, r * 128:(r + 1) * 128]
            for c in range(W // 128):
                oscr_ref[c, pl.ds(r, n, stride=dil), :] = o_ref[:, r * W + c * 128:r * W + (c + 1) * 128].astype(F32)
        return jnp.concatenate([oscr_ref[c] for c in range(W // 128)], axis=1), lscr_ref[0]

    os_, ls = zip(*[natural(o, l, d) for o, l, d in zip((o0_ref, o1_ref, o2_ref), (l0_ref, l1_ref, l2_ref), dils)])
    m = jnp.maximum(jnp.maximum(ls[0], ls[1]), ls[2])
    es = [jnp.exp(l - m) for l in ls]
    tot = es[0] + es[1] + es[2]
    y = jnp.zeros((tm, W), F32)
    for e, o in zip(es, os_):
        y = y + _mm01(e / tot, hexp_ref[...]) * o
    z = ALPHA * x_ref[...] + _mm(y.astype(BF16), w_ref[...])
    out_ref[...] = _ln_rows(z, g_ref[...], b_ref[...])


def _dilated_layer(x, B, S, w_in, w_out, g, b, *, tm=256):
    T, D = x.shape
    dils = tuple(d for _, d in DIL_GROUPS)
    wb = w_in.astype(BF16)
    vals = _proj_res(x, wb[:, 2 * len(dils) * D:], dils, seq=S, rope=False)
    outs = []
    for gi, dil in enumerate(dils):
        qk, = _proj_res(x, wb[:, 2 * gi * D:(2 * gi + 2) * D], (dil,), seq=S, rope=True)
        outs.append(_dil_group(qk, vals[gi], B, S, dil))
    hexp = np.zeros((128, D), np.float32)
    for h in range(DIL_HEADS):
        hexp[h, h * HEAD_DIM:(h + 1) * HEAD_DIM] = 1.0
    row = lambda i: (i, 0)
    fixed = lambda i: (0, 0)
    return pl.pallas_call(
        functools.partial(_dil_outproj_ln_kernel, dils=dils),
        name="dilated_outproj_ln",
        grid=(T // tm,),
        in_specs=[pl.BlockSpec((tm // d, d * D), row) for d in dils] + [pl.BlockSpec((tm // d, d * 128), row) for d in dils]
        + [pl.BlockSpec((128, D), fixed), pl.BlockSpec((D, D), fixed), pl.BlockSpec((tm, D), row),
           pl.BlockSpec((1, D), fixed), pl.BlockSpec((1, D), fixed)],
        out_specs=pl.BlockSpec((tm, D), row),
        out_shape=jax.ShapeDtypeStruct((T, D), F32),
        scratch_shapes=[pltpu.VMEM((D // 128, tm, 128), F32), pltpu.VMEM((1, tm, 128), F32)],
        compiler_params=_cparams("parallel"),
    )(outs[0][0], outs[1][0], outs[2][0], outs[0][1], outs[1][1], outs[2][1], jnp.asarray(hexp),
      w_out.astype(BF16), x, g.reshape(1, D), b.reshape(1, D))


def _pool_ln_kernel(x_ref, halo_ref, w_ref, scale_ref, g_ref, b_ref, o_ref, ext_ref):
    TS = x_ref.shape[0]
    s = pl.program_id(1)
    x = x_ref[...]
    ext_ref[0:POOL_HALO, :] = jnp.where(s == 0, 0.0, halo_ref[...])
    ext_ref[POOL_HALO:POOL_HALO + TS, :] = x
    cnt = (s * TS + lax.broadcasted_iota(I32, (TS, 1), 0) + 1).astype(F32)
    ys = []
    for gi, w in enumerate(POOL_WINDOWS):
        sl = slice(gi * POOL_GROUP, (gi + 1) * POOL_GROUP)
        xg = x[:, sl]
        tot = xg
        for j in range(1, w):
            tot = tot + ext_ref[POOL_HALO - j:POOL_HALO - j + TS, sl]
        mean = tot / jnp.minimum(cnt, float(w))
        ys.append(_mm((mean - xg).astype(BF16), w_ref[gi]))
    y = jnp.concatenate(ys, axis=1) * scale_ref[...]
    o_ref[...] = _ln_rows(ALPHA * x + y, g_ref[...], b_ref[...])


def _pool_layer(x, B, S, w_grp, scale, g, b, *, ts=512):
    T, D = x.shape
    ns = S // ts
    hb = ts // POOL_HALO
    fixed = lambda bb, s: (0, 0)
    return pl.pallas_call(
        _pool_ln_kernel,
        name="pool_ln",
        grid=(B, ns),
        in_specs=[pl.BlockSpec((ts, D), lambda bb, s: (bb * ns + s, 0)),
                  pl.BlockSpec((POOL_HALO, D), lambda bb, s: (jnp.maximum((bb * ns + s) * hb - 1, 0), 0)),
                  pl.BlockSpec((len(POOL_WINDOWS), POOL_GROUP, POOL_GROUP), lambda bb, s: (0, 0, 0)),
                  pl.BlockSpec((1, D), fixed), pl.BlockSpec((1, D), fixed), pl.BlockSpec((1, D), fixed)],
        out_specs=pl.BlockSpec((ts, D), lambda bb, s: (bb * ns + s, 0)),
        out_shape=jax.ShapeDtypeStruct((T, D), F32),
        scratch_shapes=[pltpu.VMEM((ts + POOL_HALO, D), F32)],
        compiler_params=_cparams("parallel", "arbitrary"),
    )(x, x, w_grp.astype(BF16), scale.reshape(1, D), g.reshape(1, D), b.reshape(1, D))


U32 = jnp.uint32
HI16 = 0xFFFF0000
ROW_WORDS = D_MODEL // 2


def _pack_halves(v):
    h = v.shape[1] // 2
    bits = lambda a: lax.bitcast_convert_type(a.astype(BF16).astype(F32), U32)
    return (bits(v[:, :h]) >> 16) | (bits(v[:, h:]) & U32(HI16))


def _unpack_halves(w):
    lo = lax.bitcast_convert_type(w << 16, F32)
    hi = lax.bitcast_convert_type(w & U32(HI16), F32)
    return jnp.concatenate([lo, hi], axis=1)


def _router_kernel(x_ref, wT_ref, bias_ref, triu_ref, eidx_ref, gw_ref, rank_ref, cnt_ref, xp_ref, carry_ref):
    E = N_EXPERTS
    per = E // N_EXPERT_GROUPS
    TM = x_ref.shape[0]

    @pl.when(pl.program_id(0) == 0)
    def _():
        carry_ref[...] = jnp.zeros_like(carry_ref)

    x = x_ref[...]
    xp_ref[...] = _pack_halves(x)

    scores = _sigmoid(_nt(wT_ref[...], x, precision=HIGHEST))
    biased = scores + bias_ref[...]
    eio = lax.broadcasted_iota(I32, (E, TM), 0)

    gio = lax.broadcasted_iota(I32, (per, TM), 0)
    gscore = []
    for gidx in range(N_EXPERT_GROUPS):
        slab = biased[gidx * per:(gidx + 1) * per, :]
        m1 = jnp.max(slab, axis=0, keepdims=True)
        i1 = jnp.min(jnp.where(slab == m1, gio, per), axis=0, keepdims=True)
        m2 = jnp.max(jnp.where(gio == i1, BELOW_NEG_INF, slab), axis=0, keepdims=True)
        gscore.append(m1 + m2)
    slabs = []
    for gidx in range(N_EXPERT_GROUPS):
        beat = jnp.zeros((1, TM), F32)
        for o in range(N_EXPERT_GROUPS):
            if o == gidx:
                continue
            wins = (gscore[o] >= gscore[gidx]) if o < gidx else (gscore[o] > gscore[gidx])
            beat = beat + jnp.where(wins, 1.0, 0.0)
        keep = beat < float(TOPK_GROUPS)
        slabs.append(jnp.where(keep, biased[gidx * per:(gidx + 1) * per, :], NEG_INF))
    cur = jnp.concatenate(slabs, axis=0)

    picked = jnp.zeros((E, TM), F32)
    idxs, vals = [], []
    for _ in range(TOP_K):
        m = jnp.max(cur, axis=0, keepdims=True)
        idx = jnp.min(jnp.where(cur == m, eio, E), axis=0, keepdims=True)
        hit = eio == idx
        picked = jnp.where(hit, 1.0, picked)
        cur = jnp.where(hit, BELOW_NEG_INF, cur)
        idxs.append(idx)
        vals.append(jnp.sum(jnp.where(hit, scores, 0.0), axis=0, keepdims=True))
    total = vals[0]
    for v in vals[1:]:
        total = total + v

    pos = _mm(picked.astype(BF16), triu_ref[...]) + carry_ref[...]
    for k in range(TOP_K):
        eidx_ref[k:k + 1, :] = idxs[k]
        gw_ref[k:k + 1, :] = vals[k] / total * ROUTED_SCALE
        rank_ref[k:k + 1, :] = jnp.sum(jnp.where(eio == idxs[k], pos, 0.0), axis=0, keepdims=True).astype(I32)
    carry_ref[...] = carry_ref[...] + jnp.sum(picked, axis=1, keepdims=True)
    cnt_ref[...] = carry_ref[...].astype(I32)


def _router(x, router_w, router_bias, *, tm=512):
    T, D = x.shape
    E = N_EXPERTS
    triu = jnp.triu(jnp.ones((tm, tm), F32), k=1).astype(BF16)
    col = lambda i: (0, i)
    fixed = lambda i: (0, 0)
    return pl.pallas_call(
        _router_kernel,
        name="moe_router",
        grid=(T // tm,),
        in_specs=[pl.BlockSpec((tm, D), lambda i: (i, 0)), pl.BlockSpec((E, D), fixed),
                  pl.BlockSpec((E, 1), fixed), pl.BlockSpec((tm, tm), fixed)],
        out_specs=[pl.BlockSpec((TOP_K, tm), col), pl.BlockSpec((TOP_K, tm), col), pl.BlockSpec((TOP_K, tm), col),
                   pl.BlockSpec((E, 1), fixed), pl.BlockSpec((tm, ROW_WORDS), lambda i: (i, 0))],
        out_shape=[jax.ShapeDtypeStruct((TOP_K, T), I32), jax.ShapeDtypeStruct((TOP_K, T), F32),
                   jax.ShapeDtypeStruct((TOP_K, T), I32), jax.ShapeDtypeStruct((E, 1), I32),
                   jax.ShapeDtypeStruct((T, ROW_WORDS), U32)],
        scratch_shapes=[pltpu.VMEM((E, 1), F32)],
        compiler_params=_cparams("arbitrary"),
    )(x, router_w.T.astype(F32), router_bias.reshape(E, 1).astype(F32), triu)


def _tile_indices(dest_hbm, idx_ref, sem_idx, n_idx):
    i = pl.program_id(0)

    def idx_copy(step):
        slot = step % 2
        return pltpu.make_async_copy(dest_hbm.at[step], idx_ref.at[pl.ds(slot * n_idx, n_idx)], sem_idx.at[slot])

    @pl.when(i == 0)
    def _():
        idx_copy(0).start()

    idx_copy(i).wait()

    @pl.when(i + 1 < pl.num_programs(0))
    def _():
        idx_copy(i + 1).start()

    return (i % 2) * n_idx


def _dispatch_kernel(cnt_ref, pstart_ref, dest_hbm, x_ref, rows_ref, idx_ref, zero_ref, sem_idx, sem_rows, sem_zero):
    TD = x_ref.shape[0]

    @pl.when(pl.program_id(0) == 0)
    def _():
        zero_ref[...] = jnp.zeros_like(zero_ref)

        def per_expert(e, c):
            n = cnt_ref[e]
            base = pstart_ref[e]
            pad_to = (n + EXP_BLK - 1) // EXP_BLK * EXP_BLK
            fill = lambda r: pltpu.make_async_copy(zero_ref.at[pl.ds(0, 1)], rows_ref.at[base + r], sem_zero)

            def start(r, c2):
                fill(r).start()
                return c2

            def wait(r, c2):
                fill(r).wait()
                return c2

            lax.fori_loop(n, pad_to, start, 0)
            lax.fori_loop(n, pad_to, wait, 0)
            return c

        lax.fori_loop(0, N_EXPERTS, per_expert, 0)

    base = _tile_indices(dest_hbm, idx_ref, sem_idx, TOP_K * TD)
    scatter = lambda t, k: pltpu.make_async_copy(x_ref.at[pl.ds(t, 1)], rows_ref.at[idx_ref[base + k * TD + t]], sem_rows)

    def issue(t, c):
        for k in range(TOP_K):
            scatter(t, k).start(priority=k % 2)
        return c

    def drain(t, c):
        for k in range(TOP_K):
            scatter(t, k).wait()
        return c

    lax.fori_loop(0, TD, issue, 0, unroll=8)
    lax.fori_loop(0, TD, drain, 0, unroll=8)


def _dispatch(xp, dest_tiles, counts, pad_start, n_rows, *, td):
    T = xp.shape[0]
    return pl.pallas_call(
        _dispatch_kernel,
        name="moe_dispatch",
        grid_spec=pltpu.PrefetchScalarGridSpec(
            num_scalar_prefetch=2,
            grid=(T // td,),
            in_specs=[pl.BlockSpec(memory_space=pl.ANY), pl.BlockSpec((td, ROW_WORDS), lambda i, c, p: (i, 0))],
            out_specs=pl.BlockSpec(memory_space=pl.ANY),
            scratch_shapes=[pltpu.SMEM((2 * TOP_K * td,), I32), pltpu.VMEM((8, ROW_WORDS), U32),
                            pltpu.SemaphoreType.DMA((2,)), pltpu.SemaphoreType.DMA(()), pltpu.SemaphoreType.DMA(())],
        ),
        out_shape=jax.ShapeDtypeStruct((n_rows, 1, ROW_WORDS), U32),
        compiler_params=_cparams("arbitrary"),
    )(counts, pad_start, dest_tiles, xp)


def _expert_kernel(be_ref, nu_ref, rows_hbm, wg_ref, wu_ref, wd_ref, out_hbm, xbuf, obuf, wgu_bf, wd_bf, sem_in, sem_out):
    i = pl.program_id(0)
    nu = nu_ref[0]

    @pl.when((i < nu) & ((i == 0) | (be_ref[i] != be_ref[jnp.maximum(i - 1, 0)])))
    def _():
        wgu_bf[:, :EXPERT_FF] = wg_ref[0].astype(BF16)
        wgu_bf[:, EXPERT_FF:] = wu_ref[0].astype(BF16)
        wd_bf[...] = wd_ref[0].astype(BF16)

    blk = lambda step: pl.ds(pl.multiple_of(step * EXP_BLK, EXP_BLK), EXP_BLK)
    in_copy = lambda step: pltpu.make_async_copy(rows_hbm.at[blk(step), 0], xbuf.at[step % 2], sem_in.at[step % 2])
    out_copy = lambda step: pltpu.make_async_copy(obuf.at[step % 2], out_hbm.at[blk(step), 0], sem_out.at[step % 2])

    @pl.when(i == 0)
    def _():
        in_copy(0).start()

    @pl.when(i < nu)
    def _():
        in_copy(i).wait()

        @pl.when(i + 1 < nu)
        def _():
            in_copy(i + 1).start()

        @pl.when(i >= 2)
        def _():
            out_copy(i - 2).wait()

        slot = i % 2
        gu = _mm(_unpack_halves(xbuf[slot]).astype(BF16), wgu_bf[...])
        h = _silu(gu[:, :EXPERT_FF]) * gu[:, EXPERT_FF:]
        obuf[slot] = _pack_halves(_mm(h.astype(BF16), wd_bf[...]))
        out_copy(i).start()

    @pl.when(i == pl.num_programs(0) - 1)
    def _():
        @pl.when(nu >= 2)
        def _():
            out_copy(nu - 2).wait()

        out_copy(nu - 1).wait()


def _experts(rows, blk_expert, n_used, w_gate, w_up, w_down, layer):
    R = rows.shape[0]
    D = w_down.shape[-1]
    n_blk = R // EXP_BLK
    expert = lambda i, be, nu: (layer, be[jnp.minimum(i, nu[0] - 1)], 0, 0)
    return pl.pallas_call(
        _expert_kernel,
        name="moe_experts",
        grid_spec=pltpu.PrefetchScalarGridSpec(
            num_scalar_prefetch=2,
            grid=(n_blk,),
            in_specs=[pl.BlockSpec(memory_space=pl.ANY), pl.BlockSpec((None, 1, D, EXPERT_FF), expert),
                      pl.BlockSpec((None, 1, D, EXPERT_FF), expert), pl.BlockSpec((None, 1, EXPERT_FF, D), expert)],
            out_specs=pl.BlockSpec(memory_space=pl.ANY),
            scratch_shapes=[pltpu.VMEM((2, EXP_BLK, ROW_WORDS), U32), pltpu.VMEM((2, EXP_BLK, ROW_WORDS), U32),
                            pltpu.VMEM((D, 2 * EXPERT_FF), BF16), pltpu.VMEM((EXPERT_FF, D), BF16),
                            pltpu.SemaphoreType.DMA((2,)), pltpu.SemaphoreType.DMA((2,))],
        ),
        out_shape=jax.ShapeDtypeStruct((R, 1, ROW_WORDS), U32),
        compiler_params=_cparams("arbitrary"),
    )(blk_expert, n_used, rows, w_gate, w_up, w_down)


def _combine_kernel(dest_hbm, gw_ref, x_ref, rows_ref, wsgu_ref, wsd_ref, g_ref, b_ref, p_ref, wp_ref, wpg_ref,
                    o_ref, idx_ref, buf_ref, sem_idx, sem_rows):
    TM = x_ref.shape[0]
    n_idx = TOP_K * TM
    i = pl.program_id(0)
    n = pl.num_programs(0)

    def idx_copy(step):
        slot = step % 3
        return pltpu.make_async_copy(dest_hbm.at[step], idx_ref.at[pl.ds(slot * n_idx, n_idx)], sem_idx.at[slot])

    def row_loop(step, slot, start):
        base = (step % 3) * n_idx

        def body(t, c):
            for k in range(TOP_K):
                cp = pltpu.make_async_copy(rows_ref.at[idx_ref[base + k * TM + t]], buf_ref.at[slot, k, pl.ds(t, 1)],
                                           sem_rows.at[slot])
                if start:
                    cp.start(priority=k % 2)
                else:
                    cp.wait()
            return c

        lax.fori_loop(0, TM, body, 0, unroll=8)

    def for_parity(step, fn):
        for slot in range(2):
            pl.when(step % 2 == slot)(functools.partial(fn, slot))

    @pl.when(i == 0)
    def _():
        idx_copy(0).start()
        idx_copy(0).wait()
        row_loop(0, 0, True)

        @pl.when(n > 1)
        def _():
            idx_copy(1).start()

    @pl.when(i + 1 < n)
    def _():
        idx_copy(i + 1).wait()

        @pl.when(i + 2 < n)
        def _():
            idx_copy(i + 2).start()

        for_parity(i + 1, lambda slot: row_loop(i + 1, slot, True))

    x = x_ref[...]
    gu = _mm(x.astype(BF16), wsgu_ref[...])
    ff = gu.shape[1] // 2
    f = _mm((_silu(gu[:, :ff]) * gu[:, ff:]).astype(BF16), wsd_ref[...])
    for_parity(i, lambda slot: row_loop(i, slot, False))
    gw = gw_ref[...]
    cur = i % 2
    for k in range(TOP_K):
        f = f + gw[:, k:k + 1] * _unpack_halves(buf_ref[cur, k])
    x2 = _ln_rows(ALPHA * x + f, g_ref[...], b_ref[...])
    gate = _sigmoid(_mm(x2.astype(BF16), wpg_ref[...]))
    o_ref[...] = x2 + gate * _mm(p_ref[...].astype(BF16), wp_ref[...])


def _combine(dest_tiles, gw, x, rows_out, wsgu, wsd, g, b, p, wp, wpg, *, tm):
    T, D = x.shape
    row = lambda i: (i, 0)
    fixed = lambda i: (0, 0)
    full = lambda a: pl.BlockSpec(a.shape, fixed)
    return pl.pallas_call(
        _combine_kernel,
        name="moe_combine",
        grid=(T // tm,),
        in_specs=[pl.BlockSpec(memory_space=pl.ANY), pl.BlockSpec((tm, TOP_K), row), pl.BlockSpec((tm, D), row),
                  pl.BlockSpec(memory_space=pl.ANY), full(wsgu), full(wsd), pl.BlockSpec((1, D), fixed),
                  pl.BlockSpec((1, D), fixed), pl.BlockSpec((tm, PLE_DIM), row), full(wp), full(wpg)],
        out_specs=pl.BlockSpec((tm, D), row),
        out_shape=jax.ShapeDtypeStruct((T, D), F32),
        scratch_shapes=[pltpu.SMEM((3 * TOP_K * tm,), I32), pltpu.VMEM((2, TOP_K, tm, ROW_WORDS), U32),
                        pltpu.SemaphoreType.DMA((3,)), pltpu.SemaphoreType.DMA((2,))],
        compiler_params=_cparams("arbitrary"),
    )(dest_tiles, gw, x, rows_out, wsgu, wsd, g.reshape(1, D), b.reshape(1, D), p, wp, wpg)


MOE_TILE = 256


def _moe_ple_layer(x, p, layer, router_w, router_bias, w_gate, w_up, w_down, ws_gate, ws_up, ws_down, g, b, ple_w, ple_gate_w):
    T, D = x.shape
    eidx, gw, rank, counts, xp = _router(x, router_w, router_bias)
    counts = counts.reshape(N_EXPERTS)
    padded = (counts + EXP_BLK - 1) // EXP_BLK * EXP_BLK
    pad_end = jnp.cumsum(padded)
    pad_start = pad_end - padded
    n_blk = T * TOP_K // EXP_BLK + N_EXPERTS
    e_iota = jnp.arange(N_EXPERTS, dtype=I32)
    dest = rank + jnp.sum(jnp.where(eidx[..., None] == e_iota, pad_start, 0), axis=-1)
    tm = MOE_TILE
    dest_tiles = dest.reshape(TOP_K, T // tm, tm).transpose(1, 0, 2).reshape(T // tm, TOP_K * tm)
    blk_first = jnp.arange(n_blk, dtype=I32)[:, None] * EXP_BLK
    blk_expert = jnp.minimum(jnp.sum((pad_end[None, :] <= blk_first).astype(I32), axis=-1), N_EXPERTS - 1)
    n_used = (pad_end[-1:] // EXP_BLK).astype(I32)

    rows = _dispatch(xp, dest_tiles, counts.astype(I32), pad_start.astype(I32), n_blk * EXP_BLK, td=tm)
    rows_out = _experts(rows, blk_expert, n_used, w_gate, w_up, w_down, layer)
    wsgu = jnp.concatenate([ws_gate, ws_up], axis=-1).astype(BF16)
    return _combine(dest_tiles, gw.T, x, rows_out, wsgu, ws_down.astype(BF16), g, b, p,
                    ple_w.astype(BF16), ple_gate_w.astype(BF16), tm=tm)


def kernel(x, p, ln_g, ln_b, mlstm_w_in, mlstm_conv, mlstm_ig_bias, mlstm_fg_bias, mlstm_norm_g, mlstm_w_out, nsa_w_in, nsa_cmp_pos_k, nsa_cmp_pos_v, nsa_cmp_wk, nsa_cmp_wv, nsa_w_out, dil_w_in, dil_w_out, pool_w, pool_scale, router_w, router_bias, exp_w_gate, exp_w_up, exp_w_down, sh_w_gate, sh_w_up, sh_w_down, ple_w, ple_gate_w):
    B, S, D = x.shape
    T = B * S
    xf = x.reshape(T, D)
    pf = p.reshape(DEPTH, T, PLE_DIM)
    for i in range(DEPTH):
        kind, j = i % 4, i // 4
        g1, b1 = ln_g[i, 0], ln_b[i, 0]
        if kind == 0:
            y = _mlstm_mixer(xf, B, S, mlstm_w_in[j], mlstm_conv[j], mlstm_ig_bias[j], mlstm_fg_bias[j], mlstm_norm_g[j])
            xf = _outproj_ln(y, mlstm_w_out[j].astype(BF16), xf, g1, b1)
        elif kind == 1:
            y = _nsa_mixer(xf, B, S, nsa_w_in[j], nsa_cmp_pos_k[j], nsa_cmp_pos_v[j], nsa_cmp_wk[j], nsa_cmp_wv[j])
            xf = _outproj_ln(y, nsa_w_out[j].astype(BF16), xf, g1, b1)
        elif kind == 2:
            xf = _dilated_layer(xf, B, S, dil_w_in[j], dil_w_out[j], g1, b1)
        else:
            xf = _pool_layer(xf, B, S, pool_w[j], pool_scale[j], g1, b1)
        xf = _moe_ple_layer(xf, pf[i], i, router_w[i], router_bias[i], exp_w_gate, exp_w_up, exp_w_down,
                            sh_w_gate[i], sh_w_up[i], sh_w_down[i], ln_g[i, 1], ln_b[i, 1], ple_w[i], ple_gate_w[i])
    return xf.reshape(B, S, D)
```

```python
import functools

import numpy as np
import jax
import jax.numpy as jnp
from jax import lax
from jax.experimental import pallas as pl
from jax.experimental.pallas import tpu as pltpu

F32 = jnp.float32
BF16 = jnp.bfloat16
I32 = jnp.int32
HIGHEST = lax.Precision.HIGHEST

D_MODEL = 1024
DEPTH = 4
ALPHA = (2.0 * DEPTH) ** 0.25
LN_EPS = 1e-5
NEG_INF = -1e30
TINY = 1e-30
BELOW_NEG_INF = -3e38
NEG_FLOOR = 0.999 * NEG_INF
ROPE_THETA = 10000.0
HEAD_DIM = 64
ROPE_HALF = HEAD_DIM // 2

MLSTM_HEADS = 8
MLSTM_QK_DIM = 64
MLSTM_V_DIM = 128
MLSTM_CONV = 4
MLSTM_L = 256

NSA_HEADS = 16
NSA_KV_HEADS = 4
NSA_J = NSA_HEADS // NSA_KV_HEADS
CMP_STRIDE = 16
CMP_BLK = 32
SEL_BLK = 64
N_SEL = 16
NSA_WINDOW = 512
FORCE_SCORE = 1e4
NSA_TQ = 256
NSA_TQ_SELECT = 512
NSA_TK = 512

DIL_HEADS = 16
DIL_GROUPS = ((128, 1), (512, 4), (2048, 16))
DIL_STEPS = 128

POOL_WINDOWS = (2, 4, 8, 16)
POOL_GROUP = 256
POOL_HALO = 16

N_EXPERTS = 64
TOP_K = 8
N_EXPERT_GROUPS = 8
TOPK_GROUPS = 4
EXPERT_FF = 256
ROUTED_SCALE = 2.5
EXP_BLK = 1024
PLE_DIM = 256

VMEM_LIMIT = 48 * 1024 * 1024


def _cparams(*sem):
    return pltpu.CompilerParams(dimension_semantics=sem, vmem_limit_bytes=VMEM_LIMIT)


def _nt(a, b, **kw):
    return lax.dot_general(a, b, (((1,), (1,)), ((), ())), preferred_element_type=F32, **kw)


def _tn(a, b, **kw):
    return lax.dot_general(a, b, (((0,), (0,)), ((), ())), preferred_element_type=F32, **kw)


def _mm(a, b, **kw):
    return jnp.dot(a, b, preferred_element_type=F32, **kw)


def _split3(x):
    a = x.astype(BF16)
    r = x - a.astype(F32)
    b = r.astype(BF16)
    return a, b, (r - b.astype(F32)).astype(BF16)


def _mm01(x, e01):
    eb = e01.astype(BF16)
    a, b, c = _split3(x)
    return _mm(a, eb) + _mm(b, eb) + _mm(c, eb)


def _sigmoid(z):
    return 1.0 / (1.0 + jnp.exp(-z))


def _silu(z):
    return z * _sigmoid(z)


def _ln_rows(z, g, b):
    mu = jnp.mean(z, axis=-1, keepdims=True)
    d = z - mu
    var = jnp.mean(d * d, axis=-1, keepdims=True)
    return d * lax.rsqrt(var + LN_EPS) * g + b


def _proj_kernel(*refs, rope):
    if rope:
        x_ref, w_ref, cos_ref, sin_ref, o_ref, xb_ref = refs
    else:
        x_ref, w_ref, o_ref, xb_ref = refs

    @pl.when(pl.program_id(1) == 0)
    def _():
        xb_ref[...] = x_ref[...].astype(BF16)

    acc = _mm(xb_ref[...], w_ref[...])
    if rope:
        tn = acc.shape[1]
        lane = lax.broadcasted_iota(I32, acc.shape, 1)
        lo = (lane % HEAD_DIM) < ROPE_HALF
        rot = jnp.where(lo, pltpu.roll(acc, tn - ROPE_HALF, 1), pltpu.roll(acc, ROPE_HALF, 1))
        acc = acc * cos_ref[...] + rot * sin_ref[...]
    o_ref[...] = acc.astype(o_ref.dtype)


def _proj(x, w, *, out_dtype, tm, tn, rope_tables=None, seq=None):
    T, K = x.shape
    N = w.shape[1]
    assert T % tm == 0 and N % tn == 0
    in_specs = [pl.BlockSpec((tm, K), lambda i, j: (i, 0)), pl.BlockSpec((K, tn), lambda i, j: (0, j))]
    args = [x, w]
    if rope_tables is not None:
        nseq = seq // tm
        in_specs += [pl.BlockSpec((tm, tn), lambda i, j: (i % nseq, 0))] * 2
        args += list(rope_tables)
    return pl.pallas_call(
        functools.partial(_proj_kernel, rope=rope_tables is not None),
        name="proj_rope" if rope_tables is not None else "proj",
        grid=(T // tm, N // tn),
        in_specs=in_specs,
        out_specs=pl.BlockSpec((tm, tn), lambda i, j: (i, j)),
        out_shape=jax.ShapeDtypeStruct((T, N), out_dtype),
        scratch_shapes=[pltpu.VMEM((tm, K), BF16)],
        compiler_params=_cparams("parallel", "arbitrary"),
    )(*args)


def _rope_tables(pos, width):
    inv = ROPE_THETA ** (-jnp.arange(ROPE_HALF, dtype=F32) / ROPE_HALF)
    ang = pos.astype(F32)[:, None] * inv[None, :]
    cos, sin = jnp.cos(ang), jnp.sin(ang)
    cos64 = jnp.concatenate([cos, cos], -1)
    sin64 = jnp.concatenate([-sin, sin], -1)
    rep = width // HEAD_DIM
    return jnp.tile(cos64, (1, rep)), jnp.tile(sin64, (1, rep))


def _outproj_ln_kernel(y_ref, w_ref, x_ref, g_ref, b_ref, o_ref):
    y = _mm(y_ref[...].astype(BF16), w_ref[...])
    o_ref[...] = _ln_rows(ALPHA * x_ref[...] + y, g_ref[...], b_ref[...])


def _outproj_ln(y, w, x, g, b, *, tm=512):
    T, D = x.shape
    K = y.shape[1]
    row = lambda i: (i, 0)
    fixed = lambda i: (0, 0)
    return pl.pallas_call(
        _outproj_ln_kernel,
        name="outproj_ln",
        grid=(T // tm,),
        in_specs=[pl.BlockSpec((tm, K), row), pl.BlockSpec((K, D), fixed), pl.BlockSpec((tm, D), row),
                  pl.BlockSpec((1, D), fixed), pl.BlockSpec((1, D), fixed)],
        out_specs=pl.BlockSpec((tm, D), row),
        out_shape=jax.ShapeDtypeStruct((T, D), F32),
        compiler_params=_cparams("parallel"),
    )(y, w, x, g.reshape(1, D), b.reshape(1, D))


def _log_sigmoid(z):
    return jnp.minimum(z, 0.0) - jnp.log(1.0 + jnp.exp(-jnp.abs(z)))


def _mlstm_kernel(qk_ref, v_ref, o_ref, gc_ref, gr_ref, convw_ref, bias_c_ref, bias_r_ref, ng_ref,
                  tri_ref, triT_ref, out_ref, C_ref, n_ref, m_ref, ext_ref):
    L = qk_ref.shape[0]
    H, dk, dv = MLSTM_HEADS, MLSTM_QK_DIM, MLSTM_V_DIM

    @pl.when(pl.program_id(1) == 0)
    def _():
        C_ref[...] = jnp.zeros_like(C_ref)
        n_ref[...] = jnp.zeros_like(n_ref)
        m_ref[...] = jnp.zeros_like(m_ref)
        ext_ref[0:8, :] = jnp.zeros((8, ext_ref.shape[1]), F32)

    cur = qk_ref[...]
    ext_ref[8:8 + L, :] = cur
    acc = convw_ref[3:4, :] * cur
    for j in range(MLSTM_CONV - 1):
        acc = acc + convw_ref[j:j + 1, :] * ext_ref[5 + j:5 + j + L, :]
    ext_ref[0:8, :] = cur[L - 8:L, :]
    qk = _silu(acc)

    gc = gc_ref[...] + bias_c_ref[...]
    gr = gr_ref[...] + bias_r_ref[...]
    tri01 = tri_ref[...].astype(BF16)
    b_col = sum(_mm(tri01, t) for t in _split3(_log_sigmoid(gc)))
    b_row = _mm01(_log_sigmoid(gr[H:2 * H, :]), triT_ref[...])
    ig_row = gr[0:H, :]
    tri = lax.broadcasted_iota(I32, (L, L), 0) >= lax.broadcasted_iota(I32, (L, L), 1)

    for h in range(H):
        q = (qk[:, h * dk:(h + 1) * dk] * dk ** -0.5).astype(BF16)
        k = qk[:, H * dk + h * dk:H * dk + (h + 1) * dk]
        kb = k.astype(BF16)
        v = v_ref[:, h * dv:(h + 1) * dv].astype(BF16)
        b_c = b_col[:, H + h:H + h + 1]
        ig_c = gc[:, h:h + 1]
        b_r = b_row[h:h + 1, :]
        m_prev = m_ref[h:h + 1, 0:1]
        C = C_ref[h]
        n = n_ref[h:h + 1, :]

        logD = jnp.where(tri, b_c - b_r + ig_row[h:h + 1, :], NEG_INF)
        inter = b_c + m_prev
        m_t = jnp.maximum(inter, jnp.max(logD, axis=-1, keepdims=True))
        s = _nt(q, kb) * jnp.exp(logD - m_t)
        w_inter = jnp.exp(inter - m_t)
        num = w_inter * _mm(q, C.astype(BF16)) + _mm(s.astype(BF16), v)
        den = w_inter * jnp.sum(q.astype(F32) * n, axis=-1, keepdims=True) + jnp.sum(s, axis=-1, keepdims=True)
        hv = num / jnp.maximum(jnp.abs(den), jnp.exp(-m_t))

        mu = jnp.mean(hv, axis=-1, keepdims=True)
        d = hv - mu
        hn = d * lax.rsqrt(jnp.mean(d * d, axis=-1, keepdims=True) + LN_EPS)
        og = _sigmoid(o_ref[:, h * dv:(h + 1) * dv])
        out_ref[:, h * dv:(h + 1) * dv] = (hn * ng_ref[:, h * dv:(h + 1) * dv] * og).astype(out_ref.dtype)

        bL = b_c[L - 1:L, :]
        logw = bL - b_c + ig_c
        m_new = jnp.maximum(bL + m_prev, jnp.max(logw, axis=0, keepdims=True))
        decay = jnp.exp(bL + m_prev - m_new)
        kw = k * jnp.exp(logw - m_new)
        C_ref[h] = decay * C + _tn(kw.astype(BF16), v)
        n_ref[h:h + 1, :] = decay * n + jnp.sum(kw, axis=0, keepdims=True)
        m_ref[h:h + 1, :] = jnp.broadcast_to(m_new, (1, m_ref.shape[1]))


def _mlstm_mixer(x, B, S, w_in, conv_w, ig_bias, fg_bias, norm_g):
    T = B * S
    H, L = MLSTM_HEADS, MLSTM_L
    wb = w_in.astype(BF16)
    main = _proj(x, wb[:, :3 * D_MODEL], out_dtype=F32, tm=512, tn=1024)
    wg = jnp.pad(wb[:, 3 * D_MODEL:], ((0, 0), (0, 128 - 2 * H)))
    gates = _proj(x, wg, out_dtype=F32, tm=512, tn=128)
    gates_r = gates[:, :2 * H].reshape(B, S, 2 * H).transpose(0, 2, 1)
    bias16 = jnp.concatenate([ig_bias, fg_bias]).astype(F32)
    bias_c = jnp.pad(bias16, (0, 128 - 2 * H)).reshape(1, 128)
    bias_r = bias16.reshape(2 * H, 1)
    tri = jnp.tril(jnp.ones((L, L), F32))
    nc = S // L
    rowblk = lambda c: (lambda b, i: (b * nc + i, c))
    fixed = lambda b, i: (0, 0)
    return pl.pallas_call(
        _mlstm_kernel,
        name="mlstm",
        grid=(B, nc),
        in_specs=[pl.BlockSpec((L, D_MODEL), rowblk(0)), pl.BlockSpec((L, D_MODEL), rowblk(1)),
                  pl.BlockSpec((L, D_MODEL), rowblk(2)), pl.BlockSpec((L, 128), rowblk(0)),
                  pl.BlockSpec((None, 2 * H, L), lambda b, i: (b, 0, i)),
                  pl.BlockSpec((MLSTM_CONV, D_MODEL), fixed), pl.BlockSpec((1, 128), fixed),
                  pl.BlockSpec((2 * H, 1), fixed), pl.BlockSpec((1, D_MODEL), fixed),
                  pl.BlockSpec((L, L), fixed), pl.BlockSpec((L, L), fixed)],
        out_specs=pl.BlockSpec((L, D_MODEL), rowblk(0)),
        out_shape=jax.ShapeDtypeStruct((T, D_MODEL), BF16),
        scratch_shapes=[pltpu.VMEM((H, MLSTM_QK_DIM, MLSTM_V_DIM), F32), pltpu.VMEM((H, MLSTM_QK_DIM), F32),
                        pltpu.VMEM((H, 128), F32), pltpu.VMEM((L + 8, D_MODEL), F32)],
        compiler_params=_cparams("parallel", "arbitrary"),
    )(main, main, main, gates, gates_r, conv_w.astype(F32), bias_c, bias_r, norm_g.reshape(1, D_MODEL).astype(F32),
      tri, tri.T)


def _rot_cols(w):
    shp = w.shape
    w4 = w.reshape(shp[:-1] + (shp[-1] // HEAD_DIM, 2, ROPE_HALF))
    return jnp.flip(w4, axis=-2).reshape(shp)


def _nsa_compress_kernel(chk_ref, chv_ref, wk_lo_ref, wk_hi_ref, wv_lo_ref, wv_hi_ref, pk_ref, pv_ref, wk_ref, wv_ref,
                         cos_ref, sin_ref, kc_ref, vc_ref, sh_ref):
    NCH = chk_ref.shape[0]
    G = NSA_KV_HEADS
    sh_ref[NCH:NCH + 8, :] = jnp.zeros((8, sh_ref.shape[1]), F32)
    live = lax.broadcasted_iota(I32, (NCH, 1), 0) < NCH - 1

    def blocks(ch_ref, lo_ref, hi_ref, p_ref, w_ref):
        ch = ch_ref[...]
        n = lo_ref.shape[1]
        sh_ref[0:NCH, 0:n] = _mm(ch, hi_ref[...])
        const = _mm(p_ref[...].astype(BF16), w_ref[...])[0:1, :]
        return jnp.where(live, _mm(ch, lo_ref[...]) + sh_ref[1:NCH + 1, 0:n] + jnp.concatenate([const] * G, axis=1), 0.0)

    k2 = blocks(chk_ref, wk_lo_ref, wk_hi_ref, pk_ref, wk_ref)
    v2 = blocks(chv_ref, wv_lo_ref, wv_hi_ref, pv_ref, wv_ref)
    for g in range(G):
        raw = k2[:, 2 * g * HEAD_DIM:(2 * g + 1) * HEAD_DIM]
        rot = k2[:, (2 * g + 1) * HEAD_DIM:(2 * g + 2) * HEAD_DIM]
        kc_ref[g] = (raw * cos_ref[...] + rot * sin_ref[...]).astype(kc_ref.dtype)
        vc_ref[g] = v2[:, g * HEAD_DIM:(g + 1) * HEAD_DIM].astype(vc_ref.dtype)


def _nsa_cmp_select_kernel(q_ref, kc_ref, vc_ref, ovlT_ref, ocmp_ref, sel_ref):
    TQ = q_ref.shape[0]
    NCH = kc_ref.shape[0]
    NSB = ovlT_ref.shape[0]
    q0 = pl.program_id(2) * TQ
    t = q0 + lax.broadcasted_iota(I32, (TQ, 1), 0)
    cend = lax.broadcasted_iota(I32, (1, NCH), 1) * CMP_STRIDE + (CMP_BLK - 1)
    bias = jnp.where(cend <= t, 0.0, NEG_INF)
    kc = kc_ref[...]
    vc = vc_ref[...]
    psum = jnp.zeros((TQ, NCH), F32)
    outs = []
    for j in range(NSA_J):
        qj = q_ref[:, j * HEAD_DIM:(j + 1) * HEAD_DIM]
        s = _nt(qj, kc) * HEAD_DIM ** -0.5 + bias
        m = jnp.maximum(jnp.max(s, axis=-1, keepdims=True), NEG_FLOOR)
        e = jnp.exp(s - m)
        p = e / jnp.maximum(jnp.sum(e, axis=-1, keepdims=True), TINY)
        outs.append(_mm(p.astype(BF16), vc))
        psum = psum + p
    ocmp_ref[...] = jnp.concatenate(outs, axis=1).astype(ocmp_ref.dtype)

    ovl01 = ovlT_ref[...].astype(BF16)
    imp = sum(_nt(ovl01, t) for t in _split3(psum))
    nb = lax.broadcasted_iota(I32, (NSB, 1), 0)
    qblk = (q0 + lax.broadcasted_iota(I32, (1, TQ), 1)) // SEL_BLK
    forced = (nb == 0) | (nb == qblk) | (nb == qblk - 1)
    cur = jnp.where(forced, FORCE_SCORE, jnp.where(nb > qblk, NEG_INF, imp))
    sel = jnp.zeros((NSB, TQ), F32)
    for _ in range(min(N_SEL, NSB)):
        m = jnp.max(cur, axis=0, keepdims=True)
        idx = jnp.min(jnp.where(cur == m, nb, NSB), axis=0, keepdims=True)
        hit = nb == idx
        sel = jnp.where(hit, 1.0, sel)
        cur = jnp.where(hit, BELOW_NEG_INF, cur)
    sel_ref[...] = sel.astype(sel_ref.dtype)


def _nsa_main_kernel(q_ref, ksT_ref, vs_ref, kwT_ref, vw_ref, sel_ref, ocmp_ref, gate_ref, gexp_ref, o_ref):
    TQ = q_ref.shape[0]
    S = vs_ref.shape[0]
    TK = min(NSA_TK, S)
    J = NSA_J
    q0 = pl.program_id(2) * TQ
    q4 = jnp.concatenate([q_ref[:, j * HEAD_DIM:(j + 1) * HEAD_DIM] for j in range(J)], axis=0) * HEAD_DIM ** -0.5
    t = q0 + lax.broadcasted_iota(I32, (TQ, 1), 0)
    selT = sel_ref[...]

    def attend(carry, kT, v1, bias):
        m, acc = carry
        s = _mm(q4, kT) + jnp.concatenate([bias] * J, axis=0)
        m_new = jnp.maximum(m, jnp.max(s, axis=-1, keepdims=True))
        e = jnp.exp(s - m_new)
        return m_new, jnp.exp(m - m_new) * acc + _mm(e.astype(BF16), v1)

    def finish(acc):
        return acc[:, :HEAD_DIM] / jnp.maximum(acc[:, HEAD_DIM:], TINY)

    init = (jnp.full((J * TQ, 1), NEG_FLOOR, F32), jnp.zeros((J * TQ, 2 * HEAD_DIM), F32))

    def body(c, carry):
        k0 = pl.multiple_of(c * TK, TK)
        kpos = k0 + lax.broadcasted_iota(I32, (1, TK), 1)
        expand = jnp.where(kpos // SEL_BLK == lax.broadcasted_iota(I32, (selT.shape[0], 1), 0), 1.0, 0.0).astype(BF16)
        bias = jnp.where((_tn(selT, expand) > 0.5) & (kpos <= t), 0.0, NEG_INF)
        return attend(carry, ksT_ref[:, pl.ds(k0, TK)], vs_ref[pl.ds(k0, TK), :], bias)

    nchunks = (q0 + TQ - 1) // TK + 1
    o_slc = finish(lax.fori_loop(0, nchunks, body, init)[1])

    span = min(NSA_WINDOW + TQ, S)
    start = pl.multiple_of(jnp.clip(q0 - NSA_WINDOW, 0, S - span), TQ)
    dist = t - (start + lax.broadcasted_iota(I32, (1, span), 1))
    bias = jnp.where((dist >= 0) & (dist < NSA_WINDOW), 0.0, NEG_INF)
    o_win = finish(attend(init, kwT_ref[:, pl.ds(start, span)], vw_ref[pl.ds(start, span), :], bias)[1])

    unstack = lambda o: jnp.concatenate([o[j * TQ:(j + 1) * TQ, :] for j in range(J)], axis=1)
    g = _sigmoid(gate_ref[...])
    gx = [_mm01(g, gexp_ref[c]) for c in range(3)]
    out = gx[0] * ocmp_ref[...].astype(F32) + gx[1] * unstack(o_slc) + gx[2] * unstack(o_win)
    o_ref[...] = out.astype(o_ref.dtype)


def _nsa_mixer(x, B, S, w_in, cmp_pos_k, cmp_pos_v, cmp_wk, cmp_wv):
    T = B * S
    H, G, J, dh = NSA_HEADS, NSA_KV_HEADS, NSA_J, HEAD_DIM
    kv = G * dh
    wb = w_in.astype(BF16)
    col = lambda a, n: wb[:, a:a + n]
    o_q, o_kc, o_vc, o_ks, o_vs, o_kw, o_vw, o_g = np.cumsum([0, H * dh] + [kv] * 6).tolist()
    pos = jnp.arange(S)
    w_rope = jnp.concatenate([col(o_q, H * dh), col(o_ks, kv), col(o_kw, kv)], axis=1)
    roped = _proj(x, w_rope, out_dtype=BF16, tm=512, tn=512, rope_tables=_rope_tables(pos, 512), seq=S)
    w_plain = jnp.concatenate([col(o_vs, kv), col(o_vw, kv), jnp.pad(col(o_g, 3 * H), ((0, 0), (0, 128 - 3 * H)))], axis=1)
    plain = _proj(x, w_plain, out_dtype=F32, tm=512, tn=w_plain.shape[1])

    heads_T = lambda a: a.reshape(B, S, G, dh).transpose(0, 2, 3, 1)
    heads = lambda a: a.reshape(B, S, G, dh).transpose(0, 2, 1, 3)
    ksT = heads_T(roped[:, H * dh:H * dh + kv])
    kwT = heads_T(roped[:, H * dh + kv:])
    with_ones = lambda v: jnp.concatenate([v, jnp.ones_like(v)], axis=-1)
    vs = with_ones(heads(plain[:, :kv]).astype(BF16))
    vw = with_ones(heads(plain[:, kv:2 * kv]).astype(BF16))
    gates = plain[:, 2 * kv:]

    nch = S // CMP_STRIDE
    cw = CMP_STRIDE * kv
    chk = _proj(x, col(o_kc, kv), out_dtype=BF16, tm=512, tn=kv).reshape(B, nch, cw)
    chv = _proj(x, col(o_vc, kv), out_dtype=BF16, tm=512, tn=kv).reshape(B, nch, cw)
    wk = cmp_wk.reshape(CMP_BLK * dh, dh)
    wk2 = jnp.concatenate([wk, _rot_cols(wk)], axis=1).astype(BF16)
    wv2 = cmp_wv.reshape(CMP_BLK * dh, dh).astype(BF16)
    eye = jnp.eye(G, dtype=BF16)

    def per_head(w, part):
        n = w.shape[1]
        w3 = w.reshape(2, CMP_STRIDE, dh, n)[part]
        return jnp.einsum('lde,gh->lgdhe', w3, eye).reshape(cw, G * n)

    flat8 = lambda p_: jnp.pad(p_.reshape(1, CMP_BLK * dh), ((0, 7), (0, 0))).astype(F32)
    cend = jnp.arange(nch) * CMP_STRIDE + CMP_BLK - 1
    cos_c, sin_c = _rope_tables(cend, dh)
    fixed1 = lambda b: (0, 0)
    full1 = lambda a: pl.BlockSpec(a.shape, fixed1)
    consts = (per_head(wk2, 0), per_head(wk2, 1), per_head(wv2, 0), per_head(wv2, 1), flat8(cmp_pos_k), flat8(cmp_pos_v),
              wk2, wv2, cos_c, sin_c)
    kc, vc = pl.pallas_call(
        _nsa_compress_kernel,
        name="nsa_compress",
        grid=(B,),
        in_specs=[pl.BlockSpec((None, nch, cw), lambda b: (b, 0, 0))] * 2 + [full1(c) for c in consts],
        out_specs=[pl.BlockSpec((None, G, nch, dh), lambda b: (b, 0, 0, 0))] * 2,
        out_shape=[jax.ShapeDtypeStruct((B, G, nch, dh), BF16)] * 2,
        scratch_shapes=[pltpu.VMEM((nch + 8, 2 * kv), F32)],
        compiler_params=_cparams("parallel"),
    )(chk, chv, *consts)

    nsb = S // SEL_BLK
    c_idx, s_idx = np.arange(nch), np.arange(nsb)
    ovl = ((c_idx[:, None] * CMP_STRIDE + CMP_BLK - 1 >= s_idx[None, :] * SEL_BLK)
           & (c_idx[:, None] * CMP_STRIDE < (s_idx[None, :] + 1) * SEL_BLK)).astype(np.float32)
    bgi = lambda b, g, i: (b, g, 0, 0)
    TS = min(NSA_TQ_SELECT, S)
    ns = S // TS
    sblk = lambda b, g, i: (b * ns + i, g)
    ocmp, sel = pl.pallas_call(
        _nsa_cmp_select_kernel,
        name="nsa_cmp_select",
        grid=(B, G, ns),
        in_specs=[pl.BlockSpec((TS, J * dh), sblk), pl.BlockSpec((None, None, nch, dh), bgi),
                  pl.BlockSpec((None, None, nch, dh), bgi), pl.BlockSpec((nsb, nch), lambda b, g, i: (0, 0))],
        out_specs=[pl.BlockSpec((TS, J * dh), sblk), pl.BlockSpec((None, None, nsb, TS), lambda b, g, i: (b, g, 0, i))],
        out_shape=[jax.ShapeDtypeStruct((T, H * dh), BF16), jax.ShapeDtypeStruct((B, G, nsb, S), BF16)],
        compiler_params=_cparams("parallel", "parallel", "parallel"),
    )(roped, kc, vc, jnp.asarray(ovl.T))
    TQ = NSA_TQ
    nq = S // TQ
    qblk = lambda b, g, i: (b * nq + i, g)

    gexp = np.zeros((G, 3, 128, J * dh), np.float32)
    for g in range(G):
        for c in range(3):
            for j in range(J):
                gexp[g, c, (g * J + j) * 3 + c, j * dh:(j + 1) * dh] = 1.0
    return pl.pallas_call(
        _nsa_main_kernel,
        name="nsa_main",
        grid=(B, G, nq),
        in_specs=[pl.BlockSpec((TQ, J * dh), qblk)] + [pl.BlockSpec((None, None, dh, S), bgi),
                                                       pl.BlockSpec((None, None, S, 2 * dh), bgi)] * 2 + [
            pl.BlockSpec((None, None, nsb, TQ), lambda b, g, i: (b, g, 0, i)),
            pl.BlockSpec((TQ, J * dh), qblk), pl.BlockSpec((TQ, 128), lambda b, g, i: (b * nq + i, 0)),
            pl.BlockSpec((None, 3, 128, J * dh), lambda b, g, i: (g, 0, 0, 0))],
        out_specs=pl.BlockSpec((TQ, J * dh), qblk),
        out_shape=jax.ShapeDtypeStruct((T, H * dh), BF16),
        compiler_params=_cparams("parallel", "parallel", "arbitrary"),
    )(roped, ksT, vs, kwT, vw, sel, ocmp, gates, jnp.asarray(gexp))


def _dil_kernel(q_ref, kp_ref, kc_ref, vp_ref, vc_ref, o_ref, lse_ref):
    NQ = q_ref.shape[0]
    prev_from = jnp.where(pl.program_id(2) == 0, NQ, 0)
    qi = lax.broadcasted_iota(I32, (NQ, 1), 0)
    kj = lax.broadcasted_iota(I32, (1, 2 * NQ), 1)
    dist = NQ + qi - kj
    bias = jnp.where((dist >= 0) & (dist <= NQ) & (kj >= prev_from), 0.0, NEG_INF)
    lane = lax.broadcasted_iota(I32, (NQ, 128), 1)
    first = lane < HEAD_DIM
    lse_all = jnp.zeros((NQ, 128), F32)
    zero = jnp.zeros((), q_ref.dtype)
    for hp in range(DIL_HEADS // 2):
        sl = slice(hp * 128, (hp + 1) * 128)
        q2 = q_ref[:, sl]
        kb = jnp.concatenate([kp_ref[:, sl], kc_ref[:, sl]], axis=0)
        vb = jnp.concatenate([vp_ref[:, sl], vc_ref[:, sl]], axis=0)
        outs = []
        for sub in range(2):
            keep = first if sub == 0 else jnp.logical_not(first)
            s = _nt(jnp.where(keep, q2, zero), kb) * HEAD_DIM ** -0.5 + bias
            m = jnp.max(s, axis=-1, keepdims=True)
            e = jnp.exp(s - m)
            den = jnp.sum(e, axis=-1, keepdims=True)
            outs.append(_mm((e / den).astype(BF16), vb))
            lse_all = jnp.where(lane == 2 * hp + sub, m + jnp.log(den), lse_all)
        o_ref[:, sl] = jnp.where(first, outs[0], outs[1]).astype(o_ref.dtype)
    lse_ref[...] = lse_all


def _proj_res_kernel(*refs, rope, dils):
    n_out = len(dils)
    x_ref, w_ref = refs[:2]
    rest = refs[2:]
    if rope:
        cos_ref, sin_ref = rest[:2]
        rest = rest[2:]
    out_refs, xb_ref, scr_ref = rest[:n_out], rest[n_out], rest[n_out + 1]
    tm, tn = x_ref.shape[0], w_ref.shape[1]

    @pl.when(pl.program_id(1) == 0)
    def _():
        xb_ref[...] = x_ref[...].astype(BF16)

    acc = _mm(xb_ref[...], w_ref[...])
    if rope:
        lane = lax.broadcasted_iota(I32, acc.shape, 1)
        lo = (lane % HEAD_DIM) < ROPE_HALF
        rot = jnp.where(lo, pltpu.roll(acc, tn - ROPE_HALF, 1), pltpu.roll(acc, ROPE_HALF, 1))
        acc = acc * cos_ref[...] + rot * sin_ref[...]
    if any(d > 1 for d in dils):
        for c in range(tn // 128):
            scr_ref[c] = acc[:, c * 128:(c + 1) * 128]
    for o_ref, dil in zip(out_refs, dils):
        if dil == 1:
            o_ref[...] = acc.astype(o_ref.dtype)
            continue
        for r in range(dil):
            for c in range(tn // 128):
                o_ref[:, r * tn + c * 128:r * tn + (c + 1) * 128] = (
                    scr_ref[c, pl.ds(r, tm // dil, stride=dil), :].astype(o_ref.dtype))


def _proj_res(x, w, dils, *, seq, rope, tm=1024, tn=D_MODEL):
    T, K = x.shape
    N = w.shape[1]
    nparts = N // tn
    in_specs = [pl.BlockSpec((tm, K), lambda i, j: (i, 0)), pl.BlockSpec((K, tn), lambda i, j: (0, j))]
    args = [x, w]
    if rope:
        nseq = seq // tm
        in_specs += [pl.BlockSpec((tm, tn), lambda i, j: (i % nseq, 0))] * 2
        args += list(_rope_tables(jnp.arange(seq), tn))
    return pl.pallas_call(
        functools.partial(_proj_res_kernel, rope=rope, dils=dils),
        name="proj_residue_rope" if rope else "proj_residue",
        grid=(T // tm, nparts),
        in_specs=in_specs,
        out_specs=[pl.BlockSpec((tm // d, d * tn), lambda i, j: (i, j)) for d in dils],
        out_shape=[jax.ShapeDtypeStruct((T // d, nparts * d * tn), BF16) for d in dils],
        scratch_shapes=[pltpu.VMEM((tm, K), BF16), pltpu.VMEM((tn // 128, tm, 128), F32)],
        compiler_params=_cparams("parallel", "arbitrary"),
    )(*args)


def _dil_group(qk, v, B, S, dil):
    NQ = DIL_STEPS
    U = S // dil
    nb = U // NQ
    W = D_MODEL
    qk_view = qk.reshape(B, U, 2 * dil * W)
    v_view = v.reshape(B, U, dil * W)
    cur = lambda part: (lambda b, r, n: (b, n, part * dil + r))
    prev = lambda part: (lambda b, r, n: (b, jnp.maximum(n - 1, 0), part * dil + r))
    blk = lambda f: pl.BlockSpec((None, NQ, W), f)
    o, lse = pl.pallas_call(
        _dil_kernel,
        name=f"dilated_attn_{dil}",
        grid=(B, dil, nb),
        in_specs=[blk(cur(0)), blk(prev(1)), blk(cur(1)), blk(prev(0)), blk(cur(0))],
        out_specs=[pl.BlockSpec((None, NQ, W), lambda b, r, n: (b, n, r)),
                   pl.BlockSpec((None, NQ, 128), lambda b, r, n: (b, n, r))],
        out_shape=[jax.ShapeDtypeStruct((B, U, dil * W), BF16), jax.ShapeDtypeStruct((B, U, dil * 128), F32)],
        compiler_params=_cparams("parallel", "parallel", "arbitrary"),
    )(qk_view, qk_view, qk_view, v_view, v_view)
    return o.reshape(B * U, dil * W), lse.reshape(B * U, dil * 128)


def _dil_outproj_ln_kernel(o0_ref, o1_ref, o2_ref, l0_ref, l1_ref, l2_ref, hexp_ref, w_ref, x_ref, g_ref, b_ref, out_ref,
                           oscr_ref, lscr_ref, *, dils):
    tm, W = x_ref.shape

    def natural(o_ref, l_ref, dil):
        if dil == 1:
            return o_ref[...].astype(F32), l_ref[...]
        n = tm // dil
        for r in range(dil):
            lscr_ref[0, pl.ds(r, n, stride=dil), :] = l_ref[:, r * 128:(r + 1) * 128]
            for c in range(W // 128):
                oscr_ref[c, pl.ds(r, n, stride=dil), :] = o_ref[:, r * W + c * 128:r * W + (c + 1) * 128].astype(F32)
        return jnp.concatenate([oscr_ref[c] for c in range(W // 128)], axis=1), lscr_ref[0]

    os_, ls = zip(*[natural(o, l, d) for o, l, d in zip((o0_ref, o1_ref, o2_ref), (l0_ref, l1_ref, l2_ref), dils)])
    m = jnp.maximum(jnp.maximum(ls[0], ls[1]), ls[2])
    es = [jnp.exp(l - m) for l in ls]
    tot = es[0] + es[1] + es[2]
    y = jnp.zeros((tm, W), F32)
    for e, o in zip(es, os_):
        y = y + _mm01(e / tot, hexp_ref[...]) * o
    z = ALPHA * x_ref[...] + _mm(y.astype(BF16), w_ref[...])
    out_ref[...] = _ln_rows(z, g_ref[...], b_ref[...])


def _dilated_layer(x, B, S, w_in, w_out, g, b, *, tm=256):
    T, D = x.shape
    dils = tuple(d for _, d in DIL_GROUPS)
    wb = w_in.astype(BF16)
    vals = _proj_res(x, wb[:, 2 * len(dils) * D:], dils, seq=S, rope=False)
    outs = []
    for gi, dil in enumerate(dils):
        qk, = _proj_res(x, wb[:, 2 * gi * D:(2 * gi + 2) * D], (dil,), seq=S, rope=True)
        outs.append(_dil_group(qk, vals[gi], B, S, dil))
    hexp = np.zeros((128, D), np.float32)
    for h in range(DIL_HEADS):
        hexp[h, h * HEAD_DIM:(h + 1) * HEAD_DIM] = 1.0
    row = lambda i: (i, 0)
    fixed = lambda i: (0, 0)
    return pl.pallas_call(
        functools.partial(_dil_outproj_ln_kernel, dils=dils),
        name="dilated_outproj_ln",
        grid=(T // tm,),
        in_specs=[pl.BlockSpec((tm // d, d * D), row) for d in dils] + [pl.BlockSpec((tm // d, d * 128), row) for d in dils]
        + [pl.BlockSpec((128, D), fixed), pl.BlockSpec((D, D), fixed), pl.BlockSpec((tm, D), row),
           pl.BlockSpec((1, D), fixed), pl.BlockSpec((1, D), fixed)],
        out_specs=pl.BlockSpec((tm, D), row),
        out_shape=jax.ShapeDtypeStruct((T, D), F32),
        scratch_shapes=[pltpu.VMEM((D // 128, tm, 128), F32), pltpu.VMEM((1, tm, 128), F32)],
        compiler_params=_cparams("parallel"),
    )(outs[0][0], outs[1][0], outs[2][0], outs[0][1], outs[1][1], outs[2][1], jnp.asarray(hexp),
      w_out.astype(BF16), x, g.reshape(1, D), b.reshape(1, D))


def _pool_ln_kernel(x_ref, halo_ref, w_ref, scale_ref, g_ref, b_ref, o_ref, ext_ref):
    TS = x_ref.shape[0]
    s = pl.program_id(1)
    x = x_ref[...]
    ext_ref[0:POOL_HALO, :] = jnp.where(s == 0, 0.0, halo_ref[...])
    ext_ref[POOL_HALO:POOL_HALO + TS, :] = x
    cnt = (s * TS + lax.broadcasted_iota(I32, (TS, 1), 0) + 1).astype(F32)
    ys = []
    for gi, w in enumerate(POOL_WINDOWS):
        sl = slice(gi * POOL_GROUP, (gi + 1) * POOL_GROUP)
        xg = x[:, sl]
        tot = xg
        for j in range(1, w):
            tot = tot + ext_ref[POOL_HALO - j:POOL_HALO - j + TS, sl]
        mean = tot / jnp.minimum(cnt, float(w))
        ys.append(_mm((mean - xg).astype(BF16), w_ref[gi]))
    y = jnp.concatenate(ys, axis=1) * scale_ref[...]
    o_ref[...] = _ln_rows(ALPHA * x + y, g_ref[...], b_ref[...])


def _pool_layer(x, B, S, w_grp, scale, g, b, *, ts=512):
    T, D = x.shape
    ns = S // ts
    hb = ts // POOL_HALO
    fixed = lambda bb, s: (0, 0)
    return pl.pallas_call(
        _pool_ln_kernel,
        name="pool_ln",
        grid=(B, ns),
        in_specs=[pl.BlockSpec((ts, D), lambda bb, s: (bb * ns + s, 0)),
                  pl.BlockSpec((POOL_HALO, D), lambda bb, s: (jnp.maximum((bb * ns + s) * hb - 1, 0), 0)),
                  pl.BlockSpec((len(POOL_WINDOWS), POOL_GROUP, POOL_GROUP), lambda bb, s: (0, 0, 0)),
                  pl.BlockSpec((1, D), fixed), pl.BlockSpec((1, D), fixed), pl.BlockSpec((1, D), fixed)],
        out_specs=pl.BlockSpec((ts, D), lambda bb, s: (bb * ns + s, 0)),
        out_shape=jax.ShapeDtypeStruct((T, D), F32),
        scratch_shapes=[pltpu.VMEM((ts + POOL_HALO, D), F32)],
        compiler_params=_cparams("parallel", "arbitrary"),
    )(x, x, w_grp.astype(BF16), scale.reshape(1, D), g.reshape(1, D), b.reshape(1, D))


U32 = jnp.uint32
HI16 = 0xFFFF0000
ROW_WORDS = D_MODEL // 2


def _pack_halves(v):
    h = v.shape[1] // 2
    bits = lambda a: lax.bitcast_convert_type(a.astype(BF16).astype(F32), U32)
    return (bits(v[:, :h]) >> 16) | (bits(v[:, h:]) & U32(HI16))


def _unpack_halves(w):
    lo = lax.bitcast_convert_type(w << 16, F32)
    hi = lax.bitcast_convert_type(w & U32(HI16), F32)
    return jnp.concatenate([lo, hi], axis=1)


def _router_kernel(x_ref, wT_ref, bias_ref, triu_ref, eidx_ref, gw_ref, rank_ref, cnt_ref, xp_ref, carry_ref):
    E = N_EXPERTS
    per = E // N_EXPERT_GROUPS
    TM = x_ref.shape[0]

    @pl.when(pl.program_id(0) == 0)
    def _():
        carry_ref[...] = jnp.zeros_like(carry_ref)

    x = x_ref[...]
    xp_ref[...] = _pack_halves(x)

    scores = _sigmoid(_nt(wT_ref[...], x, precision=HIGHEST))
    biased = scores + bias_ref[...]
    eio = lax.broadcasted_iota(I32, (E, TM), 0)

    gio = lax.broadcasted_iota(I32, (per, TM), 0)
    gscore = []
    for gidx in range(N_EXPERT_GROUPS):
        slab = biased[gidx * per:(gidx + 1) * per, :]
        m1 = jnp.max(slab, axis=0, keepdims=True)
        i1 = jnp.min(jnp.where(slab == m1, gio, per), axis=0, keepdims=True)
        m2 = jnp.max(jnp.where(gio == i1, BELOW_NEG_INF, slab), axis=0, keepdims=True)
        gscore.append(m1 + m2)
    slabs = []
    for gidx in range(N_EXPERT_GROUPS):
        beat = jnp.zeros((1, TM), F32)
        for o in range(N_EXPERT_GROUPS):
            if o == gidx:
                continue
            wins = (gscore[o] >= gscore[gidx]) if o < gidx else (gscore[o] > gscore[gidx])
            beat = beat + jnp.where(wins, 1.0, 0.0)
        keep = beat < float(TOPK_GROUPS)
        slabs.append(jnp.where(keep, biased[gidx * per:(gidx + 1) * per, :], NEG_INF))
    cur = jnp.concatenate(slabs, axis=0)

    picked = jnp.zeros((E, TM), F32)
    idxs, vals = [], []
    for _ in range(TOP_K):
        m = jnp.max(cur, axis=0, keepdims=True)
        idx = jnp.min(jnp.where(cur == m, eio, E), axis=0, keepdims=True)
        hit = eio == idx
        picked = jnp.where(hit, 1.0, picked)
        cur = jnp.where(hit, BELOW_NEG_INF, cur)
        idxs.append(idx)
        vals.append(jnp.sum(jnp.where(hit, scores, 0.0), axis=0, keepdims=True))
    total = vals[0]
    for v in vals[1:]:
        total = total + v

    pos = _mm(picked.astype(BF16), triu_ref[...]) + carry_ref[...]
    for k in range(TOP_K):
        eidx_ref[k:k + 1, :] = idxs[k]
        gw_ref[k:k + 1, :] = vals[k] / total * ROUTED_SCALE
        rank_ref[k:k + 1, :] = jnp.sum(jnp.where(eio == idxs[k], pos, 0.0), axis=0, keepdims=True).astype(I32)
    carry_ref[...] = carry_ref[...] + jnp.sum(picked, axis=1, keepdims=True)
    cnt_ref[...] = carry_ref[...].astype(I32)


def _router(x, router_w, router_bias, *, tm=512):
    T, D = x.shape
    E = N_EXPERTS
    triu = jnp.triu(jnp.ones((tm, tm), F32), k=1).astype(BF16)
    col = lambda i: (0, i)
    fixed = lambda i: (0, 0)
    return pl.pallas_call(
        _router_kernel,
        name="moe_router",
        grid=(T // tm,),
        in_specs=[pl.BlockSpec((tm, D), lambda i: (i, 0)), pl.BlockSpec((E, D), fixed),
                  pl.BlockSpec((E, 1), fixed), pl.BlockSpec((tm, tm), fixed)],
        out_specs=[pl.BlockSpec((TOP_K, tm), col), pl.BlockSpec((TOP_K, tm), col), pl.BlockSpec((TOP_K, tm), col),
                   pl.BlockSpec((E, 1), fixed), pl.BlockSpec((tm, ROW_WORDS), lambda i: (i, 0))],
        out_shape=[jax.ShapeDtypeStruct((TOP_K, T), I32), jax.ShapeDtypeStruct((TOP_K, T), F32),
                   jax.ShapeDtypeStruct((TOP_K, T), I32), jax.ShapeDtypeStruct((E, 1), I32),
                   jax.ShapeDtypeStruct((T, ROW_WORDS), U32)],
        scratch_shapes=[pltpu.VMEM((E, 1), F32)],
        compiler_params=_cparams("arbitrary"),
    )(x, router_w.T.astype(F32), router_bias.reshape(E, 1).astype(F32), triu)


def _tile_indices(dest_hbm, idx_ref, sem_idx, n_idx):
    i = pl.program_id(0)

    def idx_copy(step):
        slot = step % 2
        return pltpu.make_async_copy(dest_hbm.at[step], idx_ref.at[pl.ds(slot * n_idx, n_idx)], sem_idx.at[slot])

    @pl.when(i == 0)
    def _():
        idx_copy(0).start()

    idx_copy(i).wait()

    @pl.when(i + 1 < pl.num_programs(0))
    def _():
        idx_copy(i + 1).start()

    return (i % 2) * n_idx


def _dispatch_kernel(cnt_ref, pstart_ref, dest_hbm, x_ref, rows_ref, idx_ref, zero_ref, sem_idx, sem_rows, sem_zero):
    TD = x_ref.shape[0]

    @pl.when(pl.program_id(0) == 0)
    def _():
        zero_ref[...] = jnp.zeros_like(zero_ref)
        pieces = [1 << b for b in reversed(range(EXP_BLK.bit_length() - 1))]

        def per_expert(e, c):
            n = cnt_ref[e]
            pad = (n + EXP_BLK - 1) // EXP_BLK * EXP_BLK - n
            first = pstart_ref[e] + n

            def piece(size):
                at = first + (pad & ~(2 * size - 1))
                return pltpu.make_async_copy(zero_ref.at[pl.ds(0, size)], rows_ref.at[pl.ds(at, size), 0], sem_zero)

            for size in pieces:
                pl.when((pad & size) != 0)(lambda size=size: piece(size).start())
            for size in pieces:
                pl.when((pad & size) != 0)(lambda size=size: piece(size).wait())
            return c

        lax.fori_loop(0, N_EXPERTS, per_expert, 0)

    base = _tile_indices(dest_hbm, idx_ref, sem_idx, TOP_K * TD)
    scatter = lambda t, k: pltpu.make_async_copy(x_ref.at[pl.ds(t, 1)], rows_ref.at[idx_ref[base + k * TD + t]], sem_rows)

    def issue(t, c):
        for k in range(TOP_K):
            scatter(t, k).start(priority=k % 2)
        return c

    def drain(t, c):
        for k in range(TOP_K):
            scatter(t, k).wait()
        return c

    lax.fori_loop(0, TD, issue, 0, unroll=8)
    lax.fori_loop(0, TD, drain, 0, unroll=8)


def _dispatch(xp, dest_tiles, counts, pad_start, n_rows, *, td):
    T = xp.shape[0]
    return pl.pallas_call(
        _dispatch_kernel,
        name="moe_dispatch",
        grid_spec=pltpu.PrefetchScalarGridSpec(
            num_scalar_prefetch=2,
            grid=(T // td,),
            in_specs=[pl.BlockSpec(memory_space=pl.ANY), pl.BlockSpec((td, ROW_WORDS), lambda i, c, p: (i, 0))],
            out_specs=pl.BlockSpec(memory_space=pl.ANY),
            scratch_shapes=[pltpu.SMEM((2 * TOP_K * td,), I32), pltpu.VMEM((EXP_BLK // 2, ROW_WORDS), U32),
                            pltpu.SemaphoreType.DMA((2,)), pltpu.SemaphoreType.DMA(()), pltpu.SemaphoreType.DMA(())],
        ),
        out_shape=jax.ShapeDtypeStruct((n_rows, 1, ROW_WORDS), U32),
        compiler_params=_cparams("arbitrary"),
    )(counts, pad_start, dest_tiles, xp)


def _expert_kernel(be_ref, nu_ref, rows_hbm, wg_ref, wu_ref, wd_ref, out_hbm, xbuf, obuf, wgu_bf, wd_bf, sem_in, sem_out):
    i = pl.program_id(0)
    nu = nu_ref[0]

    @pl.when((i < nu) & ((i == 0) | (be_ref[i] != be_ref[jnp.maximum(i - 1, 0)])))
    def _():
        wgu_bf[:, :EXPERT_FF] = wg_ref[0].astype(BF16)
        wgu_bf[:, EXPERT_FF:] = wu_ref[0].astype(BF16)
        wd_bf[...] = wd_ref[0].astype(BF16)

    blk = lambda step: pl.ds(pl.multiple_of(step * EXP_BLK, EXP_BLK), EXP_BLK)
    in_copy = lambda step: pltpu.make_async_copy(rows_hbm.at[blk(step), 0], xbuf.at[step % 2], sem_in.at[step % 2])
    out_copy = lambda step: pltpu.make_async_copy(obuf.at[step % 2], out_hbm.at[blk(step), 0], sem_out.at[step % 2])

    @pl.when(i == 0)
    def _():
        in_copy(0).start()

    @pl.when(i < nu)
    def _():
        in_copy(i).wait()

        @pl.when(i + 1 < nu)
        def _():
            in_copy(i + 1).start()

        @pl.when(i >= 2)
        def _():
            out_copy(i - 2).wait()

        slot = i % 2
        gu = _mm(_unpack_halves(xbuf[slot]).astype(BF16), wgu_bf[...])
        h = _silu(gu[:, :EXPERT_FF]) * gu[:, EXPERT_FF:]
        obuf[slot] = _pack_halves(_mm(h.astype(BF16), wd_bf[...]))
        out_copy(i).start()

    @pl.when(i == pl.num_programs(0) - 1)
    def _():
        @pl.when(nu >= 2)
        def _():
            out_copy(nu - 2).wait()

        out_copy(nu - 1).wait()


def _experts(rows, blk_expert, n_used, w_gate, w_up, w_down, layer):
    R = rows.shape[0]
    D = w_down.shape[-1]
    n_blk = R // EXP_BLK
    expert = lambda i, be, nu: (layer, be[jnp.minimum(i, nu[0] - 1)], 0, 0)
    return pl.pallas_call(
        _expert_kernel,
        name="moe_experts",
        grid_spec=pltpu.PrefetchScalarGridSpec(
            num_scalar_prefetch=2,
            grid=(n_blk,),
            in_specs=[pl.BlockSpec(memory_space=pl.ANY), pl.BlockSpec((None, 1, D, EXPERT_FF), expert),
                      pl.BlockSpec((None, 1, D, EXPERT_FF), expert), pl.BlockSpec((None, 1, EXPERT_FF, D), expert)],
            out_specs=pl.BlockSpec(memory_space=pl.ANY),
            scratch_shapes=[pltpu.VMEM((2, EXP_BLK, ROW_WORDS), U32), pltpu.VMEM((2, EXP_BLK, ROW_WORDS), U32),
                            pltpu.VMEM((D, 2 * EXPERT_FF), BF16), pltpu.VMEM((EXPERT_FF, D), BF16),
                            pltpu.SemaphoreType.DMA((2,)), pltpu.SemaphoreType.DMA((2,))],
        ),
        out_shape=jax.ShapeDtypeStruct((R, 1, ROW_WORDS), U32),
        compiler_params=_cparams("arbitrary"),
    )(blk_expert, n_used, rows, w_gate, w_up, w_down)


def _combine_kernel(dest_hbm, gw_ref, x_ref, rows_ref, wsgu_ref, wsd_ref, g_ref, b_ref, p_ref, wp_ref, wpg_ref,
                    o_ref, idx_ref, buf_ref, sem_idx, sem_rows):
    TM = x_ref.shape[0]
    n_idx = TOP_K * TM
    i = pl.program_id(0)
    n = pl.num_programs(0)

    def idx_copy(step):
        slot = step % 3
        return pltpu.make_async_copy(dest_hbm.at[step], idx_ref.at[pl.ds(slot * n_idx, n_idx)], sem_idx.at[slot])

    def row_loop(step, slot, start):
        base = (step % 3) * n_idx

        def body(t, c):
            for k in range(TOP_K):
                cp = pltpu.make_async_copy(rows_ref.at[idx_ref[base + k * TM + t]], buf_ref.at[slot, k, pl.ds(t, 1)],
                                           sem_rows.at[slot])
                if start:
                    cp.start(priority=k % 2)
                else:
                    cp.wait()
            return c

        lax.fori_loop(0, TM, body, 0, unroll=8)

    def for_parity(step, fn):
        for slot in range(2):
            pl.when(step % 2 == slot)(functools.partial(fn, slot))

    @pl.when(i == 0)
    def _():
        idx_copy(0).start()
        idx_copy(0).wait()
        row_loop(0, 0, True)

        @pl.when(n > 1)
        def _():
            idx_copy(1).start()

    @pl.when(i + 1 < n)
    def _():
        idx_copy(i + 1).wait()

        @pl.when(i + 2 < n)
        def _():
            idx_copy(i + 2).start()

        for_parity(i + 1, lambda slot: row_loop(i + 1, slot, True))

    x = x_ref[...]
    gu = _mm(x.astype(BF16), wsgu_ref[...])
    ff = gu.shape[1] // 2
    f = _mm((_silu(gu[:, :ff]) * gu[:, ff:]).astype(BF16), wsd_ref[...])
    for_parity(i, lambda slot: row_loop(i, slot, False))
    gw = gw_ref[...]
    cur = i % 2
    for k in range(TOP_K):
        f = f + gw[:, k:k + 1] * _unpack_halves(buf_ref[cur, k])
    x2 = _ln_rows(ALPHA * x + f, g_ref[...], b_ref[...])
    gate = _sigmoid(_mm(x2.astype(BF16), wpg_ref[...]))
    o_ref[...] = x2 + gate * _mm(p_ref[...].astype(BF16), wp_ref[...])


def _combine(dest_tiles, gw, x, rows_out, wsgu, wsd, g, b, p, wp, wpg, *, tm):
    T, D = x.shape
    row = lambda i: (i, 0)
    fixed = lambda i: (0, 0)
    full = lambda a: pl.BlockSpec(a.shape, fixed)
    return pl.pallas_call(
        _combine_kernel,
        name="moe_combine",
        grid=(T // tm,),
        in_specs=[pl.BlockSpec(memory_space=pl.ANY), pl.BlockSpec((tm, TOP_K), row), pl.BlockSpec((tm, D), row),
                  pl.BlockSpec(memory_space=pl.ANY), full(wsgu), full(wsd), pl.BlockSpec((1, D), fixed),
                  pl.BlockSpec((1, D), fixed), pl.BlockSpec((tm, PLE_DIM), row), full(wp), full(wpg)],
        out_specs=pl.BlockSpec((tm, D), row),
        out_shape=jax.ShapeDtypeStruct((T, D), F32),
        scratch_shapes=[pltpu.SMEM((3 * TOP_K * tm,), I32), pltpu.VMEM((2, TOP_K, tm, ROW_WORDS), U32),
                        pltpu.SemaphoreType.DMA((3,)), pltpu.SemaphoreType.DMA((2,))],
        compiler_params=_cparams("arbitrary"),
    )(dest_tiles, gw, x, rows_out, wsgu, wsd, g.reshape(1, D), b.reshape(1, D), p, wp, wpg)


MOE_TILE = 256


def _moe_ple_layer(x, p, layer, router_w, router_bias, w_gate, w_up, w_down, ws_gate, ws_up, ws_down, g, b, ple_w, ple_gate_w):
    T, D = x.shape
    eidx, gw, rank, counts, xp = _router(x, router_w, router_bias)
    counts = counts.reshape(N_EXPERTS)
    padded = (counts + EXP_BLK - 1) // EXP_BLK * EXP_BLK
    pad_end = jnp.cumsum(padded)
    pad_start = pad_end - padded
    n_blk = T * TOP_K // EXP_BLK + N_EXPERTS
    e_iota = jnp.arange(N_EXPERTS, dtype=I32)
    dest = rank + jnp.sum(jnp.where(eidx[..., None] == e_iota, pad_start, 0), axis=-1)
    tm = MOE_TILE
    dest_tiles = dest.reshape(TOP_K, T // tm, tm).transpose(1, 0, 2).reshape(T // tm, TOP_K * tm)
    blk_first = jnp.arange(n_blk, dtype=I32)[:, None] * EXP_BLK
    blk_expert = jnp.minimum(jnp.sum((pad_end[None, :] <= blk_first).astype(I32), axis=-1), N_EXPERTS - 1)
    n_used = (pad_end[-1:] // EXP_BLK).astype(I32)

    rows = _dispatch(xp, dest_tiles, counts.astype(I32), pad_start.astype(I32), n_blk * EXP_BLK, td=tm)
    rows_out = _experts(rows, blk_expert, n_used, w_gate, w_up, w_down, layer)
    wsgu = jnp.concatenate([ws_gate, ws_up], axis=-1).astype(BF16)
    return _combine(dest_tiles, gw.T, x, rows_out, wsgu, ws_down.astype(BF16), g, b, p,
                    ple_w.astype(BF16), ple_gate_w.astype(BF16), tm=tm)


def kernel(x, p, ln_g, ln_b, mlstm_w_in, mlstm_conv, mlstm_ig_bias, mlstm_fg_bias, mlstm_norm_g, mlstm_w_out, nsa_w_in, nsa_cmp_pos_k, nsa_cmp_pos_v, nsa_cmp_wk, nsa_cmp_wv, nsa_w_out, dil_w_in, dil_w_out, pool_w, pool_scale, router_w, router_bias, exp_w_gate, exp_w_up, exp_w_down, sh_w_gate, sh_w_up, sh_w_down, ple_w, ple_gate_w):
    B, S, D = x.shape
    T = B * S
    xf = x.reshape(T, D)
    pf = p.reshape(DEPTH, T, PLE_DIM)
    for i in range(DEPTH):
        kind, j = i % 4, i // 4
        g1, b1 = ln_g[i, 0], ln_b[i, 0]
        if kind == 0:
            y = _mlstm_mixer(xf, B, S, mlstm_w_in[j], mlstm_conv[j], mlstm_ig_bias[j], mlstm_fg_bias[j], mlstm_norm_g[j])
            xf = _outproj_ln(y, mlstm_w_out[j].astype(BF16), xf, g1, b1)
        elif kind == 1:
            y = _nsa_mixer(xf, B, S, nsa_w_in[j], nsa_cmp_pos_k[j], nsa_cmp_pos_v[j], nsa_cmp_wk[j], nsa_cmp_wv[j])
            xf = _outproj_ln(y, nsa_w_out[j].astype(BF16), xf, g1, b1)
        elif kind == 2:
            xf = _dilated_layer(xf, B, S, dil_w_in[j], dil_w_out[j], g1, b1)
        else:
            xf = _pool_layer(xf, B, S, pool_w[j], pool_scale[j], g1, b1)
        xf = _moe_ple_layer(xf, pf[i], i, router_w[i], router_bias[i], exp_w_gate, exp_w_up, exp_w_down,
                            sh_w_gate[i], sh_w_up[i], sh_w_down[i], ln_g[i, 1], ln_b[i, 1], ple_w[i], ple_gate_w[i])
    return xf.reshape(B, S, D)
```

```python
import functools

import numpy as np
import jax
import jax.numpy as jnp
from jax import lax
from jax.experimental import pallas as pl
from jax.experimental.pallas import tpu as pltpu

F32 = jnp.float32
BF16 = jnp.bfloat16
I32 = jnp.int32
HIGHEST = lax.Precision.HIGHEST

D_MODEL = 1024
DEPTH = 4
ALPHA = (2.0 * DEPTH) ** 0.25
LN_EPS = 1e-5
NEG_INF = -1e30
TINY = 1e-30
BELOW_NEG_INF = -3e38
NEG_FLOOR = 0.999 * NEG_INF
ROPE_THETA = 10000.0
HEAD_DIM = 64
ROPE_HALF = HEAD_DIM // 2

MLSTM_HEADS = 8
MLSTM_QK_DIM = 64
MLSTM_V_DIM = 128
MLSTM_CONV = 4
MLSTM_L = 256

NSA_HEADS = 16
NSA_KV_HEADS = 4
NSA_J = NSA_HEADS // NSA_KV_HEADS
CMP_STRIDE = 16
CMP_BLK = 32
SEL_BLK = 64
N_SEL = 16
NSA_WINDOW = 512
FORCE_SCORE = 1e4
NSA_TQ = 256
NSA_TQ_SELECT = 512
NSA_TK = 512

DIL_HEADS = 16
DIL_GROUPS = ((128, 1), (512, 4), (2048, 16))
DIL_STEPS = 128

POOL_WINDOWS = (2, 4, 8, 16)
POOL_GROUP = 256
POOL_HALO = 16

N_EXPERTS = 64
TOP_K = 8
N_EXPERT_GROUPS = 8
TOPK_GROUPS = 4
EXPERT_FF = 256
ROUTED_SCALE = 2.5
EXP_BLK = 1024
PLE_DIM = 256

VMEM_LIMIT = 48 * 1024 * 1024


def _cparams(*sem):
    return pltpu.CompilerParams(dimension_semantics=sem, vmem_limit_bytes=VMEM_LIMIT)


def _nt(a, b, **kw):
    return lax.dot_general(a, b, (((1,), (1,)), ((), ())), preferred_element_type=F32, **kw)


def _tn(a, b, **kw):
    return lax.dot_general(a, b, (((0,), (0,)), ((), ())), preferred_element_type=F32, **kw)


def _mm(a, b, **kw):
    return jnp.dot(a, b, preferred_element_type=F32, **kw)


def _split3(x):
    a = x.astype(BF16)
    r = x - a.astype(F32)
    b = r.astype(BF16)
    return a, b, (r - b.astype(F32)).astype(BF16)


def _mm01(x, e01):
    eb = e01.astype(BF16)
    a, b, c = _split3(x)
    return _mm(a, eb) + _mm(b, eb) + _mm(c, eb)


def _sigmoid(z):
    return 1.0 / (1.0 + jnp.exp(-z))


def _silu(z):
    return z * _sigmoid(z)


def _ln_rows(z, g, b):
    mu = jnp.mean(z, axis=-1, keepdims=True)
    d = z - mu
    var = jnp.mean(d * d, axis=-1, keepdims=True)
    return d * lax.rsqrt(var + LN_EPS) * g + b


def _proj_kernel(*refs, rope):
    if rope:
        x_ref, w_ref, cos_ref, sin_ref, o_ref, xb_ref = refs
    else:
        x_ref, w_ref, o_ref, xb_ref = refs

    @pl.when(pl.program_id(1) == 0)
    def _():
        xb_ref[...] = x_ref[...].astype(BF16)

    acc = _mm(xb_ref[...], w_ref[...])
    if rope:
        tn = acc.shape[1]
        lane = lax.broadcasted_iota(I32, acc.shape, 1)
        lo = (lane % HEAD_DIM) < ROPE_HALF
        rot = jnp.where(lo, pltpu.roll(acc, tn - ROPE_HALF, 1), pltpu.roll(acc, ROPE_HALF, 1))
        acc = acc * cos_ref[...] + rot * sin_ref[...]
    o_ref[...] = acc.astype(o_ref.dtype)


def _proj(x, w, *, out_dtype, tm, tn, rope_tables=None, seq=None):
    T, K = x.shape
    N = w.shape[1]
    assert T % tm == 0 and N % tn == 0
    in_specs = [pl.BlockSpec((tm, K), lambda i, j: (i, 0)), pl.BlockSpec((K, tn), lambda i, j: (0, j))]
    args = [x, w]
    if rope_tables is not None:
        nseq = seq // tm
        in_specs += [pl.BlockSpec((tm, tn), lambda i, j: (i % nseq, 0))] * 2
        args += list(rope_tables)
    return pl.pallas_call(
        functools.partial(_proj_kernel, rope=rope_tables is not None),
        name="proj_rope" if rope_tables is not None else "proj",
        grid=(T // tm, N // tn),
        in_specs=in_specs,
        out_specs=pl.BlockSpec((tm, tn), lambda i, j: (i, j)),
        out_shape=jax.ShapeDtypeStruct((T, N), out_dtype),
        scratch_shapes=[pltpu.VMEM((tm, K), BF16)],
        compiler_params=_cparams("parallel", "arbitrary"),
    )(*args)


def _rope_tables(pos, width):
    inv = ROPE_THETA ** (-jnp.arange(ROPE_HALF, dtype=F32) / ROPE_HALF)
    ang = pos.astype(F32)[:, None] * inv[None, :]
    cos, sin = jnp.cos(ang), jnp.sin(ang)
    cos64 = jnp.concatenate([cos, cos], -1)
    sin64 = jnp.concatenate([-sin, sin], -1)
    rep = width // HEAD_DIM
    return jnp.tile(cos64, (1, rep)), jnp.tile(sin64, (1, rep))


def _outproj_ln_kernel(y_ref, w_ref, x_ref, g_ref, b_ref, o_ref):
    y = _mm(y_ref[...].astype(BF16), w_ref[...])
    o_ref[...] = _ln_rows(ALPHA * x_ref[...] + y, g_ref[...], b_ref[...])


def _outproj_ln(y, w, x, g, b, *, tm=512):
    T, D = x.shape
    K = y.shape[1]
    row = lambda i: (i, 0)
    fixed = lambda i: (0, 0)
    return pl.pallas_call(
        _outproj_ln_kernel,
        name="outproj_ln",
        grid=(T // tm,),
        in_specs=[pl.BlockSpec((tm, K), row), pl.BlockSpec((K, D), fixed), pl.BlockSpec((tm, D), row),
                  pl.BlockSpec((1, D), fixed), pl.BlockSpec((1, D), fixed)],
        out_specs=pl.BlockSpec((tm, D), row),
        out_shape=jax.ShapeDtypeStruct((T, D), F32),
        compiler_params=_cparams("parallel"),
    )(y, w, x, g.reshape(1, D), b.reshape(1, D))


def _log_sigmoid(z):
    return jnp.minimum(z, 0.0) - jnp.log(1.0 + jnp.exp(-jnp.abs(z)))


def _mlstm_kernel(qk_ref, v_ref, o_ref, gc_ref, gr_ref, convw_ref, bias_c_ref, bias_r_ref, ng_ref,
                  tri_ref, triT_ref, out_ref, C_ref, n_ref, m_ref, ext_ref):
    L = qk_ref.shape[0]
    H, dk, dv = MLSTM_HEADS, MLSTM_QK_DIM, MLSTM_V_DIM

    @pl.when(pl.program_id(1) == 0)
    def _():
        C_ref[...] = jnp.zeros_like(C_ref)
        n_ref[...] = jnp.zeros_like(n_ref)
        m_ref[...] = jnp.zeros_like(m_ref)
        ext_ref[0:8, :] = jnp.zeros((8, ext_ref.shape[1]), F32)

    cur = qk_ref[...]
    ext_ref[8:8 + L, :] = cur
    acc = convw_ref[3:4, :] * cur
    for j in range(MLSTM_CONV - 1):
        acc = acc + convw_ref[j:j + 1, :] * ext_ref[5 + j:5 + j + L, :]
    ext_ref[0:8, :] = cur[L - 8:L, :]
    qk = _silu(acc)

    gc = gc_ref[...] + bias_c_ref[...]
    gr = gr_ref[...] + bias_r_ref[...]
    tri01 = tri_ref[...].astype(BF16)
    b_col = sum(_mm(tri01, t) for t in _split3(_log_sigmoid(gc)))
    b_row = _mm01(_log_sigmoid(gr[H:2 * H, :]), triT_ref[...])
    ig_row = gr[0:H, :]
    tri = lax.broadcasted_iota(I32, (L, L), 0) >= lax.broadcasted_iota(I32, (L, L), 1)

    for h in range(H):
        q = (qk[:, h * dk:(h + 1) * dk] * dk ** -0.5).astype(BF16)
        k = qk[:, H * dk + h * dk:H * dk + (h + 1) * dk]
        kb = k.astype(BF16)
        v = v_ref[:, h * dv:(h + 1) * dv].astype(BF16)
        b_c = b_col[:, H + h:H + h + 1]
        ig_c = gc[:, h:h + 1]
        b_r = b_row[h:h + 1, :]
        m_prev = m_ref[h:h + 1, 0:1]
        C = C_ref[h]
        n = n_ref[h:h + 1, :]

        logD = jnp.where(tri, b_c - b_r + ig_row[h:h + 1, :], NEG_INF)
        inter = b_c + m_prev
        m_t = jnp.maximum(inter, jnp.max(logD, axis=-1, keepdims=True))
        s = _nt(q, kb) * jnp.exp(logD - m_t)
        w_inter = jnp.exp(inter - m_t)
        num = w_inter * _mm(q, C.astype(BF16)) + _mm(s.astype(BF16), v)
        den = w_inter * jnp.sum(q.astype(F32) * n, axis=-1, keepdims=True) + jnp.sum(s, axis=-1, keepdims=True)
        hv = num / jnp.maximum(jnp.abs(den), jnp.exp(-m_t))

        mu = jnp.mean(hv, axis=-1, keepdims=True)
        d = hv - mu
        hn = d * lax.rsqrt(jnp.mean(d * d, axis=-1, keepdims=True) + LN_EPS)
        og = _sigmoid(o_ref[:, h * dv:(h + 1) * dv])
        out_ref[:, h * dv:(h + 1) * dv] = (hn * ng_ref[:, h * dv:(h + 1) * dv] * og).astype(out_ref.dtype)

        bL = b_c[L - 1:L, :]
        logw = bL - b_c + ig_c
        m_new = jnp.maximum(bL + m_prev, jnp.max(logw, axis=0, keepdims=True))
        decay = jnp.exp(bL + m_prev - m_new)
        kw = k * jnp.exp(logw - m_new)
        C_ref[h] = decay * C + _tn(kw.astype(BF16), v)
        n_ref[h:h + 1, :] = decay * n + jnp.sum(kw, axis=0, keepdims=True)
        m_ref[h:h + 1, :] = jnp.broadcast_to(m_new, (1, m_ref.shape[1]))


def _mlstm_mixer(x, B, S, w_in, conv_w, ig_bias, fg_bias, norm_g):
    T = B * S
    H, L = MLSTM_HEADS, MLSTM_L
    wb = w_in.astype(BF16)
    main = _proj(x, wb[:, :3 * D_MODEL], out_dtype=F32, tm=512, tn=3 * D_MODEL)
    wg = jnp.pad(wb[:, 3 * D_MODEL:], ((0, 0), (0, 128 - 2 * H)))
    gates = _proj(x, wg, out_dtype=F32, tm=512, tn=128)
    gates_r = gates[:, :2 * H].reshape(B, S, 2 * H).transpose(0, 2, 1)
    bias16 = jnp.concatenate([ig_bias, fg_bias]).astype(F32)
    bias_c = jnp.pad(bias16, (0, 128 - 2 * H)).reshape(1, 128)
    bias_r = bias16.reshape(2 * H, 1)
    tri = jnp.tril(jnp.ones((L, L), F32))
    nc = S // L
    rowblk = lambda c: (lambda b, i: (b * nc + i, c))
    fixed = lambda b, i: (0, 0)
    return pl.pallas_call(
        _mlstm_kernel,
        name="mlstm",
        grid=(B, nc),
        in_specs=[pl.BlockSpec((L, D_MODEL), rowblk(0)), pl.BlockSpec((L, D_MODEL), rowblk(1)),
                  pl.BlockSpec((L, D_MODEL), rowblk(2)), pl.BlockSpec((L, 128), rowblk(0)),
                  pl.BlockSpec((None, 2 * H, L), lambda b, i: (b, 0, i)),
                  pl.BlockSpec((MLSTM_CONV, D_MODEL), fixed), pl.BlockSpec((1, 128), fixed),
                  pl.BlockSpec((2 * H, 1), fixed), pl.BlockSpec((1, D_MODEL), fixed),
                  pl.BlockSpec((L, L), fixed), pl.BlockSpec((L, L), fixed)],
        out_specs=pl.BlockSpec((L, D_MODEL), rowblk(0)),
        out_shape=jax.ShapeDtypeStruct((T, D_MODEL), BF16),
        scratch_shapes=[pltpu.VMEM((H, MLSTM_QK_DIM, MLSTM_V_DIM), F32), pltpu.VMEM((H, MLSTM_QK_DIM), F32),
                        pltpu.VMEM((H, 128), F32), pltpu.VMEM((L + 8, D_MODEL), F32)],
        compiler_params=_cparams("parallel", "arbitrary"),
    )(main, main, main, gates, gates_r, conv_w.astype(F32), bias_c, bias_r, norm_g.reshape(1, D_MODEL).astype(F32),
      tri, tri.T)


def _rot_cols(w):
    shp = w.shape
    w4 = w.reshape(shp[:-1] + (shp[-1] // HEAD_DIM, 2, ROPE_HALF))
    return jnp.flip(w4, axis=-2).reshape(shp)


def _nsa_compress_kernel(chk_ref, chv_ref, wk_lo_ref, wk_hi_ref, wv_lo_ref, wv_hi_ref, pk_ref, pv_ref, wk_ref, wv_ref,
                         cos_ref, sin_ref, kc_ref, vc_ref, sh_ref):
    NCH = chk_ref.shape[0]
    G = NSA_KV_HEADS
    sh_ref[NCH:NCH + 8, :] = jnp.zeros((8, sh_ref.shape[1]), F32)
    live = lax.broadcasted_iota(I32, (NCH, 1), 0) < NCH - 1

    def blocks(ch_ref, lo_ref, hi_ref, p_ref, w_ref):
        ch = ch_ref[...]
        n = lo_ref.shape[1]
        sh_ref[0:NCH, 0:n] = _mm(ch, hi_ref[...])
        const = _mm(p_ref[...].astype(BF16), w_ref[...])[0:1, :]
        return jnp.where(live, _mm(ch, lo_ref[...]) + sh_ref[1:NCH + 1, 0:n] + jnp.concatenate([const] * G, axis=1), 0.0)

    k2 = blocks(chk_ref, wk_lo_ref, wk_hi_ref, pk_ref, wk_ref)
    v2 = blocks(chv_ref, wv_lo_ref, wv_hi_ref, pv_ref, wv_ref)
    for g in range(G):
        raw = k2[:, 2 * g * HEAD_DIM:(2 * g + 1) * HEAD_DIM]
        rot = k2[:, (2 * g + 1) * HEAD_DIM:(2 * g + 2) * HEAD_DIM]
        kc_ref[g] = (raw * cos_ref[...] + rot * sin_ref[...]).astype(kc_ref.dtype)
        vc_ref[g] = v2[:, g * HEAD_DIM:(g + 1) * HEAD_DIM].astype(vc_ref.dtype)


def _nsa_cmp_select_kernel(q_ref, kc_ref, vc_ref, ovlT_ref, ocmp_ref, sel_ref):
    TQ = q_ref.shape[0]
    NCH = kc_ref.shape[0]
    NSB = ovlT_ref.shape[0]
    q0 = pl.program_id(2) * TQ
    t = q0 + lax.broadcasted_iota(I32, (TQ, 1), 0)
    cend = lax.broadcasted_iota(I32, (1, NCH), 1) * CMP_STRIDE + (CMP_BLK - 1)
    bias = jnp.where(cend <= t, 0.0, NEG_INF)
    kc = kc_ref[...]
    vc = vc_ref[...]
    psum = jnp.zeros((TQ, NCH), F32)
    outs = []
    for j in range(NSA_J):
        qj = q_ref[:, j * HEAD_DIM:(j + 1) * HEAD_DIM]
        s = _nt(qj, kc) * HEAD_DIM ** -0.5 + bias
        m = jnp.maximum(jnp.max(s, axis=-1, keepdims=True), NEG_FLOOR)
        e = jnp.exp(s - m)
        p = e / jnp.maximum(jnp.sum(e, axis=-1, keepdims=True), TINY)
        outs.append(_mm(p.astype(BF16), vc))
        psum = psum + p
    ocmp_ref[...] = jnp.concatenate(outs, axis=1).astype(ocmp_ref.dtype)

    ovl01 = ovlT_ref[...].astype(BF16)
    imp = sum(_nt(ovl01, t) for t in _split3(psum))
    nb = lax.broadcasted_iota(I32, (NSB, 1), 0)
    qblk = (q0 + lax.broadcasted_iota(I32, (1, TQ), 1)) // SEL_BLK
    forced = (nb == 0) | (nb == qblk) | (nb == qblk - 1)
    cur = jnp.where(forced, FORCE_SCORE, jnp.where(nb > qblk, NEG_INF, imp))
    sel = jnp.zeros((NSB, TQ), F32)
    for _ in range(min(N_SEL, NSB)):
        m = jnp.max(cur, axis=0, keepdims=True)
        idx = jnp.min(jnp.where(cur == m, nb, NSB), axis=0, keepdims=True)
        hit = nb == idx
        sel = jnp.where(hit, 1.0, sel)
        cur = jnp.where(hit, BELOW_NEG_INF, cur)
    sel_ref[...] = sel.astype(sel_ref.dtype)


def _nsa_main_kernel(q_ref, ksT_ref, vs_ref, kwT_ref, vw_ref, sel_ref, ocmp_ref, gate_ref, gexp_ref, o_ref):
    TQ = q_ref.shape[0]
    S = vs_ref.shape[0]
    TK = min(NSA_TK, S)
    J = NSA_J
    q0 = pl.program_id(2) * TQ
    q4 = jnp.concatenate([q_ref[:, j * HEAD_DIM:(j + 1) * HEAD_DIM] for j in range(J)], axis=0) * HEAD_DIM ** -0.5
    t = q0 + lax.broadcasted_iota(I32, (TQ, 1), 0)
    selT = sel_ref[...]

    def attend(carry, kT, v1, bias):
        m, acc = carry
        s = _mm(q4, kT) + jnp.concatenate([bias] * J, axis=0)
        m_new = jnp.maximum(m, jnp.max(s, axis=-1, keepdims=True))
        e = jnp.exp(s - m_new)
        return m_new, jnp.exp(m - m_new) * acc + _mm(e.astype(BF16), v1)

    def finish(acc):
        return acc[:, :HEAD_DIM] / jnp.maximum(acc[:, HEAD_DIM:], TINY)

    init = (jnp.full((J * TQ, 1), NEG_FLOOR, F32), jnp.zeros((J * TQ, 2 * HEAD_DIM), F32))

    def body(c, carry):
        k0 = pl.multiple_of(c * TK, TK)
        kpos = k0 + lax.broadcasted_iota(I32, (1, TK), 1)
        expand = jnp.where(kpos // SEL_BLK == lax.broadcasted_iota(I32, (selT.shape[0], 1), 0), 1.0, 0.0).astype(BF16)
        bias = jnp.where((_tn(selT, expand) > 0.5) & (kpos <= t), 0.0, NEG_INF)
        return attend(carry, ksT_ref[:, pl.ds(k0, TK)], vs_ref[pl.ds(k0, TK), :], bias)

    nchunks = (q0 + TQ - 1) // TK + 1
    o_slc = finish(lax.fori_loop(0, nchunks, body, init)[1])

    span = min(NSA_WINDOW + TQ, S)
    start = pl.multiple_of(jnp.clip(q0 - NSA_WINDOW, 0, S - span), TQ)
    dist = t - (start + lax.broadcasted_iota(I32, (1, span), 1))
    bias = jnp.where((dist >= 0) & (dist < NSA_WINDOW), 0.0, NEG_INF)
    o_win = finish(attend(init, kwT_ref[:, pl.ds(start, span)], vw_ref[pl.ds(start, span), :], bias)[1])

    unstack = lambda o: jnp.concatenate([o[j * TQ:(j + 1) * TQ, :] for j in range(J)], axis=1)
    g = _sigmoid(gate_ref[...])
    gx = [_mm01(g, gexp_ref[c]) for c in range(3)]
    out = gx[0] * ocmp_ref[...].astype(F32) + gx[1] * unstack(o_slc) + gx[2] * unstack(o_win)
    o_ref[...] = out.astype(o_ref.dtype)


def _nsa_mixer(x, B, S, w_in, cmp_pos_k, cmp_pos_v, cmp_wk, cmp_wv):
    T = B * S
    H, G, J, dh = NSA_HEADS, NSA_KV_HEADS, NSA_J, HEAD_DIM
    kv = G * dh
    wb = w_in.astype(BF16)
    col = lambda a, n: wb[:, a:a + n]
    o_q, o_kc, o_vc, o_ks, o_vs, o_kw, o_vw, o_g = np.cumsum([0, H * dh] + [kv] * 6).tolist()
    pos = jnp.arange(S)
    w_rope = jnp.concatenate([col(o_q, H * dh), col(o_ks, kv), col(o_kw, kv)], axis=1)
    roped = _proj(x, w_rope, out_dtype=BF16, tm=512, tn=1536, rope_tables=_rope_tables(pos, 1536), seq=S)
    w_plain = jnp.concatenate([col(o_vs, kv), col(o_vw, kv), jnp.pad(col(o_g, 3 * H), ((0, 0), (0, 128 - 3 * H)))], axis=1)
    plain = _proj(x, w_plain, out_dtype=F32, tm=512, tn=w_plain.shape[1])

    heads_T = lambda a: a.reshape(B, S, G, dh).transpose(0, 2, 3, 1)
    heads = lambda a: a.reshape(B, S, G, dh).transpose(0, 2, 1, 3)
    ksT = heads_T(roped[:, H * dh:H * dh + kv])
    kwT = heads_T(roped[:, H * dh + kv:])
    with_ones = lambda v: jnp.concatenate([v, jnp.ones_like(v)], axis=-1)
    vs = with_ones(heads(plain[:, :kv]).astype(BF16))
    vw = with_ones(heads(plain[:, kv:2 * kv]).astype(BF16))
    gates = plain[:, 2 * kv:]

    nch = S // CMP_STRIDE
    cw = CMP_STRIDE * kv
    chk = _proj(x, col(o_kc, kv), out_dtype=BF16, tm=512, tn=kv).reshape(B, nch, cw)
    chv = _proj(x, col(o_vc, kv), out_dtype=BF16, tm=512, tn=kv).reshape(B, nch, cw)
    wk = cmp_wk.reshape(CMP_BLK * dh, dh)
    wk2 = jnp.concatenate([wk, _rot_cols(wk)], axis=1).astype(BF16)
    wv2 = cmp_wv.reshape(CMP_BLK * dh, dh).astype(BF16)
    eye = jnp.eye(G, dtype=BF16)

    def per_head(w, part):
        n = w.shape[1]
        w3 = w.reshape(2, CMP_STRIDE, dh, n)[part]
        return jnp.einsum('lde,gh->lgdhe', w3, eye).reshape(cw, G * n)

    flat8 = lambda p_: jnp.pad(p_.reshape(1, CMP_BLK * dh), ((0, 7), (0, 0))).astype(F32)
    cend = jnp.arange(nch) * CMP_STRIDE + CMP_BLK - 1
    cos_c, sin_c = _rope_tables(cend, dh)
    fixed1 = lambda b: (0, 0)
    full1 = lambda a: pl.BlockSpec(a.shape, fixed1)
    consts = (per_head(wk2, 0), per_head(wk2, 1), per_head(wv2, 0), per_head(wv2, 1), flat8(cmp_pos_k), flat8(cmp_pos_v),
              wk2, wv2, cos_c, sin_c)
    kc, vc = pl.pallas_call(
        _nsa_compress_kernel,
        name="nsa_compress",
        grid=(B,),
        in_specs=[pl.BlockSpec((None, nch, cw), lambda b: (b, 0, 0))] * 2 + [full1(c) for c in consts],
        out_specs=[pl.BlockSpec((None, G, nch, dh), lambda b: (b, 0, 0, 0))] * 2,
        out_shape=[jax.ShapeDtypeStruct((B, G, nch, dh), BF16)] * 2,
        scratch_shapes=[pltpu.VMEM((nch + 8, 2 * kv), F32)],
        compiler_params=_cparams("parallel"),
    )(chk, chv, *consts)

    nsb = S // SEL_BLK
    c_idx, s_idx = np.arange(nch), np.arange(nsb)
    ovl = ((c_idx[:, None] * CMP_STRIDE + CMP_BLK - 1 >= s_idx[None, :] * SEL_BLK)
           & (c_idx[:, None] * CMP_STRIDE < (s_idx[None, :] + 1) * SEL_BLK)).astype(np.float32)
    bgi = lambda b, g, i: (b, g, 0, 0)
    TS = min(NSA_TQ_SELECT, S)
    ns = S // TS
    sblk = lambda b, g, i: (b * ns + i, g)
    ocmp, sel = pl.pallas_call(
        _nsa_cmp_select_kernel,
        name="nsa_cmp_select",
        grid=(B, G, ns),
        in_specs=[pl.BlockSpec((TS, J * dh), sblk), pl.BlockSpec((None, None, nch, dh), bgi),
                  pl.BlockSpec((None, None, nch, dh), bgi), pl.BlockSpec((nsb, nch), lambda b, g, i: (0, 0))],
        out_specs=[pl.BlockSpec((TS, J * dh), sblk), pl.BlockSpec((None, None, nsb, TS), lambda b, g, i: (b, g, 0, i))],
        out_shape=[jax.ShapeDtypeStruct((T, H * dh), BF16), jax.ShapeDtypeStruct((B, G, nsb, S), BF16)],
        compiler_params=_cparams("parallel", "parallel", "parallel"),
    )(roped, kc, vc, jnp.asarray(ovl.T))
    TQ = NSA_TQ
    nq = S // TQ
    qblk = lambda b, g, i: (b * nq + i, g)

    gexp = np.zeros((G, 3, 128, J * dh), np.float32)
    for g in range(G):
        for c in range(3):
            for j in range(J):
                gexp[g, c, (g * J + j) * 3 + c, j * dh:(j + 1) * dh] = 1.0
    return pl.pallas_call(
        _nsa_main_kernel,
        name="nsa_main",
        grid=(B, G, nq),
        in_specs=[pl.BlockSpec((TQ, J * dh), qblk)] + [pl.BlockSpec((None, None, dh, S), bgi),
                                                       pl.BlockSpec((None, None, S, 2 * dh), bgi)] * 2 + [
            pl.BlockSpec((None, None, nsb, TQ), lambda b, g, i: (b, g, 0, i)),
            pl.BlockSpec((TQ, J * dh), qblk), pl.BlockSpec((TQ, 128), lambda b, g, i: (b * nq + i, 0)),
            pl.BlockSpec((None, 3, 128, J * dh), lambda b, g, i: (g, 0, 0, 0))],
        out_specs=pl.BlockSpec((TQ, J * dh), qblk),
        out_shape=jax.ShapeDtypeStruct((T, H * dh), BF16),
        compiler_params=_cparams("parallel", "parallel", "arbitrary"),
    )(roped, ksT, vs, kwT, vw, sel, ocmp, gates, jnp.asarray(gexp))


def _dil_kernel(q_ref, kp_ref, kc_ref, vp_ref, vc_ref, o_ref, lse_ref):
    NQ = q_ref.shape[0]
    prev_from = jnp.where(pl.program_id(2) == 0, NQ, 0)
    qi = lax.broadcasted_iota(I32, (NQ, 1), 0)
    kj = lax.broadcasted_iota(I32, (1, 2 * NQ), 1)
    dist = NQ + qi - kj
    bias = jnp.where((dist >= 0) & (dist <= NQ) & (kj >= prev_from), 0.0, NEG_INF)
    lane = lax.broadcasted_iota(I32, (NQ, 128), 1)
    first = lane < HEAD_DIM
    lse_all = jnp.zeros((NQ, 128), F32)
    zero = jnp.zeros((), q_ref.dtype)
    for hp in range(DIL_HEADS // 2):
        sl = slice(hp * 128, (hp + 1) * 128)
        q2 = q_ref[:, sl]
        kb = jnp.concatenate([kp_ref[:, sl], kc_ref[:, sl]], axis=0)
        vb = jnp.concatenate([vp_ref[:, sl], vc_ref[:, sl]], axis=0)
        outs = []
        for sub in range(2):
            keep = first if sub == 0 else jnp.logical_not(first)
            s = _nt(jnp.where(keep, q2, zero), kb) * HEAD_DIM ** -0.5 + bias
            m = jnp.max(s, axis=-1, keepdims=True)
            e = jnp.exp(s - m)
            den = jnp.sum(e, axis=-1, keepdims=True)
            outs.append(_mm((e / den).astype(BF16), vb))
            lse_all = jnp.where(lane == 2 * hp + sub, m + jnp.log(den), lse_all)
        o_ref[:, sl] = jnp.where(first, outs[0], outs[1]).astype(o_ref.dtype)
    lse_ref[...] = lse_all


def _proj_res_kernel(*refs, rope, dils):
    n_out = len(dils)
    x_ref, w_ref = refs[:2]
    rest = refs[2:]
    if rope:
        cos_ref, sin_ref = rest[:2]
        rest = rest[2:]
    out_refs, xb_ref, scr_ref = rest[:n_out], rest[n_out], rest[n_out + 1]
    tm, tn = x_ref.shape[0], w_ref.shape[1]

    @pl.when(pl.program_id(1) == 0)
    def _():
        xb_ref[...] = x_ref[...].astype(BF16)

    acc = _mm(xb_ref[...], w_ref[...])
    if rope:
        lane = lax.broadcasted_iota(I32, acc.shape, 1)
        lo = (lane % HEAD_DIM) < ROPE_HALF
        rot = jnp.where(lo, pltpu.roll(acc, tn - ROPE_HALF, 1), pltpu.roll(acc, ROPE_HALF, 1))
        acc = acc * cos_ref[...] + rot * sin_ref[...]
    if any(d > 1 for d in dils):
        for c in range(tn // 128):
            scr_ref[c] = acc[:, c * 128:(c + 1) * 128]
    for o_ref, dil in zip(out_refs, dils):
        if dil == 1:
            o_ref[...] = acc.astype(o_ref.dtype)
            continue
        for r in range(dil):
            for c in range(tn // 128):
                o_ref[:, r * tn + c * 128:r * tn + (c + 1) * 128] = (
                    scr_ref[c, pl.ds(r, tm // dil, stride=dil), :].astype(o_ref.dtype))


def _proj_res(x, w, dils, *, seq, rope, tm=1024, tn=D_MODEL):
    T, K = x.shape
    N = w.shape[1]
    nparts = N // tn
    in_specs = [pl.BlockSpec((tm, K), lambda i, j: (i, 0)), pl.BlockSpec((K, tn), lambda i, j: (0, j))]
    args = [x, w]
    if rope:
        nseq = seq // tm
        in_specs += [pl.BlockSpec((tm, tn), lambda i, j: (i % nseq, 0))] * 2
        args += list(_rope_tables(jnp.arange(seq), tn))
    return pl.pallas_call(
        functools.partial(_proj_res_kernel, rope=rope, dils=dils),
        name="proj_residue_rope" if rope else "proj_residue",
        grid=(T // tm, nparts),
        in_specs=in_specs,
        out_specs=[pl.BlockSpec((tm // d, d * tn), lambda i, j: (i, j)) for d in dils],
        out_shape=[jax.ShapeDtypeStruct((T // d, nparts * d * tn), BF16) for d in dils],
        scratch_shapes=[pltpu.VMEM((tm, K), BF16), pltpu.VMEM((tn // 128, tm, 128), F32)],
        compiler_params=_cparams("parallel", "arbitrary"),
    )(*args)


def _dil_group(qk, v, B, S, dil):
    NQ = DIL_STEPS
    U = S // dil
    nb = U // NQ
    W = D_MODEL
    qk_view = qk.reshape(B, U, 2 * dil * W)
    v_view = v.reshape(B, U, dil * W)
    cur = lambda part: (lambda b, r, n: (b, n, part * dil + r))
    prev = lambda part: (lambda b, r, n: (b, jnp.maximum(n - 1, 0), part * dil + r))
    blk = lambda f: pl.BlockSpec((None, NQ, W), f)
    o, lse = pl.pallas_call(
        _dil_kernel,
        name=f"dilated_attn_{dil}",
        grid=(B, dil, nb),
        in_specs=[blk(cur(0)), blk(prev(1)), blk(cur(1)), blk(prev(0)), blk(cur(0))],
        out_specs=[pl.BlockSpec((None, NQ, W), lambda b, r, n: (b, n, r)),
                   pl.BlockSpec((None, NQ, 128), lambda b, r, n: (b, n, r))],
        out_shape=[jax.ShapeDtypeStruct((B, U, dil * W), BF16), jax.ShapeDtypeStruct((B, U, dil * 128), F32)],
        compiler_params=_cparams("parallel", "parallel", "arbitrary"),
    )(qk_view, qk_view, qk_view, v_view, v_view)
    return o.reshape(B * U, dil * W), lse.reshape(B * U, dil * 128)


def _dil_outproj_ln_kernel(o0_ref, o1_ref, o2_ref, l0_ref, l1_ref, l2_ref, hexp_ref, w_ref, x_ref, g_ref, b_ref, out_ref,
                           oscr_ref, lscr_ref, *, dils):
    tm, W = x_ref.shape

    def natural(o_ref, l_ref, dil):
        if dil == 1:
            return o_ref[...].astype(F32), l_ref[...]
        n = tm // dil
        for r in range(dil):
            lscr_ref[0, pl.ds(r, n, stride=dil), :] = l_ref[:, r * 128:(r + 1) * 128]
            for c in range(W // 128):
                oscr_ref[c, pl.ds(r, n, stride=dil), :] = o_ref[:, r * W + c * 128:r * W + (c + 1) * 128].astype(F32)
        return jnp.concatenate([oscr_ref[c] for c in range(W // 128)], axis=1), lscr_ref[0]

    os_, ls = zip(*[natural(o, l, d) for o, l, d in zip((o0_ref, o1_ref, o2_ref), (l0_ref, l1_ref, l2_ref), dils)])
    m = jnp.maximum(jnp.maximum(ls[0], ls[1]), ls[2])
    es = [jnp.exp(l - m) for l in ls]
    tot = es[0] + es[1] + es[2]
    y = jnp.zeros((tm, W), F32)
    for e, o in zip(es, os_):
        y = y + _mm01(e / tot, hexp_ref[...]) * o
    z = ALPHA * x_ref[...] + _mm(y.astype(BF16), w_ref[...])
    out_ref[...] = _ln_rows(z, g_ref[...], b_ref[...])


def _dilated_layer(x, B, S, w_in, w_out, g, b, *, tm=256):
    T, D = x.shape
    dils = tuple(d for _, d in DIL_GROUPS)
    wb = w_in.astype(BF16)
    vals = _proj_res(x, wb[:, 2 * len(dils) * D:], dils, seq=S, rope=False)
    outs = []
    for gi, dil in enumerate(dils):
        qk, = _proj_res(x, wb[:, 2 * gi * D:(2 * gi + 2) * D], (dil,), seq=S, rope=True)
        outs.append(_dil_group(qk, vals[gi], B, S, dil))
    hexp = np.zeros((128, D), np.float32)
    for h in range(DIL_HEADS):
        hexp[h, h * HEAD_DIM:(h + 1) * HEAD_DIM] = 1.0
    row = lambda i: (i, 0)
    fixed = lambda i: (0, 0)
    return pl.pallas_call(
        functools.partial(_dil_outproj_ln_kernel, dils=dils),
        name="dilated_outproj_ln",
        grid=(T // tm,),
        in_specs=[pl.BlockSpec((tm // d, d * D), row) for d in dils] + [pl.BlockSpec((tm // d, d * 128), row) for d in dils]
        + [pl.BlockSpec((128, D), fixed), pl.BlockSpec((D, D), fixed), pl.BlockSpec((tm, D), row),
           pl.BlockSpec((1, D), fixed), pl.BlockSpec((1, D), fixed)],
        out_specs=pl.BlockSpec((tm, D), row),
        out_shape=jax.ShapeDtypeStruct((T, D), F32),
        scratch_shapes=[pltpu.VMEM((D // 128, tm, 128), F32), pltpu.VMEM((1, tm, 128), F32)],
        compiler_params=_cparams("parallel"),
    )(outs[0][0], outs[1][0], outs[2][0], outs[0][1], outs[1][1], outs[2][1], jnp.asarray(hexp),
      w_out.astype(BF16), x, g.reshape(1, D), b.reshape(1, D))


def _pool_ln_kernel(x_ref, halo_ref, w_ref, scale_ref, g_ref, b_ref, o_ref, ext_ref):
    TS = x_ref.shape[0]
    s = pl.program_id(1)
    x = x_ref[...]
    ext_ref[0:POOL_HALO, :] = jnp.where(s == 0, 0.0, halo_ref[...])
    ext_ref[POOL_HALO:POOL_HALO + TS, :] = x
    cnt = (s * TS + lax.broadcasted_iota(I32, (TS, 1), 0) + 1).astype(F32)
    ys = []
    for gi, w in enumerate(POOL_WINDOWS):
        sl = slice(gi * POOL_GROUP, (gi + 1) * POOL_GROUP)
        xg = x[:, sl]
        tot = xg
        for j in range(1, w):
            tot = tot + ext_ref[POOL_HALO - j:POOL_HALO - j + TS, sl]
        mean = tot / jnp.minimum(cnt, float(w))
        ys.append(_mm((mean - xg).astype(BF16), w_ref[gi]))
    y = jnp.concatenate(ys, axis=1) * scale_ref[...]
    o_ref[...] = _ln_rows(ALPHA * x + y, g_ref[...], b_ref[...])


def _pool_layer(x, B, S, w_grp, scale, g, b, *, ts=512):
    T, D = x.shape
    ns = S // ts
    hb = ts // POOL_HALO
    fixed = lambda bb, s: (0, 0)
    return pl.pallas_call(
        _pool_ln_kernel,
        name="pool_ln",
        grid=(B, ns),
        in_specs=[pl.BlockSpec((ts, D), lambda bb, s: (bb * ns + s, 0)),
                  pl.BlockSpec((POOL_HALO, D), lambda bb, s: (jnp.maximum((bb * ns + s) * hb - 1, 0), 0)),
                  pl.BlockSpec((len(POOL_WINDOWS), POOL_GROUP, POOL_GROUP), lambda bb, s: (0, 0, 0)),
                  pl.BlockSpec((1, D), fixed), pl.BlockSpec((1, D), fixed), pl.BlockSpec((1, D), fixed)],
        out_specs=pl.BlockSpec((ts, D), lambda bb, s: (bb * ns + s, 0)),
        out_shape=jax.ShapeDtypeStruct((T, D), F32),
        scratch_shapes=[pltpu.VMEM((ts + POOL_HALO, D), F32)],
        compiler_params=_cparams("parallel", "arbitrary"),
    )(x, x, w_grp.astype(BF16), scale.reshape(1, D), g.reshape(1, D), b.reshape(1, D))


U32 = jnp.uint32
HI16 = 0xFFFF0000
ROW_WORDS = D_MODEL // 2


def _pack_halves(v):
    h = v.shape[1] // 2
    bits = lambda a: lax.bitcast_convert_type(a.astype(BF16).astype(F32), U32)
    return (bits(v[:, :h]) >> 16) | (bits(v[:, h:]) & U32(HI16))


def _unpack_halves(w):
    lo = lax.bitcast_convert_type(w << 16, F32)
    hi = lax.bitcast_convert_type(w & U32(HI16), F32)
    return jnp.concatenate([lo, hi], axis=1)


def _router_kernel(x_ref, wT_ref, bias_ref, triu_ref, eidx_ref, gw_ref, rank_ref, cnt_ref, xp_ref, carry_ref):
    E = N_EXPERTS
    per = E // N_EXPERT_GROUPS
    TM = x_ref.shape[0]

    @pl.when(pl.program_id(0) == 0)
    def _():
        carry_ref[...] = jnp.zeros_like(carry_ref)

    x = x_ref[...]
    xp_ref[...] = _pack_halves(x)

    scores = _sigmoid(_nt(wT_ref[...], x, precision=HIGHEST))
    biased = scores + bias_ref[...]
    eio = lax.broadcasted_iota(I32, (E, TM), 0)

    gio = lax.broadcasted_iota(I32, (per, TM), 0)
    gscore = []
    for gidx in range(N_EXPERT_GROUPS):
        slab = biased[gidx * per:(gidx + 1) * per, :]
        m1 = jnp.max(slab, axis=0, keepdims=True)
        i1 = jnp.min(jnp.where(slab == m1, gio, per), axis=0, keepdims=True)
        m2 = jnp.max(jnp.where(gio == i1, BELOW_NEG_INF, slab), axis=0, keepdims=True)
        gscore.append(m1 + m2)
    slabs = []
    for gidx in range(N_EXPERT_GROUPS):
        beat = jnp.zeros((1, TM), F32)
        for o in range(N_EXPERT_GROUPS):
            if o == gidx:
                continue
            wins = (gscore[o] >= gscore[gidx]) if o < gidx else (gscore[o] > gscore[gidx])
            beat = beat + jnp.where(wins, 1.0, 0.0)
        keep = beat < float(TOPK_GROUPS)
        slabs.append(jnp.where(keep, biased[gidx * per:(gidx + 1) * per, :], NEG_INF))
    cur = jnp.concatenate(slabs, axis=0)

    picked = jnp.zeros((E, TM), F32)
    idxs, vals = [], []
    for _ in range(TOP_K):
        m = jnp.max(cur, axis=0, keepdims=True)
        idx = jnp.min(jnp.where(cur == m, eio, E), axis=0, keepdims=True)
        hit = eio == idx
        picked = jnp.where(hit, 1.0, picked)
        cur = jnp.where(hit, BELOW_NEG_INF, cur)
        idxs.append(idx)
        vals.append(jnp.sum(jnp.where(hit, scores, 0.0), axis=0, keepdims=True))
    total = vals[0]
    for v in vals[1:]:
        total = total + v

    pos = _mm(picked.astype(BF16), triu_ref[...]) + carry_ref[...]
    for k in range(TOP_K):
        eidx_ref[k:k + 1, :] = idxs[k]
        gw_ref[k:k + 1, :] = vals[k] / total * ROUTED_SCALE
        rank_ref[k:k + 1, :] = jnp.sum(jnp.where(eio == idxs[k], pos, 0.0), axis=0, keepdims=True).astype(I32)
    carry_ref[...] = carry_ref[...] + jnp.sum(picked, axis=1, keepdims=True)
    cnt_ref[...] = carry_ref[...].astype(I32)


def _router(x, router_w, router_bias, *, tm=512):
    T, D = x.shape
    E = N_EXPERTS
    triu = jnp.triu(jnp.ones((tm, tm), F32), k=1).astype(BF16)
    col = lambda i: (0, i)
    fixed = lambda i: (0, 0)
    return pl.pallas_call(
        _router_kernel,
        name="moe_router",
        grid=(T // tm,),
        in_specs=[pl.BlockSpec((tm, D), lambda i: (i, 0)), pl.BlockSpec((E, D), fixed),
                  pl.BlockSpec((E, 1), fixed), pl.BlockSpec((tm, tm), fixed)],
        out_specs=[pl.BlockSpec((TOP_K, tm), col), pl.BlockSpec((TOP_K, tm), col), pl.BlockSpec((TOP_K, tm), col),
                   pl.BlockSpec((E, 1), fixed), pl.BlockSpec((tm, ROW_WORDS), lambda i: (i, 0))],
        out_shape=[jax.ShapeDtypeStruct((TOP_K, T), I32), jax.ShapeDtypeStruct((TOP_K, T), F32),
                   jax.ShapeDtypeStruct((TOP_K, T), I32), jax.ShapeDtypeStruct((E, 1), I32),
                   jax.ShapeDtypeStruct((T, ROW_WORDS), U32)],
        scratch_shapes=[pltpu.VMEM((E, 1), F32)],
        compiler_params=_cparams("arbitrary"),
    )(x, router_w.T.astype(F32), router_bias.reshape(E, 1).astype(F32), triu)


def _tile_indices(dest_hbm, idx_ref, sem_idx, n_idx):
    i = pl.program_id(0)

    def idx_copy(step):
        slot = step % 2
        return pltpu.make_async_copy(dest_hbm.at[step], idx_ref.at[pl.ds(slot * n_idx, n_idx)], sem_idx.at[slot])

    @pl.when(i == 0)
    def _():
        idx_copy(0).start()

    idx_copy(i).wait()

    @pl.when(i + 1 < pl.num_programs(0))
    def _():
        idx_copy(i + 1).start()

    return (i % 2) * n_idx


def _dispatch_kernel(cnt_ref, pstart_ref, dest_hbm, x_ref, rows_ref, idx_ref, zero_ref, sem_idx, sem_rows, sem_zero):
    TD = x_ref.shape[0]

    @pl.when(pl.program_id(0) == 0)
    def _():
        zero_ref[...] = jnp.zeros_like(zero_ref)
        pieces = [1 << b for b in reversed(range(EXP_BLK.bit_length() - 1))]

        def per_expert(e, c):
            n = cnt_ref[e]
            pad = (n + EXP_BLK - 1) // EXP_BLK * EXP_BLK - n
            first = pstart_ref[e] + n

            def piece(size):
                at = first + (pad & ~(2 * size - 1))
                return pltpu.make_async_copy(zero_ref.at[pl.ds(0, size)], rows_ref.at[pl.ds(at, size), 0], sem_zero)

            for size in pieces:
                pl.when((pad & size) != 0)(lambda size=size: piece(size).start())
            for size in pieces:
                pl.when((pad & size) != 0)(lambda size=size: piece(size).wait())
            return c

        lax.fori_loop(0, N_EXPERTS, per_expert, 0)

    base = _tile_indices(dest_hbm, idx_ref, sem_idx, TOP_K * TD)
    scatter = lambda t, k: pltpu.make_async_copy(x_ref.at[pl.ds(t, 1)], rows_ref.at[idx_ref[base + k * TD + t]], sem_rows)

    def issue(t, c):
        for k in range(TOP_K):
            scatter(t, k).start(priority=k % 2)
        return c

    def drain(t, c):
        for k in range(TOP_K):
            scatter(t, k).wait()
        return c

    lax.fori_loop(0, TD, issue, 0, unroll=8)
    lax.fori_loop(0, TD, drain, 0, unroll=8)


def _dispatch(xp, dest_tiles, counts, pad_start, n_rows, *, td):
    T = xp.shape[0]
    return pl.pallas_call(
        _dispatch_kernel,
        name="moe_dispatch",
        grid_spec=pltpu.PrefetchScalarGridSpec(
            num_scalar_prefetch=2,
            grid=(T // td,),
            in_specs=[pl.BlockSpec(memory_space=pl.ANY), pl.BlockSpec((td, ROW_WORDS), lambda i, c, p: (i, 0))],
            out_specs=pl.BlockSpec(memory_space=pl.ANY),
            scratch_shapes=[pltpu.SMEM((2 * TOP_K * td,), I32), pltpu.VMEM((EXP_BLK // 2, ROW_WORDS), U32),
                            pltpu.SemaphoreType.DMA((2,)), pltpu.SemaphoreType.DMA(()), pltpu.SemaphoreType.DMA(())],
        ),
        out_shape=jax.ShapeDtypeStruct((n_rows, 1, ROW_WORDS), U32),
        compiler_params=_cparams("arbitrary"),
    )(counts, pad_start, dest_tiles, xp)


def _expert_kernel(be_ref, nu_ref, rows_hbm, wg_ref, wu_ref, wd_ref, out_hbm, xbuf, obuf, wgu_bf, wd_bf, sem_in, sem_out):
    i = pl.program_id(0)
    nu = nu_ref[0]

    @pl.when((i < nu) & ((i == 0) | (be_ref[i] != be_ref[jnp.maximum(i - 1, 0)])))
    def _():
        wgu_bf[:, :EXPERT_FF] = wg_ref[0].astype(BF16)
        wgu_bf[:, EXPERT_FF:] = wu_ref[0].astype(BF16)
        wd_bf[...] = wd_ref[0].astype(BF16)

    blk = lambda step: pl.ds(pl.multiple_of(step * EXP_BLK, EXP_BLK), EXP_BLK)
    in_copy = lambda step: pltpu.make_async_copy(rows_hbm.at[blk(step), 0], xbuf.at[step % 2], sem_in.at[step % 2])
    out_copy = lambda step: pltpu.make_async_copy(obuf.at[step % 2], out_hbm.at[blk(step), 0], sem_out.at[step % 2])

    @pl.when(i == 0)
    def _():
        in_copy(0).start()

    @pl.when(i < nu)
    def _():
        in_copy(i).wait()

        @pl.when(i + 1 < nu)
        def _():
            in_copy(i + 1).start()

        @pl.when(i >= 2)
        def _():
            out_copy(i - 2).wait()

        slot = i % 2
        gu = _mm(_unpack_halves(xbuf[slot]).astype(BF16), wgu_bf[...])
        h = _silu(gu[:, :EXPERT_FF]) * gu[:, EXPERT_FF:]
        obuf[slot] = _pack_halves(_mm(h.astype(BF16), wd_bf[...]))
        out_copy(i).start()

    @pl.when(i == pl.num_programs(0) - 1)
    def _():
        @pl.when(nu >= 2)
        def _():
            out_copy(nu - 2).wait()

        out_copy(nu - 1).wait()


def _experts(rows, blk_expert, n_used, w_gate, w_up, w_down, layer):
    R = rows.shape[0]
    D = w_down.shape[-1]
    n_blk = R // EXP_BLK
    expert = lambda i, be, nu: (layer, be[jnp.minimum(i, nu[0] - 1)], 0, 0)
    return pl.pallas_call(
        _expert_kernel,
        name="moe_experts",
        grid_spec=pltpu.PrefetchScalarGridSpec(
            num_scalar_prefetch=2,
            grid=(n_blk,),
            in_specs=[pl.BlockSpec(memory_space=pl.ANY), pl.BlockSpec((None, 1, D, EXPERT_FF), expert),
                      pl.BlockSpec((None, 1, D, EXPERT_FF), expert), pl.BlockSpec((None, 1, EXPERT_FF, D), expert)],
            out_specs=pl.BlockSpec(memory_space=pl.ANY),
            scratch_shapes=[pltpu.VMEM((2, EXP_BLK, ROW_WORDS), U32), pltpu.VMEM((2, EXP_BLK, ROW_WORDS), U32),
                            pltpu.VMEM((D, 2 * EXPERT_FF), BF16), pltpu.VMEM((EXPERT_FF, D), BF16),
                            pltpu.SemaphoreType.DMA((2,)), pltpu.SemaphoreType.DMA((2,))],
        ),
        out_shape=jax.ShapeDtypeStruct((R, 1, ROW_WORDS), U32),
        compiler_params=_cparams("arbitrary"),
    )(blk_expert, n_used, rows, w_gate, w_up, w_down)


def _combine_kernel(dest_hbm, gw_ref, x_ref, rows_ref, wsgu_ref, wsd_ref, g_ref, b_ref, p_ref, wp_ref, wpg_ref,
                    o_ref, idx_ref, buf_ref, sem_idx, sem_rows):
    TM = x_ref.shape[0]
    n_idx = TOP_K * TM
    i = pl.program_id(0)
    n = pl.num_programs(0)

    def idx_copy(step):
        slot = step % 3
        return pltpu.make_async_copy(dest_hbm.at[step], idx_ref.at[pl.ds(slot * n_idx, n_idx)], sem_idx.at[slot])

    def row_loop(step, slot, start):
        base = (step % 3) * n_idx

        def body(t, c):
            for k in range(TOP_K):
                cp = pltpu.make_async_copy(rows_ref.at[idx_ref[base + k * TM + t]], buf_ref.at[slot, k, pl.ds(t, 1)],
                                           sem_rows.at[slot])
                if start:
                    cp.start(priority=k % 2)
                else:
                    cp.wait()
            return c

        lax.fori_loop(0, TM, body, 0, unroll=8)

    def for_parity(step, fn):
        for slot in range(2):
            pl.when(step % 2 == slot)(functools.partial(fn, slot))

    @pl.when(i == 0)
    def _():
        idx_copy(0).start()
        idx_copy(0).wait()
        row_loop(0, 0, True)

        @pl.when(n > 1)
        def _():
            idx_copy(1).start()

    @pl.when(i + 1 < n)
    def _():
        idx_copy(i + 1).wait()

        @pl.when(i + 2 < n)
        def _():
            idx_copy(i + 2).start()

        for_parity(i + 1, lambda slot: row_loop(i + 1, slot, True))

    x = x_ref[...]
    gu = _mm(x.astype(BF16), wsgu_ref[...])
    ff = gu.shape[1] // 2
    f = _mm((_silu(gu[:, :ff]) * gu[:, ff:]).astype(BF16), wsd_ref[...])
    for_parity(i, lambda slot: row_loop(i, slot, False))
    gw = gw_ref[...]
    cur = i % 2
    for k in range(TOP_K):
        f = f + gw[:, k:k + 1] * _unpack_halves(buf_ref[cur, k])
    x2 = _ln_rows(ALPHA * x + f, g_ref[...], b_ref[...])
    gate = _sigmoid(_mm(x2.astype(BF16), wpg_ref[...]))
    o_ref[...] = x2 + gate * _mm(p_ref[...].astype(BF16), wp_ref[...])


def _combine(dest_tiles, gw, x, rows_out, wsgu, wsd, g, b, p, wp, wpg, *, tm):
    T, D = x.shape
    row = lambda i: (i, 0)
    fixed = lambda i: (0, 0)
    full = lambda a: pl.BlockSpec(a.shape, fixed)
    return pl.pallas_call(
        _combine_kernel,
        name="moe_combine",
        grid=(T // tm,),
        in_specs=[pl.BlockSpec(memory_space=pl.ANY), pl.BlockSpec((tm, TOP_K), row), pl.BlockSpec((tm, D), row),
                  pl.BlockSpec(memory_space=pl.ANY), full(wsgu), full(wsd), pl.BlockSpec((1, D), fixed),
                  pl.BlockSpec((1, D), fixed), pl.BlockSpec((tm, PLE_DIM), row), full(wp), full(wpg)],
        out_specs=pl.BlockSpec((tm, D), row),
        out_shape=jax.ShapeDtypeStruct((T, D), F32),
        scratch_shapes=[pltpu.SMEM((3 * TOP_K * tm,), I32), pltpu.VMEM((2, TOP_K, tm, ROW_WORDS), U32),
                        pltpu.SemaphoreType.DMA((3,)), pltpu.SemaphoreType.DMA((2,))],
        compiler_params=_cparams("arbitrary"),
    )(dest_tiles, gw, x, rows_out, wsgu, wsd, g.reshape(1, D), b.reshape(1, D), p, wp, wpg)


MOE_TILE = 256


def _moe_ple_layer(x, p, layer, router_w, router_bias, w_gate, w_up, w_down, ws_gate, ws_up, ws_down, g, b, ple_w, ple_gate_w):
    T, D = x.shape
    eidx, gw, rank, counts, xp = _router(x, router_w, router_bias)
    counts = counts.reshape(N_EXPERTS)
    padded = (counts + EXP_BLK - 1) // EXP_BLK * EXP_BLK
    pad_end = jnp.cumsum(padded)
    pad_start = pad_end - padded
    n_blk = T * TOP_K // EXP_BLK + N_EXPERTS
    e_iota = jnp.arange(N_EXPERTS, dtype=I32)
    dest = rank + jnp.sum(jnp.where(eidx[..., None] == e_iota, pad_start, 0), axis=-1)
    tm = MOE_TILE
    dest_tiles = dest.reshape(TOP_K, T // tm, tm).transpose(1, 0, 2).reshape(T // tm, TOP_K * tm)
    blk_first = jnp.arange(n_blk, dtype=I32)[:, None] * EXP_BLK
    blk_expert = jnp.minimum(jnp.sum((pad_end[None, :] <= blk_first).astype(I32), axis=-1), N_EXPERTS - 1)
    n_used = (pad_end[-1:] // EXP_BLK).astype(I32)

    rows = _dispatch(xp, dest_tiles, counts.astype(I32), pad_start.astype(I32), n_blk * EXP_BLK, td=tm)
    rows_out = _experts(rows, blk_expert, n_used, w_gate, w_up, w_down, layer)
    wsgu = jnp.concatenate([ws_gate, ws_up], axis=-1).astype(BF16)
    return _combine(dest_tiles, gw.T, x, rows_out, wsgu, ws_down.astype(BF16), g, b, p,
                    ple_w.astype(BF16), ple_gate_w.astype(BF16), tm=tm)


def kernel(x, p, ln_g, ln_b, mlstm_w_in, mlstm_conv, mlstm_ig_bias, mlstm_fg_bias, mlstm_norm_g, mlstm_w_out, nsa_w_in, nsa_cmp_pos_k, nsa_cmp_pos_v, nsa_cmp_wk, nsa_cmp_wv, nsa_w_out, dil_w_in, dil_w_out, pool_w, pool_scale, router_w, router_bias, exp_w_gate, exp_w_up, exp_w_down, sh_w_gate, sh_w_up, sh_w_down, ple_w, ple_gate_w):
    B, S, D = x.shape
    T = B * S
    xf = x.reshape(T, D)
    pf = p.reshape(DEPTH, T, PLE_DIM)
    for i in range(DEPTH):
        kind, j = i % 4, i // 4
        g1, b1 = ln_g[i, 0], ln_b[i, 0]
        if kind == 0:
            y = _mlstm_mixer(xf, B, S, mlstm_w_in[j], mlstm_conv[j], mlstm_ig_bias[j], mlstm_fg_bias[j], mlstm_norm_g[j])
            xf = _outproj_ln(y, mlstm_w_out[j].astype(BF16), xf, g1, b1)
        elif kind == 1:
            y = _nsa_mixer(xf, B, S, nsa_w_in[j], nsa_cmp_pos_k[j], nsa_cmp_pos_v[j], nsa_cmp_wk[j], nsa_cmp_wv[j])
            xf = _outproj_ln(y, nsa_w_out[j].astype(BF16), xf, g1, b1)
        elif kind == 2:
            xf = _dilated_layer(xf, B, S, dil_w_in[j], dil_w_out[j], g1, b1)
        else:
            xf = _pool_layer(xf, B, S, pool_w[j], pool_scale[j], g1, b1)
        xf = _moe_ple_layer(xf, pf[i], i, router_w[i], router_bias[i], exp_w_gate, exp_w_up, exp_w_down,
                            sh_w_gate[i], sh_w_up[i], sh_w_down[i], ln_g[i, 1], ln_b[i, 1], ple_w[i], ple_gate_w[i])
    return xf.reshape(B, S, D)
```

```python
import functools

import numpy as np
import jax
import jax.numpy as jnp
from jax import lax
from jax.experimental import pallas as pl
from jax.experimental.pallas import tpu as pltpu

F32 = jnp.float32
BF16 = jnp.bfloat16
I32 = jnp.int32
HIGHEST = lax.Precision.HIGHEST

D_MODEL = 1024
DEPTH = 4
ALPHA = (2.0 * DEPTH) ** 0.25
LN_EPS = 1e-5
NEG_INF = -1e30
TINY = 1e-30
BELOW_NEG_INF = -3e38
NEG_FLOOR = 0.999 * NEG_INF
ROPE_THETA = 10000.0
HEAD_DIM = 64
ROPE_HALF = HEAD_DIM // 2

MLSTM_HEADS = 8
MLSTM_QK_DIM = 64
MLSTM_V_DIM = 128
MLSTM_CONV = 4
MLSTM_L = 256

NSA_HEADS = 16
NSA_KV_HEADS = 4
NSA_J = NSA_HEADS // NSA_KV_HEADS
CMP_STRIDE = 16
CMP_BLK = 32
SEL_BLK = 64
N_SEL = 16
NSA_WINDOW = 512
FORCE_SCORE = 1e4
NSA_TQ = 256
NSA_TQ_SELECT = 512
NSA_TK = 512

DIL_HEADS = 16
DIL_GROUPS = ((128, 1), (512, 4), (2048, 16))
DIL_STEPS = 128

POOL_WINDOWS = (2, 4, 8, 16)
POOL_GROUP = 256
POOL_HALO = 16

N_EXPERTS = 64
TOP_K = 8
N_EXPERT_GROUPS = 8
TOPK_GROUPS = 4
EXPERT_FF = 256
ROUTED_SCALE = 2.5
EXP_BLK = 1024
PLE_DIM = 256

VMEM_LIMIT = 48 * 1024 * 1024


def _cparams(*sem):
    return pltpu.CompilerParams(dimension_semantics=sem, vmem_limit_bytes=VMEM_LIMIT)


def _nt(a, b, **kw):
    return lax.dot_general(a, b, (((1,), (1,)), ((), ())), preferred_element_type=F32, **kw)


def _tn(a, b, **kw):
    return lax.dot_general(a, b, (((0,), (0,)), ((), ())), preferred_element_type=F32, **kw)


def _mm(a, b, **kw):
    return jnp.dot(a, b, preferred_element_type=F32, **kw)


def _split3(x):
    a = x.astype(BF16)
    r = x - a.astype(F32)
    b = r.astype(BF16)
    return a, b, (r - b.astype(F32)).astype(BF16)


def _mm01(x, e01):
    eb = e01.astype(BF16)
    a, b, c = _split3(x)
    return _mm(a, eb) + _mm(b, eb) + _mm(c, eb)


def _sigmoid(z):
    return 1.0 / (1.0 + jnp.exp(-z))


def _silu(z):
    return z * _sigmoid(z)


def _ln_rows(z, g, b):
    mu = jnp.mean(z, axis=-1, keepdims=True)
    d = z - mu
    var = jnp.mean(d * d, axis=-1, keepdims=True)
    return d * lax.rsqrt(var + LN_EPS) * g + b


def _proj_kernel(*refs, rope):
    if rope:
        x_ref, w_ref, cos_ref, sin_ref, o_ref, xb_ref = refs
    else:
        x_ref, w_ref, o_ref, xb_ref = refs

    @pl.when(pl.program_id(1) == 0)
    def _():
        xb_ref[...] = x_ref[...].astype(BF16)

    acc = _mm(xb_ref[...], w_ref[...])
    if rope:
        tn = acc.shape[1]
        lane = lax.broadcasted_iota(I32, acc.shape, 1)
        lo = (lane % HEAD_DIM) < ROPE_HALF
        rot = jnp.where(lo, pltpu.roll(acc, tn - ROPE_HALF, 1), pltpu.roll(acc, ROPE_HALF, 1))
        acc = acc * cos_ref[...] + rot * sin_ref[...]
    o_ref[...] = acc.astype(o_ref.dtype)


def _proj(x, w, *, out_dtype, tm, tn, rope_tables=None, seq=None):
    T, K = x.shape
    N = w.shape[1]
    assert T % tm == 0 and N % tn == 0
    in_specs = [pl.BlockSpec((tm, K), lambda i, j: (i, 0)), pl.BlockSpec((K, tn), lambda i, j: (0, j))]
    args = [x, w]
    if rope_tables is not None:
        nseq = seq // tm
        in_specs += [pl.BlockSpec((tm, tn), lambda i, j: (i % nseq, 0))] * 2
        args += list(rope_tables)
    return pl.pallas_call(
        functools.partial(_proj_kernel, rope=rope_tables is not None),
        name="proj_rope" if rope_tables is not None else "proj",
        grid=(T // tm, N // tn),
        in_specs=in_specs,
        out_specs=pl.BlockSpec((tm, tn), lambda i, j: (i, j)),
        out_shape=jax.ShapeDtypeStruct((T, N), out_dtype),
        scratch_shapes=[pltpu.VMEM((tm, K), BF16)],
        compiler_params=_cparams("parallel", "arbitrary"),
    )(*args)


def _rope_tables(pos, width):
    inv = ROPE_THETA ** (-jnp.arange(ROPE_HALF, dtype=F32) / ROPE_HALF)
    ang = pos.astype(F32)[:, None] * inv[None, :]
    cos, sin = jnp.cos(ang), jnp.sin(ang)
    cos64 = jnp.concatenate([cos, cos], -1)
    sin64 = jnp.concatenate([-sin, sin], -1)
    rep = width // HEAD_DIM
    return jnp.tile(cos64, (1, rep)), jnp.tile(sin64, (1, rep))


def _outproj_ln_kernel(y_ref, w_ref, x_ref, g_ref, b_ref, o_ref):
    y = _mm(y_ref[...].astype(BF16), w_ref[...])
    o_ref[...] = _ln_rows(ALPHA * x_ref[...] + y, g_ref[...], b_ref[...])


def _outproj_ln(y, w, x, g, b, *, tm=512):
    T, D = x.shape
    K = y.shape[1]
    row = lambda i: (i, 0)
    fixed = lambda i: (0, 0)
    return pl.pallas_call(
        _outproj_ln_kernel,
        name="outproj_ln",
        grid=(T // tm,),
        in_specs=[pl.BlockSpec((tm, K), row), pl.BlockSpec((K, D), fixed), pl.BlockSpec((tm, D), row),
                  pl.BlockSpec((1, D), fixed), pl.BlockSpec((1, D), fixed)],
        out_specs=pl.BlockSpec((tm, D), row),
        out_shape=jax.ShapeDtypeStruct((T, D), F32),
        compiler_params=_cparams("parallel"),
    )(y, w, x, g.reshape(1, D), b.reshape(1, D))


def _log_sigmoid(z):
    return jnp.minimum(z, 0.0) - jnp.log(1.0 + jnp.exp(-jnp.abs(z)))


def _mlstm_kernel(qk_ref, v_ref, o_ref, gc_ref, gr_ref, convw_ref, bias_c_ref, bias_r_ref, ng_ref,
                  tri_ref, triT_ref, out_ref, C_ref, n_ref, m_ref, ext_ref):
    L = qk_ref.shape[0]
    H, dk, dv = MLSTM_HEADS, MLSTM_QK_DIM, MLSTM_V_DIM

    @pl.when(pl.program_id(1) == 0)
    def _():
        C_ref[...] = jnp.zeros_like(C_ref)
        n_ref[...] = jnp.zeros_like(n_ref)
        m_ref[...] = jnp.zeros_like(m_ref)
        ext_ref[0:8, :] = jnp.zeros((8, ext_ref.shape[1]), F32)

    cur = qk_ref[...]
    ext_ref[8:8 + L, :] = cur
    acc = convw_ref[3:4, :] * cur
    for j in range(MLSTM_CONV - 1):
        acc = acc + convw_ref[j:j + 1, :] * ext_ref[5 + j:5 + j + L, :]
    ext_ref[0:8, :] = cur[L - 8:L, :]
    qk = _silu(acc)

    gc = gc_ref[...] + bias_c_ref[...]
    gr = gr_ref[...] + bias_r_ref[...]
    tri01 = tri_ref[...].astype(BF16)
    b_col = sum(_mm(tri01, t) for t in _split3(_log_sigmoid(gc)))
    b_row = _mm01(_log_sigmoid(gr[H:2 * H, :]), triT_ref[...])
    ig_row = gr[0:H, :]
    tri = lax.broadcasted_iota(I32, (L, L), 0) >= lax.broadcasted_iota(I32, (L, L), 1)

    for h in range(H):
        q = (qk[:, h * dk:(h + 1) * dk] * dk ** -0.5).astype(BF16)
        k = qk[:, H * dk + h * dk:H * dk + (h + 1) * dk]
        kb = k.astype(BF16)
        v = v_ref[:, h * dv:(h + 1) * dv].astype(BF16)
        b_c = b_col[:, H + h:H + h + 1]
        ig_c = gc[:, h:h + 1]
        b_r = b_row[h:h + 1, :]
        m_prev = m_ref[h:h + 1, 0:1]
        C = C_ref[h]
        n = n_ref[h:h + 1, :]

        logD = jnp.where(tri, b_c - b_r + ig_row[h:h + 1, :], NEG_INF)
        inter = b_c + m_prev
        m_t = jnp.maximum(inter, jnp.max(logD, axis=-1, keepdims=True))
        s = _nt(q, kb) * jnp.exp(logD - m_t)
        w_inter = jnp.exp(inter - m_t)
        num = w_inter * _mm(q, C.astype(BF16)) + _mm(s.astype(BF16), v)
        den = w_inter * jnp.sum(q.astype(F32) * n, axis=-1, keepdims=True) + jnp.sum(s, axis=-1, keepdims=True)
        hv = num / jnp.maximum(jnp.abs(den), jnp.exp(-m_t))

        mu = jnp.mean(hv, axis=-1, keepdims=True)
        d = hv - mu
        hn = d * lax.rsqrt(jnp.mean(d * d, axis=-1, keepdims=True) + LN_EPS)
        og = _sigmoid(o_ref[:, h * dv:(h + 1) * dv])
        out_ref[:, h * dv:(h + 1) * dv] = (hn * ng_ref[:, h * dv:(h + 1) * dv] * og).astype(out_ref.dtype)

        bL = b_c[L - 1:L, :]
        logw = bL - b_c + ig_c
        m_new = jnp.maximum(bL + m_prev, jnp.max(logw, axis=0, keepdims=True))
        decay = jnp.exp(bL + m_prev - m_new)
        kw = k * jnp.exp(logw - m_new)
        C_ref[h] = decay * C + _tn(kw.astype(BF16), v)
        n_ref[h:h + 1, :] = decay * n + jnp.sum(kw, axis=0, keepdims=True)
        m_ref[h:h + 1, :] = jnp.broadcast_to(m_new, (1, m_ref.shape[1]))


def _mlstm_mixer(x, B, S, w_in, conv_w, ig_bias, fg_bias, norm_g):
    T = B * S
    H, L = MLSTM_HEADS, MLSTM_L
    wb = w_in.astype(BF16)
    main = _proj(x, wb[:, :3 * D_MODEL], out_dtype=F32, tm=512, tn=3 * D_MODEL)
    wg = jnp.pad(wb[:, 3 * D_MODEL:], ((0, 0), (0, 128 - 2 * H)))
    gates = _proj(x, wg, out_dtype=F32, tm=512, tn=128)
    gates_r = gates[:, :2 * H].reshape(B, S, 2 * H).transpose(0, 2, 1)
    bias16 = jnp.concatenate([ig_bias, fg_bias]).astype(F32)
    bias_c = jnp.pad(bias16, (0, 128 - 2 * H)).reshape(1, 128)
    bias_r = bias16.reshape(2 * H, 1)
    tri = jnp.tril(jnp.ones((L, L), F32))
    nc = S // L
    rowblk = lambda c: (lambda b, i: (b * nc + i, c))
    fixed = lambda b, i: (0, 0)
    return pl.pallas_call(
        _mlstm_kernel,
        name="mlstm",
        grid=(B, nc),
        in_specs=[pl.BlockSpec((L, D_MODEL), rowblk(0)), pl.BlockSpec((L, D_MODEL), rowblk(1)),
                  pl.BlockSpec((L, D_MODEL), rowblk(2)), pl.BlockSpec((L, 128), rowblk(0)),
                  pl.BlockSpec((None, 2 * H, L), lambda b, i: (b, 0, i)),
                  pl.BlockSpec((MLSTM_CONV, D_MODEL), fixed), pl.BlockSpec((1, 128), fixed),
                  pl.BlockSpec((2 * H, 1), fixed), pl.BlockSpec((1, D_MODEL), fixed),
                  pl.BlockSpec((L, L), fixed), pl.BlockSpec((L, L), fixed)],
        out_specs=pl.BlockSpec((L, D_MODEL), rowblk(0)),
        out_shape=jax.ShapeDtypeStruct((T, D_MODEL), BF16),
        scratch_shapes=[pltpu.VMEM((H, MLSTM_QK_DIM, MLSTM_V_DIM), F32), pltpu.VMEM((H, MLSTM_QK_DIM), F32),
                        pltpu.VMEM((H, 128), F32), pltpu.VMEM((L + 8, D_MODEL), F32)],
        compiler_params=_cparams("parallel", "arbitrary"),
    )(main, main, main, gates, gates_r, conv_w.astype(F32), bias_c, bias_r, norm_g.reshape(1, D_MODEL).astype(F32),
      tri, tri.T)


def _rot_cols(w):
    shp = w.shape
    w4 = w.reshape(shp[:-1] + (shp[-1] // HEAD_DIM, 2, ROPE_HALF))
    return jnp.flip(w4, axis=-2).reshape(shp)


def _nsa_compress_kernel(chk_ref, chv_ref, wk_lo_ref, wk_hi_ref, wv_lo_ref, wv_hi_ref, pk_ref, pv_ref, wk_ref, wv_ref,
                         cos_ref, sin_ref, kc_ref, vc_ref, sh_ref):
    NCH = chk_ref.shape[0]
    G = NSA_KV_HEADS
    sh_ref[NCH:NCH + 8, :] = jnp.zeros((8, sh_ref.shape[1]), F32)
    live = lax.broadcasted_iota(I32, (NCH, 1), 0) < NCH - 1

    def blocks(ch_ref, lo_ref, hi_ref, p_ref, w_ref):
        ch = ch_ref[...]
        n = lo_ref.shape[1]
        sh_ref[0:NCH, 0:n] = _mm(ch, hi_ref[...])
        const = _mm(p_ref[...].astype(BF16), w_ref[...])[0:1, :]
        return jnp.where(live, _mm(ch, lo_ref[...]) + sh_ref[1:NCH + 1, 0:n] + jnp.concatenate([const] * G, axis=1), 0.0)

    k2 = blocks(chk_ref, wk_lo_ref, wk_hi_ref, pk_ref, wk_ref)
    v2 = blocks(chv_ref, wv_lo_ref, wv_hi_ref, pv_ref, wv_ref)
    for g in range(G):
        raw = k2[:, 2 * g * HEAD_DIM:(2 * g + 1) * HEAD_DIM]
        rot = k2[:, (2 * g + 1) * HEAD_DIM:(2 * g + 2) * HEAD_DIM]
        kc_ref[g] = (raw * cos_ref[...] + rot * sin_ref[...]).astype(kc_ref.dtype)
        vc_ref[g] = v2[:, g * HEAD_DIM:(g + 1) * HEAD_DIM].astype(vc_ref.dtype)


def _nsa_cmp_select_kernel(q_ref, kc_ref, vc_ref, ovlT_ref, ocmp_ref, sel_ref):
    TQ = q_ref.shape[0]
    NCH = kc_ref.shape[0]
    NSB = ovlT_ref.shape[0]
    q0 = pl.program_id(2) * TQ
    t = q0 + lax.broadcasted_iota(I32, (TQ, 1), 0)
    cend = lax.broadcasted_iota(I32, (1, NCH), 1) * CMP_STRIDE + (CMP_BLK - 1)
    bias = jnp.where(cend <= t, 0.0, NEG_INF)
    kc = kc_ref[...]
    vc = vc_ref[...]
    psum = jnp.zeros((TQ, NCH), F32)
    outs = []
    for j in range(NSA_J):
        qj = q_ref[:, j * HEAD_DIM:(j + 1) * HEAD_DIM]
        s = _nt(qj, kc) * HEAD_DIM ** -0.5 + bias
        m = jnp.maximum(jnp.max(s, axis=-1, keepdims=True), NEG_FLOOR)
        e = jnp.exp(s - m)
        p = e / jnp.maximum(jnp.sum(e, axis=-1, keepdims=True), TINY)
        outs.append(_mm(p.astype(BF16), vc))
        psum = psum + p
    ocmp_ref[...] = jnp.concatenate(outs, axis=1).astype(ocmp_ref.dtype)

    ovl01 = ovlT_ref[...].astype(BF16)
    imp = sum(_nt(ovl01, t) for t in _split3(psum))
    nb = lax.broadcasted_iota(I32, (NSB, 1), 0)
    qblk = (q0 + lax.broadcasted_iota(I32, (1, TQ), 1)) // SEL_BLK
    forced = (nb == 0) | (nb == qblk) | (nb == qblk - 1)
    cur = jnp.where(forced, FORCE_SCORE, jnp.where(nb > qblk, NEG_INF, imp))
    sel = jnp.zeros((NSB, TQ), F32)
    for _ in range(min(N_SEL, NSB)):
        m = jnp.max(cur, axis=0, keepdims=True)
        idx = jnp.min(jnp.where(cur == m, nb, NSB), axis=0, keepdims=True)
        hit = nb == idx
        sel = jnp.where(hit, 1.0, sel)
        cur = jnp.where(hit, BELOW_NEG_INF, cur)
    sel_ref[...] = sel.astype(sel_ref.dtype)


def _nsa_main_kernel(q_ref, ksT_ref, vs_ref, kwT_ref, vw_ref, sel_ref, ocmp_ref, gate_ref, gexp_ref, o_ref):
    TQ = q_ref.shape[0]
    S = vs_ref.shape[0]
    TK = min(NSA_TK, S)
    J = NSA_J
    q0 = pl.program_id(2) * TQ
    q4 = jnp.concatenate([q_ref[:, j * HEAD_DIM:(j + 1) * HEAD_DIM] for j in range(J)], axis=0) * HEAD_DIM ** -0.5
    t = q0 + lax.broadcasted_iota(I32, (TQ, 1), 0)
    selT = sel_ref[...]

    def attend(carry, kT, v1, bias):
        m, acc = carry
        s = _mm(q4, kT) + jnp.concatenate([bias] * J, axis=0)
        m_new = jnp.maximum(m, jnp.max(s, axis=-1, keepdims=True))
        e = jnp.exp(s - m_new)
        return m_new, jnp.exp(m - m_new) * acc + _mm(e.astype(BF16), v1)

    def finish(acc):
        return acc[:, :HEAD_DIM] / jnp.maximum(acc[:, HEAD_DIM:], TINY)

    init = (jnp.full((J * TQ, 1), NEG_FLOOR, F32), jnp.zeros((J * TQ, 2 * HEAD_DIM), F32))

    def body(c, carry):
        k0 = pl.multiple_of(c * TK, TK)
        kpos = k0 + lax.broadcasted_iota(I32, (1, TK), 1)
        expand = jnp.where(kpos // SEL_BLK == lax.broadcasted_iota(I32, (selT.shape[0], 1), 0), 1.0, 0.0).astype(BF16)
        bias = jnp.where((_tn(selT, expand) > 0.5) & (kpos <= t), 0.0, NEG_INF)
        return attend(carry, ksT_ref[:, pl.ds(k0, TK)], vs_ref[pl.ds(k0, TK), :], bias)

    nchunks = (q0 + TQ - 1) // TK + 1
    o_slc = finish(lax.fori_loop(0, nchunks, body, init)[1])

    span = min(NSA_WINDOW + TQ, S)
    start = pl.multiple_of(jnp.clip(q0 - NSA_WINDOW, 0, S - span), TQ)
    dist = t - (start + lax.broadcasted_iota(I32, (1, span), 1))
    bias = jnp.where((dist >= 0) & (dist < NSA_WINDOW), 0.0, NEG_INF)
    o_win = finish(attend(init, kwT_ref[:, pl.ds(start, span)], vw_ref[pl.ds(start, span), :], bias)[1])

    unstack = lambda o: jnp.concatenate([o[j * TQ:(j + 1) * TQ, :] for j in range(J)], axis=1)
    g = _sigmoid(gate_ref[...])
    gx = [_mm01(g, gexp_ref[c]) for c in range(3)]
    out = gx[0] * ocmp_ref[...].astype(F32) + gx[1] * unstack(o_slc) + gx[2] * unstack(o_win)
    o_ref[...] = out.astype(o_ref.dtype)


def _nsa_mixer(x, B, S, w_in, cmp_pos_k, cmp_pos_v, cmp_wk, cmp_wv):
    T = B * S
    H, G, J, dh = NSA_HEADS, NSA_KV_HEADS, NSA_J, HEAD_DIM
    kv = G * dh
    wb = w_in.astype(BF16)
    col = lambda a, n: wb[:, a:a + n]
    o_q, o_kc, o_vc, o_ks, o_vs, o_kw, o_vw, o_g = np.cumsum([0, H * dh] + [kv] * 6).tolist()
    pos = jnp.arange(S)
    w_rope = jnp.concatenate([col(o_q, H * dh), col(o_ks, kv), col(o_kw, kv)], axis=1)
    roped = _proj(x, w_rope, out_dtype=BF16, tm=512, tn=1536, rope_tables=_rope_tables(pos, 1536), seq=S)
    w_plain = jnp.concatenate([col(o_vs, kv), col(o_vw, kv), jnp.pad(col(o_g, 3 * H), ((0, 0), (0, 128 - 3 * H)))], axis=1)
    plain = _proj(x, w_plain, out_dtype=F32, tm=512, tn=w_plain.shape[1])

    heads_T = lambda a: a.reshape(B, S, G, dh).transpose(0, 2, 3, 1)
    heads = lambda a: a.reshape(B, S, G, dh).transpose(0, 2, 1, 3)
    ksT = heads_T(roped[:, H * dh:H * dh + kv])
    kwT = heads_T(roped[:, H * dh + kv:])
    with_ones = lambda v: jnp.concatenate([v, jnp.ones_like(v)], axis=-1)
    vs = with_ones(heads(plain[:, :kv]).astype(BF16))
    vw = with_ones(heads(plain[:, kv:2 * kv]).astype(BF16))
    gates = plain[:, 2 * kv:]

    nch = S // CMP_STRIDE
    cw = CMP_STRIDE * kv
    chk = _proj(x, col(o_kc, kv), out_dtype=BF16, tm=512, tn=kv).reshape(B, nch, cw)
    chv = _proj(x, col(o_vc, kv), out_dtype=BF16, tm=512, tn=kv).reshape(B, nch, cw)
    wk = cmp_wk.reshape(CMP_BLK * dh, dh)
    wk2 = jnp.concatenate([wk, _rot_cols(wk)], axis=1).astype(BF16)
    wv2 = cmp_wv.reshape(CMP_BLK * dh, dh).astype(BF16)
    eye = jnp.eye(G, dtype=BF16)

    def per_head(w, part):
        n = w.shape[1]
        w3 = w.reshape(2, CMP_STRIDE, dh, n)[part]
        return jnp.einsum('lde,gh->lgdhe', w3, eye).reshape(cw, G * n)

    flat8 = lambda p_: jnp.pad(p_.reshape(1, CMP_BLK * dh), ((0, 7), (0, 0))).astype(F32)
    cend = jnp.arange(nch) * CMP_STRIDE + CMP_BLK - 1
    cos_c, sin_c = _rope_tables(cend, dh)
    fixed1 = lambda b: (0, 0)
    full1 = lambda a: pl.BlockSpec(a.shape, fixed1)
    consts = (per_head(wk2, 0), per_head(wk2, 1), per_head(wv2, 0), per_head(wv2, 1), flat8(cmp_pos_k), flat8(cmp_pos_v),
              wk2, wv2, cos_c, sin_c)
    kc, vc = pl.pallas_call(
        _nsa_compress_kernel,
        name="nsa_compress",
        grid=(B,),
        in_specs=[pl.BlockSpec((None, nch, cw), lambda b: (b, 0, 0))] * 2 + [full1(c) for c in consts],
        out_specs=[pl.BlockSpec((None, G, nch, dh), lambda b: (b, 0, 0, 0))] * 2,
        out_shape=[jax.ShapeDtypeStruct((B, G, nch, dh), BF16)] * 2,
        scratch_shapes=[pltpu.VMEM((nch + 8, 2 * kv), F32)],
        compiler_params=_cparams("parallel"),
    )(chk, chv, *consts)

    nsb = S // SEL_BLK
    c_idx, s_idx = np.arange(nch), np.arange(nsb)
    ovl = ((c_idx[:, None] * CMP_STRIDE + CMP_BLK - 1 >= s_idx[None, :] * SEL_BLK)
           & (c_idx[:, None] * CMP_STRIDE < (s_idx[None, :] + 1) * SEL_BLK)).astype(np.float32)
    bgi = lambda b, g, i: (b, g, 0, 0)
    TS = min(NSA_TQ_SELECT, S)
    ns = S // TS
    sblk = lambda b, g, i: (b * ns + i, g)
    ocmp, sel = pl.pallas_call(
        _nsa_cmp_select_kernel,
        name="nsa_cmp_select",
        grid=(B, G, ns),
        in_specs=[pl.BlockSpec((TS, J * dh), sblk), pl.BlockSpec((None, None, nch, dh), bgi),
                  pl.BlockSpec((None, None, nch, dh), bgi), pl.BlockSpec((nsb, nch), lambda b, g, i: (0, 0))],
        out_specs=[pl.BlockSpec((TS, J * dh), sblk), pl.BlockSpec((None, None, nsb, TS), lambda b, g, i: (b, g, 0, i))],
        out_shape=[jax.ShapeDtypeStruct((T, H * dh), BF16), jax.ShapeDtypeStruct((B, G, nsb, S), BF16)],
        compiler_params=_cparams("parallel", "parallel", "parallel"),
    )(roped, kc, vc, jnp.asarray(ovl.T))
    TQ = NSA_TQ
    nq = S // TQ
    qblk = lambda b, g, i: (b * nq + i, g)

    gexp = np.zeros((G, 3, 128, J * dh), np.float32)
    for g in range(G):
        for c in range(3):
            for j in range(J):
                gexp[g, c, (g * J + j) * 3 + c, j * dh:(j + 1) * dh] = 1.0
    return pl.pallas_call(
        _nsa_main_kernel,
        name="nsa_main",
        grid=(B, G, nq),
        in_specs=[pl.BlockSpec((TQ, J * dh), qblk)] + [pl.BlockSpec((None, None, dh, S), bgi),
                                                       pl.BlockSpec((None, None, S, 2 * dh), bgi)] * 2 + [
            pl.BlockSpec((None, None, nsb, TQ), lambda b, g, i: (b, g, 0, i)),
            pl.BlockSpec((TQ, J * dh), qblk), pl.BlockSpec((TQ, 128), lambda b, g, i: (b * nq + i, 0)),
            pl.BlockSpec((None, 3, 128, J * dh), lambda b, g, i: (g, 0, 0, 0))],
        out_specs=pl.BlockSpec((TQ, J * dh), qblk),
        out_shape=jax.ShapeDtypeStruct((T, H * dh), BF16),
        compiler_params=_cparams("parallel", "parallel", "arbitrary"),
    )(roped, ksT, vs, kwT, vw, sel, ocmp, gates, jnp.asarray(gexp))


def _dil_kernel(q_ref, kp_ref, kc_ref, vp_ref, vc_ref, o_ref, lse_ref):
    NQ = q_ref.shape[0]
    prev_from = jnp.where(pl.program_id(2) == 0, NQ, 0)
    qi = lax.broadcasted_iota(I32, (NQ, 1), 0)
    kj = lax.broadcasted_iota(I32, (1, 2 * NQ), 1)
    dist = NQ + qi - kj
    bias = jnp.where((dist >= 0) & (dist <= NQ) & (kj >= prev_from), 0.0, NEG_INF)
    lane = lax.broadcasted_iota(I32, (NQ, 128), 1)
    first = lane < HEAD_DIM
    lse_all = jnp.zeros((NQ, 128), F32)
    zero = jnp.zeros((), q_ref.dtype)
    for hp in range(DIL_HEADS // 2):
        sl = slice(hp * 128, (hp + 1) * 128)
        q2 = q_ref[:, sl]
        kb = jnp.concatenate([kp_ref[:, sl], kc_ref[:, sl]], axis=0)
        vb = jnp.concatenate([vp_ref[:, sl], vc_ref[:, sl]], axis=0)
        outs = []
        for sub in range(2):
            keep = first if sub == 0 else jnp.logical_not(first)
            s = _nt(jnp.where(keep, q2, zero), kb) * HEAD_DIM ** -0.5 + bias
            m = jnp.max(s, axis=-1, keepdims=True)
            e = jnp.exp(s - m)
            den = jnp.sum(e, axis=-1, keepdims=True)
            outs.append(_mm((e / den).astype(BF16), vb))
            lse_all = jnp.where(lane == 2 * hp + sub, m + jnp.log(den), lse_all)
        o_ref[:, sl] = jnp.where(first, outs[0], outs[1]).astype(o_ref.dtype)
    lse_ref[...] = lse_all


def _proj_res_kernel(*refs, rope, dils):
    n_out = len(dils)
    x_ref, w_ref = refs[:2]
    rest = refs[2:]
    if rope:
        cos_ref, sin_ref = rest[:2]
        rest = rest[2:]
    out_refs, xb_ref, scr_ref = rest[:n_out], rest[n_out], rest[n_out + 1]
    tm, tn = x_ref.shape[0], w_ref.shape[1]

    @pl.when(pl.program_id(1) == 0)
    def _():
        xb_ref[...] = x_ref[...].astype(BF16)

    acc = _mm(xb_ref[...], w_ref[...])
    if rope:
        lane = lax.broadcasted_iota(I32, acc.shape, 1)
        lo = (lane % HEAD_DIM) < ROPE_HALF
        rot = jnp.where(lo, pltpu.roll(acc, tn - ROPE_HALF, 1), pltpu.roll(acc, ROPE_HALF, 1))
        rep = lambda t: jnp.concatenate([t] * (tn // 128), axis=1)
        acc = acc * rep(cos_ref[...]) + rot * rep(sin_ref[...])
    if any(d > 1 for d in dils):
        for c in range(tn // 128):
            scr_ref[c] = acc[:, c * 128:(c + 1) * 128]
    for o_ref, dil in zip(out_refs, dils):
        if dil == 1:
            o_ref[...] = acc.astype(o_ref.dtype)
            continue
        for r in range(dil):
            for c in range(tn // 128):
                o_ref[:, r * tn + c * 128:r * tn + (c + 1) * 128] = (
                    scr_ref[c, pl.ds(r, tm // dil, stride=dil), :].astype(o_ref.dtype))


def _proj_res(x, w, dils, *, seq, rope, tm, tn):
    T, K = x.shape
    N = w.shape[1]
    nparts = N // tn
    in_specs = [pl.BlockSpec((tm, K), lambda i, j: (i, 0)), pl.BlockSpec((K, tn), lambda i, j: (0, j))]
    args = [x, w]
    if rope:
        nseq = seq // tm
        in_specs += [pl.BlockSpec((tm, 128), lambda i, j: (i % nseq, 0))] * 2
        args += list(_rope_tables(jnp.arange(seq), 128))
    return pl.pallas_call(
        functools.partial(_proj_res_kernel, rope=rope, dils=dils),
        name="proj_residue_rope" if rope else "proj_residue",
        grid=(T // tm, nparts),
        in_specs=in_specs,
        out_specs=[pl.BlockSpec((tm // d, d * tn), lambda i, j: (i, j)) for d in dils],
        out_shape=[jax.ShapeDtypeStruct((T // d, nparts * d * tn), BF16) for d in dils],
        scratch_shapes=[pltpu.VMEM((tm, K), BF16), pltpu.VMEM((tn // 128, tm, 128), F32)],
        compiler_params=_cparams("parallel", "arbitrary"),
    )(*args)


def _dil_group(qk, v, B, S, dil):
    NQ = DIL_STEPS
    U = S // dil
    nb = U // NQ
    W = D_MODEL
    qk_view = qk.reshape(B, U, 2 * dil * W)
    v_view = v.reshape(B, U, dil * W)
    cur = lambda part, np_: (lambda b, r, n: (b, n, r * np_ + part))
    prev = lambda part, np_: (lambda b, r, n: (b, jnp.maximum(n - 1, 0), r * np_ + part))
    blk = lambda f: pl.BlockSpec((None, NQ, W), f)
    o, lse = pl.pallas_call(
        _dil_kernel,
        name=f"dilated_attn_{dil}",
        grid=(B, dil, nb),
        in_specs=[blk(cur(0, 2)), blk(prev(1, 2)), blk(cur(1, 2)), blk(prev(0, 1)), blk(cur(0, 1))],
        out_specs=[pl.BlockSpec((None, NQ, W), lambda b, r, n: (b, n, r)),
                   pl.BlockSpec((None, NQ, 128), lambda b, r, n: (b, n, r))],
        out_shape=[jax.ShapeDtypeStruct((B, U, dil * W), BF16), jax.ShapeDtypeStruct((B, U, dil * 128), F32)],
        compiler_params=_cparams("parallel", "parallel", "arbitrary"),
    )(qk_view, qk_view, qk_view, v_view, v_view)
    return o.reshape(B * U, dil * W), lse.reshape(B * U, dil * 128)


def _dil_outproj_ln_kernel(o0_ref, o1_ref, o2_ref, l0_ref, l1_ref, l2_ref, hexp_ref, w_ref, x_ref, g_ref, b_ref, out_ref,
                           oscr_ref, lscr_ref, *, dils):
    tm, W = x_ref.shape

    def natural(o_ref, l_ref, dil):
        if dil == 1:
            return o_ref[...].astype(F32), l_ref[...]
        n = tm // dil
        for r in range(dil):
            lscr_ref[0, pl.ds(r, n, stride=dil), :] = l_ref[:, r * 128:(r + 1) * 128]
            for c in range(W // 128):
                oscr_ref[c, pl.ds(r, n, stride=dil), :] = o_ref[:, r * W + c * 128:r * W + (c + 1) * 128].astype(F32)
        return jnp.concatenate([oscr_ref[c] for c in range(W // 128)], axis=1), lscr_ref[0]

    os_, ls = zip(*[natural(o, l, d) for o, l, d in zip((o0_ref, o1_ref, o2_ref), (l0_ref, l1_ref, l2_ref), dils)])
    m = jnp.maximum(jnp.maximum(ls[0], ls[1]), ls[2])
    es = [jnp.exp(l - m) for l in ls]
    tot = es[0] + es[1] + es[2]
    y = jnp.zeros((tm, W), F32)
    for e, o in zip(es, os_):
        y = y + _mm01(e / tot, hexp_ref[...]) * o
    z = ALPHA * x_ref[...] + _mm(y.astype(BF16), w_ref[...])
    out_ref[...] = _ln_rows(z, g_ref[...], b_ref[...])


def _dilated_layer(x, B, S, w_in, w_out, g, b, *, tm=256):
    T, D = x.shape
    dils = tuple(d for _, d in DIL_GROUPS)
    wb = w_in.astype(BF16)
    vals = _proj_res(x, wb[:, 2 * len(dils) * D:], dils, seq=S, rope=False, tm=1024, tn=D)
    outs = []
    for gi, dil in enumerate(dils):
        qk, = _proj_res(x, wb[:, 2 * gi * D:(2 * gi + 2) * D], (dil,), seq=S, rope=True, tm=512, tn=2 * D)
        outs.append(_dil_group(qk, vals[gi], B, S, dil))
    hexp = np.zeros((128, D), np.float32)
    for h in range(DIL_HEADS):
        hexp[h, h * HEAD_DIM:(h + 1) * HEAD_DIM] = 1.0
    row = lambda i: (i, 0)
    fixed = lambda i: (0, 0)
    return pl.pallas_call(
        functools.partial(_dil_outproj_ln_kernel, dils=dils),
        name="dilated_outproj_ln",
        grid=(T // tm,),
        in_specs=[pl.BlockSpec((tm // d, d * D), row) for d in dils] + [pl.BlockSpec((tm // d, d * 128), row) for d in dils]
        + [pl.BlockSpec((128, D), fixed), pl.BlockSpec((D, D), fixed), pl.BlockSpec((tm, D), row),
           pl.BlockSpec((1, D), fixed), pl.BlockSpec((1, D), fixed)],
        out_specs=pl.BlockSpec((tm, D), row),
        out_shape=jax.ShapeDtypeStruct((T, D), F32),
        scratch_shapes=[pltpu.VMEM((D // 128, tm, 128), F32), pltpu.VMEM((1, tm, 128), F32)],
        compiler_params=_cparams("parallel"),
    )(outs[0][0], outs[1][0], outs[2][0], outs[0][1], outs[1][1], outs[2][1], jnp.asarray(hexp),
      w_out.astype(BF16), x, g.reshape(1, D), b.reshape(1, D))


def _pool_ln_kernel(x_ref, halo_ref, w_ref, scale_ref, g_ref, b_ref, o_ref, ext_ref):
    TS = x_ref.shape[0]
    s = pl.program_id(1)
    x = x_ref[...]
    ext_ref[0:POOL_HALO, :] = jnp.where(s == 0, 0.0, halo_ref[...])
    ext_ref[POOL_HALO:POOL_HALO + TS, :] = x
    cnt = (s * TS + lax.broadcasted_iota(I32, (TS, 1), 0) + 1).astype(F32)
    ys = []
    for gi, w in enumerate(POOL_WINDOWS):
        sl = slice(gi * POOL_GROUP, (gi + 1) * POOL_GROUP)
        xg = x[:, sl]
        tot = xg
        for j in range(1, w):
            tot = tot + ext_ref[POOL_HALO - j:POOL_HALO - j + TS, sl]
        mean = tot / jnp.minimum(cnt, float(w))
        ys.append(_mm((mean - xg).astype(BF16), w_ref[gi]))
    y = jnp.concatenate(ys, axis=1) * scale_ref[...]
    o_ref[...] = _ln_rows(ALPHA * x + y, g_ref[...], b_ref[...])


def _pool_layer(x, B, S, w_grp, scale, g, b, *, ts=512):
    T, D = x.shape
    ns = S // ts
    hb = ts // POOL_HALO
    fixed = lambda bb, s: (0, 0)
    return pl.pallas_call(
        _pool_ln_kernel,
        name="pool_ln",
        grid=(B, ns),
        in_specs=[pl.BlockSpec((ts, D), lambda bb, s: (bb * ns + s, 0)),
                  pl.BlockSpec((POOL_HALO, D), lambda bb, s: (jnp.maximum((bb * ns + s) * hb - 1, 0), 0)),
                  pl.BlockSpec((len(POOL_WINDOWS), POOL_GROUP, POOL_GROUP), lambda bb, s: (0, 0, 0)),
                  pl.BlockSpec((1, D), fixed), pl.BlockSpec((1, D), fixed), pl.BlockSpec((1, D), fixed)],
        out_specs=pl.BlockSpec((ts, D), lambda bb, s: (bb * ns + s, 0)),
        out_shape=jax.ShapeDtypeStruct((T, D), F32),
        scratch_shapes=[pltpu.VMEM((ts + POOL_HALO, D), F32)],
        compiler_params=_cparams("parallel", "arbitrary"),
    )(x, x, w_grp.astype(BF16), scale.reshape(1, D), g.reshape(1, D), b.reshape(1, D))


U32 = jnp.uint32
HI16 = 0xFFFF0000
ROW_WORDS = D_MODEL // 2


def _pack_halves(v):
    h = v.shape[1] // 2
    bits = lambda a: lax.bitcast_convert_type(a.astype(BF16).astype(F32), U32)
    return (bits(v[:, :h]) >> 16) | (bits(v[:, h:]) & U32(HI16))


def _unpack_halves(w):
    lo = lax.bitcast_convert_type(w << 16, F32)
    hi = lax.bitcast_convert_type(w & U32(HI16), F32)
    return jnp.concatenate([lo, hi], axis=1)


def _router_kernel(x_ref, wT_ref, bias_ref, triu_ref, eidx_ref, gw_ref, rank_ref, cnt_ref, xp_ref, carry_ref):
    E = N_EXPERTS
    per = E // N_EXPERT_GROUPS
    TM = x_ref.shape[0]

    @pl.when(pl.program_id(0) == 0)
    def _():
        carry_ref[...] = jnp.zeros_like(carry_ref)

    x = x_ref[...]
    xp_ref[...] = _pack_halves(x)

    scores = _sigmoid(_nt(wT_ref[...], x, precision=HIGHEST))
    biased = scores + bias_ref[...]
    eio = lax.broadcasted_iota(I32, (E, TM), 0)

    gio = lax.broadcasted_iota(I32, (per, TM), 0)
    gscore = []
    for gidx in range(N_EXPERT_GROUPS):
        slab = biased[gidx * per:(gidx + 1) * per, :]
        m1 = jnp.max(slab, axis=0, keepdims=True)
        i1 = jnp.min(jnp.where(slab == m1, gio, per), axis=0, keepdims=True)
        m2 = jnp.max(jnp.where(gio == i1, BELOW_NEG_INF, slab), axis=0, keepdims=True)
        gscore.append(m1 + m2)
    slabs = []
    for gidx in range(N_EXPERT_GROUPS):
        beat = jnp.zeros((1, TM), F32)
        for o in range(N_EXPERT_GROUPS):
            if o == gidx:
                continue
            wins = (gscore[o] >= gscore[gidx]) if o < gidx else (gscore[o] > gscore[gidx])
            beat = beat + jnp.where(wins, 1.0, 0.0)
        keep = beat < float(TOPK_GROUPS)
        slabs.append(jnp.where(keep, biased[gidx * per:(gidx + 1) * per, :], NEG_INF))
    cur = jnp.concatenate(slabs, axis=0)

    picked = jnp.zeros((E, TM), F32)
    idxs, vals = [], []
    for _ in range(TOP_K):
        m = jnp.max(cur, axis=0, keepdims=True)
        idx = jnp.min(jnp.where(cur == m, eio, E), axis=0, keepdims=True)
        hit = eio == idx
        picked = jnp.where(hit, 1.0, picked)
        cur = jnp.where(hit, BELOW_NEG_INF, cur)
        idxs.append(idx)
        vals.append(jnp.sum(jnp.where(hit, scores, 0.0), axis=0, keepdims=True))
    total = vals[0]
    for v in vals[1:]:
        total = total + v

    pos = _mm(picked.astype(BF16), triu_ref[...]) + carry_ref[...]
    for k in range(TOP_K):
        eidx_ref[k:k + 1, :] = idxs[k]
        gw_ref[k:k + 1, :] = vals[k] / total * ROUTED_SCALE
        rank_ref[k:k + 1, :] = jnp.sum(jnp.where(eio == idxs[k], pos, 0.0), axis=0, keepdims=True).astype(I32)
    carry_ref[...] = carry_ref[...] + jnp.sum(picked, axis=1, keepdims=True)
    cnt_ref[...] = carry_ref[...].astype(I32)


def _router(x, router_w, router_bias, *, tm=512):
    T, D = x.shape
    E = N_EXPERTS
    triu = jnp.triu(jnp.ones((tm, tm), F32), k=1).astype(BF16)
    col = lambda i: (0, i)
    fixed = lambda i: (0, 0)
    return pl.pallas_call(
        _router_kernel,
        name="moe_router",
        grid=(T // tm,),
        in_specs=[pl.BlockSpec((tm, D), lambda i: (i, 0)), pl.BlockSpec((E, D), fixed),
                  pl.BlockSpec((E, 1), fixed), pl.BlockSpec((tm, tm), fixed)],
        out_specs=[pl.BlockSpec((TOP_K, tm), col), pl.BlockSpec((TOP_K, tm), col), pl.BlockSpec((TOP_K, tm), col),
                   pl.BlockSpec((E, 1), fixed), pl.BlockSpec((tm, ROW_WORDS), lambda i: (i, 0))],
        out_shape=[jax.ShapeDtypeStruct((TOP_K, T), I32), jax.ShapeDtypeStruct((TOP_K, T), F32),
                   jax.ShapeDtypeStruct((TOP_K, T), I32), jax.ShapeDtypeStruct((E, 1), I32),
                   jax.ShapeDtypeStruct((T, ROW_WORDS), U32)],
        scratch_shapes=[pltpu.VMEM((E, 1), F32)],
        compiler_params=_cparams("arbitrary"),
    )(x, router_w.T.astype(F32), router_bias.reshape(E, 1).astype(F32), triu)


def _tile_indices(dest_hbm, idx_ref, sem_idx, n_idx):
    i = pl.program_id(0)

    def idx_copy(step):
        slot = step % 2
        return pltpu.make_async_copy(dest_hbm.at[step], idx_ref.at[pl.ds(slot * n_idx, n_idx)], sem_idx.at[slot])

    @pl.when(i == 0)
    def _():
        idx_copy(0).start()

    idx_copy(i).wait()

    @pl.when(i + 1 < pl.num_programs(0))
    def _():
        idx_copy(i + 1).start()

    return (i % 2) * n_idx


def _dispatch_kernel(cnt_ref, pstart_ref, dest_hbm, x_ref, rows_ref, idx_ref, zero_ref, sem_idx, sem_rows, sem_zero):
    TD = x_ref.shape[0]

    @pl.when(pl.program_id(0) == 0)
    def _():
        zero_ref[...] = jnp.zeros_like(zero_ref)
        pieces = [1 << b for b in reversed(range(EXP_BLK.bit_length() - 1))]

        def per_expert(e, c):
            n = cnt_ref[e]
            pad = (n + EXP_BLK - 1) // EXP_BLK * EXP_BLK - n
            first = pstart_ref[e] + n

            def piece(size):
                at = first + (pad & ~(2 * size - 1))
                return pltpu.make_async_copy(zero_ref.at[pl.ds(0, size)], rows_ref.at[pl.ds(at, size), 0], sem_zero)

            for size in pieces:
                pl.when((pad & size) != 0)(lambda size=size: piece(size).start())
            for size in pieces:
                pl.when((pad & size) != 0)(lambda size=size: piece(size).wait())
            return c

        lax.fori_loop(0, N_EXPERTS, per_expert, 0)

    base = _tile_indices(dest_hbm, idx_ref, sem_idx, TOP_K * TD)
    scatter = lambda t, k: pltpu.make_async_copy(x_ref.at[pl.ds(t, 1)], rows_ref.at[idx_ref[base + k * TD + t]], sem_rows)

    def issue(t, c):
        for k in range(TOP_K):
            scatter(t, k).start(priority=k % 2)
        return c

    def drain(t, c):
        for k in range(TOP_K):
            scatter(t, k).wait()
        return c

    lax.fori_loop(0, TD, issue, 0, unroll=8)
    lax.fori_loop(0, TD, drain, 0, unroll=8)


def _dispatch(xp, dest_tiles, counts, pad_start, n_rows, *, td):
    T = xp.shape[0]
    return pl.pallas_call(
        _dispatch_kernel,
        name="moe_dispatch",
        grid_spec=pltpu.PrefetchScalarGridSpec(
            num_scalar_prefetch=2,
            grid=(T // td,),
            in_specs=[pl.BlockSpec(memory_space=pl.ANY), pl.BlockSpec((td, ROW_WORDS), lambda i, c, p: (i, 0))],
            out_specs=pl.BlockSpec(memory_space=pl.ANY),
            scratch_shapes=[pltpu.SMEM((2 * TOP_K * td,), I32), pltpu.VMEM((EXP_BLK // 2, ROW_WORDS), U32),
                            pltpu.SemaphoreType.DMA((2,)), pltpu.SemaphoreType.DMA(()), pltpu.SemaphoreType.DMA(())],
        ),
        out_shape=jax.ShapeDtypeStruct((n_rows, 1, ROW_WORDS), U32),
        compiler_params=_cparams("arbitrary"),
    )(counts, pad_start, dest_tiles, xp)


def _expert_kernel(be_ref, nu_ref, rows_hbm, wg_ref, wu_ref, wd_ref, out_hbm, xbuf, obuf, wgu_bf, wd_bf, sem_in, sem_out):
    i = pl.program_id(0)
    nu = nu_ref[0]

    @pl.when((i < nu) & ((i == 0) | (be_ref[i] != be_ref[jnp.maximum(i - 1, 0)])))
    def _():
        wgu_bf[:, :EXPERT_FF] = wg_ref[0].astype(BF16)
        wgu_bf[:, EXPERT_FF:] = wu_ref[0].astype(BF16)
        wd_bf[...] = wd_ref[0].astype(BF16)

    blk = lambda step: pl.ds(pl.multiple_of(step * EXP_BLK, EXP_BLK), EXP_BLK)
    in_copy = lambda step: pltpu.make_async_copy(rows_hbm.at[blk(step), 0], xbuf.at[step % 2], sem_in.at[step % 2])
    out_copy = lambda step: pltpu.make_async_copy(obuf.at[step % 2], out_hbm.at[blk(step), 0], sem_out.at[step % 2])

    @pl.when(i == 0)
    def _():
        in_copy(0).start()

    @pl.when(i < nu)
    def _():
        in_copy(i).wait()

        @pl.when(i + 1 < nu)
        def _():
            in_copy(i + 1).start()

        @pl.when(i >= 2)
        def _():
            out_copy(i - 2).wait()

        slot = i % 2
        gu = _mm(_unpack_halves(xbuf[slot]).astype(BF16), wgu_bf[...])
        h = _silu(gu[:, :EXPERT_FF]) * gu[:, EXPERT_FF:]
        obuf[slot] = _pack_halves(_mm(h.astype(BF16), wd_bf[...]))
        out_copy(i).start()

    @pl.when(i == pl.num_programs(0) - 1)
    def _():
        @pl.when(nu >= 2)
        def _():
            out_copy(nu - 2).wait()

        out_copy(nu - 1).wait()


def _experts(rows, blk_expert, n_used, w_gate, w_up, w_down, layer):
    R = rows.shape[0]
    D = w_down.shape[-1]
    n_blk = R // EXP_BLK
    expert = lambda i, be, nu: (layer, be[jnp.minimum(i, nu[0] - 1)], 0, 0)
    return pl.pallas_call(
        _expert_kernel,
        name="moe_experts",
        grid_spec=pltpu.PrefetchScalarGridSpec(
            num_scalar_prefetch=2,
            grid=(n_blk,),
            in_specs=[pl.BlockSpec(memory_space=pl.ANY), pl.BlockSpec((None, 1, D, EXPERT_FF), expert),
                      pl.BlockSpec((None, 1, D, EXPERT_FF), expert), pl.BlockSpec((None, 1, EXPERT_FF, D), expert)],
            out_specs=pl.BlockSpec(memory_space=pl.ANY),
            scratch_shapes=[pltpu.VMEM((2, EXP_BLK, ROW_WORDS), U32), pltpu.VMEM((2, EXP_BLK, ROW_WORDS), U32),
                            pltpu.VMEM((D, 2 * EXPERT_FF), BF16), pltpu.VMEM((EXPERT_FF, D), BF16),
                            pltpu.SemaphoreType.DMA((2,)), pltpu.SemaphoreType.DMA((2,))],
        ),
        out_shape=jax.ShapeDtypeStruct((R, 1, ROW_WORDS), U32),
        compiler_params=_cparams("arbitrary"),
    )(blk_expert, n_used, rows, w_gate, w_up, w_down)


def _combine_kernel(dest_hbm, gw_ref, x_ref, rows_ref, wsgu_ref, wsd_ref, g_ref, b_ref, p_ref, wp_ref, wpg_ref,
                    o_ref, idx_ref, buf_ref, sem_idx, sem_rows):
    TM = x_ref.shape[0]
    n_idx = TOP_K * TM
    i = pl.program_id(0)
    n = pl.num_programs(0)

    def idx_copy(step):
        slot = step % 3
        return pltpu.make_async_copy(dest_hbm.at[step], idx_ref.at[pl.ds(slot * n_idx, n_idx)], sem_idx.at[slot])

    def row_loop(step, slot, start):
        base = (step % 3) * n_idx

        def body(t, c):
            for k in range(TOP_K):
                cp = pltpu.make_async_copy(rows_ref.at[idx_ref[base + k * TM + t]], buf_ref.at[slot, k, pl.ds(t, 1)],
                                           sem_rows.at[slot])
                if start:
                    cp.start(priority=k % 2)
                else:
                    cp.wait()
            return c

        lax.fori_loop(0, TM, body, 0, unroll=8)

    def for_parity(step, fn):
        for slot in range(2):
            pl.when(step % 2 == slot)(functools.partial(fn, slot))

    @pl.when(i == 0)
    def _():
        idx_copy(0).start()
        idx_copy(0).wait()
        row_loop(0, 0, True)

        @pl.when(n > 1)
        def _():
            idx_copy(1).start()

    @pl.when(i + 1 < n)
    def _():
        idx_copy(i + 1).wait()

        @pl.when(i + 2 < n)
        def _():
            idx_copy(i + 2).start()

        for_parity(i + 1, lambda slot: row_loop(i + 1, slot, True))

    x = x_ref[...]
    gu = _mm(x.astype(BF16), wsgu_ref[...])
    ff = gu.shape[1] // 2
    f = _mm((_silu(gu[:, :ff]) * gu[:, ff:]).astype(BF16), wsd_ref[...])
    for_parity(i, lambda slot: row_loop(i, slot, False))
    gw = gw_ref[...]
    cur = i % 2
    for k in range(TOP_K):
        f = f + gw[:, k:k + 1] * _unpack_halves(buf_ref[cur, k])
    x2 = _ln_rows(ALPHA * x + f, g_ref[...], b_ref[...])
    gate = _sigmoid(_mm(x2.astype(BF16), wpg_ref[...]))
    o_ref[...] = x2 + gate * _mm(p_ref[...].astype(BF16), wp_ref[...])


def _combine(dest_tiles, gw, x, rows_out, wsgu, wsd, g, b, p, wp, wpg, *, tm):
    T, D = x.shape
    row = lambda i: (i, 0)
    fixed = lambda i: (0, 0)
    full = lambda a: pl.BlockSpec(a.shape, fixed)
    return pl.pallas_call(
        _combine_kernel,
        name="moe_combine",
        grid=(T // tm,),
        in_specs=[pl.BlockSpec(memory_space=pl.ANY), pl.BlockSpec((tm, TOP_K), row), pl.BlockSpec((tm, D), row),
                  pl.BlockSpec(memory_space=pl.ANY), full(wsgu), full(wsd), pl.BlockSpec((1, D), fixed),
                  pl.BlockSpec((1, D), fixed), pl.BlockSpec((tm, PLE_DIM), row), full(wp), full(wpg)],
        out_specs=pl.BlockSpec((tm, D), row),
        out_shape=jax.ShapeDtypeStruct((T, D), F32),
        scratch_shapes=[pltpu.SMEM((3 * TOP_K * tm,), I32), pltpu.VMEM((2, TOP_K, tm, ROW_WORDS), U32),
                        pltpu.SemaphoreType.DMA((3,)), pltpu.SemaphoreType.DMA((2,))],
        compiler_params=_cparams("arbitrary"),
    )(dest_tiles, gw, x, rows_out, wsgu, wsd, g.reshape(1, D), b.reshape(1, D), p, wp, wpg)


MOE_TILE = 256


def _moe_ple_layer(x, p, layer, router_w, router_bias, w_gate, w_up, w_down, ws_gate, ws_up, ws_down, g, b, ple_w, ple_gate_w):
    T, D = x.shape
    eidx, gw, rank, counts, xp = _router(x, router_w, router_bias)
    counts = counts.reshape(N_EXPERTS)
    padded = (counts + EXP_BLK - 1) // EXP_BLK * EXP_BLK
    pad_end = jnp.cumsum(padded)
    pad_start = pad_end - padded
    n_blk = T * TOP_K // EXP_BLK + N_EXPERTS
    e_iota = jnp.arange(N_EXPERTS, dtype=I32)
    dest = rank + jnp.sum(jnp.where(eidx[..., None] == e_iota, pad_start, 0), axis=-1)
    tm = MOE_TILE
    dest_tiles = dest.reshape(TOP_K, T // tm, tm).transpose(1, 0, 2).reshape(T // tm, TOP_K * tm)
    blk_first = jnp.arange(n_blk, dtype=I32)[:, None] * EXP_BLK
    blk_expert = jnp.minimum(jnp.sum((pad_end[None, :] <= blk_first).astype(I32), axis=-1), N_EXPERTS - 1)
    n_used = (pad_end[-1:] // EXP_BLK).astype(I32)

    rows = _dispatch(xp, dest_tiles, counts.astype(I32), pad_start.astype(I32), n_blk * EXP_BLK, td=tm)
    rows_out = _experts(rows, blk_expert, n_used, w_gate, w_up, w_down, layer)
    wsgu = jnp.concatenate([ws_gate, ws_up], axis=-1).astype(BF16)
    return _combine(dest_tiles, gw.T, x, rows_out, wsgu, ws_down.astype(BF16), g, b, p,
                    ple_w.astype(BF16), ple_gate_w.astype(BF16), tm=tm)


def kernel(x, p, ln_g, ln_b, mlstm_w_in, mlstm_conv, mlstm_ig_bias, mlstm_fg_bias, mlstm_norm_g, mlstm_w_out, nsa_w_in, nsa_cmp_pos_k, nsa_cmp_pos_v, nsa_cmp_wk, nsa_cmp_wv, nsa_w_out, dil_w_in, dil_w_out, pool_w, pool_scale, router_w, router_bias, exp_w_gate, exp_w_up, exp_w_down, sh_w_gate, sh_w_up, sh_w_down, ple_w, ple_gate_w):
    B, S, D = x.shape
    T = B * S
    xf = x.reshape(T, D)
    pf = p.reshape(DEPTH, T, PLE_DIM)
    for i in range(DEPTH):
        kind, j = i % 4, i // 4
        g1, b1 = ln_g[i, 0], ln_b[i, 0]
        if kind == 0:
            y = _mlstm_mixer(xf, B, S, mlstm_w_in[j], mlstm_conv[j], mlstm_ig_bias[j], mlstm_fg_bias[j], mlstm_norm_g[j])
            xf = _outproj_ln(y, mlstm_w_out[j].astype(BF16), xf, g1, b1)
        elif kind == 1:
            y = _nsa_mixer(xf, B, S, nsa_w_in[j], nsa_cmp_pos_k[j], nsa_cmp_pos_v[j], nsa_cmp_wk[j], nsa_cmp_wv[j])
            xf = _outproj_ln(y, nsa_w_out[j].astype(BF16), xf, g1, b1)
        elif kind == 2:
            xf = _dilated_layer(xf, B, S, dil_w_in[j], dil_w_out[j], g1, b1)
        else:
            xf = _pool_layer(xf, B, S, pool_w[j], pool_scale[j], g1, b1)
        xf = _moe_ple_layer(xf, pf[i], i, router_w[i], router_bias[i], exp_w_gate, exp_w_up, exp_w_down,
                            sh_w_gate[i], sh_w_up[i], sh_w_down[i], ln_g[i, 1], ln_b[i, 1], ple_w[i], ple_gate_w[i])
    return xf.reshape(B, S, D)
```

```python
import functools

import numpy as np
import jax
import jax.numpy as jnp
from jax import lax
from jax.experimental import pallas as pl
from jax.experimental.pallas import tpu as pltpu

F32 = jnp.float32
BF16 = jnp.bfloat16
I32 = jnp.int32
HIGHEST = lax.Precision.HIGHEST

D_MODEL = 1024
DEPTH = 4
ALPHA = (2.0 * DEPTH) ** 0.25
LN_EPS = 1e-5
NEG_INF = -1e30
TINY = 1e-30
BELOW_NEG_INF = -3e38
NEG_FLOOR = 0.999 * NEG_INF
ROPE_THETA = 10000.0
HEAD_DIM = 64
ROPE_HALF = HEAD_DIM // 2

MLSTM_HEADS = 8
MLSTM_QK_DIM = 64
MLSTM_V_DIM = 128
MLSTM_CONV = 4
MLSTM_L = 256

NSA_HEADS = 16
NSA_KV_HEADS = 4
NSA_J = NSA_HEADS // NSA_KV_HEADS
CMP_STRIDE = 16
CMP_BLK = 32
SEL_BLK = 64
N_SEL = 16
NSA_WINDOW = 512
FORCE_SCORE = 1e4
NSA_TQ = 256
NSA_TQ_SELECT = 512
NSA_TK = 512

DIL_HEADS = 16
DIL_GROUPS = ((128, 1), (512, 4), (2048, 16))
DIL_STEPS = 128

POOL_WINDOWS = (2, 4, 8, 16)
POOL_GROUP = 256
POOL_HALO = 16

N_EXPERTS = 64
TOP_K = 8
N_EXPERT_GROUPS = 8
TOPK_GROUPS = 4
EXPERT_FF = 256
ROUTED_SCALE = 2.5
EXP_BLK = 1024
PLE_DIM = 256

VMEM_LIMIT = 48 * 1024 * 1024


def _cparams(*sem):
    return pltpu.CompilerParams(dimension_semantics=sem, vmem_limit_bytes=VMEM_LIMIT)


def _nt(a, b, **kw):
    return lax.dot_general(a, b, (((1,), (1,)), ((), ())), preferred_element_type=F32, **kw)


def _tn(a, b, **kw):
    return lax.dot_general(a, b, (((0,), (0,)), ((), ())), preferred_element_type=F32, **kw)


def _mm(a, b, **kw):
    return jnp.dot(a, b, preferred_element_type=F32, **kw)


def _split3(x):
    a = x.astype(BF16)
    r = x - a.astype(F32)
    b = r.astype(BF16)
    return a, b, (r - b.astype(F32)).astype(BF16)


def _mm01(x, e01):
    eb = e01.astype(BF16)
    a, b, c = _split3(x)
    return _mm(a, eb) + _mm(b, eb) + _mm(c, eb)


def _sigmoid(z):
    return 1.0 / (1.0 + jnp.exp(-z))


def _silu(z):
    return z * _sigmoid(z)


def _ln_rows(z, g, b):
    mu = jnp.mean(z, axis=-1, keepdims=True)
    d = z - mu
    var = jnp.mean(d * d, axis=-1, keepdims=True)
    return d * lax.rsqrt(var + LN_EPS) * g + b


def _proj_kernel(*refs, rope):
    if rope:
        x_ref, w_ref, cos_ref, sin_ref, o_ref, xb_ref = refs
    else:
        x_ref, w_ref, o_ref, xb_ref = refs

    @pl.when(pl.program_id(1) == 0)
    def _():
        xb_ref[...] = x_ref[...].astype(BF16)

    acc = _mm(xb_ref[...], w_ref[...])
    if rope:
        tn = acc.shape[1]
        lane = lax.broadcasted_iota(I32, acc.shape, 1)
        lo = (lane % HEAD_DIM) < ROPE_HALF
        rot = jnp.where(lo, pltpu.roll(acc, tn - ROPE_HALF, 1), pltpu.roll(acc, ROPE_HALF, 1))
        acc = acc * cos_ref[...] + rot * sin_ref[...]
    o_ref[...] = acc.astype(o_ref.dtype)


def _proj(x, w, *, out_dtype, tm, tn, rope_tables=None, seq=None):
    T, K = x.shape
    N = w.shape[1]
    assert T % tm == 0 and N % tn == 0
    in_specs = [pl.BlockSpec((tm, K), lambda i, j: (i, 0)), pl.BlockSpec((K, tn), lambda i, j: (0, j))]
    args = [x, w]
    if rope_tables is not None:
        nseq = seq // tm
        in_specs += [pl.BlockSpec((tm, tn), lambda i, j: (i % nseq, 0))] * 2
        args += list(rope_tables)
    return pl.pallas_call(
        functools.partial(_proj_kernel, rope=rope_tables is not None),
        name="proj_rope" if rope_tables is not None else "proj",
        grid=(T // tm, N // tn),
        in_specs=in_specs,
        out_specs=pl.BlockSpec((tm, tn), lambda i, j: (i, j)),
        out_shape=jax.ShapeDtypeStruct((T, N), out_dtype),
        scratch_shapes=[pltpu.VMEM((tm, K), BF16)],
        compiler_params=_cparams("parallel", "arbitrary"),
    )(*args)


def _rope_tables(pos, width):
    inv = ROPE_THETA ** (-jnp.arange(ROPE_HALF, dtype=F32) / ROPE_HALF)
    ang = pos.astype(F32)[:, None] * inv[None, :]
    cos, sin = jnp.cos(ang), jnp.sin(ang)
    cos64 = jnp.concatenate([cos, cos], -1)
    sin64 = jnp.concatenate([-sin, sin], -1)
    rep = width // HEAD_DIM
    return jnp.tile(cos64, (1, rep)), jnp.tile(sin64, (1, rep))


def _outproj_ln_kernel(y_ref, w_ref, x_ref, g_ref, b_ref, o_ref):
    y = _mm(y_ref[...].astype(BF16), w_ref[...])
    o_ref[...] = _ln_rows(ALPHA * x_ref[...] + y, g_ref[...], b_ref[...])


def _outproj_ln(y, w, x, g, b, *, tm=512):
    T, D = x.shape
    K = y.shape[1]
    row = lambda i: (i, 0)
    fixed = lambda i: (0, 0)
    return pl.pallas_call(
        _outproj_ln_kernel,
        name="outproj_ln",
        grid=(T // tm,),
        in_specs=[pl.BlockSpec((tm, K), row), pl.BlockSpec((K, D), fixed), pl.BlockSpec((tm, D), row),
                  pl.BlockSpec((1, D), fixed), pl.BlockSpec((1, D), fixed)],
        out_specs=pl.BlockSpec((tm, D), row),
        out_shape=jax.ShapeDtypeStruct((T, D), F32),
        compiler_params=_cparams("parallel"),
    )(y, w, x, g.reshape(1, D), b.reshape(1, D))


def _log_sigmoid(z):
    return jnp.minimum(z, 0.0) - jnp.log(1.0 + jnp.exp(-jnp.abs(z)))


def _mlstm_kernel(qk_ref, v_ref, o_ref, gc_ref, gr_ref, convw_ref, bias_c_ref, bias_r_ref, ng_ref,
                  tri_ref, triT_ref, out_ref, C_ref, n_ref, m_ref, ext_ref):
    L = qk_ref.shape[0]
    H, dk, dv = MLSTM_HEADS, MLSTM_QK_DIM, MLSTM_V_DIM

    @pl.when(pl.program_id(1) == 0)
    def _():
        C_ref[...] = jnp.zeros_like(C_ref)
        n_ref[...] = jnp.zeros_like(n_ref)
        m_ref[...] = jnp.zeros_like(m_ref)
        ext_ref[0:8, :] = jnp.zeros((8, ext_ref.shape[1]), F32)

    cur = qk_ref[...]
    ext_ref[8:8 + L, :] = cur
    acc = convw_ref[3:4, :] * cur
    for j in range(MLSTM_CONV - 1):
        acc = acc + convw_ref[j:j + 1, :] * ext_ref[5 + j:5 + j + L, :]
    ext_ref[0:8, :] = cur[L - 8:L, :]
    qk = _silu(acc)

    gc = gc_ref[...] + bias_c_ref[...]
    gr = gr_ref[...] + bias_r_ref[...]
    tri01 = tri_ref[...].astype(BF16)
    b_col = sum(_mm(tri01, t) for t in _split3(_log_sigmoid(gc)))
    b_row = _mm01(_log_sigmoid(gr[H:2 * H, :]), triT_ref[...])
    ig_row = gr[0:H, :]
    tri = lax.broadcasted_iota(I32, (L, L), 0) >= lax.broadcasted_iota(I32, (L, L), 1)

    for h in range(H):
        q = (qk[:, h * dk:(h + 1) * dk] * dk ** -0.5).astype(BF16)
        k = qk[:, H * dk + h * dk:H * dk + (h + 1) * dk]
        kb = k.astype(BF16)
        v = v_ref[:, h * dv:(h + 1) * dv].astype(BF16)
        b_c = b_col[:, H + h:H + h + 1]
        ig_c = gc[:, h:h + 1]
        b_r = b_row[h:h + 1, :]
        m_prev = m_ref[h:h + 1, 0:1]
        C = C_ref[h]
        n = n_ref[h:h + 1, :]

        logD = jnp.where(tri, b_c - b_r + ig_row[h:h + 1, :], NEG_INF)
        inter = b_c + m_prev
        m_t = jnp.maximum(inter, jnp.max(logD, axis=-1, keepdims=True))
        s = _nt(q, kb) * jnp.exp(logD - m_t)
        w_inter = jnp.exp(inter - m_t)
        num = w_inter * _mm(q, C.astype(BF16)) + _mm(s.astype(BF16), v)
        den = w_inter * jnp.sum(q.astype(F32) * n, axis=-1, keepdims=True) + jnp.sum(s, axis=-1, keepdims=True)
        hv = num / jnp.maximum(jnp.abs(den), jnp.exp(-m_t))

        mu = jnp.mean(hv, axis=-1, keepdims=True)
        d = hv - mu
        hn = d * lax.rsqrt(jnp.mean(d * d, axis=-1, keepdims=True) + LN_EPS)
        og = _sigmoid(o_ref[:, h * dv:(h + 1) * dv])
        out_ref[:, h * dv:(h + 1) * dv] = (hn * ng_ref[:, h * dv:(h + 1) * dv] * og).astype(out_ref.dtype)

        bL = b_c[L - 1:L, :]
        logw = bL - b_c + ig_c
        m_new = jnp.maximum(bL + m_prev, jnp.max(logw, axis=0, keepdims=True))
        decay = jnp.exp(bL + m_prev - m_new)
        kw = k * jnp.exp(logw - m_new)
        C_ref[h] = decay * C + _tn(kw.astype(BF16), v)
        n_ref[h:h + 1, :] = decay * n + jnp.sum(kw, axis=0, keepdims=True)
        m_ref[h:h + 1, :] = jnp.broadcast_to(m_new, (1, m_ref.shape[1]))


def _mlstm_mixer(x, B, S, w_in, conv_w, ig_bias, fg_bias, norm_g):
    T = B * S
    H, L = MLSTM_HEADS, MLSTM_L
    wb = w_in.astype(BF16)
    main = _proj(x, wb[:, :3 * D_MODEL], out_dtype=F32, tm=512, tn=3 * D_MODEL)
    wg = jnp.pad(wb[:, 3 * D_MODEL:], ((0, 0), (0, 128 - 2 * H)))
    gates = _proj(x, wg, out_dtype=F32, tm=512, tn=128)
    gates_r = gates[:, :2 * H].reshape(B, S, 2 * H).transpose(0, 2, 1)
    bias16 = jnp.concatenate([ig_bias, fg_bias]).astype(F32)
    bias_c = jnp.pad(bias16, (0, 128 - 2 * H)).reshape(1, 128)
    bias_r = bias16.reshape(2 * H, 1)
    tri = jnp.tril(jnp.ones((L, L), F32))
    nc = S // L
    rowblk = lambda c: (lambda b, i: (b * nc + i, c))
    fixed = lambda b, i: (0, 0)
    return pl.pallas_call(
        _mlstm_kernel,
        name="mlstm",
        grid=(B, nc),
        in_specs=[pl.BlockSpec((L, D_MODEL), rowblk(0)), pl.BlockSpec((L, D_MODEL), rowblk(1)),
                  pl.BlockSpec((L, D_MODEL), rowblk(2)), pl.BlockSpec((L, 128), rowblk(0)),
                  pl.BlockSpec((None, 2 * H, L), lambda b, i: (b, 0, i)),
                  pl.BlockSpec((MLSTM_CONV, D_MODEL), fixed), pl.BlockSpec((1, 128), fixed),
                  pl.BlockSpec((2 * H, 1), fixed), pl.BlockSpec((1, D_MODEL), fixed),
                  pl.BlockSpec((L, L), fixed), pl.BlockSpec((L, L), fixed)],
        out_specs=pl.BlockSpec((L, D_MODEL), rowblk(0)),
        out_shape=jax.ShapeDtypeStruct((T, D_MODEL), BF16),
        scratch_shapes=[pltpu.VMEM((H, MLSTM_QK_DIM, MLSTM_V_DIM), F32), pltpu.VMEM((H, MLSTM_QK_DIM), F32),
                        pltpu.VMEM((H, 128), F32), pltpu.VMEM((L + 8, D_MODEL), F32)],
        compiler_params=_cparams("parallel", "arbitrary"),
    )(main, main, main, gates, gates_r, conv_w.astype(F32), bias_c, bias_r, norm_g.reshape(1, D_MODEL).astype(F32),
      tri, tri.T)


def _rot_cols(w):
    shp = w.shape
    w4 = w.reshape(shp[:-1] + (shp[-1] // HEAD_DIM, 2, ROPE_HALF))
    return jnp.flip(w4, axis=-2).reshape(shp)


def _nsa_compress_kernel(chk_ref, chv_ref, wk_lo_ref, wk_hi_ref, wv_lo_ref, wv_hi_ref, pk_ref, pv_ref, wk_ref, wv_ref,
                         cos_ref, sin_ref, kc_ref, vc_ref, sh_ref):
    NCH = chk_ref.shape[0]
    G = NSA_KV_HEADS
    sh_ref[NCH:NCH + 8, :] = jnp.zeros((8, sh_ref.shape[1]), F32)
    live = lax.broadcasted_iota(I32, (NCH, 1), 0) < NCH - 1

    def blocks(ch_ref, lo_ref, hi_ref, p_ref, w_ref):
        ch = ch_ref[...]
        n = lo_ref.shape[1]
        sh_ref[0:NCH, 0:n] = _mm(ch, hi_ref[...])
        const = _mm(p_ref[...].astype(BF16), w_ref[...])[0:1, :]
        return jnp.where(live, _mm(ch, lo_ref[...]) + sh_ref[1:NCH + 1, 0:n] + jnp.concatenate([const] * G, axis=1), 0.0)

    k2 = blocks(chk_ref, wk_lo_ref, wk_hi_ref, pk_ref, wk_ref)
    v2 = blocks(chv_ref, wv_lo_ref, wv_hi_ref, pv_ref, wv_ref)
    for g in range(G):
        raw = k2[:, 2 * g * HEAD_DIM:(2 * g + 1) * HEAD_DIM]
        rot = k2[:, (2 * g + 1) * HEAD_DIM:(2 * g + 2) * HEAD_DIM]
        kc_ref[g] = (raw * cos_ref[...] + rot * sin_ref[...]).astype(kc_ref.dtype)
        vc_ref[g] = v2[:, g * HEAD_DIM:(g + 1) * HEAD_DIM].astype(vc_ref.dtype)


def _nsa_cmp_select_kernel(q_ref, kc_ref, vc_ref, ovlT_ref, ocmp_ref, sel_ref):
    TQ = q_ref.shape[0]
    NCH = kc_ref.shape[0]
    NSB = ovlT_ref.shape[0]
    q0 = pl.program_id(2) * TQ
    t = q0 + lax.broadcasted_iota(I32, (TQ, 1), 0)
    cend = lax.broadcasted_iota(I32, (1, NCH), 1) * CMP_STRIDE + (CMP_BLK - 1)
    bias = jnp.where(cend <= t, 0.0, NEG_INF)
    kc = kc_ref[...]
    vc = vc_ref[...]
    psum = jnp.zeros((TQ, NCH), F32)
    outs = []
    for j in range(NSA_J):
        qj = q_ref[:, j * HEAD_DIM:(j + 1) * HEAD_DIM]
        s = _nt(qj, kc) * HEAD_DIM ** -0.5 + bias
        m = jnp.maximum(jnp.max(s, axis=-1, keepdims=True), NEG_FLOOR)
        e = jnp.exp(s - m)
        p = e / jnp.maximum(jnp.sum(e, axis=-1, keepdims=True), TINY)
        outs.append(_mm(p.astype(BF16), vc))
        psum = psum + p
    ocmp_ref[...] = jnp.concatenate(outs, axis=1).astype(ocmp_ref.dtype)

    ovl01 = ovlT_ref[...].astype(BF16)
    imp = sum(_nt(ovl01, t) for t in _split3(psum))
    nb = lax.broadcasted_iota(I32, (NSB, 1), 0)
    qblk = (q0 + lax.broadcasted_iota(I32, (1, TQ), 1)) // SEL_BLK
    forced = (nb == 0) | (nb == qblk) | (nb == qblk - 1)
    cur = jnp.where(forced, FORCE_SCORE, jnp.where(nb > qblk, NEG_INF, imp))
    sel = jnp.zeros((NSB, TQ), F32)
    for _ in range(min(N_SEL, NSB)):
        m = jnp.max(cur, axis=0, keepdims=True)
        idx = jnp.min(jnp.where(cur == m, nb, NSB), axis=0, keepdims=True)
        hit = nb == idx
        sel = jnp.where(hit, 1.0, sel)
        cur = jnp.where(hit, BELOW_NEG_INF, cur)
    sel_ref[...] = sel.astype(sel_ref.dtype)


def _nsa_main_kernel(q_ref, ksT_ref, vs_ref, kwT_ref, vw_ref, sel_ref, ocmp_ref, gate_ref, o_ref):
    TQ = q_ref.shape[0]
    S = vs_ref.shape[0]
    TK = min(NSA_TK, S)
    J = NSA_J
    q0 = pl.program_id(2) * TQ
    q4 = jnp.concatenate([q_ref[:, j * HEAD_DIM:(j + 1) * HEAD_DIM] for j in range(J)], axis=0) * HEAD_DIM ** -0.5
    t = q0 + lax.broadcasted_iota(I32, (TQ, 1), 0)
    selT = sel_ref[...]

    def attend(carry, kT, v1, bias):
        m, acc = carry
        s = _mm(q4, kT) + jnp.concatenate([bias] * J, axis=0)
        m_new = jnp.maximum(m, jnp.max(s, axis=-1, keepdims=True))
        e = jnp.exp(s - m_new)
        return m_new, jnp.exp(m - m_new) * acc + _mm(e.astype(BF16), v1)

    def finish(acc):
        return acc[:, :HEAD_DIM] / jnp.maximum(acc[:, HEAD_DIM:], TINY)

    init = (jnp.full((J * TQ, 1), NEG_FLOOR, F32), jnp.zeros((J * TQ, 2 * HEAD_DIM), F32))

    def body(c, carry):
        k0 = pl.multiple_of(c * TK, TK)
        kpos = k0 + lax.broadcasted_iota(I32, (1, TK), 1)
        expand = jnp.where(kpos // SEL_BLK == lax.broadcasted_iota(I32, (selT.shape[0], 1), 0), 1.0, 0.0).astype(BF16)
        bias = jnp.where((_tn(selT, expand) > 0.5) & (kpos <= t), 0.0, NEG_INF)
        return attend(carry, ksT_ref[:, pl.ds(k0, TK)], vs_ref[pl.ds(k0, TK), :], bias)

    nchunks = (q0 + TQ - 1) // TK + 1
    o_slc = finish(lax.fori_loop(0, nchunks, body, init)[1])

    span = min(NSA_WINDOW + TQ, S)
    start = pl.multiple_of(jnp.clip(q0 - NSA_WINDOW, 0, S - span), TQ)
    dist = t - (start + lax.broadcasted_iota(I32, (1, span), 1))
    bias = jnp.where((dist >= 0) & (dist < NSA_WINDOW), 0.0, NEG_INF)
    o_win = finish(attend(init, kwT_ref[:, pl.ds(start, span)], vw_ref[pl.ds(start, span), :], bias)[1])

    unstack = lambda o: jnp.concatenate([o[j * TQ:(j + 1) * TQ, :] for j in range(J)], axis=1)
    g = _sigmoid(gate_ref[...])
    head_of_lane = lax.broadcasted_iota(I32, (TQ, J * HEAD_DIM), 1) // HEAD_DIM
    gx = []
    for c in range(3):
        spread = jnp.zeros((TQ, J * HEAD_DIM), F32)
        for j in range(J):
            spread = jnp.where(head_of_lane == j, g[:, j * 3 + c:j * 3 + c + 1], spread)
        gx.append(spread)
    out = gx[0] * ocmp_ref[...].astype(F32) + gx[1] * unstack(o_slc) + gx[2] * unstack(o_win)
    o_ref[...] = out.astype(o_ref.dtype)


def _nsa_mixer(x, B, S, w_in, cmp_pos_k, cmp_pos_v, cmp_wk, cmp_wv):
    T = B * S
    H, G, J, dh = NSA_HEADS, NSA_KV_HEADS, NSA_J, HEAD_DIM
    kv = G * dh
    wb = w_in.astype(BF16)
    col = lambda a, n: wb[:, a:a + n]
    o_q, o_kc, o_vc, o_ks, o_vs, o_kw, o_vw, o_g = np.cumsum([0, H * dh] + [kv] * 6).tolist()
    pos = jnp.arange(S)
    w_rope = jnp.concatenate([col(o_q, H * dh), col(o_ks, kv), col(o_kw, kv)], axis=1)
    roped = _proj(x, w_rope, out_dtype=BF16, tm=512, tn=1536, rope_tables=_rope_tables(pos, 1536), seq=S)
    w_plain = jnp.concatenate([col(o_vs, kv), col(o_vw, kv), jnp.pad(col(o_g, 3 * H), ((0, 0), (0, 128 - 3 * H)))], axis=1)
    plain = _proj(x, w_plain, out_dtype=F32, tm=512, tn=w_plain.shape[1])

    heads_T = lambda a: a.reshape(B, S, G, dh).transpose(0, 2, 3, 1)
    heads = lambda a: a.reshape(B, S, G, dh).transpose(0, 2, 1, 3)
    ksT = heads_T(roped[:, H * dh:H * dh + kv])
    kwT = heads_T(roped[:, H * dh + kv:])
    with_ones = lambda v: jnp.concatenate([v, jnp.ones_like(v)], axis=-1)
    vs = with_ones(heads(plain[:, :kv]).astype(BF16))
    vw = with_ones(heads(plain[:, kv:2 * kv]).astype(BF16))
    gates = plain[:, 2 * kv:]

    nch = S // CMP_STRIDE
    cw = CMP_STRIDE * kv
    chk = _proj(x, col(o_kc, kv), out_dtype=BF16, tm=512, tn=kv).reshape(B, nch, cw)
    chv = _proj(x, col(o_vc, kv), out_dtype=BF16, tm=512, tn=kv).reshape(B, nch, cw)
    wk = cmp_wk.reshape(CMP_BLK * dh, dh)
    wk2 = jnp.concatenate([wk, _rot_cols(wk)], axis=1).astype(BF16)
    wv2 = cmp_wv.reshape(CMP_BLK * dh, dh).astype(BF16)
    eye = jnp.eye(G, dtype=BF16)

    def per_head(w, part):
        n = w.shape[1]
        w3 = w.reshape(2, CMP_STRIDE, dh, n)[part]
        return jnp.einsum('lde,gh->lgdhe', w3, eye).reshape(cw, G * n)

    flat8 = lambda p_: jnp.pad(p_.reshape(1, CMP_BLK * dh), ((0, 7), (0, 0))).astype(F32)
    cend = jnp.arange(nch) * CMP_STRIDE + CMP_BLK - 1
    cos_c, sin_c = _rope_tables(cend, dh)
    fixed1 = lambda b: (0, 0)
    full1 = lambda a: pl.BlockSpec(a.shape, fixed1)
    consts = (per_head(wk2, 0), per_head(wk2, 1), per_head(wv2, 0), per_head(wv2, 1), flat8(cmp_pos_k), flat8(cmp_pos_v),
              wk2, wv2, cos_c, sin_c)
    kc, vc = pl.pallas_call(
        _nsa_compress_kernel,
        name="nsa_compress",
        grid=(B,),
        in_specs=[pl.BlockSpec((None, nch, cw), lambda b: (b, 0, 0))] * 2 + [full1(c) for c in consts],
        out_specs=[pl.BlockSpec((None, G, nch, dh), lambda b: (b, 0, 0, 0))] * 2,
        out_shape=[jax.ShapeDtypeStruct((B, G, nch, dh), BF16)] * 2,
        scratch_shapes=[pltpu.VMEM((nch + 8, 2 * kv), F32)],
        compiler_params=_cparams("parallel"),
    )(chk, chv, *consts)

    nsb = S // SEL_BLK
    c_idx, s_idx = np.arange(nch), np.arange(nsb)
    ovl = ((c_idx[:, None] * CMP_STRIDE + CMP_BLK - 1 >= s_idx[None, :] * SEL_BLK)
           & (c_idx[:, None] * CMP_STRIDE < (s_idx[None, :] + 1) * SEL_BLK)).astype(np.float32)
    bgi = lambda b, g, i: (b, g, 0, 0)
    TS = min(NSA_TQ_SELECT, S)
    ns = S // TS
    sblk = lambda b, g, i: (b * ns + i, g)
    ocmp, sel = pl.pallas_call(
        _nsa_cmp_select_kernel,
        name="nsa_cmp_select",
        grid=(B, G, ns),
        in_specs=[pl.BlockSpec((TS, J * dh), sblk), pl.BlockSpec((None, None, nch, dh), bgi),
                  pl.BlockSpec((None, None, nch, dh), bgi), pl.BlockSpec((nsb, nch), lambda b, g, i: (0, 0))],
        out_specs=[pl.BlockSpec((TS, J * dh), sblk), pl.BlockSpec((None, None, nsb, TS), lambda b, g, i: (b, g, 0, i))],
        out_shape=[jax.ShapeDtypeStruct((T, H * dh), BF16), jax.ShapeDtypeStruct((B, G, nsb, S), BF16)],
        compiler_params=_cparams("parallel", "parallel", "parallel"),
    )(roped, kc, vc, jnp.asarray(ovl.T))
    TQ = NSA_TQ
    nq = S // TQ
    qblk = lambda b, g, i: (b * nq + i, g)

    gates_g = gates[:, :3 * H].reshape(B, S, G, J * 3).transpose(0, 2, 1, 3)
    return pl.pallas_call(
        _nsa_main_kernel,
        name="nsa_main",
        grid=(B, G, nq),
        in_specs=[pl.BlockSpec((TQ, J * dh), qblk)] + [pl.BlockSpec((None, None, dh, S), bgi),
                                                       pl.BlockSpec((None, None, S, 2 * dh), bgi)] * 2 + [
            pl.BlockSpec((None, None, nsb, TQ), lambda b, g, i: (b, g, 0, i)),
            pl.BlockSpec((TQ, J * dh), qblk), pl.BlockSpec((None, None, TQ, J * 3), lambda b, g, i: (b, g, i, 0))],
        out_specs=pl.BlockSpec((TQ, J * dh), qblk),
        out_shape=jax.ShapeDtypeStruct((T, H * dh), BF16),
        compiler_params=_cparams("parallel", "parallel", "arbitrary"),
    )(roped, ksT, vs, kwT, vw, sel, ocmp, gates_g)


def _dil_kernel(q_ref, kp_ref, kc_ref, vp_ref, vc_ref, o_ref, lse_ref):
    NQ = q_ref.shape[0]
    prev_from = jnp.where(pl.program_id(2) == 0, NQ, 0)
    qi = lax.broadcasted_iota(I32, (NQ, 1), 0)
    kj = lax.broadcasted_iota(I32, (1, 2 * NQ), 1)
    dist = NQ + qi - kj
    bias = jnp.where((dist >= 0) & (dist <= NQ) & (kj >= prev_from), 0.0, NEG_INF)
    lane = lax.broadcasted_iota(I32, (NQ, 128), 1)
    first = lane < HEAD_DIM
    lse_all = jnp.zeros((NQ, 128), F32)
    zero = jnp.zeros((), q_ref.dtype)
    for hp in range(DIL_HEADS // 2):
        sl = slice(hp * 128, (hp + 1) * 128)
        q2 = q_ref[:, sl]
        kb = jnp.concatenate([kp_ref[:, sl], kc_ref[:, sl]], axis=0)
        vb = jnp.concatenate([vp_ref[:, sl], vc_ref[:, sl]], axis=0)
        outs = []
        for sub in range(2):
            keep = first if sub == 0 else jnp.logical_not(first)
            s = _nt(jnp.where(keep, q2, zero), kb) * HEAD_DIM ** -0.5 + bias
            m = jnp.max(s, axis=-1, keepdims=True)
            e = jnp.exp(s - m)
            den = jnp.sum(e, axis=-1, keepdims=True)
            outs.append(_mm((e / den).astype(BF16), vb))
            lse_all = jnp.where(lane == 2 * hp + sub, m + jnp.log(den), lse_all)
        o_ref[:, sl] = jnp.where(first, outs[0], outs[1]).astype(o_ref.dtype)
    lse_ref[...] = lse_all


def _proj_res_kernel(*refs, rope, dils):
    n_out = len(dils)
    x_ref, w_ref = refs[:2]
    rest = refs[2:]
    if rope:
        cos_ref, sin_ref = rest[:2]
        rest = rest[2:]
    out_refs, xb_ref, scr_ref = rest[:n_out], rest[n_out], rest[n_out + 1]
    tm, tn = x_ref.shape[0], w_ref.shape[1]

    @pl.when(pl.program_id(1) == 0)
    def _():
        xb_ref[...] = x_ref[...].astype(BF16)

    acc = _mm(xb_ref[...], w_ref[...])
    if rope:
        lane = lax.broadcasted_iota(I32, acc.shape, 1)
        lo = (lane % HEAD_DIM) < ROPE_HALF
        rot = jnp.where(lo, pltpu.roll(acc, tn - ROPE_HALF, 1), pltpu.roll(acc, ROPE_HALF, 1))
        rep = lambda t: jnp.concatenate([t] * (tn // 128), axis=1)
        acc = acc * rep(cos_ref[...]) + rot * rep(sin_ref[...])
    if any(d > 1 for d in dils):
        for c in range(tn // 128):
            scr_ref[c] = acc[:, c * 128:(c + 1) * 128]
    for o_ref, dil in zip(out_refs, dils):
        if dil == 1:
            o_ref[...] = acc.astype(o_ref.dtype)
            continue
        for r in range(dil):
            for c in range(tn // 128):
                o_ref[:, r * tn + c * 128:r * tn + (c + 1) * 128] = (
                    scr_ref[c, pl.ds(r, tm // dil, stride=dil), :].astype(o_ref.dtype))


def _proj_res(x, w, dils, *, seq, rope, tm, tn):
    T, K = x.shape
    N = w.shape[1]
    nparts = N // tn
    in_specs = [pl.BlockSpec((tm, K), lambda i, j: (i, 0)), pl.BlockSpec((K, tn), lambda i, j: (0, j))]
    args = [x, w]
    if rope:
        nseq = seq // tm
        in_specs += [pl.BlockSpec((tm, 128), lambda i, j: (i % nseq, 0))] * 2
        args += list(_rope_tables(jnp.arange(seq), 128))
    return pl.pallas_call(
        functools.partial(_proj_res_kernel, rope=rope, dils=dils),
        name="proj_residue_rope" if rope else "proj_residue",
        grid=(T // tm, nparts),
        in_specs=in_specs,
        out_specs=[pl.BlockSpec((tm // d, d * tn), lambda i, j: (i, j)) for d in dils],
        out_shape=[jax.ShapeDtypeStruct((T // d, nparts * d * tn), BF16) for d in dils],
        scratch_shapes=[pltpu.VMEM((tm, K), BF16), pltpu.VMEM((tn // 128, tm, 128), F32)],
        compiler_params=_cparams("parallel", "arbitrary"),
    )(*args)


def _dil_group(qk, v, B, S, dil):
    NQ = DIL_STEPS
    U = S // dil
    nb = U // NQ
    W = D_MODEL
    qk_view = qk.reshape(B, U, 2 * dil * W)
    v_view = v.reshape(B, U, dil * W)
    cur = lambda part, np_: (lambda b, r, n: (b, n, r * np_ + part))
    prev = lambda part, np_: (lambda b, r, n: (b, jnp.maximum(n - 1, 0), r * np_ + part))
    blk = lambda f: pl.BlockSpec((None, NQ, W), f)
    o, lse = pl.pallas_call(
        _dil_kernel,
        name=f"dilated_attn_{dil}",
        grid=(B, dil, nb),
        in_specs=[blk(cur(0, 2)), blk(prev(1, 2)), blk(cur(1, 2)), blk(prev(0, 1)), blk(cur(0, 1))],
        out_specs=[pl.BlockSpec((None, NQ, W), lambda b, r, n: (b, n, r)),
                   pl.BlockSpec((None, NQ, 128), lambda b, r, n: (b, n, r))],
        out_shape=[jax.ShapeDtypeStruct((B, U, dil * W), BF16), jax.ShapeDtypeStruct((B, U, dil * 128), F32)],
        compiler_params=_cparams("parallel", "parallel", "arbitrary"),
    )(qk_view, qk_view, qk_view, v_view, v_view)
    return o.reshape(B * U, dil * W), lse.reshape(B * U, dil * 128)


def _dil_outproj_ln_kernel(o0_ref, o1_ref, o2_ref, l0_ref, l1_ref, l2_ref, hexp_ref, w_ref, x_ref, g_ref, b_ref, out_ref,
                           oscr_ref, lscr_ref, *, dils):
    tm, W = x_ref.shape

    def natural(o_ref, l_ref, dil):
        if dil == 1:
            return o_ref[...].astype(F32), l_ref[...]
        n = tm // dil
        for r in range(dil):
            lscr_ref[0, pl.ds(r, n, stride=dil), :] = l_ref[:, r * 128:(r + 1) * 128]
            for c in range(W // 128):
                oscr_ref[c, pl.ds(r, n, stride=dil), :] = o_ref[:, r * W + c * 128:r * W + (c + 1) * 128].astype(F32)
        return jnp.concatenate([oscr_ref[c] for c in range(W // 128)], axis=1), lscr_ref[0]

    os_, ls = zip(*[natural(o, l, d) for o, l, d in zip((o0_ref, o1_ref, o2_ref), (l0_ref, l1_ref, l2_ref), dils)])
    m = jnp.maximum(jnp.maximum(ls[0], ls[1]), ls[2])
    es = [jnp.exp(l - m) for l in ls]
    tot = es[0] + es[1] + es[2]
    y = jnp.zeros((tm, W), F32)
    for e, o in zip(es, os_):
        y = y + _mm01(e / tot, hexp_ref[...]) * o
    z = ALPHA * x_ref[...] + _mm(y.astype(BF16), w_ref[...])
    out_ref[...] = _ln_rows(z, g_ref[...], b_ref[...])


def _dilated_layer(x, B, S, w_in, w_out, g, b, *, tm=256):
    T, D = x.shape
    dils = tuple(d for _, d in DIL_GROUPS)
    wb = w_in.astype(BF16)
    vals = _proj_res(x, wb[:, 2 * len(dils) * D:], dils, seq=S, rope=False, tm=1024, tn=D)
    outs = []
    for gi, dil in enumerate(dils):
        qk, = _proj_res(x, wb[:, 2 * gi * D:(2 * gi + 2) * D], (dil,), seq=S, rope=True, tm=512, tn=2 * D)
        outs.append(_dil_group(qk, vals[gi], B, S, dil))
    hexp = np.zeros((128, D), np.float32)
    for h in range(DIL_HEADS):
        hexp[h, h * HEAD_DIM:(h + 1) * HEAD_DIM] = 1.0
    row = lambda i: (i, 0)
    fixed = lambda i: (0, 0)
    return pl.pallas_call(
        functools.partial(_dil_outproj_ln_kernel, dils=dils),
        name="dilated_outproj_ln",
        grid=(T // tm,),
        in_specs=[pl.BlockSpec((tm // d, d * D), row) for d in dils] + [pl.BlockSpec((tm // d, d * 128), row) for d in dils]
        + [pl.BlockSpec((128, D), fixed), pl.BlockSpec((D, D), fixed), pl.BlockSpec((tm, D), row),
           pl.BlockSpec((1, D), fixed), pl.BlockSpec((1, D), fixed)],
        out_specs=pl.BlockSpec((tm, D), row),
        out_shape=jax.ShapeDtypeStruct((T, D), F32),
        scratch_shapes=[pltpu.VMEM((D // 128, tm, 128), F32), pltpu.VMEM((1, tm, 128), F32)],
        compiler_params=_cparams("parallel"),
    )(outs[0][0], outs[1][0], outs[2][0], outs[0][1], outs[1][1], outs[2][1], jnp.asarray(hexp),
      w_out.astype(BF16), x, g.reshape(1, D), b.reshape(1, D))


def _pool_ln_kernel(x_ref, halo_ref, w_ref, scale_ref, g_ref, b_ref, o_ref, ext_ref):
    TS = x_ref.shape[0]
    s = pl.program_id(1)
    x = x_ref[...]
    ext_ref[0:POOL_HALO, :] = jnp.where(s == 0, 0.0, halo_ref[...])
    ext_ref[POOL_HALO:POOL_HALO + TS, :] = x
    cnt = (s * TS + lax.broadcasted_iota(I32, (TS, 1), 0) + 1).astype(F32)
    ys = []
    for gi, w in enumerate(POOL_WINDOWS):
        sl = slice(gi * POOL_GROUP, (gi + 1) * POOL_GROUP)
        xg = x[:, sl]
        tot = xg
        for j in range(1, w):
            tot = tot + ext_ref[POOL_HALO - j:POOL_HALO - j + TS, sl]
        mean = tot / jnp.minimum(cnt, float(w))
        ys.append(_mm((mean - xg).astype(BF16), w_ref[gi]))
    y = jnp.concatenate(ys, axis=1) * scale_ref[...]
    o_ref[...] = _ln_rows(ALPHA * x + y, g_ref[...], b_ref[...])


def _pool_layer(x, B, S, w_grp, scale, g, b, *, ts=512):
    T, D = x.shape
    ns = S // ts
    hb = ts // POOL_HALO
    fixed = lambda bb, s: (0, 0)
    return pl.pallas_call(
        _pool_ln_kernel,
        name="pool_ln",
        grid=(B, ns),
        in_specs=[pl.BlockSpec((ts, D), lambda bb, s: (bb * ns + s, 0)),
                  pl.BlockSpec((POOL_HALO, D), lambda bb, s: (jnp.maximum((bb * ns + s) * hb - 1, 0), 0)),
                  pl.BlockSpec((len(POOL_WINDOWS), POOL_GROUP, POOL_GROUP), lambda bb, s: (0, 0, 0)),
                  pl.BlockSpec((1, D), fixed), pl.BlockSpec((1, D), fixed), pl.BlockSpec((1, D), fixed)],
        out_specs=pl.BlockSpec((ts, D), lambda bb, s: (bb * ns + s, 0)),
        out_shape=jax.ShapeDtypeStruct((T, D), F32),
        scratch_shapes=[pltpu.VMEM((ts + POOL_HALO, D), F32)],
        compiler_params=_cparams("parallel", "arbitrary"),
    )(x, x, w_grp.astype(BF16), scale.reshape(1, D), g.reshape(1, D), b.reshape(1, D))


U32 = jnp.uint32
HI16 = 0xFFFF0000
ROW_WORDS = D_MODEL // 2


def _pack_halves(v):
    h = v.shape[1] // 2
    bits = lambda a: lax.bitcast_convert_type(a.astype(BF16).astype(F32), U32)
    return (bits(v[:, :h]) >> 16) | (bits(v[:, h:]) & U32(HI16))


def _unpack_halves(w):
    lo = lax.bitcast_convert_type(w << 16, F32)
    hi = lax.bitcast_convert_type(w & U32(HI16), F32)
    return jnp.concatenate([lo, hi], axis=1)


def _router_kernel(x_ref, wT_ref, bias_ref, triu_ref, eidx_ref, gw_ref, rank_ref, cnt_ref, xp_ref, carry_ref):
    E = N_EXPERTS
    per = E // N_EXPERT_GROUPS
    TM = x_ref.shape[0]

    @pl.when(pl.program_id(0) == 0)
    def _():
        carry_ref[...] = jnp.zeros_like(carry_ref)

    x = x_ref[...]
    xp_ref[...] = _pack_halves(x)

    scores = _sigmoid(_nt(wT_ref[...], x, precision=HIGHEST))
    biased = scores + bias_ref[...]
    eio = lax.broadcasted_iota(I32, (E, TM), 0)

    gio = lax.broadcasted_iota(I32, (per, TM), 0)
    gscore = []
    for gidx in range(N_EXPERT_GROUPS):
        slab = biased[gidx * per:(gidx + 1) * per, :]
        m1 = jnp.max(slab, axis=0, keepdims=True)
        i1 = jnp.min(jnp.where(slab == m1, gio, per), axis=0, keepdims=True)
        m2 = jnp.max(jnp.where(gio == i1, BELOW_NEG_INF, slab), axis=0, keepdims=True)
        gscore.append(m1 + m2)
    slabs = []
    for gidx in range(N_EXPERT_GROUPS):
        beat = jnp.zeros((1, TM), F32)
        for o in range(N_EXPERT_GROUPS):
            if o == gidx:
                continue
            wins = (gscore[o] >= gscore[gidx]) if o < gidx else (gscore[o] > gscore[gidx])
            beat = beat + jnp.where(wins, 1.0, 0.0)
        keep = beat < float(TOPK_GROUPS)
        slabs.append(jnp.where(keep, biased[gidx * per:(gidx + 1) * per, :], NEG_INF))
    cur = jnp.concatenate(slabs, axis=0)

    picked = jnp.zeros((E, TM), F32)
    idxs, vals = [], []
    for _ in range(TOP_K):
        m = jnp.max(cur, axis=0, keepdims=True)
        idx = jnp.min(jnp.where(cur == m, eio, E), axis=0, keepdims=True)
        hit = eio == idx
        picked = jnp.where(hit, 1.0, picked)
        cur = jnp.where(hit, BELOW_NEG_INF, cur)
        idxs.append(idx)
        vals.append(jnp.sum(jnp.where(hit, scores, 0.0), axis=0, keepdims=True))
    total = vals[0]
    for v in vals[1:]:
        total = total + v

    pos = _mm(picked.astype(BF16), triu_ref[...]) + carry_ref[...]
    for k in range(TOP_K):
        eidx_ref[k:k + 1, :] = idxs[k]
        gw_ref[k:k + 1, :] = vals[k] / total * ROUTED_SCALE
        rank_ref[k:k + 1, :] = jnp.sum(jnp.where(eio == idxs[k], pos, 0.0), axis=0, keepdims=True).astype(I32)
    carry_ref[...] = carry_ref[...] + jnp.sum(picked, axis=1, keepdims=True)
    cnt_ref[...] = carry_ref[...].astype(I32)


def _router(x, router_w, router_bias, *, tm=512):
    T, D = x.shape
    E = N_EXPERTS
    triu = jnp.triu(jnp.ones((tm, tm), F32), k=1).astype(BF16)
    col = lambda i: (0, i)
    fixed = lambda i: (0, 0)
    return pl.pallas_call(
        _router_kernel,
        name="moe_router",
        grid=(T // tm,),
        in_specs=[pl.BlockSpec((tm, D), lambda i: (i, 0)), pl.BlockSpec((E, D), fixed),
                  pl.BlockSpec((E, 1), fixed), pl.BlockSpec((tm, tm), fixed)],
        out_specs=[pl.BlockSpec((TOP_K, tm), col), pl.BlockSpec((TOP_K, tm), col), pl.BlockSpec((TOP_K, tm), col),
                   pl.BlockSpec((E, 1), fixed), pl.BlockSpec((tm, ROW_WORDS), lambda i: (i, 0))],
        out_shape=[jax.ShapeDtypeStruct((TOP_K, T), I32), jax.ShapeDtypeStruct((TOP_K, T), F32),
                   jax.ShapeDtypeStruct((TOP_K, T), I32), jax.ShapeDtypeStruct((E, 1), I32),
                   jax.ShapeDtypeStruct((T, ROW_WORDS), U32)],
        scratch_shapes=[pltpu.VMEM((E, 1), F32)],
        compiler_params=_cparams("arbitrary"),
    )(x, router_w.T.astype(F32), router_bias.reshape(E, 1).astype(F32), triu)


def _tile_indices(dest_hbm, idx_ref, sem_idx, n_idx):
    i = pl.program_id(0)

    def idx_copy(step):
        slot = step % 2
        return pltpu.make_async_copy(dest_hbm.at[step], idx_ref.at[pl.ds(slot * n_idx, n_idx)], sem_idx.at[slot])

    @pl.when(i == 0)
    def _():
        idx_copy(0).start()

    idx_copy(i).wait()

    @pl.when(i + 1 < pl.num_programs(0))
    def _():
        idx_copy(i + 1).start()

    return (i % 2) * n_idx


def _dispatch_kernel(cnt_ref, pstart_ref, dest_hbm, x_ref, rows_ref, idx_ref, zero_ref, sem_idx, sem_rows, sem_zero):
    TD = x_ref.shape[0]

    @pl.when(pl.program_id(0) == 0)
    def _():
        zero_ref[...] = jnp.zeros_like(zero_ref)
        pieces = [1 << b for b in reversed(range(EXP_BLK.bit_length() - 1))]

        def per_expert(e, c):
            n = cnt_ref[e]
            pad = (n + EXP_BLK - 1) // EXP_BLK * EXP_BLK - n
            first = pstart_ref[e] + n

            def piece(size):
                at = first + (pad & ~(2 * size - 1))
                return pltpu.make_async_copy(zero_ref.at[pl.ds(0, size)], rows_ref.at[pl.ds(at, size), 0], sem_zero)

            for size in pieces:
                pl.when((pad & size) != 0)(lambda size=size: piece(size).start())
            for size in pieces:
                pl.when((pad & size) != 0)(lambda size=size: piece(size).wait())
            return c

        lax.fori_loop(0, N_EXPERTS, per_expert, 0)

    base = _tile_indices(dest_hbm, idx_ref, sem_idx, TOP_K * TD)
    scatter = lambda t, k: pltpu.make_async_copy(x_ref.at[pl.ds(t, 1)], rows_ref.at[idx_ref[base + k * TD + t]], sem_rows)

    def issue(t, c):
        for k in range(TOP_K):
            scatter(t, k).start(priority=k % 2)
        return c

    def drain(t, c):
        for k in range(TOP_K):
            scatter(t, k).wait()
        return c

    lax.fori_loop(0, TD, issue, 0, unroll=8)
    lax.fori_loop(0, TD, drain, 0, unroll=8)


def _dispatch(xp, dest_tiles, counts, pad_start, n_rows, *, td):
    T = xp.shape[0]
    return pl.pallas_call(
        _dispatch_kernel,
        name="moe_dispatch",
        grid_spec=pltpu.PrefetchScalarGridSpec(
            num_scalar_prefetch=2,
            grid=(T // td,),
            in_specs=[pl.BlockSpec(memory_space=pl.ANY), pl.BlockSpec((td, ROW_WORDS), lambda i, c, p: (i, 0))],
            out_specs=pl.BlockSpec(memory_space=pl.ANY),
            scratch_shapes=[pltpu.SMEM((2 * TOP_K * td,), I32), pltpu.VMEM((EXP_BLK // 2, ROW_WORDS), U32),
                            pltpu.SemaphoreType.DMA((2,)), pltpu.SemaphoreType.DMA(()), pltpu.SemaphoreType.DMA(())],
        ),
        out_shape=jax.ShapeDtypeStruct((n_rows, 1, ROW_WORDS), U32),
        compiler_params=_cparams("arbitrary"),
    )(counts, pad_start, dest_tiles, xp)


def _expert_kernel(be_ref, nu_ref, rows_hbm, wg_ref, wu_ref, wd_ref, out_hbm, xbuf, obuf, wgu_bf, wd_bf, sem_in, sem_out):
    i = pl.program_id(0)
    nu = nu_ref[0]

    @pl.when((i < nu) & ((i == 0) | (be_ref[i] != be_ref[jnp.maximum(i - 1, 0)])))
    def _():
        wgu_bf[:, :EXPERT_FF] = wg_ref[0].astype(BF16)
        wgu_bf[:, EXPERT_FF:] = wu_ref[0].astype(BF16)
        wd_bf[...] = wd_ref[0].astype(BF16)

    blk = lambda step: pl.ds(pl.multiple_of(step * EXP_BLK, EXP_BLK), EXP_BLK)
    in_copy = lambda step: pltpu.make_async_copy(rows_hbm.at[blk(step), 0], xbuf.at[step % 2], sem_in.at[step % 2])
    out_copy = lambda step: pltpu.make_async_copy(obuf.at[step % 2], out_hbm.at[blk(step), 0], sem_out.at[step % 2])

    @pl.when(i == 0)
    def _():
        in_copy(0).start()

    @pl.when(i < nu)
    def _():
        in_copy(i).wait()

        @pl.when(i + 1 < nu)
        def _():
            in_copy(i + 1).start()

        @pl.when(i >= 2)
        def _():
            out_copy(i - 2).wait()

        slot = i % 2
        gu = _mm(_unpack_halves(xbuf[slot]).astype(BF16), wgu_bf[...])
        h = _silu(gu[:, :EXPERT_FF]) * gu[:, EXPERT_FF:]
        obuf[slot] = _pack_halves(_mm(h.astype(BF16), wd_bf[...]))
        out_copy(i).start()

    @pl.when(i == pl.num_programs(0) - 1)
    def _():
        @pl.when(nu >= 2)
        def _():
            out_copy(nu - 2).wait()

        out_copy(nu - 1).wait()


def _experts(rows, blk_expert, n_used, w_gate, w_up, w_down, layer):
    R = rows.shape[0]
    D = w_down.shape[-1]
    n_blk = R // EXP_BLK
    expert = lambda i, be, nu: (layer, be[jnp.minimum(i, nu[0] - 1)], 0, 0)
    return pl.pallas_call(
        _expert_kernel,
        name="moe_experts",
        grid_spec=pltpu.PrefetchScalarGridSpec(
            num_scalar_prefetch=2,
            grid=(n_blk,),
            in_specs=[pl.BlockSpec(memory_space=pl.ANY), pl.BlockSpec((None, 1, D, EXPERT_FF), expert),
                      pl.BlockSpec((None, 1, D, EXPERT_FF), expert), pl.BlockSpec((None, 1, EXPERT_FF, D), expert)],
            out_specs=pl.BlockSpec(memory_space=pl.ANY),
            scratch_shapes=[pltpu.VMEM((2, EXP_BLK, ROW_WORDS), U32), pltpu.VMEM((2, EXP_BLK, ROW_WORDS), U32),
                            pltpu.VMEM((D, 2 * EXPERT_FF), BF16), pltpu.VMEM((EXPERT_FF, D), BF16),
                            pltpu.SemaphoreType.DMA((2,)), pltpu.SemaphoreType.DMA((2,))],
        ),
        out_shape=jax.ShapeDtypeStruct((R, 1, ROW_WORDS), U32),
        compiler_params=_cparams("arbitrary"),
    )(blk_expert, n_used, rows, w_gate, w_up, w_down)


def _combine_kernel(dest_hbm, gw_ref, x_ref, rows_ref, wsgu_ref, wsd_ref, g_ref, b_ref, p_ref, wp_ref, wpg_ref,
                    o_ref, idx_ref, buf_ref, sem_idx, sem_rows):
    TM = x_ref.shape[0]
    n_idx = TOP_K * TM
    i = pl.program_id(0)
    n = pl.num_programs(0)

    def idx_copy(step):
        slot = step % 3
        return pltpu.make_async_copy(dest_hbm.at[step], idx_ref.at[pl.ds(slot * n_idx, n_idx)], sem_idx.at[slot])

    def row_loop(step, slot, start):
        base = (step % 3) * n_idx

        def body(t, c):
            for k in range(TOP_K):
                cp = pltpu.make_async_copy(rows_ref.at[idx_ref[base + k * TM + t]], buf_ref.at[slot, k, pl.ds(t, 1)],
                                           sem_rows.at[slot])
                if start:
                    cp.start(priority=k % 2)
                else:
                    cp.wait()
            return c

        lax.fori_loop(0, TM, body, 0, unroll=8)

    def for_parity(step, fn):
        for slot in range(2):
            pl.when(step % 2 == slot)(functools.partial(fn, slot))

    @pl.when(i == 0)
    def _():
        idx_copy(0).start()
        idx_copy(0).wait()
        row_loop(0, 0, True)

        @pl.when(n > 1)
        def _():
            idx_copy(1).start()

    @pl.when(i + 1 < n)
    def _():
        idx_copy(i + 1).wait()

        @pl.when(i + 2 < n)
        def _():
            idx_copy(i + 2).start()

        for_parity(i + 1, lambda slot: row_loop(i + 1, slot, True))

    x = x_ref[...]
    gu = _mm(x.astype(BF16), wsgu_ref[...])
    ff = gu.shape[1] // 2
    f = _mm((_silu(gu[:, :ff]) * gu[:, ff:]).astype(BF16), wsd_ref[...])
    for_parity(i, lambda slot: row_loop(i, slot, False))
    gw = gw_ref[...]
    cur = i % 2
    for k in range(TOP_K):
        f = f + gw[:, k:k + 1] * _unpack_halves(buf_ref[cur, k])
    x2 = _ln_rows(ALPHA * x + f, g_ref[...], b_ref[...])
    gate = _sigmoid(_mm(x2.astype(BF16), wpg_ref[...]))
    o_ref[...] = x2 + gate * _mm(p_ref[...].astype(BF16), wp_ref[...])


def _combine(dest_tiles, gw, x, rows_out, wsgu, wsd, g, b, p, wp, wpg, *, tm):
    T, D = x.shape
    row = lambda i: (i, 0)
    fixed = lambda i: (0, 0)
    full = lambda a: pl.BlockSpec(a.shape, fixed)
    return pl.pallas_call(
        _combine_kernel,
        name="moe_combine",
        grid=(T // tm,),
        in_specs=[pl.BlockSpec(memory_space=pl.ANY), pl.BlockSpec((tm, TOP_K), row), pl.BlockSpec((tm, D), row),
                  pl.BlockSpec(memory_space=pl.ANY), full(wsgu), full(wsd), pl.BlockSpec((1, D), fixed),
                  pl.BlockSpec((1, D), fixed), pl.BlockSpec((tm, PLE_DIM), row), full(wp), full(wpg)],
        out_specs=pl.BlockSpec((tm, D), row),
        out_shape=jax.ShapeDtypeStruct((T, D), F32),
        scratch_shapes=[pltpu.SMEM((3 * TOP_K * tm,), I32), pltpu.VMEM((2, TOP_K, tm, ROW_WORDS), U32),
                        pltpu.SemaphoreType.DMA((3,)), pltpu.SemaphoreType.DMA((2,))],
        compiler_params=_cparams("arbitrary"),
    )(dest_tiles, gw, x, rows_out, wsgu, wsd, g.reshape(1, D), b.reshape(1, D), p, wp, wpg)


MOE_TILE = 256


def _moe_ple_layer(x, p, layer, router_w, router_bias, w_gate, w_up, w_down, ws_gate, ws_up, ws_down, g, b, ple_w, ple_gate_w):
    T, D = x.shape
    eidx, gw, rank, counts, xp = _router(x, router_w, router_bias)
    counts = counts.reshape(N_EXPERTS)
    padded = (counts + EXP_BLK - 1) // EXP_BLK * EXP_BLK
    pad_end = jnp.cumsum(padded)
    pad_start = pad_end - padded
    n_blk = T * TOP_K // EXP_BLK + N_EXPERTS
    e_iota = jnp.arange(N_EXPERTS, dtype=I32)
    dest = rank + jnp.sum(jnp.where(eidx[..., None] == e_iota, pad_start, 0), axis=-1)
    tm = MOE_TILE
    dest_tiles = dest.reshape(TOP_K, T // tm, tm).transpose(1, 0, 2).reshape(T // tm, TOP_K * tm)
    blk_first = jnp.arange(n_blk, dtype=I32)[:, None] * EXP_BLK
    blk_expert = jnp.minimum(jnp.sum((pad_end[None, :] <= blk_first).astype(I32), axis=-1), N_EXPERTS - 1)
    n_used = (pad_end[-1:] // EXP_BLK).astype(I32)

    rows = _dispatch(xp, dest_tiles, counts.astype(I32), pad_start.astype(I32), n_blk * EXP_BLK, td=tm)
    rows_out = _experts(rows, blk_expert, n_used, w_gate, w_up, w_down, layer)
    wsgu = jnp.concatenate([ws_gate, ws_up], axis=-1).astype(BF16)
    return _combine(dest_tiles, gw.T, x, rows_out, wsgu, ws_down.astype(BF16), g, b, p,
                    ple_w.astype(BF16), ple_gate_w.astype(BF16), tm=tm)


def kernel(x, p, ln_g, ln_b, mlstm_w_in, mlstm_conv, mlstm_ig_bias, mlstm_fg_bias, mlstm_norm_g, mlstm_w_out, nsa_w_in, nsa_cmp_pos_k, nsa_cmp_pos_v, nsa_cmp_wk, nsa_cmp_wv, nsa_w_out, dil_w_in, dil_w_out, pool_w, pool_scale, router_w, router_bias, exp_w_gate, exp_w_up, exp_w_down, sh_w_gate, sh_w_up, sh_w_down, ple_w, ple_gate_w):
    B, S, D = x.shape
    T = B * S
    xf = x.reshape(T, D)
    pf = p.reshape(DEPTH, T, PLE_DIM)
    for i in range(DEPTH):
        kind, j = i % 4, i // 4
        g1, b1 = ln_g[i, 0], ln_b[i, 0]
        if kind == 0:
            y = _mlstm_mixer(xf, B, S, mlstm_w_in[j], mlstm_conv[j], mlstm_ig_bias[j], mlstm_fg_bias[j], mlstm_norm_g[j])
            xf = _outproj_ln(y, mlstm_w_out[j].astype(BF16), xf, g1, b1)
        elif kind == 1:
            y = _nsa_mixer(xf, B, S, nsa_w_in[j], nsa_cmp_pos_k[j], nsa_cmp_pos_v[j], nsa_cmp_wk[j], nsa_cmp_wv[j])
            xf = _outproj_ln(y, nsa_w_out[j].astype(BF16), xf, g1, b1)
        elif kind == 2:
            xf = _dilated_layer(xf, B, S, dil_w_in[j], dil_w_out[j], g1, b1)
        else:
            xf = _pool_layer(xf, B, S, pool_w[j], pool_scale[j], g1, b1)
        xf = _moe_ple_layer(xf, pf[i], i, router_w[i], router_bias[i], exp_w_gate, exp_w_up, exp_w_down,
                            sh_w_gate[i], sh_w_up[i], sh_w_down[i], ln_g[i, 1], ln_b[i, 1], ple_w[i], ple_gate_w[i])
    return xf.reshape(B, S, D)
```
